```python
import jax, jax.numpy as jnp
from jax import lax
import numpy as np

D_MODEL = 1024
BATCH = 8
SEQ = 8192
DEPTH = 1

CHUNK = 64
PLE_DIM = 256
HEAD_DIM = 64
SWA_HEADS = 8
SWA_KV_HEADS = 2
SWA_GROUP = SWA_HEADS // SWA_KV_HEADS
WINDOW = 128
SWA_BLOCK = WINDOW
FOX_HEADS = 8
FOX_BLOCK = 128
D_FF = 4 * D_MODEL
N_BRANCH = 2
RMS_EPS = 1e-6

SWA_Q = SWA_HEADS * HEAD_DIM
SWA_KV = SWA_KV_HEADS * HEAD_DIM
FOX_W = FOX_HEADS * HEAD_DIM
GATE_W = N_BRANCH * D_MODEL
SPLIT_POINTS = tuple(np.cumsum([SWA_Q, SWA_KV, SWA_KV, FOX_W, FOX_W, FOX_W, FOX_HEADS]).tolist())
D_IN = SPLIT_POINTS[-1] + GATE_W

kernel_name = "hybrid_swa_sink_fox_gated_block"


def alibi_slopes(n_heads):
    return jnp.asarray(np.array([2.0 ** (-8.0 * (h + 1) / n_heads) for h in range(n_heads)], dtype=np.float32))


def rms_norm(x, g):
    x32 = x.astype(jnp.float32)
    y = x32 * lax.rsqrt(jnp.mean(jnp.square(x32), axis=-1, keepdims=True) + RMS_EPS)
    return (y * g.astype(jnp.float32)).astype(x.dtype)


def sliding_window_attention(q, k, v, sinks):
    B, S = q.shape[0], q.shape[1]
    nb = S // SWA_BLOCK
    SB = SWA_BLOCK
    qb = q.reshape(B, nb, SB, SWA_KV_HEADS, SWA_GROUP, HEAD_DIM)
    kb = k.reshape(B, nb, SB, SWA_KV_HEADS, HEAD_DIM)
    vb = v.reshape(B, nb, SB, SWA_KV_HEADS, HEAD_DIM)

    def band(t):
        prev = jnp.pad(t[:, :-1], ((0, 0), (1, 0), (0, 0), (0, 0), (0, 0)))
        return jnp.concatenate([prev, t], axis=2)

    k_band, v_band = band(kb), band(vb)
    s = jnp.einsum('bnqkgd,bnskd->bnkgqs', qb, k_band).astype(jnp.float32) * (HEAD_DIM ** -0.5)

    qi = jnp.arange(SB)[:, None] + SB
    si = jnp.arange(2 * SB)[None, :]
    chunk_diff = qi // CHUNK - si // CHUNK
    band_ok = (chunk_diff >= 0) & (chunk_diff <= WINDOW // CHUNK)
    real_key = (jnp.arange(nb)[:, None, None] > 0) | (si >= SB)[None]
    mask = band_ok[None] & real_key

    slopes = alibi_slopes(SWA_HEADS).reshape(SWA_KV_HEADS, SWA_GROUP)
    alibi = -slopes[:, :, None, None] * jnp.abs(qi - si).astype(jnp.float32)
    s = jnp.where(mask[None, :, None, None], s + alibi[None, None], -jnp.inf)

    sink = jnp.broadcast_to(sinks.astype(jnp.float32).reshape(SWA_KV_HEADS, SWA_GROUP)[None, None, :, :, None, None],
                            s.shape[:-1] + (1,))
    probs = jax.nn.softmax(jnp.concatenate([s, sink], axis=-1), axis=-1)[..., :-1]
    out = jnp.einsum('bnkgqs,bnskd->bnqkgd', probs.astype(v.dtype), v_band)
    return out.reshape(B, S, SWA_Q)


def forgetting_attention(q, k, v, f_logit):
    B, S = q.shape[0], q.shape[1]
    nb = S // FOX_BLOCK
    log_f = jax.nn.log_sigmoid(f_logit.astype(jnp.float32))
    c = jnp.cumsum(log_f, axis=1)
    c_k = c.transpose(0, 2, 1)
    kh = k.transpose(0, 2, 1, 3)
    vh = v.transpose(0, 2, 1, 3)
    q_blocks = q.reshape(B, nb, FOX_BLOCK, FOX_HEADS, HEAD_DIM).transpose(1, 0, 3, 2, 4)
    c_blocks = c.reshape(B, nb, FOX_BLOCK, FOX_HEADS).transpose(1, 0, 3, 2)
    k_pos = jnp.arange(S)
    scale = HEAD_DIM ** -0.5

    def one_block(args):
        qb, cq, n = args
        s = jnp.einsum('bhqd,bhsd->bhqs', qb, kh).astype(jnp.float32) * scale
        s = s + cq[..., None] - c_k[:, :, None, :]
        q_pos = n * FOX_BLOCK + jnp.arange(FOX_BLOCK)
        s = jnp.where((k_pos[None, :] <= q_pos[:, None])[None, None], s, -jnp.inf)
        probs = jax.nn.softmax(s, axis=-1)
        return jnp.einsum('bhqs,bhsd->bhqd', probs.astype(vh.dtype), vh)

    out = lax.map(one_block, (q_blocks, c_blocks, jnp.arange(nb, dtype=jnp.int32)))
    return out.transpose(1, 0, 3, 2, 4).reshape(B, S, FOX_W)


def _fwd_setup_inputs(seed: int = 0) -> dict:
    key = jax.random.key(seed)
    ks = jax.random.split(key, 18)

    def dense(k, fan_in, fan_out):
        return jax.random.normal(k, (DEPTH, fan_in, fan_out), jnp.float32) * fan_in ** -0.5

    def gain(k, shape):
        return 1.0 + 0.02 * jax.random.normal(k, shape, jnp.float32)

    return {
        "x": jax.random.normal(ks[0], (BATCH, SEQ, D_MODEL), jnp.float32),
        "p": jax.random.normal(ks[1], (DEPTH, BATCH, SEQ, PLE_DIM), jnp.float32),
        "g_mix": gain(ks[2], (DEPTH, D_MODEL)),
        "w_in": dense(ks[3], D_MODEL, D_IN),
        "b_forget": 3.0 + 0.5 * jax.random.normal(ks[4], (DEPTH, FOX_HEADS), jnp.float32),
        "swa_sinks": 0.5 * jax.random.normal(ks[5], (DEPTH, SWA_HEADS), jnp.float32),
        "w_br_swa": dense(ks[6], SWA_Q, D_MODEL),
        "w_br_fox": dense(ks[7], FOX_W, D_MODEL),
        "w_mix_out": dense(ks[8], D_MODEL, D_MODEL),
        "g_mlp": gain(ks[9], (DEPTH, D_MODEL)),
        "w_ff1": dense(ks[10], D_MODEL, D_FF),
        "w_ff2": dense(ks[11], D_FF, D_MODEL),
        "g_ple": gain(ks[12], (DEPTH, D_MODEL)),
        "w_ple_gate": dense(ks[13], D_MODEL, D_MODEL),
        "w_ple_proj": dense(ks[14], PLE_DIM, D_MODEL),
        "g_final": gain(ks[15], (D_MODEL,)),
    }


def _fwd_reference(x, p, g_mix, w_in, b_forget, swa_sinks, w_br_swa, w_br_fox, w_mix_out,
              g_mlp, w_ff1, w_ff2, g_ple, w_ple_gate, w_ple_proj, g_final):
    B, S = x.shape[0], x.shape[1]
    h = x
    for i in range(DEPTH):
        u = rms_norm(h, g_mix[i])
        z = u @ w_in[i]
        q_a, k_a, v_a, q_b, k_b, v_b, f_b, gate_logits = jnp.split(z, SPLIT_POINTS, axis=-1)
        y_a = sliding_window_attention(
            q_a.reshape(B, S, SWA_HEADS, HEAD_DIM),
            k_a.reshape(B, S, SWA_KV_HEADS, HEAD_DIM),
            v_a.reshape(B, S, SWA_KV_HEADS, HEAD_DIM),
            swa_sinks[i]) @ w_br_swa[i]
        y_b = forgetting_attention(
            q_b.reshape(B, S, FOX_HEADS, HEAD_DIM),
            k_b.reshape(B, S, FOX_HEADS, HEAD_DIM),
            v_b.reshape(B, S, FOX_HEADS, HEAD_DIM),
            f_b + b_forget[i]) @ w_br_fox[i]
        gates = jax.nn.sigmoid(gate_logits).reshape(B, S, N_BRANCH, D_MODEL)
        mixed = gates[:, :, 0] * y_a + gates[:, :, 1] * y_b
        h = h + mixed @ w_mix_out[i]
        u = rms_norm(h, g_mlp[i])
        h = h + jnp.square(jax.nn.relu(u @ w_ff1[i])) @ w_ff2[i]
        ple_gate = jax.nn.sigmoid(rms_norm(h, g_ple[i]) @ w_ple_gate[i])
        h = h + ple_gate * (p[i] @ w_ple_proj[i])
    return rms_norm(h, g_final)


import jax as _jax
import jax.numpy as _jnp

TWIN_FORMAT = 'train_step'
FWD_PARAMS = ['x', 'p', 'g_mix', 'w_in', 'b_forget', 'swa_sinks', 'w_br_swa', 'w_br_fox', 'w_mix_out', 'g_mlp', 'w_ff1', 'w_ff2', 'g_ple', 'w_ple_gate', 'w_ple_proj', 'g_final']
TWIN_WEIGHTS = ['g_mix', 'w_in', 'b_forget', 'swa_sinks', 'w_br_swa', 'w_br_fox', 'w_mix_out', 'g_mlp', 'w_ff1', 'w_ff2', 'g_ple', 'w_ple_gate', 'w_ple_proj', 'g_final']
TWIN_DIFF_INPUT = 'x'
TWIN_INPUTS = ['x', 'p', 'g_mix', 'w_in', 'b_forget', 'swa_sinks', 'w_br_swa', 'w_br_fox', 'w_mix_out', 'g_mlp', 'w_ff1', 'w_ff2', 'g_ple', 'w_ple_gate', 'w_ple_proj', 'g_final', 'loss_target', 'm_g_mix', 'm_w_in', 'm_b_forget', 'm_swa_sinks', 'm_w_br_swa', 'm_w_br_fox', 'm_w_mix_out', 'm_g_mlp', 'm_w_ff1', 'm_w_ff2', 'm_g_ple', 'm_w_ple_gate', 'm_w_ple_proj', 'm_g_final', 'v_g_mix', 'v_w_in', 'v_b_forget', 'v_swa_sinks', 'v_w_br_swa', 'v_w_br_fox', 'v_w_mix_out', 'v_g_mlp', 'v_w_ff1', 'v_w_ff2', 'v_g_ple', 'v_w_ple_gate', 'v_w_ple_proj', 'v_g_final']
TWIN_OUTPUTS = ['loss', 'grad_x', 'grad_g_mix', 'grad_w_in', 'grad_b_forget', 'grad_swa_sinks', 'grad_w_br_swa', 'grad_w_br_fox', 'grad_w_mix_out', 'grad_g_mlp', 'grad_w_ff1', 'grad_w_ff2', 'grad_g_ple', 'grad_w_ple_gate', 'grad_w_ple_proj', 'grad_g_final', 'delta_g_mix', 'delta_w_in', 'delta_b_forget', 'delta_swa_sinks', 'delta_w_br_swa', 'delta_w_br_fox', 'delta_w_mix_out', 'delta_g_mlp', 'delta_w_ff1', 'delta_w_ff2', 'delta_g_ple', 'delta_w_ple_gate', 'delta_w_ple_proj', 'delta_g_final', 'new_m_g_mix', 'new_m_w_in', 'new_m_b_forget', 'new_m_swa_sinks', 'new_m_w_br_swa', 'new_m_w_br_fox', 'new_m_w_mix_out', 'new_m_g_mlp', 'new_m_w_ff1', 'new_m_w_ff2', 'new_m_g_ple', 'new_m_w_ple_gate', 'new_m_w_ple_proj', 'new_m_g_final', 'new_v_g_mix', 'new_v_w_in', 'new_v_b_forget', 'new_v_swa_sinks', 'new_v_w_br_swa', 'new_v_w_br_fox', 'new_v_w_mix_out', 'new_v_g_mlp', 'new_v_w_ff1', 'new_v_w_ff2', 'new_v_g_ple', 'new_v_w_ple_gate', 'new_v_w_ple_proj', 'new_v_g_final']
TWIN_LEAF_KINDS = {'loss': 'loss', 'grad_x': 'grad_x', 'grad_g_mix': 'grad_w', 'grad_w_in': 'grad_w', 'grad_b_forget': 'grad_w', 'grad_swa_sinks': 'grad_w', 'grad_w_br_swa': 'grad_w', 'grad_w_br_fox': 'grad_w', 'grad_w_mix_out': 'grad_w', 'grad_g_mlp': 'grad_w', 'grad_w_ff1': 'grad_w', 'grad_w_ff2': 'grad_w', 'grad_g_ple': 'grad_w', 'grad_w_ple_gate': 'grad_w', 'grad_w_ple_proj': 'grad_w', 'grad_g_final': 'grad_w', 'delta_g_mix': 'delta_w', 'delta_w_in': 'delta_w', 'delta_b_forget': 'delta_w', 'delta_swa_sinks': 'delta_w', 'delta_w_br_swa': 'delta_w', 'delta_w_br_fox': 'delta_w', 'delta_w_mix_out': 'delta_w', 'delta_g_mlp': 'delta_w', 'delta_w_ff1': 'delta_w', 'delta_w_ff2': 'delta_w', 'delta_g_ple': 'delta_w', 'delta_w_ple_gate': 'delta_w', 'delta_w_ple_proj': 'delta_w', 'delta_g_final': 'delta_w', 'new_m_g_mix': 'new_m', 'new_m_w_in': 'new_m', 'new_m_b_forget': 'new_m', 'new_m_swa_sinks': 'new_m', 'new_m_w_br_swa': 'new_m', 'new_m_w_br_fox': 'new_m', 'new_m_w_mix_out': 'new_m', 'new_m_g_mlp': 'new_m', 'new_m_w_ff1': 'new_m', 'new_m_w_ff2': 'new_m', 'new_m_g_ple': 'new_m', 'new_m_w_ple_gate': 'new_m', 'new_m_w_ple_proj': 'new_m', 'new_m_g_final': 'new_m', 'new_v_g_mix': 'new_v', 'new_v_w_in': 'new_v', 'new_v_b_forget': 'new_v', 'new_v_swa_sinks': 'new_v', 'new_v_w_br_swa': 'new_v', 'new_v_w_br_fox': 'new_v', 'new_v_w_mix_out': 'new_v', 'new_v_g_mlp': 'new_v', 'new_v_w_ff1': 'new_v', 'new_v_w_ff2': 'new_v', 'new_v_g_ple': 'new_v', 'new_v_w_ple_gate': 'new_v', 'new_v_w_ple_proj': 'new_v', 'new_v_g_final': 'new_v'}


def _forward(args):
    return _fwd_reference(*[args[k] for k in FWD_PARAMS])


def _output_shape():
    def fwd():
        inp = _fwd_setup_inputs(0)
        return _fwd_reference(*[inp[k] for k in FWD_PARAMS])
    out = _jax.eval_shape(fwd)
    return out.shape, out.dtype

N_MICROBATCH = 1
ADAM_LR = 0.001
ADAM_B1 = 0.9
ADAM_B2 = 0.999
ADAM_EPS = 1e-08
ADAM_WD = 0.01
ADAM_STEP = 10
PER_EXAMPLE_BATCH_AXIS = {'x': 0, 'p': 1, 'loss_target': 0}
SHARED_INPUTS = []
_WEIGHT_DTYPES = {'g_mix': _jnp.float32, 'w_in': _jnp.float32, 'b_forget': _jnp.float32, 'swa_sinks': _jnp.float32, 'w_br_swa': _jnp.float32, 'w_br_fox': _jnp.float32, 'w_mix_out': _jnp.float32, 'g_mlp': _jnp.float32, 'w_ff1': _jnp.float32, 'w_ff2': _jnp.float32, 'g_ple': _jnp.float32, 'w_ple_gate': _jnp.float32, 'w_ple_proj': _jnp.float32, 'g_final': _jnp.float32}
MOMENT_SCALE = {'g_mix': 9.030698e-02, 'w_in': 4.336974e-02, 'b_forget': 8.027354e-01, 'swa_sinks': 1.148945e-01, 'w_br_swa': 3.370953e-02, 'w_br_fox': 4.555161e-02, 'w_mix_out': 5.658632e-02, 'g_mlp': 2.176490e-01, 'w_ff1': 1.066572e-01, 'w_ff2': 2.151759e-01, 'g_ple': 3.412697e-02, 'w_ple_gate': 3.514602e-02, 'w_ple_proj': 8.106644e-02, 'g_final': 6.453457e+01}


def _to_microbatches(a, axis):
    t = _jnp.moveaxis(a, axis, 0)
    t = t.reshape((N_MICROBATCH, t.shape[0] // N_MICROBATCH) + t.shape[1:])
    return _jnp.moveaxis(t, 1, axis + 1)


def setup_inputs(seed: int = 0) -> dict:
    inp = _fwd_setup_inputs(seed)
    key = _jax.random.fold_in(_jax.random.key(seed), 7919)
    shape, _ = _output_shape()
    out = dict(inp)
    out["loss_target"] = _jax.random.normal(_jax.random.fold_in(key, 0), shape, _jnp.float32)
    for i, name in enumerate(TWIN_WEIGHTS):
        w = inp[name].astype(_jnp.float32)
        if MOMENT_SCALE is None:
            s = _jnp.sqrt(_jnp.mean(_jnp.square(w)) + 1e-30)
        else:
            s = MOMENT_SCALE[name]
        km, kv = _jax.random.split(_jax.random.fold_in(key, i + 1))
        out[name] = w
        out["m_" + name] = s * _jax.random.normal(km, w.shape, _jnp.float32)
        out["v_" + name] = (s * s) * _jax.random.uniform(kv, w.shape, _jnp.float32, 0.5, 1.5)
    if N_MICROBATCH > 1:
        for name, axis in PER_EXAMPLE_BATCH_AXIS.items():
            out[name] = _to_microbatches(out[name], axis)
    return {'x': out['x'], 'p': out['p'], 'g_mix': out['g_mix'], 'w_in': out['w_in'], 'b_forget': out['b_forget'], 'swa_sinks': out['swa_sinks'], 'w_br_swa': out['w_br_swa'], 'w_br_fox': out['w_br_fox'], 'w_mix_out': out['w_mix_out'], 'g_mlp': out['g_mlp'], 'w_ff1': out['w_ff1'], 'w_ff2': out['w_ff2'], 'g_ple': out['g_ple'], 'w_ple_gate': out['w_ple_gate'], 'w_ple_proj': out['w_ple_proj'], 'g_final': out['g_final'], 'loss_target': out['loss_target'], 'm_g_mix': out['m_g_mix'], 'm_w_in': out['m_w_in'], 'm_b_forget': out['m_b_forget'], 'm_swa_sinks': out['m_swa_sinks'], 'm_w_br_swa': out['m_w_br_swa'], 'm_w_br_fox': out['m_w_br_fox'], 'm_w_mix_out': out['m_w_mix_out'], 'm_g_mlp': out['m_g_mlp'], 'm_w_ff1': out['m_w_ff1'], 'm_w_ff2': out['m_w_ff2'], 'm_g_ple': out['m_g_ple'], 'm_w_ple_gate': out['m_w_ple_gate'], 'm_w_ple_proj': out['m_w_ple_proj'], 'm_g_final': out['m_g_final'], 'v_g_mix': out['v_g_mix'], 'v_w_in': out['v_w_in'], 'v_b_forget': out['v_b_forget'], 'v_swa_sinks': out['v_swa_sinks'], 'v_w_br_swa': out['v_w_br_swa'], 'v_w_br_fox': out['v_w_br_fox'], 'v_w_mix_out': out['v_w_mix_out'], 'v_g_mlp': out['v_g_mlp'], 'v_w_ff1': out['v_w_ff1'], 'v_w_ff2': out['v_w_ff2'], 'v_g_ple': out['v_g_ple'], 'v_w_ple_gate': out['v_w_ple_gate'], 'v_w_ple_proj': out['v_w_ple_proj'], 'v_g_final': out['v_g_final']}


def _loss(weights, diff, rest, loss_target):
    with _jax.named_scope("forward"):
        args = {**rest, TWIN_DIFF_INPUT: diff, **{k: w.astype(_WEIGHT_DTYPES[k]) for k, w in weights.items()}}
        y = _forward(args)
    with _jax.named_scope("loss_head"):
        err = _jnp.square(y.astype(_jnp.float32) - loss_target)
        return 0.5 * _jnp.sum(_jnp.mean(err, axis=-1)) if err.ndim else 0.5 * err


def _adamw(w, g, m, v):
    m = ADAM_B1 * m + (1.0 - ADAM_B1) * g
    v = ADAM_B2 * v + (1.0 - ADAM_B2) * _jnp.square(g)
    m_hat = m / (1.0 - ADAM_B1 ** ADAM_STEP)
    v_hat = v / (1.0 - ADAM_B2 ** ADAM_STEP)
    delta = -ADAM_LR * (m_hat / (_jnp.sqrt(v_hat) + ADAM_EPS) + ADAM_WD * w)
    return delta, m, v


def reference(x, p, g_mix, w_in, b_forget, swa_sinks, w_br_swa, w_br_fox, w_mix_out, g_mlp, w_ff1, w_ff2, g_ple, w_ple_gate, w_ple_proj, g_final, loss_target, m_g_mix, m_w_in, m_b_forget, m_swa_sinks, m_w_br_swa, m_w_br_fox, m_w_mix_out, m_g_mlp, m_w_ff1, m_w_ff2, m_g_ple, m_w_ple_gate, m_w_ple_proj, m_g_final, v_g_mix, v_w_in, v_b_forget, v_swa_sinks, v_w_br_swa, v_w_br_fox, v_w_mix_out, v_g_mlp, v_w_ff1, v_w_ff2, v_g_ple, v_w_ple_gate, v_w_ple_proj, v_g_final):
    given = dict(x=x, p=p, g_mix=g_mix, w_in=w_in, b_forget=b_forget, swa_sinks=swa_sinks, w_br_swa=w_br_swa, w_br_fox=w_br_fox, w_mix_out=w_mix_out, g_mlp=g_mlp, w_ff1=w_ff1, w_ff2=w_ff2, g_ple=g_ple, w_ple_gate=w_ple_gate, w_ple_proj=w_ple_proj, g_final=g_final, loss_target=loss_target, m_g_mix=m_g_mix, m_w_in=m_w_in, m_b_forget=m_b_forget, m_swa_sinks=m_swa_sinks, m_w_br_swa=m_w_br_swa, m_w_br_fox=m_w_br_fox, m_w_mix_out=m_w_mix_out, m_g_mlp=m_g_mlp, m_w_ff1=m_w_ff1, m_w_ff2=m_w_ff2, m_g_ple=m_g_ple, m_w_ple_gate=m_w_ple_gate, m_w_ple_proj=m_w_ple_proj, m_g_final=m_g_final, v_g_mix=v_g_mix, v_w_in=v_w_in, v_b_forget=v_b_forget, v_swa_sinks=v_swa_sinks, v_w_br_swa=v_w_br_swa, v_w_br_fox=v_w_br_fox, v_w_mix_out=v_w_mix_out, v_g_mlp=v_g_mlp, v_w_ff1=v_w_ff1, v_w_ff2=v_w_ff2, v_g_ple=v_g_ple, v_w_ple_gate=v_w_ple_gate, v_w_ple_proj=v_w_ple_proj, v_g_final=v_g_final)
    weights = {n: given[n] for n in TWIN_WEIGHTS}
    shared = {n: given[n] for n in SHARED_INPUTS}
    per_example = {n: given[n] for n in ['x', 'p']}
    grad_fn = _jax.value_and_grad(_loss, argnums=(0, 1))

    def one_microbatch(ex, loss_target):
        ex = dict(ex)
        diff = ex.pop(TWIN_DIFF_INPUT)
        return grad_fn(weights, diff, {**shared, **ex}, loss_target)

    if N_MICROBATCH == 1:
        loss, (grad_w, grad_x) = one_microbatch(per_example, given["loss_target"])
    else:
        def body(carry, xs):
            loss_sum, grad_sum = carry
            l_k, (gw_k, gx_k) = one_microbatch(xs[0], xs[1])
            with _jax.named_scope("update"):
                return (loss_sum + l_k, _jax.tree.map(_jnp.add, grad_sum, gw_k)), gx_k

        init = (_jnp.zeros((), _jnp.float32), _jax.tree.map(_jnp.zeros_like, weights))
        (loss, grad_w), grad_x = _jax.lax.scan(body, init, (per_example, given["loss_target"]))
    with _jax.named_scope("update"):
        delta_w, new_m, new_v = {}, {}, {}
        for n in TWIN_WEIGHTS:
            delta_w[n], new_m[n], new_v[n] = _adamw(weights[n], grad_w[n], given["m_" + n], given["v_" + n])
    return (loss, grad_x, *[grad_w[n] for n in TWIN_WEIGHTS], *[delta_w[n] for n in TWIN_WEIGHTS],
            *[new_m[n] for n in TWIN_WEIGHTS], *[new_v[n] for n in TWIN_WEIGHTS])
```

```python
import functools

import numpy as np
import jax
import jax.numpy as jnp
from jax import lax
from jax.experimental import pallas as pl
from jax.experimental.pallas import tpu as pltpu

F32 = jnp.float32
BF16 = jnp.bfloat16

D_MODEL = 1024
HEAD_DIM = 64
SWA_HEADS = 8
FOX_HEADS = 8
CHUNK_SHIFT = 6
SWA_BLOCK = 128
WINDOW_CHUNKS = 2
D_FF = 4096
PLE_DIM = 256
RMS_EPS = 1e-6
N_MAIN = 2304
N_FPAD = 128
N_GATE = 2048
N_ALL = N_MAIN + N_FPAD + N_GATE
D_IN = N_MAIN + FOX_HEADS + N_GATE
SCALE = HEAD_DIM ** -0.5
NEG = -1e30

ADAM_LR = 0.001
ADAM_B1 = 0.9
ADAM_B2 = 0.999
ADAM_EPS = 1e-08
ADAM_WD = 0.01
ADAM_STEP = 10

N_DEV = 8
LANES = 128
V7X_VMEM_BYTES = 64 * 1024 * 1024
VMEM_LIMIT = V7X_VMEM_BYTES * 3 // 4
MESH = pl.DeviceIdType.MESH
AXES = ("x", "y", "c")

_NT = (((1,), (1,)), ((), ()))
_TN = (((0,), (0,)), ((), ()))


def _params(n_grid):
    return pltpu.CompilerParams(dimension_semantics=("arbitrary",) * n_grid, vmem_limit_bytes=VMEM_LIMIT)


def _chunks(n, step):
    return [(s, min(step, n - s)) for s in range(0, n, step)]


def _sigmoid(x):
    return 1.0 / (1.0 + jnp.exp(-x))


def _dot(a, b):
    return jnp.dot(a, b, preferred_element_type=F32)


def _rms(h):
    return lax.rsqrt(jnp.mean(h * h, axis=-1, keepdims=True) + RMS_EPS)


def _rms_bwd(h, g, du):
    rs = _rms(h)
    n = h * rs
    dn = du * g
    dh = rs * (dn - n * jnp.mean(dn * n, axis=-1, keepdims=True))
    return dh, jnp.sum(du * n, axis=0, keepdims=True)


def _acc_rows(ref, i, row):
    @pl.when(i == 0)
    def _():
        ref[...] = jnp.zeros_like(ref)
    ref[...] += jnp.broadcast_to(row, ref.shape)


def _row_call(body, name, n_rows, tm, row_ins, const_ins, row_outs, acc_outs):
    n_ri, n_ci, n_ro = len(row_ins), len(const_ins), len(row_outs)

    def kern(*refs):
        i = pl.program_id(0)
        body(i, refs[:n_ri], refs[n_ri:n_ri + n_ci], refs[n_ri + n_ci:n_ri + n_ci + n_ro],
             refs[n_ri + n_ci + n_ro:])

    in_specs = [pl.BlockSpec((tm, a.shape[1]), lambda i: (i, 0)) for a in row_ins]
    in_specs += [pl.BlockSpec(a.shape, lambda i: (0, 0), pipeline_mode=pl.Buffered(1)) for a in const_ins]
    out_specs = [pl.BlockSpec((tm, c), lambda i: (i, 0)) for c, _ in row_outs]
    out_specs += [pl.BlockSpec((8, c), lambda i: (0, 0)) for c in acc_outs]
    out_shape = [jax.ShapeDtypeStruct((n_rows, c), dt) for c, dt in row_outs]
    out_shape += [jax.ShapeDtypeStruct((8, c), F32) for c in acc_outs]
    return pl.pallas_call(kern, grid=(n_rows // tm,), in_specs=in_specs, out_specs=out_specs,
                          out_shape=out_shape, name=name, compiler_params=_params(1))(*row_ins, *const_ins)


def _in_proj(x, g_mix, w_all, tm):
    def body(i, ins, consts, outs, accs):
        x_ref, = ins
        g_ref, w_ref = consts
        u_ref, zm_ref, zfg_ref = outs
        xv = x_ref[...]
        u = ((xv * _rms(xv)) * g_ref[...]).astype(BF16)
        u_ref[...] = u
        for s, n in _chunks(N_MAIN, 768):
            zm_ref[:, s:s + n] = _dot(u, w_ref[:, s:s + n]).astype(BF16)
        for s, n in _chunks(N_FPAD + N_GATE, 512):
            zfg_ref[:, s:s + n] = _dot(u, w_ref[:, N_MAIN + s:N_MAIN + s + n])

    return _row_call(body, "in_proj", x.shape[0], tm, [x], [g_mix, w_all],
                     [(D_MODEL, BF16), (N_MAIN, BF16), (N_FPAD + N_GATE, F32)], [])


def _mix_fwd(attn_a, attn_b, zfg, x, w_sa, w_fo, w_mo, g_mlp, tm):
    def body(i, ins, consts, outs, accs):
        aa_ref, ab_ref, zfg_ref, x_ref = ins
        wsa_ref, wfo_ref, wmo_ref, g_ref = consts
        ya_ref, yb_ref, mx_ref, h1_ref, u2_ref = outs
        ya = _dot(aa_ref[...], wsa_ref[...])
        yb = _dot(ab_ref[...], wfo_ref[...])
        g0 = _sigmoid(zfg_ref[:, N_FPAD:N_FPAD + D_MODEL])
        g1 = _sigmoid(zfg_ref[:, N_FPAD + D_MODEL:N_FPAD + 2 * D_MODEL])
        mixed = (g0 * ya + g1 * yb).astype(BF16)
        ya_ref[...] = ya.astype(BF16)
        yb_ref[...] = yb.astype(BF16)
        mx_ref[...] = mixed
        h1 = x_ref[...] + _dot(mixed, wmo_ref[...])
        h1_ref[...] = h1
        u2_ref[...] = ((h1 * _rms(h1)) * g_ref[...]).astype(BF16)

    return _row_call(body, "mix_fwd", x.shape[0], tm, [attn_a, attn_b, zfg, x], [w_sa, w_fo, w_mo, g_mlp],
                     [(D_MODEL, BF16), (D_MODEL, BF16), (D_MODEL, BF16), (D_MODEL, F32), (D_MODEL, BF16)], [])


def _ffn_fwd(u2, h1, w_ff1, w_ff2, tm):
    def body(i, ins, consts, outs, accs):
        u_ref, h1_ref = ins
        w1_ref, w2_ref = consts
        a_ref, r_ref, h2_ref = outs
        u = u_ref[...]
        acc = h1_ref[...]
        for s, n in _chunks(D_FF, 512):
            a = _dot(u, w1_ref[:, s:s + n])
            a_ref[:, s:s + n] = a.astype(BF16)
            r = jnp.square(jnp.maximum(a, 0.0)).astype(BF16)
            r_ref[:, s:s + n] = r
            acc = acc + _dot(r, w2_ref[s:s + n, :])
        h2_ref[...] = acc

    return _row_call(body, "ffn_fwd", u2.shape[0], tm, [u2, h1], [w_ff1, w_ff2],
                     [(D_FF, BF16), (D_FF, BF16), (D_MODEL, F32)], [])


def _head_fwd_bwd(h2, p, tgt, g_ple, w_pg, w_pp, g_fin, tm):
    def body(i, ins, consts, outs, accs):
        h2_ref, p_ref, t_ref = ins
        gp_ref, wpg_ref, wpp_ref, gf_ref = consts
        dh3_ref, dlg_ref, dpp_ref, u3_ref = outs
        loss_ref, dgf_ref = accs
        h2 = h2_ref[...]
        u3 = ((h2 * _rms(h2)) * gp_ref[...]).astype(BF16)
        u3_ref[...] = u3
        pg = _sigmoid(_dot(u3, wpg_ref[...]))
        pp = _dot(p_ref[...].astype(BF16), wpp_ref[...])
        h3 = h2 + pg * pp
        rs3 = _rms(h3)
        n3 = h3 * rs3
        gf = gf_ref[...]
        err = n3 * gf - t_ref[...]
        row_loss = 0.5 * jnp.mean(err * err, axis=-1, keepdims=True)
        _acc_rows(loss_ref, i, jnp.broadcast_to(jnp.sum(row_loss, axis=0, keepdims=True), (1, LANES)))
        dy = err * (1.0 / D_MODEL)
        _acc_rows(dgf_ref, i, jnp.sum(dy * n3, axis=0, keepdims=True))
        dn = dy * gf
        dh3 = rs3 * (dn - n3 * jnp.mean(dn * n3, axis=-1, keepdims=True))
        dh3_ref[...] = dh3
        dpp_ref[...] = (dh3 * pg).astype(BF16)
        dlg_ref[...] = ((dh3 * pp) * pg * (1.0 - pg)).astype(BF16)

    return _row_call(body, "head_fwd_bwd", h2.shape[0], tm, [h2, p, tgt], [g_ple, w_pg, w_pp, g_fin],
                     [(D_MODEL, F32), (D_MODEL, BF16), (D_MODEL, BF16), (D_MODEL, BF16)], [LANES, D_MODEL])


def _ffn_bwd_a(dlg, dh3, h2, a, w_pg_t, g_ple, w_ff2_t, tm):
    def body(i, ins, consts, outs, accs):
        dlg_ref, dh3_ref, h2_ref, a_ref = ins
        wpgt_ref, gp_ref, w2t_ref = consts
        dh2_ref, dh2b_ref, da_ref = outs
        dgp_ref, = accs
        du3 = _dot(dlg_ref[...], wpgt_ref[...])
        dh, dg = _rms_bwd(h2_ref[...], gp_ref[...], du3)
        _acc_rows(dgp_ref, i, dg)
        dh2 = dh3_ref[...] + dh
        dh2_ref[...] = dh2
        dh2b = dh2.astype(BF16)
        dh2b_ref[...] = dh2b
        for s, n in _chunks(D_FF, 512):
            dr = _dot(dh2b, w2t_ref[:, s:s + n])
            av = a_ref[:, s:s + n].astype(F32)
            da_ref[:, s:s + n] = (dr * (2.0 * jnp.maximum(av, 0.0))).astype(BF16)

    return _row_call(body, "ffn_bwd_a", h2.shape[0], tm, [dlg, dh3, h2, a], [w_pg_t, g_ple, w_ff2_t],
                     [(D_MODEL, F32), (D_MODEL, BF16), (D_FF, BF16)], [D_MODEL])


def _ffn_bwd_b(da, dh2, h1, ya, yb, zfg, w_ff1_t, g_mlp, w_mo_t, w_sa_t, w_fo_t, tm):
    def body(i, ins, consts, outs, accs):
        da_ref, dh2_ref, h1_ref, ya_ref, yb_ref, zfg_ref = ins
        w1t_ref, gm_ref, wmot_ref, wsat_ref, wfot_ref = consts
        dh1_ref, dh1b_ref, dgl_ref, dya_ref, dyb_ref, daa_ref, dab_ref = outs
        dgm_ref, = accs
        du2 = _dot(da_ref[...], w1t_ref[...])
        dh, dg = _rms_bwd(h1_ref[...], gm_ref[...], du2)
        _acc_rows(dgm_ref, i, dg)
        dh1 = dh2_ref[...] + dh
        dh1_ref[...] = dh1
        dh1b = dh1.astype(BF16)
        dh1b_ref[...] = dh1b
        dmx = _dot(dh1b, wmot_ref[...])
        g0 = _sigmoid(zfg_ref[:, N_FPAD:N_FPAD + D_MODEL])
        g1 = _sigmoid(zfg_ref[:, N_FPAD + D_MODEL:N_FPAD + 2 * D_MODEL])
        dya = (dmx * g0).astype(BF16)
        dyb = (dmx * g1).astype(BF16)
        dya_ref[...] = dya
        dyb_ref[...] = dyb
        dgl_ref[:, 0:D_MODEL] = ((dmx * ya_ref[...].astype(F32)) * g0 * (1.0 - g0)).astype(BF16)
        dgl_ref[:, D_MODEL:2 * D_MODEL] = ((dmx * yb_ref[...].astype(F32)) * g1 * (1.0 - g1)).astype(BF16)
        daa_ref[...] = _dot(dya, wsat_ref[...]).astype(BF16)
        dab_ref[...] = _dot(dyb, wfot_ref[...]).astype(BF16)

    half = D_MODEL // 2
    return _row_call(body, "ffn_bwd_b", h1.shape[0], tm, [da, dh2, h1, ya, yb, zfg],
                     [w_ff1_t, g_mlp, w_mo_t, w_sa_t, w_fo_t],
                     [(D_MODEL, F32), (D_MODEL, BF16), (N_GATE, BF16), (D_MODEL, BF16), (D_MODEL, BF16),
                      (half, BF16), (half, BF16)], [D_MODEL])


def _in_proj_bwd(dz, dh1, x, w_all_t, g_mix, tm):
    def body(i, ins, consts, outs, accs):
        dz_ref, dh1_ref, x_ref = ins
        wt_ref, g_ref = consts
        dx_ref, = outs
        dgx_ref, = accs
        du1 = _dot(dz_ref[...], wt_ref[...])
        dh, dg = _rms_bwd(x_ref[...], g_ref[...], du1)
        _acc_rows(dgx_ref, i, dg)
        dx_ref[...] = dh1_ref[...] + dh

    return _row_call(body, "in_proj_bwd", x.shape[0], tm, [dz, dh1, x], [w_all_t, g_mix],
                     [(D_MODEL, F32)], [D_MODEL])


def _matmul_tn(a, b, name, ts):
    n_rows, ka = a.shape
    n = b.shape[1]
    tk = min(ka, 1024)
    tn = 896 if n % 1024 else 1024
    assert ka % tk == 0 and n % tn == 0 and n_rows % ts == 0

    def kern(a_ref, b_ref, o_ref):
        @pl.when(pl.program_id(2) == 0)
        def _():
            o_ref[...] = jnp.zeros_like(o_ref)
        o_ref[...] += lax.dot_general(a_ref[...].astype(BF16), b_ref[...], _TN, preferred_element_type=F32)

    return pl.pallas_call(
        kern, grid=(ka // tk, n // tn, n_rows // ts),
        in_specs=[pl.BlockSpec((ts, tk), lambda i, j, s: (s, i)), pl.BlockSpec((ts, tn), lambda i, j, s: (s, j))],
        out_specs=pl.BlockSpec((tk, tn), lambda i, j, s: (i, j)),
        out_shape=jax.ShapeDtypeStruct((ka, n), F32), name=name, compiler_params=_params(3))(a, b)


SCAN_CHUNK = 512


def _decay_cumsum(f_t, b_col):
    n_tok = f_t.shape[1]
    ch = min(SCAN_CHUNK, n_tok)

    def kern(f_ref, b_ref, c_ref):
        r = lax.broadcasted_iota(jnp.int32, (ch, ch), 0)
        c = lax.broadcasted_iota(jnp.int32, (ch, ch), 1)
        tri = (r <= c).astype(F32)
        carry = jnp.zeros((8, 1), F32)
        for k in range(n_tok // ch):
            xv = f_ref[:, k * ch:(k + 1) * ch] + b_ref[...]
            lf = jnp.minimum(xv, 0.0) - jnp.log(1.0 + jnp.exp(-jnp.abs(xv)))
            cs = jnp.dot(lf, tri, precision=lax.Precision.HIGHEST, preferred_element_type=F32) + carry
            c_ref[:, k * ch:(k + 1) * ch] = cs
            carry = cs[:, ch - 1:ch]

    return pl.pallas_call(kern, out_shape=jax.ShapeDtypeStruct((8, n_tok), F32), name="decay_cumsum",
                          compiler_params=_params(0))(f_t, b_col)


def _decay_bwd(g_t, f_t, b_col):
    n_tok = f_t.shape[1]
    ch = min(SCAN_CHUNK, n_tok)

    def kern(g_ref, f_ref, b_ref, df_ref, db_ref):
        r = lax.broadcasted_iota(jnp.int32, (ch, ch), 0)
        c = lax.broadcasted_iota(jnp.int32, (ch, ch), 1)
        tri = (r >= c).astype(F32)
        carry = jnp.zeros((8, 1), F32)
        tot = jnp.zeros((8, 1), F32)
        for k in reversed(range(n_tok // ch)):
            gv = g_ref[:, k * ch:(k + 1) * ch]
            rc = jnp.dot(gv, tri, precision=lax.Precision.HIGHEST, preferred_element_type=F32) + carry
            carry = rc[:, 0:1]
            xv = f_ref[:, k * ch:(k + 1) * ch] + b_ref[...]
            df = -rc / (1.0 + jnp.exp(xv))
            df_ref[:, k * ch:(k + 1) * ch] = df
            tot = tot + jnp.sum(df, axis=1, keepdims=True)
        db_ref[...] = jnp.broadcast_to(tot, db_ref.shape)

    return pl.pallas_call(kern, out_shape=[jax.ShapeDtypeStruct((8, n_tok), F32),
                                           jax.ShapeDtypeStruct((8, LANES), F32)],
                          name="decay_bwd", compiler_params=_params(0))(g_t, f_t, b_col)


def _swa_band_mask(n):
    row = lax.broadcasted_iota(jnp.int32, (SWA_BLOCK, 2 * SWA_BLOCK), 0) + SWA_BLOCK
    col = lax.broadcasted_iota(jnp.int32, (SWA_BLOCK, 2 * SWA_BLOCK), 1)
    cd = (row >> CHUNK_SHIFT) - (col >> CHUNK_SHIFT)
    first_real = jnp.where(n > 0, 0, SWA_BLOCK)
    ok = (cd >= 0) & (cd <= WINDOW_CHUNKS) & (col >= first_real)
    dist = jnp.abs(row - col).astype(F32)
    return ok, dist


def _swap_halves(t):
    return pltpu.roll(t.astype(F32), HEAD_DIM, axis=1).astype(t.dtype)


def _swa_specs():
    blk = SWA_BLOCK
    q = pl.BlockSpec((blk, 4 * LANES), lambda n: (n, 0))
    kp = pl.BlockSpec((blk, LANES), lambda n: (jnp.maximum(n - 1, 0), 4))
    kc = pl.BlockSpec((blk, LANES), lambda n: (n, 4))
    vp = pl.BlockSpec((blk, LANES), lambda n: (jnp.maximum(n - 1, 0), 5))
    vc = pl.BlockSpec((blk, LANES), lambda n: (n, 5))
    return q, kp, kc, vp, vc


def _swa_fwd(zm, sinks):
    n_tok = zm.shape[0]
    blk = SWA_BLOCK

    def kern(q_ref, kp_ref, kc_ref, vp_ref, vc_ref, sink_ref, o_ref, lse_ref):
        n = pl.program_id(0)
        ok, dist = _swa_band_mask(n)
        k2 = jnp.concatenate([kp_ref[...], kc_ref[...]], axis=0)
        v2 = jnp.concatenate([vp_ref[...], vc_ref[...]], axis=0)
        ksw, vsw = _swap_halves(k2), _swap_halves(v2)
        lane = lax.broadcasted_iota(jnp.int32, (blk, LANES), 1)
        lo = lane < HEAD_DIM
        lse_t = jnp.zeros((blk, LANES), F32)
        for pair in range(SWA_HEADS // 2):
            q2 = q_ref[:, pair * LANES:(pair + 1) * LANES]
            kvh = pair // 2
            outs = []
            for a in range(2):
                h = 2 * pair + a
                qa = jnp.where(lo if a == 0 else ~lo, q2, jnp.zeros_like(q2)) * SCALE
                kx, vx = (k2, v2) if a == kvh else (ksw, vsw)
                s = lax.dot_general(qa, kx, _NT, preferred_element_type=F32)
                s = jnp.where(ok, s - (2.0 ** -(h + 1)) * dist, NEG)
                sink = sink_ref[h]
                m = jnp.maximum(jnp.max(s, axis=-1, keepdims=True), sink)
                e = jnp.exp(s - m)
                l = jnp.sum(e, axis=-1, keepdims=True) + jnp.exp(sink - m)
                pn = (e * (1.0 / l)).astype(BF16)
                outs.append(_dot(pn, vx))
                lse_t = jnp.where(lane == h, m + jnp.log(l), lse_t)
            o_ref[:, pair * LANES:(pair + 1) * LANES] = jnp.where(lo, outs[0], outs[1]).astype(BF16)
        lse_ref[...] = lse_t

    q, kp, kc, vp, vc = _swa_specs()
    return pl.pallas_call(
        kern, grid=(n_tok // blk,),
        in_specs=[q, kp, kc, vp, vc, pl.BlockSpec(memory_space=pltpu.SMEM)],
        out_specs=[pl.BlockSpec((blk, 4 * LANES), lambda n: (n, 0)), pl.BlockSpec((blk, LANES), lambda n: (n, 0))],
        out_shape=[jax.ShapeDtypeStruct((n_tok, 4 * LANES), BF16), jax.ShapeDtypeStruct((n_tok, LANES), F32)],
        name="swa_fwd", compiler_params=_params(1))(zm, zm, zm, zm, zm, sinks)


def _swa_bwd(zm, sinks, d_out, out, lse):
    n_tok = zm.shape[0]
    blk = SWA_BLOCK

    def kern(q_ref, kp_ref, kc_ref, vp_ref, vc_ref, do_ref, o_ref, lse_ref, sink_ref,
             dq_ref, dkp_ref, dkc_ref, dvp_ref, dvc_ref, dsk_ref):
        n = pl.program_id(0)

        @pl.when(n == 0)
        def _():
            dsk_ref[...] = jnp.zeros_like(dsk_ref)

        ok, dist = _swa_band_mask(n)
        k2 = jnp.concatenate([kp_ref[...], kc_ref[...]], axis=0)
        v2 = jnp.concatenate([vp_ref[...], vc_ref[...]], axis=0)
        ksw, vsw = _swap_halves(k2), _swap_halves(v2)
        lane = lax.broadcasted_iota(jnp.int32, (blk, LANES), 1)
        lo = lane < HEAD_DIM
        lse_t = lse_ref[...]
        zero = jnp.zeros((2 * blk, LANES), F32)
        dk_same, dk_swap, dv_same, dv_swap = zero, zero, zero, zero
        for pair in range(SWA_HEADS // 2):
            cols = slice(pair * LANES, (pair + 1) * LANES)
            q2, do2, o2 = q_ref[:, cols], do_ref[:, cols], o_ref[:, cols]
            kvh = pair // 2
            dqs = []
            for a in range(2):
                h = 2 * pair + a
                half = lo if a == 0 else ~lo
                qa = jnp.where(half, q2, jnp.zeros_like(q2)) * SCALE
                doa = jnp.where(half, do2, jnp.zeros_like(do2))
                kx, vx = (k2, v2) if a == kvh else (ksw, vsw)
                s = lax.dot_general(qa, kx, _NT, preferred_element_type=F32)
                s = jnp.where(ok, s - (2.0 ** -(h + 1)) * dist, NEG)
                lse_h = lse_t[:, h:h + 1]
                prob = jnp.exp(s - lse_h)
                dd = jnp.sum(doa.astype(F32) * o2.astype(F32), axis=-1, keepdims=True)
                dp = lax.dot_general(doa, vx, _NT, preferred_element_type=F32)
                ds = (prob * (dp - dd)).astype(BF16)
                p_sink = jnp.exp(sink_ref[h] - lse_h)
                dsk_ref[h:h + 1, :] += jnp.broadcast_to(-jnp.sum(p_sink * dd, axis=0, keepdims=True), (1, LANES))
                dqs.append(_dot(ds, kx) * SCALE)
                dk_c = lax.dot_general(ds, qa, _TN, preferred_element_type=F32)
                dv_c = lax.dot_general(prob.astype(BF16), doa, _TN, preferred_element_type=F32)
                if a == kvh:
                    dk_same, dv_same = dk_same + dk_c, dv_same + dv_c
                else:
                    dk_swap, dv_swap = dk_swap + dk_c, dv_swap + dv_c
            dq_ref[:, cols] = jnp.where(lo, dqs[0], dqs[1]).astype(BF16)
        dk = dk_same + pltpu.roll(dk_swap, HEAD_DIM, axis=1)
        dv = dv_same + pltpu.roll(dv_swap, HEAD_DIM, axis=1)
        dkp_ref[...] = dk[0:blk]
        dkc_ref[...] = dk[blk:2 * blk]
        dvp_ref[...] = dv[0:blk]
        dvc_ref[...] = dv[blk:2 * blk]

    q, kp, kc, vp, vc = _swa_specs()
    wide = pl.BlockSpec((blk, 4 * LANES), lambda n: (n, 0))
    narrow = pl.BlockSpec((blk, LANES), lambda n: (n, 0))
    part = jax.ShapeDtypeStruct((n_tok, LANES), F32)
    return pl.pallas_call(
        kern, grid=(n_tok // blk,),
        in_specs=[q, kp, kc, vp, vc, wide, wide, narrow, pl.BlockSpec(memory_space=pltpu.SMEM)],
        out_specs=[wide, narrow, narrow, narrow, narrow, pl.BlockSpec((8, LANES), lambda n: (0, 0))],
        out_shape=[jax.ShapeDtypeStruct((n_tok, 4 * LANES), BF16), part, part, part, part,
                   jax.ShapeDtypeStruct((8, LANES), F32)],
        name="swa_bwd", compiler_params=_params(1))(zm, zm, zm, zm, zm, d_out, out, lse, sinks)


Q_COL, K_COL, V_COL = 6, 10, 14


def _causal(t, tq, tk):
    row = lax.broadcasted_iota(jnp.int32, (tq, tk), 0)
    col = lax.broadcasted_iota(jnp.int32, (tq, tk), 1)
    return jnp.where(col <= row, t, NEG)


def _fox_fwd(zm, c_pairs, tq):
    n_tok = zm.shape[0]
    nq = n_tok // tq
    pairs = [(i, j) for i in range(nq) for j in range(i + 1)]
    ii = np.asarray([p[0] for p in pairs], np.int32)
    jj = np.asarray([p[1] for p in pairs], np.int32)

    def kern(ii_ref, jj_ref, q_ref, k_ref, v_ref, ck_ref, o_ref, ln_ref, qs_ref, m_ref, l_ref, acc_ref):
        step = pl.program_id(1)
        i, j = ii_ref[step], jj_ref[step]
        lane = lax.broadcasted_iota(jnp.int32, (tq, LANES), 1)
        lo = lane < HEAD_DIM

        @pl.when(j == 0)
        def _():
            q2 = q_ref[...]
            zq = jnp.zeros_like(q2)
            qs_ref[0] = jnp.where(lo, q2, zq) * SCALE
            qs_ref[1] = jnp.where(lo, zq, q2) * SCALE
            m_ref[...] = jnp.full(m_ref.shape, NEG, F32)
            l_ref[...] = jnp.zeros(l_ref.shape, F32)
            acc_ref[...] = jnp.zeros(acc_ref.shape, F32)

        def update(diag):
            kv, vv = k_ref[...], v_ref[...]
            for a in range(2):
                t = lax.dot_general(qs_ref[a], kv, _NT, preferred_element_type=F32) - ck_ref[a:a + 1, :]
                if diag:
                    t = _causal(t, tq, tq)
                m_old = m_ref[a]
                m_new = jnp.maximum(m_old, jnp.max(t, axis=-1, keepdims=True))
                alpha = jnp.exp(m_old - m_new)
                e = jnp.exp(t - m_new)
                l_ref[a] = alpha * l_ref[a] + jnp.sum(e, axis=-1, keepdims=True)
                acc_ref[a] = alpha * acc_ref[a] + _dot(e.astype(BF16), vv)
                m_ref[a] = m_new

        @pl.when(j < i)
        def _():
            update(False)

        @pl.when(j == i)
        def _():
            update(True)
            o_ref[...] = jnp.where(lo, acc_ref[0] / l_ref[0], acc_ref[1] / l_ref[1]).astype(BF16)
            ln_ref[...] = jnp.where(lo, m_ref[0] + jnp.log(l_ref[0]), m_ref[1] + jnp.log(l_ref[1]))

    blk = (tq, LANES)
    grid_spec = pltpu.PrefetchScalarGridSpec(
        num_scalar_prefetch=2, grid=(FOX_HEADS // 2, len(pairs)),
        in_specs=[pl.BlockSpec(blk, lambda hp, s, ii, jj: (ii[s], Q_COL + hp)),
                  pl.BlockSpec(blk, lambda hp, s, ii, jj: (jj[s], K_COL + hp)),
                  pl.BlockSpec(blk, lambda hp, s, ii, jj: (jj[s], V_COL + hp)),
                  pl.BlockSpec((None, 2, tq), lambda hp, s, ii, jj: (hp, 0, jj[s]))],
        out_specs=[pl.BlockSpec(blk, lambda hp, s, ii, jj: (ii[s], hp)),
                   pl.BlockSpec(blk, lambda hp, s, ii, jj: (ii[s], hp))],
        scratch_shapes=[pltpu.VMEM((2, tq, LANES), BF16), pltpu.VMEM((2, tq, 1), F32),
                        pltpu.VMEM((2, tq, 1), F32), pltpu.VMEM((2, tq, LANES), F32)])
    return pl.pallas_call(
        kern, grid_spec=grid_spec,
        out_shape=[jax.ShapeDtypeStruct((n_tok, 4 * LANES), BF16), jax.ShapeDtypeStruct((n_tok, 4 * LANES), F32)],
        name="fox_fwd", compiler_params=_params(2))(ii, jj, zm, zm, zm, c_pairs)


def _fox_bwd(zm, c_pairs, d_out, out, lnorm, tq):
    n_tok = zm.shape[0]
    nq = n_tok // tq
    pairs = [(i, j) for j in range(nq) for i in range(j, nq)]
    ii = np.asarray([p[0] for p in pairs], np.int32)
    jj = np.asarray([p[1] for p in pairs], np.int32)

    def kern(ii_ref, jj_ref, q_ref, k_ref, v_ref, ck_ref, do_ref, o_ref, ln_ref,
             dq_ref, dk_ref, dv_ref, gs_ref, rs_ref, dk_acc, dv_acc):
        step = pl.program_id(1)
        i, j = ii_ref[step], jj_ref[step]
        lane = lax.broadcasted_iota(jnp.int32, (tq, LANES), 1)
        lo = lane < HEAD_DIM

        @pl.when(step == 0)
        def _():
            dq_ref[...] = jnp.zeros_like(dq_ref)
            rs_ref[...] = jnp.zeros_like(rs_ref)

        @pl.when(i == j)
        def _():
            dk_acc[...] = jnp.zeros(dk_acc.shape, F32)
            dv_acc[...] = jnp.zeros(dv_acc.shape, F32)
            gs_ref[...] = jnp.zeros_like(gs_ref)

        def update(diag):
            q2, do2, o2 = q_ref[...], do_ref[...], o_ref[...]
            kv, vv = k_ref[...], v_ref[...]
            ln2 = ln_ref[...]
            zq = jnp.zeros_like(q2)
            of = o2.astype(F32)
            dqs, rsums = [], []
            for a in range(2):
                half = lo if a == 0 else ~lo
                qa = jnp.where(half, q2, zq) * SCALE
                doa = jnp.where(half, do2, zq)
                t = lax.dot_general(qa, kv, _NT, preferred_element_type=F32) - ck_ref[a:a + 1, :]
                if diag:
                    t = _causal(t, tq, tq)
                ln_a = ln2[:, a * HEAD_DIM:a * HEAD_DIM + 1]
                prob = jnp.exp(t - ln_a)
                dd = jnp.sum(doa.astype(F32) * of, axis=-1, keepdims=True)
                dp = lax.dot_general(doa, vv, _NT, preferred_element_type=F32)
                ds32 = prob * (dp - dd)
                gs_ref[a:a + 1, :] += jnp.sum(ds32, axis=0, keepdims=True)
                rsums.append(jnp.sum(ds32, axis=-1, keepdims=True))
                ds = ds32.astype(BF16)
                dqs.append(_dot(ds, kv))
                dk_acc[a] += lax.dot_general(ds, qa, _TN, preferred_element_type=F32)
                dv_acc[a] += lax.dot_general(prob.astype(BF16), doa, _TN, preferred_element_type=F32)
            rows = pl.ds(pl.multiple_of(i * tq, tq), tq)
            dq_ref[rows, :] += jnp.where(lo, dqs[0], dqs[1]) * SCALE
            rs_ref[rows, :] += jnp.where(lo, rsums[0], rsums[1])

        @pl.when(i == j)
        def _():
            update(True)

        @pl.when(i > j)
        def _():
            update(False)

        @pl.when(i == nq - 1)
        def _():
            dk_ref[...] = (dk_acc[0] + dk_acc[1]).astype(BF16)
            dv_ref[...] = (dv_acc[0] + dv_acc[1]).astype(BF16)

    blk = (tq, LANES)
    by_i = lambda col: (lambda hp, s, ii, jj: (ii[s], col + hp))
    by_j = lambda col: (lambda hp, s, ii, jj: (jj[s], col + hp))
    per_key = pl.BlockSpec((None, 2, tq), lambda hp, s, ii, jj: (hp, 0, jj[s]))
    grid_spec = pltpu.PrefetchScalarGridSpec(
        num_scalar_prefetch=2, grid=(FOX_HEADS // 2, len(pairs)),
        in_specs=[pl.BlockSpec(blk, by_i(Q_COL)), pl.BlockSpec(blk, by_j(K_COL)), pl.BlockSpec(blk, by_j(V_COL)),
                  per_key, pl.BlockSpec(blk, by_i(0)), pl.BlockSpec(blk, by_i(0)), pl.BlockSpec(blk, by_i(0))],
        out_specs=[pl.BlockSpec((n_tok, LANES), lambda hp, s, ii, jj: (0, hp)),
                   pl.BlockSpec(blk, by_j(0)), pl.BlockSpec(blk, by_j(0)), per_key,
                   pl.BlockSpec((n_tok, LANES), lambda hp, s, ii, jj: (0, hp))],
        scratch_shapes=[pltpu.VMEM((2, tq, LANES), F32), pltpu.VMEM((2, tq, LANES), F32)])
    wide = lambda dt: jax.ShapeDtypeStruct((n_tok, 4 * LANES), dt)
    return pl.pallas_call(
        kern, grid_spec=grid_spec,
        out_shape=[wide(F32), wide(BF16), wide(BF16), jax.ShapeDtypeStruct(c_pairs.shape, F32), wide(F32)],
        name="fox_bwd", compiler_params=_params(2))(ii, jj, zm, zm, zm, c_pairs, d_out, out, lnorm)


def _my_pos():
    return lax.axis_index("x"), lax.axis_index("y"), lax.axis_index("c")


def _all_gather(shard):
    n_rows, n_cols = shard.shape

    def kern(x_ref, out_ref, send_sems, recv_sems, local_sem):
        x, y, c = _my_pos()
        me, sibling = (x, y, c), (x, y, 1 - c)
        chips = [(1 - x, y), (x, 1 - y), (1 - x, 1 - y)]

        def slot(px, py, pc):
            return out_ref.at[4 * px + 2 * py + pc]

        def copy(k, block, to, src=None):
            return pltpu.make_async_remote_copy(
                src_ref=slot(*block) if src is None else src, dst_ref=slot(*block),
                send_sem=send_sems.at[k], recv_sem=recv_sems.at[k], device_id=to, device_id_type=MESH)

        mine = pltpu.make_async_copy(x_ref, slot(*me), local_sem)
        mine.start()
        first = [copy(0, me, sibling, src=x_ref)]
        first += [copy(1 + k, me, (*chip, c), src=x_ref) for k, chip in enumerate(chips)]
        for cp in first:
            cp.start()
        passed = [copy(4 + k, (*chip, c), sibling) for k, chip in enumerate(chips)]
        for k, chip in enumerate(chips):
            copy(1 + k, (*chip, c), me).wait_recv()
            passed[k].start()
        copy(0, sibling, me).wait_recv()
        for k, chip in enumerate(chips):
            copy(4 + k, (*chip, 1 - c), me).wait_recv()
        for cp in first + passed:
            cp.wait_send()
        mine.wait()

    return pl.pallas_call(
        kern, out_shape=jax.ShapeDtypeStruct((N_DEV, n_rows, n_cols), shard.dtype),
        in_specs=[pl.BlockSpec(memory_space=pl.ANY)], out_specs=pl.BlockSpec(memory_space=pl.ANY),
        scratch_shapes=[pltpu.SemaphoreType.DMA((7,)), pltpu.SemaphoreType.DMA((7,)), pltpu.SemaphoreType.DMA],
        name="weight_all_gather")(shard)


def _grad_exchange(grads, small):
    _, n_rows, n_cols = grads.shape

    def kern(g_ref, s_ref, parts_ref, sall_ref, send_sems, recv_sems, ssend_sems, srecv_sems, local_sems):
        x, y, c = _my_pos()
        my_id = 4 * x + 2 * y + c
        own = pltpu.make_async_copy(g_ref.at[my_id], parts_ref.at[0], local_sems.at[0])
        own_s = pltpu.make_async_copy(s_ref, sall_ref.at[my_id], local_sems.at[1])
        own.start()
        own_s.start()
        sends = []
        for k in range(1, N_DEV):
            px, py, pc = x ^ (k >> 2), y ^ ((k >> 1) & 1), c ^ (k & 1)
            big = pltpu.make_async_remote_copy(
                src_ref=g_ref.at[4 * px + 2 * py + pc], dst_ref=parts_ref.at[k],
                send_sem=send_sems.at[k], recv_sem=recv_sems.at[k], device_id=(px, py, pc), device_id_type=MESH)
            sml = pltpu.make_async_remote_copy(
                src_ref=s_ref, dst_ref=sall_ref.at[my_id],
                send_sem=ssend_sems.at[k], recv_sem=srecv_sems.at[k], device_id=(px, py, pc), device_id_type=MESH)
            sml.start()
            big.start()
            sends += [(big, sml, 4 * px + 2 * py + pc)]
        for k, (big, sml, peer_id) in enumerate(sends, start=1):
            pltpu.make_async_remote_copy(
                src_ref=s_ref, dst_ref=sall_ref.at[peer_id], send_sem=ssend_sems.at[k], recv_sem=srecv_sems.at[k],
                device_id=(x, y, c), device_id_type=MESH).wait_recv()
            big.wait_recv()
        for big, sml, _ in sends:
            big.wait_send()
            sml.wait_send()
        own.wait()
        own_s.wait()

    any_spec = pl.BlockSpec(memory_space=pl.ANY)
    return pl.pallas_call(
        kern, out_shape=[jax.ShapeDtypeStruct((N_DEV, n_rows, n_cols), grads.dtype),
                         jax.ShapeDtypeStruct((N_DEV,) + small.shape, small.dtype)],
        in_specs=[any_spec, any_spec], out_specs=[any_spec, any_spec],
        scratch_shapes=[pltpu.SemaphoreType.DMA((N_DEV,)), pltpu.SemaphoreType.DMA((N_DEV,)),
                        pltpu.SemaphoreType.DMA((N_DEV,)), pltpu.SemaphoreType.DMA((N_DEV,)),
                        pltpu.SemaphoreType.DMA((2,))],
        name="grad_exchange")(grads, small)


def _adamw(parts, w, m, v, name, tr):
    n_parts, n_rows, n_cols = parts.shape

    def kern(p_ref, w_ref, m_ref, v_ref, g_out, d_out, m_out, v_out):
        g = p_ref[0]
        for k in range(1, n_parts):
            g = g + p_ref[k]
        m_new = ADAM_B1 * m_ref[...] + (1.0 - ADAM_B1) * g
        v_new = ADAM_B2 * v_ref[...] + (1.0 - ADAM_B2) * jnp.square(g)
        m_hat = m_new / (1.0 - ADAM_B1 ** ADAM_STEP)
        v_hat = v_new / (1.0 - ADAM_B2 ** ADAM_STEP)
        g_out[...] = g
        d_out[...] = -ADAM_LR * (m_hat / (jnp.sqrt(v_hat) + ADAM_EPS) + ADAM_WD * w_ref[...])
        m_out[...] = m_new
        v_out[...] = v_new

    row = pl.BlockSpec((tr, n_cols), lambda i: (i, 0))
    out = jax.ShapeDtypeStruct((n_rows, n_cols), F32)
    return pl.pallas_call(
        kern, grid=(n_rows // tr,),
        in_specs=[pl.BlockSpec((n_parts, tr, n_cols), lambda i: (0, i, 0)), row, row, row],
        out_specs=[row, row, row, row], out_shape=[out, out, out, out], name=name,
        compiler_params=_params(1))(parts, w, m, v)


SHARDED = {
    "w_in": ((D_MODEL, D_IN), 1), "w_br_swa": ((512, D_MODEL), 1), "w_br_fox": ((512, D_MODEL), 1),
    "w_mix_out": ((D_MODEL, D_MODEL), 0), "w_ff1": ((D_MODEL, D_FF), 1), "w_ff2": ((D_FF, D_MODEL), 0),
    "w_ple_gate": ((D_MODEL, D_MODEL), 0), "w_ple_proj": ((PLE_DIM, D_MODEL), 1),
}
PACK_COLS = 1024
PACK_ELEMS = sum(s[0] * s[1] // N_DEV for s, _ in SHARDED.values())
PACK_TILE_ROWS = 200
PACK_ROWS = -(-PACK_ELEMS // (PACK_COLS * PACK_TILE_ROWS)) * PACK_TILE_ROWS
SMALL = ("g_mix", "g_mlp", "g_ple", "g_final", "b_forget", "swa_sinks")


def _pack_shards(shards):
    flat = jnp.concatenate([shards[n].reshape(-1) for n in SHARDED])
    return jnp.pad(flat, (0, PACK_ROWS * PACK_COLS - PACK_ELEMS)).reshape(PACK_ROWS, PACK_COLS)


def _unpack_shards(slab, lead):
    out, off = {}, 0
    flat = slab.reshape(-1)
    for n, (shape, axis) in SHARDED.items():
        shp = (shape[0] // N_DEV, shape[1]) if axis == 0 else (shape[0], shape[1] // N_DEV)
        out[n] = flat[off:off + shp[0] * shp[1]].reshape(lead + shp)
        off += shp[0] * shp[1]
    return out


def _unpack_gathered(slabs):
    out, off = {}, 0
    flat = slabs.reshape(N_DEV, -1)
    for n, (shape, axis) in SHARDED.items():
        size = shape[0] * shape[1] // N_DEV
        blk = flat[:, off:off + size]
        if axis == 0:
            out[n] = blk.reshape(shape)
        else:
            out[n] = blk.reshape(N_DEV, shape[0], shape[1] // N_DEV).transpose(1, 0, 2).reshape(shape)
        off += size
    return out


def _pack_full(full):
    cols = []
    for n, (shape, axis) in SHARDED.items():
        a = full[n]
        if axis == 0:
            cols.append(a.reshape(N_DEV, -1))
        else:
            cols.append(a.reshape(shape[0], N_DEV, shape[1] // N_DEV).transpose(1, 0, 2).reshape(N_DEV, -1))
    flat = jnp.concatenate(cols, axis=1)
    flat = jnp.pad(flat, ((0, 0), (0, PACK_ROWS * PACK_COLS - PACK_ELEMS)))
    return flat.reshape(N_DEV, PACK_ROWS, PACK_COLS)


def _pack_small(vals):
    rows = [jnp.pad(vals[n].reshape(-1), (0, PACK_COLS - vals[n].size)) for n in SMALL]
    rows += [jnp.zeros((PACK_COLS,), F32)] * (8 - len(SMALL))
    return jnp.stack(rows)


def _unpack_small(slab, like):
    return {n: slab[r, :like[n].size].reshape(like[n].shape) for r, n in enumerate(SMALL)}


def _local_step(x, p, tgt, w, small, tm, tq, ts):
    n_tok = x.shape[0]
    w_in = w["w_in"]
    w_all = jnp.concatenate([w_in[:, :N_MAIN], jnp.pad(w_in[:, N_MAIN:N_MAIN + FOX_HEADS], ((0, 0), (0, N_FPAD - FOX_HEADS))),
                             w_in[:, N_MAIN + FOX_HEADS:]], axis=1)
    row = lambda v: v.reshape(1, -1)
    g_mix, g_mlp, g_ple, g_fin = row(small["g_mix"]), row(small["g_mlp"]), row(small["g_ple"]), row(small["g_final"])
    sinks = small["swa_sinks"].reshape(-1)
    b_col = small["b_forget"].reshape(FOX_HEADS, 1)

    u1, zm, zfg = _in_proj(x, g_mix, w_all, tm)
    f_t = zfg[:, :FOX_HEADS].T
    c_pairs = _decay_cumsum(f_t, b_col).reshape(FOX_HEADS // 2, 2, n_tok)
    attn_a, lse_a = _swa_fwd(zm, sinks)
    attn_b, ln_b = _fox_fwd(zm, c_pairs, tq)
    ya, yb, mixed, h1, u2 = _mix_fwd(attn_a, attn_b, zfg, x, w["w_br_swa"], w["w_br_fox"], w["w_mix_out"], g_mlp, tm)
    a, r, h2 = _ffn_fwd(u2, h1, w["w_ff1"], w["w_ff2"], tm // 2)
    dh3, dlg, dpp, u3, loss_acc, dgf = _head_fwd_bwd(h2, p, tgt, g_ple, w["w_ple_gate"], w["w_ple_proj"], g_fin, tm)

    dh2, dh2b, da, dgp = _ffn_bwd_a(dlg, dh3, h2, a, w["w_ple_gate"].T, g_ple, w["w_ff2"].T, tm // 2)
    dh1, dh1b, dgl, dya, dyb, daa, dab, dgm = _ffn_bwd_b(
        da, dh2, h1, ya, yb, zfg, w["w_ff1"].T, g_mlp, w["w_mix_out"].T, w["w_br_swa"].T, w["w_br_fox"].T, tm // 2)
    dq_a, dkp, dkc, dvp, dvc, dsk = _swa_bwd(zm, sinks, daa, attn_a, lse_a)
    dq_b, dk_b, dv_b, g_pairs, rs = _fox_bwd(zm, c_pairs, dab, attn_b, ln_b, tq)

    up = lambda t: jnp.concatenate([t[SWA_BLOCK:], jnp.zeros((SWA_BLOCK, LANES), F32)], axis=0)
    dk_a, dv_a = dkc + up(dkp), dvc + up(dvp)
    rs_t = rs.reshape(n_tok, FOX_HEADS, HEAD_DIM)[:, :, 0].T
    df_t, db = _decay_bwd(g_pairs.reshape(FOX_HEADS, n_tok) - rs_t, f_t, b_col)
    df = jnp.pad(df_t.T, ((0, 0), (0, N_FPAD - FOX_HEADS)))
    dz = jnp.concatenate([dq_a, dk_a.astype(BF16), dv_a.astype(BF16), dq_b.astype(BF16), dk_b, dv_b,
                          df.astype(BF16), dgl], axis=1)
    dx, dgx = _in_proj_bwd(dz, dh1, x, w_all.T, g_mix, tm)

    dw_all = _matmul_tn(u1, dz, "dw_in", ts)
    dw = {
        "w_in": jnp.concatenate([dw_all[:, :N_MAIN], dw_all[:, N_MAIN:N_MAIN + FOX_HEADS], dw_all[:, N_MAIN + N_FPAD:]], axis=1),
        "w_br_swa": _matmul_tn(attn_a, dya, "dw_br_swa", ts),
        "w_br_fox": _matmul_tn(attn_b, dyb, "dw_br_fox", ts),
        "w_mix_out": _matmul_tn(mixed, dh1b, "dw_mix_out", ts),
        "w_ff1": _matmul_tn(u2, da, "dw_ff1", ts),
        "w_ff2": _matmul_tn(r, dh2b, "dw_ff2", ts),
        "w_ple_gate": _matmul_tn(u3, dlg, "dw_ple_gate", ts),
        "w_ple_proj": _matmul_tn(p, dpp, "dw_ple_proj", ts),
    }
    dsmall = {"g_mix": dgx[0], "g_mlp": dgm[0], "g_ple": dgp[0], "g_final": dgf[0],
              "b_forget": db[:, 0], "swa_sinks": dsk[:, 0]}
    return loss_acc[0, 0], dx, dw, dsmall


def kernel(x, p, g_mix, w_in, b_forget, swa_sinks, w_br_swa, w_br_fox, w_mix_out, g_mlp, w_ff1, w_ff2, g_ple, w_ple_gate, w_ple_proj, g_final, loss_target, m_g_mix, m_w_in, m_b_forget, m_swa_sinks, m_w_br_swa, m_w_br_fox, m_w_mix_out, m_g_mlp, m_w_ff1, m_w_ff2, m_g_ple, m_w_ple_gate, m_w_ple_proj, m_g_final, v_g_mix, v_w_in, v_b_forget, v_swa_sinks, v_w_br_swa, v_w_br_fox, v_w_mix_out, v_g_mlp, v_w_ff1, v_w_ff2, v_g_ple, v_w_ple_gate, v_w_ple_proj, v_g_final):
    given = dict(g_mix=g_mix, w_in=w_in, b_forget=b_forget, swa_sinks=swa_sinks, w_br_swa=w_br_swa, w_br_fox=w_br_fox,
                 w_mix_out=w_mix_out, g_mlp=g_mlp, w_ff1=w_ff1, w_ff2=w_ff2, g_ple=g_ple, w_ple_gate=w_ple_gate,
                 w_ple_proj=w_ple_proj, g_final=g_final)
    mom = dict(g_mix=m_g_mix, w_in=m_w_in, b_forget=m_b_forget, swa_sinks=m_swa_sinks, w_br_swa=m_w_br_swa,
               w_br_fox=m_w_br_fox, w_mix_out=m_w_mix_out, g_mlp=m_g_mlp, w_ff1=m_w_ff1, w_ff2=m_w_ff2, g_ple=m_g_ple,
               w_ple_gate=m_w_ple_gate, w_ple_proj=m_w_ple_proj, g_final=m_g_final)
    vel = dict(g_mix=v_g_mix, w_in=v_w_in, b_forget=v_b_forget, swa_sinks=v_swa_sinks, w_br_swa=v_w_br_swa,
               w_br_fox=v_w_br_fox, w_mix_out=v_w_mix_out, g_mlp=v_g_mlp, w_ff1=v_w_ff1, w_ff2=v_w_ff2, g_ple=v_g_ple,
               w_ple_gate=v_w_ple_gate, w_ple_proj=v_w_ple_proj, g_final=v_g_final)
    names = list(given)

    w_slab = _pack_shards({n: given[n] for n in SHARDED})
    full = _unpack_gathered(_all_gather(w_slab.astype(BF16)))
    small = {n: given[n].reshape(-1) for n in SMALL}

    n_tok = x.shape[1]
    tile = min(512, n_tok // 2)
    loss_part, dx, dw, dsmall = _local_step(x[0], p[0, 0], loss_target[0], full, small, tm=tile, tq=tile, ts=tile)
    loss = lax.psum(loss_part, AXES)

    parts, small_all = _grad_exchange(_pack_full(dw), _pack_small(dsmall))
    res = _adamw(parts, w_slab, _pack_shards({n: mom[n] for n in SHARDED}),
                 _pack_shards({n: vel[n] for n in SHARDED}), "adamw_shards", PACK_TILE_ROWS)
    res_s = _adamw(small_all, _pack_small(small), _pack_small({n: mom[n] for n in SMALL}),
                   _pack_small({n: vel[n] for n in SMALL}), "adamw_small", 8)

    outs = []
    for big, sml in zip(res, res_s):
        d = _unpack_shards(big, (1,))
        d.update(_unpack_small(sml, given))
        outs.append([d[n] for n in names])
    return (loss, dx[None], *outs[0], *outs[1], *outs[2], *outs[3])
```

```python
import numpy as np
import jax
import jax.numpy as jnp
from jax import lax
from jax.experimental import pallas as pl
from jax.experimental.pallas import tpu as pltpu

F32 = jnp.float32
BF16 = jnp.bfloat16

D_MODEL = 1024
HEAD_DIM = 64
SWA_HEADS = 8
FOX_HEADS = 8
CHUNK_SHIFT = 6
SWA_BLOCK = 128
WINDOW_CHUNKS = 2
D_FF = 4096
PLE_DIM = 256
RMS_EPS = 1e-6
N_MAIN = 2304
N_FPAD = 128
N_GATE = 2048
N_ALL = N_MAIN + N_FPAD + N_GATE
D_IN = N_MAIN + FOX_HEADS + N_GATE
SCALE = HEAD_DIM ** -0.5
NEG = -1e30

ADAM_LR = 0.001
ADAM_B1 = 0.9
ADAM_B2 = 0.999
ADAM_EPS = 1e-08
ADAM_WD = 0.01
ADAM_STEP = 10

N_DEV = 8
LANES = 128
V7X_VMEM_BYTES = 64 * 1024 * 1024
VMEM_LIMIT = V7X_VMEM_BYTES * 3 // 4
MESH = pl.DeviceIdType.MESH
AXES = ("x", "y", "c")

_NT = (((1,), (1,)), ((), ()))
_TN = (((0,), (0,)), ((), ()))


def _params(n_grid):
    return pltpu.CompilerParams(dimension_semantics=("arbitrary",) * n_grid, vmem_limit_bytes=VMEM_LIMIT)


def _chunks(n, step):
    return [(s, min(step, n - s)) for s in range(0, n, step)]


def _sigmoid(x):
    return 1.0 / (1.0 + jnp.exp(-x))


def _dot(a, b):
    return jnp.dot(a, b, preferred_element_type=F32)


def _dot_nt(a, b):
    return lax.dot_general(a, b, _NT, preferred_element_type=F32)


def _dot_tn(a, b):
    return lax.dot_general(a, b, _TN, preferred_element_type=F32)


def _rms(h):
    return lax.rsqrt(jnp.mean(h * h, axis=-1, keepdims=True) + RMS_EPS)


def _rms_bwd(h, g, du):
    rs = _rms(h)
    n = h * rs
    dn = du * g
    dh = rs * (dn - n * jnp.mean(dn * n, axis=-1, keepdims=True))
    return dh, jnp.sum(du * n, axis=0, keepdims=True)


def _acc_rows(ref, i, row):
    @pl.when(i == 0)
    def _():
        ref[...] = jnp.zeros_like(ref)
    ref[...] += jnp.broadcast_to(row, ref.shape)


def _row_call(body, name, n_rows, tm, row_ins, const_ins, row_outs, acc_outs):
    n_ri, n_ci, n_ro = len(row_ins), len(const_ins), len(row_outs)

    def kern(*refs):
        i = pl.program_id(0)
        body(i, refs[:n_ri], refs[n_ri:n_ri + n_ci], refs[n_ri + n_ci:n_ri + n_ci + n_ro],
             refs[n_ri + n_ci + n_ro:])

    def whole(a):
        zeros = (0,) * a.ndim
        return pl.BlockSpec(a.shape, lambda i: zeros, pipeline_mode=pl.Buffered(1))

    in_specs = [pl.BlockSpec((tm, a.shape[1]), lambda i: (i, 0)) for a in row_ins]
    in_specs += [whole(a) for a in const_ins]
    out_specs = [pl.BlockSpec((tm, c), lambda i: (i, 0)) for c, _ in row_outs]
    out_specs += [pl.BlockSpec((8, c), lambda i: (0, 0)) for c in acc_outs]
    out_shape = [jax.ShapeDtypeStruct((n_rows, c), dt) for c, dt in row_outs]
    out_shape += [jax.ShapeDtypeStruct((8, c), F32) for c in acc_outs]
    return pl.pallas_call(kern, grid=(n_rows // tm,), in_specs=in_specs, out_specs=out_specs,
                          out_shape=out_shape, name=name, compiler_params=_params(1))(*row_ins, *const_ins)


def _in_proj(x, g_mix, w_all, tm):
    def body(i, ins, consts, outs, accs):
        x_ref, = ins
        g_ref, w_ref = consts
        u_ref, zm_ref, zfg_ref = outs
        xv = x_ref[...]
        u = ((xv * _rms(xv)) * g_ref[...]).astype(BF16)
        u_ref[...] = u
        for s, n in _chunks(N_MAIN, 768):
            zm_ref[:, s:s + n] = _dot(u, w_ref[:, s:s + n]).astype(BF16)
        for s, n in _chunks(N_FPAD + N_GATE, 512):
            zfg_ref[:, s:s + n] = _dot(u, w_ref[:, N_MAIN + s:N_MAIN + s + n])

    return _row_call(body, "in_proj", x.shape[0], tm, [x], [g_mix, w_all],
                     [(D_MODEL, BF16), (N_MAIN, BF16), (N_FPAD + N_GATE, F32)], [])


def _mix_fwd(attn_a, attn_b, zfg, x, w_sa, w_fo, w_mo, g_mlp, tm):
    def body(i, ins, consts, outs, accs):
        aa_ref, ab_ref, zfg_ref, x_ref = ins
        wsa_ref, wfo_ref, wmo_ref, g_ref = consts
        ya_ref, yb_ref, mx_ref, h1_ref, u2_ref = outs
        ya = _dot(aa_ref[...], wsa_ref[...])
        yb = _dot(ab_ref[...], wfo_ref[...])
        g0 = _sigmoid(zfg_ref[:, N_FPAD:N_FPAD + D_MODEL])
        g1 = _sigmoid(zfg_ref[:, N_FPAD + D_MODEL:N_FPAD + 2 * D_MODEL])
        mixed = (g0 * ya + g1 * yb).astype(BF16)
        ya_ref[...] = ya.astype(BF16)
        yb_ref[...] = yb.astype(BF16)
        mx_ref[...] = mixed
        h1 = x_ref[...] + _dot(mixed, wmo_ref[...])
        h1_ref[...] = h1
        u2_ref[...] = ((h1 * _rms(h1)) * g_ref[...]).astype(BF16)

    return _row_call(body, "mix_fwd", x.shape[0], tm, [attn_a, attn_b, zfg, x], [w_sa, w_fo, w_mo, g_mlp],
                     [(D_MODEL, BF16), (D_MODEL, BF16), (D_MODEL, BF16), (D_MODEL, F32), (D_MODEL, BF16)], [])


def _ffn_fwd(u2, h1, w1s, w2s, tm):
    ch = D_FF // N_DEV

    def body(i, ins, consts, outs, accs):
        u_ref, h1_ref = ins
        w1_ref, w2_ref = consts
        a_ref, r_ref, h2_ref = outs
        u = u_ref[...]
        acc = h1_ref[...]
        for c in range(N_DEV):
            a = _dot(u, w1_ref[c])
            a_ref[:, c * ch:(c + 1) * ch] = a.astype(BF16)
            r = jnp.square(jnp.maximum(a, 0.0)).astype(BF16)
            r_ref[:, c * ch:(c + 1) * ch] = r
            acc = acc + _dot(r, w2_ref[c])
        h2_ref[...] = acc

    return _row_call(body, "ffn_fwd", u2.shape[0], tm, [u2, h1], [w1s, w2s],
                     [(D_FF, BF16), (D_FF, BF16), (D_MODEL, F32)], [])


def _head_fwd_bwd(h2, p, tgt, g_ple, w_pg, w_pp, g_fin, tm):
    def body(i, ins, consts, outs, accs):
        h2_ref, p_ref, t_ref = ins
        gp_ref, wpg_ref, wpp_ref, gf_ref = consts
        dh3_ref, dlg_ref, dpp_ref, u3_ref = outs
        loss_ref, dgf_ref = accs
        h2 = h2_ref[...]
        u3 = ((h2 * _rms(h2)) * gp_ref[...]).astype(BF16)
        u3_ref[...] = u3
        pg = _sigmoid(_dot(u3, wpg_ref[...]))
        pp = _dot(p_ref[...].astype(BF16), wpp_ref[...])
        h3 = h2 + pg * pp
        rs3 = _rms(h3)
        n3 = h3 * rs3
        gf = gf_ref[...]
        err = n3 * gf - t_ref[...]
        row_loss = 0.5 * jnp.mean(err * err, axis=-1, keepdims=True)
        _acc_rows(loss_ref, i, jnp.broadcast_to(jnp.sum(row_loss, axis=0, keepdims=True), (1, LANES)))
        dy = err * (1.0 / D_MODEL)
        _acc_rows(dgf_ref, i, jnp.sum(dy * n3, axis=0, keepdims=True))
        dn = dy * gf
        dh3 = rs3 * (dn - n3 * jnp.mean(dn * n3, axis=-1, keepdims=True))
        dh3_ref[...] = dh3
        dpp_ref[...] = (dh3 * pg).astype(BF16)
        dlg_ref[...] = ((dh3 * pp) * pg * (1.0 - pg)).astype(BF16)

    return _row_call(body, "head_fwd_bwd", h2.shape[0], tm, [h2, p, tgt], [g_ple, w_pg, w_pp, g_fin],
                     [(D_MODEL, F32), (D_MODEL, BF16), (D_MODEL, BF16), (D_MODEL, BF16)], [LANES, D_MODEL])


def _ffn_bwd_a(dlg, dh3, h2, a, w_pg, g_ple, w2s, tm):
    ch = D_FF // N_DEV

    def body(i, ins, consts, outs, accs):
        dlg_ref, dh3_ref, h2_ref, a_ref = ins
        wpg_ref, gp_ref, w2_ref = consts
        dh2_ref, dh2b_ref, da_ref = outs
        dgp_ref, = accs
        du3 = _dot_nt(dlg_ref[...], wpg_ref[...])
        dh, dg = _rms_bwd(h2_ref[...], gp_ref[...], du3)
        _acc_rows(dgp_ref, i, dg)
        dh2 = dh3_ref[...] + dh
        dh2_ref[...] = dh2
        dh2b = dh2.astype(BF16)
        dh2b_ref[...] = dh2b
        for c in range(N_DEV):
            dr = _dot_nt(dh2b, w2_ref[c])
            av = a_ref[:, c * ch:(c + 1) * ch].astype(F32)
            da_ref[:, c * ch:(c + 1) * ch] = (dr * (2.0 * jnp.maximum(av, 0.0))).astype(BF16)

    return _row_call(body, "ffn_bwd_a", h2.shape[0], tm, [dlg, dh3, h2, a], [w_pg, g_ple, w2s],
                     [(D_MODEL, F32), (D_MODEL, BF16), (D_FF, BF16)], [D_MODEL])


def _ffn_bwd_b(da, dh2, h1, ya, yb, zfg, w1s, g_mlp, w_mo, w_sa, w_fo, tm):
    ch = D_FF // N_DEV

    def body(i, ins, consts, outs, accs):
        da_ref, dh2_ref, h1_ref, ya_ref, yb_ref, zfg_ref = ins
        w1_ref, gm_ref, wmo_ref, wsa_ref, wfo_ref = consts
        dh1_ref, dh1b_ref, dgl_ref, dya_ref, dyb_ref, daa_ref, dab_ref = outs
        dgm_ref, = accs
        du2 = _dot_nt(da_ref[:, 0:ch], w1_ref[0])
        for c in range(1, N_DEV):
            du2 = du2 + _dot_nt(da_ref[:, c * ch:(c + 1) * ch], w1_ref[c])
        dh, dg = _rms_bwd(h1_ref[...], gm_ref[...], du2)
        _acc_rows(dgm_ref, i, dg)
        dh1 = dh2_ref[...] + dh
        dh1_ref[...] = dh1
        dh1b = dh1.astype(BF16)
        dh1b_ref[...] = dh1b
        dmx = _dot_nt(dh1b, wmo_ref[...])
        g0 = _sigmoid(zfg_ref[:, N_FPAD:N_FPAD + D_MODEL])
        g1 = _sigmoid(zfg_ref[:, N_FPAD + D_MODEL:N_FPAD + 2 * D_MODEL])
        dya = (dmx * g0).astype(BF16)
        dyb = (dmx * g1).astype(BF16)
        dya_ref[...] = dya
        dyb_ref[...] = dyb
        dgl_ref[:, 0:D_MODEL] = ((dmx * ya_ref[...].astype(F32)) * g0 * (1.0 - g0)).astype(BF16)
        dgl_ref[:, D_MODEL:2 * D_MODEL] = ((dmx * yb_ref[...].astype(F32)) * g1 * (1.0 - g1)).astype(BF16)
        daa_ref[...] = _dot_nt(dya, wsa_ref[...]).astype(BF16)
        dab_ref[...] = _dot_nt(dyb, wfo_ref[...]).astype(BF16)

    half = D_MODEL // 2
    return _row_call(body, "ffn_bwd_b", h1.shape[0], tm, [da, dh2, h1, ya, yb, zfg],
                     [w1s, g_mlp, w_mo, w_sa, w_fo],
                     [(D_MODEL, F32), (D_MODEL, BF16), (N_GATE, BF16), (D_MODEL, BF16), (D_MODEL, BF16),
                      (half, BF16), (half, BF16)], [D_MODEL])


def _in_proj_bwd(dz, dh1, x, w_all, g_mix, tm):
    def body(i, ins, consts, outs, accs):
        dz_ref, dh1_ref, x_ref = ins
        w_ref, g_ref = consts
        dx_ref, = outs
        dgx_ref, = accs
        du1 = _dot_nt(dz_ref[...], w_ref[...])
        dh, dg = _rms_bwd(x_ref[...], g_ref[...], du1)
        _acc_rows(dgx_ref, i, dg)
        dx_ref[...] = dh1_ref[...] + dh

    return _row_call(body, "in_proj_bwd", x.shape[0], tm, [dz, dh1, x], [w_all, g_mix],
                     [(D_MODEL, F32)], [D_MODEL])


def _matmul_tn(a, b, name, ts, stack_cols=0):
    n_rows, ka = a.shape
    n = b.shape[1]
    tk = min(ka, 1024)
    tn = stack_cols if stack_cols else (896 if n % 1024 else 1024)
    assert ka % tk == 0 and n % tn == 0 and n_rows % ts == 0 and (not stack_cols or tk == ka)

    def kern(a_ref, b_ref, o_ref):
        @pl.when(pl.program_id(2) == 0)
        def _():
            o_ref[...] = jnp.zeros_like(o_ref)
        o_ref[...] += _dot_tn(a_ref[...].astype(BF16), b_ref[...])

    if stack_cols:
        out_spec = pl.BlockSpec((None, tk, tn), lambda i, j, s: (j, 0, 0))
        out_shape = jax.ShapeDtypeStruct((n // tn, ka, tn), F32)
    else:
        out_spec = pl.BlockSpec((tk, tn), lambda i, j, s: (i, j))
        out_shape = jax.ShapeDtypeStruct((ka, n), F32)
    return pl.pallas_call(
        kern, grid=(ka // tk, n // tn, n_rows // ts),
        in_specs=[pl.BlockSpec((ts, tk), lambda i, j, s: (s, i)), pl.BlockSpec((ts, tn), lambda i, j, s: (s, j))],
        out_specs=out_spec, out_shape=out_shape, name=name, compiler_params=_params(3))(a, b)


SCAN_CHUNK = 512


def _decay_cumsum(f_t, b_col):
    n_tok = f_t.shape[1]
    ch = min(SCAN_CHUNK, n_tok)

    def kern(f_ref, b_ref, c_ref):
        r = lax.broadcasted_iota(jnp.int32, (ch, ch), 0)
        c = lax.broadcasted_iota(jnp.int32, (ch, ch), 1)
        tri = (r <= c).astype(F32)
        carry = jnp.zeros((8, 1), F32)
        for k in range(n_tok // ch):
            xv = f_ref[:, k * ch:(k + 1) * ch] + b_ref[...]
            lf = jnp.minimum(xv, 0.0) - jnp.log(1.0 + jnp.exp(-jnp.abs(xv)))
            cs = jnp.dot(lf, tri, precision=lax.Precision.HIGHEST, preferred_element_type=F32) + carry
            c_ref[:, k * ch:(k + 1) * ch] = cs
            carry = cs[:, ch - 1:ch]

    return pl.pallas_call(kern, out_shape=jax.ShapeDtypeStruct((8, n_tok), F32), name="decay_cumsum",
                          compiler_params=_params(0))(f_t, b_col)


def _decay_bwd(g_t, f_t, b_col):
    n_tok = f_t.shape[1]
    ch = min(SCAN_CHUNK, n_tok)

    def kern(g_ref, f_ref, b_ref, df_ref, db_ref):
        r = lax.broadcasted_iota(jnp.int32, (ch, ch), 0)
        c = lax.broadcasted_iota(jnp.int32, (ch, ch), 1)
        tri = (r >= c).astype(F32)
        carry = jnp.zeros((8, 1), F32)
        tot = jnp.zeros((8, 1), F32)
        for k in reversed(range(n_tok // ch)):
            gv = g_ref[:, k * ch:(k + 1) * ch]
            rc = jnp.dot(gv, tri, precision=lax.Precision.HIGHEST, preferred_element_type=F32) + carry
            carry = rc[:, 0:1]
            xv = f_ref[:, k * ch:(k + 1) * ch] + b_ref[...]
            df = -rc / (1.0 + jnp.exp(xv))
            df_ref[:, k * ch:(k + 1) * ch] = df
            tot = tot + jnp.sum(df, axis=1, keepdims=True)
        db_ref[...] = jnp.broadcast_to(tot, db_ref.shape)

    return pl.pallas_call(kern, out_shape=[jax.ShapeDtypeStruct((8, n_tok), F32),
                                           jax.ShapeDtypeStruct((8, LANES), F32)],
                          name="decay_bwd", compiler_params=_params(0))(g_t, f_t, b_col)


def _swa_band_mask(n):
    row = lax.broadcasted_iota(jnp.int32, (SWA_BLOCK, 2 * SWA_BLOCK), 0) + SWA_BLOCK
    col = lax.broadcasted_iota(jnp.int32, (SWA_BLOCK, 2 * SWA_BLOCK), 1)
    cd = (row >> CHUNK_SHIFT) - (col >> CHUNK_SHIFT)
    first_real = jnp.where(n > 0, 0, SWA_BLOCK)
    ok = (cd >= 0) & (cd <= WINDOW_CHUNKS) & (col >= first_real)
    dist = jnp.abs(row - col).astype(F32)
    return ok, dist


def _swap_halves(t):
    return pltpu.roll(t.astype(F32), HEAD_DIM, axis=1).astype(t.dtype)


def _swa_specs():
    blk = SWA_BLOCK
    q = pl.BlockSpec((blk, 4 * LANES), lambda n: (n, 0))
    kp = pl.BlockSpec((blk, LANES), lambda n: (jnp.maximum(n - 1, 0), 4))
    kc = pl.BlockSpec((blk, LANES), lambda n: (n, 4))
    vp = pl.BlockSpec((blk, LANES), lambda n: (jnp.maximum(n - 1, 0), 5))
    vc = pl.BlockSpec((blk, LANES), lambda n: (n, 5))
    return q, kp, kc, vp, vc


def _swa_fwd(zm, sinks):
    n_tok = zm.shape[0]
    blk = SWA_BLOCK

    def kern(q_ref, kp_ref, kc_ref, vp_ref, vc_ref, sink_ref, o_ref, lse_ref):
        n = pl.program_id(0)
        ok, dist = _swa_band_mask(n)
        k2 = jnp.concatenate([kp_ref[...], kc_ref[...]], axis=0)
        v2 = jnp.concatenate([vp_ref[...], vc_ref[...]], axis=0)
        ksw, vsw = _swap_halves(k2), _swap_halves(v2)
        lane = lax.broadcasted_iota(jnp.int32, (blk, LANES), 1)
        lo = lane < HEAD_DIM
        lse_t = jnp.zeros((blk, LANES), F32)
        for pair in range(SWA_HEADS // 2):
            q2 = q_ref[:, pair * LANES:(pair + 1) * LANES]
            kvh = pair // 2
            outs = []
            for a in range(2):
                h = 2 * pair + a
                qa = jnp.where(lo if a == 0 else ~lo, q2, jnp.zeros_like(q2)) * SCALE
                kx, vx = (k2, v2) if a == kvh else (ksw, vsw)
                s = _dot_nt(qa, kx)
                s = jnp.where(ok, s - (2.0 ** -(h + 1)) * dist, NEG)
                sink = sink_ref[h]
                m = jnp.maximum(jnp.max(s, axis=-1, keepdims=True), sink)
                e = jnp.exp(s - m)
                l = jnp.sum(e, axis=-1, keepdims=True) + jnp.exp(sink - m)
                pn = (e * (1.0 / l)).astype(BF16)
                outs.append(_dot(pn, vx))
                lse_t = jnp.where(lane == h, m + jnp.log(l), lse_t)
            o_ref[:, pair * LANES:(pair + 1) * LANES] = jnp.where(lo, outs[0], outs[1]).astype(BF16)
        lse_ref[...] = lse_t

    q, kp, kc, vp, vc = _swa_specs()
    return pl.pallas_call(
        kern, grid=(n_tok // blk,),
        in_specs=[q, kp, kc, vp, vc, pl.BlockSpec(memory_space=pltpu.SMEM)],
        out_specs=[pl.BlockSpec((blk, 4 * LANES), lambda n: (n, 0)), pl.BlockSpec((blk, LANES), lambda n: (n, 0))],
        out_shape=[jax.ShapeDtypeStruct((n_tok, 4 * LANES), BF16), jax.ShapeDtypeStruct((n_tok, LANES), F32)],
        name="swa_fwd", compiler_params=_params(1))(zm, zm, zm, zm, zm, sinks)


def _swa_bwd(zm, sinks, d_out, out, lse):
    n_tok = zm.shape[0]
    blk = SWA_BLOCK

    def kern(q_ref, kp_ref, kc_ref, vp_ref, vc_ref, do_ref, o_ref, lse_ref, sink_ref,
             dq_ref, dkp_ref, dkc_ref, dvp_ref, dvc_ref, dsk_ref):
        n = pl.program_id(0)

        @pl.when(n == 0)
        def _():
            dsk_ref[...] = jnp.zeros_like(dsk_ref)

        ok, dist = _swa_band_mask(n)
        k2 = jnp.concatenate([kp_ref[...], kc_ref[...]], axis=0)
        v2 = jnp.concatenate([vp_ref[...], vc_ref[...]], axis=0)
        ksw, vsw = _swap_halves(k2), _swap_halves(v2)
        lane = lax.broadcasted_iota(jnp.int32, (blk, LANES), 1)
        lo = lane < HEAD_DIM
        lse_t = lse_ref[...]
        zero = jnp.zeros((2 * blk, LANES), F32)
        dk_same, dk_swap, dv_same, dv_swap = zero, zero, zero, zero
        for pair in range(SWA_HEADS // 2):
            cols = slice(pair * LANES, (pair + 1) * LANES)
            q2, do2, o2 = q_ref[:, cols], do_ref[:, cols], o_ref[:, cols]
            kvh = pair // 2
            dqs = []
            for a in range(2):
                h = 2 * pair + a
                half = lo if a == 0 else ~lo
                qa = jnp.where(half, q2, jnp.zeros_like(q2)) * SCALE
                doa = jnp.where(half, do2, jnp.zeros_like(do2))
                kx, vx = (k2, v2) if a == kvh else (ksw, vsw)
                s = _dot_nt(qa, kx)
                s = jnp.where(ok, s - (2.0 ** -(h + 1)) * dist, NEG)
                lse_h = lse_t[:, h:h + 1]
                prob = jnp.exp(s - lse_h)
                dd = jnp.sum(doa.astype(F32) * o2.astype(F32), axis=-1, keepdims=True)
                dp = _dot_nt(doa, vx)
                ds = (prob * (dp - dd)).astype(BF16)
                p_sink = jnp.exp(sink_ref[h] - lse_h)
                dsk_ref[h:h + 1, :] += jnp.broadcast_to(-jnp.sum(p_sink * dd, axis=0, keepdims=True), (1, LANES))
                dqs.append(_dot(ds, kx) * SCALE)
                dk_c = _dot_tn(ds, qa)
                dv_c = _dot_tn(prob.astype(BF16), doa)
                if a == kvh:
                    dk_same, dv_same = dk_same + dk_c, dv_same + dv_c
                else:
                    dk_swap, dv_swap = dk_swap + dk_c, dv_swap + dv_c
            dq_ref[:, cols] = jnp.where(lo, dqs[0], dqs[1]).astype(BF16)
        dk = dk_same + pltpu.roll(dk_swap, HEAD_DIM, axis=1)
        dv = dv_same + pltpu.roll(dv_swap, HEAD_DIM, axis=1)
        dkp_ref[...] = dk[0:blk]
        dkc_ref[...] = dk[blk:2 * blk]
        dvp_ref[...] = dv[0:blk]
        dvc_ref[...] = dv[blk:2 * blk]

    q, kp, kc, vp, vc = _swa_specs()
    wide = pl.BlockSpec((blk, 4 * LANES), lambda n: (n, 0))
    narrow = pl.BlockSpec((blk, LANES), lambda n: (n, 0))
    part = jax.ShapeDtypeStruct((n_tok, LANES), F32)
    return pl.pallas_call(
        kern, grid=(n_tok // blk,),
        in_specs=[q, kp, kc, vp, vc, wide, wide, narrow, pl.BlockSpec(memory_space=pltpu.SMEM)],
        out_specs=[wide, narrow, narrow, narrow, narrow, pl.BlockSpec((8, LANES), lambda n: (0, 0))],
        out_shape=[jax.ShapeDtypeStruct((n_tok, 4 * LANES), BF16), part, part, part, part,
                   jax.ShapeDtypeStruct((8, LANES), F32)],
        name="swa_bwd", compiler_params=_params(1))(zm, zm, zm, zm, zm, d_out, out, lse, sinks)


Q_COL, K_COL, V_COL = 6, 10, 14


def _causal(t, tq, tk):
    row = lax.broadcasted_iota(jnp.int32, (tq, tk), 0)
    col = lax.broadcasted_iota(jnp.int32, (tq, tk), 1)
    return jnp.where(col <= row, t, NEG)


def _lane_tile(stat, width):
    return jnp.tile(stat, (1, width // LANES))


def _fox_fwd(zm, c_pairs, tq):
    n_tok = zm.shape[0]
    nq = n_tok // tq
    pairs = [(i, j) for i in range(nq) for j in range(i + 1)]
    ii = np.asarray([p[0] for p in pairs], np.int32)
    jj = np.asarray([p[1] for p in pairs], np.int32)

    def kern(ii_ref, jj_ref, q_ref, k_ref, v_ref, ck_ref, o_ref, ln_ref, qs_ref, m_ref, l_ref, acc_ref):
        step = pl.program_id(1)
        i, j = ii_ref[step], jj_ref[step]
        lane = lax.broadcasted_iota(jnp.int32, (tq, LANES), 1)
        lo = lane < HEAD_DIM

        @pl.when(j == 0)
        def _():
            q2 = q_ref[...]
            zq = jnp.zeros_like(q2)
            qs_ref[0] = jnp.where(lo, q2, zq) * SCALE
            qs_ref[1] = jnp.where(lo, zq, q2) * SCALE
            m_ref[...] = jnp.full(m_ref.shape, NEG, F32)
            l_ref[...] = jnp.zeros(l_ref.shape, F32)
            acc_ref[...] = jnp.zeros(acc_ref.shape, F32)

        def update(diag):
            kv = k_ref[...]
            v_ones = jnp.concatenate([v_ref[...], jnp.ones((tq, LANES), BF16)], axis=1)
            for a in range(2):
                t = _dot_nt(qs_ref[a], kv) - ck_ref[a:a + 1, :]
                if diag:
                    t = _causal(t, tq, tq)
                m_old = m_ref[a]
                m_new = jnp.maximum(m_old, jnp.max(t, axis=-1, keepdims=True))
                alpha = jnp.exp(m_old - m_new)
                e = jnp.exp(t - _lane_tile(m_new, tq)).astype(BF16)
                pv = _dot(e, v_ones)
                acc_ref[a] = alpha * acc_ref[a] + pv[:, :LANES]
                l_ref[a] = alpha * l_ref[a] + pv[:, LANES:]
                m_ref[a] = m_new

        @pl.when(j < i)
        def _():
            update(False)

        @pl.when(j == i)
        def _():
            update(True)
            o_ref[...] = jnp.where(lo, acc_ref[0] / l_ref[0], acc_ref[1] / l_ref[1]).astype(BF16)
            ln_ref[:, :LANES] = m_ref[0] + jnp.log(l_ref[0])
            ln_ref[:, LANES:] = m_ref[1] + jnp.log(l_ref[1])

    blk = (tq, LANES)
    grid_spec = pltpu.PrefetchScalarGridSpec(
        num_scalar_prefetch=2, grid=(FOX_HEADS // 2, len(pairs)),
        in_specs=[pl.BlockSpec(blk, lambda hp, s, ii, jj: (ii[s], Q_COL + hp)),
                  pl.BlockSpec(blk, lambda hp, s, ii, jj: (jj[s], K_COL + hp)),
                  pl.BlockSpec(blk, lambda hp, s, ii, jj: (jj[s], V_COL + hp)),
                  pl.BlockSpec((None, 2, tq), lambda hp, s, ii, jj: (hp, 0, jj[s]))],
        out_specs=[pl.BlockSpec(blk, lambda hp, s, ii, jj: (ii[s], hp)),
                   pl.BlockSpec((tq, 2 * LANES), lambda hp, s, ii, jj: (ii[s], hp))],
        scratch_shapes=[pltpu.VMEM((2, tq, LANES), BF16), pltpu.VMEM((2, tq, LANES), F32),
                        pltpu.VMEM((2, tq, LANES), F32), pltpu.VMEM((2, tq, LANES), F32)])
    return pl.pallas_call(
        kern, grid_spec=grid_spec,
        out_shape=[jax.ShapeDtypeStruct((n_tok, 4 * LANES), BF16),
                   jax.ShapeDtypeStruct((n_tok, FOX_HEADS * LANES), F32)],
        name="fox_fwd", compiler_params=_params(2))(ii, jj, zm, zm, zm, c_pairs)


def _fox_delta(d_out, out, tm):
    def body(i, ins, consts, outs, accs):
        do_ref, o_ref = ins
        dl_ref, = outs
        lane = lax.broadcasted_iota(jnp.int32, (tm, LANES), 1)
        lo = lane < HEAD_DIM
        for pair in range(FOX_HEADS // 2):
            cols = slice(pair * LANES, (pair + 1) * LANES)
            prod = do_ref[:, cols].astype(F32) * o_ref[:, cols].astype(F32)
            for a in range(2):
                dd = jnp.sum(jnp.where(lo if a == 0 else ~lo, prod, 0.0), axis=-1, keepdims=True)
                h = 2 * pair + a
                dl_ref[:, h * LANES:(h + 1) * LANES] = jnp.broadcast_to(dd, (tm, LANES))

    return _row_call(body, "fox_delta", d_out.shape[0], tm, [d_out, out], [], [(FOX_HEADS * LANES, F32)], [])[0]


def _fox_bwd(zm, c_pairs, d_out, lnorm, delta, tq):
    n_tok = zm.shape[0]
    nq = n_tok // tq
    pairs = [(i, j) for j in range(nq) for i in range(j, nq)]
    ii = np.asarray([p[0] for p in pairs], np.int32)
    jj = np.asarray([p[1] for p in pairs], np.int32)

    def kern(ii_ref, jj_ref, q_ref, k_ref, v_ref, ck_ref, do_ref, ln_ref, dl_ref,
             dq_ref, dk_ref, dv_ref, cs_ref, rs_ref, dk_acc, dv_acc):
        step = pl.program_id(1)
        i, j = ii_ref[step], jj_ref[step]
        lane = lax.broadcasted_iota(jnp.int32, (tq, LANES), 1)
        lo = lane < HEAD_DIM

        @pl.when(step == 0)
        def _():
            dq_ref[...] = jnp.zeros_like(dq_ref)
            rs_ref[...] = jnp.zeros_like(rs_ref)

        @pl.when(i == j)
        def _():
            dk_acc[...] = jnp.zeros(dk_acc.shape, F32)
            dv_acc[...] = jnp.zeros(dv_acc.shape, F32)

        def update(diag):
            q2, do2 = q_ref[...], do_ref[...]
            kv, vv = k_ref[...], v_ref[...]
            zq = jnp.zeros_like(q2)
            ones = jnp.ones((tq, LANES), BF16)
            k_ones = jnp.concatenate([kv, ones], axis=1)
            dq_rs = []
            for a in range(2):
                half = lo if a == 0 else ~lo
                qa = jnp.where(half, q2, zq) * SCALE
                doa = jnp.where(half, do2, zq)
                t = _dot_nt(qa, kv) - ck_ref[a:a + 1, :]
                if diag:
                    t = _causal(t, tq, tq)
                prob = jnp.exp(t - _lane_tile(ln_ref[:, a * LANES:(a + 1) * LANES], tq))
                dp = _dot_nt(doa, vv)
                ds = (prob * (dp - _lane_tile(dl_ref[:, a * LANES:(a + 1) * LANES], tq))).astype(BF16)
                dq_rs.append(_dot(ds, k_ones))
                dk_acc[a] += _dot_tn(ds, jnp.concatenate([qa, ones], axis=1))
                dv_acc[a] += _dot_tn(prob.astype(BF16), doa)
            rows = pl.ds(pl.multiple_of(i * tq, tq), tq)
            dq_ref[rows, :] += jnp.where(lo, dq_rs[0][:, :LANES], dq_rs[1][:, :LANES]) * SCALE
            rs_ref[rows, :] += jnp.where(lo, dq_rs[0][:, LANES:], dq_rs[1][:, LANES:])

        @pl.when(i == j)
        def _():
            update(True)

        @pl.when(i > j)
        def _():
            update(False)

        @pl.when(i == nq - 1)
        def _():
            dk_ref[...] = (dk_acc[0, :, :LANES] + dk_acc[1, :, :LANES]).astype(BF16)
            cs_ref[...] = jnp.where(lo, dk_acc[0, :, LANES:], dk_acc[1, :, LANES:])
            dv_ref[...] = (dv_acc[0] + dv_acc[1]).astype(BF16)

    blk = (tq, LANES)
    by_i = lambda col: (lambda hp, s, ii, jj: (ii[s], col + hp))
    by_j = lambda col: (lambda hp, s, ii, jj: (jj[s], col + hp))
    stat = pl.BlockSpec((tq, 2 * LANES), by_i(0))
    whole = pl.BlockSpec((n_tok, LANES), lambda hp, s, ii, jj: (0, hp))
    grid_spec = pltpu.PrefetchScalarGridSpec(
        num_scalar_prefetch=2, grid=(FOX_HEADS // 2, len(pairs)),
        in_specs=[pl.BlockSpec(blk, by_i(Q_COL)), pl.BlockSpec(blk, by_j(K_COL)), pl.BlockSpec(blk, by_j(V_COL)),
                  pl.BlockSpec((None, 2, tq), lambda hp, s, ii, jj: (hp, 0, jj[s])),
                  pl.BlockSpec(blk, by_i(0)), stat, stat],
        out_specs=[whole, pl.BlockSpec(blk, by_j(0)), pl.BlockSpec(blk, by_j(0)), pl.BlockSpec(blk, by_j(0)), whole],
        scratch_shapes=[pltpu.VMEM((2, tq, 2 * LANES), F32), pltpu.VMEM((2, tq, LANES), F32)])
    wide = lambda dt: jax.ShapeDtypeStruct((n_tok, 4 * LANES), dt)
    return pl.pallas_call(
        kern, grid_spec=grid_spec, out_shape=[wide(F32), wide(BF16), wide(BF16), wide(F32), wide(F32)],
        name="fox_bwd", compiler_params=_params(2))(ii, jj, zm, zm, zm, c_pairs, d_out, lnorm, delta)


def _my_pos():
    return lax.axis_index("x"), lax.axis_index("y"), lax.axis_index("c")


def _all_gather(shards):
    n_w = len(shards)

    def kern(*refs):
        x_refs, out_refs = refs[:n_w], refs[n_w:2 * n_w]
        send_sems, recv_sems, local_sems = refs[2 * n_w:]
        x, y, c = _my_pos()
        me, sibling = (x, y, c), (x, y, 1 - c)
        chips = [(1 - x, y), (x, 1 - y), (1 - x, 1 - y)]

        def slot(w, px, py, pc):
            return out_refs[w].at[4 * px + 2 * py + pc]

        def copy(w, k, block, to, src=None):
            return pltpu.make_async_remote_copy(
                src_ref=slot(w, *block) if src is None else src, dst_ref=slot(w, *block),
                send_sem=send_sems.at[7 * w + k], recv_sem=recv_sems.at[7 * w + k], device_id=to, device_id_type=MESH)

        local, started = [], []
        for w in range(n_w):
            mine = pltpu.make_async_copy(x_refs[w], slot(w, *me), local_sems.at[w])
            mine.start()
            local.append(mine)
            first = [copy(w, 0, me, sibling, src=x_refs[w])]
            first += [copy(w, 1 + k, me, (*chip, c), src=x_refs[w]) for k, chip in enumerate(chips)]
            for cp in first:
                cp.start()
            started += first
        for k, chip in enumerate(chips):
            for w in range(n_w):
                copy(w, 1 + k, (*chip, c), me).wait_recv()
                passed = copy(w, 4 + k, (*chip, c), sibling)
                passed.start()
                started.append(passed)
        for w in range(n_w):
            copy(w, 0, sibling, me).wait_recv()
            for k, chip in enumerate(chips):
                copy(w, 4 + k, (*chip, 1 - c), me).wait_recv()
        for cp in started:
            cp.wait_send()
        for cp in local:
            cp.wait()

    any_spec = pl.BlockSpec(memory_space=pl.ANY)
    return pl.pallas_call(
        kern, out_shape=[jax.ShapeDtypeStruct((N_DEV,) + s.shape, s.dtype) for s in shards],
        in_specs=[any_spec] * n_w, out_specs=[any_spec] * n_w,
        scratch_shapes=[pltpu.SemaphoreType.DMA((7 * n_w,)), pltpu.SemaphoreType.DMA((7 * n_w,)),
                        pltpu.SemaphoreType.DMA((n_w,))],
        name="weight_all_gather")(*shards)


def _grad_exchange(grads, small):
    n_w = len(grads)

    def kern(*refs):
        g_refs, s_ref = refs[:n_w], refs[n_w]
        part_refs, sall_ref = refs[n_w + 1:2 * n_w + 1], refs[2 * n_w + 1]
        send_sems, recv_sems, local_sems = refs[2 * n_w + 2:]
        x, y, c = _my_pos()
        my_id = 4 * x + 2 * y + c
        n_sem = n_w + 1
        local = [pltpu.make_async_copy(g_refs[w].at[my_id], part_refs[w].at[0], local_sems.at[w]) for w in range(n_w)]
        local.append(pltpu.make_async_copy(s_ref, sall_ref.at[my_id], local_sems.at[n_w]))
        for cp in local:
            cp.start()
        sends = []
        for k in range(1, N_DEV):
            px, py, pc = x ^ (k >> 2), y ^ ((k >> 1) & 1), c ^ (k & 1)
            peer_id = 4 * px + 2 * py + pc
            sml = pltpu.make_async_remote_copy(
                src_ref=s_ref, dst_ref=sall_ref.at[my_id], send_sem=send_sems.at[n_sem * k + n_w],
                recv_sem=recv_sems.at[n_sem * k + n_w], device_id=(px, py, pc), device_id_type=MESH)
            sml.start()
            sends.append(sml)
            for w in range(n_w):
                big = pltpu.make_async_remote_copy(
                    src_ref=g_refs[w].at[peer_id], dst_ref=part_refs[w].at[k], send_sem=send_sems.at[n_sem * k + w],
                    recv_sem=recv_sems.at[n_sem * k + w], device_id=(px, py, pc), device_id_type=MESH)
                big.start()
                sends.append(big)
        for k in range(1, N_DEV):
            px, py, pc = x ^ (k >> 2), y ^ ((k >> 1) & 1), c ^ (k & 1)
            peer_id = 4 * px + 2 * py + pc
            pltpu.make_async_remote_copy(
                src_ref=s_ref, dst_ref=sall_ref.at[peer_id], send_sem=send_sems.at[n_sem * k + n_w],
                recv_sem=recv_sems.at[n_sem * k + n_w], device_id=(x, y, c), device_id_type=MESH).wait_recv()
            for w in range(n_w):
                pltpu.make_async_remote_copy(
                    src_ref=g_refs[w].at[my_id], dst_ref=part_refs[w].at[k], send_sem=send_sems.at[n_sem * k + w],
                    recv_sem=recv_sems.at[n_sem * k + w], device_id=(x, y, c), device_id_type=MESH).wait_recv()
        for cp in sends:
            cp.wait_send()
        for cp in local:
            cp.wait()

    any_spec = pl.BlockSpec(memory_space=pl.ANY)
    n_sems = (n_w + 1) * N_DEV
    return pl.pallas_call(
        kern, out_shape=[jax.ShapeDtypeStruct(g.shape, g.dtype) for g in grads]
        + [jax.ShapeDtypeStruct((N_DEV,) + small.shape, small.dtype)],
        in_specs=[any_spec] * (n_w + 1), out_specs=[any_spec] * (n_w + 1),
        scratch_shapes=[pltpu.SemaphoreType.DMA((n_sems,)), pltpu.SemaphoreType.DMA((n_sems,)),
                        pltpu.SemaphoreType.DMA((n_w + 1,))],
        name="grad_exchange")(*grads, small)


ADAMW_BLOCK_BYTES = 4 * 1024 * 1024


def _adamw(parts, w, m, v, name):
    n_parts, n_rows, n_cols = parts.shape
    limit = max(8, ADAMW_BLOCK_BYTES // (n_parts * n_cols * 4))
    tr = max(t for t in range(8, n_rows + 1, 8) if n_rows % t == 0 and t <= limit)

    def kern(p_ref, w_ref, m_ref, v_ref, g_out, d_out, m_out, v_out):
        g = p_ref[0]
        for k in range(1, n_parts):
            g = g + p_ref[k]
        m_new = ADAM_B1 * m_ref[...] + (1.0 - ADAM_B1) * g
        v_new = ADAM_B2 * v_ref[...] + (1.0 - ADAM_B2) * jnp.square(g)
        m_hat = m_new / (1.0 - ADAM_B1 ** ADAM_STEP)
        v_hat = v_new / (1.0 - ADAM_B2 ** ADAM_STEP)
        g_out[...] = g
        d_out[...] = -ADAM_LR * (m_hat / (jnp.sqrt(v_hat) + ADAM_EPS) + ADAM_WD * w_ref[...])
        m_out[...] = m_new
        v_out[...] = v_new

    row = pl.BlockSpec((tr, n_cols), lambda i: (i, 0))
    out = jax.ShapeDtypeStruct((n_rows, n_cols), F32)
    return pl.pallas_call(
        kern, grid=(n_rows // tr,),
        in_specs=[pl.BlockSpec((n_parts, tr, n_cols), lambda i: (0, i, 0)), row, row, row],
        out_specs=[row, row, row, row], out_shape=[out, out, out, out], name=name,
        compiler_params=_params(1))(parts, w, m, v)


SHARDED = {
    "w_in": ((D_MODEL, D_IN), 1), "w_br_swa": ((512, D_MODEL), 1), "w_br_fox": ((512, D_MODEL), 1),
    "w_mix_out": ((D_MODEL, D_MODEL), 0), "w_ff1": ((D_MODEL, D_FF), 1), "w_ff2": ((D_FF, D_MODEL), 0),
    "w_ple_gate": ((D_MODEL, D_MODEL), 0), "w_ple_proj": ((PLE_DIM, D_MODEL), 1),
}
W_IN_SHARD = D_IN // N_DEV
W_IN_PAD = 640
SMALL = ("g_mix", "g_mlp", "g_ple", "g_final", "b_forget", "swa_sinks")
SMALL_COLS = 1024


def _wire_shard(name, a):
    a = a.reshape(a.shape[-2:])
    return jnp.pad(a, ((0, 0), (0, W_IN_PAD - W_IN_SHARD))) if name == "w_in" else a


def _from_wire(name, a):
    return (a[:, :W_IN_SHARD] if name == "w_in" else a)[None]


def _cols_to_full(stacked):
    return jnp.concatenate([stacked[d] for d in range(N_DEV)], axis=1)


def _full_to_cols(full):
    n = full.shape[1] // N_DEV
    return jnp.stack([full[:, d * n:(d + 1) * n] for d in range(N_DEV)])


def _w_all_from_wire(stacked):
    w_in = jnp.concatenate([stacked[d][:, :W_IN_SHARD] for d in range(N_DEV)], axis=1)
    fpad = jnp.zeros((D_MODEL, N_FPAD - FOX_HEADS), stacked.dtype)
    return jnp.concatenate([w_in[:, :N_MAIN + FOX_HEADS], fpad, w_in[:, N_MAIN + FOX_HEADS:]], axis=1)


def _dw_in_to_wire(dw_all):
    dw_in = jnp.concatenate([dw_all[:, :N_MAIN + FOX_HEADS], dw_all[:, N_MAIN + N_FPAD:]], axis=1)
    pad = jnp.zeros((D_MODEL, W_IN_PAD - W_IN_SHARD), dw_all.dtype)
    return jnp.stack([jnp.concatenate([dw_in[:, d * W_IN_SHARD:(d + 1) * W_IN_SHARD], pad], axis=1)
                      for d in range(N_DEV)])


def _pack_small(vals):
    rows = [jnp.pad(vals[n].reshape(-1), (0, SMALL_COLS - vals[n].size)) for n in SMALL]
    rows += [jnp.zeros((SMALL_COLS,), F32)] * (8 - len(SMALL))
    return jnp.stack(rows)


def _unpack_small(slab, like):
    return {n: slab[r, :like[n].size].reshape(like[n].shape) for r, n in enumerate(SMALL)}


def _local_step(x, p, tgt, w, small, tm, tq, ts):
    n_tok = x.shape[0]
    row = lambda v: v.reshape(1, -1)
    g_mix, g_mlp, g_ple, g_fin = row(small["g_mix"]), row(small["g_mlp"]), row(small["g_ple"]), row(small["g_final"])
    sinks = small["swa_sinks"].reshape(-1)
    b_col = small["b_forget"].reshape(FOX_HEADS, 1)

    u1, zm, zfg = _in_proj(x, g_mix, w["w_all"], tm)
    f_t = zfg[:, :FOX_HEADS].T
    c_pairs = _decay_cumsum(f_t, b_col).reshape(FOX_HEADS // 2, 2, n_tok)
    attn_a, lse_a = _swa_fwd(zm, sinks)
    attn_b, ln_b = _fox_fwd(zm, c_pairs, tq)
    ya, yb, mixed, h1, u2 = _mix_fwd(attn_a, attn_b, zfg, x, w["w_br_swa"], w["w_br_fox"], w["w_mix_out"], g_mlp, tm)
    a, r, h2 = _ffn_fwd(u2, h1, w["w_ff1"], w["w_ff2"], tm // 2)
    dh3, dlg, dpp, u3, loss_acc, dgf = _head_fwd_bwd(h2, p, tgt, g_ple, w["w_ple_gate"], w["w_ple_proj"], g_fin, tm)

    dh2, dh2b, da, dgp = _ffn_bwd_a(dlg, dh3, h2, a, w["w_ple_gate"], g_ple, w["w_ff2"], tm // 2)
    dh1, dh1b, dgl, dya, dyb, daa, dab, dgm = _ffn_bwd_b(
        da, dh2, h1, ya, yb, zfg, w["w_ff1"], g_mlp, w["w_mix_out"], w["w_br_swa"], w["w_br_fox"], tm // 2)
    dq_a, dkp, dkc, dvp, dvc, dsk = _swa_bwd(zm, sinks, daa, attn_a, lse_a)
    delta_b = _fox_delta(dab, attn_b, tm)
    dq_b, dk_b, dv_b, cs, rs = _fox_bwd(zm, c_pairs, dab, ln_b, delta_b, tq)

    up = lambda t: jnp.concatenate([t[SWA_BLOCK:], jnp.zeros((SWA_BLOCK, LANES), F32)], axis=0)
    dk_a, dv_a = dkc + up(dkp), dvc + up(dvp)
    per_head = lambda t: t.reshape(n_tok, FOX_HEADS, HEAD_DIM)[:, :, 0].T
    df_t, db = _decay_bwd(per_head(cs) - per_head(rs), f_t, b_col)
    df = jnp.pad(df_t.T, ((0, 0), (0, N_FPAD - FOX_HEADS)))
    dz = jnp.concatenate([dq_a, dk_a.astype(BF16), dv_a.astype(BF16), dq_b.astype(BF16), dk_b, dv_b,
                          df.astype(BF16), dgl], axis=1)
    dx, dgx = _in_proj_bwd(dz, dh1, x, w["w_all"], g_mix, tm)

    dw = {
        "w_all": _matmul_tn(u1, dz, "dw_in", ts),
        "w_br_swa": _matmul_tn(attn_a, dya, "dw_br_swa", ts),
        "w_br_fox": _matmul_tn(attn_b, dyb, "dw_br_fox", ts),
        "w_mix_out": _matmul_tn(mixed, dh1b, "dw_mix_out", ts),
        "w_ff1": _matmul_tn(u2, da, "dw_ff1", ts, stack_cols=D_FF // N_DEV),
        "w_ff2": _matmul_tn(r, dh2b, "dw_ff2", ts),
        "w_ple_gate": _matmul_tn(u3, dlg, "dw_ple_gate", ts),
        "w_ple_proj": _matmul_tn(p, dpp, "dw_ple_proj", ts),
    }
    dsmall = {"g_mix": dgx[0], "g_mlp": dgm[0], "g_ple": dgp[0], "g_final": dgf[0],
              "b_forget": db[:, 0], "swa_sinks": dsk[:, 0]}
    return loss_acc[0, 0], dx, dw, dsmall


def _gathered_to_local(g):
    rows = lambda t: t.reshape(-1, t.shape[-1])
    return {"w_all": _w_all_from_wire(g["w_in"]), "w_br_swa": _cols_to_full(g["w_br_swa"]),
            "w_br_fox": _cols_to_full(g["w_br_fox"]), "w_mix_out": rows(g["w_mix_out"]), "w_ff1": g["w_ff1"],
            "w_ff2": g["w_ff2"], "w_ple_gate": rows(g["w_ple_gate"]), "w_ple_proj": _cols_to_full(g["w_ple_proj"])}


def _local_to_wire(dw):
    by_rows = lambda t: t.reshape(N_DEV, t.shape[0] // N_DEV, t.shape[1])
    return {"w_in": _dw_in_to_wire(dw["w_all"]), "w_br_swa": _full_to_cols(dw["w_br_swa"]),
            "w_br_fox": _full_to_cols(dw["w_br_fox"]), "w_mix_out": by_rows(dw["w_mix_out"]), "w_ff1": dw["w_ff1"],
            "w_ff2": by_rows(dw["w_ff2"]), "w_ple_gate": by_rows(dw["w_ple_gate"]),
            "w_ple_proj": _full_to_cols(dw["w_ple_proj"])}


def kernel(x, p, g_mix, w_in, b_forget, swa_sinks, w_br_swa, w_br_fox, w_mix_out, g_mlp, w_ff1, w_ff2, g_ple, w_ple_gate, w_ple_proj, g_final, loss_target, m_g_mix, m_w_in, m_b_forget, m_swa_sinks, m_w_br_swa, m_w_br_fox, m_w_mix_out, m_g_mlp, m_w_ff1, m_w_ff2, m_g_ple, m_w_ple_gate, m_w_ple_proj, m_g_final, v_g_mix, v_w_in, v_b_forget, v_swa_sinks, v_w_br_swa, v_w_br_fox, v_w_mix_out, v_g_mlp, v_w_ff1, v_w_ff2, v_g_ple, v_w_ple_gate, v_w_ple_proj, v_g_final):
    given = dict(g_mix=g_mix, w_in=w_in, b_forget=b_forget, swa_sinks=swa_sinks, w_br_swa=w_br_swa, w_br_fox=w_br_fox,
                 w_mix_out=w_mix_out, g_mlp=g_mlp, w_ff1=w_ff1, w_ff2=w_ff2, g_ple=g_ple, w_ple_gate=w_ple_gate,
                 w_ple_proj=w_ple_proj, g_final=g_final)
    mom = dict(g_mix=m_g_mix, w_in=m_w_in, b_forget=m_b_forget, swa_sinks=m_swa_sinks, w_br_swa=m_w_br_swa,
               w_br_fox=m_w_br_fox, w_mix_out=m_w_mix_out, g_mlp=m_g_mlp, w_ff1=m_w_ff1, w_ff2=m_w_ff2, g_ple=m_g_ple,
               w_ple_gate=m_w_ple_gate, w_ple_proj=m_w_ple_proj, g_final=m_g_final)
    vel = dict(g_mix=v_g_mix, w_in=v_w_in, b_forget=v_b_forget, swa_sinks=v_swa_sinks, w_br_swa=v_w_br_swa,
               w_br_fox=v_w_br_fox, w_mix_out=v_w_mix_out, g_mlp=v_g_mlp, w_ff1=v_w_ff1, w_ff2=v_w_ff2, g_ple=v_g_ple,
               w_ple_gate=v_w_ple_gate, w_ple_proj=v_w_ple_proj, g_final=v_g_final)
    names = list(given)
    sharded = list(SHARDED)

    w_wire = {n: _wire_shard(n, given[n]) for n in sharded}
    gathered = _all_gather([w_wire[n].astype(BF16) for n in sharded])
    local_w = _gathered_to_local(dict(zip(sharded, gathered)))
    small = {n: given[n].reshape(-1) for n in SMALL}

    n_tok = x.shape[1]
    tile = min(512, n_tok // 2)
    loss_part, dx, dw, dsmall = _local_step(x[0], p[0, 0], loss_target[0], local_w, small, tm=tile, tq=tile, ts=tile)
    loss = lax.psum(loss_part, AXES)

    dw_wire = _local_to_wire(dw)
    *parts, small_all = _grad_exchange([dw_wire[n] for n in sharded], _pack_small(dsmall))

    res = {}
    for n, part in zip(sharded, parts):
        flat = part.reshape(N_DEV, -1, part.shape[-1])
        outs = _adamw(flat, w_wire[n], _wire_shard(n, mom[n]), _wire_shard(n, vel[n]), "adamw_" + n)
        res[n] = [_from_wire(n, o) for o in outs]
    outs_s = _adamw(small_all, _pack_small(small), _pack_small({n: mom[n] for n in SMALL}),
                    _pack_small({n: vel[n] for n in SMALL}), "adamw_small")
    small_res = [_unpack_small(o, given) for o in outs_s]

    groups = [[res[n][k] if n in res else small_res[k][n] for n in names] for k in range(4)]
    return (loss, dx[None], *groups[0], *groups[1], *groups[2], *groups[3])
```

```python
import numpy as np
import jax
import jax.numpy as jnp
from jax import lax
from jax.experimental import pallas as pl
from jax.experimental.pallas import tpu as pltpu

F32 = jnp.float32
BF16 = jnp.bfloat16

D_MODEL = 1024
HEAD_DIM = 64
SWA_HEADS = 8
FOX_HEADS = 8
CHUNK_SHIFT = 6
SWA_BLOCK = 128
WINDOW_CHUNKS = 2
D_FF = 4096
PLE_DIM = 256
RMS_EPS = 1e-6
N_MAIN = 2304
N_FPAD = 128
N_GATE = 2048
N_ALL = N_MAIN + N_FPAD + N_GATE
D_IN = N_MAIN + FOX_HEADS + N_GATE
SCALE = HEAD_DIM ** -0.5
NEG = -1e30

ADAM_LR = 0.001
ADAM_B1 = 0.9
ADAM_B2 = 0.999
ADAM_EPS = 1e-08
ADAM_WD = 0.01
ADAM_STEP = 10

N_DEV = 8
LANES = 128
V7X_VMEM_BYTES = 64 * 1024 * 1024
VMEM_LIMIT = V7X_VMEM_BYTES * 3 // 4
MESH = pl.DeviceIdType.MESH
AXES = ("x", "y", "c")

_NT = (((1,), (1,)), ((), ()))
_TN = (((0,), (0,)), ((), ()))


def _params(n_grid):
    return pltpu.CompilerParams(dimension_semantics=("arbitrary",) * n_grid, vmem_limit_bytes=VMEM_LIMIT)


def _chunks(n, step):
    return [(s, min(step, n - s)) for s in range(0, n, step)]


def _sigmoid(x):
    return 1.0 / (1.0 + jnp.exp(-x))


def _dot(a, b):
    return jnp.dot(a, b, preferred_element_type=F32)


def _dot_nt(a, b):
    return lax.dot_general(a, b, _NT, preferred_element_type=F32)


def _dot_tn(a, b):
    return lax.dot_general(a, b, _TN, preferred_element_type=F32)


def _rms(h):
    return lax.rsqrt(jnp.mean(h * h, axis=-1, keepdims=True) + RMS_EPS)


def _rms_bwd(h, g, du):
    rs = _rms(h)
    n = h * rs
    dn = du * g
    dh = rs * (dn - n * jnp.mean(dn * n, axis=-1, keepdims=True))
    return dh, jnp.sum(du * n, axis=0, keepdims=True)


def _acc_rows(ref, i, row):
    @pl.when(i == 0)
    def _():
        ref[...] = jnp.zeros_like(ref)
    ref[...] += jnp.broadcast_to(row, ref.shape)


def _row_call(body, name, n_rows, tm, row_ins, const_ins, row_outs, acc_outs):
    n_ri, n_ci, n_ro = len(row_ins), len(const_ins), len(row_outs)

    def kern(*refs):
        i = pl.program_id(0)
        body(i, refs[:n_ri], refs[n_ri:n_ri + n_ci], refs[n_ri + n_ci:n_ri + n_ci + n_ro],
             refs[n_ri + n_ci + n_ro:])

    def whole(a):
        zeros = (0,) * a.ndim
        return pl.BlockSpec(a.shape, lambda i: zeros, pipeline_mode=pl.Buffered(1))

    in_specs = [pl.BlockSpec((tm, a.shape[1]), lambda i: (i, 0)) for a in row_ins]
    in_specs += [whole(a) for a in const_ins]
    out_specs = [pl.BlockSpec((tm, c), lambda i: (i, 0)) for c, _ in row_outs]
    out_specs += [pl.BlockSpec((8, c), lambda i: (0, 0)) for c in acc_outs]
    out_shape = [jax.ShapeDtypeStruct((n_rows, c), dt) for c, dt in row_outs]
    out_shape += [jax.ShapeDtypeStruct((8, c), F32) for c in acc_outs]
    return pl.pallas_call(kern, grid=(n_rows // tm,), in_specs=in_specs, out_specs=out_specs,
                          out_shape=out_shape, name=name, compiler_params=_params(1))(*row_ins, *const_ins)


def _in_proj(x, g_mix, w_all, tm):
    def body(i, ins, consts, outs, accs):
        x_ref, = ins
        g_ref, w_ref = consts
        u_ref, zm_ref, zfg_ref = outs
        xv = x_ref[...]
        u = ((xv * _rms(xv)) * g_ref[...]).astype(BF16)
        u_ref[...] = u
        for s, n in _chunks(N_MAIN, 768):
            zm_ref[:, s:s + n] = _dot(u, w_ref[:, s:s + n]).astype(BF16)
        for s, n in _chunks(N_FPAD + N_GATE, 512):
            zfg_ref[:, s:s + n] = _dot(u, w_ref[:, N_MAIN + s:N_MAIN + s + n])

    return _row_call(body, "in_proj", x.shape[0], tm, [x], [g_mix, w_all],
                     [(D_MODEL, BF16), (N_MAIN, BF16), (N_FPAD + N_GATE, F32)], [])


def _mix_fwd(attn_a, attn_b, zfg, x, w_sa, w_fo, w_mo, g_mlp, tm):
    def body(i, ins, consts, outs, accs):
        aa_ref, ab_ref, zfg_ref, x_ref = ins
        wsa_ref, wfo_ref, wmo_ref, g_ref = consts
        ya_ref, yb_ref, mx_ref, h1_ref, u2_ref = outs
        ya = _dot(aa_ref[...], wsa_ref[...])
        yb = _dot(ab_ref[...], wfo_ref[...])
        g0 = _sigmoid(zfg_ref[:, N_FPAD:N_FPAD + D_MODEL])
        g1 = _sigmoid(zfg_ref[:, N_FPAD + D_MODEL:N_FPAD + 2 * D_MODEL])
        mixed = (g0 * ya + g1 * yb).astype(BF16)
        ya_ref[...] = ya.astype(BF16)
        yb_ref[...] = yb.astype(BF16)
        mx_ref[...] = mixed
        h1 = x_ref[...] + _dot(mixed, wmo_ref[...])
        h1_ref[...] = h1
        u2_ref[...] = ((h1 * _rms(h1)) * g_ref[...]).astype(BF16)

    return _row_call(body, "mix_fwd", x.shape[0], tm, [attn_a, attn_b, zfg, x], [w_sa, w_fo, w_mo, g_mlp],
                     [(D_MODEL, BF16), (D_MODEL, BF16), (D_MODEL, BF16), (D_MODEL, F32), (D_MODEL, BF16)], [])


def _ffn_fwd(u2, h1, w1s, w2s, tm):
    ch = D_FF // N_DEV

    def body(i, ins, consts, outs, accs):
        u_ref, h1_ref = ins
        w1_ref, w2_ref = consts
        a_ref, r_ref, h2_ref = outs
        u = u_ref[...]
        acc = h1_ref[...]
        for c in range(N_DEV):
            a = _dot(u, w1_ref[c])
            a_ref[:, c * ch:(c + 1) * ch] = a.astype(BF16)
            r = jnp.square(jnp.maximum(a, 0.0)).astype(BF16)
            r_ref[:, c * ch:(c + 1) * ch] = r
            acc = acc + _dot(r, w2_ref[c])
        h2_ref[...] = acc

    return _row_call(body, "ffn_fwd", u2.shape[0], tm, [u2, h1], [w1s, w2s],
                     [(D_FF, BF16), (D_FF, BF16), (D_MODEL, F32)], [])


def _head_fwd_bwd(h2, p, tgt, g_ple, w_pg, w_pp, g_fin, tm):
    def body(i, ins, consts, outs, accs):
        h2_ref, p_ref, t_ref = ins
        gp_ref, wpg_ref, wpp_ref, gf_ref = consts
        dh3_ref, dlg_ref, dpp_ref, u3_ref = outs
        loss_ref, dgf_ref = accs
        h2 = h2_ref[...]
        u3 = ((h2 * _rms(h2)) * gp_ref[...]).astype(BF16)
        u3_ref[...] = u3
        pg = _sigmoid(_dot(u3, wpg_ref[...]))
        pp = _dot(p_ref[...].astype(BF16), wpp_ref[...])
        h3 = h2 + pg * pp
        rs3 = _rms(h3)
        n3 = h3 * rs3
        gf = gf_ref[...]
        err = n3 * gf - t_ref[...]
        row_loss = 0.5 * jnp.mean(err * err, axis=-1, keepdims=True)
        _acc_rows(loss_ref, i, jnp.broadcast_to(jnp.sum(row_loss, axis=0, keepdims=True), (1, LANES)))
        dy = err * (1.0 / D_MODEL)
        _acc_rows(dgf_ref, i, jnp.sum(dy * n3, axis=0, keepdims=True))
        dn = dy * gf
        dh3 = rs3 * (dn - n3 * jnp.mean(dn * n3, axis=-1, keepdims=True))
        dh3_ref[...] = dh3
        dpp_ref[...] = (dh3 * pg).astype(BF16)
        dlg_ref[...] = ((dh3 * pp) * pg * (1.0 - pg)).astype(BF16)

    return _row_call(body, "head_fwd_bwd", h2.shape[0], tm, [h2, p, tgt], [g_ple, w_pg, w_pp, g_fin],
                     [(D_MODEL, F32), (D_MODEL, BF16), (D_MODEL, BF16), (D_MODEL, BF16)], [LANES, D_MODEL])


def _ffn_bwd_a(dlg, dh3, h2, a, w_pg, g_ple, w2s, tm):
    ch = D_FF // N_DEV

    def body(i, ins, consts, outs, accs):
        dlg_ref, dh3_ref, h2_ref, a_ref = ins
        wpg_ref, gp_ref, w2_ref = consts
        dh2_ref, dh2b_ref, da_ref = outs
        dgp_ref, = accs
        du3 = _dot_nt(dlg_ref[...], wpg_ref[...])
        dh, dg = _rms_bwd(h2_ref[...], gp_ref[...], du3)
        _acc_rows(dgp_ref, i, dg)
        dh2 = dh3_ref[...] + dh
        dh2_ref[...] = dh2
        dh2b = dh2.astype(BF16)
        dh2b_ref[...] = dh2b
        for c in range(N_DEV):
            dr = _dot_nt(dh2b, w2_ref[c])
            av = a_ref[:, c * ch:(c + 1) * ch].astype(F32)
            da_ref[:, c * ch:(c + 1) * ch] = (dr * (2.0 * jnp.maximum(av, 0.0))).astype(BF16)

    return _row_call(body, "ffn_bwd_a", h2.shape[0], tm, [dlg, dh3, h2, a], [w_pg, g_ple, w2s],
                     [(D_MODEL, F32), (D_MODEL, BF16), (D_FF, BF16)], [D_MODEL])


def _ffn_bwd_b(da, dh2, h1, ya, yb, zfg, w1s, g_mlp, w_mo, w_sa, w_fo, tm):
    ch = D_FF // N_DEV

    def body(i, ins, consts, outs, accs):
        da_ref, dh2_ref, h1_ref, ya_ref, yb_ref, zfg_ref = ins
        w1_ref, gm_ref, wmo_ref, wsa_ref, wfo_ref = consts
        dh1_ref, dh1b_ref, dgl_ref, dya_ref, dyb_ref, daa_ref, dab_ref = outs
        dgm_ref, = accs
        du2 = _dot_nt(da_ref[:, 0:ch], w1_ref[0])
        for c in range(1, N_DEV):
            du2 = du2 + _dot_nt(da_ref[:, c * ch:(c + 1) * ch], w1_ref[c])
        dh, dg = _rms_bwd(h1_ref[...], gm_ref[...], du2)
        _acc_rows(dgm_ref, i, dg)
        dh1 = dh2_ref[...] + dh
        dh1_ref[...] = dh1
        dh1b = dh1.astype(BF16)
        dh1b_ref[...] = dh1b
        dmx = _dot_nt(dh1b, wmo_ref[...])
        g0 = _sigmoid(zfg_ref[:, N_FPAD:N_FPAD + D_MODEL])
        g1 = _sigmoid(zfg_ref[:, N_FPAD + D_MODEL:N_FPAD + 2 * D_MODEL])
        dya = (dmx * g0).astype(BF16)
        dyb = (dmx * g1).astype(BF16)
        dya_ref[...] = dya
        dyb_ref[...] = dyb
        dgl_ref[:, 0:D_MODEL] = ((dmx * ya_ref[...].astype(F32)) * g0 * (1.0 - g0)).astype(BF16)
        dgl_ref[:, D_MODEL:2 * D_MODEL] = ((dmx * yb_ref[...].astype(F32)) * g1 * (1.0 - g1)).astype(BF16)
        daa_ref[...] = _dot_nt(dya, wsa_ref[...]).astype(BF16)
        dab_ref[...] = _dot_nt(dyb, wfo_ref[...]).astype(BF16)

    half = D_MODEL // 2
    return _row_call(body, "ffn_bwd_b", h1.shape[0], tm, [da, dh2, h1, ya, yb, zfg],
                     [w1s, g_mlp, w_mo, w_sa, w_fo],
                     [(D_MODEL, F32), (D_MODEL, BF16), (N_GATE, BF16), (D_MODEL, BF16), (D_MODEL, BF16),
                      (half, BF16), (half, BF16)], [D_MODEL])


def _in_proj_bwd(dz, dh1, x, w_all, g_mix, tm):
    def body(i, ins, consts, outs, accs):
        dz_ref, dh1_ref, x_ref = ins
        w_ref, g_ref = consts
        dx_ref, = outs
        dgx_ref, = accs
        du1 = _dot_nt(dz_ref[...], w_ref[...])
        dh, dg = _rms_bwd(x_ref[...], g_ref[...], du1)
        _acc_rows(dgx_ref, i, dg)
        dx_ref[...] = dh1_ref[...] + dh

    return _row_call(body, "in_proj_bwd", x.shape[0], tm, [dz, dh1, x], [w_all, g_mix],
                     [(D_MODEL, F32)], [D_MODEL])


def _matmul_tn(a, b, name, ts, stack_cols=0):
    n_rows, ka = a.shape
    n = b.shape[1]
    tk = min(ka, 1024)
    tn = 896 if n % 1024 else 1024
    n_stack = tn // stack_cols if stack_cols else 0
    assert ka % tk == 0 and n % tn == 0 and n_rows % ts == 0 and (not stack_cols or tk == ka)
    n_steps = n_rows // ts

    def kern(a_ref, b_ref, o_ref, acc_ref):
        s = pl.program_id(2)

        @pl.when(s == 0)
        def _():
            acc_ref[...] = jnp.zeros_like(acc_ref)
        acc_ref[...] += _dot_tn(a_ref[...].astype(BF16), b_ref[...])

        @pl.when(s == n_steps - 1)
        def _():
            if stack_cols:
                for c in range(n_stack):
                    o_ref[c] = acc_ref[:, c * stack_cols:(c + 1) * stack_cols].astype(BF16)
            else:
                o_ref[...] = acc_ref[...].astype(BF16)

    if stack_cols:
        out_spec = pl.BlockSpec((n_stack, tk, stack_cols), lambda i, j, s: (j, 0, 0))
        out_shape = jax.ShapeDtypeStruct((n // stack_cols, ka, stack_cols), BF16)
    else:
        out_spec = pl.BlockSpec((tk, tn), lambda i, j, s: (i, j))
        out_shape = jax.ShapeDtypeStruct((ka, n), BF16)
    return pl.pallas_call(
        kern, grid=(ka // tk, n // tn, n_steps),
        in_specs=[pl.BlockSpec((ts, tk), lambda i, j, s: (s, i)), pl.BlockSpec((ts, tn), lambda i, j, s: (s, j))],
        out_specs=out_spec, out_shape=out_shape, scratch_shapes=[pltpu.VMEM((tk, tn), F32)], name=name,
        compiler_params=_params(3))(a, b)


SCAN_CHUNK = 512


def _decay_cumsum(f_t, b_col):
    n_tok = f_t.shape[1]
    ch = min(SCAN_CHUNK, n_tok)

    def kern(f_ref, b_ref, c_ref):
        r = lax.broadcasted_iota(jnp.int32, (ch, ch), 0)
        c = lax.broadcasted_iota(jnp.int32, (ch, ch), 1)
        tri = (r <= c).astype(F32)
        carry = jnp.zeros((8, 1), F32)
        for k in range(n_tok // ch):
            xv = f_ref[:, k * ch:(k + 1) * ch] + b_ref[...]
            lf = jnp.minimum(xv, 0.0) - jnp.log(1.0 + jnp.exp(-jnp.abs(xv)))
            cs = jnp.dot(lf, tri, precision=lax.Precision.HIGHEST, preferred_element_type=F32) + carry
            c_ref[:, k * ch:(k + 1) * ch] = cs
            carry = cs[:, ch - 1:ch]

    return pl.pallas_call(kern, out_shape=jax.ShapeDtypeStruct((8, n_tok), F32), name="decay_cumsum",
                          compiler_params=_params(0))(f_t, b_col)


def _decay_bwd(g_t, f_t, b_col):
    n_tok = f_t.shape[1]
    ch = min(SCAN_CHUNK, n_tok)

    def kern(g_ref, f_ref, b_ref, df_ref, db_ref):
        r = lax.broadcasted_iota(jnp.int32, (ch, ch), 0)
        c = lax.broadcasted_iota(jnp.int32, (ch, ch), 1)
        tri = (r >= c).astype(F32)
        carry = jnp.zeros((8, 1), F32)
        tot = jnp.zeros((8, 1), F32)
        for k in reversed(range(n_tok // ch)):
            gv = g_ref[:, k * ch:(k + 1) * ch]
            rc = jnp.dot(gv, tri, precision=lax.Precision.HIGHEST, preferred_element_type=F32) + carry
            carry = rc[:, 0:1]
            xv = f_ref[:, k * ch:(k + 1) * ch] + b_ref[...]
            df = -rc / (1.0 + jnp.exp(xv))
            df_ref[:, k * ch:(k + 1) * ch] = df
            tot = tot + jnp.sum(df, axis=1, keepdims=True)
        db_ref[...] = jnp.broadcast_to(tot, db_ref.shape)

    return pl.pallas_call(kern, out_shape=[jax.ShapeDtypeStruct((8, n_tok), F32),
                                           jax.ShapeDtypeStruct((8, LANES), F32)],
                          name="decay_bwd", compiler_params=_params(0))(g_t, f_t, b_col)


def _swa_band_mask(n):
    row = lax.broadcasted_iota(jnp.int32, (SWA_BLOCK, 2 * SWA_BLOCK), 0) + SWA_BLOCK
    col = lax.broadcasted_iota(jnp.int32, (SWA_BLOCK, 2 * SWA_BLOCK), 1)
    cd = (row >> CHUNK_SHIFT) - (col >> CHUNK_SHIFT)
    first_real = jnp.where(n > 0, 0, SWA_BLOCK)
    ok = (cd >= 0) & (cd <= WINDOW_CHUNKS) & (col >= first_real)
    dist = jnp.abs(row - col).astype(F32)
    return ok, dist


def _swap_halves(t):
    return pltpu.roll(t.astype(F32), HEAD_DIM, axis=1).astype(t.dtype)


def _swa_specs():
    blk = SWA_BLOCK
    q = pl.BlockSpec((blk, 4 * LANES), lambda n: (n, 0))
    kp = pl.BlockSpec((blk, LANES), lambda n: (jnp.maximum(n - 1, 0), 4))
    kc = pl.BlockSpec((blk, LANES), lambda n: (n, 4))
    vp = pl.BlockSpec((blk, LANES), lambda n: (jnp.maximum(n - 1, 0), 5))
    vc = pl.BlockSpec((blk, LANES), lambda n: (n, 5))
    return q, kp, kc, vp, vc


def _swa_fwd(zm, sinks):
    n_tok = zm.shape[0]
    blk = SWA_BLOCK

    def kern(q_ref, kp_ref, kc_ref, vp_ref, vc_ref, sink_ref, o_ref, lse_ref):
        n = pl.program_id(0)
        ok, dist = _swa_band_mask(n)
        k2 = jnp.concatenate([kp_ref[...], kc_ref[...]], axis=0)
        v2 = jnp.concatenate([vp_ref[...], vc_ref[...]], axis=0)
        ksw, vsw = _swap_halves(k2), _swap_halves(v2)
        lane = lax.broadcasted_iota(jnp.int32, (blk, LANES), 1)
        lo = lane < HEAD_DIM
        lse_t = jnp.zeros((blk, LANES), F32)
        for pair in range(SWA_HEADS // 2):
            q2 = q_ref[:, pair * LANES:(pair + 1) * LANES]
            kvh = pair // 2
            outs = []
            for a in range(2):
                h = 2 * pair + a
                qa = jnp.where(lo if a == 0 else ~lo, q2, jnp.zeros_like(q2)) * SCALE
                kx, vx = (k2, v2) if a == kvh else (ksw, vsw)
                s = _dot_nt(qa, kx)
                s = jnp.where(ok, s - (2.0 ** -(h + 1)) * dist, NEG)
                sink = sink_ref[h]
                m = jnp.maximum(jnp.max(s, axis=-1, keepdims=True), sink)
                e = jnp.exp(s - m)
                l = jnp.sum(e, axis=-1, keepdims=True) + jnp.exp(sink - m)
                pn = (e * (1.0 / l)).astype(BF16)
                outs.append(_dot(pn, vx))
                lse_t = jnp.where(lane == h, m + jnp.log(l), lse_t)
            o_ref[:, pair * LANES:(pair + 1) * LANES] = jnp.where(lo, outs[0], outs[1]).astype(BF16)
        lse_ref[...] = lse_t

    q, kp, kc, vp, vc = _swa_specs()
    return pl.pallas_call(
        kern, grid=(n_tok // blk,),
        in_specs=[q, kp, kc, vp, vc, pl.BlockSpec(memory_space=pltpu.SMEM)],
        out_specs=[pl.BlockSpec((blk, 4 * LANES), lambda n: (n, 0)), pl.BlockSpec((blk, LANES), lambda n: (n, 0))],
        out_shape=[jax.ShapeDtypeStruct((n_tok, 4 * LANES), BF16), jax.ShapeDtypeStruct((n_tok, LANES), F32)],
        name="swa_fwd", compiler_params=_params(1))(zm, zm, zm, zm, zm, sinks)


def _swa_bwd(zm, sinks, d_out, out, lse):
    n_tok = zm.shape[0]
    blk = SWA_BLOCK

    def kern(q_ref, kp_ref, kc_ref, vp_ref, vc_ref, do_ref, o_ref, lse_ref, sink_ref,
             dq_ref, dkp_ref, dkc_ref, dvp_ref, dvc_ref, dsk_ref):
        n = pl.program_id(0)

        @pl.when(n == 0)
        def _():
            dsk_ref[...] = jnp.zeros_like(dsk_ref)

        ok, dist = _swa_band_mask(n)
        k2 = jnp.concatenate([kp_ref[...], kc_ref[...]], axis=0)
        v2 = jnp.concatenate([vp_ref[...], vc_ref[...]], axis=0)
        ksw, vsw = _swap_halves(k2), _swap_halves(v2)
        lane = lax.broadcasted_iota(jnp.int32, (blk, LANES), 1)
        lo = lane < HEAD_DIM
        lse_t = lse_ref[...]
        zero = jnp.zeros((2 * blk, LANES), F32)
        dk_same, dk_swap, dv_same, dv_swap = zero, zero, zero, zero
        for pair in range(SWA_HEADS // 2):
            cols = slice(pair * LANES, (pair + 1) * LANES)
            q2, do2, o2 = q_ref[:, cols], do_ref[:, cols], o_ref[:, cols]
            kvh = pair // 2
            dqs = []
            for a in range(2):
                h = 2 * pair + a
                half = lo if a == 0 else ~lo
                qa = jnp.where(half, q2, jnp.zeros_like(q2)) * SCALE
                doa = jnp.where(half, do2, jnp.zeros_like(do2))
                kx, vx = (k2, v2) if a == kvh else (ksw, vsw)
                s = _dot_nt(qa, kx)
                s = jnp.where(ok, s - (2.0 ** -(h + 1)) * dist, NEG)
                lse_h = lse_t[:, h:h + 1]
                prob = jnp.exp(s - lse_h)
                dd = jnp.sum(doa.astype(F32) * o2.astype(F32), axis=-1, keepdims=True)
                dp = _dot_nt(doa, vx)
                ds = (prob * (dp - dd)).astype(BF16)
                p_sink = jnp.exp(sink_ref[h] - lse_h)
                dsk_ref[h:h + 1, :] += jnp.broadcast_to(-jnp.sum(p_sink * dd, axis=0, keepdims=True), (1, LANES))
                dqs.append(_dot(ds, kx) * SCALE)
                dk_c = _dot_tn(ds, qa)
                dv_c = _dot_tn(prob.astype(BF16), doa)
                if a == kvh:
                    dk_same, dv_same = dk_same + dk_c, dv_same + dv_c
                else:
                    dk_swap, dv_swap = dk_swap + dk_c, dv_swap + dv_c
            dq_ref[:, cols] = jnp.where(lo, dqs[0], dqs[1]).astype(BF16)
        dk = dk_same + pltpu.roll(dk_swap, HEAD_DIM, axis=1)
        dv = dv_same + pltpu.roll(dv_swap, HEAD_DIM, axis=1)
        dkp_ref[...] = dk[0:blk]
        dkc_ref[...] = dk[blk:2 * blk]
        dvp_ref[...] = dv[0:blk]
        dvc_ref[...] = dv[blk:2 * blk]

    q, kp, kc, vp, vc = _swa_specs()
    wide = pl.BlockSpec((blk, 4 * LANES), lambda n: (n, 0))
    narrow = pl.BlockSpec((blk, LANES), lambda n: (n, 0))
    part = jax.ShapeDtypeStruct((n_tok, LANES), F32)
    return pl.pallas_call(
        kern, grid=(n_tok // blk,),
        in_specs=[q, kp, kc, vp, vc, wide, wide, narrow, pl.BlockSpec(memory_space=pltpu.SMEM)],
        out_specs=[wide, narrow, narrow, narrow, narrow, pl.BlockSpec((8, LANES), lambda n: (0, 0))],
        out_shape=[jax.ShapeDtypeStruct((n_tok, 4 * LANES), BF16), part, part, part, part,
                   jax.ShapeDtypeStruct((8, LANES), F32)],
        name="swa_bwd", compiler_params=_params(1))(zm, zm, zm, zm, zm, d_out, out, lse, sinks)


Q_COL, K_COL, V_COL = 6, 10, 14


def _causal(t, tq, tk):
    row = lax.broadcasted_iota(jnp.int32, (tq, tk), 0)
    col = lax.broadcasted_iota(jnp.int32, (tq, tk), 1)
    return jnp.where(col <= row, t, NEG)


def _lane_tile(stat, width):
    return jnp.tile(stat, (1, width // LANES))


def _fox_fwd(zm, c_pairs, tq):
    n_tok = zm.shape[0]
    nq = n_tok // tq
    pairs = [(i, j) for i in range(nq) for j in range(i + 1)]
    ii = np.asarray([p[0] for p in pairs], np.int32)
    jj = np.asarray([p[1] for p in pairs], np.int32)

    def kern(ii_ref, jj_ref, q_ref, k_ref, v_ref, ck_ref, o_ref, ln_ref, qs_ref, m_ref, l_ref, acc_ref):
        step = pl.program_id(1)
        i, j = ii_ref[step], jj_ref[step]
        lane = lax.broadcasted_iota(jnp.int32, (tq, LANES), 1)
        lo = lane < HEAD_DIM

        @pl.when(j == 0)
        def _():
            q2 = q_ref[...]
            zq = jnp.zeros_like(q2)
            qs_ref[0] = jnp.where(lo, q2, zq) * SCALE
            qs_ref[1] = jnp.where(lo, zq, q2) * SCALE
            m_ref[...] = jnp.full(m_ref.shape, NEG, F32)
            l_ref[...] = jnp.zeros(l_ref.shape, F32)
            acc_ref[...] = jnp.zeros(acc_ref.shape, F32)

        def update(diag):
            kv = k_ref[...]
            v_ones = jnp.concatenate([v_ref[...], jnp.ones((tq, LANES), BF16)], axis=1)
            for a in range(2):
                t = _dot_nt(qs_ref[a], kv) - ck_ref[a:a + 1, :]
                if diag:
                    t = _causal(t, tq, tq)
                m_old = m_ref[a]
                m_new = jnp.maximum(m_old, jnp.max(t, axis=-1, keepdims=True))
                alpha = jnp.exp(m_old - m_new)
                e = jnp.exp(t - _lane_tile(m_new, tq)).astype(BF16)
                pv = _dot(e, v_ones)
                acc_ref[a] = alpha * acc_ref[a] + pv[:, :LANES]
                l_ref[a] = alpha * l_ref[a] + pv[:, LANES:]
                m_ref[a] = m_new

        @pl.when(j < i)
        def _():
            update(False)

        @pl.when(j == i)
        def _():
            update(True)
            o_ref[...] = jnp.where(lo, acc_ref[0] / l_ref[0], acc_ref[1] / l_ref[1]).astype(BF16)
            ln_ref[:, :LANES] = m_ref[0] + jnp.log(l_ref[0])
            ln_ref[:, LANES:] = m_ref[1] + jnp.log(l_ref[1])

    blk = (tq, LANES)
    grid_spec = pltpu.PrefetchScalarGridSpec(
        num_scalar_prefetch=2, grid=(FOX_HEADS // 2, len(pairs)),
        in_specs=[pl.BlockSpec(blk, lambda hp, s, ii, jj: (ii[s], Q_COL + hp)),
                  pl.BlockSpec(blk, lambda hp, s, ii, jj: (jj[s], K_COL + hp)),
                  pl.BlockSpec(blk, lambda hp, s, ii, jj: (jj[s], V_COL + hp)),
                  pl.BlockSpec((None, 2, tq), lambda hp, s, ii, jj: (hp, 0, jj[s]))],
        out_specs=[pl.BlockSpec(blk, lambda hp, s, ii, jj: (ii[s], hp)),
                   pl.BlockSpec((tq, 2 * LANES), lambda hp, s, ii, jj: (ii[s], hp))],
        scratch_shapes=[pltpu.VMEM((2, tq, LANES), BF16), pltpu.VMEM((2, tq, LANES), F32),
                        pltpu.VMEM((2, tq, LANES), F32), pltpu.VMEM((2, tq, LANES), F32)])
    return pl.pallas_call(
        kern, grid_spec=grid_spec,
        out_shape=[jax.ShapeDtypeStruct((n_tok, 4 * LANES), BF16),
                   jax.ShapeDtypeStruct((n_tok, FOX_HEADS * LANES), F32)],
        name="fox_fwd", compiler_params=_params(2))(ii, jj, zm, zm, zm, c_pairs)


def _fox_delta(d_out, out, tm):
    def body(i, ins, consts, outs, accs):
        do_ref, o_ref = ins
        dl_ref, = outs
        lane = lax.broadcasted_iota(jnp.int32, (tm, LANES), 1)
        lo = lane < HEAD_DIM
        for pair in range(FOX_HEADS // 2):
            cols = slice(pair * LANES, (pair + 1) * LANES)
            prod = do_ref[:, cols].astype(F32) * o_ref[:, cols].astype(F32)
            for a in range(2):
                dd = jnp.sum(jnp.where(lo if a == 0 else ~lo, prod, 0.0), axis=-1, keepdims=True)
                h = 2 * pair + a
                dl_ref[:, h * LANES:(h + 1) * LANES] = jnp.broadcast_to(dd, (tm, LANES))

    return _row_call(body, "fox_delta", d_out.shape[0], tm, [d_out, out], [], [(FOX_HEADS * LANES, F32)], [])[0]


def _fox_bwd(zm, c_pairs, d_out, lnorm, delta, tq):
    n_tok = zm.shape[0]
    nq = n_tok // tq
    pairs = [(i, j) for i in range(nq) for j in range(i + 1)]
    ii = np.asarray([p[0] for p in pairs], np.int32)
    jj = np.asarray([p[1] for p in pairs], np.int32)

    def kern(ii_ref, jj_ref, q_ref, k_ref, v_ref, ck_ref, do_ref, ln_ref, dl_ref,
             dq_ref, dk_ref, dv_ref, cs_ref, rs_ref, qs_ref, qo_ref, dos_ref, dq_acc):
        step = pl.program_id(1)
        i, j = ii_ref[step], jj_ref[step]
        lane = lax.broadcasted_iota(jnp.int32, (tq, LANES), 1)
        lo = lane < HEAD_DIM

        @pl.when(step == 0)
        def _():
            dk_ref[...] = jnp.zeros_like(dk_ref)
            dv_ref[...] = jnp.zeros_like(dv_ref)
            cs_ref[...] = jnp.zeros_like(cs_ref)

        @pl.when(j == 0)
        def _():
            q2, do2 = q_ref[...], do_ref[...]
            zq = jnp.zeros_like(q2)
            ones = jnp.ones((tq, LANES), BF16)
            for a in range(2):
                half = lo if a == 0 else ~lo
                qa = jnp.where(half, q2, zq) * SCALE
                qs_ref[a] = qa
                qo_ref[a] = jnp.concatenate([qa, ones], axis=1)
                dos_ref[a] = jnp.where(half, do2, zq)
            dq_acc[...] = jnp.zeros(dq_acc.shape, F32)

        def update(diag):
            kv, vv = k_ref[...], v_ref[...]
            k_ones = jnp.concatenate([kv, jnp.ones((tq, LANES), BF16)], axis=1)
            dks, dv = [], None
            for a in range(2):
                t = _dot_nt(qs_ref[a], kv) - ck_ref[a:a + 1, :]
                if diag:
                    t = _causal(t, tq, tq)
                prob = jnp.exp(t - _lane_tile(ln_ref[:, a * LANES:(a + 1) * LANES], tq))
                dp = _dot_nt(dos_ref[a], vv)
                ds = (prob * (dp - _lane_tile(dl_ref[:, a * LANES:(a + 1) * LANES], tq))).astype(BF16)
                dq_acc[a] += _dot(ds, k_ones)
                dks.append(_dot_tn(ds, qo_ref[a]))
                dv_a = _dot_tn(prob.astype(BF16), dos_ref[a])
                dv = dv_a if dv is None else dv + dv_a
            rows = pl.ds(pl.multiple_of(j * tq, tq), tq)
            dk_ref[rows, :] += dks[0][:, :LANES] + dks[1][:, :LANES]
            cs_ref[rows, :] += jnp.where(lo, dks[0][:, LANES:], dks[1][:, LANES:])
            dv_ref[rows, :] += dv

        @pl.when(j < i)
        def _():
            update(False)

        @pl.when(j == i)
        def _():
            update(True)
            dq_ref[...] = jnp.where(lo, dq_acc[0, :, :LANES], dq_acc[1, :, :LANES]) * SCALE
            rs_ref[...] = jnp.where(lo, dq_acc[0, :, LANES:], dq_acc[1, :, LANES:])

    blk = (tq, LANES)
    by_i = lambda col: (lambda hp, s, ii, jj: (ii[s], col + hp))
    by_j = lambda col: (lambda hp, s, ii, jj: (jj[s], col + hp))
    stat = pl.BlockSpec((tq, 2 * LANES), by_i(0))
    whole = pl.BlockSpec((n_tok, LANES), lambda hp, s, ii, jj: (0, hp))
    grid_spec = pltpu.PrefetchScalarGridSpec(
        num_scalar_prefetch=2, grid=(FOX_HEADS // 2, len(pairs)),
        in_specs=[pl.BlockSpec(blk, by_i(Q_COL)), pl.BlockSpec(blk, by_j(K_COL)), pl.BlockSpec(blk, by_j(V_COL)),
                  pl.BlockSpec((None, 2, tq), lambda hp, s, ii, jj: (hp, 0, jj[s])),
                  pl.BlockSpec(blk, by_i(0)), stat, stat],
        out_specs=[pl.BlockSpec(blk, by_i(0)), whole, whole, whole, pl.BlockSpec(blk, by_i(0))],
        scratch_shapes=[pltpu.VMEM((2, tq, LANES), BF16), pltpu.VMEM((2, tq, 2 * LANES), BF16),
                        pltpu.VMEM((2, tq, LANES), BF16), pltpu.VMEM((2, tq, 2 * LANES), F32)])
    wide = jax.ShapeDtypeStruct((n_tok, 4 * LANES), F32)
    return pl.pallas_call(
        kern, grid_spec=grid_spec, out_shape=[wide] * 5,
        name="fox_bwd", compiler_params=_params(2))(ii, jj, zm, zm, zm, c_pairs, d_out, lnorm, delta)


def _my_pos():
    return lax.axis_index("x"), lax.axis_index("y"), lax.axis_index("c")


def _all_gather(shards):
    n_w = len(shards)

    def kern(*refs):
        x_refs, out_refs = refs[:n_w], refs[n_w:2 * n_w]
        send_sems, recv_sems, local_sems = refs[2 * n_w:]
        x, y, c = _my_pos()
        me, sibling = (x, y, c), (x, y, 1 - c)
        chips = [(1 - x, y), (x, 1 - y), (1 - x, 1 - y)]

        def slot(w, px, py, pc):
            return out_refs[w].at[4 * px + 2 * py + pc]

        def copy(w, k, block, to, src=None):
            return pltpu.make_async_remote_copy(
                src_ref=slot(w, *block) if src is None else src, dst_ref=slot(w, *block),
                send_sem=send_sems.at[7 * w + k], recv_sem=recv_sems.at[7 * w + k], device_id=to, device_id_type=MESH)

        local, started = [], []
        for w in range(n_w):
            mine = pltpu.make_async_copy(x_refs[w], slot(w, *me), local_sems.at[w])
            mine.start()
            local.append(mine)
            first = [copy(w, 0, me, sibling, src=x_refs[w])]
            first += [copy(w, 1 + k, me, (*chip, c), src=x_refs[w]) for k, chip in enumerate(chips)]
            for cp in first:
                cp.start()
            started += first
        for k, chip in enumerate(chips):
            for w in range(n_w):
                copy(w, 1 + k, (*chip, c), me).wait_recv()
                passed = copy(w, 4 + k, (*chip, c), sibling)
                passed.start()
                started.append(passed)
        for w in range(n_w):
            copy(w, 0, sibling, me).wait_recv()
            for k, chip in enumerate(chips):
                copy(w, 4 + k, (*chip, 1 - c), me).wait_recv()
        for cp in started:
            cp.wait_send()
        for cp in local:
            cp.wait()

    any_spec = pl.BlockSpec(memory_space=pl.ANY)
    return pl.pallas_call(
        kern, out_shape=[jax.ShapeDtypeStruct((N_DEV,) + s.shape, s.dtype) for s in shards],
        in_specs=[any_spec] * n_w, out_specs=[any_spec] * n_w,
        scratch_shapes=[pltpu.SemaphoreType.DMA((7 * n_w,)), pltpu.SemaphoreType.DMA((7 * n_w,)),
                        pltpu.SemaphoreType.DMA((n_w,))],
        name="weight_all_gather")(*shards)


def _grad_exchange(grads, small):
    n_w = len(grads)

    def kern(*refs):
        g_refs, s_ref = refs[:n_w], refs[n_w]
        part_refs, sall_ref = refs[n_w + 1:2 * n_w + 1], refs[2 * n_w + 1]
        send_sems, recv_sems, local_sems = refs[2 * n_w + 2:]
        x, y, c = _my_pos()
        my_id = 4 * x + 2 * y + c
        n_sem = n_w + 1
        local = [pltpu.make_async_copy(g_refs[w].at[my_id], part_refs[w].at[0], local_sems.at[w]) for w in range(n_w)]
        local.append(pltpu.make_async_copy(s_ref, sall_ref.at[my_id], local_sems.at[n_w]))
        for cp in local:
            cp.start()
        sends = []
        for k in range(1, N_DEV):
            px, py, pc = x ^ (k >> 2), y ^ ((k >> 1) & 1), c ^ (k & 1)
            peer_id = 4 * px + 2 * py + pc
            sml = pltpu.make_async_remote_copy(
                src_ref=s_ref, dst_ref=sall_ref.at[my_id], send_sem=send_sems.at[n_sem * k + n_w],
                recv_sem=recv_sems.at[n_sem * k + n_w], device_id=(px, py, pc), device_id_type=MESH)
            sml.start()
            sends.append(sml)
            for w in range(n_w):
                big = pltpu.make_async_remote_copy(
                    src_ref=g_refs[w].at[peer_id], dst_ref=part_refs[w].at[k], send_sem=send_sems.at[n_sem * k + w],
                    recv_sem=recv_sems.at[n_sem * k + w], device_id=(px, py, pc), device_id_type=MESH)
                big.start()
                sends.append(big)
        for k in range(1, N_DEV):
            px, py, pc = x ^ (k >> 2), y ^ ((k >> 1) & 1), c ^ (k & 1)
            peer_id = 4 * px + 2 * py + pc
            pltpu.make_async_remote_copy(
                src_ref=s_ref, dst_ref=sall_ref.at[peer_id], send_sem=send_sems.at[n_sem * k + n_w],
                recv_sem=recv_sems.at[n_sem * k + n_w], device_id=(x, y, c), device_id_type=MESH).wait_recv()
            for w in range(n_w):
                pltpu.make_async_remote_copy(
                    src_ref=g_refs[w].at[my_id], dst_ref=part_refs[w].at[k], send_sem=send_sems.at[n_sem * k + w],
                    recv_sem=recv_sems.at[n_sem * k + w], device_id=(x, y, c), device_id_type=MESH).wait_recv()
        for cp in sends:
            cp.wait_send()
        for cp in local:
            cp.wait()

    any_spec = pl.BlockSpec(memory_space=pl.ANY)
    n_sems = (n_w + 1) * N_DEV
    return pl.pallas_call(
        kern, out_shape=[jax.ShapeDtypeStruct(g.shape, g.dtype) for g in grads]
        + [jax.ShapeDtypeStruct((N_DEV,) + small.shape, small.dtype)],
        in_specs=[any_spec] * (n_w + 1), out_specs=[any_spec] * (n_w + 1),
        scratch_shapes=[pltpu.SemaphoreType.DMA((n_sems,)), pltpu.SemaphoreType.DMA((n_sems,)),
                        pltpu.SemaphoreType.DMA((n_w + 1,))],
        name="grad_exchange")(*grads, small)


ADAMW_BLOCK_BYTES = 2 * 1024 * 1024


def _adamw(parts, w, m, v, name):
    n_parts, n_rows, n_cols = parts.shape
    limit = max(8, ADAMW_BLOCK_BYTES // (n_parts * n_cols * parts.dtype.itemsize))
    tr = max(t for t in range(8, n_rows + 1, 8) if n_rows % t == 0 and t <= limit)

    def kern(p_ref, w_ref, m_ref, v_ref, g_out, d_out, m_out, v_out):
        g = p_ref[0].astype(F32)
        for k in range(1, n_parts):
            g = g + p_ref[k].astype(F32)
        m_new = ADAM_B1 * m_ref[...] + (1.0 - ADAM_B1) * g
        v_new = ADAM_B2 * v_ref[...] + (1.0 - ADAM_B2) * jnp.square(g)
        m_hat = m_new / (1.0 - ADAM_B1 ** ADAM_STEP)
        v_hat = v_new / (1.0 - ADAM_B2 ** ADAM_STEP)
        g_out[...] = g
        d_out[...] = -ADAM_LR * (m_hat / (jnp.sqrt(v_hat) + ADAM_EPS) + ADAM_WD * w_ref[...])
        m_out[...] = m_new
        v_out[...] = v_new

    row = pl.BlockSpec((tr, n_cols), lambda i: (i, 0))
    out = jax.ShapeDtypeStruct((n_rows, n_cols), F32)
    return pl.pallas_call(
        kern, grid=(n_rows // tr,),
        in_specs=[pl.BlockSpec((n_parts, tr, n_cols), lambda i: (0, i, 0)), row, row, row],
        out_specs=[row, row, row, row], out_shape=[out, out, out, out], name=name,
        compiler_params=_params(1))(parts, w, m, v)


SHARDED = {
    "w_in": ((D_MODEL, D_IN), 1), "w_br_swa": ((512, D_MODEL), 1), "w_br_fox": ((512, D_MODEL), 1),
    "w_mix_out": ((D_MODEL, D_MODEL), 0), "w_ff1": ((D_MODEL, D_FF), 1), "w_ff2": ((D_FF, D_MODEL), 0),
    "w_ple_gate": ((D_MODEL, D_MODEL), 0), "w_ple_proj": ((PLE_DIM, D_MODEL), 1),
}
W_IN_SHARD = D_IN // N_DEV
W_IN_PAD = 640
SMALL = ("g_mix", "g_mlp", "g_ple", "g_final", "b_forget", "swa_sinks")
SMALL_COLS = 1024


def _wire_shard(name, a):
    a = a.reshape(a.shape[-2:])
    return jnp.pad(a, ((0, 0), (0, W_IN_PAD - W_IN_SHARD))) if name == "w_in" else a


def _from_wire(name, a):
    return (a[:, :W_IN_SHARD] if name == "w_in" else a)[None]


def _cols_to_full(stacked):
    return jnp.concatenate([stacked[d] for d in range(N_DEV)], axis=1)


def _full_to_cols(full):
    n = full.shape[1] // N_DEV
    return jnp.stack([full[:, d * n:(d + 1) * n] for d in range(N_DEV)])


def _w_all_from_wire(stacked):
    w_in = jnp.concatenate([stacked[d][:, :W_IN_SHARD] for d in range(N_DEV)], axis=1)
    fpad = jnp.zeros((D_MODEL, N_FPAD - FOX_HEADS), stacked.dtype)
    return jnp.concatenate([w_in[:, :N_MAIN + FOX_HEADS], fpad, w_in[:, N_MAIN + FOX_HEADS:]], axis=1)


def _dw_in_to_wire(dw_all):
    dw_in = jnp.concatenate([dw_all[:, :N_MAIN + FOX_HEADS], dw_all[:, N_MAIN + N_FPAD:]], axis=1)
    pad = jnp.zeros((D_MODEL, W_IN_PAD - W_IN_SHARD), dw_all.dtype)
    return jnp.stack([jnp.concatenate([dw_in[:, d * W_IN_SHARD:(d + 1) * W_IN_SHARD], pad], axis=1)
                      for d in range(N_DEV)])


def _pack_small(vals):
    rows = [jnp.pad(vals[n].reshape(-1), (0, SMALL_COLS - vals[n].size)) for n in SMALL]
    rows += [jnp.zeros((SMALL_COLS,), F32)] * (8 - len(SMALL))
    return jnp.stack(rows)


def _unpack_small(slab, like):
    return {n: slab[r, :like[n].size].reshape(like[n].shape) for r, n in enumerate(SMALL)}


def _local_step(x, p, tgt, w, small, tm, tq, ts):
    n_tok = x.shape[0]
    row = lambda v: v.reshape(1, -1)
    g_mix, g_mlp, g_ple, g_fin = row(small["g_mix"]), row(small["g_mlp"]), row(small["g_ple"]), row(small["g_final"])
    sinks = small["swa_sinks"].reshape(-1)
    b_col = small["b_forget"].reshape(FOX_HEADS, 1)

    u1, zm, zfg = _in_proj(x, g_mix, w["w_all"], tm)
    f_t = zfg[:, :FOX_HEADS].T
    c_pairs = _decay_cumsum(f_t, b_col).reshape(FOX_HEADS // 2, 2, n_tok)
    attn_a, lse_a = _swa_fwd(zm, sinks)
    attn_b, ln_b = _fox_fwd(zm, c_pairs, tq)
    ya, yb, mixed, h1, u2 = _mix_fwd(attn_a, attn_b, zfg, x, w["w_br_swa"], w["w_br_fox"], w["w_mix_out"], g_mlp, tm)
    a, r, h2 = _ffn_fwd(u2, h1, w["w_ff1"], w["w_ff2"], tm // 2)
    dh3, dlg, dpp, u3, loss_acc, dgf = _head_fwd_bwd(h2, p, tgt, g_ple, w["w_ple_gate"], w["w_ple_proj"], g_fin, tm)

    dh2, dh2b, da, dgp = _ffn_bwd_a(dlg, dh3, h2, a, w["w_ple_gate"], g_ple, w["w_ff2"], tm // 2)
    dh1, dh1b, dgl, dya, dyb, daa, dab, dgm = _ffn_bwd_b(
        da, dh2, h1, ya, yb, zfg, w["w_ff1"], g_mlp, w["w_mix_out"], w["w_br_swa"], w["w_br_fox"], tm // 2)
    dq_a, dkp, dkc, dvp, dvc, dsk = _swa_bwd(zm, sinks, daa, attn_a, lse_a)
    delta_b = _fox_delta(dab, attn_b, tm)
    dq_b, dk_b, dv_b, cs, rs = _fox_bwd(zm, c_pairs, dab, ln_b, delta_b, tq)

    up = lambda t: jnp.concatenate([t[SWA_BLOCK:], jnp.zeros((SWA_BLOCK, LANES), F32)], axis=0)
    dk_a, dv_a = dkc + up(dkp), dvc + up(dvp)
    per_head = lambda t: t.reshape(n_tok, FOX_HEADS, HEAD_DIM)[:, :, 0].T
    df_t, db = _decay_bwd(per_head(cs) - per_head(rs), f_t, b_col)
    df = jnp.pad(df_t.T, ((0, 0), (0, N_FPAD - FOX_HEADS)))
    dz = jnp.concatenate([dq_a, dk_a.astype(BF16), dv_a.astype(BF16), dq_b.astype(BF16), dk_b.astype(BF16), dv_b.astype(BF16),
                          df.astype(BF16), dgl], axis=1)
    dx, dgx = _in_proj_bwd(dz, dh1, x, w["w_all"], g_mix, tm)

    dw = {
        "w_all": _matmul_tn(u1, dz, "dw_in", ts),
        "w_br_swa": _matmul_tn(attn_a, dya, "dw_br_swa", ts),
        "w_br_fox": _matmul_tn(attn_b, dyb, "dw_br_fox", ts),
        "w_mix_out": _matmul_tn(mixed, dh1b, "dw_mix_out", ts),
        "w_ff1": _matmul_tn(u2, da, "dw_ff1", ts, stack_cols=D_FF // N_DEV),
        "w_ff2": _matmul_tn(r, dh2b, "dw_ff2", ts),
        "w_ple_gate": _matmul_tn(u3, dlg, "dw_ple_gate", ts),
        "w_ple_proj": _matmul_tn(p, dpp, "dw_ple_proj", ts),
    }
    dsmall = {"g_mix": dgx[0], "g_mlp": dgm[0], "g_ple": dgp[0], "g_final": dgf[0],
              "b_forget": db[:, 0], "swa_sinks": dsk[:, 0]}
    return loss_acc[0, 0], dx, dw, dsmall


def _gathered_to_local(g):
    rows = lambda t: t.reshape(-1, t.shape[-1])
    return {"w_all": _w_all_from_wire(g["w_in"]), "w_br_swa": _cols_to_full(g["w_br_swa"]),
            "w_br_fox": _cols_to_full(g["w_br_fox"]), "w_mix_out": rows(g["w_mix_out"]), "w_ff1": g["w_ff1"],
            "w_ff2": g["w_ff2"], "w_ple_gate": rows(g["w_ple_gate"]), "w_ple_proj": _cols_to_full(g["w_ple_proj"])}


def _local_to_wire(dw):
    by_rows = lambda t: t.reshape(N_DEV, t.shape[0] // N_DEV, t.shape[1])
    return {"w_in": _dw_in_to_wire(dw["w_all"]), "w_br_swa": _full_to_cols(dw["w_br_swa"]),
            "w_br_fox": _full_to_cols(dw["w_br_fox"]), "w_mix_out": by_rows(dw["w_mix_out"]), "w_ff1": dw["w_ff1"],
            "w_ff2": by_rows(dw["w_ff2"]), "w_ple_gate": by_rows(dw["w_ple_gate"]),
            "w_ple_proj": _full_to_cols(dw["w_ple_proj"])}


def kernel(x, p, g_mix, w_in, b_forget, swa_sinks, w_br_swa, w_br_fox, w_mix_out, g_mlp, w_ff1, w_ff2, g_ple, w_ple_gate, w_ple_proj, g_final, loss_target, m_g_mix, m_w_in, m_b_forget, m_swa_sinks, m_w_br_swa, m_w_br_fox, m_w_mix_out, m_g_mlp, m_w_ff1, m_w_ff2, m_g_ple, m_w_ple_gate, m_w_ple_proj, m_g_final, v_g_mix, v_w_in, v_b_forget, v_swa_sinks, v_w_br_swa, v_w_br_fox, v_w_mix_out, v_g_mlp, v_w_ff1, v_w_ff2, v_g_ple, v_w_ple_gate, v_w_ple_proj, v_g_final):
    given = dict(g_mix=g_mix, w_in=w_in, b_forget=b_forget, swa_sinks=swa_sinks, w_br_swa=w_br_swa, w_br_fox=w_br_fox,
                 w_mix_out=w_mix_out, g_mlp=g_mlp, w_ff1=w_ff1, w_ff2=w_ff2, g_ple=g_ple, w_ple_gate=w_ple_gate,
                 w_ple_proj=w_ple_proj, g_final=g_final)
    mom = dict(g_mix=m_g_mix, w_in=m_w_in, b_forget=m_b_forget, swa_sinks=m_swa_sinks, w_br_swa=m_w_br_swa,
               w_br_fox=m_w_br_fox, w_mix_out=m_w_mix_out, g_mlp=m_g_mlp, w_ff1=m_w_ff1, w_ff2=m_w_ff2, g_ple=m_g_ple,
               w_ple_gate=m_w_ple_gate, w_ple_proj=m_w_ple_proj, g_final=m_g_final)
    vel = dict(g_mix=v_g_mix, w_in=v_w_in, b_forget=v_b_forget, swa_sinks=v_swa_sinks, w_br_swa=v_w_br_swa,
               w_br_fox=v_w_br_fox, w_mix_out=v_w_mix_out, g_mlp=v_g_mlp, w_ff1=v_w_ff1, w_ff2=v_w_ff2, g_ple=v_g_ple,
               w_ple_gate=v_w_ple_gate, w_ple_proj=v_w_ple_proj, g_final=v_g_final)
    names = list(given)
    sharded = list(SHARDED)

    w_wire = {n: _wire_shard(n, given[n]) for n in sharded}
    gathered = _all_gather([w_wire[n].astype(BF16) for n in sharded])
    local_w = _gathered_to_local(dict(zip(sharded, gathered)))
    small = {n: given[n].reshape(-1) for n in SMALL}

    n_tok = x.shape[1]
    tile = min(512, n_tok // 2)
    loss_part, dx, dw, dsmall = _local_step(x[0], p[0, 0], loss_target[0], local_w, small, tm=tile, tq=tile, ts=tile)
    loss = lax.psum(loss_part, AXES)

    dw_wire = _local_to_wire(dw)
    *parts, small_all = _grad_exchange([dw_wire[n] for n in sharded], _pack_small(dsmall))

    res = {}
    for n, part in zip(sharded, parts):
        flat = part.reshape(N_DEV, -1, part.shape[-1])
        outs = _adamw(flat, w_wire[n], _wire_shard(n, mom[n]), _wire_shard(n, vel[n]), "adamw_" + n)
        res[n] = [_from_wire(n, o) for o in outs]
    outs_s = _adamw(small_all, _pack_small(small), _pack_small({n: mom[n] for n in SMALL}),
                    _pack_small({n: vel[n] for n in SMALL}), "adamw_small")
    small_res = [_unpack_small(o, given) for o in outs_s]

    groups = [[res[n][k] if n in res else small_res[k][n] for n in names] for k in range(4)]
    return (loss, dx[None], *groups[0], *groups[1], *groups[2], *groups[3])
```

```python
import numpy as np
import jax
import jax.numpy as jnp
from jax import lax
from jax.experimental import pallas as pl
from jax.experimental.pallas import tpu as pltpu

F32 = jnp.float32
BF16 = jnp.bfloat16

D_MODEL = 1024
HEAD_DIM = 64
SWA_HEADS = 8
FOX_HEADS = 8
CHUNK_SHIFT = 6
SWA_BLOCK = 128
WINDOW_CHUNKS = 2
D_FF = 4096
PLE_DIM = 256
RMS_EPS = 1e-6
N_MAIN = 2304
N_FPAD = 128
N_GATE = 2048
N_ALL = N_MAIN + N_FPAD + N_GATE
D_IN = N_MAIN + FOX_HEADS + N_GATE
SCALE = HEAD_DIM ** -0.5
NEG = -1e30

ADAM_LR = 0.001
ADAM_B1 = 0.9
ADAM_B2 = 0.999
ADAM_EPS = 1e-08
ADAM_WD = 0.01
ADAM_STEP = 10

N_DEV = 8
LANES = 128
V7X_VMEM_BYTES = 64 * 1024 * 1024
VMEM_LIMIT = V7X_VMEM_BYTES * 3 // 4
MESH = pl.DeviceIdType.MESH
AXES = ("x", "y", "c")

_NT = (((1,), (1,)), ((), ()))
_TN = (((0,), (0,)), ((), ()))


def _params(n_grid):
    return pltpu.CompilerParams(dimension_semantics=("arbitrary",) * n_grid, vmem_limit_bytes=VMEM_LIMIT)


def _chunks(n, step):
    return [(s, min(step, n - s)) for s in range(0, n, step)]


def _sigmoid(x):
    return 1.0 / (1.0 + jnp.exp(-x))


def _dot(a, b):
    return jnp.dot(a, b, preferred_element_type=F32)


def _dot_nt(a, b):
    return lax.dot_general(a, b, _NT, preferred_element_type=F32)


def _dot_tn(a, b):
    return lax.dot_general(a, b, _TN, preferred_element_type=F32)


def _rms(h):
    return lax.rsqrt(jnp.mean(h * h, axis=-1, keepdims=True) + RMS_EPS)


def _rms_bwd(h, g, du):
    rs = _rms(h)
    n = h * rs
    dn = du * g
    dh = rs * (dn - n * jnp.mean(dn * n, axis=-1, keepdims=True))
    return dh, jnp.sum(du * n, axis=0, keepdims=True)


def _acc_rows(ref, i, row):
    @pl.when(i == 0)
    def _():
        ref[...] = jnp.zeros_like(ref)
    ref[...] += jnp.broadcast_to(row, ref.shape)


def _row_call(body, name, n_rows, tm, row_ins, const_ins, row_outs, acc_outs):
    n_ri, n_ci, n_ro = len(row_ins), len(const_ins), len(row_outs)

    def kern(*refs):
        i = pl.program_id(0)
        body(i, refs[:n_ri], refs[n_ri:n_ri + n_ci], refs[n_ri + n_ci:n_ri + n_ci + n_ro],
             refs[n_ri + n_ci + n_ro:])

    def whole(a):
        zeros = (0,) * a.ndim
        return pl.BlockSpec(a.shape, lambda i: zeros, pipeline_mode=pl.Buffered(1))

    in_specs = [pl.BlockSpec((tm, a.shape[1]), lambda i: (i, 0)) for a in row_ins]
    in_specs += [whole(a) for a in const_ins]
    out_specs = [pl.BlockSpec((tm, c), lambda i: (i, 0)) for c, _ in row_outs]
    out_specs += [pl.BlockSpec((8, c), lambda i: (0, 0)) for c in acc_outs]
    out_shape = [jax.ShapeDtypeStruct((n_rows, c), dt) for c, dt in row_outs]
    out_shape += [jax.ShapeDtypeStruct((8, c), F32) for c in acc_outs]
    return pl.pallas_call(kern, grid=(n_rows // tm,), in_specs=in_specs, out_specs=out_specs,
                          out_shape=out_shape, name=name, compiler_params=_params(1))(*row_ins, *const_ins)


def _in_proj(x, g_mix, w_all, tm):
    def body(i, ins, consts, outs, accs):
        x_ref, = ins
        g_ref, w_ref = consts
        u_ref, zm_ref, zfg_ref = outs
        xv = x_ref[...]
        u = ((xv * _rms(xv)) * g_ref[...]).astype(BF16)
        u_ref[...] = u
        for s, n in _chunks(N_MAIN, 768):
            zm_ref[:, s:s + n] = _dot(u, w_ref[:, s:s + n]).astype(BF16)
        for s, n in _chunks(N_FPAD + N_GATE, 512):
            zfg_ref[:, s:s + n] = _dot(u, w_ref[:, N_MAIN + s:N_MAIN + s + n])

    return _row_call(body, "in_proj", x.shape[0], tm, [x], [g_mix, w_all],
                     [(D_MODEL, BF16), (N_MAIN, BF16), (N_FPAD + N_GATE, F32)], [])


def _mix_fwd(attn_a, attn_b, zfg, x, w_sa, w_fo, w_mo, g_mlp, tm):
    def body(i, ins, consts, outs, accs):
        aa_ref, ab_ref, zfg_ref, x_ref = ins
        wsa_ref, wfo_ref, wmo_ref, g_ref = consts
        ya_ref, yb_ref, mx_ref, h1_ref, u2_ref = outs
        ya = _dot(aa_ref[...], wsa_ref[...])
        yb = _dot(ab_ref[...], wfo_ref[...])
        g0 = _sigmoid(zfg_ref[:, N_FPAD:N_FPAD + D_MODEL])
        g1 = _sigmoid(zfg_ref[:, N_FPAD + D_MODEL:N_FPAD + 2 * D_MODEL])
        mixed = (g0 * ya + g1 * yb).astype(BF16)
        ya_ref[...] = ya.astype(BF16)
        yb_ref[...] = yb.astype(BF16)
        mx_ref[...] = mixed
        h1 = x_ref[...] + _dot(mixed, wmo_ref[...])
        h1_ref[...] = h1
        u2_ref[...] = ((h1 * _rms(h1)) * g_ref[...]).astype(BF16)

    return _row_call(body, "mix_fwd", x.shape[0], tm, [attn_a, attn_b, zfg, x], [w_sa, w_fo, w_mo, g_mlp],
                     [(D_MODEL, BF16), (D_MODEL, BF16), (D_MODEL, BF16), (D_MODEL, F32), (D_MODEL, BF16)], [])


def _ffn_fwd(u2, h1, w1s, w2s, tm):
    ch = D_FF // N_DEV

    def body(i, ins, consts, outs, accs):
        u_ref, h1_ref = ins
        w1_ref, w2_ref = consts
        a_ref, r_ref, h2_ref = outs
        u = u_ref[...]
        acc = h1_ref[...]
        for c in range(N_DEV):
            a = _dot(u, w1_ref[c])
            a_ref[:, c * ch:(c + 1) * ch] = a.astype(BF16)
            r = jnp.square(jnp.maximum(a, 0.0)).astype(BF16)
            r_ref[:, c * ch:(c + 1) * ch] = r
            acc = acc + _dot(r, w2_ref[c])
        h2_ref[...] = acc

    return _row_call(body, "ffn_fwd", u2.shape[0], tm, [u2, h1], [w1s, w2s],
                     [(D_FF, BF16), (D_FF, BF16), (D_MODEL, F32)], [])


def _head_fwd_bwd(h2, p, tgt, g_ple, w_pg, w_pp, g_fin, tm):
    def body(i, ins, consts, outs, accs):
        h2_ref, p_ref, t_ref = ins
        gp_ref, wpg_ref, wpp_ref, gf_ref = consts
        dh3_ref, dlg_ref, dpp_ref, u3_ref = outs
        loss_ref, dgf_ref = accs
        h2 = h2_ref[...]
        u3 = ((h2 * _rms(h2)) * gp_ref[...]).astype(BF16)
        u3_ref[...] = u3
        pg = _sigmoid(_dot(u3, wpg_ref[...]))
        pp = _dot(p_ref[...].astype(BF16), wpp_ref[...])
        h3 = h2 + pg * pp
        rs3 = _rms(h3)
        n3 = h3 * rs3
        gf = gf_ref[...]
        err = n3 * gf - t_ref[...]
        row_loss = 0.5 * jnp.mean(err * err, axis=-1, keepdims=True)
        _acc_rows(loss_ref, i, jnp.broadcast_to(jnp.sum(row_loss, axis=0, keepdims=True), (1, LANES)))
        dy = err * (1.0 / D_MODEL)
        _acc_rows(dgf_ref, i, jnp.sum(dy * n3, axis=0, keepdims=True))
        dn = dy * gf
        dh3 = rs3 * (dn - n3 * jnp.mean(dn * n3, axis=-1, keepdims=True))
        dh3_ref[...] = dh3
        dpp_ref[...] = (dh3 * pg).astype(BF16)
        dlg_ref[...] = ((dh3 * pp) * pg * (1.0 - pg)).astype(BF16)

    return _row_call(body, "head_fwd_bwd", h2.shape[0], tm, [h2, p, tgt], [g_ple, w_pg, w_pp, g_fin],
                     [(D_MODEL, F32), (D_MODEL, BF16), (D_MODEL, BF16), (D_MODEL, BF16)], [LANES, D_MODEL])


def _ffn_bwd_a(dlg, dh3, h2, a, w_pg, g_ple, w2s, tm):
    ch = D_FF // N_DEV

    def body(i, ins, consts, outs, accs):
        dlg_ref, dh3_ref, h2_ref, a_ref = ins
        wpg_ref, gp_ref, w2_ref = consts
        dh2_ref, dh2b_ref, da_ref = outs
        dgp_ref, = accs
        du3 = _dot_nt(dlg_ref[...], wpg_ref[...])
        dh, dg = _rms_bwd(h2_ref[...], gp_ref[...], du3)
        _acc_rows(dgp_ref, i, dg)
        dh2 = dh3_ref[...] + dh
        dh2_ref[...] = dh2
        dh2b = dh2.astype(BF16)
        dh2b_ref[...] = dh2b
        for c in range(N_DEV):
            dr = _dot_nt(dh2b, w2_ref[c])
            av = a_ref[:, c * ch:(c + 1) * ch].astype(F32)
            da_ref[:, c * ch:(c + 1) * ch] = (dr * (2.0 * jnp.maximum(av, 0.0))).astype(BF16)

    return _row_call(body, "ffn_bwd_a", h2.shape[0], tm, [dlg, dh3, h2, a], [w_pg, g_ple, w2s],
                     [(D_MODEL, F32), (D_MODEL, BF16), (D_FF, BF16)], [D_MODEL])


def _ffn_bwd_b(da, dh2, h1, ya, yb, zfg, w1s, g_mlp, w_mo, w_sa, w_fo, tm):
    ch = D_FF // N_DEV

    def body(i, ins, consts, outs, accs):
        da_ref, dh2_ref, h1_ref, ya_ref, yb_ref, zfg_ref = ins
        w1_ref, gm_ref, wmo_ref, wsa_ref, wfo_ref = consts
        dh1_ref, dh1b_ref, dgl_ref, dya_ref, dyb_ref, daa_ref, dab_ref = outs
        dgm_ref, = accs
        du2 = _dot_nt(da_ref[:, 0:ch], w1_ref[0])
        for c in range(1, N_DEV):
            du2 = du2 + _dot_nt(da_ref[:, c * ch:(c + 1) * ch], w1_ref[c])
        dh, dg = _rms_bwd(h1_ref[...], gm_ref[...], du2)
        _acc_rows(dgm_ref, i, dg)
        dh1 = dh2_ref[...] + dh
        dh1_ref[...] = dh1
        dh1b = dh1.astype(BF16)
        dh1b_ref[...] = dh1b
        dmx = _dot_nt(dh1b, wmo_ref[...])
        g0 = _sigmoid(zfg_ref[:, N_FPAD:N_FPAD + D_MODEL])
        g1 = _sigmoid(zfg_ref[:, N_FPAD + D_MODEL:N_FPAD + 2 * D_MODEL])
        dya = (dmx * g0).astype(BF16)
        dyb = (dmx * g1).astype(BF16)
        dya_ref[...] = dya
        dyb_ref[...] = dyb
        dgl_ref[:, 0:D_MODEL] = ((dmx * ya_ref[...].astype(F32)) * g0 * (1.0 - g0)).astype(BF16)
        dgl_ref[:, D_MODEL:2 * D_MODEL] = ((dmx * yb_ref[...].astype(F32)) * g1 * (1.0 - g1)).astype(BF16)
        daa_ref[...] = _dot_nt(dya, wsa_ref[...]).astype(BF16)
        dab_ref[...] = _dot_nt(dyb, wfo_ref[...]).astype(BF16)

    half = D_MODEL // 2
    return _row_call(body, "ffn_bwd_b", h1.shape[0], tm, [da, dh2, h1, ya, yb, zfg],
                     [w1s, g_mlp, w_mo, w_sa, w_fo],
                     [(D_MODEL, F32), (D_MODEL, BF16), (N_GATE, BF16), (D_MODEL, BF16), (D_MODEL, BF16),
                      (half, BF16), (half, BF16)], [D_MODEL])


def _in_proj_bwd(dz, dh1, x, w_all, g_mix, tm):
    def body(i, ins, consts, outs, accs):
        dz_ref, dh1_ref, x_ref = ins
        w_ref, g_ref = consts
        dx_ref, = outs
        dgx_ref, = accs
        du1 = _dot_nt(dz_ref[...], w_ref[...])
        dh, dg = _rms_bwd(x_ref[...], g_ref[...], du1)
        _acc_rows(dgx_ref, i, dg)
        dx_ref[...] = dh1_ref[...] + dh

    return _row_call(body, "in_proj_bwd", x.shape[0], tm, [dz, dh1, x], [w_all, g_mix],
                     [(D_MODEL, F32)], [D_MODEL])


def _matmul_tn(a, b, name, ts, stack_cols=0):
    n_rows, ka = a.shape
    n = b.shape[1]
    tk = min(ka, 1024)
    tn = 896 if n % 1024 else 1024
    n_stack = tn // stack_cols if stack_cols else 0
    assert ka % tk == 0 and n % tn == 0 and n_rows % ts == 0 and (not stack_cols or tk == ka)
    n_steps = n_rows // ts

    def kern(a_ref, b_ref, o_ref, acc_ref):
        s = pl.program_id(2)

        @pl.when(s == 0)
        def _():
            acc_ref[...] = jnp.zeros_like(acc_ref)
        acc_ref[...] += _dot_tn(a_ref[...].astype(BF16), b_ref[...])

        @pl.when(s == n_steps - 1)
        def _():
            if stack_cols:
                for c in range(n_stack):
                    o_ref[c] = acc_ref[:, c * stack_cols:(c + 1) * stack_cols].astype(BF16)
            else:
                o_ref[...] = acc_ref[...].astype(BF16)

    if stack_cols:
        out_spec = pl.BlockSpec((n_stack, tk, stack_cols), lambda i, j, s: (j, 0, 0))
        out_shape = jax.ShapeDtypeStruct((n // stack_cols, ka, stack_cols), BF16)
    else:
        out_spec = pl.BlockSpec((tk, tn), lambda i, j, s: (i, j))
        out_shape = jax.ShapeDtypeStruct((ka, n), BF16)
    return pl.pallas_call(
        kern, grid=(ka // tk, n // tn, n_steps),
        in_specs=[pl.BlockSpec((ts, tk), lambda i, j, s: (s, i)), pl.BlockSpec((ts, tn), lambda i, j, s: (s, j))],
        out_specs=out_spec, out_shape=out_shape, scratch_shapes=[pltpu.VMEM((tk, tn), F32)], name=name,
        compiler_params=_params(3))(a, b)


SCAN_CHUNK = 512


def _decay_cumsum(f_t, b_col):
    n_tok = f_t.shape[1]
    ch = min(SCAN_CHUNK, n_tok)

    def kern(f_ref, b_ref, c_ref):
        r = lax.broadcasted_iota(jnp.int32, (ch, ch), 0)
        c = lax.broadcasted_iota(jnp.int32, (ch, ch), 1)
        tri = (r <= c).astype(F32)
        carry = jnp.zeros((8, 1), F32)
        for k in range(n_tok // ch):
            xv = f_ref[:, k * ch:(k + 1) * ch] + b_ref[...]
            lf = jnp.minimum(xv, 0.0) - jnp.log(1.0 + jnp.exp(-jnp.abs(xv)))
            cs = jnp.dot(lf, tri, precision=lax.Precision.HIGHEST, preferred_element_type=F32) + carry
            c_ref[:, k * ch:(k + 1) * ch] = cs
            carry = cs[:, ch - 1:ch]

    return pl.pallas_call(kern, out_shape=jax.ShapeDtypeStruct((8, n_tok), F32), name="decay_cumsum",
                          compiler_params=_params(0))(f_t, b_col)


def _decay_bwd(g_t, f_t, b_col):
    n_tok = f_t.shape[1]
    ch = min(SCAN_CHUNK, n_tok)

    def kern(g_ref, f_ref, b_ref, df_ref, db_ref):
        r = lax.broadcasted_iota(jnp.int32, (ch, ch), 0)
        c = lax.broadcasted_iota(jnp.int32, (ch, ch), 1)
        tri = (r >= c).astype(F32)
        carry = jnp.zeros((8, 1), F32)
        tot = jnp.zeros((8, 1), F32)
        for k in reversed(range(n_tok // ch)):
            gv = g_ref[:, k * ch:(k + 1) * ch]
            rc = jnp.dot(gv, tri, precision=lax.Precision.HIGHEST, preferred_element_type=F32) + carry
            carry = rc[:, 0:1]
            xv = f_ref[:, k * ch:(k + 1) * ch] + b_ref[...]
            df = -rc / (1.0 + jnp.exp(xv))
            df_ref[:, k * ch:(k + 1) * ch] = df
            tot = tot + jnp.sum(df, axis=1, keepdims=True)
        db_ref[...] = jnp.broadcast_to(tot, db_ref.shape)

    return pl.pallas_call(kern, out_shape=[jax.ShapeDtypeStruct((8, n_tok), F32),
                                           jax.ShapeDtypeStruct((8, LANES), F32)],
                          name="decay_bwd", compiler_params=_params(0))(g_t, f_t, b_col)


def _swa_band_mask(n):
    row = lax.broadcasted_iota(jnp.int32, (SWA_BLOCK, 2 * SWA_BLOCK), 0) + SWA_BLOCK
    col = lax.broadcasted_iota(jnp.int32, (SWA_BLOCK, 2 * SWA_BLOCK), 1)
    cd = (row >> CHUNK_SHIFT) - (col >> CHUNK_SHIFT)
    first_real = jnp.where(n > 0, 0, SWA_BLOCK)
    ok = (cd >= 0) & (cd <= WINDOW_CHUNKS) & (col >= first_real)
    dist = jnp.abs(row - col).astype(F32)
    return ok, dist


def _swap_halves(t):
    return pltpu.roll(t.astype(F32), HEAD_DIM, axis=1).astype(t.dtype)


def _swa_specs():
    blk = SWA_BLOCK
    q = pl.BlockSpec((blk, 4 * LANES), lambda n: (n, 0))
    kp = pl.BlockSpec((blk, LANES), lambda n: (jnp.maximum(n - 1, 0), 4))
    kc = pl.BlockSpec((blk, LANES), lambda n: (n, 4))
    vp = pl.BlockSpec((blk, LANES), lambda n: (jnp.maximum(n - 1, 0), 5))
    vc = pl.BlockSpec((blk, LANES), lambda n: (n, 5))
    return q, kp, kc, vp, vc


def _swa_fwd(zm, sinks):
    n_tok = zm.shape[0]
    blk = SWA_BLOCK

    def kern(q_ref, kp_ref, kc_ref, vp_ref, vc_ref, sink_ref, o_ref, lse_ref):
        n = pl.program_id(0)
        ok, dist = _swa_band_mask(n)
        k2 = jnp.concatenate([kp_ref[...], kc_ref[...]], axis=0)
        v2 = jnp.concatenate([vp_ref[...], vc_ref[...]], axis=0)
        ksw, vsw = _swap_halves(k2), _swap_halves(v2)
        lane = lax.broadcasted_iota(jnp.int32, (blk, LANES), 1)
        lo = lane < HEAD_DIM
        lse_t = jnp.zeros((blk, LANES), F32)
        for pair in range(SWA_HEADS // 2):
            q2 = q_ref[:, pair * LANES:(pair + 1) * LANES]
            kvh = pair // 2
            outs = []
            for a in range(2):
                h = 2 * pair + a
                qa = jnp.where(lo if a == 0 else ~lo, q2, jnp.zeros_like(q2)) * SCALE
                kx, vx = (k2, v2) if a == kvh else (ksw, vsw)
                s = _dot_nt(qa, kx)
                s = jnp.where(ok, s - (2.0 ** -(h + 1)) * dist, NEG)
                sink = sink_ref[h]
                m = jnp.maximum(jnp.max(s, axis=-1, keepdims=True), sink)
                e = jnp.exp(s - m)
                l = jnp.sum(e, axis=-1, keepdims=True) + jnp.exp(sink - m)
                pn = (e * (1.0 / l)).astype(BF16)
                outs.append(_dot(pn, vx))
                lse_t = jnp.where(lane == h, m + jnp.log(l), lse_t)
            o_ref[:, pair * LANES:(pair + 1) * LANES] = jnp.where(lo, outs[0], outs[1]).astype(BF16)
        lse_ref[...] = lse_t

    q, kp, kc, vp, vc = _swa_specs()
    return pl.pallas_call(
        kern, grid=(n_tok // blk,),
        in_specs=[q, kp, kc, vp, vc, pl.BlockSpec(memory_space=pltpu.SMEM)],
        out_specs=[pl.BlockSpec((blk, 4 * LANES), lambda n: (n, 0)), pl.BlockSpec((blk, LANES), lambda n: (n, 0))],
        out_shape=[jax.ShapeDtypeStruct((n_tok, 4 * LANES), BF16), jax.ShapeDtypeStruct((n_tok, LANES), F32)],
        name="swa_fwd", compiler_params=_params(1))(zm, zm, zm, zm, zm, sinks)


def _swa_bwd(zm, sinks, d_out, out, lse):
    n_tok = zm.shape[0]
    blk = SWA_BLOCK

    def kern(q_ref, kp_ref, kc_ref, vp_ref, vc_ref, do_ref, o_ref, lse_ref, sink_ref,
             dq_ref, dkp_ref, dkc_ref, dvp_ref, dvc_ref, dsk_ref):
        n = pl.program_id(0)

        @pl.when(n == 0)
        def _():
            dsk_ref[...] = jnp.zeros_like(dsk_ref)

        ok, dist = _swa_band_mask(n)
        k2 = jnp.concatenate([kp_ref[...], kc_ref[...]], axis=0)
        v2 = jnp.concatenate([vp_ref[...], vc_ref[...]], axis=0)
        ksw, vsw = _swap_halves(k2), _swap_halves(v2)
        lane = lax.broadcasted_iota(jnp.int32, (blk, LANES), 1)
        lo = lane < HEAD_DIM
        lse_t = lse_ref[...]
        zero = jnp.zeros((2 * blk, LANES), F32)
        dk_same, dk_swap, dv_same, dv_swap = zero, zero, zero, zero
        for pair in range(SWA_HEADS // 2):
            cols = slice(pair * LANES, (pair + 1) * LANES)
            q2, do2, o2 = q_ref[:, cols], do_ref[:, cols], o_ref[:, cols]
            kvh = pair // 2
            dqs = []
            for a in range(2):
                h = 2 * pair + a
                half = lo if a == 0 else ~lo
                qa = jnp.where(half, q2, jnp.zeros_like(q2)) * SCALE
                doa = jnp.where(half, do2, jnp.zeros_like(do2))
                kx, vx = (k2, v2) if a == kvh else (ksw, vsw)
                s = _dot_nt(qa, kx)
                s = jnp.where(ok, s - (2.0 ** -(h + 1)) * dist, NEG)
                lse_h = lse_t[:, h:h + 1]
                prob = jnp.exp(s - lse_h)
                dd = jnp.sum(doa.astype(F32) * o2.astype(F32), axis=-1, keepdims=True)
                dp = _dot_nt(doa, vx)
                ds = (prob * (dp - dd)).astype(BF16)
                p_sink = jnp.exp(sink_ref[h] - lse_h)
                dsk_ref[h:h + 1, :] += jnp.broadcast_to(-jnp.sum(p_sink * dd, axis=0, keepdims=True), (1, LANES))
                dqs.append(_dot(ds, kx) * SCALE)
                dk_c = _dot_tn(ds, qa)
                dv_c = _dot_tn(prob.astype(BF16), doa)
                if a == kvh:
                    dk_same, dv_same = dk_same + dk_c, dv_same + dv_c
                else:
                    dk_swap, dv_swap = dk_swap + dk_c, dv_swap + dv_c
            dq_ref[:, cols] = jnp.where(lo, dqs[0], dqs[1]).astype(BF16)
        dk = dk_same + pltpu.roll(dk_swap, HEAD_DIM, axis=1)
        dv = dv_same + pltpu.roll(dv_swap, HEAD_DIM, axis=1)
        dkp_ref[...] = dk[0:blk]
        dkc_ref[...] = dk[blk:2 * blk]
        dvp_ref[...] = dv[0:blk]
        dvc_ref[...] = dv[blk:2 * blk]

    q, kp, kc, vp, vc = _swa_specs()
    wide = pl.BlockSpec((blk, 4 * LANES), lambda n: (n, 0))
    narrow = pl.BlockSpec((blk, LANES), lambda n: (n, 0))
    part = jax.ShapeDtypeStruct((n_tok, LANES), F32)
    return pl.pallas_call(
        kern, grid=(n_tok // blk,),
        in_specs=[q, kp, kc, vp, vc, wide, wide, narrow, pl.BlockSpec(memory_space=pltpu.SMEM)],
        out_specs=[wide, narrow, narrow, narrow, narrow, pl.BlockSpec((8, LANES), lambda n: (0, 0))],
        out_shape=[jax.ShapeDtypeStruct((n_tok, 4 * LANES), BF16), part, part, part, part,
                   jax.ShapeDtypeStruct((8, LANES), F32)],
        name="swa_bwd", compiler_params=_params(1))(zm, zm, zm, zm, zm, d_out, out, lse, sinks)


def _my_pos():
    return lax.axis_index("x"), lax.axis_index("y"), lax.axis_index("c")


def _peer(k):
    x, y, c = _my_pos()
    px, py, pc = x ^ (k >> 2), y ^ ((k >> 1) & 1), c ^ (k & 1)
    return (px, py, pc), 4 * px + 2 * py + pc


def _gather_copies(x_refs, out_refs, send_sems, recv_sems, local_sems):
    x, y, c = _my_pos()
    my_id = 4 * x + 2 * y + c
    local = [pltpu.make_async_copy(x_refs[w], out_refs[w].at[my_id], local_sems.at[w]) for w in range(len(x_refs))]
    sends, arrivals = [], []
    for k in range(1, N_DEV):
        peer, peer_id = _peer(k)
        for w in range(len(x_refs)):
            sems = dict(send_sem=send_sems.at[7 * w + k - 1], recv_sem=recv_sems.at[7 * w + k - 1],
                        device_id=peer, device_id_type=MESH)
            sends.append(pltpu.make_async_remote_copy(src_ref=x_refs[w], dst_ref=out_refs[w].at[my_id], **sems))
            arrivals.append(pltpu.make_async_remote_copy(src_ref=x_refs[w], dst_ref=out_refs[w].at[peer_id], **sems))
    return local, sends, arrivals


def _scatter_copies(g_refs, part_refs, send_sems, recv_sems, local_sems):
    x, y, c = _my_pos()
    my_id = 4 * x + 2 * y + c
    local = [pltpu.make_async_copy(g_refs[w].at[my_id], part_refs[w].at[0], local_sems.at[w])
             for w in range(len(g_refs))]
    sends, arrivals = [], []
    for k in range(1, N_DEV):
        peer, peer_id = _peer(k)
        for w in range(len(g_refs)):
            sems = dict(send_sem=send_sems.at[7 * w + k - 1], recv_sem=recv_sems.at[7 * w + k - 1],
                        device_id=peer, device_id_type=MESH)
            sends.append(pltpu.make_async_remote_copy(src_ref=g_refs[w].at[peer_id], dst_ref=part_refs[w].at[k], **sems))
            arrivals.append(pltpu.make_async_remote_copy(src_ref=g_refs[w].at[my_id], dst_ref=part_refs[w].at[k], **sems))
    return local, sends, arrivals


def _start_copies(local, sends, arrivals):
    for cp in local + sends:
        cp.start()


def _finish_copies(local, sends, arrivals):
    for cp in arrivals:
        cp.wait_recv()
    for cp in sends:
        cp.wait_send()
    for cp in local:
        cp.wait()


def _exchange_scratch(n_arrays):
    return [pltpu.SemaphoreType.DMA((7 * n_arrays,)), pltpu.SemaphoreType.DMA((7 * n_arrays,)),
            pltpu.SemaphoreType.DMA((n_arrays,))]


class _Ride:
    def __init__(self, arrays, out_shape, copies):
        self.arrays, self.out_shape, self.copies = list(arrays), list(out_shape), copies
        any_spec = pl.BlockSpec(memory_space=pl.ANY)
        self.in_specs = [any_spec] * len(self.arrays)
        self.out_specs = [any_spec] * len(self.arrays)
        self.scratch = _exchange_scratch(len(self.arrays)) if self.arrays else []

    def specs(self):
        return self

    def at_first_step(self, n0, n1, in_refs, out_refs, sems):
        @pl.when(jnp.logical_and(pl.program_id(0) == 0, pl.program_id(1) == 0))
        def _():
            _start_copies(*self.copies(in_refs, out_refs, *sems))

    def at_last_step(self, n0, n1, in_refs, out_refs, sems):
        @pl.when(jnp.logical_and(pl.program_id(0) == n0 - 1, pl.program_id(1) == n1 - 1))
        def _():
            _finish_copies(*self.copies(in_refs, out_refs, *sems))


_NO_RIDE = _Ride([], [], None)


def _gather_ride(shards):
    return _Ride(shards, [jax.ShapeDtypeStruct((N_DEV,) + s.shape, s.dtype) for s in shards], _gather_copies)


def _scatter_ride(grads):
    return _Ride(grads, [jax.ShapeDtypeStruct(g.shape, g.dtype) for g in grads], _scatter_copies)


Q_COL, K_COL, V_COL = 6, 10, 14


def _causal(t, tq, tk):
    row = lax.broadcasted_iota(jnp.int32, (tq, tk), 0)
    col = lax.broadcasted_iota(jnp.int32, (tq, tk), 1)
    return jnp.where(col <= row, t, NEG)


def _lane_tile(stat, width):
    return jnp.tile(stat, (1, width // LANES))


def _fox_fwd(zm, c_pairs, tq, ride=None):
    n_tok = zm.shape[0]
    nq = n_tok // tq
    pairs = [(i, j) for i in range(nq) for j in range(i + 1)]
    ii = np.asarray([p[0] for p in pairs], np.int32)
    jj = np.asarray([p[1] for p in pairs], np.int32)

    n_ride = len(ride.arrays) if ride else 0

    def kern(ii_ref, jj_ref, q_ref, k_ref, v_ref, ck_ref, *more):
        ride_in, (o_ref, ln_ref), ride_out = more[:n_ride], more[n_ride:n_ride + 2], more[n_ride + 2:2 * n_ride + 2]
        qs_ref, m_ref, l_ref, acc_ref = more[2 * n_ride + 2:2 * n_ride + 6]
        step = pl.program_id(1)
        i, j = ii_ref[step], jj_ref[step]
        lane = lax.broadcasted_iota(jnp.int32, (tq, LANES), 1)
        lo = lane < HEAD_DIM
        if ride:
            ride.at_first_step(FOX_HEADS // 2, len(pairs), ride_in, ride_out, more[2 * n_ride + 6:])

        @pl.when(j == 0)
        def _():
            q2 = q_ref[...]
            zq = jnp.zeros_like(q2)
            qs_ref[0] = jnp.where(lo, q2, zq) * SCALE
            qs_ref[1] = jnp.where(lo, zq, q2) * SCALE
            m_ref[...] = jnp.full(m_ref.shape, NEG, F32)
            l_ref[...] = jnp.zeros(l_ref.shape, F32)
            acc_ref[...] = jnp.zeros(acc_ref.shape, F32)

        def update(diag):
            kv = k_ref[...]
            v_ones = jnp.concatenate([v_ref[...], jnp.ones((tq, LANES), BF16)], axis=1)
            for a in range(2):
                t = _dot_nt(qs_ref[a], kv) - ck_ref[a:a + 1, :]
                if diag:
                    t = _causal(t, tq, tq)
                m_old = m_ref[a]
                m_new = jnp.maximum(m_old, jnp.max(t, axis=-1, keepdims=True))
                alpha = jnp.exp(m_old - m_new)
                e = jnp.exp(t - _lane_tile(m_new, tq)).astype(BF16)
                pv = _dot(e, v_ones)
                acc_ref[a] = alpha * acc_ref[a] + pv[:, :LANES]
                l_ref[a] = alpha * l_ref[a] + pv[:, LANES:]
                m_ref[a] = m_new

        @pl.when(j < i)
        def _():
            update(False)

        @pl.when(j == i)
        def _():
            update(True)
            o_ref[...] = jnp.where(lo, acc_ref[0] / l_ref[0], acc_ref[1] / l_ref[1]).astype(BF16)
            ln_ref[:, :LANES] = m_ref[0] + jnp.log(l_ref[0])
            ln_ref[:, LANES:] = m_ref[1] + jnp.log(l_ref[1])

        if ride:
            ride.at_last_step(FOX_HEADS // 2, len(pairs), ride_in, ride_out, more[2 * n_ride + 6:])

    blk = (tq, LANES)
    extra = ride.specs() if ride else _NO_RIDE
    grid_spec = pltpu.PrefetchScalarGridSpec(
        num_scalar_prefetch=2, grid=(FOX_HEADS // 2, len(pairs)),
        in_specs=[pl.BlockSpec(blk, lambda hp, s, ii, jj: (ii[s], Q_COL + hp)),
                  pl.BlockSpec(blk, lambda hp, s, ii, jj: (jj[s], K_COL + hp)),
                  pl.BlockSpec(blk, lambda hp, s, ii, jj: (jj[s], V_COL + hp)),
                  pl.BlockSpec((None, 2, tq), lambda hp, s, ii, jj: (hp, 0, jj[s]))] + extra.in_specs,
        out_specs=[pl.BlockSpec(blk, lambda hp, s, ii, jj: (ii[s], hp)),
                   pl.BlockSpec((tq, 2 * LANES), lambda hp, s, ii, jj: (ii[s], hp))] + extra.out_specs,
        scratch_shapes=[pltpu.VMEM((2, tq, LANES), BF16), pltpu.VMEM((2, tq, LANES), F32),
                        pltpu.VMEM((2, tq, LANES), F32), pltpu.VMEM((2, tq, LANES), F32)] + extra.scratch)
    return pl.pallas_call(
        kern, grid_spec=grid_spec,
        out_shape=[jax.ShapeDtypeStruct((n_tok, 4 * LANES), BF16),
                   jax.ShapeDtypeStruct((n_tok, FOX_HEADS * LANES), F32)] + extra.out_shape,
        name="fox_fwd", compiler_params=_params(2))(ii, jj, zm, zm, zm, c_pairs, *extra.arrays)


def _fox_delta(d_out, out, tm):
    def body(i, ins, consts, outs, accs):
        do_ref, o_ref = ins
        dl_ref, = outs
        lane = lax.broadcasted_iota(jnp.int32, (tm, LANES), 1)
        lo = lane < HEAD_DIM
        for pair in range(FOX_HEADS // 2):
            cols = slice(pair * LANES, (pair + 1) * LANES)
            prod = do_ref[:, cols].astype(F32) * o_ref[:, cols].astype(F32)
            for a in range(2):
                dd = jnp.sum(jnp.where(lo if a == 0 else ~lo, prod, 0.0), axis=-1, keepdims=True)
                h = 2 * pair + a
                dl_ref[:, h * LANES:(h + 1) * LANES] = jnp.broadcast_to(dd, (tm, LANES))

    return _row_call(body, "fox_delta", d_out.shape[0], tm, [d_out, out], [], [(FOX_HEADS * LANES, F32)], [])[0]


def _fox_bwd(zm, c_pairs, d_out, lnorm, delta, tq, ride=None):
    n_tok = zm.shape[0]
    nq = n_tok // tq
    pairs = [(i, j) for i in range(nq) for j in range(i + 1)]
    ii = np.asarray([p[0] for p in pairs], np.int32)
    jj = np.asarray([p[1] for p in pairs], np.int32)

    n_ride = len(ride.arrays) if ride else 0

    def kern(ii_ref, jj_ref, q_ref, k_ref, v_ref, ck_ref, do_ref, ln_ref, dl_ref, *more):
        ride_in, ride_out = more[:n_ride], more[n_ride + 5:2 * n_ride + 5]
        dq_ref, dk_ref, dv_ref, cs_ref, rs_ref = more[n_ride:n_ride + 5]
        qs_ref, qo_ref, dos_ref, dq_acc = more[2 * n_ride + 5:2 * n_ride + 9]
        step = pl.program_id(1)
        i, j = ii_ref[step], jj_ref[step]
        lane = lax.broadcasted_iota(jnp.int32, (tq, LANES), 1)
        lo = lane < HEAD_DIM
        if ride:
            ride.at_first_step(FOX_HEADS // 2, len(pairs), ride_in, ride_out, more[2 * n_ride + 9:])

        @pl.when(step == 0)
        def _():
            dk_ref[...] = jnp.zeros_like(dk_ref)
            dv_ref[...] = jnp.zeros_like(dv_ref)
            cs_ref[...] = jnp.zeros_like(cs_ref)

        @pl.when(j == 0)
        def _():
            q2, do2 = q_ref[...], do_ref[...]
            zq = jnp.zeros_like(q2)
            ones = jnp.ones((tq, LANES), BF16)
            for a in range(2):
                half = lo if a == 0 else ~lo
                qa = jnp.where(half, q2, zq) * SCALE
                qs_ref[a] = qa
                qo_ref[a] = jnp.concatenate([qa, ones], axis=1)
                dos_ref[a] = jnp.where(half, do2, zq)
            dq_acc[...] = jnp.zeros(dq_acc.shape, F32)

        def update(diag):
            kv, vv = k_ref[...], v_ref[...]
            k_ones = jnp.concatenate([kv, jnp.ones((tq, LANES), BF16)], axis=1)
            dks, dv = [], None
            for a in range(2):
                t = _dot_nt(qs_ref[a], kv) - ck_ref[a:a + 1, :]
                if diag:
                    t = _causal(t, tq, tq)
                prob = jnp.exp(t - _lane_tile(ln_ref[:, a * LANES:(a + 1) * LANES], tq))
                dp = _dot_nt(dos_ref[a], vv)
                ds = (prob * (dp - _lane_tile(dl_ref[:, a * LANES:(a + 1) * LANES], tq))).astype(BF16)
                dq_acc[a] += _dot(ds, k_ones)
                dks.append(_dot_tn(ds, qo_ref[a]))
                dv_a = _dot_tn(prob.astype(BF16), dos_ref[a])
                dv = dv_a if dv is None else dv + dv_a
            rows = pl.ds(pl.multiple_of(j * tq, tq), tq)
            dk_ref[rows, :] += dks[0][:, :LANES] + dks[1][:, :LANES]
            cs_ref[rows, :] += jnp.where(lo, dks[0][:, LANES:], dks[1][:, LANES:])
            dv_ref[rows, :] += dv

        @pl.when(j < i)
        def _():
            update(False)

        @pl.when(j == i)
        def _():
            update(True)
            dq_ref[...] = jnp.where(lo, dq_acc[0, :, :LANES], dq_acc[1, :, :LANES]) * SCALE
            rs_ref[...] = jnp.where(lo, dq_acc[0, :, LANES:], dq_acc[1, :, LANES:])

        if ride:
            ride.at_last_step(FOX_HEADS // 2, len(pairs), ride_in, ride_out, more[2 * n_ride + 9:])

    blk = (tq, LANES)
    extra = ride.specs() if ride else _NO_RIDE
    by_i = lambda col: (lambda hp, s, ii, jj: (ii[s], col + hp))
    by_j = lambda col: (lambda hp, s, ii, jj: (jj[s], col + hp))
    stat = pl.BlockSpec((tq, 2 * LANES), by_i(0))
    whole = pl.BlockSpec((n_tok, LANES), lambda hp, s, ii, jj: (0, hp))
    grid_spec = pltpu.PrefetchScalarGridSpec(
        num_scalar_prefetch=2, grid=(FOX_HEADS // 2, len(pairs)),
        in_specs=[pl.BlockSpec(blk, by_i(Q_COL)), pl.BlockSpec(blk, by_j(K_COL)), pl.BlockSpec(blk, by_j(V_COL)),
                  pl.BlockSpec((None, 2, tq), lambda hp, s, ii, jj: (hp, 0, jj[s])),
                  pl.BlockSpec(blk, by_i(0)), stat, stat] + extra.in_specs,
        out_specs=[pl.BlockSpec(blk, by_i(0)), whole, whole, whole, pl.BlockSpec(blk, by_i(0))] + extra.out_specs,
        scratch_shapes=[pltpu.VMEM((2, tq, LANES), BF16), pltpu.VMEM((2, tq, 2 * LANES), BF16),
                        pltpu.VMEM((2, tq, LANES), BF16), pltpu.VMEM((2, tq, 2 * LANES), F32)] + extra.scratch)
    wide = jax.ShapeDtypeStruct((n_tok, 4 * LANES), F32)
    return pl.pallas_call(
        kern, grid_spec=grid_spec, out_shape=[wide] * 5 + extra.out_shape,
        name="fox_bwd", compiler_params=_params(2))(ii, jj, zm, zm, zm, c_pairs, d_out, lnorm, delta, *extra.arrays)


def _all_gather(shards):
    n_w = len(shards)

    def kern(*refs):
        x_refs, out_refs = refs[:n_w], refs[n_w:2 * n_w]
        send_sems, recv_sems, local_sems = refs[2 * n_w:]
        x, y, c = _my_pos()
        me, sibling = (x, y, c), (x, y, 1 - c)
        chips = [(1 - x, y), (x, 1 - y), (1 - x, 1 - y)]

        def slot(w, px, py, pc):
            return out_refs[w].at[4 * px + 2 * py + pc]

        def copy(w, k, block, to, src=None):
            return pltpu.make_async_remote_copy(
                src_ref=slot(w, *block) if src is None else src, dst_ref=slot(w, *block),
                send_sem=send_sems.at[7 * w + k], recv_sem=recv_sems.at[7 * w + k], device_id=to, device_id_type=MESH)

        local, started = [], []
        for w in range(n_w):
            mine = pltpu.make_async_copy(x_refs[w], slot(w, *me), local_sems.at[w])
            mine.start()
            local.append(mine)
            first = [copy(w, 0, me, sibling, src=x_refs[w])]
            first += [copy(w, 1 + k, me, (*chip, c), src=x_refs[w]) for k, chip in enumerate(chips)]
            for cp in first:
                cp.start()
            started += first
        for k, chip in enumerate(chips):
            for w in range(n_w):
                copy(w, 1 + k, (*chip, c), me).wait_recv()
                passed = copy(w, 4 + k, (*chip, c), sibling)
                passed.start()
                started.append(passed)
        for w in range(n_w):
            copy(w, 0, sibling, me).wait_recv()
            for k, chip in enumerate(chips):
                copy(w, 4 + k, (*chip, 1 - c), me).wait_recv()
        for cp in started:
            cp.wait_send()
        for cp in local:
            cp.wait()

    any_spec = pl.BlockSpec(memory_space=pl.ANY)
    return pl.pallas_call(
        kern, out_shape=[jax.ShapeDtypeStruct((N_DEV,) + s.shape, s.dtype) for s in shards],
        in_specs=[any_spec] * n_w, out_specs=[any_spec] * n_w,
        scratch_shapes=[pltpu.SemaphoreType.DMA((7 * n_w,)), pltpu.SemaphoreType.DMA((7 * n_w,)),
                        pltpu.SemaphoreType.DMA((n_w,))],
        name="weight_all_gather")(*shards)


def _grad_exchange(grads, small):
    n_w = len(grads)

    def kern(*refs):
        g_refs, s_ref = refs[:n_w], refs[n_w]
        part_refs, sall_ref = refs[n_w + 1:2 * n_w + 1], refs[2 * n_w + 1]
        big = _scatter_copies(g_refs, part_refs, *refs[2 * n_w + 2:2 * n_w + 5])
        sml = _gather_copies([s_ref], [sall_ref], *refs[2 * n_w + 5:])
        _start_copies(*sml)
        _start_copies(*big)
        _finish_copies(*sml)
        _finish_copies(*big)

    any_spec = pl.BlockSpec(memory_space=pl.ANY)
    return pl.pallas_call(
        kern, out_shape=[jax.ShapeDtypeStruct(g.shape, g.dtype) for g in grads]
        + [jax.ShapeDtypeStruct((N_DEV,) + small.shape, small.dtype)],
        in_specs=[any_spec] * (n_w + 1), out_specs=[any_spec] * (n_w + 1),
        scratch_shapes=_exchange_scratch(n_w) + _exchange_scratch(1), name="grad_exchange")(*grads, small)


ADAMW_BLOCK_BYTES = 2 * 1024 * 1024


def _adamw(parts, w, m, v, name):
    n_parts, n_rows, n_cols = parts.shape
    limit = max(8, ADAMW_BLOCK_BYTES // (n_parts * n_cols * parts.dtype.itemsize))
    tr = max(t for t in range(8, n_rows + 1, 8) if n_rows % t == 0 and t <= limit)

    def kern(p_ref, w_ref, m_ref, v_ref, g_out, d_out, m_out, v_out):
        g = p_ref[0].astype(F32)
        for k in range(1, n_parts):
            g = g + p_ref[k].astype(F32)
        m_new = ADAM_B1 * m_ref[...] + (1.0 - ADAM_B1) * g
        v_new = ADAM_B2 * v_ref[...] + (1.0 - ADAM_B2) * jnp.square(g)
        m_hat = m_new / (1.0 - ADAM_B1 ** ADAM_STEP)
        v_hat = v_new / (1.0 - ADAM_B2 ** ADAM_STEP)
        g_out[...] = g
        d_out[...] = -ADAM_LR * (m_hat / (jnp.sqrt(v_hat) + ADAM_EPS) + ADAM_WD * w_ref[...])
        m_out[...] = m_new
        v_out[...] = v_new

    row = pl.BlockSpec((tr, n_cols), lambda i: (i, 0))
    out = jax.ShapeDtypeStruct((n_rows, n_cols), F32)
    return pl.pallas_call(
        kern, grid=(n_rows // tr,),
        in_specs=[pl.BlockSpec((n_parts, tr, n_cols), lambda i: (0, i, 0)), row, row, row],
        out_specs=[row, row, row, row], out_shape=[out, out, out, out], name=name,
        compiler_params=_params(1))(parts, w, m, v)


SHARDED = {
    "w_in": ((D_MODEL, D_IN), 1), "w_br_swa": ((512, D_MODEL), 1), "w_br_fox": ((512, D_MODEL), 1),
    "w_mix_out": ((D_MODEL, D_MODEL), 0), "w_ff1": ((D_MODEL, D_FF), 1), "w_ff2": ((D_FF, D_MODEL), 0),
    "w_ple_gate": ((D_MODEL, D_MODEL), 0), "w_ple_proj": ((PLE_DIM, D_MODEL), 1),
}
W_IN_SHARD = D_IN // N_DEV
W_IN_PAD = 640
SMALL = ("g_mix", "g_mlp", "g_ple", "g_final", "b_forget", "swa_sinks")
SMALL_COLS = 1024


def _wire_shard(name, a):
    a = a.reshape(a.shape[-2:])
    return jnp.pad(a, ((0, 0), (0, W_IN_PAD - W_IN_SHARD))) if name == "w_in" else a


def _from_wire(name, a):
    return (a[:, :W_IN_SHARD] if name == "w_in" else a)[None]


def _cols_to_full(stacked):
    return jnp.concatenate([stacked[d] for d in range(N_DEV)], axis=1)


def _full_to_cols(full):
    n = full.shape[1] // N_DEV
    return jnp.stack([full[:, d * n:(d + 1) * n] for d in range(N_DEV)])


def _w_all_from_wire(stacked):
    w_in = jnp.concatenate([stacked[d][:, :W_IN_SHARD] for d in range(N_DEV)], axis=1)
    fpad = jnp.zeros((D_MODEL, N_FPAD - FOX_HEADS), stacked.dtype)
    return jnp.concatenate([w_in[:, :N_MAIN + FOX_HEADS], fpad, w_in[:, N_MAIN + FOX_HEADS:]], axis=1)


def _dw_in_to_wire(dw_all):
    dw_in = jnp.concatenate([dw_all[:, :N_MAIN + FOX_HEADS], dw_all[:, N_MAIN + N_FPAD:]], axis=1)
    pad = jnp.zeros((D_MODEL, W_IN_PAD - W_IN_SHARD), dw_all.dtype)
    return jnp.stack([jnp.concatenate([dw_in[:, d * W_IN_SHARD:(d + 1) * W_IN_SHARD], pad], axis=1)
                      for d in range(N_DEV)])


def _pack_small(vals):
    rows = [jnp.pad(vals[n].reshape(-1), (0, SMALL_COLS - vals[n].size)) for n in SMALL]
    rows += [jnp.zeros((SMALL_COLS,), F32)] * (8 - len(SMALL))
    return jnp.stack(rows)


def _unpack_small(slab, like):
    return {n: slab[r, :like[n].size].reshape(like[n].shape) for r, n in enumerate(SMALL)}


def _local_step(x, p, tgt, w, small, tm, tq, ts, late_shards=None):
    n_tok = x.shape[0]
    row = lambda v: v.reshape(1, -1)
    g_mix, g_mlp, g_ple, g_fin = row(small["g_mix"]), row(small["g_mlp"]), row(small["g_ple"]), row(small["g_final"])
    sinks = small["swa_sinks"].reshape(-1)
    b_col = small["b_forget"].reshape(FOX_HEADS, 1)

    u1, zm, zfg = _in_proj(x, g_mix, w["w_all"], tm)
    f_t = zfg[:, :FOX_HEADS].T
    c_pairs = _decay_cumsum(f_t, b_col).reshape(FOX_HEADS // 2, 2, n_tok)
    attn_a, lse_a = _swa_fwd(zm, sinks)
    if late_shards is None:
        attn_b, ln_b = _fox_fwd(zm, c_pairs, tq)
    else:
        attn_b, ln_b, *late = _fox_fwd(zm, c_pairs, tq, _gather_ride(list(late_shards.values())))
        w = {**w, **_gathered_to_local(dict(zip(late_shards, late)))}
    ya, yb, mixed, h1, u2 = _mix_fwd(attn_a, attn_b, zfg, x, w["w_br_swa"], w["w_br_fox"], w["w_mix_out"], g_mlp, tm)
    a, r, h2 = _ffn_fwd(u2, h1, w["w_ff1"], w["w_ff2"], tm // 2)
    dh3, dlg, dpp, u3, loss_acc, dgf = _head_fwd_bwd(h2, p, tgt, g_ple, w["w_ple_gate"], w["w_ple_proj"], g_fin, tm)

    dh2, dh2b, da, dgp = _ffn_bwd_a(dlg, dh3, h2, a, w["w_ple_gate"], g_ple, w["w_ff2"], tm // 2)
    dh1, dh1b, dgl, dya, dyb, daa, dab, dgm = _ffn_bwd_b(
        da, dh2, h1, ya, yb, zfg, w["w_ff1"], g_mlp, w["w_mix_out"], w["w_br_swa"], w["w_br_fox"], tm // 2)
    dq_a, dkp, dkc, dvp, dvc, dsk = _swa_bwd(zm, sinks, daa, attn_a, lse_a)
    delta_b = _fox_delta(dab, attn_b, tm)
    dw = {
        "w_br_swa": _matmul_tn(attn_a, dya, "dw_br_swa", ts),
        "w_br_fox": _matmul_tn(attn_b, dyb, "dw_br_fox", ts),
        "w_mix_out": _matmul_tn(mixed, dh1b, "dw_mix_out", ts),
        "w_ff1": _matmul_tn(u2, da, "dw_ff1", ts, stack_cols=D_FF // N_DEV),
        "w_ff2": _matmul_tn(r, dh2b, "dw_ff2", ts),
        "w_ple_gate": _matmul_tn(u3, dlg, "dw_ple_gate", ts),
        "w_ple_proj": _matmul_tn(p, dpp, "dw_ple_proj", ts),
    }
    if late_shards is None:
        dq_b, dk_b, dv_b, cs, rs = _fox_bwd(zm, c_pairs, dab, ln_b, delta_b, tq)
        late_parts = None
    else:
        wire = _local_to_wire(dw)
        dq_b, dk_b, dv_b, cs, rs, *parts = _fox_bwd(zm, c_pairs, dab, ln_b, delta_b, tq,
                                                    _scatter_ride([wire[n] for n in late_shards]))
        late_parts = dict(zip(late_shards, parts))

    up = lambda t: jnp.concatenate([t[SWA_BLOCK:], jnp.zeros((SWA_BLOCK, LANES), F32)], axis=0)
    dk_a, dv_a = dkc + up(dkp), dvc + up(dvp)
    per_head = lambda t: t.reshape(n_tok, FOX_HEADS, HEAD_DIM)[:, :, 0].T
    df_t, db = _decay_bwd(per_head(cs) - per_head(rs), f_t, b_col)
    df = jnp.pad(df_t.T, ((0, 0), (0, N_FPAD - FOX_HEADS)))
    dz = jnp.concatenate([dq_a, dk_a.astype(BF16), dv_a.astype(BF16), dq_b.astype(BF16), dk_b.astype(BF16), dv_b.astype(BF16),
                          df.astype(BF16), dgl], axis=1)
    dx, dgx = _in_proj_bwd(dz, dh1, x, w["w_all"], g_mix, tm)

    dw["w_all"] = _matmul_tn(u1, dz, "dw_in", ts)
    dsmall = {"g_mix": dgx[0], "g_mlp": dgm[0], "g_ple": dgp[0], "g_final": dgf[0],
              "b_forget": db[:, 0], "swa_sinks": dsk[:, 0]}
    return loss_acc[0, 0], dx, dw, dsmall, late_parts


_ROWS = lambda t: t.reshape(-1, t.shape[-1])
_BY_ROWS = lambda t: t.reshape(N_DEV, t.shape[0] // N_DEV, t.shape[1])
_SAME = lambda t: t
LOCAL_LAYOUT = {
    "w_in": ("w_all", _w_all_from_wire, _dw_in_to_wire), "w_br_swa": ("w_br_swa", _cols_to_full, _full_to_cols),
    "w_br_fox": ("w_br_fox", _cols_to_full, _full_to_cols), "w_mix_out": ("w_mix_out", _ROWS, _BY_ROWS),
    "w_ff1": ("w_ff1", _SAME, _SAME), "w_ff2": ("w_ff2", _SAME, _BY_ROWS),
    "w_ple_gate": ("w_ple_gate", _ROWS, _BY_ROWS), "w_ple_proj": ("w_ple_proj", _cols_to_full, _full_to_cols),
}


def _gathered_to_local(g):
    return {LOCAL_LAYOUT[n][0]: LOCAL_LAYOUT[n][1](t) for n, t in g.items()}


def _local_to_wire(dw):
    names = {local: n for n, (local, _, _) in LOCAL_LAYOUT.items()}
    return {names[local]: LOCAL_LAYOUT[names[local]][2](t) for local, t in dw.items()}


def kernel(x, p, g_mix, w_in, b_forget, swa_sinks, w_br_swa, w_br_fox, w_mix_out, g_mlp, w_ff1, w_ff2, g_ple, w_ple_gate, w_ple_proj, g_final, loss_target, m_g_mix, m_w_in, m_b_forget, m_swa_sinks, m_w_br_swa, m_w_br_fox, m_w_mix_out, m_g_mlp, m_w_ff1, m_w_ff2, m_g_ple, m_w_ple_gate, m_w_ple_proj, m_g_final, v_g_mix, v_w_in, v_b_forget, v_swa_sinks, v_w_br_swa, v_w_br_fox, v_w_mix_out, v_g_mlp, v_w_ff1, v_w_ff2, v_g_ple, v_w_ple_gate, v_w_ple_proj, v_g_final):
    given = dict(g_mix=g_mix, w_in=w_in, b_forget=b_forget, swa_sinks=swa_sinks, w_br_swa=w_br_swa, w_br_fox=w_br_fox,
                 w_mix_out=w_mix_out, g_mlp=g_mlp, w_ff1=w_ff1, w_ff2=w_ff2, g_ple=g_ple, w_ple_gate=w_ple_gate,
                 w_ple_proj=w_ple_proj, g_final=g_final)
    mom = dict(g_mix=m_g_mix, w_in=m_w_in, b_forget=m_b_forget, swa_sinks=m_swa_sinks, w_br_swa=m_w_br_swa,
               w_br_fox=m_w_br_fox, w_mix_out=m_w_mix_out, g_mlp=m_g_mlp, w_ff1=m_w_ff1, w_ff2=m_w_ff2, g_ple=m_g_ple,
               w_ple_gate=m_w_ple_gate, w_ple_proj=m_w_ple_proj, g_final=m_g_final)
    vel = dict(g_mix=v_g_mix, w_in=v_w_in, b_forget=v_b_forget, swa_sinks=v_swa_sinks, w_br_swa=v_w_br_swa,
               w_br_fox=v_w_br_fox, w_mix_out=v_w_mix_out, g_mlp=v_g_mlp, w_ff1=v_w_ff1, w_ff2=v_w_ff2, g_ple=v_g_ple,
               w_ple_gate=v_w_ple_gate, w_ple_proj=v_w_ple_proj, g_final=v_g_final)
    names = list(given)
    sharded = list(SHARDED)

    w_wire = {n: _wire_shard(n, given[n]) for n in sharded}
    late = [n for n in sharded if n != "w_in"]
    gathered = _all_gather([w_wire["w_in"].astype(BF16)])
    local_w = _gathered_to_local({"w_in": gathered[0]})
    small = {n: given[n].reshape(-1) for n in SMALL}

    n_tok = x.shape[1]
    tile = min(512, n_tok // 2)
    loss_part, dx, dw, dsmall, parts = _local_step(
        x[0], p[0, 0], loss_target[0], local_w, small, tm=tile, tq=tile, ts=tile,
        late_shards={n: w_wire[n].astype(BF16) for n in late})
    loss = lax.psum(loss_part, AXES)

    parts["w_in"], small_all = _grad_exchange([_dw_in_to_wire(dw["w_all"])], _pack_small(dsmall))

    res = {}
    for n in sharded:
        part = parts[n]
        flat = part.reshape(N_DEV, -1, part.shape[-1])
        outs = _adamw(flat, w_wire[n], _wire_shard(n, mom[n]), _wire_shard(n, vel[n]), "adamw_" + n)
        res[n] = [_from_wire(n, o) for o in outs]
    outs_s = _adamw(small_all, _pack_small(small), _pack_small({n: mom[n] for n in SMALL}),
                    _pack_small({n: vel[n] for n in SMALL}), "adamw_small")
    small_res = [_unpack_small(o, given) for o in outs_s]

    groups = [[res[n][k] if n in res else small_res[k][n] for n in names] for k in range(4)]
    return (loss, dx[None], *groups[0], *groups[1], *groups[2], *groups[3])
```

```python
import numpy as np
import jax
import jax.numpy as jnp
from jax import lax
from jax.experimental import pallas as pl
from jax.experimental.pallas import tpu as pltpu

F32 = jnp.float32
BF16 = jnp.bfloat16

D_MODEL = 1024
HEAD_DIM = 64
SWA_HEADS = 8
FOX_HEADS = 8
CHUNK_SHIFT = 6
SWA_BLOCK = 128
WINDOW_CHUNKS = 2
D_FF = 4096
PLE_DIM = 256
RMS_EPS = 1e-6
N_MAIN = 2304
N_FPAD = 128
N_GATE = 2048
N_ALL = N_MAIN + N_FPAD + N_GATE
D_IN = N_MAIN + FOX_HEADS + N_GATE
SCALE = HEAD_DIM ** -0.5
NEG = -1e30

ADAM_LR = 0.001
ADAM_B1 = 0.9
ADAM_B2 = 0.999
ADAM_EPS = 1e-08
ADAM_WD = 0.01
ADAM_STEP = 10

N_DEV = 8
LANES = 128
V7X_VMEM_BYTES = 64 * 1024 * 1024
VMEM_LIMIT = V7X_VMEM_BYTES * 3 // 4
FOX_BWD_VMEM = V7X_VMEM_BYTES * 7 // 8
MESH = pl.DeviceIdType.MESH
AXES = ("x", "y", "c")

_NT = (((1,), (1,)), ((), ()))
_TN = (((0,), (0,)), ((), ()))


def _params(n_grid, vmem_limit=VMEM_LIMIT):
    return pltpu.CompilerParams(dimension_semantics=("arbitrary",) * n_grid, vmem_limit_bytes=vmem_limit)


def _chunks(n, step):
    return [(s, min(step, n - s)) for s in range(0, n, step)]


def _sigmoid(x):
    return 1.0 / (1.0 + jnp.exp(-x))


def _dot(a, b):
    return jnp.dot(a, b, preferred_element_type=F32)


def _dot_nt(a, b):
    return lax.dot_general(a, b, _NT, preferred_element_type=F32)


def _dot_tn(a, b):
    return lax.dot_general(a, b, _TN, preferred_element_type=F32)


def _rms(h):
    return lax.rsqrt(jnp.mean(h * h, axis=-1, keepdims=True) + RMS_EPS)


def _rms_bwd(h, g, du):
    rs = _rms(h)
    n = h * rs
    dn = du * g
    dh = rs * (dn - n * jnp.mean(dn * n, axis=-1, keepdims=True))
    return dh, jnp.sum(du * n, axis=0, keepdims=True)


def _acc_rows(ref, i, row):
    @pl.when(i == 0)
    def _():
        ref[...] = jnp.zeros_like(ref)
    ref[...] += jnp.broadcast_to(row, ref.shape)


def _row_call(body, name, n_rows, tm, row_ins, const_ins, row_outs, acc_outs):
    n_ri, n_ci, n_ro = len(row_ins), len(const_ins), len(row_outs)

    def kern(*refs):
        i = pl.program_id(0)
        body(i, refs[:n_ri], refs[n_ri:n_ri + n_ci], refs[n_ri + n_ci:n_ri + n_ci + n_ro],
             refs[n_ri + n_ci + n_ro:])

    def whole(a):
        zeros = (0,) * a.ndim
        return pl.BlockSpec(a.shape, lambda i: zeros, pipeline_mode=pl.Buffered(1))

    in_specs = [pl.BlockSpec((tm, a.shape[1]), lambda i: (i, 0)) for a in row_ins]
    in_specs += [whole(a) for a in const_ins]
    out_specs = [pl.BlockSpec((tm, c), lambda i: (i, 0)) for c, _ in row_outs]
    out_specs += [pl.BlockSpec((8, c), lambda i: (0, 0)) for c in acc_outs]
    out_shape = [jax.ShapeDtypeStruct((n_rows, c), dt) for c, dt in row_outs]
    out_shape += [jax.ShapeDtypeStruct((8, c), F32) for c in acc_outs]
    return pl.pallas_call(kern, grid=(n_rows // tm,), in_specs=in_specs, out_specs=out_specs,
                          out_shape=out_shape, name=name, compiler_params=_params(1))(*row_ins, *const_ins)


def _in_proj(x, g_mix, w_all, tm):
    def body(i, ins, consts, outs, accs):
        x_ref, = ins
        g_ref, w_ref = consts
        u_ref, zm_ref, zfg_ref, zf_ref = outs
        xv = x_ref[...]
        u = ((xv * _rms(xv)) * g_ref[...]).astype(BF16)
        u_ref[...] = u
        for s, n in _chunks(N_MAIN, 768):
            zm_ref[:, s:s + n] = _dot(u, w_ref[:, s:s + n]).astype(BF16)
        for s, n in _chunks(N_FPAD + N_GATE, 512):
            zfg_ref[:, s:s + n] = _dot(u, w_ref[:, N_MAIN + s:N_MAIN + s + n])
        zf_ref[...] = zfg_ref[:, :N_FPAD]

    return _row_call(body, "in_proj", x.shape[0], tm, [x], [g_mix, w_all],
                     [(D_MODEL, BF16), (N_MAIN, BF16), (N_FPAD + N_GATE, F32), (N_FPAD, F32)], [])


def _mix_fwd(attn_a, attn_b, zfg, x, w_sa, w_fo, w_mo, g_mlp, tm):
    def body(i, ins, consts, outs, accs):
        aa_ref, ab_ref, zfg_ref, x_ref = ins
        wsa_ref, wfo_ref, wmo_ref, g_ref = consts
        ya_ref, yb_ref, mx_ref, h1_ref, u2_ref = outs
        ya = _dot(aa_ref[...], wsa_ref[...])
        yb = _dot(ab_ref[...], wfo_ref[...])
        g0 = _sigmoid(zfg_ref[:, N_FPAD:N_FPAD + D_MODEL])
        g1 = _sigmoid(zfg_ref[:, N_FPAD + D_MODEL:N_FPAD + 2 * D_MODEL])
        mixed = (g0 * ya + g1 * yb).astype(BF16)
        ya_ref[...] = ya.astype(BF16)
        yb_ref[...] = yb.astype(BF16)
        mx_ref[...] = mixed
        h1 = x_ref[...] + _dot(mixed, wmo_ref[...])
        h1_ref[...] = h1
        u2_ref[...] = ((h1 * _rms(h1)) * g_ref[...]).astype(BF16)

    return _row_call(body, "mix_fwd", x.shape[0], tm, [attn_a, attn_b, zfg, x], [w_sa, w_fo, w_mo, g_mlp],
                     [(D_MODEL, BF16), (D_MODEL, BF16), (D_MODEL, BF16), (D_MODEL, F32), (D_MODEL, BF16)], [])


def _ffn_fwd(u2, h1, w1s, w2s, tm):
    ch = D_FF // N_DEV

    def body(i, ins, consts, outs, accs):
        u_ref, h1_ref = ins
        w1_ref, w2_ref = consts
        a_ref, r_ref, h2_ref = outs
        u = u_ref[...]
        acc = h1_ref[...]
        for c in range(N_DEV):
            a = _dot(u, w1_ref[c])
            a_ref[:, c * ch:(c + 1) * ch] = a.astype(BF16)
            r = jnp.square(jnp.maximum(a, 0.0)).astype(BF16)
            r_ref[:, c * ch:(c + 1) * ch] = r
            acc = acc + _dot(r, w2_ref[c])
        h2_ref[...] = acc

    return _row_call(body, "ffn_fwd", u2.shape[0], tm, [u2, h1], [w1s, w2s],
                     [(D_FF, BF16), (D_FF, BF16), (D_MODEL, F32)], [])


def _head_fwd_bwd(h2, p, tgt, g_ple, w_pg, w_pp, g_fin, tm):
    def body(i, ins, consts, outs, accs):
        h2_ref, p_ref, t_ref = ins
        gp_ref, wpg_ref, wpp_ref, gf_ref = consts
        dh3_ref, dlg_ref, dpp_ref, u3_ref = outs
        loss_ref, dgf_ref = accs
        h2 = h2_ref[...]
        u3 = ((h2 * _rms(h2)) * gp_ref[...]).astype(BF16)
        u3_ref[...] = u3
        pg = _sigmoid(_dot(u3, wpg_ref[...]))
        pp = _dot(p_ref[...].astype(BF16), wpp_ref[...])
        h3 = h2 + pg * pp
        rs3 = _rms(h3)
        n3 = h3 * rs3
        gf = gf_ref[...]
        err = n3 * gf - t_ref[...]
        row_loss = 0.5 * jnp.mean(err * err, axis=-1, keepdims=True)
        _acc_rows(loss_ref, i, jnp.broadcast_to(jnp.sum(row_loss, axis=0, keepdims=True), (1, LANES)))
        dy = err * (1.0 / D_MODEL)
        _acc_rows(dgf_ref, i, jnp.sum(dy * n3, axis=0, keepdims=True))
        dn = dy * gf
        dh3 = rs3 * (dn - n3 * jnp.mean(dn * n3, axis=-1, keepdims=True))
        dh3_ref[...] = dh3
        dpp_ref[...] = (dh3 * pg).astype(BF16)
        dlg_ref[...] = ((dh3 * pp) * pg * (1.0 - pg)).astype(BF16)

    return _row_call(body, "head_fwd_bwd", h2.shape[0], tm, [h2, p, tgt], [g_ple, w_pg, w_pp, g_fin],
                     [(D_MODEL, F32), (D_MODEL, BF16), (D_MODEL, BF16), (D_MODEL, BF16)], [LANES, D_MODEL])


def _ffn_bwd_a(dlg, dh3, h2, a, w_pg, g_ple, w2s, tm):
    ch = D_FF // N_DEV

    def body(i, ins, consts, outs, accs):
        dlg_ref, dh3_ref, h2_ref, a_ref = ins
        wpg_ref, gp_ref, w2_ref = consts
        dh2_ref, dh2b_ref, da_ref = outs
        dgp_ref, = accs
        du3 = _dot_nt(dlg_ref[...], wpg_ref[...])
        dh, dg = _rms_bwd(h2_ref[...], gp_ref[...], du3)
        _acc_rows(dgp_ref, i, dg)
        dh2 = dh3_ref[...] + dh
        dh2_ref[...] = dh2
        dh2b = dh2.astype(BF16)
        dh2b_ref[...] = dh2b
        for c in range(N_DEV):
            dr = _dot_nt(dh2b, w2_ref[c])
            av = a_ref[:, c * ch:(c + 1) * ch].astype(F32)
            da_ref[:, c * ch:(c + 1) * ch] = (dr * (2.0 * jnp.maximum(av, 0.0))).astype(BF16)

    return _row_call(body, "ffn_bwd_a", h2.shape[0], tm, [dlg, dh3, h2, a], [w_pg, g_ple, w2s],
                     [(D_MODEL, F32), (D_MODEL, BF16), (D_FF, BF16)], [D_MODEL])


def _ffn_bwd_b(da, dh2, h1, ya, yb, zfg, w1s, g_mlp, w_mo, w_sa, w_fo, tm):
    ch = D_FF // N_DEV

    def body(i, ins, consts, outs, accs):
        da_ref, dh2_ref, h1_ref, ya_ref, yb_ref, zfg_ref = ins
        w1_ref, gm_ref, wmo_ref, wsa_ref, wfo_ref = consts
        dh1_ref, dh1b_ref, dgl_ref, dya_ref, dyb_ref, daa_ref, dab_ref = outs
        dgm_ref, = accs
        du2 = _dot_nt(da_ref[:, 0:ch], w1_ref[0])
        for c in range(1, N_DEV):
            du2 = du2 + _dot_nt(da_ref[:, c * ch:(c + 1) * ch], w1_ref[c])
        dh, dg = _rms_bwd(h1_ref[...], gm_ref[...], du2)
        _acc_rows(dgm_ref, i, dg)
        dh1 = dh2_ref[...] + dh
        dh1_ref[...] = dh1
        dh1b = dh1.astype(BF16)
        dh1b_ref[...] = dh1b
        dmx = _dot_nt(dh1b, wmo_ref[...])
        g0 = _sigmoid(zfg_ref[:, N_FPAD:N_FPAD + D_MODEL])
        g1 = _sigmoid(zfg_ref[:, N_FPAD + D_MODEL:N_FPAD + 2 * D_MODEL])
        dya = (dmx * g0).astype(BF16)
        dyb = (dmx * g1).astype(BF16)
        dya_ref[...] = dya
        dyb_ref[...] = dyb
        dgl_ref[:, 0:D_MODEL] = ((dmx * ya_ref[...].astype(F32)) * g0 * (1.0 - g0)).astype(BF16)
        dgl_ref[:, D_MODEL:2 * D_MODEL] = ((dmx * yb_ref[...].astype(F32)) * g1 * (1.0 - g1)).astype(BF16)
        daa_ref[...] = _dot_nt(dya, wsa_ref[...]).astype(BF16)
        dab_ref[...] = _dot_nt(dyb, wfo_ref[...]).astype(BF16)

    half = D_MODEL // 2
    return _row_call(body, "ffn_bwd_b", h1.shape[0], tm, [da, dh2, h1, ya, yb, zfg],
                     [w1s, g_mlp, w_mo, w_sa, w_fo],
                     [(D_MODEL, F32), (D_MODEL, BF16), (N_GATE, BF16), (D_MODEL, BF16), (D_MODEL, BF16),
                      (half, BF16), (half, BF16)], [D_MODEL])


def _in_proj_bwd(dz, dh1, x, w_all, g_mix, tm):
    def body(i, ins, consts, outs, accs):
        dz_ref, dh1_ref, x_ref = ins
        w_ref, g_ref = consts
        dx_ref, = outs
        dgx_ref, = accs
        du1 = _dot_nt(dz_ref[...], w_ref[...])
        dh, dg = _rms_bwd(x_ref[...], g_ref[...], du1)
        _acc_rows(dgx_ref, i, dg)
        dx_ref[...] = dh1_ref[...] + dh

    return _row_call(body, "in_proj_bwd", x.shape[0], tm, [dz, dh1, x], [w_all, g_mix],
                     [(D_MODEL, F32)], [D_MODEL])


def _matmul_tn(a, b, name, ts, stack_cols=0):
    n_rows, ka = a.shape
    n = b.shape[1]
    tk = min(ka, 1024)
    tn = 896 if n % 1024 else 1024
    n_stack = tn // stack_cols if stack_cols else 0
    assert ka % tk == 0 and n % tn == 0 and n_rows % ts == 0 and (not stack_cols or tk == ka)
    n_steps = n_rows // ts

    def kern(a_ref, b_ref, o_ref, acc_ref):
        s = pl.program_id(2)

        @pl.when(s == 0)
        def _():
            acc_ref[...] = jnp.zeros_like(acc_ref)
        acc_ref[...] += _dot_tn(a_ref[...].astype(BF16), b_ref[...])

        @pl.when(s == n_steps - 1)
        def _():
            if stack_cols:
                for c in range(n_stack):
                    o_ref[c] = acc_ref[:, c * stack_cols:(c + 1) * stack_cols].astype(BF16)
            else:
                o_ref[...] = acc_ref[...].astype(BF16)

    if stack_cols:
        out_spec = pl.BlockSpec((n_stack, tk, stack_cols), lambda i, j, s: (j, 0, 0))
        out_shape = jax.ShapeDtypeStruct((n // stack_cols, ka, stack_cols), BF16)
    else:
        out_spec = pl.BlockSpec((tk, tn), lambda i, j, s: (i, j))
        out_shape = jax.ShapeDtypeStruct((ka, n), BF16)
    return pl.pallas_call(
        kern, grid=(ka // tk, n // tn, n_steps),
        in_specs=[pl.BlockSpec((ts, tk), lambda i, j, s: (s, i)), pl.BlockSpec((ts, tn), lambda i, j, s: (s, j))],
        out_specs=out_spec, out_shape=out_shape, scratch_shapes=[pltpu.VMEM((tk, tn), F32)], name=name,
        compiler_params=_params(3))(a, b)


SCAN_CHUNK = 512


def _decay_cumsum(f_t, b_col):
    n_tok = f_t.shape[1]
    ch = min(SCAN_CHUNK, n_tok)

    def kern(f_ref, b_ref, c_ref):
        r = lax.broadcasted_iota(jnp.int32, (ch, ch), 0)
        c = lax.broadcasted_iota(jnp.int32, (ch, ch), 1)
        tri = (r <= c).astype(F32)
        carry = jnp.zeros((8, 1), F32)
        for k in range(n_tok // ch):
            xv = f_ref[:, k * ch:(k + 1) * ch] + b_ref[...]
            lf = jnp.minimum(xv, 0.0) - jnp.log(1.0 + jnp.exp(-jnp.abs(xv)))
            cs = jnp.dot(lf, tri, precision=lax.Precision.HIGHEST, preferred_element_type=F32) + carry
            c_ref[:, k * ch:(k + 1) * ch] = cs
            carry = cs[:, ch - 1:ch]

    return pl.pallas_call(kern, out_shape=jax.ShapeDtypeStruct((8, n_tok), F32), name="decay_cumsum",
                          compiler_params=_params(0))(f_t, b_col)


def _decay_bwd(g_t, f_t, b_col):
    n_tok = f_t.shape[1]
    ch = min(SCAN_CHUNK, n_tok)

    def kern(g_ref, f_ref, b_ref, df_ref, db_ref):
        r = lax.broadcasted_iota(jnp.int32, (ch, ch), 0)
        c = lax.broadcasted_iota(jnp.int32, (ch, ch), 1)
        tri = (r >= c).astype(F32)
        carry = jnp.zeros((8, 1), F32)
        tot = jnp.zeros((8, 1), F32)
        for k in reversed(range(n_tok // ch)):
            gv = g_ref[:, k * ch:(k + 1) * ch]
            rc = jnp.dot(gv, tri, precision=lax.Precision.HIGHEST, preferred_element_type=F32) + carry
            carry = rc[:, 0:1]
            xv = f_ref[:, k * ch:(k + 1) * ch] + b_ref[...]
            df = -rc / (1.0 + jnp.exp(xv))
            df_ref[:, k * ch:(k + 1) * ch] = df
            tot = tot + jnp.sum(df, axis=1, keepdims=True)
        db_ref[...] = jnp.broadcast_to(tot, db_ref.shape)

    return pl.pallas_call(kern, out_shape=[jax.ShapeDtypeStruct((8, n_tok), F32),
                                           jax.ShapeDtypeStruct((8, LANES), F32)],
                          name="decay_bwd", compiler_params=_params(0))(g_t, f_t, b_col)


def _swa_band_mask(n):
    row = lax.broadcasted_iota(jnp.int32, (SWA_BLOCK, 2 * SWA_BLOCK), 0) + SWA_BLOCK
    col = lax.broadcasted_iota(jnp.int32, (SWA_BLOCK, 2 * SWA_BLOCK), 1)
    cd = (row >> CHUNK_SHIFT) - (col >> CHUNK_SHIFT)
    first_real = jnp.where(n > 0, 0, SWA_BLOCK)
    ok = (cd >= 0) & (cd <= WINDOW_CHUNKS) & (col >= first_real)
    dist = jnp.abs(row - col).astype(F32)
    return ok, dist


def _swap_halves(t):
    return pltpu.roll(t.astype(F32), HEAD_DIM, axis=1).astype(t.dtype)


def _swa_specs():
    blk = SWA_BLOCK
    q = pl.BlockSpec((blk, 4 * LANES), lambda n: (n, 0))
    kp = pl.BlockSpec((blk, LANES), lambda n: (jnp.maximum(n - 1, 0), 4))
    kc = pl.BlockSpec((blk, LANES), lambda n: (n, 4))
    vp = pl.BlockSpec((blk, LANES), lambda n: (jnp.maximum(n - 1, 0), 5))
    vc = pl.BlockSpec((blk, LANES), lambda n: (n, 5))
    return q, kp, kc, vp, vc


def _swa_fwd(zm, sinks):
    n_tok = zm.shape[0]
    blk = SWA_BLOCK

    def kern(q_ref, kp_ref, kc_ref, vp_ref, vc_ref, sink_ref, o_ref, lse_ref):
        n = pl.program_id(0)
        ok, dist = _swa_band_mask(n)
        k2 = jnp.concatenate([kp_ref[...], kc_ref[...]], axis=0)
        v2 = jnp.concatenate([vp_ref[...], vc_ref[...]], axis=0)
        ksw, vsw = _swap_halves(k2), _swap_halves(v2)
        lane = lax.broadcasted_iota(jnp.int32, (blk, LANES), 1)
        lo = lane < HEAD_DIM
        lse_t = jnp.zeros((blk, LANES), F32)
        for pair in range(SWA_HEADS // 2):
            q2 = q_ref[:, pair * LANES:(pair + 1) * LANES]
            kvh = pair // 2
            outs = []
            for a in range(2):
                h = 2 * pair + a
                qa = jnp.where(lo if a == 0 else ~lo, q2, jnp.zeros_like(q2)) * SCALE
                kx, vx = (k2, v2) if a == kvh else (ksw, vsw)
                s = _dot_nt(qa, kx)
                s = jnp.where(ok, s - (2.0 ** -(h + 1)) * dist, NEG)
                sink = sink_ref[h]
                m = jnp.maximum(jnp.max(s, axis=-1, keepdims=True), sink)
                e = jnp.exp(s - m)
                l = jnp.sum(e, axis=-1, keepdims=True) + jnp.exp(sink - m)
                pn = (e * (1.0 / l)).astype(BF16)
                outs.append(_dot(pn, vx))
                lse_t = jnp.where(lane == h, m + jnp.log(l), lse_t)
            o_ref[:, pair * LANES:(pair + 1) * LANES] = jnp.where(lo, outs[0], outs[1]).astype(BF16)
        lse_ref[...] = lse_t

    q, kp, kc, vp, vc = _swa_specs()
    return pl.pallas_call(
        kern, grid=(n_tok // blk,),
        in_specs=[q, kp, kc, vp, vc, pl.BlockSpec(memory_space=pltpu.SMEM)],
        out_specs=[pl.BlockSpec((blk, 4 * LANES), lambda n: (n, 0)), pl.BlockSpec((blk, LANES), lambda n: (n, 0))],
        out_shape=[jax.ShapeDtypeStruct((n_tok, 4 * LANES), BF16), jax.ShapeDtypeStruct((n_tok, LANES), F32)],
        name="swa_fwd", compiler_params=_params(1))(zm, zm, zm, zm, zm, sinks)


def _swa_bwd(zm, sinks, d_out, out, lse):
    n_tok = zm.shape[0]
    blk = SWA_BLOCK

    def kern(q_ref, kp_ref, kc_ref, vp_ref, vc_ref, do_ref, o_ref, lse_ref, sink_ref,
             dq_ref, dkp_ref, dkc_ref, dvp_ref, dvc_ref, dsk_ref):
        n = pl.program_id(0)

        @pl.when(n == 0)
        def _():
            dsk_ref[...] = jnp.zeros_like(dsk_ref)

        ok, dist = _swa_band_mask(n)
        k2 = jnp.concatenate([kp_ref[...], kc_ref[...]], axis=0)
        v2 = jnp.concatenate([vp_ref[...], vc_ref[...]], axis=0)
        ksw, vsw = _swap_halves(k2), _swap_halves(v2)
        lane = lax.broadcasted_iota(jnp.int32, (blk, LANES), 1)
        lo = lane < HEAD_DIM
        lse_t = lse_ref[...]
        zero = jnp.zeros((2 * blk, LANES), F32)
        dk_same, dk_swap, dv_same, dv_swap = zero, zero, zero, zero
        for pair in range(SWA_HEADS // 2):
            cols = slice(pair * LANES, (pair + 1) * LANES)
            q2, do2, o2 = q_ref[:, cols], do_ref[:, cols], o_ref[:, cols]
            kvh = pair // 2
            dqs = []
            for a in range(2):
                h = 2 * pair + a
                half = lo if a == 0 else ~lo
                qa = jnp.where(half, q2, jnp.zeros_like(q2)) * SCALE
                doa = jnp.where(half, do2, jnp.zeros_like(do2))
                kx, vx = (k2, v2) if a == kvh else (ksw, vsw)
                s = _dot_nt(qa, kx)
                s = jnp.where(ok, s - (2.0 ** -(h + 1)) * dist, NEG)
                lse_h = lse_t[:, h:h + 1]
                prob = jnp.exp(s - lse_h)
                dd = jnp.sum(doa.astype(F32) * o2.astype(F32), axis=-1, keepdims=True)
                dp = _dot_nt(doa, vx)
                ds = (prob * (dp - dd)).astype(BF16)
                p_sink = jnp.exp(sink_ref[h] - lse_h)
                dsk_ref[h:h + 1, :] += jnp.broadcast_to(-jnp.sum(p_sink * dd, axis=0, keepdims=True), (1, LANES))
                dqs.append(_dot(ds, kx) * SCALE)
                dk_c = _dot_tn(ds, qa)
                dv_c = _dot_tn(prob.astype(BF16), doa)
                if a == kvh:
                    dk_same, dv_same = dk_same + dk_c, dv_same + dv_c
                else:
                    dk_swap, dv_swap = dk_swap + dk_c, dv_swap + dv_c
            dq_ref[:, cols] = jnp.where(lo, dqs[0], dqs[1]).astype(BF16)
        dk = dk_same + pltpu.roll(dk_swap, HEAD_DIM, axis=1)
        dv = dv_same + pltpu.roll(dv_swap, HEAD_DIM, axis=1)
        dkp_ref[...] = dk[0:blk]
        dkc_ref[...] = dk[blk:2 * blk]
        dvp_ref[...] = dv[0:blk]
        dvc_ref[...] = dv[blk:2 * blk]

    q, kp, kc, vp, vc = _swa_specs()
    wide = pl.BlockSpec((blk, 4 * LANES), lambda n: (n, 0))
    narrow = pl.BlockSpec((blk, LANES), lambda n: (n, 0))
    part = jax.ShapeDtypeStruct((n_tok, LANES), F32)
    return pl.pallas_call(
        kern, grid=(n_tok // blk,),
        in_specs=[q, kp, kc, vp, vc, wide, wide, narrow, pl.BlockSpec(memory_space=pltpu.SMEM)],
        out_specs=[wide, narrow, narrow, narrow, narrow, pl.BlockSpec((8, LANES), lambda n: (0, 0))],
        out_shape=[jax.ShapeDtypeStruct((n_tok, 4 * LANES), BF16), part, part, part, part,
                   jax.ShapeDtypeStruct((8, LANES), F32)],
        name="swa_bwd", compiler_params=_params(1))(zm, zm, zm, zm, zm, d_out, out, lse, sinks)


def _my_pos():
    return lax.axis_index("x"), lax.axis_index("y"), lax.axis_index("c")


def _peer(k):
    x, y, c = _my_pos()
    px, py, pc = x ^ (k >> 2), y ^ ((k >> 1) & 1), c ^ (k & 1)
    return (px, py, pc), 4 * px + 2 * py + pc


def _gather_copies(x_refs, out_refs, send_sems, recv_sems, local_sems):
    x, y, c = _my_pos()
    my_id = 4 * x + 2 * y + c
    local = [pltpu.make_async_copy(x_refs[w], out_refs[w].at[my_id], local_sems.at[w]) for w in range(len(x_refs))]
    sends, arrivals = [], []
    for k in range(1, N_DEV):
        peer, peer_id = _peer(k)
        for w in range(len(x_refs)):
            sems = dict(send_sem=send_sems.at[7 * w + k - 1], recv_sem=recv_sems.at[7 * w + k - 1],
                        device_id=peer, device_id_type=MESH)
            sends.append(pltpu.make_async_remote_copy(src_ref=x_refs[w], dst_ref=out_refs[w].at[my_id], **sems))
            arrivals.append(pltpu.make_async_remote_copy(src_ref=x_refs[w], dst_ref=out_refs[w].at[peer_id], **sems))
    return local, sends, arrivals


def _scatter_copies(g_refs, part_refs, send_sems, recv_sems, local_sems):
    x, y, c = _my_pos()
    my_id = 4 * x + 2 * y + c
    local = [pltpu.make_async_copy(g_refs[w].at[my_id], part_refs[w].at[0], local_sems.at[w])
             for w in range(len(g_refs))]
    sends, arrivals = [], []
    for k in range(1, N_DEV):
        peer, peer_id = _peer(k)
        for w in range(len(g_refs)):
            sems = dict(send_sem=send_sems.at[7 * w + k - 1], recv_sem=recv_sems.at[7 * w + k - 1],
                        device_id=peer, device_id_type=MESH)
            sends.append(pltpu.make_async_remote_copy(src_ref=g_refs[w].at[peer_id], dst_ref=part_refs[w].at[k], **sems))
            arrivals.append(pltpu.make_async_remote_copy(src_ref=g_refs[w].at[my_id], dst_ref=part_refs[w].at[k], **sems))
    return local, sends, arrivals


def _start_copies(local, sends, arrivals):
    for cp in local + sends:
        cp.start()


def _finish_copies(local, sends, arrivals):
    for cp in arrivals:
        cp.wait_recv()
    for cp in sends:
        cp.wait_send()
    for cp in local:
        cp.wait()


def _exchange_scratch(n_arrays):
    return [pltpu.SemaphoreType.DMA((7 * n_arrays,)), pltpu.SemaphoreType.DMA((7 * n_arrays,)),
            pltpu.SemaphoreType.DMA((n_arrays,))]


class _Ride:
    def __init__(self, arrays, out_shape, copies):
        self.arrays, self.out_shape, self.copies = list(arrays), list(out_shape), copies
        any_spec = pl.BlockSpec(memory_space=pl.ANY)
        self.in_specs = [any_spec] * len(self.arrays)
        self.out_specs = [any_spec] * len(self.arrays)
        self.scratch = _exchange_scratch(len(self.arrays)) if self.arrays else []

    def specs(self):
        return self

    def at_first_step(self, n0, n1, in_refs, out_refs, sems):
        @pl.when(jnp.logical_and(pl.program_id(0) == 0, pl.program_id(1) == 0))
        def _():
            _start_copies(*self.copies(in_refs, out_refs, *sems))

    def at_last_step(self, n0, n1, in_refs, out_refs, sems):
        @pl.when(jnp.logical_and(pl.program_id(0) == n0 - 1, pl.program_id(1) == n1 - 1))
        def _():
            _finish_copies(*self.copies(in_refs, out_refs, *sems))


_NO_RIDE = _Ride([], [], None)


def _gather_ride(shards):
    return _Ride(shards, [jax.ShapeDtypeStruct((N_DEV,) + s.shape, s.dtype) for s in shards], _gather_copies)


def _scatter_ride(grads):
    return _Ride(grads, [jax.ShapeDtypeStruct(g.shape, g.dtype) for g in grads], _scatter_copies)


Q_COL, K_COL, V_COL = 6, 10, 14


def _causal(t, tq, tk):
    row = lax.broadcasted_iota(jnp.int32, (tq, tk), 0)
    col = lax.broadcasted_iota(jnp.int32, (tq, tk), 1)
    return jnp.where(col <= row, t, NEG)


def _lane_tile(stat, width):
    return jnp.tile(stat, (1, width // LANES))


def _fox_steps(nq):
    steps = [(i2, j, 0 if j < 2 * i2 else 1 + j - 2 * i2) for i2 in range(nq // 2) for j in range(2 * i2 + 2)]
    return [np.asarray(col, np.int32) for col in zip(*steps)]


_SWEEPS = {0: [(0, False), (1, False)], 1: [(0, True), (1, False)], 2: [(1, True)]}


def _fox_fwd(zm, c_pairs, tq, ride=None):
    n_tok = zm.shape[0]
    ii, jj, kk = _fox_steps(n_tok // tq)
    n_steps = len(ii)
    n_ride = len(ride.arrays) if ride else 0

    def kern(ii_ref, jj_ref, kk_ref, q_ref, k_ref, v_ref, ck_ref, *more):
        ride_in, (o_ref, ln_ref), ride_out = more[:n_ride], more[n_ride:n_ride + 2], more[n_ride + 2:2 * n_ride + 2]
        qs_ref, m_ref, l_ref, acc_ref = more[2 * n_ride + 2:2 * n_ride + 6]
        step = pl.program_id(1)
        j, kind = jj_ref[step], kk_ref[step]
        lo = lax.broadcasted_iota(jnp.int32, (2 * tq, LANES), 1) < HEAD_DIM
        if ride:
            ride.at_first_step(FOX_HEADS // 2, n_steps, ride_in, ride_out, more[2 * n_ride + 6:])

        @pl.when(j == 0)
        def _():
            q2 = q_ref[...]
            zq = jnp.zeros_like(q2)
            qs_ref[0] = jnp.where(lo, q2, zq) * SCALE
            qs_ref[1] = jnp.where(lo, zq, q2) * SCALE
            m_ref[...] = jnp.full(m_ref.shape, NEG, F32)
            l_ref[...] = jnp.zeros(l_ref.shape, F32)
            acc_ref[...] = jnp.zeros(acc_ref.shape, F32)

        def sweep(subs):
            kv = k_ref[...]
            v_ones = jnp.concatenate([v_ref[...], jnp.ones((tq, LANES), BF16)], axis=1)
            for sub, diag in subs:
                rows = slice(sub * tq, (sub + 1) * tq)
                for a in range(2):
                    t = _dot_nt(qs_ref[a, rows], kv) - ck_ref[a:a + 1, :]
                    if diag:
                        t = _causal(t, tq, tq)
                    m_old = m_ref[a, rows]
                    m_new = jnp.maximum(m_old, jnp.max(t, axis=-1, keepdims=True))
                    alpha = jnp.exp(m_old - m_new)
                    e = jnp.exp(t - _lane_tile(m_new, tq)).astype(BF16)
                    pv = _dot(e, v_ones)
                    acc_ref[a, rows] = alpha * acc_ref[a, rows] + pv[:, :LANES]
                    l_ref[a, rows] = alpha * l_ref[a, rows] + pv[:, LANES:]
                    m_ref[a, rows] = m_new

        for kind_id, subs in _SWEEPS.items():
            pl.when(kind == kind_id)(lambda subs=subs: sweep(subs))

        @pl.when(kind == 2)
        def _():
            o_ref[...] = jnp.where(lo, acc_ref[0] / l_ref[0], acc_ref[1] / l_ref[1]).astype(BF16)
            ln_ref[:, :LANES] = m_ref[0] + jnp.log(l_ref[0])
            ln_ref[:, LANES:] = m_ref[1] + jnp.log(l_ref[1])

        if ride:
            ride.at_last_step(FOX_HEADS // 2, n_steps, ride_in, ride_out, more[2 * n_ride + 6:])

    blk = (tq, LANES)
    by_i = lambda col: (lambda hp, s, ii, jj, kk: (ii[s], col + hp))
    by_j = lambda col: (lambda hp, s, ii, jj, kk: (jj[s], col + hp))
    extra = ride.specs() if ride else _NO_RIDE
    grid_spec = pltpu.PrefetchScalarGridSpec(
        num_scalar_prefetch=3, grid=(FOX_HEADS // 2, n_steps),
        in_specs=[pl.BlockSpec((2 * tq, LANES), by_i(Q_COL)), pl.BlockSpec(blk, by_j(K_COL)),
                  pl.BlockSpec(blk, by_j(V_COL)),
                  pl.BlockSpec((None, 2, tq), lambda hp, s, ii, jj, kk: (hp, 0, jj[s]))] + extra.in_specs,
        out_specs=[pl.BlockSpec((2 * tq, LANES), by_i(0)), pl.BlockSpec((2 * tq, 2 * LANES), by_i(0))] + extra.out_specs,
        scratch_shapes=[pltpu.VMEM((2, 2 * tq, LANES), BF16), pltpu.VMEM((2, 2 * tq, LANES), F32),
                        pltpu.VMEM((2, 2 * tq, LANES), F32), pltpu.VMEM((2, 2 * tq, LANES), F32)] + extra.scratch)
    return pl.pallas_call(
        kern, grid_spec=grid_spec,
        out_shape=[jax.ShapeDtypeStruct((n_tok, 4 * LANES), BF16),
                   jax.ShapeDtypeStruct((n_tok, FOX_HEADS * LANES), F32)] + extra.out_shape,
        name="fox_fwd", compiler_params=_params(2))(ii, jj, kk, zm, zm, zm, c_pairs, *extra.arrays)


def _fox_delta(d_out, out, tm):
    def body(i, ins, consts, outs, accs):
        do_ref, o_ref = ins
        dl_ref, = outs
        lane = lax.broadcasted_iota(jnp.int32, (tm, LANES), 1)
        lo = lane < HEAD_DIM
        for pair in range(FOX_HEADS // 2):
            cols = slice(pair * LANES, (pair + 1) * LANES)
            prod = do_ref[:, cols].astype(F32) * o_ref[:, cols].astype(F32)
            for a in range(2):
                dd = jnp.sum(jnp.where(lo if a == 0 else ~lo, prod, 0.0), axis=-1, keepdims=True)
                h = 2 * pair + a
                dl_ref[:, h * LANES:(h + 1) * LANES] = jnp.broadcast_to(dd, (tm, LANES))

    return _row_call(body, "fox_delta", d_out.shape[0], tm, [d_out, out], [], [(FOX_HEADS * LANES, F32)], [])[0]


def _fox_bwd(zm, c_pairs, d_out, lnorm, delta, tq, ride=None):
    n_tok = zm.shape[0]
    ii, jj, kk = _fox_steps(n_tok // tq)
    n_steps = len(ii)
    n_ride = len(ride.arrays) if ride else 0

    def kern(ii_ref, jj_ref, kk_ref, q_ref, k_ref, v_ref, ck_ref, do_ref, ln_ref, dl_ref, *more):
        ride_in, ride_out = more[:n_ride], more[n_ride + 5:2 * n_ride + 5]
        dq_ref, dk_ref, dv_ref, cs_ref, rs_ref = more[n_ride:n_ride + 5]
        qs_ref, qo_ref, dos_ref, dq_acc = more[2 * n_ride + 5:2 * n_ride + 9]
        step = pl.program_id(1)
        j, kind = jj_ref[step], kk_ref[step]
        lo = lax.broadcasted_iota(jnp.int32, (2 * tq, LANES), 1) < HEAD_DIM
        if ride:
            ride.at_first_step(FOX_HEADS // 2, n_steps, ride_in, ride_out, more[2 * n_ride + 9:])

        @pl.when(step == 0)
        def _():
            dk_ref[...] = jnp.zeros_like(dk_ref)
            dv_ref[...] = jnp.zeros_like(dv_ref)
            cs_ref[...] = jnp.zeros_like(cs_ref)

        @pl.when(j == 0)
        def _():
            q2, do2 = q_ref[...], do_ref[...]
            zq = jnp.zeros_like(q2)
            ones = jnp.ones((2 * tq, LANES), BF16)
            for a in range(2):
                half = lo if a == 0 else ~lo
                qa = jnp.where(half, q2, zq) * SCALE
                qs_ref[a] = qa
                qo_ref[a] = jnp.concatenate([qa, ones], axis=1)
                dos_ref[a] = jnp.where(half, do2, zq)
            dq_acc[...] = jnp.zeros(dq_acc.shape, F32)

        def sweep(subs):
            kv, vv = k_ref[...], v_ref[...]
            k_ones = jnp.concatenate([kv, jnp.ones((tq, LANES), BF16)], axis=1)
            dk, dv, sums = None, None, [None, None]
            for sub, diag in subs:
                rows = slice(sub * tq, (sub + 1) * tq)
                for a in range(2):
                    t = _dot_nt(qs_ref[a, rows], kv) - ck_ref[a:a + 1, :]
                    if diag:
                        t = _causal(t, tq, tq)
                    prob = jnp.exp(t - _lane_tile(ln_ref[rows, a * LANES:(a + 1) * LANES], tq))
                    dp = _dot_nt(dos_ref[a, rows], vv)
                    ds = (prob * (dp - _lane_tile(dl_ref[rows, a * LANES:(a + 1) * LANES], tq))).astype(BF16)
                    dq_acc[a, rows] += _dot(ds, k_ones)
                    dk_cs = _dot_tn(ds, qo_ref[a, rows])
                    dv_a = _dot_tn(prob.astype(BF16), dos_ref[a, rows])
                    dk = dk_cs[:, :LANES] if dk is None else dk + dk_cs[:, :LANES]
                    dv = dv_a if dv is None else dv + dv_a
                    sums[a] = dk_cs[:, LANES:] if sums[a] is None else sums[a] + dk_cs[:, LANES:]
            keys = pl.ds(pl.multiple_of(j * tq, tq), tq)
            dk_ref[keys, :] += dk
            cs_ref[keys, :] += jnp.where(lo[:tq], sums[0], sums[1])
            dv_ref[keys, :] += dv

        for kind_id, subs in _SWEEPS.items():
            pl.when(kind == kind_id)(lambda subs=subs: sweep(subs))

        @pl.when(kind == 2)
        def _():
            dq_ref[...] = jnp.where(lo, dq_acc[0, :, :LANES], dq_acc[1, :, :LANES]) * SCALE
            rs_ref[...] = jnp.where(lo, dq_acc[0, :, LANES:], dq_acc[1, :, LANES:])

        if ride:
            ride.at_last_step(FOX_HEADS // 2, n_steps, ride_in, ride_out, more[2 * n_ride + 9:])

    blk = (tq, LANES)
    by_i = lambda col: (lambda hp, s, ii, jj, kk: (ii[s], col + hp))
    by_j = lambda col: (lambda hp, s, ii, jj, kk: (jj[s], col + hp))
    resident = pl.BlockSpec((2 * tq, LANES), by_i(0))
    stat = pl.BlockSpec((2 * tq, 2 * LANES), by_i(0))
    whole = pl.BlockSpec((n_tok, LANES), lambda hp, s, ii, jj, kk: (0, hp))
    extra = ride.specs() if ride else _NO_RIDE
    grid_spec = pltpu.PrefetchScalarGridSpec(
        num_scalar_prefetch=3, grid=(FOX_HEADS // 2, n_steps),
        in_specs=[pl.BlockSpec((2 * tq, LANES), by_i(Q_COL)), pl.BlockSpec(blk, by_j(K_COL)),
                  pl.BlockSpec(blk, by_j(V_COL)),
                  pl.BlockSpec((None, 2, tq), lambda hp, s, ii, jj, kk: (hp, 0, jj[s])),
                  resident, stat, stat] + extra.in_specs,
        out_specs=[resident, whole, whole, whole, resident] + extra.out_specs,
        scratch_shapes=[pltpu.VMEM((2, 2 * tq, LANES), BF16), pltpu.VMEM((2, 2 * tq, 2 * LANES), BF16),
                        pltpu.VMEM((2, 2 * tq, LANES), BF16), pltpu.VMEM((2, 2 * tq, 2 * LANES), F32)] + extra.scratch)
    wide = jax.ShapeDtypeStruct((n_tok, 4 * LANES), F32)
    return pl.pallas_call(
        kern, grid_spec=grid_spec, out_shape=[wide] * 5 + extra.out_shape, name="fox_bwd",
        compiler_params=_params(2, FOX_BWD_VMEM))(ii, jj, kk, zm, zm, zm, c_pairs, d_out, lnorm, delta, *extra.arrays)


def _all_gather(shards):
    n_w = len(shards)

    def kern(*refs):
        x_refs, out_refs = refs[:n_w], refs[n_w:2 * n_w]
        send_sems, recv_sems, local_sems = refs[2 * n_w:]
        x, y, c = _my_pos()
        me, sibling = (x, y, c), (x, y, 1 - c)
        chips = [(1 - x, y), (x, 1 - y), (1 - x, 1 - y)]

        def slot(w, px, py, pc):
            return out_refs[w].at[4 * px + 2 * py + pc]

        def copy(w, k, block, to, src=None):
            return pltpu.make_async_remote_copy(
                src_ref=slot(w, *block) if src is None else src, dst_ref=slot(w, *block),
                send_sem=send_sems.at[7 * w + k], recv_sem=recv_sems.at[7 * w + k], device_id=to, device_id_type=MESH)

        local, started = [], []
        for w in range(n_w):
            mine = pltpu.make_async_copy(x_refs[w], slot(w, *me), local_sems.at[w])
            mine.start()
            local.append(mine)
            first = [copy(w, 0, me, sibling, src=x_refs[w])]
            first += [copy(w, 1 + k, me, (*chip, c), src=x_refs[w]) for k, chip in enumerate(chips)]
            for cp in first:
                cp.start()
            started += first
        for k, chip in enumerate(chips):
            for w in range(n_w):
                copy(w, 1 + k, (*chip, c), me).wait_recv()
                passed = copy(w, 4 + k, (*chip, c), sibling)
                passed.start()
                started.append(passed)
        for w in range(n_w):
            copy(w, 0, sibling, me).wait_recv()
            for k, chip in enumerate(chips):
                copy(w, 4 + k, (*chip, 1 - c), me).wait_recv()
        for cp in started:
            cp.wait_send()
        for cp in local:
            cp.wait()

    any_spec = pl.BlockSpec(memory_space=pl.ANY)
    return pl.pallas_call(
        kern, out_shape=[jax.ShapeDtypeStruct((N_DEV,) + s.shape, s.dtype) for s in shards],
        in_specs=[any_spec] * n_w, out_specs=[any_spec] * n_w,
        scratch_shapes=[pltpu.SemaphoreType.DMA((7 * n_w,)), pltpu.SemaphoreType.DMA((7 * n_w,)),
                        pltpu.SemaphoreType.DMA((n_w,))],
        name="weight_all_gather")(*shards)


def _grad_exchange(grads, small):
    n_w = len(grads)

    def kern(*refs):
        g_refs, s_ref = refs[:n_w], refs[n_w]
        part_refs, sall_ref = refs[n_w + 1:2 * n_w + 1], refs[2 * n_w + 1]
        big = _scatter_copies(g_refs, part_refs, *refs[2 * n_w + 2:2 * n_w + 5])
        sml = _gather_copies([s_ref], [sall_ref], *refs[2 * n_w + 5:])
        _start_copies(*sml)
        _start_copies(*big)
        _finish_copies(*sml)
        _finish_copies(*big)

    any_spec = pl.BlockSpec(memory_space=pl.ANY)
    return pl.pallas_call(
        kern, out_shape=[jax.ShapeDtypeStruct(g.shape, g.dtype) for g in grads]
        + [jax.ShapeDtypeStruct((N_DEV,) + small.shape, small.dtype)],
        in_specs=[any_spec] * (n_w + 1), out_specs=[any_spec] * (n_w + 1),
        scratch_shapes=_exchange_scratch(n_w) + _exchange_scratch(1), name="grad_exchange")(*grads, small)


ADAMW_BLOCK_BYTES = 2 * 1024 * 1024


def _adamw(parts, w, m, v, name):
    n_parts, n_rows, n_cols = parts.shape
    limit = max(8, ADAMW_BLOCK_BYTES // (n_parts * n_cols * parts.dtype.itemsize))
    tr = max(t for t in range(8, n_rows + 1, 8) if n_rows % t == 0 and t <= limit)

    def kern(p_ref, w_ref, m_ref, v_ref, g_out, d_out, m_out, v_out):
        g = p_ref[0].astype(F32)
        for k in range(1, n_parts):
            g = g + p_ref[k].astype(F32)
        m_new = ADAM_B1 * m_ref[...] + (1.0 - ADAM_B1) * g
        v_new = ADAM_B2 * v_ref[...] + (1.0 - ADAM_B2) * jnp.square(g)
        m_hat = m_new / (1.0 - ADAM_B1 ** ADAM_STEP)
        v_hat = v_new / (1.0 - ADAM_B2 ** ADAM_STEP)
        g_out[...] = g
        d_out[...] = -ADAM_LR * (m_hat / (jnp.sqrt(v_hat) + ADAM_EPS) + ADAM_WD * w_ref[...])
        m_out[...] = m_new
        v_out[...] = v_new

    row = pl.BlockSpec((tr, n_cols), lambda i: (i, 0))
    out = jax.ShapeDtypeStruct((n_rows, n_cols), F32)
    return pl.pallas_call(
        kern, grid=(n_rows // tr,),
        in_specs=[pl.BlockSpec((n_parts, tr, n_cols), lambda i: (0, i, 0)), row, row, row],
        out_specs=[row, row, row, row], out_shape=[out, out, out, out], name=name,
        compiler_params=_params(1))(parts, w, m, v)


SHARDED = {
    "w_in": ((D_MODEL, D_IN), 1), "w_br_swa": ((512, D_MODEL), 1), "w_br_fox": ((512, D_MODEL), 1),
    "w_mix_out": ((D_MODEL, D_MODEL), 0), "w_ff1": ((D_MODEL, D_FF), 1), "w_ff2": ((D_FF, D_MODEL), 0),
    "w_ple_gate": ((D_MODEL, D_MODEL), 0), "w_ple_proj": ((PLE_DIM, D_MODEL), 1),
}
W_IN_SHARD = D_IN // N_DEV
W_IN_PAD = 640
SMALL = ("g_mix", "g_mlp", "g_ple", "g_final", "b_forget", "swa_sinks")
SMALL_COLS = 1024


def _wire_shard(name, a):
    a = a.reshape(a.shape[-2:])
    return jnp.pad(a, ((0, 0), (0, W_IN_PAD - W_IN_SHARD))) if name == "w_in" else a


def _from_wire(name, a):
    return (a[:, :W_IN_SHARD] if name == "w_in" else a)[None]


def _cols_to_full(stacked):
    return jnp.concatenate([stacked[d] for d in range(N_DEV)], axis=1)


def _full_to_cols(full):
    n = full.shape[1] // N_DEV
    return jnp.stack([full[:, d * n:(d + 1) * n] for d in range(N_DEV)])


def _w_all_from_wire(stacked):
    w_in = jnp.concatenate([stacked[d][:, :W_IN_SHARD] for d in range(N_DEV)], axis=1)
    fpad = jnp.zeros((D_MODEL, N_FPAD - FOX_HEADS), stacked.dtype)
    return jnp.concatenate([w_in[:, :N_MAIN + FOX_HEADS], fpad, w_in[:, N_MAIN + FOX_HEADS:]], axis=1)


def _dw_in_to_wire(dw_all):
    dw_in = jnp.concatenate([dw_all[:, :N_MAIN + FOX_HEADS], dw_all[:, N_MAIN + N_FPAD:]], axis=1)
    pad = jnp.zeros((D_MODEL, W_IN_PAD - W_IN_SHARD), dw_all.dtype)
    return jnp.stack([jnp.concatenate([dw_in[:, d * W_IN_SHARD:(d + 1) * W_IN_SHARD], pad], axis=1)
                      for d in range(N_DEV)])


def _pack_small(vals):
    rows = [jnp.pad(vals[n].reshape(-1), (0, SMALL_COLS - vals[n].size)) for n in SMALL]
    rows += [jnp.zeros((SMALL_COLS,), F32)] * (8 - len(SMALL))
    return jnp.stack(rows)


def _unpack_small(slab, like):
    return {n: slab[r, :like[n].size].reshape(like[n].shape) for r, n in enumerate(SMALL)}


def _local_step(x, p, tgt, w, small, tm, tq, ts, late_shards=None):
    n_tok = x.shape[0]
    row = lambda v: v.reshape(1, -1)
    g_mix, g_mlp, g_ple, g_fin = row(small["g_mix"]), row(small["g_mlp"]), row(small["g_ple"]), row(small["g_final"])
    sinks = small["swa_sinks"].reshape(-1)
    b_col = small["b_forget"].reshape(FOX_HEADS, 1)

    u1, zm, zfg, zf = _in_proj(x, g_mix, w["w_all"], tm)
    f_t = zf[:, :FOX_HEADS].T
    c_pairs = _decay_cumsum(f_t, b_col).reshape(FOX_HEADS // 2, 2, n_tok)
    attn_a, lse_a = _swa_fwd(zm, sinks)
    if late_shards is None:
        attn_b, ln_b = _fox_fwd(zm, c_pairs, tq)
    else:
        attn_b, ln_b, *late = _fox_fwd(zm, c_pairs, tq, _gather_ride(list(late_shards.values())))
        w = {**w, **_gathered_to_local(dict(zip(late_shards, late)))}
    ya, yb, mixed, h1, u2 = _mix_fwd(attn_a, attn_b, zfg, x, w["w_br_swa"], w["w_br_fox"], w["w_mix_out"], g_mlp, tm)
    a, r, h2 = _ffn_fwd(u2, h1, w["w_ff1"], w["w_ff2"], tm // 2)
    dh3, dlg, dpp, u3, loss_acc, dgf = _head_fwd_bwd(h2, p, tgt, g_ple, w["w_ple_gate"], w["w_ple_proj"], g_fin, tm)

    dh2, dh2b, da, dgp = _ffn_bwd_a(dlg, dh3, h2, a, w["w_ple_gate"], g_ple, w["w_ff2"], tm // 2)
    dh1, dh1b, dgl, dya, dyb, daa, dab, dgm = _ffn_bwd_b(
        da, dh2, h1, ya, yb, zfg, w["w_ff1"], g_mlp, w["w_mix_out"], w["w_br_swa"], w["w_br_fox"], tm // 2)
    dq_a, dkp, dkc, dvp, dvc, dsk = _swa_bwd(zm, sinks, daa, attn_a, lse_a)
    delta_b = _fox_delta(dab, attn_b, tm)
    dw = {
        "w_br_swa": _matmul_tn(attn_a, dya, "dw_br_swa", ts),
        "w_br_fox": _matmul_tn(attn_b, dyb, "dw_br_fox", ts),
        "w_mix_out": _matmul_tn(mixed, dh1b, "dw_mix_out", ts),
        "w_ff1": _matmul_tn(u2, da, "dw_ff1", ts, stack_cols=D_FF // N_DEV),
        "w_ff2": _matmul_tn(r, dh2b, "dw_ff2", ts),
        "w_ple_gate": _matmul_tn(u3, dlg, "dw_ple_gate", ts),
        "w_ple_proj": _matmul_tn(p, dpp, "dw_ple_proj", ts),
    }
    if late_shards is None:
        dq_b, dk_b, dv_b, cs, rs = _fox_bwd(zm, c_pairs, dab, ln_b, delta_b, tq)
        late_parts = None
    else:
        wire = _local_to_wire(dw)
        dq_b, dk_b, dv_b, cs, rs, *parts = _fox_bwd(zm, c_pairs, dab, ln_b, delta_b, tq,
                                                    _scatter_ride([wire[n] for n in late_shards]))
        late_parts = dict(zip(late_shards, parts))

    up = lambda t: jnp.concatenate([t[SWA_BLOCK:], jnp.zeros((SWA_BLOCK, LANES), F32)], axis=0)
    dk_a, dv_a = dkc + up(dkp), dvc + up(dvp)
    per_head = lambda t: t[:, ::HEAD_DIM].T
    df_t, db = _decay_bwd(per_head(cs) - per_head(rs), f_t, b_col)
    df = jnp.pad(df_t.T, ((0, 0), (0, N_FPAD - FOX_HEADS)))
    dz = jnp.concatenate([dq_a, dk_a.astype(BF16), dv_a.astype(BF16), dq_b.astype(BF16), dk_b.astype(BF16), dv_b.astype(BF16),
                          df.astype(BF16), dgl], axis=1)
    dx, dgx = _in_proj_bwd(dz, dh1, x, w["w_all"], g_mix, tm)

    dw["w_all"] = _matmul_tn(u1, dz, "dw_in", ts)
    dsmall = {"g_mix": dgx[0], "g_mlp": dgm[0], "g_ple": dgp[0], "g_final": dgf[0],
              "b_forget": db[:, 0], "swa_sinks": dsk[:, 0]}
    return loss_acc[0, 0], dx, dw, dsmall, late_parts


_ROWS = lambda t: t.reshape(-1, t.shape[-1])
_BY_ROWS = lambda t: t.reshape(N_DEV, t.shape[0] // N_DEV, t.shape[1])
_SAME = lambda t: t
LOCAL_LAYOUT = {
    "w_in": ("w_all", _w_all_from_wire, _dw_in_to_wire), "w_br_swa": ("w_br_swa", _cols_to_full, _full_to_cols),
    "w_br_fox": ("w_br_fox", _cols_to_full, _full_to_cols), "w_mix_out": ("w_mix_out", _ROWS, _BY_ROWS),
    "w_ff1": ("w_ff1", _SAME, _SAME), "w_ff2": ("w_ff2", _SAME, _BY_ROWS),
    "w_ple_gate": ("w_ple_gate", _ROWS, _BY_ROWS), "w_ple_proj": ("w_ple_proj", _cols_to_full, _full_to_cols),
}


def _gathered_to_local(g):
    return {LOCAL_LAYOUT[n][0]: LOCAL_LAYOUT[n][1](t) for n, t in g.items()}


def _local_to_wire(dw):
    names = {local: n for n, (local, _, _) in LOCAL_LAYOUT.items()}
    return {names[local]: LOCAL_LAYOUT[names[local]][2](t) for local, t in dw.items()}


def kernel(x, p, g_mix, w_in, b_forget, swa_sinks, w_br_swa, w_br_fox, w_mix_out, g_mlp, w_ff1, w_ff2, g_ple, w_ple_gate, w_ple_proj, g_final, loss_target, m_g_mix, m_w_in, m_b_forget, m_swa_sinks, m_w_br_swa, m_w_br_fox, m_w_mix_out, m_g_mlp, m_w_ff1, m_w_ff2, m_g_ple, m_w_ple_gate, m_w_ple_proj, m_g_final, v_g_mix, v_w_in, v_b_forget, v_swa_sinks, v_w_br_swa, v_w_br_fox, v_w_mix_out, v_g_mlp, v_w_ff1, v_w_ff2, v_g_ple, v_w_ple_gate, v_w_ple_proj, v_g_final):
    given = dict(g_mix=g_mix, w_in=w_in, b_forget=b_forget, swa_sinks=swa_sinks, w_br_swa=w_br_swa, w_br_fox=w_br_fox,
                 w_mix_out=w_mix_out, g_mlp=g_mlp, w_ff1=w_ff1, w_ff2=w_ff2, g_ple=g_ple, w_ple_gate=w_ple_gate,
                 w_ple_proj=w_ple_proj, g_final=g_final)
    mom = dict(g_mix=m_g_mix, w_in=m_w_in, b_forget=m_b_forget, swa_sinks=m_swa_sinks, w_br_swa=m_w_br_swa,
               w_br_fox=m_w_br_fox, w_mix_out=m_w_mix_out, g_mlp=m_g_mlp, w_ff1=m_w_ff1, w_ff2=m_w_ff2, g_ple=m_g_ple,
               w_ple_gate=m_w_ple_gate, w_ple_proj=m_w_ple_proj, g_final=m_g_final)
    vel = dict(g_mix=v_g_mix, w_in=v_w_in, b_forget=v_b_forget, swa_sinks=v_swa_sinks, w_br_swa=v_w_br_swa,
               w_br_fox=v_w_br_fox, w_mix_out=v_w_mix_out, g_mlp=v_g_mlp, w_ff1=v_w_ff1, w_ff2=v_w_ff2, g_ple=v_g_ple,
               w_ple_gate=v_w_ple_gate, w_ple_proj=v_w_ple_proj, g_final=v_g_final)
    names = list(given)
    sharded = list(SHARDED)

    w_wire = {n: _wire_shard(n, given[n]) for n in sharded}
    late = [n for n in sharded if n != "w_in"]
    gathered = _all_gather([w_wire["w_in"].astype(BF16)])
    local_w = _gathered_to_local({"w_in": gathered[0]})
    small = {n: given[n].reshape(-1) for n in SMALL}

    n_tok = x.shape[1]
    tile = min(512, n_tok // 4)
    loss_part, dx, dw, dsmall, parts = _local_step(
        x[0], p[0, 0], loss_target[0], local_w, small, tm=tile, tq=tile, ts=tile,
        late_shards={n: w_wire[n].astype(BF16) for n in late})
    loss = lax.psum(loss_part, AXES)

    parts["w_in"], small_all = _grad_exchange([_dw_in_to_wire(dw["w_all"])], _pack_small(dsmall))

    res = {}
    for n in sharded:
        part = parts[n]
        flat = part.reshape(N_DEV, -1, part.shape[-1])
        outs = _adamw(flat, w_wire[n], _wire_shard(n, mom[n]), _wire_shard(n, vel[n]), "adamw_" + n)
        res[n] = [_from_wire(n, o) for o in outs]
    outs_s = _adamw(small_all, _pack_small(small), _pack_small({n: mom[n] for n in SMALL}),
                    _pack_small({n: vel[n] for n in SMALL}), "adamw_small")
    small_res = [_unpack_small(o, given) for o in outs_s]

    groups = [[res[n][k] if n in res else small_res[k][n] for n in names] for k in range(4)]
    return (loss, dx[None], *groups[0], *groups[1], *groups[2], *groups[3])
```

```python
import numpy as np
import jax
import jax.numpy as jnp
from jax import lax
from jax.experimental import pallas as pl
from jax.experimental.pallas import tpu as pltpu

F32 = jnp.float32
BF16 = jnp.bfloat16

D_MODEL = 1024
HEAD_DIM = 64
SWA_HEADS = 8
FOX_HEADS = 8
CHUNK_SHIFT = 6
SWA_BLOCK = 128
WINDOW_CHUNKS = 2
D_FF = 4096
PLE_DIM = 256
RMS_EPS = 1e-6
N_MAIN = 2304
N_FPAD = 128
N_GATE = 2048
N_ALL = N_MAIN + N_FPAD + N_GATE
D_IN = N_MAIN + FOX_HEADS + N_GATE
SCALE = HEAD_DIM ** -0.5
NEG = -1e30

ADAM_LR = 0.001
ADAM_B1 = 0.9
ADAM_B2 = 0.999
ADAM_EPS = 1e-08
ADAM_WD = 0.01
ADAM_STEP = 10

N_DEV = 8
LANES = 128
V7X_VMEM_BYTES = 64 * 1024 * 1024
VMEM_LIMIT = V7X_VMEM_BYTES * 3 // 4
FOX_BWD_VMEM = V7X_VMEM_BYTES * 7 // 8
MESH = pl.DeviceIdType.MESH
AXES = ("x", "y", "c")

_NT = (((1,), (1,)), ((), ()))
_TN = (((0,), (0,)), ((), ()))


def _params(n_grid, vmem_limit=VMEM_LIMIT):
    return pltpu.CompilerParams(dimension_semantics=("arbitrary",) * n_grid, vmem_limit_bytes=vmem_limit)


def _chunks(n, step):
    return [(s, min(step, n - s)) for s in range(0, n, step)]


def _sigmoid(x):
    return 1.0 / (1.0 + jnp.exp(-x))


def _dot(a, b):
    return jnp.dot(a, b, preferred_element_type=F32)


def _dot_nt(a, b):
    return lax.dot_general(a, b, _NT, preferred_element_type=F32)


def _dot_tn(a, b):
    return lax.dot_general(a, b, _TN, preferred_element_type=F32)


def _lane_concat(stacked_ref):
    return jnp.concatenate([stacked_ref[d] for d in range(N_DEV)], axis=1)


def _rms(h):
    return lax.rsqrt(jnp.mean(h * h, axis=-1, keepdims=True) + RMS_EPS)


def _rms_bwd(h, g, du):
    rs = _rms(h)
    n = h * rs
    dn = du * g
    dh = rs * (dn - n * jnp.mean(dn * n, axis=-1, keepdims=True))
    return dh, jnp.sum(du * n, axis=0, keepdims=True)


def _acc_rows(ref, i, row):
    @pl.when(i == 0)
    def _():
        ref[...] = jnp.zeros_like(ref)
    ref[...] += jnp.broadcast_to(row, ref.shape)


def _row_call(body, name, n_rows, tm, row_ins, const_ins, row_outs, acc_outs, ride=None):
    n_ri, n_ci, n_ro, n_ao = len(row_ins), len(const_ins), len(row_outs), len(acc_outs)
    extra = ride if ride else _NO_RIDE
    n_ride = len(extra.arrays)
    grid = (n_rows // tm,)

    def kern(*refs):
        i = pl.program_id(0)
        ins, refs = refs[:n_ri + n_ci], refs[n_ri + n_ci:]
        ride_in, refs = refs[:n_ride], refs[n_ride:]
        outs, refs = refs[:n_ro + n_ao], refs[n_ro + n_ao:]
        ride_out, sems = refs[:n_ride], refs[n_ride:]
        if ride:
            ride.at_first_step(grid, ride_in, ride_out, sems)
        body(i, ins[:n_ri], ins[n_ri:], outs[:n_ro], outs[n_ro:])
        if ride:
            ride.at_last_step(grid, ride_in, ride_out, sems)

    def whole(a):
        zeros = (0,) * a.ndim
        return pl.BlockSpec(a.shape, lambda i: zeros, pipeline_mode=pl.Buffered(1))

    in_specs = [pl.BlockSpec((tm, a.shape[1]), lambda i: (i, 0)) for a in row_ins]
    in_specs += [whole(a) for a in const_ins] + extra.in_specs
    out_specs = [pl.BlockSpec((tm, c), lambda i: (i, 0)) for c, _ in row_outs]
    out_specs += [pl.BlockSpec((8, c), lambda i: (0, 0)) for c in acc_outs] + extra.out_specs
    out_shape = [jax.ShapeDtypeStruct((n_rows, c), dt) for c, dt in row_outs]
    out_shape += [jax.ShapeDtypeStruct((8, c), F32) for c in acc_outs] + extra.out_shape
    return pl.pallas_call(kern, grid=grid, in_specs=in_specs, out_specs=out_specs, out_shape=out_shape,
                          scratch_shapes=extra.scratch, name=name,
                          compiler_params=_params(1))(*row_ins, *const_ins, *extra.arrays)


def _in_proj(x, g_mix, w_all, tm):
    def body(i, ins, consts, outs, accs):
        x_ref, = ins
        g_ref, w_ref = consts
        u_ref, zm_ref, zfg_ref, zf_ref = outs
        xv = x_ref[...]
        u = ((xv * _rms(xv)) * g_ref[...]).astype(BF16)
        u_ref[...] = u
        for s, n in _chunks(N_MAIN, 768):
            zm_ref[:, s:s + n] = _dot(u, w_ref[:, s:s + n]).astype(BF16)
        for s, n in _chunks(N_FPAD + N_GATE, 512):
            zfg_ref[:, s:s + n] = _dot(u, w_ref[:, N_MAIN + s:N_MAIN + s + n])
        zf_ref[...] = zfg_ref[:, :N_FPAD]

    return _row_call(body, "in_proj", x.shape[0], tm, [x], [g_mix, w_all],
                     [(D_MODEL, BF16), (N_MAIN, BF16), (N_FPAD + N_GATE, F32), (N_FPAD, F32)], [])


def _mix_fwd(attn_a, attn_b, zfg, x, w_sa, w_fo, w_mo, g_mlp, tm):
    def body(i, ins, consts, outs, accs):
        aa_ref, ab_ref, zfg_ref, x_ref = ins
        wsa_ref, wfo_ref, wmo_ref, g_ref = consts
        ya_ref, yb_ref, mx_ref, h1_ref, u2_ref = outs
        ya = _dot(aa_ref[...], _lane_concat(wsa_ref))
        yb = _dot(ab_ref[...], _lane_concat(wfo_ref))
        g0 = _sigmoid(zfg_ref[:, N_FPAD:N_FPAD + D_MODEL])
        g1 = _sigmoid(zfg_ref[:, N_FPAD + D_MODEL:N_FPAD + 2 * D_MODEL])
        mixed = (g0 * ya + g1 * yb).astype(BF16)
        ya_ref[...] = ya.astype(BF16)
        yb_ref[...] = yb.astype(BF16)
        mx_ref[...] = mixed
        h1 = x_ref[...] + _dot(mixed, wmo_ref[...])
        h1_ref[...] = h1
        u2_ref[...] = ((h1 * _rms(h1)) * g_ref[...]).astype(BF16)

    return _row_call(body, "mix_fwd", x.shape[0], tm, [attn_a, attn_b, zfg, x], [w_sa, w_fo, w_mo, g_mlp],
                     [(D_MODEL, BF16), (D_MODEL, BF16), (D_MODEL, BF16), (D_MODEL, F32), (D_MODEL, BF16)], [])


def _ffn_fwd(u2, h1, w1s, w2s, tm):
    ch = D_FF // N_DEV

    def body(i, ins, consts, outs, accs):
        u_ref, h1_ref = ins
        w1_ref, w2_ref = consts
        a_ref, r_ref, h2_ref = outs
        u = u_ref[...]
        acc = h1_ref[...]
        for c in range(N_DEV):
            a = _dot(u, w1_ref[c])
            a_ref[:, c * ch:(c + 1) * ch] = a.astype(BF16)
            r = jnp.square(jnp.maximum(a, 0.0)).astype(BF16)
            r_ref[:, c * ch:(c + 1) * ch] = r
            acc = acc + _dot(r, w2_ref[c])
        h2_ref[...] = acc

    return _row_call(body, "ffn_fwd", u2.shape[0], tm, [u2, h1], [w1s, w2s],
                     [(D_FF, BF16), (D_FF, BF16), (D_MODEL, F32)], [])


def _head_fwd_bwd(h2, p, tgt, g_ple, w_pg, w_pp, g_fin, tm):
    def body(i, ins, consts, outs, accs):
        h2_ref, p_ref, t_ref = ins
        gp_ref, wpg_ref, wpp_ref, gf_ref = consts
        dh3_ref, dlg_ref, dpp_ref, u3_ref = outs
        loss_ref, dgf_ref = accs
        h2 = h2_ref[...]
        u3 = ((h2 * _rms(h2)) * gp_ref[...]).astype(BF16)
        u3_ref[...] = u3
        pg = _sigmoid(_dot(u3, wpg_ref[...]))
        pp = _dot(p_ref[...].astype(BF16), _lane_concat(wpp_ref))
        h3 = h2 + pg * pp
        rs3 = _rms(h3)
        n3 = h3 * rs3
        gf = gf_ref[...]
        err = n3 * gf - t_ref[...]
        row_loss = 0.5 * jnp.mean(err * err, axis=-1, keepdims=True)
        _acc_rows(loss_ref, i, jnp.broadcast_to(jnp.sum(row_loss, axis=0, keepdims=True), (1, LANES)))
        dy = err * (1.0 / D_MODEL)
        _acc_rows(dgf_ref, i, jnp.sum(dy * n3, axis=0, keepdims=True))
        dn = dy * gf
        dh3 = rs3 * (dn - n3 * jnp.mean(dn * n3, axis=-1, keepdims=True))
        dh3_ref[...] = dh3
        dpp_ref[...] = (dh3 * pg).astype(BF16)
        dlg_ref[...] = ((dh3 * pp) * pg * (1.0 - pg)).astype(BF16)

    return _row_call(body, "head_fwd_bwd", h2.shape[0], tm, [h2, p, tgt], [g_ple, w_pg, w_pp, g_fin],
                     [(D_MODEL, F32), (D_MODEL, BF16), (D_MODEL, BF16), (D_MODEL, BF16)], [LANES, D_MODEL])


def _ffn_bwd_a(dlg, dh3, h2, a, w_pg, g_ple, w2s, tm):
    ch = D_FF // N_DEV

    def body(i, ins, consts, outs, accs):
        dlg_ref, dh3_ref, h2_ref, a_ref = ins
        wpg_ref, gp_ref, w2_ref = consts
        dh2_ref, dh2b_ref, da_ref = outs
        dgp_ref, = accs
        du3 = _dot_nt(dlg_ref[...], wpg_ref[...])
        dh, dg = _rms_bwd(h2_ref[...], gp_ref[...], du3)
        _acc_rows(dgp_ref, i, dg)
        dh2 = dh3_ref[...] + dh
        dh2_ref[...] = dh2
        dh2b = dh2.astype(BF16)
        dh2b_ref[...] = dh2b
        for c in range(N_DEV):
            dr = _dot_nt(dh2b, w2_ref[c])
            av = a_ref[:, c * ch:(c + 1) * ch].astype(F32)
            da_ref[:, c * ch:(c + 1) * ch] = (dr * (2.0 * jnp.maximum(av, 0.0))).astype(BF16)

    return _row_call(body, "ffn_bwd_a", h2.shape[0], tm, [dlg, dh3, h2, a], [w_pg, g_ple, w2s],
                     [(D_MODEL, F32), (D_MODEL, BF16), (D_FF, BF16)], [D_MODEL])


def _ffn_bwd_b(da, dh2, h1, ya, yb, zfg, w1s, g_mlp, w_mo, w_sa, w_fo, tm):
    ch = D_FF // N_DEV

    def body(i, ins, consts, outs, accs):
        da_ref, dh2_ref, h1_ref, ya_ref, yb_ref, zfg_ref = ins
        w1_ref, gm_ref, wmo_ref, wsa_ref, wfo_ref = consts
        dh1_ref, dh1b_ref, dgl_ref, dya_ref, dyb_ref, daa_ref, dab_ref = outs
        dgm_ref, = accs
        du2 = _dot_nt(da_ref[:, 0:ch], w1_ref[0])
        for c in range(1, N_DEV):
            du2 = du2 + _dot_nt(da_ref[:, c * ch:(c + 1) * ch], w1_ref[c])
        dh, dg = _rms_bwd(h1_ref[...], gm_ref[...], du2)
        _acc_rows(dgm_ref, i, dg)
        dh1 = dh2_ref[...] + dh
        dh1_ref[...] = dh1
        dh1b = dh1.astype(BF16)
        dh1b_ref[...] = dh1b
        dmx = _dot_nt(dh1b, wmo_ref[...])
        g0 = _sigmoid(zfg_ref[:, N_FPAD:N_FPAD + D_MODEL])
        g1 = _sigmoid(zfg_ref[:, N_FPAD + D_MODEL:N_FPAD + 2 * D_MODEL])
        dya = (dmx * g0).astype(BF16)
        dyb = (dmx * g1).astype(BF16)
        dya_ref[...] = dya
        dyb_ref[...] = dyb
        dgl_ref[:, 0:D_MODEL] = ((dmx * ya_ref[...].astype(F32)) * g0 * (1.0 - g0)).astype(BF16)
        dgl_ref[:, D_MODEL:2 * D_MODEL] = ((dmx * yb_ref[...].astype(F32)) * g1 * (1.0 - g1)).astype(BF16)
        daa_ref[...] = _dot_nt(dya, _lane_concat(wsa_ref)).astype(BF16)
        dab_ref[...] = _dot_nt(dyb, _lane_concat(wfo_ref)).astype(BF16)

    half = D_MODEL // 2
    return _row_call(body, "ffn_bwd_b", h1.shape[0], tm, [da, dh2, h1, ya, yb, zfg],
                     [w1s, g_mlp, w_mo, w_sa, w_fo],
                     [(D_MODEL, F32), (D_MODEL, BF16), (N_GATE, BF16), (D_MODEL, BF16), (D_MODEL, BF16),
                      (half, BF16), (half, BF16)], [D_MODEL])


def _in_proj_bwd(dz, dh1, x, w_all, g_mix, tm, ride=None):
    def body(i, ins, consts, outs, accs):
        dz_ref, dh1_ref, x_ref = ins
        w_ref, g_ref = consts
        dx_ref, = outs
        dgx_ref, = accs
        du1 = _dot_nt(dz_ref[...], w_ref[...])
        dh, dg = _rms_bwd(x_ref[...], g_ref[...], du1)
        _acc_rows(dgx_ref, i, dg)
        dx_ref[...] = dh1_ref[...] + dh

    return _row_call(body, "in_proj_bwd", x.shape[0], tm, [dz, dh1, x], [w_all, g_mix],
                     [(D_MODEL, F32)], [D_MODEL], ride)


def _matmul_tn(a, b, name, ts, stack_cols=0):
    n_rows, ka = a.shape
    n = b.shape[1]
    tk = min(ka, 1024)
    tn = 896 if n % 1024 else 1024
    n_stack = tn // stack_cols if stack_cols else 0
    assert ka % tk == 0 and n % tn == 0 and n_rows % ts == 0 and (not stack_cols or tk == ka)
    n_steps = n_rows // ts

    def kern(a_ref, b_ref, o_ref, acc_ref):
        s = pl.program_id(2)

        @pl.when(s == 0)
        def _():
            acc_ref[...] = jnp.zeros_like(acc_ref)
        acc_ref[...] += _dot_tn(a_ref[...].astype(BF16), b_ref[...])

        @pl.when(s == n_steps - 1)
        def _():
            if stack_cols:
                for c in range(n_stack):
                    o_ref[c] = acc_ref[:, c * stack_cols:(c + 1) * stack_cols].astype(BF16)
            else:
                o_ref[...] = acc_ref[...].astype(BF16)

    if stack_cols:
        out_spec = pl.BlockSpec((n_stack, tk, stack_cols), lambda i, j, s: (j, 0, 0))
        out_shape = jax.ShapeDtypeStruct((n // stack_cols, ka, stack_cols), BF16)
    else:
        out_spec = pl.BlockSpec((tk, tn), lambda i, j, s: (i, j))
        out_shape = jax.ShapeDtypeStruct((ka, n), BF16)
    return pl.pallas_call(
        kern, grid=(ka // tk, n // tn, n_steps),
        in_specs=[pl.BlockSpec((ts, tk), lambda i, j, s: (s, i)), pl.BlockSpec((ts, tn), lambda i, j, s: (s, j))],
        out_specs=out_spec, out_shape=out_shape, scratch_shapes=[pltpu.VMEM((tk, tn), F32)], name=name,
        compiler_params=_params(3))(a, b)


SCAN_CHUNK = 512


def _decay_cumsum(f_t, b_col):
    n_tok = f_t.shape[1]
    ch = min(SCAN_CHUNK, n_tok)

    def kern(f_ref, b_ref, c_ref):
        r = lax.broadcasted_iota(jnp.int32, (ch, ch), 0)
        c = lax.broadcasted_iota(jnp.int32, (ch, ch), 1)
        tri = (r <= c).astype(F32)
        carry = jnp.zeros((8, 1), F32)
        for k in range(n_tok // ch):
            xv = f_ref[:, k * ch:(k + 1) * ch] + b_ref[...]
            lf = jnp.minimum(xv, 0.0) - jnp.log(1.0 + jnp.exp(-jnp.abs(xv)))
            cs = jnp.dot(lf, tri, precision=lax.Precision.HIGHEST, preferred_element_type=F32) + carry
            c_ref[:, k * ch:(k + 1) * ch] = cs
            carry = cs[:, ch - 1:ch]

    return pl.pallas_call(kern, out_shape=jax.ShapeDtypeStruct((8, n_tok), F32), name="decay_cumsum",
                          compiler_params=_params(0))(f_t, b_col)


def _decay_bwd(cs, rs, f_t, b_col):
    n_tok = f_t.shape[1]
    ch = min(SCAN_CHUNK, n_tok)
    n_ch = n_tok // ch

    def kern(cs_ref, rs_ref, f_ref, b_ref, df_ref, db_ref, carry_ref):
        k = pl.program_id(0)

        @pl.when(k == 0)
        def _():
            carry_ref[...] = jnp.zeros_like(carry_ref)
            db_ref[...] = jnp.zeros_like(db_ref)

        r = lax.broadcasted_iota(jnp.int32, (ch, ch), 0)
        c = lax.broadcasted_iota(jnp.int32, (ch, ch), 1)
        tri = (r >= c).astype(F32)
        head = lax.broadcasted_iota(jnp.int32, (8, 4 * LANES), 0)
        lane = lax.broadcasted_iota(jnp.int32, (8, 4 * LANES), 1)
        pick = (lane == HEAD_DIM * head).astype(F32)
        dc = lax.dot_general(pick, rs_ref[...] - cs_ref[...], _NT, precision=lax.Precision.HIGHEST,
                             preferred_element_type=F32)
        rc = jnp.dot(dc, tri, precision=lax.Precision.HIGHEST, preferred_element_type=F32) + carry_ref[:, 0:1]
        carry_ref[...] = jnp.broadcast_to(rc[:, 0:1], carry_ref.shape)
        df = rc / (1.0 + jnp.exp(f_ref[...] + b_ref[...]))
        df_ref[...] = df
        db_ref[...] += jnp.broadcast_to(jnp.sum(df, axis=1, keepdims=True), db_ref.shape)

    back = lambda k: n_ch - 1 - k
    wide = pl.BlockSpec((ch, 4 * LANES), lambda k: (back(k), 0))
    row = pl.BlockSpec((8, ch), lambda k: (0, back(k)))
    return pl.pallas_call(
        kern, grid=(n_ch,),
        in_specs=[wide, wide, row, pl.BlockSpec((8, 1), lambda k: (0, 0))],
        out_specs=[row, pl.BlockSpec((8, LANES), lambda k: (0, 0))],
        out_shape=[jax.ShapeDtypeStruct((8, n_tok), F32), jax.ShapeDtypeStruct((8, LANES), F32)],
        scratch_shapes=[pltpu.VMEM((8, LANES), F32)], name="decay_bwd", compiler_params=_params(1))(cs, rs, f_t, b_col)


def _swa_band_mask(n):
    row = lax.broadcasted_iota(jnp.int32, (SWA_BLOCK, 2 * SWA_BLOCK), 0) + SWA_BLOCK
    col = lax.broadcasted_iota(jnp.int32, (SWA_BLOCK, 2 * SWA_BLOCK), 1)
    cd = (row >> CHUNK_SHIFT) - (col >> CHUNK_SHIFT)
    first_real = jnp.where(n > 0, 0, SWA_BLOCK)
    ok = (cd >= 0) & (cd <= WINDOW_CHUNKS) & (col >= first_real)
    dist = jnp.abs(row - col).astype(F32)
    return ok, dist


def _swap_halves(t):
    return pltpu.roll(t.astype(F32), HEAD_DIM, axis=1).astype(t.dtype)


def _swa_specs():
    blk = SWA_BLOCK
    q = pl.BlockSpec((blk, 4 * LANES), lambda n: (n, 0))
    kp = pl.BlockSpec((blk, LANES), lambda n: (jnp.maximum(n - 1, 0), 4))
    kc = pl.BlockSpec((blk, LANES), lambda n: (n, 4))
    vp = pl.BlockSpec((blk, LANES), lambda n: (jnp.maximum(n - 1, 0), 5))
    vc = pl.BlockSpec((blk, LANES), lambda n: (n, 5))
    return q, kp, kc, vp, vc


def _swa_fwd(zm, sinks):
    n_tok = zm.shape[0]
    blk = SWA_BLOCK

    def kern(q_ref, kp_ref, kc_ref, vp_ref, vc_ref, sink_ref, o_ref, lse_ref):
        n = pl.program_id(0)
        ok, dist = _swa_band_mask(n)
        k2 = jnp.concatenate([kp_ref[...], kc_ref[...]], axis=0)
        v2 = jnp.concatenate([vp_ref[...], vc_ref[...]], axis=0)
        ksw, vsw = _swap_halves(k2), _swap_halves(v2)
        lane = lax.broadcasted_iota(jnp.int32, (blk, LANES), 1)
        lo = lane < HEAD_DIM
        lse_t = jnp.zeros((blk, LANES), F32)
        for pair in range(SWA_HEADS // 2):
            q2 = q_ref[:, pair * LANES:(pair + 1) * LANES]
            kvh = pair // 2
            outs = []
            for a in range(2):
                h = 2 * pair + a
                qa = jnp.where(lo if a == 0 else ~lo, q2, jnp.zeros_like(q2)) * SCALE
                kx, vx = (k2, v2) if a == kvh else (ksw, vsw)
                s = _dot_nt(qa, kx)
                s = jnp.where(ok, s - (2.0 ** -(h + 1)) * dist, NEG)
                sink = sink_ref[h]
                m = jnp.maximum(jnp.max(s, axis=-1, keepdims=True), sink)
                e = jnp.exp(s - m)
                l = jnp.sum(e, axis=-1, keepdims=True) + jnp.exp(sink - m)
                pn = (e * (1.0 / l)).astype(BF16)
                outs.append(_dot(pn, vx))
                lse_t = jnp.where(lane == h, m + jnp.log(l), lse_t)
            o_ref[:, pair * LANES:(pair + 1) * LANES] = jnp.where(lo, outs[0], outs[1]).astype(BF16)
        lse_ref[...] = lse_t

    q, kp, kc, vp, vc = _swa_specs()
    return pl.pallas_call(
        kern, grid=(n_tok // blk,),
        in_specs=[q, kp, kc, vp, vc, pl.BlockSpec(memory_space=pltpu.SMEM)],
        out_specs=[pl.BlockSpec((blk, 4 * LANES), lambda n: (n, 0)), pl.BlockSpec((blk, LANES), lambda n: (n, 0))],
        out_shape=[jax.ShapeDtypeStruct((n_tok, 4 * LANES), BF16), jax.ShapeDtypeStruct((n_tok, LANES), F32)],
        name="swa_fwd", compiler_params=_params(1))(zm, zm, zm, zm, zm, sinks)


def _swa_bwd(zm, sinks, d_out, out, lse):
    n_tok = zm.shape[0]
    blk = SWA_BLOCK

    def kern(q_ref, kp_ref, kc_ref, vp_ref, vc_ref, do_ref, o_ref, lse_ref, sink_ref,
             dq_ref, dkp_ref, dkc_ref, dvp_ref, dvc_ref, dsk_ref):
        n = pl.program_id(0)

        @pl.when(n == 0)
        def _():
            dsk_ref[...] = jnp.zeros_like(dsk_ref)

        ok, dist = _swa_band_mask(n)
        k2 = jnp.concatenate([kp_ref[...], kc_ref[...]], axis=0)
        v2 = jnp.concatenate([vp_ref[...], vc_ref[...]], axis=0)
        ksw, vsw = _swap_halves(k2), _swap_halves(v2)
        lane = lax.broadcasted_iota(jnp.int32, (blk, LANES), 1)
        lo = lane < HEAD_DIM
        lse_t = lse_ref[...]
        zero = jnp.zeros((2 * blk, LANES), F32)
        dk_same, dk_swap, dv_same, dv_swap = zero, zero, zero, zero
        for pair in range(SWA_HEADS // 2):
            cols = slice(pair * LANES, (pair + 1) * LANES)
            q2, do2, o2 = q_ref[:, cols], do_ref[:, cols], o_ref[:, cols]
            kvh = pair // 2
            dqs = []
            for a in range(2):
                h = 2 * pair + a
                half = lo if a == 0 else ~lo
                qa = jnp.where(half, q2, jnp.zeros_like(q2)) * SCALE
                doa = jnp.where(half, do2, jnp.zeros_like(do2))
                kx, vx = (k2, v2) if a == kvh else (ksw, vsw)
                s = _dot_nt(qa, kx)
                s = jnp.where(ok, s - (2.0 ** -(h + 1)) * dist, NEG)
                lse_h = lse_t[:, h:h + 1]
                prob = jnp.exp(s - lse_h)
                dd = jnp.sum(doa.astype(F32) * o2.astype(F32), axis=-1, keepdims=True)
                dp = _dot_nt(doa, vx)
                ds = (prob * (dp - dd)).astype(BF16)
                p_sink = jnp.exp(sink_ref[h] - lse_h)
                dsk_ref[h:h + 1, :] += jnp.broadcast_to(-jnp.sum(p_sink * dd, axis=0, keepdims=True), (1, LANES))
                dqs.append(_dot(ds, kx) * SCALE)
                dk_c = _dot_tn(ds, qa)
                dv_c = _dot_tn(prob.astype(BF16), doa)
                if a == kvh:
                    dk_same, dv_same = dk_same + dk_c, dv_same + dv_c
                else:
                    dk_swap, dv_swap = dk_swap + dk_c, dv_swap + dv_c
            dq_ref[:, cols] = jnp.where(lo, dqs[0], dqs[1]).astype(BF16)
        dk = dk_same + pltpu.roll(dk_swap, HEAD_DIM, axis=1)
        dv = dv_same + pltpu.roll(dv_swap, HEAD_DIM, axis=1)
        dkp_ref[...] = dk[0:blk]
        dkc_ref[...] = dk[blk:2 * blk]
        dvp_ref[...] = dv[0:blk]
        dvc_ref[...] = dv[blk:2 * blk]

    q, kp, kc, vp, vc = _swa_specs()
    wide = pl.BlockSpec((blk, 4 * LANES), lambda n: (n, 0))
    narrow = pl.BlockSpec((blk, LANES), lambda n: (n, 0))
    part = jax.ShapeDtypeStruct((n_tok, LANES), F32)
    return pl.pallas_call(
        kern, grid=(n_tok // blk,),
        in_specs=[q, kp, kc, vp, vc, wide, wide, narrow, pl.BlockSpec(memory_space=pltpu.SMEM)],
        out_specs=[wide, narrow, narrow, narrow, narrow, pl.BlockSpec((8, LANES), lambda n: (0, 0))],
        out_shape=[jax.ShapeDtypeStruct((n_tok, 4 * LANES), BF16), part, part, part, part,
                   jax.ShapeDtypeStruct((8, LANES), F32)],
        name="swa_bwd", compiler_params=_params(1))(zm, zm, zm, zm, zm, d_out, out, lse, sinks)


def _my_pos():
    return lax.axis_index("x"), lax.axis_index("y"), lax.axis_index("c")


def _peer(k):
    x, y, c = _my_pos()
    px, py, pc = x ^ (k >> 2), y ^ ((k >> 1) & 1), c ^ (k & 1)
    return (px, py, pc), 4 * px + 2 * py + pc


def _gather_copies(x_refs, out_refs, send_sems, recv_sems, local_sems):
    x, y, c = _my_pos()
    my_id = 4 * x + 2 * y + c
    local = [pltpu.make_async_copy(x_refs[w], out_refs[w].at[my_id], local_sems.at[w]) for w in range(len(x_refs))]
    sends, arrivals = [], []
    for k in range(1, N_DEV):
        peer, peer_id = _peer(k)
        for w in range(len(x_refs)):
            sems = dict(send_sem=send_sems.at[7 * w + k - 1], recv_sem=recv_sems.at[7 * w + k - 1],
                        device_id=peer, device_id_type=MESH)
            sends.append(pltpu.make_async_remote_copy(src_ref=x_refs[w], dst_ref=out_refs[w].at[my_id], **sems))
            arrivals.append(pltpu.make_async_remote_copy(src_ref=x_refs[w], dst_ref=out_refs[w].at[peer_id], **sems))
    return local, sends, arrivals


def _scatter_copies(g_refs, part_refs, send_sems, recv_sems, local_sems):
    x, y, c = _my_pos()
    my_id = 4 * x + 2 * y + c
    local = [pltpu.make_async_copy(g_refs[w].at[my_id], part_refs[w].at[0], local_sems.at[w])
             for w in range(len(g_refs))]
    sends, arrivals = [], []
    for k in range(1, N_DEV):
        peer, peer_id = _peer(k)
        for w in range(len(g_refs)):
            sems = dict(send_sem=send_sems.at[7 * w + k - 1], recv_sem=recv_sems.at[7 * w + k - 1],
                        device_id=peer, device_id_type=MESH)
            sends.append(pltpu.make_async_remote_copy(src_ref=g_refs[w].at[peer_id], dst_ref=part_refs[w].at[k], **sems))
            arrivals.append(pltpu.make_async_remote_copy(src_ref=g_refs[w].at[my_id], dst_ref=part_refs[w].at[k], **sems))
    return local, sends, arrivals


def _start_copies(local, sends, arrivals):
    for cp in local + sends:
        cp.start()


def _finish_copies(local, sends, arrivals):
    for cp in arrivals:
        cp.wait_recv()
    for cp in sends:
        cp.wait_send()
    for cp in local:
        cp.wait()


def _exchange_scratch(n_arrays):
    return [pltpu.SemaphoreType.DMA((7 * n_arrays,)), pltpu.SemaphoreType.DMA((7 * n_arrays,)),
            pltpu.SemaphoreType.DMA((n_arrays,))]


class _Ride:
    def __init__(self, arrays, out_shape, copies):
        self.arrays, self.out_shape, self.copies = list(arrays), list(out_shape), copies
        any_spec = pl.BlockSpec(memory_space=pl.ANY)
        self.in_specs = [any_spec] * len(self.arrays)
        self.out_specs = [any_spec] * len(self.arrays)
        self.scratch = _exchange_scratch(len(self.arrays)) if self.arrays else []

    def specs(self):
        return self

    @staticmethod
    def _at(grid, last):
        hit = [pl.program_id(d) == (n - 1 if last else 0) for d, n in enumerate(grid)]
        return hit[0] if len(hit) == 1 else jnp.logical_and(*hit)

    def at_first_step(self, grid, in_refs, out_refs, sems):
        @pl.when(self._at(grid, False))
        def _():
            _start_copies(*self.copies(in_refs, out_refs, *sems))

    def at_last_step(self, grid, in_refs, out_refs, sems):
        @pl.when(self._at(grid, True))
        def _():
            _finish_copies(*self.copies(in_refs, out_refs, *sems))


_NO_RIDE = _Ride([], [], None)


def _gather_ride(shards):
    return _Ride(shards, [jax.ShapeDtypeStruct((N_DEV,) + s.shape, s.dtype) for s in shards], _gather_copies)


def _scatter_ride(grads):
    return _Ride(grads, [jax.ShapeDtypeStruct(g.shape, g.dtype) for g in grads], _scatter_copies)


Q_COL, K_COL, V_COL = 6, 10, 14


def _causal(t, tq, tk):
    row = lax.broadcasted_iota(jnp.int32, (tq, tk), 0)
    col = lax.broadcasted_iota(jnp.int32, (tq, tk), 1)
    return jnp.where(col <= row, t, NEG)


def _lane_tile(stat, width):
    return jnp.tile(stat, (1, width // LANES))


def _fox_steps(nq):
    steps = [(i2, j, 0 if j < 2 * i2 else 1 + j - 2 * i2) for i2 in range(nq // 2) for j in range(2 * i2 + 2)]
    return [np.asarray(col, np.int32) for col in zip(*steps)]


_SWEEPS = {0: [(0, False), (1, False)], 1: [(0, True), (1, False)], 2: [(1, True)]}


def _fox_fwd(zm, c_pairs, tq, ride=None):
    n_tok = zm.shape[0]
    ii, jj, kk = _fox_steps(n_tok // tq)
    n_steps = len(ii)
    n_ride = len(ride.arrays) if ride else 0

    def kern(ii_ref, jj_ref, kk_ref, q_ref, k_ref, v_ref, ck_ref, *more):
        ride_in, (o_ref, ln_ref), ride_out = more[:n_ride], more[n_ride:n_ride + 2], more[n_ride + 2:2 * n_ride + 2]
        qs_ref, m_ref, l_ref, acc_ref = more[2 * n_ride + 2:2 * n_ride + 6]
        step = pl.program_id(1)
        j, kind = jj_ref[step], kk_ref[step]
        lo = lax.broadcasted_iota(jnp.int32, (2 * tq, LANES), 1) < HEAD_DIM
        if ride:
            ride.at_first_step((FOX_HEADS // 2, n_steps), ride_in, ride_out, more[2 * n_ride + 6:])

        @pl.when(j == 0)
        def _():
            q2 = q_ref[...]
            zq = jnp.zeros_like(q2)
            qs_ref[0] = jnp.where(lo, q2, zq) * SCALE
            qs_ref[1] = jnp.where(lo, zq, q2) * SCALE
            m_ref[...] = jnp.full(m_ref.shape, NEG, F32)
            l_ref[...] = jnp.zeros(l_ref.shape, F32)
            acc_ref[...] = jnp.zeros(acc_ref.shape, F32)

        def sweep(subs):
            kv = k_ref[...]
            v_ones = jnp.concatenate([v_ref[...], jnp.ones((tq, LANES), BF16)], axis=1)
            for sub, diag in subs:
                rows = slice(sub * tq, (sub + 1) * tq)
                for a in range(2):
                    t = _dot_nt(qs_ref[a, rows], kv) - ck_ref[a:a + 1, :]
                    if diag:
                        t = _causal(t, tq, tq)
                    m_old = m_ref[a, rows]
                    m_new = jnp.maximum(m_old, jnp.max(t, axis=-1, keepdims=True))
                    alpha = jnp.exp(m_old - m_new)
                    e = jnp.exp(t - _lane_tile(m_new, tq)).astype(BF16)
                    pv = _dot(e, v_ones)
                    acc_ref[a, rows] = alpha * acc_ref[a, rows] + pv[:, :LANES]
                    l_ref[a, rows] = alpha * l_ref[a, rows] + pv[:, LANES:]
                    m_ref[a, rows] = m_new

        for kind_id, subs in _SWEEPS.items():
            pl.when(kind == kind_id)(lambda subs=subs: sweep(subs))

        @pl.when(kind == 2)
        def _():
            o_ref[...] = jnp.where(lo, acc_ref[0] / l_ref[0], acc_ref[1] / l_ref[1]).astype(BF16)
            ln_ref[:, :LANES] = m_ref[0] + jnp.log(l_ref[0])
            ln_ref[:, LANES:] = m_ref[1] + jnp.log(l_ref[1])

        if ride:
            ride.at_last_step((FOX_HEADS // 2, n_steps), ride_in, ride_out, more[2 * n_ride + 6:])

    blk = (tq, LANES)
    by_i = lambda col: (lambda hp, s, ii, jj, kk: (ii[s], col + hp))
    by_j = lambda col: (lambda hp, s, ii, jj, kk: (jj[s], col + hp))
    extra = ride.specs() if ride else _NO_RIDE
    grid_spec = pltpu.PrefetchScalarGridSpec(
        num_scalar_prefetch=3, grid=(FOX_HEADS // 2, n_steps),
        in_specs=[pl.BlockSpec((2 * tq, LANES), by_i(Q_COL)), pl.BlockSpec(blk, by_j(K_COL)),
                  pl.BlockSpec(blk, by_j(V_COL)),
                  pl.BlockSpec((None, 2, tq), lambda hp, s, ii, jj, kk: (hp, 0, jj[s]))] + extra.in_specs,
        out_specs=[pl.BlockSpec((2 * tq, LANES), by_i(0)), pl.BlockSpec((2 * tq, 2 * LANES), by_i(0))] + extra.out_specs,
        scratch_shapes=[pltpu.VMEM((2, 2 * tq, LANES), BF16), pltpu.VMEM((2, 2 * tq, LANES), F32),
                        pltpu.VMEM((2, 2 * tq, LANES), F32), pltpu.VMEM((2, 2 * tq, LANES), F32)] + extra.scratch)
    return pl.pallas_call(
        kern, grid_spec=grid_spec,
        out_shape=[jax.ShapeDtypeStruct((n_tok, 4 * LANES), BF16),
                   jax.ShapeDtypeStruct((n_tok, FOX_HEADS * LANES), F32)] + extra.out_shape,
        name="fox_fwd", compiler_params=_params(2))(ii, jj, kk, zm, zm, zm, c_pairs, *extra.arrays)


def _fox_delta(d_out, out, tm):
    def body(i, ins, consts, outs, accs):
        do_ref, o_ref = ins
        dl_ref, = outs
        lane = lax.broadcasted_iota(jnp.int32, (tm, LANES), 1)
        lo = lane < HEAD_DIM
        for pair in range(FOX_HEADS // 2):
            cols = slice(pair * LANES, (pair + 1) * LANES)
            prod = do_ref[:, cols].astype(F32) * o_ref[:, cols].astype(F32)
            for a in range(2):
                dd = jnp.sum(jnp.where(lo if a == 0 else ~lo, prod, 0.0), axis=-1, keepdims=True)
                h = 2 * pair + a
                dl_ref[:, h * LANES:(h + 1) * LANES] = jnp.broadcast_to(dd, (tm, LANES))

    return _row_call(body, "fox_delta", d_out.shape[0], tm, [d_out, out], [], [(FOX_HEADS * LANES, F32)], [])[0]


def _fox_bwd(zm, c_pairs, d_out, lnorm, delta, tq, ride=None):
    n_tok = zm.shape[0]
    ii, jj, kk = _fox_steps(n_tok // tq)
    n_steps = len(ii)
    n_ride = len(ride.arrays) if ride else 0

    def kern(ii_ref, jj_ref, kk_ref, q_ref, k_ref, v_ref, ck_ref, do_ref, ln_ref, dl_ref, *more):
        ride_in, ride_out = more[:n_ride], more[n_ride + 5:2 * n_ride + 5]
        dq_ref, dk_ref, dv_ref, cs_ref, rs_ref = more[n_ride:n_ride + 5]
        qs_ref, qo_ref, dos_ref, dq_acc = more[2 * n_ride + 5:2 * n_ride + 9]
        step = pl.program_id(1)
        j, kind = jj_ref[step], kk_ref[step]
        lo = lax.broadcasted_iota(jnp.int32, (2 * tq, LANES), 1) < HEAD_DIM
        if ride:
            ride.at_first_step((FOX_HEADS // 2, n_steps), ride_in, ride_out, more[2 * n_ride + 9:])

        @pl.when(step == 0)
        def _():
            dk_ref[...] = jnp.zeros_like(dk_ref)
            dv_ref[...] = jnp.zeros_like(dv_ref)
            cs_ref[...] = jnp.zeros_like(cs_ref)

        @pl.when(j == 0)
        def _():
            q2, do2 = q_ref[...], do_ref[...]
            zq = jnp.zeros_like(q2)
            ones = jnp.ones((2 * tq, LANES), BF16)
            for a in range(2):
                half = lo if a == 0 else ~lo
                qa = jnp.where(half, q2, zq) * SCALE
                qs_ref[a] = qa
                qo_ref[a] = jnp.concatenate([qa, ones], axis=1)
                dos_ref[a] = jnp.where(half, do2, zq)
            dq_acc[...] = jnp.zeros(dq_acc.shape, F32)

        def sweep(subs):
            kv, vv = k_ref[...], v_ref[...]
            k_ones = jnp.concatenate([kv, jnp.ones((tq, LANES), BF16)], axis=1)
            dk, dv, sums = None, None, [None, None]
            for sub, diag in subs:
                rows = slice(sub * tq, (sub + 1) * tq)
                for a in range(2):
                    t = _dot_nt(qs_ref[a, rows], kv) - ck_ref[a:a + 1, :]
                    if diag:
                        t = _causal(t, tq, tq)
                    prob = jnp.exp(t - _lane_tile(ln_ref[rows, a * LANES:(a + 1) * LANES], tq))
                    dp = _dot_nt(dos_ref[a, rows], vv)
                    ds = (prob * (dp - _lane_tile(dl_ref[rows, a * LANES:(a + 1) * LANES], tq))).astype(BF16)
                    dq_acc[a, rows] += _dot(ds, k_ones)
                    dk_cs = _dot_tn(ds, qo_ref[a, rows])
                    dv_a = _dot_tn(prob.astype(BF16), dos_ref[a, rows])
                    dk = dk_cs[:, :LANES] if dk is None else dk + dk_cs[:, :LANES]
                    dv = dv_a if dv is None else dv + dv_a
                    sums[a] = dk_cs[:, LANES:] if sums[a] is None else sums[a] + dk_cs[:, LANES:]
            keys = pl.ds(pl.multiple_of(j * tq, tq), tq)
            dk_ref[keys, :] += dk
            cs_ref[keys, :] += jnp.where(lo[:tq], sums[0], sums[1])
            dv_ref[keys, :] += dv

        for kind_id, subs in _SWEEPS.items():
            pl.when(kind == kind_id)(lambda subs=subs: sweep(subs))

        @pl.when(kind == 2)
        def _():
            dq_ref[...] = jnp.where(lo, dq_acc[0, :, :LANES], dq_acc[1, :, :LANES]) * SCALE
            rs_ref[...] = jnp.where(lo, dq_acc[0, :, LANES:], dq_acc[1, :, LANES:])

        if ride:
            ride.at_last_step((FOX_HEADS // 2, n_steps), ride_in, ride_out, more[2 * n_ride + 9:])

    blk = (tq, LANES)
    by_i = lambda col: (lambda hp, s, ii, jj, kk: (ii[s], col + hp))
    by_j = lambda col: (lambda hp, s, ii, jj, kk: (jj[s], col + hp))
    resident = pl.BlockSpec((2 * tq, LANES), by_i(0))
    stat = pl.BlockSpec((2 * tq, 2 * LANES), by_i(0))
    whole = pl.BlockSpec((n_tok, LANES), lambda hp, s, ii, jj, kk: (0, hp))
    extra = ride.specs() if ride else _NO_RIDE
    grid_spec = pltpu.PrefetchScalarGridSpec(
        num_scalar_prefetch=3, grid=(FOX_HEADS // 2, n_steps),
        in_specs=[pl.BlockSpec((2 * tq, LANES), by_i(Q_COL)), pl.BlockSpec(blk, by_j(K_COL)),
                  pl.BlockSpec(blk, by_j(V_COL)),
                  pl.BlockSpec((None, 2, tq), lambda hp, s, ii, jj, kk: (hp, 0, jj[s])),
                  resident, stat, stat] + extra.in_specs,
        out_specs=[resident, whole, whole, whole, resident] + extra.out_specs,
        scratch_shapes=[pltpu.VMEM((2, 2 * tq, LANES), BF16), pltpu.VMEM((2, 2 * tq, 2 * LANES), BF16),
                        pltpu.VMEM((2, 2 * tq, LANES), BF16), pltpu.VMEM((2, 2 * tq, 2 * LANES), F32)] + extra.scratch)
    wide = jax.ShapeDtypeStruct((n_tok, 4 * LANES), F32)
    return pl.pallas_call(
        kern, grid_spec=grid_spec, out_shape=[wide] * 5 + extra.out_shape, name="fox_bwd",
        compiler_params=_params(2, FOX_BWD_VMEM))(ii, jj, kk, zm, zm, zm, c_pairs, d_out, lnorm, delta, *extra.arrays)


def _all_gather(shards):
    n_w = len(shards)

    def kern(*refs):
        x_refs, out_refs = refs[:n_w], refs[n_w:2 * n_w]
        send_sems, recv_sems, local_sems = refs[2 * n_w:]
        x, y, c = _my_pos()
        me, sibling = (x, y, c), (x, y, 1 - c)
        chips = [(1 - x, y), (x, 1 - y), (1 - x, 1 - y)]

        def slot(w, px, py, pc):
            return out_refs[w].at[4 * px + 2 * py + pc]

        def copy(w, k, block, to, src=None):
            return pltpu.make_async_remote_copy(
                src_ref=slot(w, *block) if src is None else src, dst_ref=slot(w, *block),
                send_sem=send_sems.at[7 * w + k], recv_sem=recv_sems.at[7 * w + k], device_id=to, device_id_type=MESH)

        local, started = [], []
        for w in range(n_w):
            mine = pltpu.make_async_copy(x_refs[w], slot(w, *me), local_sems.at[w])
            mine.start()
            local.append(mine)
            first = [copy(w, 0, me, sibling, src=x_refs[w])]
            first += [copy(w, 1 + k, me, (*chip, c), src=x_refs[w]) for k, chip in enumerate(chips)]
            for cp in first:
                cp.start()
            started += first
        for k, chip in enumerate(chips):
            for w in range(n_w):
                copy(w, 1 + k, (*chip, c), me).wait_recv()
                passed = copy(w, 4 + k, (*chip, c), sibling)
                passed.start()
                started.append(passed)
        for w in range(n_w):
            copy(w, 0, sibling, me).wait_recv()
            for k, chip in enumerate(chips):
                copy(w, 4 + k, (*chip, 1 - c), me).wait_recv()
        for cp in started:
            cp.wait_send()
        for cp in local:
            cp.wait()

    any_spec = pl.BlockSpec(memory_space=pl.ANY)
    return pl.pallas_call(
        kern, out_shape=[jax.ShapeDtypeStruct((N_DEV,) + s.shape, s.dtype) for s in shards],
        in_specs=[any_spec] * n_w, out_specs=[any_spec] * n_w,
        scratch_shapes=[pltpu.SemaphoreType.DMA((7 * n_w,)), pltpu.SemaphoreType.DMA((7 * n_w,)),
                        pltpu.SemaphoreType.DMA((n_w,))],
        name="weight_all_gather")(*shards)


def _small_exchange(small):
    def kern(s_ref, sall_ref, *sems):
        copies = _gather_copies([s_ref], [sall_ref], *sems)
        _start_copies(*copies)
        _finish_copies(*copies)

    any_spec = pl.BlockSpec(memory_space=pl.ANY)
    return pl.pallas_call(
        kern, out_shape=jax.ShapeDtypeStruct((N_DEV,) + small.shape, small.dtype), in_specs=[any_spec],
        out_specs=any_spec, scratch_shapes=_exchange_scratch(1), name="small_grad_exchange")(small)


ADAMW_BLOCK_BYTES = 2 * 1024 * 1024


def _adamw(parts, w, m, v, name):
    n_parts, n_rows, n_cols = parts.shape
    limit = max(8, ADAMW_BLOCK_BYTES // (n_parts * n_cols * parts.dtype.itemsize))
    tr = max(t for t in range(8, n_rows + 1, 8) if n_rows % t == 0 and t <= limit)

    def kern(p_ref, w_ref, m_ref, v_ref, g_out, d_out, m_out, v_out):
        g = p_ref[0].astype(F32)
        for k in range(1, n_parts):
            g = g + p_ref[k].astype(F32)
        m_new = ADAM_B1 * m_ref[...] + (1.0 - ADAM_B1) * g
        v_new = ADAM_B2 * v_ref[...] + (1.0 - ADAM_B2) * jnp.square(g)
        m_hat = m_new / (1.0 - ADAM_B1 ** ADAM_STEP)
        v_hat = v_new / (1.0 - ADAM_B2 ** ADAM_STEP)
        g_out[...] = g
        d_out[...] = -ADAM_LR * (m_hat / (jnp.sqrt(v_hat) + ADAM_EPS) + ADAM_WD * w_ref[...])
        m_out[...] = m_new
        v_out[...] = v_new

    row = pl.BlockSpec((tr, n_cols), lambda i: (i, 0))
    out = jax.ShapeDtypeStruct((n_rows, n_cols), F32)
    return pl.pallas_call(
        kern, grid=(n_rows // tr,),
        in_specs=[pl.BlockSpec((n_parts, tr, n_cols), lambda i: (0, i, 0)), row, row, row],
        out_specs=[row, row, row, row], out_shape=[out, out, out, out], name=name,
        compiler_params=_params(1))(parts, w, m, v)


SHARDED = {
    "w_in": ((D_MODEL, D_IN), 1), "w_br_swa": ((512, D_MODEL), 1), "w_br_fox": ((512, D_MODEL), 1),
    "w_mix_out": ((D_MODEL, D_MODEL), 0), "w_ff1": ((D_MODEL, D_FF), 1), "w_ff2": ((D_FF, D_MODEL), 0),
    "w_ple_gate": ((D_MODEL, D_MODEL), 0), "w_ple_proj": ((PLE_DIM, D_MODEL), 1),
}
W_IN_SHARD = D_IN // N_DEV
W_IN_PAD = 640
SMALL = ("g_mix", "g_mlp", "g_ple", "g_final", "b_forget", "swa_sinks")
SMALL_COLS = 1024


def _wire_shard(name, a):
    a = a.reshape(a.shape[-2:])
    return jnp.pad(a, ((0, 0), (0, W_IN_PAD - W_IN_SHARD))) if name == "w_in" else a


def _from_wire(name, a):
    return (a[:, :W_IN_SHARD] if name == "w_in" else a)[None]


def _w_all_from_wire(stacked):
    w_in = jnp.concatenate([stacked[d][:, :W_IN_SHARD] for d in range(N_DEV)], axis=1)
    fpad = jnp.zeros((D_MODEL, N_FPAD - FOX_HEADS), stacked.dtype)
    return jnp.concatenate([w_in[:, :N_MAIN + FOX_HEADS], fpad, w_in[:, N_MAIN + FOX_HEADS:]], axis=1)


def _dw_in_to_wire(dw_all):
    dw_in = jnp.concatenate([dw_all[:, :N_MAIN + FOX_HEADS], dw_all[:, N_MAIN + N_FPAD:]], axis=1)
    pad = jnp.zeros((D_MODEL, W_IN_PAD - W_IN_SHARD), dw_all.dtype)
    return jnp.stack([jnp.concatenate([dw_in[:, d * W_IN_SHARD:(d + 1) * W_IN_SHARD], pad], axis=1)
                      for d in range(N_DEV)])


def _pack_small(vals):
    rows = [jnp.pad(vals[n].reshape(-1), (0, SMALL_COLS - vals[n].size)) for n in SMALL]
    rows += [jnp.zeros((SMALL_COLS,), F32)] * (8 - len(SMALL))
    return jnp.stack(rows)


def _unpack_small(slab, like):
    return {n: slab[r, :like[n].size].reshape(like[n].shape) for r, n in enumerate(SMALL)}


def _local_step(x, p, tgt, w, small, tm, tq, ts, late_shards=None):
    n_tok = x.shape[0]
    row = lambda v: v.reshape(1, -1)
    g_mix, g_mlp, g_ple, g_fin = row(small["g_mix"]), row(small["g_mlp"]), row(small["g_ple"]), row(small["g_final"])
    sinks = small["swa_sinks"].reshape(-1)
    b_col = small["b_forget"].reshape(FOX_HEADS, 1)

    u1, zm, zfg, zf = _in_proj(x, g_mix, w["w_all"], tm)
    f_t = zf[:, :FOX_HEADS].T
    c_pairs = _decay_cumsum(f_t, b_col).reshape(FOX_HEADS // 2, 2, n_tok)
    attn_a, lse_a = _swa_fwd(zm, sinks)
    if late_shards is None:
        attn_b, ln_b = _fox_fwd(zm, c_pairs, tq)
    else:
        attn_b, ln_b, *late = _fox_fwd(zm, c_pairs, tq, _gather_ride(list(late_shards.values())))
        w = {**w, **_gathered_to_local(dict(zip(late_shards, late)))}
    ya, yb, mixed, h1, u2 = _mix_fwd(attn_a, attn_b, zfg, x, w["w_br_swa"], w["w_br_fox"], w["w_mix_out"], g_mlp, tm)
    a, r, h2 = _ffn_fwd(u2, h1, w["w_ff1"], w["w_ff2"], tm // 2)
    dh3, dlg, dpp, u3, loss_acc, dgf = _head_fwd_bwd(h2, p, tgt, g_ple, w["w_ple_gate"], w["w_ple_proj"], g_fin, tm)

    dh2, dh2b, da, dgp = _ffn_bwd_a(dlg, dh3, h2, a, w["w_ple_gate"], g_ple, w["w_ff2"], tm // 2)
    dh1, dh1b, dgl, dya, dyb, daa, dab, dgm = _ffn_bwd_b(
        da, dh2, h1, ya, yb, zfg, w["w_ff1"], g_mlp, w["w_mix_out"], w["w_br_swa"], w["w_br_fox"], tm // 2)
    dq_a, dkp, dkc, dvp, dvc, dsk = _swa_bwd(zm, sinks, daa, attn_a, lse_a)
    delta_b = _fox_delta(dab, attn_b, tm)
    dw = {
        "w_br_swa": _matmul_tn(attn_a, dya, "dw_br_swa", ts, stack_cols=D_MODEL // N_DEV),
        "w_br_fox": _matmul_tn(attn_b, dyb, "dw_br_fox", ts, stack_cols=D_MODEL // N_DEV),
        "w_mix_out": _matmul_tn(mixed, dh1b, "dw_mix_out", ts),
        "w_ff1": _matmul_tn(u2, da, "dw_ff1", ts, stack_cols=D_FF // N_DEV),
        "w_ff2": _matmul_tn(r, dh2b, "dw_ff2", ts),
        "w_ple_gate": _matmul_tn(u3, dlg, "dw_ple_gate", ts),
        "w_ple_proj": _matmul_tn(p, dpp, "dw_ple_proj", ts, stack_cols=D_MODEL // N_DEV),
    }
    if late_shards is None:
        dq_b, dk_b, dv_b, cs, rs = _fox_bwd(zm, c_pairs, dab, ln_b, delta_b, tq)
        late_parts = None
    else:
        wire = _local_to_wire(dw)
        dq_b, dk_b, dv_b, cs, rs, *parts = _fox_bwd(zm, c_pairs, dab, ln_b, delta_b, tq,
                                                    _scatter_ride([wire[n] for n in late_shards]))
        late_parts = dict(zip(late_shards, parts))

    up = lambda t: jnp.concatenate([t[SWA_BLOCK:], jnp.zeros((SWA_BLOCK, LANES), F32)], axis=0)
    dk_a, dv_a = dkc + up(dkp), dvc + up(dvp)
    df_t, db = _decay_bwd(cs, rs, f_t, b_col)
    df = jnp.pad(df_t.T, ((0, 0), (0, N_FPAD - FOX_HEADS)))
    dz = jnp.concatenate([dq_a, dk_a.astype(BF16), dv_a.astype(BF16), dq_b.astype(BF16), dk_b.astype(BF16), dv_b.astype(BF16),
                          df.astype(BF16), dgl], axis=1)
    dw["w_all"] = _matmul_tn(u1, dz, "dw_in", ts)
    if late_shards is None:
        dx, dgx = _in_proj_bwd(dz, dh1, x, w["w_all"], g_mix, tm)
    else:
        dx, dgx, late_parts["w_in"] = _in_proj_bwd(dz, dh1, x, w["w_all"], g_mix, tm,
                                                   _scatter_ride([_dw_in_to_wire(dw["w_all"])]))
    dsmall = {"g_mix": dgx[0], "g_mlp": dgm[0], "g_ple": dgp[0], "g_final": dgf[0],
              "b_forget": db[:, 0], "swa_sinks": dsk[:, 0]}
    return loss_acc[0, 0], dx, dw, dsmall, late_parts


_ROWS = lambda t: t.reshape(-1, t.shape[-1])
_BY_ROWS = lambda t: t.reshape(N_DEV, t.shape[0] // N_DEV, t.shape[1])
_SAME = lambda t: t
LOCAL_LAYOUT = {
    "w_in": ("w_all", _w_all_from_wire, _dw_in_to_wire), "w_br_swa": ("w_br_swa", _SAME, _SAME),
    "w_br_fox": ("w_br_fox", _SAME, _SAME), "w_mix_out": ("w_mix_out", _ROWS, _BY_ROWS),
    "w_ff1": ("w_ff1", _SAME, _SAME), "w_ff2": ("w_ff2", _SAME, _BY_ROWS),
    "w_ple_gate": ("w_ple_gate", _ROWS, _BY_ROWS), "w_ple_proj": ("w_ple_proj", _SAME, _SAME),
}


def _gathered_to_local(g):
    return {LOCAL_LAYOUT[n][0]: LOCAL_LAYOUT[n][1](t) for n, t in g.items()}


def _local_to_wire(dw):
    names = {local: n for n, (local, _, _) in LOCAL_LAYOUT.items()}
    return {names[local]: LOCAL_LAYOUT[names[local]][2](t) for local, t in dw.items()}


def kernel(x, p, g_mix, w_in, b_forget, swa_sinks, w_br_swa, w_br_fox, w_mix_out, g_mlp, w_ff1, w_ff2, g_ple, w_ple_gate, w_ple_proj, g_final, loss_target, m_g_mix, m_w_in, m_b_forget, m_swa_sinks, m_w_br_swa, m_w_br_fox, m_w_mix_out, m_g_mlp, m_w_ff1, m_w_ff2, m_g_ple, m_w_ple_gate, m_w_ple_proj, m_g_final, v_g_mix, v_w_in, v_b_forget, v_swa_sinks, v_w_br_swa, v_w_br_fox, v_w_mix_out, v_g_mlp, v_w_ff1, v_w_ff2, v_g_ple, v_w_ple_gate, v_w_ple_proj, v_g_final):
    given = dict(g_mix=g_mix, w_in=w_in, b_forget=b_forget, swa_sinks=swa_sinks, w_br_swa=w_br_swa, w_br_fox=w_br_fox,
                 w_mix_out=w_mix_out, g_mlp=g_mlp, w_ff1=w_ff1, w_ff2=w_ff2, g_ple=g_ple, w_ple_gate=w_ple_gate,
                 w_ple_proj=w_ple_proj, g_final=g_final)
    mom = dict(g_mix=m_g_mix, w_in=m_w_in, b_forget=m_b_forget, swa_sinks=m_swa_sinks, w_br_swa=m_w_br_swa,
               w_br_fox=m_w_br_fox, w_mix_out=m_w_mix_out, g_mlp=m_g_mlp, w_ff1=m_w_ff1, w_ff2=m_w_ff2, g_ple=m_g_ple,
               w_ple_gate=m_w_ple_gate, w_ple_proj=m_w_ple_proj, g_final=m_g_final)
    vel = dict(g_mix=v_g_mix, w_in=v_w_in, b_forget=v_b_forget, swa_sinks=v_swa_sinks, w_br_swa=v_w_br_swa,
               w_br_fox=v_w_br_fox, w_mix_out=v_w_mix_out, g_mlp=v_g_mlp, w_ff1=v_w_ff1, w_ff2=v_w_ff2, g_ple=v_g_ple,
               w_ple_gate=v_w_ple_gate, w_ple_proj=v_w_ple_proj, g_final=v_g_final)
    names = list(given)
    sharded = list(SHARDED)

    w_wire = {n: _wire_shard(n, given[n]) for n in sharded}
    late = [n for n in sharded if n != "w_in"]
    gathered = _all_gather([w_wire["w_in"].astype(BF16)])
    local_w = _gathered_to_local({"w_in": gathered[0]})
    small = {n: given[n].reshape(-1) for n in SMALL}

    n_tok = x.shape[1]
    tile = min(512, n_tok // 4)
    loss_part, dx, dw, dsmall, parts = _local_step(
        x[0], p[0, 0], loss_target[0], local_w, small, tm=tile, tq=tile, ts=min(2048, n_tok // 4),
        late_shards={n: w_wire[n].astype(BF16) for n in late})
    loss = lax.psum(loss_part, AXES)

    small_all = _small_exchange(_pack_small(dsmall))

    res = {}
    for n in sharded:
        part = parts[n]
        flat = part.reshape(N_DEV, -1, part.shape[-1])
        outs = _adamw(flat, w_wire[n], _wire_shard(n, mom[n]), _wire_shard(n, vel[n]), "adamw_" + n)
        res[n] = [_from_wire(n, o) for o in outs]
    outs_s = _adamw(small_all, _pack_small(small), _pack_small({n: mom[n] for n in SMALL}),
                    _pack_small({n: vel[n] for n in SMALL}), "adamw_small")
    small_res = [_unpack_small(o, given) for o in outs_s]

    groups = [[res[n][k] if n in res else small_res[k][n] for n in names] for k in range(4)]
    return (loss, dx[None], *groups[0], *groups[1], *groups[2], *groups[3])
```

```python
import numpy as np
import jax
import jax.numpy as jnp
from jax import lax
from jax.experimental import pallas as pl
from jax.experimental.pallas import tpu as pltpu

F32 = jnp.float32
BF16 = jnp.bfloat16

D_MODEL = 1024
HEAD_DIM = 64
SWA_HEADS = 8
FOX_HEADS = 8
CHUNK_SHIFT = 6
SWA_BLOCK = 128
WINDOW_CHUNKS = 2
D_FF = 4096
PLE_DIM = 256
RMS_EPS = 1e-6
N_MAIN = 2304
N_FPAD = 128
N_GATE = 2048
N_ALL = N_MAIN + N_FPAD + N_GATE
D_IN = N_MAIN + FOX_HEADS + N_GATE
SCALE = HEAD_DIM ** -0.5
NEG = -1e30

ADAM_LR = 0.001
ADAM_B1 = 0.9
ADAM_B2 = 0.999
ADAM_EPS = 1e-08
ADAM_WD = 0.01
ADAM_STEP = 10

N_DEV = 8
LANES = 128
V7X_VMEM_BYTES = 64 * 1024 * 1024
VMEM_LIMIT = V7X_VMEM_BYTES * 3 // 4
FOX_BWD_VMEM = V7X_VMEM_BYTES * 7 // 8
MESH = pl.DeviceIdType.MESH
AXES = ("x", "y", "c")

_NT = (((1,), (1,)), ((), ()))
_TN = (((0,), (0,)), ((), ()))


def _params(n_grid, vmem_limit=VMEM_LIMIT):
    return pltpu.CompilerParams(dimension_semantics=("arbitrary",) * n_grid, vmem_limit_bytes=vmem_limit)


def _chunks(n, step):
    return [(s, min(step, n - s)) for s in range(0, n, step)]


def _sigmoid(x):
    return 1.0 / (1.0 + jnp.exp(-x))


def _dot(a, b):
    return jnp.dot(a, b, preferred_element_type=F32)


def _dot_nt(a, b):
    return lax.dot_general(a, b, _NT, preferred_element_type=F32)


def _dot_tn(a, b):
    return lax.dot_general(a, b, _TN, preferred_element_type=F32)


def _lane_concat(stacked_ref):
    return jnp.concatenate([stacked_ref[d] for d in range(N_DEV)], axis=1)


def _rms(h):
    return lax.rsqrt(jnp.mean(h * h, axis=-1, keepdims=True) + RMS_EPS)


def _rms_bwd(h, g, du):
    rs = _rms(h)
    n = h * rs
    dn = du * g
    dh = rs * (dn - n * jnp.mean(dn * n, axis=-1, keepdims=True))
    return dh, jnp.sum(du * n, axis=0, keepdims=True)


def _acc_rows(ref, i, row):
    @pl.when(i == 0)
    def _():
        ref[...] = jnp.zeros_like(ref)
    ref[...] += jnp.broadcast_to(row, ref.shape)


def _row_call(body, name, n_rows, tm, row_ins, const_ins, row_outs, acc_outs, ride=None):
    n_ri, n_ci, n_ro, n_ao = len(row_ins), len(const_ins), len(row_outs), len(acc_outs)
    extra = ride if ride else _NO_RIDE
    n_ride = len(extra.arrays)
    grid = (n_rows // tm,)

    def kern(*refs):
        i = pl.program_id(0)
        ins, refs = refs[:n_ri + n_ci], refs[n_ri + n_ci:]
        ride_in, refs = refs[:n_ride], refs[n_ride:]
        outs, refs = refs[:n_ro + n_ao], refs[n_ro + n_ao:]
        ride_out, sems = refs[:n_ride], refs[n_ride:]
        if ride:
            ride.at_first_step(grid, ride_in, ride_out, sems)
        body(i, ins[:n_ri], ins[n_ri:], outs[:n_ro], outs[n_ro:])
        if ride:
            ride.at_last_step(grid, ride_in, ride_out, sems)

    def whole(a):
        zeros = (0,) * a.ndim
        return pl.BlockSpec(a.shape, lambda i: zeros, pipeline_mode=pl.Buffered(1))

    in_specs = [pl.BlockSpec((tm, a.shape[1]), lambda i: (i, 0)) for a in row_ins]
    in_specs += [whole(a) for a in const_ins] + extra.in_specs
    out_specs = [pl.BlockSpec((tm, c), lambda i: (i, 0)) for c, _ in row_outs]
    out_specs += [pl.BlockSpec((8, c), lambda i: (0, 0)) for c in acc_outs] + extra.out_specs
    out_shape = [jax.ShapeDtypeStruct((n_rows, c), dt) for c, dt in row_outs]
    out_shape += [jax.ShapeDtypeStruct((8, c), F32) for c in acc_outs] + extra.out_shape
    return pl.pallas_call(kern, grid=grid, in_specs=in_specs, out_specs=out_specs, out_shape=out_shape,
                          scratch_shapes=extra.scratch, name=name,
                          compiler_params=_params(1))(*row_ins, *const_ins, *extra.arrays)


def _in_proj(x, g_mix, w_all, tm):
    def body(i, ins, consts, outs, accs):
        x_ref, = ins
        g_ref, w_ref = consts
        u_ref, zm_ref, zfg_ref, zf_ref = outs
        xv = x_ref[...]
        u = ((xv * _rms(xv)) * g_ref[...]).astype(BF16)
        u_ref[...] = u
        for s, n in _chunks(N_MAIN, 768):
            zm_ref[:, s:s + n] = _dot(u, w_ref[:, s:s + n]).astype(BF16)
        for s, n in _chunks(N_FPAD + N_GATE, 512):
            zfg_ref[:, s:s + n] = _dot(u, w_ref[:, N_MAIN + s:N_MAIN + s + n])
        zf_ref[...] = zfg_ref[:, :N_FPAD]

    return _row_call(body, "in_proj", x.shape[0], tm, [x], [g_mix, w_all],
                     [(D_MODEL, BF16), (N_MAIN, BF16), (N_FPAD + N_GATE, F32), (N_FPAD, F32)], [])


def _mix_fwd(attn_a, attn_b, zfg, x, w_sa, w_fo, w_mo, g_mlp, tm):
    def body(i, ins, consts, outs, accs):
        aa_ref, ab_ref, zfg_ref, x_ref = ins
        wsa_ref, wfo_ref, wmo_ref, g_ref = consts
        ya_ref, yb_ref, mx_ref, h1_ref, u2_ref = outs
        ya = _dot(aa_ref[...], _lane_concat(wsa_ref))
        yb = _dot(ab_ref[...], _lane_concat(wfo_ref))
        g0 = _sigmoid(zfg_ref[:, N_FPAD:N_FPAD + D_MODEL])
        g1 = _sigmoid(zfg_ref[:, N_FPAD + D_MODEL:N_FPAD + 2 * D_MODEL])
        mixed = (g0 * ya + g1 * yb).astype(BF16)
        ya_ref[...] = ya.astype(BF16)
        yb_ref[...] = yb.astype(BF16)
        mx_ref[...] = mixed
        h1 = x_ref[...] + _dot(mixed, wmo_ref[...])
        h1_ref[...] = h1
        u2_ref[...] = ((h1 * _rms(h1)) * g_ref[...]).astype(BF16)

    return _row_call(body, "mix_fwd", x.shape[0], tm, [attn_a, attn_b, zfg, x], [w_sa, w_fo, w_mo, g_mlp],
                     [(D_MODEL, BF16), (D_MODEL, BF16), (D_MODEL, BF16), (D_MODEL, F32), (D_MODEL, BF16)], [])


def _ffn_fwd(u2, h1, w1s, w2s, tm):
    ch = D_FF // N_DEV

    def body(i, ins, consts, outs, accs):
        u_ref, h1_ref = ins
        w1_ref, w2_ref = consts
        a_ref, r_ref, h2_ref = outs
        u = u_ref[...]
        acc = h1_ref[...]
        for c in range(N_DEV):
            a = _dot(u, w1_ref[c])
            a_ref[:, c * ch:(c + 1) * ch] = a.astype(BF16)
            r = jnp.square(jnp.maximum(a, 0.0)).astype(BF16)
            r_ref[:, c * ch:(c + 1) * ch] = r
            acc = acc + _dot(r, w2_ref[c])
        h2_ref[...] = acc

    return _row_call(body, "ffn_fwd", u2.shape[0], tm, [u2, h1], [w1s, w2s],
                     [(D_FF, BF16), (D_FF, BF16), (D_MODEL, F32)], [])


def _head_fwd_bwd(h2, p, tgt, g_ple, w_pg, w_pp, g_fin, tm):
    def body(i, ins, consts, outs, accs):
        h2_ref, p_ref, t_ref = ins
        gp_ref, wpg_ref, wpp_ref, gf_ref = consts
        dh3_ref, dlg_ref, dpp_ref, u3_ref = outs
        loss_ref, dgf_ref = accs
        h2 = h2_ref[...]
        u3 = ((h2 * _rms(h2)) * gp_ref[...]).astype(BF16)
        u3_ref[...] = u3
        pg = _sigmoid(_dot(u3, wpg_ref[...]))
        pp = _dot(p_ref[...].astype(BF16), _lane_concat(wpp_ref))
        h3 = h2 + pg * pp
        rs3 = _rms(h3)
        n3 = h3 * rs3
        gf = gf_ref[...]
        err = n3 * gf - t_ref[...]
        row_loss = 0.5 * jnp.mean(err * err, axis=-1, keepdims=True)
        _acc_rows(loss_ref, i, jnp.broadcast_to(jnp.sum(row_loss, axis=0, keepdims=True), (1, LANES)))
        dy = err * (1.0 / D_MODEL)
        _acc_rows(dgf_ref, i, jnp.sum(dy * n3, axis=0, keepdims=True))
        dn = dy * gf
        dh3 = rs3 * (dn - n3 * jnp.mean(dn * n3, axis=-1, keepdims=True))
        dh3_ref[...] = dh3
        dpp_ref[...] = (dh3 * pg).astype(BF16)
        dlg_ref[...] = ((dh3 * pp) * pg * (1.0 - pg)).astype(BF16)

    return _row_call(body, "head_fwd_bwd", h2.shape[0], tm, [h2, p, tgt], [g_ple, w_pg, w_pp, g_fin],
                     [(D_MODEL, F32), (D_MODEL, BF16), (D_MODEL, BF16), (D_MODEL, BF16)], [LANES, D_MODEL])


def _ffn_bwd_a(dlg, dh3, h2, a, w_pg, g_ple, w2s, tm):
    ch = D_FF // N_DEV

    def body(i, ins, consts, outs, accs):
        dlg_ref, dh3_ref, h2_ref, a_ref = ins
        wpg_ref, gp_ref, w2_ref = consts
        dh2_ref, dh2b_ref, da_ref = outs
        dgp_ref, = accs
        du3 = _dot_nt(dlg_ref[...], wpg_ref[...])
        dh, dg = _rms_bwd(h2_ref[...], gp_ref[...], du3)
        _acc_rows(dgp_ref, i, dg)
        dh2 = dh3_ref[...] + dh
        dh2_ref[...] = dh2
        dh2b = dh2.astype(BF16)
        dh2b_ref[...] = dh2b
        for c in range(N_DEV):
            dr = _dot_nt(dh2b, w2_ref[c])
            av = a_ref[:, c * ch:(c + 1) * ch].astype(F32)
            da_ref[:, c * ch:(c + 1) * ch] = (dr * (2.0 * jnp.maximum(av, 0.0))).astype(BF16)

    return _row_call(body, "ffn_bwd_a", h2.shape[0], tm, [dlg, dh3, h2, a], [w_pg, g_ple, w2s],
                     [(D_MODEL, F32), (D_MODEL, BF16), (D_FF, BF16)], [D_MODEL])


def _ffn_bwd_b(da, dh2, h1, ya, yb, zfg, w1s, g_mlp, w_mo, w_sa, w_fo, tm):
    ch = D_FF // N_DEV

    def body(i, ins, consts, outs, accs):
        da_ref, dh2_ref, h1_ref, ya_ref, yb_ref, zfg_ref = ins
        w1_ref, gm_ref, wmo_ref, wsa_ref, wfo_ref = consts
        dh1_ref, dh1b_ref, dgl_ref, dya_ref, dyb_ref, daa_ref, dab_ref = outs
        dgm_ref, = accs
        du2 = _dot_nt(da_ref[:, 0:ch], w1_ref[0])
        for c in range(1, N_DEV):
            du2 = du2 + _dot_nt(da_ref[:, c * ch:(c + 1) * ch], w1_ref[c])
        dh, dg = _rms_bwd(h1_ref[...], gm_ref[...], du2)
        _acc_rows(dgm_ref, i, dg)
        dh1 = dh2_ref[...] + dh
        dh1_ref[...] = dh1
        dh1b = dh1.astype(BF16)
        dh1b_ref[...] = dh1b
        dmx = _dot_nt(dh1b, wmo_ref[...])
        g0 = _sigmoid(zfg_ref[:, N_FPAD:N_FPAD + D_MODEL])
        g1 = _sigmoid(zfg_ref[:, N_FPAD + D_MODEL:N_FPAD + 2 * D_MODEL])
        dya = (dmx * g0).astype(BF16)
        dyb = (dmx * g1).astype(BF16)
        dya_ref[...] = dya
        dyb_ref[...] = dyb
        dgl_ref[:, 0:D_MODEL] = ((dmx * ya_ref[...].astype(F32)) * g0 * (1.0 - g0)).astype(BF16)
        dgl_ref[:, D_MODEL:2 * D_MODEL] = ((dmx * yb_ref[...].astype(F32)) * g1 * (1.0 - g1)).astype(BF16)
        daa_ref[...] = _dot_nt(dya, _lane_concat(wsa_ref)).astype(BF16)
        dab_ref[...] = _dot_nt(dyb, _lane_concat(wfo_ref)).astype(BF16)

    half = D_MODEL // 2
    return _row_call(body, "ffn_bwd_b", h1.shape[0], tm, [da, dh2, h1, ya, yb, zfg],
                     [w1s, g_mlp, w_mo, w_sa, w_fo],
                     [(D_MODEL, F32), (D_MODEL, BF16), (N_GATE, BF16), (D_MODEL, BF16), (D_MODEL, BF16),
                      (half, BF16), (half, BF16)], [D_MODEL])


def _in_proj_bwd(dz, dh1, x, w_all, g_mix, tm, ride=None):
    def body(i, ins, consts, outs, accs):
        dz_ref, dh1_ref, x_ref = ins
        w_ref, g_ref = consts
        dx_ref, = outs
        dgx_ref, = accs
        du1 = _dot_nt(dz_ref[...], w_ref[...])
        dh, dg = _rms_bwd(x_ref[...], g_ref[...], du1)
        _acc_rows(dgx_ref, i, dg)
        dx_ref[...] = dh1_ref[...] + dh

    return _row_call(body, "in_proj_bwd", x.shape[0], tm, [dz, dh1, x], [w_all, g_mix],
                     [(D_MODEL, F32)], [D_MODEL], ride)


def _matmul_tn(a, b, name, ts, stack_cols=0):
    n_rows, ka = a.shape
    n = b.shape[1]
    tk = min(ka, 1024)
    tn = 896 if n % 1024 else 1024
    n_stack = tn // stack_cols if stack_cols else 0
    assert ka % tk == 0 and n % tn == 0 and n_rows % ts == 0 and (not stack_cols or tk == ka)
    n_steps = n_rows // ts

    def kern(a_ref, b_ref, o_ref, acc_ref):
        s = pl.program_id(2)

        @pl.when(s == 0)
        def _():
            acc_ref[...] = jnp.zeros_like(acc_ref)
        acc_ref[...] += _dot_tn(a_ref[...].astype(BF16), b_ref[...])

        @pl.when(s == n_steps - 1)
        def _():
            if stack_cols:
                for c in range(n_stack):
                    o_ref[c] = acc_ref[:, c * stack_cols:(c + 1) * stack_cols].astype(BF16)
            else:
                o_ref[...] = acc_ref[...].astype(BF16)

    if stack_cols:
        out_spec = pl.BlockSpec((n_stack, tk, stack_cols), lambda i, j, s: (j, 0, 0))
        out_shape = jax.ShapeDtypeStruct((n // stack_cols, ka, stack_cols), BF16)
    else:
        out_spec = pl.BlockSpec((tk, tn), lambda i, j, s: (i, j))
        out_shape = jax.ShapeDtypeStruct((ka, n), BF16)
    return pl.pallas_call(
        kern, grid=(ka // tk, n // tn, n_steps),
        in_specs=[pl.BlockSpec((ts, tk), lambda i, j, s: (s, i)), pl.BlockSpec((ts, tn), lambda i, j, s: (s, j))],
        out_specs=out_spec, out_shape=out_shape, scratch_shapes=[pltpu.VMEM((tk, tn), F32)], name=name,
        compiler_params=_params(3))(a, b)


SCAN_CHUNK = 512


def _decay_cumsum(f_t, b_col):
    n_tok = f_t.shape[1]
    ch = min(SCAN_CHUNK, n_tok)

    def kern(f_ref, b_ref, c_ref):
        r = lax.broadcasted_iota(jnp.int32, (ch, ch), 0)
        c = lax.broadcasted_iota(jnp.int32, (ch, ch), 1)
        tri = (r <= c).astype(F32)
        carry = jnp.zeros((8, 1), F32)
        for k in range(n_tok // ch):
            xv = f_ref[:, k * ch:(k + 1) * ch] + b_ref[...]
            lf = jnp.minimum(xv, 0.0) - jnp.log(1.0 + jnp.exp(-jnp.abs(xv)))
            cs = jnp.dot(lf, tri, precision=lax.Precision.HIGHEST, preferred_element_type=F32) + carry
            c_ref[:, k * ch:(k + 1) * ch] = cs
            carry = cs[:, ch - 1:ch]

    return pl.pallas_call(kern, out_shape=jax.ShapeDtypeStruct((8, n_tok), F32), name="decay_cumsum",
                          compiler_params=_params(0))(f_t, b_col)


def _decay_bwd(cs, rs, f_t, b_col):
    n_tok = f_t.shape[1]
    ch = min(SCAN_CHUNK, n_tok)
    n_ch = n_tok // ch

    def kern(cs_ref, rs_ref, f_ref, b_ref, df_ref, db_ref, carry_ref):
        k = pl.program_id(0)

        @pl.when(k == 0)
        def _():
            carry_ref[...] = jnp.zeros_like(carry_ref)
            db_ref[...] = jnp.zeros_like(db_ref)

        r = lax.broadcasted_iota(jnp.int32, (ch, ch), 0)
        c = lax.broadcasted_iota(jnp.int32, (ch, ch), 1)
        tri = (r >= c).astype(F32)
        head = lax.broadcasted_iota(jnp.int32, (8, 4 * LANES), 0)
        lane = lax.broadcasted_iota(jnp.int32, (8, 4 * LANES), 1)
        pick = (lane == HEAD_DIM * head).astype(F32)
        dc = lax.dot_general(pick, rs_ref[...] - cs_ref[...], _NT, precision=lax.Precision.HIGHEST,
                             preferred_element_type=F32)
        rc = jnp.dot(dc, tri, precision=lax.Precision.HIGHEST, preferred_element_type=F32) + carry_ref[:, 0:1]
        carry_ref[...] = jnp.broadcast_to(rc[:, 0:1], carry_ref.shape)
        df = rc / (1.0 + jnp.exp(f_ref[...] + b_ref[...]))
        df_ref[...] = df
        db_ref[...] += jnp.broadcast_to(jnp.sum(df, axis=1, keepdims=True), db_ref.shape)

    back = lambda k: n_ch - 1 - k
    wide = pl.BlockSpec((ch, 4 * LANES), lambda k: (back(k), 0))
    row = pl.BlockSpec((8, ch), lambda k: (0, back(k)))
    return pl.pallas_call(
        kern, grid=(n_ch,),
        in_specs=[wide, wide, row, pl.BlockSpec((8, 1), lambda k: (0, 0))],
        out_specs=[row, pl.BlockSpec((8, LANES), lambda k: (0, 0))],
        out_shape=[jax.ShapeDtypeStruct((8, n_tok), F32), jax.ShapeDtypeStruct((8, LANES), F32)],
        scratch_shapes=[pltpu.VMEM((8, LANES), F32)], name="decay_bwd", compiler_params=_params(1))(cs, rs, f_t, b_col)


def _swa_band_mask(n):
    row = lax.broadcasted_iota(jnp.int32, (SWA_BLOCK, 2 * SWA_BLOCK), 0) + SWA_BLOCK
    col = lax.broadcasted_iota(jnp.int32, (SWA_BLOCK, 2 * SWA_BLOCK), 1)
    cd = (row >> CHUNK_SHIFT) - (col >> CHUNK_SHIFT)
    first_real = jnp.where(n > 0, 0, SWA_BLOCK)
    ok = (cd >= 0) & (cd <= WINDOW_CHUNKS) & (col >= first_real)
    dist = jnp.abs(row - col).astype(F32)
    return ok, dist


def _swap_halves(t):
    return pltpu.roll(t.astype(F32), HEAD_DIM, axis=1).astype(t.dtype)


def _swa_specs():
    blk = SWA_BLOCK
    q = pl.BlockSpec((blk, 4 * LANES), lambda n: (n, 0))
    kp = pl.BlockSpec((blk, LANES), lambda n: (jnp.maximum(n - 1, 0), 4))
    kc = pl.BlockSpec((blk, LANES), lambda n: (n, 4))
    vp = pl.BlockSpec((blk, LANES), lambda n: (jnp.maximum(n - 1, 0), 5))
    vc = pl.BlockSpec((blk, LANES), lambda n: (n, 5))
    return q, kp, kc, vp, vc


SWA_GROUPS = ([h for h in range(SWA_HEADS) if h % 2 == h // 4], [h for h in range(SWA_HEADS) if h % 2 != h // 4])


def _stack_heads(ref, heads, lo, mask_halves):
    tiles = []
    for h in heads:
        t = ref[:, (h // 2) * LANES:(h // 2 + 1) * LANES]
        tiles.append(jnp.where(lo if h % 2 == 0 else ~lo, t, jnp.zeros_like(t)) if mask_halves else t)
    return jnp.concatenate(tiles, axis=0)


def _per_head_column(values, heads):
    return jnp.concatenate([jnp.full((SWA_BLOCK, 1), values(h), F32) for h in heads], axis=0)


def _swa_scores(q_ref, kx, heads, lo, ok, dist):
    qa = _stack_heads(q_ref, heads, lo, True) * SCALE
    s = _dot_nt(qa, kx) - _per_head_column(lambda h: 2.0 ** -(h + 1), heads) * dist
    return qa, jnp.where(ok, s, NEG)


def _swa_fwd(zm, sinks):
    n_tok = zm.shape[0]
    blk = SWA_BLOCK

    def kern(q_ref, kp_ref, kc_ref, vp_ref, vc_ref, sink_ref, o_ref, lse_ref):
        n = pl.program_id(0)
        ok, dist = _swa_band_mask(n)
        k2 = jnp.concatenate([kp_ref[...], kc_ref[...]], axis=0)
        v2 = jnp.concatenate([vp_ref[...], vc_ref[...]], axis=0)
        ksw, vsw = _swap_halves(k2), _swap_halves(v2)
        lane = lax.broadcasted_iota(jnp.int32, (blk, LANES), 1)
        lo = lane < HEAD_DIM
        lse_t = jnp.zeros((blk, LANES), F32)
        for pair in range(SWA_HEADS // 2):
            q2 = q_ref[:, pair * LANES:(pair + 1) * LANES]
            outs = []
            for a in range(2):
                h = 2 * pair + a
                qa = jnp.where(lo if a == 0 else ~lo, q2, jnp.zeros_like(q2)) * SCALE
                kx, vx = (k2, v2) if h in SWA_GROUPS[0] else (ksw, vsw)
                s = _dot_nt(qa, kx)
                s = jnp.where(ok, s - (2.0 ** -(h + 1)) * dist, NEG)
                sink = sink_ref[h]
                m = jnp.maximum(jnp.max(s, axis=-1, keepdims=True), sink)
                e = jnp.exp(s - m)
                l = jnp.sum(e, axis=-1, keepdims=True) + jnp.exp(sink - m)
                pn = (e * (1.0 / l)).astype(BF16)
                outs.append(_dot(pn, vx))
                lse_t = jnp.where(lane == h, m + jnp.log(l), lse_t)
            o_ref[:, pair * LANES:(pair + 1) * LANES] = jnp.where(lo, outs[0], outs[1]).astype(BF16)
        lse_ref[...] = lse_t

    q, kp, kc, vp, vc = _swa_specs()
    return pl.pallas_call(
        kern, grid=(n_tok // blk,),
        in_specs=[q, kp, kc, vp, vc, pl.BlockSpec(memory_space=pltpu.SMEM)],
        out_specs=[pl.BlockSpec((blk, 4 * LANES), lambda n: (n, 0)), pl.BlockSpec((blk, LANES), lambda n: (n, 0))],
        out_shape=[jax.ShapeDtypeStruct((n_tok, 4 * LANES), BF16), jax.ShapeDtypeStruct((n_tok, LANES), F32)],
        name="swa_fwd", compiler_params=_params(1))(zm, zm, zm, zm, zm, sinks)


def _swa_bwd(zm, sinks, d_out, out, lse):
    n_tok = zm.shape[0]
    blk = SWA_BLOCK

    def kern(q_ref, kp_ref, kc_ref, vp_ref, vc_ref, do_ref, o_ref, lse_ref, sink_ref,
             dq_ref, dkp_ref, dkc_ref, dvp_ref, dvc_ref, dsk_ref):
        n = pl.program_id(0)

        @pl.when(n == 0)
        def _():
            dsk_ref[...] = jnp.zeros_like(dsk_ref)

        ok, dist = _swa_band_mask(n)
        ok, dist = jnp.concatenate([ok] * 4, axis=0), jnp.concatenate([dist] * 4, axis=0)
        k2 = jnp.concatenate([kp_ref[...], kc_ref[...]], axis=0)
        v2 = jnp.concatenate([vp_ref[...], vc_ref[...]], axis=0)
        lane = lax.broadcasted_iota(jnp.int32, (blk, LANES), 1)
        lo = lane < HEAD_DIM
        lse_t = lse_ref[...]
        dqs, dkv = {}, []
        for heads, kx, vx in ((SWA_GROUPS[0], k2, v2), (SWA_GROUPS[1], _swap_halves(k2), _swap_halves(v2))):
            qa, s = _swa_scores(q_ref, kx, heads, lo, ok, dist)
            doa = _stack_heads(do_ref, heads, lo, True)
            lse_g = jnp.concatenate([lse_t[:, h:h + 1] for h in heads], axis=0)
            prob = jnp.exp(s - lse_g)
            dd = jnp.sum(doa.astype(F32) * _stack_heads(o_ref, heads, lo, False).astype(F32), axis=-1, keepdims=True)
            ds = (prob * (_dot_nt(doa, vx) - dd)).astype(BF16)
            sink_part = -jnp.exp(_per_head_column(lambda h: sink_ref[h], heads) - lse_g) * dd
            dq = _dot(ds, kx) * SCALE
            for r, h in enumerate(heads):
                dqs[h] = dq[r * blk:(r + 1) * blk]
                dsk_ref[h:h + 1, :] += jnp.broadcast_to(
                    jnp.sum(sink_part[r * blk:(r + 1) * blk], axis=0, keepdims=True), (1, LANES))
            dkv.append((_dot_tn(ds, qa), _dot_tn(prob.astype(BF16), doa)))
        for pair in range(SWA_HEADS // 2):
            dq_ref[:, pair * LANES:(pair + 1) * LANES] = jnp.where(lo, dqs[2 * pair], dqs[2 * pair + 1]).astype(BF16)
        dk = dkv[0][0] + pltpu.roll(dkv[1][0], HEAD_DIM, axis=1)
        dv = dkv[0][1] + pltpu.roll(dkv[1][1], HEAD_DIM, axis=1)
        dkp_ref[...] = dk[0:blk]
        dkc_ref[...] = dk[blk:2 * blk]
        dvp_ref[...] = dv[0:blk]
        dvc_ref[...] = dv[blk:2 * blk]

    q, kp, kc, vp, vc = _swa_specs()
    wide = pl.BlockSpec((blk, 4 * LANES), lambda n: (n, 0))
    narrow = pl.BlockSpec((blk, LANES), lambda n: (n, 0))
    part = jax.ShapeDtypeStruct((n_tok, LANES), F32)
    return pl.pallas_call(
        kern, grid=(n_tok // blk,),
        in_specs=[q, kp, kc, vp, vc, wide, wide, narrow, pl.BlockSpec(memory_space=pltpu.SMEM)],
        out_specs=[wide, narrow, narrow, narrow, narrow, pl.BlockSpec((8, LANES), lambda n: (0, 0))],
        out_shape=[jax.ShapeDtypeStruct((n_tok, 4 * LANES), BF16), part, part, part, part,
                   jax.ShapeDtypeStruct((8, LANES), F32)],
        name="swa_bwd", compiler_params=_params(1))(zm, zm, zm, zm, zm, d_out, out, lse, sinks)


def _my_pos():
    return lax.axis_index("x"), lax.axis_index("y"), lax.axis_index("c")


def _peer(k):
    x, y, c = _my_pos()
    px, py, pc = x ^ (k >> 2), y ^ ((k >> 1) & 1), c ^ (k & 1)
    return (px, py, pc), 4 * px + 2 * py + pc


def _gather_copies(x_refs, out_refs, send_sems, recv_sems, local_sems):
    x, y, c = _my_pos()
    my_id = 4 * x + 2 * y + c
    local = [pltpu.make_async_copy(x_refs[w], out_refs[w].at[my_id], local_sems.at[w]) for w in range(len(x_refs))]
    sends, arrivals = [], []
    for k in range(1, N_DEV):
        peer, peer_id = _peer(k)
        for w in range(len(x_refs)):
            sems = dict(send_sem=send_sems.at[7 * w + k - 1], recv_sem=recv_sems.at[7 * w + k - 1],
                        device_id=peer, device_id_type=MESH)
            sends.append(pltpu.make_async_remote_copy(src_ref=x_refs[w], dst_ref=out_refs[w].at[my_id], **sems))
            arrivals.append(pltpu.make_async_remote_copy(src_ref=x_refs[w], dst_ref=out_refs[w].at[peer_id], **sems))
    return local, sends, arrivals


def _scatter_copies(g_refs, part_refs, send_sems, recv_sems, local_sems):
    x, y, c = _my_pos()
    my_id = 4 * x + 2 * y + c
    local = [pltpu.make_async_copy(g_refs[w].at[my_id], part_refs[w].at[0], local_sems.at[w])
             for w in range(len(g_refs))]
    sends, arrivals = [], []
    for k in range(1, N_DEV):
        peer, peer_id = _peer(k)
        for w in range(len(g_refs)):
            sems = dict(send_sem=send_sems.at[7 * w + k - 1], recv_sem=recv_sems.at[7 * w + k - 1],
                        device_id=peer, device_id_type=MESH)
            sends.append(pltpu.make_async_remote_copy(src_ref=g_refs[w].at[peer_id], dst_ref=part_refs[w].at[k], **sems))
            arrivals.append(pltpu.make_async_remote_copy(src_ref=g_refs[w].at[my_id], dst_ref=part_refs[w].at[k], **sems))
    return local, sends, arrivals


def _start_copies(local, sends, arrivals):
    for cp in local + sends:
        cp.start()


def _finish_copies(local, sends, arrivals):
    for cp in arrivals:
        cp.wait_recv()
    for cp in sends:
        cp.wait_send()
    for cp in local:
        cp.wait()


def _exchange_scratch(n_arrays):
    return [pltpu.SemaphoreType.DMA((7 * n_arrays,)), pltpu.SemaphoreType.DMA((7 * n_arrays,)),
            pltpu.SemaphoreType.DMA((n_arrays,))]


class _Ride:
    def __init__(self, arrays, out_shape, copies):
        self.arrays, self.out_shape, self.copies = list(arrays), list(out_shape), copies
        any_spec = pl.BlockSpec(memory_space=pl.ANY)
        self.in_specs = [any_spec] * len(self.arrays)
        self.out_specs = [any_spec] * len(self.arrays)
        self.scratch = _exchange_scratch(len(self.arrays)) if self.arrays else []

    def specs(self):
        return self

    @staticmethod
    def _at(grid, last):
        hit = [pl.program_id(d) == (n - 1 if last else 0) for d, n in enumerate(grid)]
        return hit[0] if len(hit) == 1 else jnp.logical_and(*hit)

    def at_first_step(self, grid, in_refs, out_refs, sems):
        @pl.when(self._at(grid, False))
        def _():
            _start_copies(*self.copies(in_refs, out_refs, *sems))

    def at_last_step(self, grid, in_refs, out_refs, sems):
        @pl.when(self._at(grid, True))
        def _():
            _finish_copies(*self.copies(in_refs, out_refs, *sems))


_NO_RIDE = _Ride([], [], None)


def _gather_ride(shards):
    return _Ride(shards, [jax.ShapeDtypeStruct((N_DEV,) + s.shape, s.dtype) for s in shards], _gather_copies)


def _scatter_ride(grads):
    return _Ride(grads, [jax.ShapeDtypeStruct(g.shape, g.dtype) for g in grads], _scatter_copies)


Q_COL, K_COL, V_COL = 6, 10, 14


def _causal(t, tq, tk):
    row = lax.broadcasted_iota(jnp.int32, (tq, tk), 0)
    col = lax.broadcasted_iota(jnp.int32, (tq, tk), 1)
    return jnp.where(col <= row, t, NEG)


def _lane_tile(stat, width):
    return jnp.tile(stat, (1, width // LANES))


def _fox_steps(nq):
    steps = [(i2, j, 0 if j < 2 * i2 else 1 + j - 2 * i2) for i2 in range(nq // 2) for j in range(2 * i2 + 2)]
    return [np.asarray(col, np.int32) for col in zip(*steps)]


_SWEEPS = {0: [(0, False), (1, False)], 1: [(0, True), (1, False)], 2: [(1, True)]}


def _fox_fwd(zm, c_pairs, tq, ride=None):
    n_tok = zm.shape[0]
    ii, jj, kk = _fox_steps(n_tok // tq)
    n_steps = len(ii)
    n_ride = len(ride.arrays) if ride else 0

    def kern(ii_ref, jj_ref, kk_ref, q_ref, k_ref, v_ref, ck_ref, *more):
        ride_in, (o_ref, ln_ref), ride_out = more[:n_ride], more[n_ride:n_ride + 2], more[n_ride + 2:2 * n_ride + 2]
        qs_ref, m_ref, l_ref, acc_ref = more[2 * n_ride + 2:2 * n_ride + 6]
        step = pl.program_id(1)
        j, kind = jj_ref[step], kk_ref[step]
        lo = lax.broadcasted_iota(jnp.int32, (2 * tq, LANES), 1) < HEAD_DIM
        if ride:
            ride.at_first_step((FOX_HEADS // 2, n_steps), ride_in, ride_out, more[2 * n_ride + 6:])

        @pl.when(j == 0)
        def _():
            q2 = q_ref[...]
            zq = jnp.zeros_like(q2)
            qs_ref[0] = jnp.where(lo, q2, zq) * SCALE
            qs_ref[1] = jnp.where(lo, zq, q2) * SCALE
            m_ref[...] = jnp.full(m_ref.shape, NEG, F32)
            l_ref[...] = jnp.zeros(l_ref.shape, F32)
            acc_ref[...] = jnp.zeros(acc_ref.shape, F32)

        def sweep(subs):
            kv = k_ref[...]
            v_ones = jnp.concatenate([v_ref[...], jnp.ones((tq, LANES), BF16)], axis=1)
            for sub, diag in subs:
                rows = slice(sub * tq, (sub + 1) * tq)
                for a in range(2):
                    t = _dot_nt(qs_ref[a, rows], kv) - ck_ref[a:a + 1, :]
                    if diag:
                        t = _causal(t, tq, tq)
                    m_old = m_ref[a, rows]
                    m_new = jnp.maximum(m_old, jnp.max(t, axis=-1, keepdims=True))
                    alpha = jnp.exp(m_old - m_new)
                    e = jnp.exp(t - _lane_tile(m_new, tq)).astype(BF16)
                    pv = _dot(e, v_ones)
                    acc_ref[a, rows] = alpha * acc_ref[a, rows] + pv[:, :LANES]
                    l_ref[a, rows] = alpha * l_ref[a, rows] + pv[:, LANES:]
                    m_ref[a, rows] = m_new

        for kind_id, subs in _SWEEPS.items():
            pl.when(kind == kind_id)(lambda subs=subs: sweep(subs))

        @pl.when(kind == 2)
        def _():
            o_ref[...] = jnp.where(lo, acc_ref[0] / l_ref[0], acc_ref[1] / l_ref[1]).astype(BF16)
            ln_ref[:, :LANES] = m_ref[0] + jnp.log(l_ref[0])
            ln_ref[:, LANES:] = m_ref[1] + jnp.log(l_ref[1])

        if ride:
            ride.at_last_step((FOX_HEADS // 2, n_steps), ride_in, ride_out, more[2 * n_ride + 6:])

    blk = (tq, LANES)
    by_i = lambda col: (lambda hp, s, ii, jj, kk: (ii[s], col + hp))
    by_j = lambda col: (lambda hp, s, ii, jj, kk: (jj[s], col + hp))
    extra = ride.specs() if ride else _NO_RIDE
    grid_spec = pltpu.PrefetchScalarGridSpec(
        num_scalar_prefetch=3, grid=(FOX_HEADS // 2, n_steps),
        in_specs=[pl.BlockSpec((2 * tq, LANES), by_i(Q_COL)), pl.BlockSpec(blk, by_j(K_COL)),
                  pl.BlockSpec(blk, by_j(V_COL)),
                  pl.BlockSpec((None, 2, tq), lambda hp, s, ii, jj, kk: (hp, 0, jj[s]))] + extra.in_specs,
        out_specs=[pl.BlockSpec((2 * tq, LANES), by_i(0)), pl.BlockSpec((2 * tq, 2 * LANES), by_i(0))] + extra.out_specs,
        scratch_shapes=[pltpu.VMEM((2, 2 * tq, LANES), BF16), pltpu.VMEM((2, 2 * tq, LANES), F32),
                        pltpu.VMEM((2, 2 * tq, LANES), F32), pltpu.VMEM((2, 2 * tq, LANES), F32)] + extra.scratch)
    return pl.pallas_call(
        kern, grid_spec=grid_spec,
        out_shape=[jax.ShapeDtypeStruct((n_tok, 4 * LANES), BF16),
                   jax.ShapeDtypeStruct((n_tok, FOX_HEADS * LANES), F32)] + extra.out_shape,
        name="fox_fwd", compiler_params=_params(2))(ii, jj, kk, zm, zm, zm, c_pairs, *extra.arrays)


def _fox_delta(d_out, out, tm):
    def body(i, ins, consts, outs, accs):
        do_ref, o_ref = ins
        dl_ref, = outs
        lane = lax.broadcasted_iota(jnp.int32, (tm, LANES), 1)
        lo = lane < HEAD_DIM
        for pair in range(FOX_HEADS // 2):
            cols = slice(pair * LANES, (pair + 1) * LANES)
            prod = do_ref[:, cols].astype(F32) * o_ref[:, cols].astype(F32)
            for a in range(2):
                dd = jnp.sum(jnp.where(lo if a == 0 else ~lo, prod, 0.0), axis=-1, keepdims=True)
                h = 2 * pair + a
                dl_ref[:, h * LANES:(h + 1) * LANES] = jnp.broadcast_to(dd, (tm, LANES))

    return _row_call(body, "fox_delta", d_out.shape[0], tm, [d_out, out], [], [(FOX_HEADS * LANES, F32)], [])[0]


def _fox_bwd(zm, c_pairs, d_out, lnorm, delta, tq, ride=None):
    n_tok = zm.shape[0]
    ii, jj, kk = _fox_steps(n_tok // tq)
    n_steps = len(ii)
    n_ride = len(ride.arrays) if ride else 0

    def kern(ii_ref, jj_ref, kk_ref, q_ref, k_ref, v_ref, ck_ref, do_ref, ln_ref, dl_ref, *more):
        ride_in, ride_out = more[:n_ride], more[n_ride + 5:2 * n_ride + 5]
        dq_ref, dk_ref, dv_ref, cs_ref, rs_ref = more[n_ride:n_ride + 5]
        qs_ref, qo_ref, dos_ref, dq_acc = more[2 * n_ride + 5:2 * n_ride + 9]
        step = pl.program_id(1)
        j, kind = jj_ref[step], kk_ref[step]
        lo = lax.broadcasted_iota(jnp.int32, (2 * tq, LANES), 1) < HEAD_DIM
        if ride:
            ride.at_first_step((FOX_HEADS // 2, n_steps), ride_in, ride_out, more[2 * n_ride + 9:])

        @pl.when(step == 0)
        def _():
            dk_ref[...] = jnp.zeros_like(dk_ref)
            dv_ref[...] = jnp.zeros_like(dv_ref)
            cs_ref[...] = jnp.zeros_like(cs_ref)

        @pl.when(j == 0)
        def _():
            q2, do2 = q_ref[...], do_ref[...]
            zq = jnp.zeros_like(q2)
            ones = jnp.ones((2 * tq, LANES), BF16)
            for a in range(2):
                half = lo if a == 0 else ~lo
                qa = jnp.where(half, q2, zq) * SCALE
                qs_ref[a] = qa
                qo_ref[a] = jnp.concatenate([qa, ones], axis=1)
                dos_ref[a] = jnp.where(half, do2, zq)
            dq_acc[...] = jnp.zeros(dq_acc.shape, F32)

        def sweep(subs):
            kv, vv = k_ref[...], v_ref[...]
            k_ones = jnp.concatenate([kv, jnp.ones((tq, LANES), BF16)], axis=1)
            dk, dv, sums = None, None, [None, None]
            for sub, diag in subs:
                rows = slice(sub * tq, (sub + 1) * tq)
                for a in range(2):
                    t = _dot_nt(qs_ref[a, rows], kv) - ck_ref[a:a + 1, :]
                    if diag:
                        t = _causal(t, tq, tq)
                    prob = jnp.exp(t - _lane_tile(ln_ref[rows, a * LANES:(a + 1) * LANES], tq))
                    dp = _dot_nt(dos_ref[a, rows], vv)
                    ds = (prob * (dp - _lane_tile(dl_ref[rows, a * LANES:(a + 1) * LANES], tq))).astype(BF16)
                    dq_acc[a, rows] += _dot(ds, k_ones)
                    dk_cs = _dot_tn(ds, qo_ref[a, rows])
                    dv_a = _dot_tn(prob.astype(BF16), dos_ref[a, rows])
                    dk = dk_cs[:, :LANES] if dk is None else dk + dk_cs[:, :LANES]
                    dv = dv_a if dv is None else dv + dv_a
                    sums[a] = dk_cs[:, LANES:] if sums[a] is None else sums[a] + dk_cs[:, LANES:]
            keys = pl.ds(pl.multiple_of(j * tq, tq), tq)
            dk_ref[keys, :] += dk
            cs_ref[keys, :] += jnp.where(lo[:tq], sums[0], sums[1])
            dv_ref[keys, :] += dv

        for kind_id, subs in _SWEEPS.items():
            pl.when(kind == kind_id)(lambda subs=subs: sweep(subs))

        @pl.when(kind == 2)
        def _():
            dq_ref[...] = jnp.where(lo, dq_acc[0, :, :LANES], dq_acc[1, :, :LANES]) * SCALE
            rs_ref[...] = jnp.where(lo, dq_acc[0, :, LANES:], dq_acc[1, :, LANES:])

        if ride:
            ride.at_last_step((FOX_HEADS // 2, n_steps), ride_in, ride_out, more[2 * n_ride + 9:])

    blk = (tq, LANES)
    by_i = lambda col: (lambda hp, s, ii, jj, kk: (ii[s], col + hp))
    by_j = lambda col: (lambda hp, s, ii, jj, kk: (jj[s], col + hp))
    resident = pl.BlockSpec((2 * tq, LANES), by_i(0))
    stat = pl.BlockSpec((2 * tq, 2 * LANES), by_i(0))
    whole = pl.BlockSpec((n_tok, LANES), lambda hp, s, ii, jj, kk: (0, hp))
    extra = ride.specs() if ride else _NO_RIDE
    grid_spec = pltpu.PrefetchScalarGridSpec(
        num_scalar_prefetch=3, grid=(FOX_HEADS // 2, n_steps),
        in_specs=[pl.BlockSpec((2 * tq, LANES), by_i(Q_COL)), pl.BlockSpec(blk, by_j(K_COL)),
                  pl.BlockSpec(blk, by_j(V_COL)),
                  pl.BlockSpec((None, 2, tq), lambda hp, s, ii, jj, kk: (hp, 0, jj[s])),
                  resident, stat, stat] + extra.in_specs,
        out_specs=[resident, whole, whole, whole, resident] + extra.out_specs,
        scratch_shapes=[pltpu.VMEM((2, 2 * tq, LANES), BF16), pltpu.VMEM((2, 2 * tq, 2 * LANES), BF16),
                        pltpu.VMEM((2, 2 * tq, LANES), BF16), pltpu.VMEM((2, 2 * tq, 2 * LANES), F32)] + extra.scratch)
    wide = jax.ShapeDtypeStruct((n_tok, 4 * LANES), F32)
    return pl.pallas_call(
        kern, grid_spec=grid_spec, out_shape=[wide] * 5 + extra.out_shape, name="fox_bwd",
        compiler_params=_params(2, FOX_BWD_VMEM))(ii, jj, kk, zm, zm, zm, c_pairs, d_out, lnorm, delta, *extra.arrays)


def _all_gather(shards):
    n_w = len(shards)

    def kern(*refs):
        x_refs, out_refs = refs[:n_w], refs[n_w:2 * n_w]
        send_sems, recv_sems, local_sems = refs[2 * n_w:]
        x, y, c = _my_pos()
        me, sibling = (x, y, c), (x, y, 1 - c)
        chips = [(1 - x, y), (x, 1 - y), (1 - x, 1 - y)]

        def slot(w, px, py, pc):
            return out_refs[w].at[4 * px + 2 * py + pc]

        def copy(w, k, block, to, src=None):
            return pltpu.make_async_remote_copy(
                src_ref=slot(w, *block) if src is None else src, dst_ref=slot(w, *block),
                send_sem=send_sems.at[7 * w + k], recv_sem=recv_sems.at[7 * w + k], device_id=to, device_id_type=MESH)

        local, started = [], []
        for w in range(n_w):
            mine = pltpu.make_async_copy(x_refs[w], slot(w, *me), local_sems.at[w])
            mine.start()
            local.append(mine)
            first = [copy(w, 0, me, sibling, src=x_refs[w])]
            first += [copy(w, 1 + k, me, (*chip, c), src=x_refs[w]) for k, chip in enumerate(chips)]
            for cp in first:
                cp.start()
            started += first
        for k, chip in enumerate(chips):
            for w in range(n_w):
                copy(w, 1 + k, (*chip, c), me).wait_recv()
                passed = copy(w, 4 + k, (*chip, c), sibling)
                passed.start()
                started.append(passed)
        for w in range(n_w):
            copy(w, 0, sibling, me).wait_recv()
            for k, chip in enumerate(chips):
                copy(w, 4 + k, (*chip, 1 - c), me).wait_recv()
        for cp in started:
            cp.wait_send()
        for cp in local:
            cp.wait()

    any_spec = pl.BlockSpec(memory_space=pl.ANY)
    return pl.pallas_call(
        kern, out_shape=[jax.ShapeDtypeStruct((N_DEV,) + s.shape, s.dtype) for s in shards],
        in_specs=[any_spec] * n_w, out_specs=[any_spec] * n_w,
        scratch_shapes=[pltpu.SemaphoreType.DMA((7 * n_w,)), pltpu.SemaphoreType.DMA((7 * n_w,)),
                        pltpu.SemaphoreType.DMA((n_w,))],
        name="weight_all_gather")(*shards)


def _small_exchange(small):
    def kern(s_ref, sall_ref, *sems):
        copies = _gather_copies([s_ref], [sall_ref], *sems)
        _start_copies(*copies)
        _finish_copies(*copies)

    any_spec = pl.BlockSpec(memory_space=pl.ANY)
    return pl.pallas_call(
        kern, out_shape=jax.ShapeDtypeStruct((N_DEV,) + small.shape, small.dtype), in_specs=[any_spec],
        out_specs=any_spec, scratch_shapes=_exchange_scratch(1), name="small_grad_exchange")(small)


ADAMW_BLOCK_BYTES = 2 * 1024 * 1024


def _adamw(parts, w, m, v, name):
    n_parts, n_rows, n_cols = parts.shape
    limit = max(8, ADAMW_BLOCK_BYTES // (n_parts * n_cols * parts.dtype.itemsize))
    tr = max(t for t in range(8, n_rows + 1, 8) if n_rows % t == 0 and t <= limit)

    def kern(p_ref, w_ref, m_ref, v_ref, g_out, d_out, m_out, v_out):
        g = p_ref[0].astype(F32)
        for k in range(1, n_parts):
            g = g + p_ref[k].astype(F32)
        m_new = ADAM_B1 * m_ref[...] + (1.0 - ADAM_B1) * g
        v_new = ADAM_B2 * v_ref[...] + (1.0 - ADAM_B2) * jnp.square(g)
        m_hat = m_new / (1.0 - ADAM_B1 ** ADAM_STEP)
        v_hat = v_new / (1.0 - ADAM_B2 ** ADAM_STEP)
        g_out[...] = g
        d_out[...] = -ADAM_LR * (m_hat / (jnp.sqrt(v_hat) + ADAM_EPS) + ADAM_WD * w_ref[...])
        m_out[...] = m_new
        v_out[...] = v_new

    row = pl.BlockSpec((tr, n_cols), lambda i: (i, 0))
    out = jax.ShapeDtypeStruct((n_rows, n_cols), F32)
    return pl.pallas_call(
        kern, grid=(n_rows // tr,),
        in_specs=[pl.BlockSpec((n_parts, tr, n_cols), lambda i: (0, i, 0)), row, row, row],
        out_specs=[row, row, row, row], out_shape=[out, out, out, out], name=name,
        compiler_params=_params(1))(parts, w, m, v)


SHARDED = {
    "w_in": ((D_MODEL, D_IN), 1), "w_br_swa": ((512, D_MODEL), 1), "w_br_fox": ((512, D_MODEL), 1),
    "w_mix_out": ((D_MODEL, D_MODEL), 0), "w_ff1": ((D_MODEL, D_FF), 1), "w_ff2": ((D_FF, D_MODEL), 0),
    "w_ple_gate": ((D_MODEL, D_MODEL), 0), "w_ple_proj": ((PLE_DIM, D_MODEL), 1),
}
W_IN_SHARD = D_IN // N_DEV
W_IN_PAD = 640
SMALL = ("g_mix", "g_mlp", "g_ple", "g_final", "b_forget", "swa_sinks")
SMALL_COLS = 1024


def _wire_shard(name, a):
    a = a.reshape(a.shape[-2:])
    return jnp.pad(a, ((0, 0), (0, W_IN_PAD - W_IN_SHARD))) if name == "w_in" else a


def _from_wire(name, a):
    return (a[:, :W_IN_SHARD] if name == "w_in" else a)[None]


def _w_all_from_wire(stacked):
    w_in = jnp.concatenate([stacked[d][:, :W_IN_SHARD] for d in range(N_DEV)], axis=1)
    fpad = jnp.zeros((D_MODEL, N_FPAD - FOX_HEADS), stacked.dtype)
    return jnp.concatenate([w_in[:, :N_MAIN + FOX_HEADS], fpad, w_in[:, N_MAIN + FOX_HEADS:]], axis=1)


def _dw_in_to_wire(dw_all):
    dw_in = jnp.concatenate([dw_all[:, :N_MAIN + FOX_HEADS], dw_all[:, N_MAIN + N_FPAD:]], axis=1)
    pad = jnp.zeros((D_MODEL, W_IN_PAD - W_IN_SHARD), dw_all.dtype)
    return jnp.stack([jnp.concatenate([dw_in[:, d * W_IN_SHARD:(d + 1) * W_IN_SHARD], pad], axis=1)
                      for d in range(N_DEV)])


def _pack_small(vals):
    rows = [jnp.pad(vals[n].reshape(-1), (0, SMALL_COLS - vals[n].size)) for n in SMALL]
    rows += [jnp.zeros((SMALL_COLS,), F32)] * (8 - len(SMALL))
    return jnp.stack(rows)


def _unpack_small(slab, like):
    return {n: slab[r, :like[n].size].reshape(like[n].shape) for r, n in enumerate(SMALL)}


def _local_step(x, p, tgt, w, small, tm, tq, ts, late_shards=None):
    n_tok = x.shape[0]
    row = lambda v: v.reshape(1, -1)
    g_mix, g_mlp, g_ple, g_fin = row(small["g_mix"]), row(small["g_mlp"]), row(small["g_ple"]), row(small["g_final"])
    sinks = small["swa_sinks"].reshape(-1)
    b_col = small["b_forget"].reshape(FOX_HEADS, 1)

    u1, zm, zfg, zf = _in_proj(x, g_mix, w["w_all"], tm)
    f_t = zf[:, :FOX_HEADS].T
    c_pairs = _decay_cumsum(f_t, b_col).reshape(FOX_HEADS // 2, 2, n_tok)
    attn_a, lse_a = _swa_fwd(zm, sinks)
    if late_shards is None:
        attn_b, ln_b = _fox_fwd(zm, c_pairs, tq)
    else:
        attn_b, ln_b, *late = _fox_fwd(zm, c_pairs, tq, _gather_ride(list(late_shards.values())))
        w = {**w, **_gathered_to_local(dict(zip(late_shards, late)))}
    ya, yb, mixed, h1, u2 = _mix_fwd(attn_a, attn_b, zfg, x, w["w_br_swa"], w["w_br_fox"], w["w_mix_out"], g_mlp, tm)
    a, r, h2 = _ffn_fwd(u2, h1, w["w_ff1"], w["w_ff2"], tm // 2)
    dh3, dlg, dpp, u3, loss_acc, dgf = _head_fwd_bwd(h2, p, tgt, g_ple, w["w_ple_gate"], w["w_ple_proj"], g_fin, tm)

    dh2, dh2b, da, dgp = _ffn_bwd_a(dlg, dh3, h2, a, w["w_ple_gate"], g_ple, w["w_ff2"], tm // 2)
    dh1, dh1b, dgl, dya, dyb, daa, dab, dgm = _ffn_bwd_b(
        da, dh2, h1, ya, yb, zfg, w["w_ff1"], g_mlp, w["w_mix_out"], w["w_br_swa"], w["w_br_fox"], tm // 2)
    dq_a, dkp, dkc, dvp, dvc, dsk = _swa_bwd(zm, sinks, daa, attn_a, lse_a)
    delta_b = _fox_delta(dab, attn_b, tm)
    dw = {
        "w_br_swa": _matmul_tn(attn_a, dya, "dw_br_swa", ts, stack_cols=D_MODEL // N_DEV),
        "w_br_fox": _matmul_tn(attn_b, dyb, "dw_br_fox", ts, stack_cols=D_MODEL // N_DEV),
        "w_mix_out": _matmul_tn(mixed, dh1b, "dw_mix_out", ts),
        "w_ff1": _matmul_tn(u2, da, "dw_ff1", ts, stack_cols=D_FF // N_DEV),
        "w_ff2": _matmul_tn(r, dh2b, "dw_ff2", ts),
        "w_ple_gate": _matmul_tn(u3, dlg, "dw_ple_gate", ts),
        "w_ple_proj": _matmul_tn(p, dpp, "dw_ple_proj", ts, stack_cols=D_MODEL // N_DEV),
    }
    if late_shards is None:
        dq_b, dk_b, dv_b, cs, rs = _fox_bwd(zm, c_pairs, dab, ln_b, delta_b, tq)
        late_parts = None
    else:
        wire = _local_to_wire(dw)
        dq_b, dk_b, dv_b, cs, rs, *parts = _fox_bwd(zm, c_pairs, dab, ln_b, delta_b, tq,
                                                    _scatter_ride([wire[n] for n in late_shards]))
        late_parts = dict(zip(late_shards, parts))

    up = lambda t: jnp.concatenate([t[SWA_BLOCK:], jnp.zeros((SWA_BLOCK, LANES), F32)], axis=0)
    dk_a, dv_a = dkc + up(dkp), dvc + up(dvp)
    df_t, db = _decay_bwd(cs, rs, f_t, b_col)
    df = jnp.pad(df_t.T, ((0, 0), (0, N_FPAD - FOX_HEADS)))
    dz = jnp.concatenate([dq_a, dk_a.astype(BF16), dv_a.astype(BF16), dq_b.astype(BF16), dk_b.astype(BF16), dv_b.astype(BF16),
                          df.astype(BF16), dgl], axis=1)
    dw["w_all"] = _matmul_tn(u1, dz, "dw_in", ts)
    if late_shards is None:
        dx, dgx = _in_proj_bwd(dz, dh1, x, w["w_all"], g_mix, tm)
    else:
        dx, dgx, late_parts["w_in"] = _in_proj_bwd(dz, dh1, x, w["w_all"], g_mix, tm,
                                                   _scatter_ride([_dw_in_to_wire(dw["w_all"])]))
    dsmall = {"g_mix": dgx[0], "g_mlp": dgm[0], "g_ple": dgp[0], "g_final": dgf[0],
              "b_forget": db[:, 0], "swa_sinks": dsk[:, 0]}
    return loss_acc[0, 0], dx, dw, dsmall, late_parts


_ROWS = lambda t: t.reshape(-1, t.shape[-1])
_BY_ROWS = lambda t: t.reshape(N_DEV, t.shape[0] // N_DEV, t.shape[1])
_SAME = lambda t: t
LOCAL_LAYOUT = {
    "w_in": ("w_all", _w_all_from_wire, _dw_in_to_wire), "w_br_swa": ("w_br_swa", _SAME, _SAME),
    "w_br_fox": ("w_br_fox", _SAME, _SAME), "w_mix_out": ("w_mix_out", _ROWS, _BY_ROWS),
    "w_ff1": ("w_ff1", _SAME, _SAME), "w_ff2": ("w_ff2", _SAME, _BY_ROWS),
    "w_ple_gate": ("w_ple_gate", _ROWS, _BY_ROWS), "w_ple_proj": ("w_ple_proj", _SAME, _SAME),
}


def _gathered_to_local(g):
    return {LOCAL_LAYOUT[n][0]: LOCAL_LAYOUT[n][1](t) for n, t in g.items()}


def _local_to_wire(dw):
    names = {local: n for n, (local, _, _) in LOCAL_LAYOUT.items()}
    return {names[local]: LOCAL_LAYOUT[names[local]][2](t) for local, t in dw.items()}


def kernel(x, p, g_mix, w_in, b_forget, swa_sinks, w_br_swa, w_br_fox, w_mix_out, g_mlp, w_ff1, w_ff2, g_ple, w_ple_gate, w_ple_proj, g_final, loss_target, m_g_mix, m_w_in, m_b_forget, m_swa_sinks, m_w_br_swa, m_w_br_fox, m_w_mix_out, m_g_mlp, m_w_ff1, m_w_ff2, m_g_ple, m_w_ple_gate, m_w_ple_proj, m_g_final, v_g_mix, v_w_in, v_b_forget, v_swa_sinks, v_w_br_swa, v_w_br_fox, v_w_mix_out, v_g_mlp, v_w_ff1, v_w_ff2, v_g_ple, v_w_ple_gate, v_w_ple_proj, v_g_final):
    given = dict(g_mix=g_mix, w_in=w_in, b_forget=b_forget, swa_sinks=swa_sinks, w_br_swa=w_br_swa, w_br_fox=w_br_fox,
                 w_mix_out=w_mix_out, g_mlp=g_mlp, w_ff1=w_ff1, w_ff2=w_ff2, g_ple=g_ple, w_ple_gate=w_ple_gate,
                 w_ple_proj=w_ple_proj, g_final=g_final)
    mom = dict(g_mix=m_g_mix, w_in=m_w_in, b_forget=m_b_forget, swa_sinks=m_swa_sinks, w_br_swa=m_w_br_swa,
               w_br_fox=m_w_br_fox, w_mix_out=m_w_mix_out, g_mlp=m_g_mlp, w_ff1=m_w_ff1, w_ff2=m_w_ff2, g_ple=m_g_ple,
               w_ple_gate=m_w_ple_gate, w_ple_proj=m_w_ple_proj, g_final=m_g_final)
    vel = dict(g_mix=v_g_mix, w_in=v_w_in, b_forget=v_b_forget, swa_sinks=v_swa_sinks, w_br_swa=v_w_br_swa,
               w_br_fox=v_w_br_fox, w_mix_out=v_w_mix_out, g_mlp=v_g_mlp, w_ff1=v_w_ff1, w_ff2=v_w_ff2, g_ple=v_g_ple,
               w_ple_gate=v_w_ple_gate, w_ple_proj=v_w_ple_proj, g_final=v_g_final)
    names = list(given)
    sharded = list(SHARDED)

    w_wire = {n: _wire_shard(n, given[n]) for n in sharded}
    late = [n for n in sharded if n != "w_in"]
    gathered = _all_gather([w_wire["w_in"].astype(BF16)])
    local_w = _gathered_to_local({"w_in": gathered[0]})
    small = {n: given[n].reshape(-1) for n in SMALL}

    n_tok = x.shape[1]
    tile = min(512, n_tok // 4)
    loss_part, dx, dw, dsmall, parts = _local_step(
        x[0], p[0, 0], loss_target[0], local_w, small, tm=tile, tq=tile, ts=min(2048, n_tok // 4),
        late_shards={n: w_wire[n].astype(BF16) for n in late})
    loss = lax.psum(loss_part, AXES)

    small_all = _small_exchange(_pack_small(dsmall))

    res = {}
    for n in sharded:
        part = parts[n]
        flat = part.reshape(N_DEV, -1, part.shape[-1])
        outs = _adamw(flat, w_wire[n], _wire_shard(n, mom[n]), _wire_shard(n, vel[n]), "adamw_" + n)
        res[n] = [_from_wire(n, o) for o in outs]
    outs_s = _adamw(small_all, _pack_small(small), _pack_small({n: mom[n] for n in SMALL}),
                    _pack_small({n: vel[n] for n in SMALL}), "adamw_small")
    small_res = [_unpack_small(o, given) for o in outs_s]

    groups = [[res[n][k] if n in res else small_res[k][n] for n in names] for k in range(4)]
    return (loss, dx[None], *groups[0], *groups[1], *groups[2], *groups[3])
```

```python
import numpy as np
import jax
import jax.numpy as jnp
from jax import lax
from jax.experimental import pallas as pl
from jax.experimental.pallas import tpu as pltpu

F32 = jnp.float32
BF16 = jnp.bfloat16

D_MODEL = 1024
HEAD_DIM = 64
SWA_HEADS = 8
FOX_HEADS = 8
CHUNK_SHIFT = 6
SWA_BLOCK = 128
WINDOW_CHUNKS = 2
D_FF = 4096
PLE_DIM = 256
RMS_EPS = 1e-6
N_MAIN = 2304
N_FPAD = 128
N_GATE = 2048
N_ALL = N_MAIN + N_FPAD + N_GATE
D_IN = N_MAIN + FOX_HEADS + N_GATE
SCALE = HEAD_DIM ** -0.5
NEG = -1e30

ADAM_LR = 0.001
ADAM_B1 = 0.9
ADAM_B2 = 0.999
ADAM_EPS = 1e-08
ADAM_WD = 0.01
ADAM_STEP = 10

N_DEV = 8
LANES = 128
V7X_VMEM_BYTES = 64 * 1024 * 1024
VMEM_LIMIT = V7X_VMEM_BYTES * 3 // 4
FOX_BWD_VMEM = V7X_VMEM_BYTES * 7 // 8
MESH = pl.DeviceIdType.MESH
AXES = ("x", "y", "c")

_NT = (((1,), (1,)), ((), ()))
_TN = (((0,), (0,)), ((), ()))


def _params(n_grid, vmem_limit=VMEM_LIMIT):
    return pltpu.CompilerParams(dimension_semantics=("arbitrary",) * n_grid, vmem_limit_bytes=vmem_limit)


def _chunks(n, step):
    return [(s, min(step, n - s)) for s in range(0, n, step)]


def _sigmoid(x):
    return 1.0 / (1.0 + jnp.exp(-x))


def _dot(a, b):
    return jnp.dot(a, b, preferred_element_type=F32)


def _dot_nt(a, b):
    return lax.dot_general(a, b, _NT, preferred_element_type=F32)


def _dot_tn(a, b):
    return lax.dot_general(a, b, _TN, preferred_element_type=F32)


def _lane_concat(stacked_ref):
    return jnp.concatenate([stacked_ref[d] for d in range(N_DEV)], axis=1)


def _rms(h):
    return lax.rsqrt(jnp.mean(h * h, axis=-1, keepdims=True) + RMS_EPS)


def _rms_bwd(h, g, du):
    rs = _rms(h)
    n = h * rs
    dn = du * g
    dh = rs * (dn - n * jnp.mean(dn * n, axis=-1, keepdims=True))
    return dh, jnp.sum(du * n, axis=0, keepdims=True)


def _acc_rows(ref, i, row):
    @pl.when(i == 0)
    def _():
        ref[...] = jnp.zeros_like(ref)
    ref[...] += jnp.broadcast_to(row, ref.shape)


def _row_call(body, name, n_rows, tm, row_ins, const_ins, row_outs, acc_outs, ride=None):
    n_ri, n_ci, n_ro, n_ao = len(row_ins), len(const_ins), len(row_outs), len(acc_outs)
    extra = ride if ride else _NO_RIDE
    n_ride = len(extra.arrays)
    grid = (n_rows // tm,)

    def kern(*refs):
        i = pl.program_id(0)
        ins, refs = refs[:n_ri + n_ci], refs[n_ri + n_ci:]
        ride_in, refs = refs[:n_ride], refs[n_ride:]
        outs, refs = refs[:n_ro + n_ao], refs[n_ro + n_ao:]
        ride_out, sems = refs[:n_ride], refs[n_ride:]
        if ride:
            ride.at_first_step(grid, ride_in, ride_out, sems)
        body(i, ins[:n_ri], ins[n_ri:], outs[:n_ro], outs[n_ro:])
        if ride:
            ride.at_last_step(grid, ride_in, ride_out, sems)

    def whole(a):
        zeros = (0,) * a.ndim
        return pl.BlockSpec(a.shape, lambda i: zeros, pipeline_mode=pl.Buffered(1))

    in_specs = [pl.BlockSpec((tm, a.shape[1]), lambda i: (i, 0)) for a in row_ins]
    in_specs += [whole(a) for a in const_ins] + extra.in_specs
    out_specs = [pl.BlockSpec((tm, c), lambda i: (i, 0)) for c, _ in row_outs]
    out_specs += [pl.BlockSpec((8, c), lambda i: (0, 0)) for c in acc_outs] + extra.out_specs
    out_shape = [jax.ShapeDtypeStruct((n_rows, c), dt) for c, dt in row_outs]
    out_shape += [jax.ShapeDtypeStruct((8, c), F32) for c in acc_outs] + extra.out_shape
    return pl.pallas_call(kern, grid=grid, in_specs=in_specs, out_specs=out_specs, out_shape=out_shape,
                          scratch_shapes=extra.scratch, name=name,
                          compiler_params=_params(1))(*row_ins, *const_ins, *extra.arrays)


def _in_proj(x, g_mix, w_all, tm):
    def body(i, ins, consts, outs, accs):
        x_ref, = ins
        g_ref, w_ref = consts
        u_ref, zm_ref, zfg_ref, zf_ref = outs
        xv = x_ref[...]
        u = ((xv * _rms(xv)) * g_ref[...]).astype(BF16)
        u_ref[...] = u
        for s, n in _chunks(N_MAIN, 768):
            zm_ref[:, s:s + n] = _dot(u, w_ref[:, s:s + n]).astype(BF16)
        for s, n in _chunks(N_FPAD + N_GATE, 512):
            zfg_ref[:, s:s + n] = _dot(u, w_ref[:, N_MAIN + s:N_MAIN + s + n])
        zf_ref[...] = zfg_ref[:, :N_FPAD]

    return _row_call(body, "in_proj", x.shape[0], tm, [x], [g_mix, w_all],
                     [(D_MODEL, BF16), (N_MAIN, BF16), (N_FPAD + N_GATE, F32), (N_FPAD, F32)], [])


def _mix_fwd(attn_a, attn_b, zfg, x, w_sa, w_fo, w_mo, g_mlp, tm):
    def body(i, ins, consts, outs, accs):
        aa_ref, ab_ref, zfg_ref, x_ref = ins
        wsa_ref, wfo_ref, wmo_ref, g_ref = consts
        ya_ref, yb_ref, mx_ref, h1_ref, u2_ref = outs
        ya = _dot(aa_ref[...], _lane_concat(wsa_ref))
        yb = _dot(ab_ref[...], _lane_concat(wfo_ref))
        g0 = _sigmoid(zfg_ref[:, N_FPAD:N_FPAD + D_MODEL])
        g1 = _sigmoid(zfg_ref[:, N_FPAD + D_MODEL:N_FPAD + 2 * D_MODEL])
        mixed = (g0 * ya + g1 * yb).astype(BF16)
        ya_ref[...] = ya.astype(BF16)
        yb_ref[...] = yb.astype(BF16)
        mx_ref[...] = mixed
        h1 = x_ref[...] + _dot(mixed, wmo_ref[...])
        h1_ref[...] = h1
        u2_ref[...] = ((h1 * _rms(h1)) * g_ref[...]).astype(BF16)

    return _row_call(body, "mix_fwd", x.shape[0], tm, [attn_a, attn_b, zfg, x], [w_sa, w_fo, w_mo, g_mlp],
                     [(D_MODEL, BF16), (D_MODEL, BF16), (D_MODEL, BF16), (D_MODEL, F32), (D_MODEL, BF16)], [])


def _ffn_fwd(u2, h1, w1s, w2s, tm):
    ch = D_FF // N_DEV

    def body(i, ins, consts, outs, accs):
        u_ref, h1_ref = ins
        w1_ref, w2_ref = consts
        a_ref, r_ref, h2_ref = outs
        u = u_ref[...]
        acc = h1_ref[...]
        for c in range(N_DEV):
            a = _dot(u, w1_ref[c])
            a_ref[:, c * ch:(c + 1) * ch] = a.astype(BF16)
            r = jnp.square(jnp.maximum(a, 0.0)).astype(BF16)
            r_ref[:, c * ch:(c + 1) * ch] = r
            acc = acc + _dot(r, w2_ref[c])
        h2_ref[...] = acc

    return _row_call(body, "ffn_fwd", u2.shape[0], tm, [u2, h1], [w1s, w2s],
                     [(D_FF, BF16), (D_FF, BF16), (D_MODEL, F32)], [])


def _head_fwd_bwd(h2, p, tgt, g_ple, w_pg, w_pp, g_fin, tm):
    def body(i, ins, consts, outs, accs):
        h2_ref, p_ref, t_ref = ins
        gp_ref, wpg_ref, wpp_ref, gf_ref = consts
        dh3_ref, dlg_ref, dpp_ref, u3_ref = outs
        loss_ref, dgf_ref = accs
        h2 = h2_ref[...]
        u3 = ((h2 * _rms(h2)) * gp_ref[...]).astype(BF16)
        u3_ref[...] = u3
        pg = _sigmoid(_dot(u3, wpg_ref[...]))
        pp = _dot(p_ref[...].astype(BF16), _lane_concat(wpp_ref))
        h3 = h2 + pg * pp
        rs3 = _rms(h3)
        n3 = h3 * rs3
        gf = gf_ref[...]
        err = n3 * gf - t_ref[...]
        row_loss = 0.5 * jnp.mean(err * err, axis=-1, keepdims=True)
        _acc_rows(loss_ref, i, jnp.broadcast_to(jnp.sum(row_loss, axis=0, keepdims=True), (1, LANES)))
        dy = err * (1.0 / D_MODEL)
        _acc_rows(dgf_ref, i, jnp.sum(dy * n3, axis=0, keepdims=True))
        dn = dy * gf
        dh3 = rs3 * (dn - n3 * jnp.mean(dn * n3, axis=-1, keepdims=True))
        dh3_ref[...] = dh3
        dpp_ref[...] = (dh3 * pg).astype(BF16)
        dlg_ref[...] = ((dh3 * pp) * pg * (1.0 - pg)).astype(BF16)

    return _row_call(body, "head_fwd_bwd", h2.shape[0], tm, [h2, p, tgt], [g_ple, w_pg, w_pp, g_fin],
                     [(D_MODEL, F32), (D_MODEL, BF16), (D_MODEL, BF16), (D_MODEL, BF16)], [LANES, D_MODEL])


def _ffn_bwd_a(dlg, dh3, h2, a, w_pg, g_ple, w2s, tm):
    ch = D_FF // N_DEV

    def body(i, ins, consts, outs, accs):
        dlg_ref, dh3_ref, h2_ref, a_ref = ins
        wpg_ref, gp_ref, w2_ref = consts
        dh2_ref, dh2b_ref, da_ref = outs
        dgp_ref, = accs
        du3 = _dot_nt(dlg_ref[...], wpg_ref[...])
        dh, dg = _rms_bwd(h2_ref[...], gp_ref[...], du3)
        _acc_rows(dgp_ref, i, dg)
        dh2 = dh3_ref[...] + dh
        dh2_ref[...] = dh2
        dh2b = dh2.astype(BF16)
        dh2b_ref[...] = dh2b
        for c in range(N_DEV):
            dr = _dot_nt(dh2b, w2_ref[c])
            av = a_ref[:, c * ch:(c + 1) * ch].astype(F32)
            da_ref[:, c * ch:(c + 1) * ch] = (dr * (2.0 * jnp.maximum(av, 0.0))).astype(BF16)

    return _row_call(body, "ffn_bwd_a", h2.shape[0], tm, [dlg, dh3, h2, a], [w_pg, g_ple, w2s],
                     [(D_MODEL, F32), (D_MODEL, BF16), (D_FF, BF16)], [D_MODEL])


def _ffn_bwd_b(da, dh2, h1, ya, yb, zfg, w1s, g_mlp, w_mo, w_sa, w_fo, tm):
    ch = D_FF // N_DEV

    def body(i, ins, consts, outs, accs):
        da_ref, dh2_ref, h1_ref, ya_ref, yb_ref, zfg_ref = ins
        w1_ref, gm_ref, wmo_ref, wsa_ref, wfo_ref = consts
        dh1_ref, dh1b_ref, dgl_ref, dya_ref, dyb_ref, daa_ref, dab_ref = outs
        dgm_ref, = accs
        du2 = _dot_nt(da_ref[:, 0:ch], w1_ref[0])
        for c in range(1, N_DEV):
            du2 = du2 + _dot_nt(da_ref[:, c * ch:(c + 1) * ch], w1_ref[c])
        dh, dg = _rms_bwd(h1_ref[...], gm_ref[...], du2)
        _acc_rows(dgm_ref, i, dg)
        dh1 = dh2_ref[...] + dh
        dh1_ref[...] = dh1
        dh1b = dh1.astype(BF16)
        dh1b_ref[...] = dh1b
        dmx = _dot_nt(dh1b, wmo_ref[...])
        g0 = _sigmoid(zfg_ref[:, N_FPAD:N_FPAD + D_MODEL])
        g1 = _sigmoid(zfg_ref[:, N_FPAD + D_MODEL:N_FPAD + 2 * D_MODEL])
        dya = (dmx * g0).astype(BF16)
        dyb = (dmx * g1).astype(BF16)
        dya_ref[...] = dya
        dyb_ref[...] = dyb
        dgl_ref[:, 0:D_MODEL] = ((dmx * ya_ref[...].astype(F32)) * g0 * (1.0 - g0)).astype(BF16)
        dgl_ref[:, D_MODEL:2 * D_MODEL] = ((dmx * yb_ref[...].astype(F32)) * g1 * (1.0 - g1)).astype(BF16)
        daa_ref[...] = _dot_nt(dya, _lane_concat(wsa_ref)).astype(BF16)
        dab_ref[...] = _dot_nt(dyb, _lane_concat(wfo_ref)).astype(BF16)

    half = D_MODEL // 2
    return _row_call(body, "ffn_bwd_b", h1.shape[0], tm, [da, dh2, h1, ya, yb, zfg],
                     [w1s, g_mlp, w_mo, w_sa, w_fo],
                     [(D_MODEL, F32), (D_MODEL, BF16), (N_GATE, BF16), (D_MODEL, BF16), (D_MODEL, BF16),
                      (half, BF16), (half, BF16)], [D_MODEL])


def _in_proj_bwd(dz, dh1, x, w_all, g_mix, tm, ride=None):
    def body(i, ins, consts, outs, accs):
        dz_ref, dh1_ref, x_ref = ins
        w_ref, g_ref = consts
        dx_ref, = outs
        dgx_ref, = accs
        du1 = _dot_nt(dz_ref[...], w_ref[...])
        dh, dg = _rms_bwd(x_ref[...], g_ref[...], du1)
        _acc_rows(dgx_ref, i, dg)
        dx_ref[...] = dh1_ref[...] + dh

    return _row_call(body, "in_proj_bwd", x.shape[0], tm, [dz, dh1, x], [w_all, g_mix],
                     [(D_MODEL, F32)], [D_MODEL], ride)


def _matmul_tn(a, b, name, ts, stack_cols=0):
    n_rows, ka = a.shape
    n = b.shape[1]
    tk = min(ka, 1024)
    tn = 896 if n % 1024 else 1024
    n_stack = tn // stack_cols if stack_cols else 0
    assert ka % tk == 0 and n % tn == 0 and n_rows % ts == 0 and (not stack_cols or tk == ka)
    n_steps = n_rows // ts

    def kern(a_ref, b_ref, o_ref, acc_ref):
        s = pl.program_id(2)

        @pl.when(s == 0)
        def _():
            acc_ref[...] = jnp.zeros_like(acc_ref)
        acc_ref[...] += _dot_tn(a_ref[...].astype(BF16), b_ref[...])

        @pl.when(s == n_steps - 1)
        def _():
            if stack_cols:
                for c in range(n_stack):
                    o_ref[c] = acc_ref[:, c * stack_cols:(c + 1) * stack_cols].astype(BF16)
            else:
                o_ref[...] = acc_ref[...].astype(BF16)

    if stack_cols:
        out_spec = pl.BlockSpec((n_stack, tk, stack_cols), lambda i, j, s: (j, 0, 0))
        out_shape = jax.ShapeDtypeStruct((n // stack_cols, ka, stack_cols), BF16)
    else:
        out_spec = pl.BlockSpec((tk, tn), lambda i, j, s: (i, j))
        out_shape = jax.ShapeDtypeStruct((ka, n), BF16)
    return pl.pallas_call(
        kern, grid=(ka // tk, n // tn, n_steps),
        in_specs=[pl.BlockSpec((ts, tk), lambda i, j, s: (s, i)), pl.BlockSpec((ts, tn), lambda i, j, s: (s, j))],
        out_specs=out_spec, out_shape=out_shape, scratch_shapes=[pltpu.VMEM((tk, tn), F32)], name=name,
        compiler_params=_params(3))(a, b)


SCAN_CHUNK = 512


def _decay_cumsum(f_t, b_col):
    n_tok = f_t.shape[1]
    ch = min(SCAN_CHUNK, n_tok)

    def kern(f_ref, b_ref, c_ref):
        r = lax.broadcasted_iota(jnp.int32, (ch, ch), 0)
        c = lax.broadcasted_iota(jnp.int32, (ch, ch), 1)
        tri = (r <= c).astype(F32)
        carry = jnp.zeros((8, 1), F32)
        for k in range(n_tok // ch):
            xv = f_ref[:, k * ch:(k + 1) * ch] + b_ref[...]
            lf = jnp.minimum(xv, 0.0) - jnp.log(1.0 + jnp.exp(-jnp.abs(xv)))
            cs = jnp.dot(lf, tri, precision=lax.Precision.HIGHEST, preferred_element_type=F32) + carry
            c_ref[:, k * ch:(k + 1) * ch] = cs
            carry = cs[:, ch - 1:ch]

    return pl.pallas_call(kern, out_shape=jax.ShapeDtypeStruct((8, n_tok), F32), name="decay_cumsum",
                          compiler_params=_params(0))(f_t, b_col)


def _decay_bwd(cs, rs, f_t, b_col):
    n_tok = f_t.shape[1]
    ch = min(SCAN_CHUNK, n_tok)
    n_ch = n_tok // ch

    def kern(cs_ref, rs_ref, f_ref, b_ref, df_ref, db_ref, carry_ref):
        k = pl.program_id(0)

        @pl.when(k == 0)
        def _():
            carry_ref[...] = jnp.zeros_like(carry_ref)
            db_ref[...] = jnp.zeros_like(db_ref)

        r = lax.broadcasted_iota(jnp.int32, (ch, ch), 0)
        c = lax.broadcasted_iota(jnp.int32, (ch, ch), 1)
        tri = (r >= c).astype(F32)
        head = lax.broadcasted_iota(jnp.int32, (8, 4 * LANES), 0)
        lane = lax.broadcasted_iota(jnp.int32, (8, 4 * LANES), 1)
        pick = (lane == HEAD_DIM * head).astype(F32)
        dc = lax.dot_general(pick, rs_ref[...] - cs_ref[...], _NT, precision=lax.Precision.HIGHEST,
                             preferred_element_type=F32)
        rc = jnp.dot(dc, tri, precision=lax.Precision.HIGHEST, preferred_element_type=F32) + carry_ref[:, 0:1]
        carry_ref[...] = jnp.broadcast_to(rc[:, 0:1], carry_ref.shape)
        df = rc / (1.0 + jnp.exp(f_ref[...] + b_ref[...]))
        df_ref[...] = df
        db_ref[...] += jnp.broadcast_to(jnp.sum(df, axis=1, keepdims=True), db_ref.shape)

    back = lambda k: n_ch - 1 - k
    wide = pl.BlockSpec((ch, 4 * LANES), lambda k: (back(k), 0))
    row = pl.BlockSpec((8, ch), lambda k: (0, back(k)))
    return pl.pallas_call(
        kern, grid=(n_ch,),
        in_specs=[wide, wide, row, pl.BlockSpec((8, 1), lambda k: (0, 0))],
        out_specs=[row, pl.BlockSpec((8, LANES), lambda k: (0, 0))],
        out_shape=[jax.ShapeDtypeStruct((8, n_tok), F32), jax.ShapeDtypeStruct((8, LANES), F32)],
        scratch_shapes=[pltpu.VMEM((8, LANES), F32)], name="decay_bwd", compiler_params=_params(1))(cs, rs, f_t, b_col)


def _swa_band_mask(n):
    row = lax.broadcasted_iota(jnp.int32, (SWA_BLOCK, 2 * SWA_BLOCK), 0) + SWA_BLOCK
    col = lax.broadcasted_iota(jnp.int32, (SWA_BLOCK, 2 * SWA_BLOCK), 1)
    cd = (row >> CHUNK_SHIFT) - (col >> CHUNK_SHIFT)
    first_real = jnp.where(n > 0, 0, SWA_BLOCK)
    ok = (cd >= 0) & (cd <= WINDOW_CHUNKS) & (col >= first_real)
    dist = jnp.abs(row - col).astype(F32)
    return ok, dist


def _swap_halves(t):
    return pltpu.roll(t.astype(F32), HEAD_DIM, axis=1).astype(t.dtype)


def _swa_specs():
    blk = SWA_BLOCK
    q = pl.BlockSpec((blk, 4 * LANES), lambda n: (n, 0))
    kp = pl.BlockSpec((blk, LANES), lambda n: (jnp.maximum(n - 1, 0), 4))
    kc = pl.BlockSpec((blk, LANES), lambda n: (n, 4))
    vp = pl.BlockSpec((blk, LANES), lambda n: (jnp.maximum(n - 1, 0), 5))
    vc = pl.BlockSpec((blk, LANES), lambda n: (n, 5))
    return q, kp, kc, vp, vc


SWA_GROUPS = ([h for h in range(SWA_HEADS) if h % 2 == h // 4], [h for h in range(SWA_HEADS) if h % 2 != h // 4])


def _stack_heads(ref, heads, lo, mask_halves):
    tiles = []
    for h in heads:
        t = ref[:, (h // 2) * LANES:(h // 2 + 1) * LANES]
        tiles.append(jnp.where(lo if h % 2 == 0 else ~lo, t, jnp.zeros_like(t)) if mask_halves else t)
    return jnp.concatenate(tiles, axis=0)


def _per_head_column(values, heads):
    return jnp.concatenate([jnp.full((SWA_BLOCK, 1), values(h), F32) for h in heads], axis=0)


def _swa_scores(q_ref, kx, heads, lo, ok, dist):
    qa = _stack_heads(q_ref, heads, lo, True) * SCALE
    s = _dot_nt(qa, kx) - _per_head_column(lambda h: 2.0 ** -(h + 1), heads) * dist
    return qa, jnp.where(ok, s, NEG)


def _swa_fwd(zm, sinks):
    n_tok = zm.shape[0]
    blk = SWA_BLOCK

    def kern(q_ref, kp_ref, kc_ref, vp_ref, vc_ref, sink_ref, o_ref, lse_ref):
        n = pl.program_id(0)
        ok, dist = _swa_band_mask(n)
        k2 = jnp.concatenate([kp_ref[...], kc_ref[...]], axis=0)
        v2 = jnp.concatenate([vp_ref[...], vc_ref[...]], axis=0)
        ksw, vsw = _swap_halves(k2), _swap_halves(v2)
        lane = lax.broadcasted_iota(jnp.int32, (blk, LANES), 1)
        lo = lane < HEAD_DIM
        lse_t = jnp.zeros((blk, LANES), F32)
        for pair in range(SWA_HEADS // 2):
            q2 = q_ref[:, pair * LANES:(pair + 1) * LANES]
            outs = []
            for a in range(2):
                h = 2 * pair + a
                qa = jnp.where(lo if a == 0 else ~lo, q2, jnp.zeros_like(q2)) * SCALE
                kx, vx = (k2, v2) if h in SWA_GROUPS[0] else (ksw, vsw)
                s = _dot_nt(qa, kx)
                s = jnp.where(ok, s - (2.0 ** -(h + 1)) * dist, NEG)
                sink = sink_ref[h]
                m = jnp.maximum(jnp.max(s, axis=-1, keepdims=True), sink)
                e = jnp.exp(s - m)
                l = jnp.sum(e, axis=-1, keepdims=True) + jnp.exp(sink - m)
                pn = (e * (1.0 / l)).astype(BF16)
                outs.append(_dot(pn, vx))
                lse_t = jnp.where(lane == h, m + jnp.log(l), lse_t)
            o_ref[:, pair * LANES:(pair + 1) * LANES] = jnp.where(lo, outs[0], outs[1]).astype(BF16)
        lse_ref[...] = lse_t

    q, kp, kc, vp, vc = _swa_specs()
    return pl.pallas_call(
        kern, grid=(n_tok // blk,),
        in_specs=[q, kp, kc, vp, vc, pl.BlockSpec(memory_space=pltpu.SMEM)],
        out_specs=[pl.BlockSpec((blk, 4 * LANES), lambda n: (n, 0)), pl.BlockSpec((blk, LANES), lambda n: (n, 0))],
        out_shape=[jax.ShapeDtypeStruct((n_tok, 4 * LANES), BF16), jax.ShapeDtypeStruct((n_tok, LANES), F32)],
        name="swa_fwd", compiler_params=_params(1))(zm, zm, zm, zm, zm, sinks)


def _swa_bwd(zm, sinks, d_out, out, lse):
    n_tok = zm.shape[0]
    blk = SWA_BLOCK

    def kern(q_ref, kp_ref, kc_ref, vp_ref, vc_ref, do_ref, o_ref, lse_ref, sink_ref,
             dq_ref, dkp_ref, dkc_ref, dvp_ref, dvc_ref, dsk_ref):
        n = pl.program_id(0)

        @pl.when(n == 0)
        def _():
            dsk_ref[...] = jnp.zeros_like(dsk_ref)

        ok, dist = _swa_band_mask(n)
        ok, dist = jnp.concatenate([ok] * 4, axis=0), jnp.concatenate([dist] * 4, axis=0)
        k2 = jnp.concatenate([kp_ref[...], kc_ref[...]], axis=0)
        v2 = jnp.concatenate([vp_ref[...], vc_ref[...]], axis=0)
        lane = lax.broadcasted_iota(jnp.int32, (blk, LANES), 1)
        lo = lane < HEAD_DIM
        lse_t = lse_ref[...]
        dqs, dkv = {}, []
        for heads, kx, vx in ((SWA_GROUPS[0], k2, v2), (SWA_GROUPS[1], _swap_halves(k2), _swap_halves(v2))):
            qa, s = _swa_scores(q_ref, kx, heads, lo, ok, dist)
            doa = _stack_heads(do_ref, heads, lo, True)
            lse_g = jnp.concatenate([lse_t[:, h:h + 1] for h in heads], axis=0)
            prob = jnp.exp(s - lse_g)
            dd = jnp.sum(doa.astype(F32) * _stack_heads(o_ref, heads, lo, False).astype(F32), axis=-1, keepdims=True)
            ds = (prob * (_dot_nt(doa, vx) - dd)).astype(BF16)
            sink_part = -jnp.exp(_per_head_column(lambda h: sink_ref[h], heads) - lse_g) * dd
            dq = _dot(ds, kx) * SCALE
            for r, h in enumerate(heads):
                dqs[h] = dq[r * blk:(r + 1) * blk]
                dsk_ref[h:h + 1, :] += jnp.broadcast_to(
                    jnp.sum(sink_part[r * blk:(r + 1) * blk], axis=0, keepdims=True), (1, LANES))
            dkv.append((_dot_tn(ds, qa), _dot_tn(prob.astype(BF16), doa)))
        for pair in range(SWA_HEADS // 2):
            dq_ref[:, pair * LANES:(pair + 1) * LANES] = jnp.where(lo, dqs[2 * pair], dqs[2 * pair + 1]).astype(BF16)
        dk = dkv[0][0] + pltpu.roll(dkv[1][0], HEAD_DIM, axis=1)
        dv = dkv[0][1] + pltpu.roll(dkv[1][1], HEAD_DIM, axis=1)
        dkp_ref[...] = dk[0:blk]
        dkc_ref[...] = dk[blk:2 * blk]
        dvp_ref[...] = dv[0:blk]
        dvc_ref[...] = dv[blk:2 * blk]

    q, kp, kc, vp, vc = _swa_specs()
    wide = pl.BlockSpec((blk, 4 * LANES), lambda n: (n, 0))
    narrow = pl.BlockSpec((blk, LANES), lambda n: (n, 0))
    part = jax.ShapeDtypeStruct((n_tok, LANES), F32)
    return pl.pallas_call(
        kern, grid=(n_tok // blk,),
        in_specs=[q, kp, kc, vp, vc, wide, wide, narrow, pl.BlockSpec(memory_space=pltpu.SMEM)],
        out_specs=[wide, narrow, narrow, narrow, narrow, pl.BlockSpec((8, LANES), lambda n: (0, 0))],
        out_shape=[jax.ShapeDtypeStruct((n_tok, 4 * LANES), BF16), part, part, part, part,
                   jax.ShapeDtypeStruct((8, LANES), F32)],
        name="swa_bwd", compiler_params=_params(1))(zm, zm, zm, zm, zm, d_out, out, lse, sinks)


def _my_pos():
    return lax.axis_index("x"), lax.axis_index("y"), lax.axis_index("c")


def _peer(k):
    x, y, c = _my_pos()
    px, py, pc = x ^ (k >> 2), y ^ ((k >> 1) & 1), c ^ (k & 1)
    return (px, py, pc), 4 * px + 2 * py + pc


def _gather_copies(x_refs, out_refs, send_sems, recv_sems, local_sems):
    x, y, c = _my_pos()
    my_id = 4 * x + 2 * y + c
    local = [pltpu.make_async_copy(x_refs[w], out_refs[w].at[my_id], local_sems.at[w]) for w in range(len(x_refs))]
    sends, arrivals = [], []
    for k in range(1, N_DEV):
        peer, peer_id = _peer(k)
        for w in range(len(x_refs)):
            sems = dict(send_sem=send_sems.at[7 * w + k - 1], recv_sem=recv_sems.at[7 * w + k - 1],
                        device_id=peer, device_id_type=MESH)
            sends.append(pltpu.make_async_remote_copy(src_ref=x_refs[w], dst_ref=out_refs[w].at[my_id], **sems))
            arrivals.append(pltpu.make_async_remote_copy(src_ref=x_refs[w], dst_ref=out_refs[w].at[peer_id], **sems))
    return local, sends, arrivals


def _scatter_copies(g_refs, part_refs, send_sems, recv_sems, local_sems):
    x, y, c = _my_pos()
    my_id = 4 * x + 2 * y + c
    local = [pltpu.make_async_copy(g_refs[w].at[my_id], part_refs[w].at[0], local_sems.at[w])
             for w in range(len(g_refs))]
    sends, arrivals = [], []
    for k in range(1, N_DEV):
        peer, peer_id = _peer(k)
        for w in range(len(g_refs)):
            sems = dict(send_sem=send_sems.at[7 * w + k - 1], recv_sem=recv_sems.at[7 * w + k - 1],
                        device_id=peer, device_id_type=MESH)
            sends.append(pltpu.make_async_remote_copy(src_ref=g_refs[w].at[peer_id], dst_ref=part_refs[w].at[k], **sems))
            arrivals.append(pltpu.make_async_remote_copy(src_ref=g_refs[w].at[my_id], dst_ref=part_refs[w].at[k], **sems))
    return local, sends, arrivals


def _start_copies(local, sends, arrivals):
    for cp in local + sends:
        cp.start()


def _finish_copies(local, sends, arrivals):
    for cp in arrivals:
        cp.wait_recv()
    for cp in sends:
        cp.wait_send()
    for cp in local:
        cp.wait()


def _exchange_scratch(n_arrays):
    return [pltpu.SemaphoreType.DMA((7 * n_arrays,)), pltpu.SemaphoreType.DMA((7 * n_arrays,)),
            pltpu.SemaphoreType.DMA((n_arrays,))]


class _Ride:
    def __init__(self, arrays, out_shape, copies):
        self.arrays, self.out_shape, self.copies = list(arrays), list(out_shape), copies
        any_spec = pl.BlockSpec(memory_space=pl.ANY)
        self.in_specs = [any_spec] * len(self.arrays)
        self.out_specs = [any_spec] * len(self.arrays)
        self.scratch = _exchange_scratch(len(self.arrays)) if self.arrays else []

    def specs(self):
        return self

    @staticmethod
    def _at(grid, last):
        hit = [pl.program_id(d) == (n - 1 if last else 0) for d, n in enumerate(grid)]
        return hit[0] if len(hit) == 1 else jnp.logical_and(*hit)

    def at_first_step(self, grid, in_refs, out_refs, sems):
        @pl.when(self._at(grid, False))
        def _():
            _start_copies(*self.copies(in_refs, out_refs, *sems))

    def at_last_step(self, grid, in_refs, out_refs, sems):
        @pl.when(self._at(grid, True))
        def _():
            _finish_copies(*self.copies(in_refs, out_refs, *sems))


_NO_RIDE = _Ride([], [], None)


def _gather_ride(shards):
    return _Ride(shards, [jax.ShapeDtypeStruct((N_DEV,) + s.shape, s.dtype) for s in shards], _gather_copies)


def _scatter_ride(grads):
    return _Ride(grads, [jax.ShapeDtypeStruct(g.shape, g.dtype) for g in grads], _scatter_copies)


Q_COL, K_COL, V_COL = 6, 10, 14


def _causal(t, tq, tk):
    row = lax.broadcasted_iota(jnp.int32, (tq, tk), 0)
    col = lax.broadcasted_iota(jnp.int32, (tq, tk), 1)
    return jnp.where(col <= row, t, NEG)


def _lane_tile(stat, width):
    return jnp.tile(stat, (1, width // LANES))


def _fox_steps(nq):
    steps = [(i2, j, 0 if j < 2 * i2 else 1 + j - 2 * i2) for i2 in range(nq // 2) for j in range(2 * i2 + 2)]
    return [np.asarray(col, np.int32) for col in zip(*steps)]


_SWEEPS = {0: [(0, False), (1, False)], 1: [(0, True), (1, False)], 2: [(1, True)]}


EXP_ZERO = 110.0
NORM_SLACK = 1.001


def _fox_dead_steps(zm, c_pairs, tq):
    n_tok = zm.shape[0]
    nq = n_tok // tq

    def block_norms(col):
        t = zm[:, col * LANES:(col + FOX_HEADS // 2) * LANES].astype(F32).reshape(nq, tq, FOX_HEADS, HEAD_DIM)
        return jnp.sqrt(jnp.max(jnp.sum(t * t, axis=-1), axis=1))

    qn, kn = block_norms(Q_COL) * SCALE, block_norms(K_COL)
    cb = c_pairs.reshape(FOX_HEADS, nq, tq)
    c_max, c_min = jnp.max(cb, axis=-1).T, jnp.min(cb, axis=-1).T
    both = lambda t: jnp.max(t.reshape(nq // 2, 2, FOX_HEADS), axis=1)
    qn2, kn2, c_max2 = both(qn), both(kn), both(c_max)
    gap = qn2[:, None] * (kn[None] + kn2[:, None]) * NORM_SLACK + (c_max2[:, None] - c_min[None])
    below = jnp.arange(nq)[None, :] < 2 * jnp.arange(nq // 2)[:, None]
    dead = jnp.logical_and(gap < -EXP_ZERO, below[..., None])
    dead = jnp.all(dead.reshape(nq // 2, nq, FOX_HEADS // 2, 2), axis=-1)
    return dead.transpose(2, 0, 1).reshape(FOX_HEADS // 2, -1).astype(F32)


def _fox_fwd(zm, c_pairs, dead, tq, ride=None):
    n_tok = zm.shape[0]
    nq = n_tok // tq
    ii, jj, kk = _fox_steps(nq)
    n_steps = len(ii)
    n_ride = len(ride.arrays) if ride else 0

    def kern(ii_ref, jj_ref, kk_ref, q_ref, k_ref, v_ref, ck_ref, dead_ref, *more):
        ride_in, (o_ref, ln_ref), ride_out = more[:n_ride], more[n_ride:n_ride + 2], more[n_ride + 2:2 * n_ride + 2]
        qs_ref, m_ref, l_ref, acc_ref = more[2 * n_ride + 2:2 * n_ride + 6]
        step = pl.program_id(1)
        j, kind = jj_ref[step], kk_ref[step]
        lo = lax.broadcasted_iota(jnp.int32, (2 * tq, LANES), 1) < HEAD_DIM
        if ride:
            ride.at_first_step((FOX_HEADS // 2, n_steps), ride_in, ride_out, more[2 * n_ride + 6:])

        @pl.when(j == 0)
        def _():
            q2 = q_ref[...]
            zq = jnp.zeros_like(q2)
            qs_ref[0] = jnp.where(lo, q2, zq) * SCALE
            qs_ref[1] = jnp.where(lo, zq, q2) * SCALE
            m_ref[...] = jnp.full(m_ref.shape, NEG, F32)
            l_ref[...] = jnp.zeros(l_ref.shape, F32)
            acc_ref[...] = jnp.zeros(acc_ref.shape, F32)

        def sweep(subs):
            kv = k_ref[...]
            v_ones = jnp.concatenate([v_ref[...], jnp.ones((tq, LANES), BF16)], axis=1)
            for sub, diag in subs:
                rows = slice(sub * tq, (sub + 1) * tq)
                for a in range(2):
                    t = _dot_nt(qs_ref[a, rows], kv) - ck_ref[a:a + 1, :]
                    if diag:
                        t = _causal(t, tq, tq)
                    m_old = m_ref[a, rows]
                    m_new = jnp.maximum(m_old, jnp.max(t, axis=-1, keepdims=True))
                    alpha = jnp.exp(m_old - m_new)
                    e = jnp.exp(t - _lane_tile(m_new, tq)).astype(BF16)
                    pv = _dot(e, v_ones)
                    acc_ref[a, rows] = alpha * acc_ref[a, rows] + pv[:, :LANES]
                    l_ref[a, rows] = alpha * l_ref[a, rows] + pv[:, LANES:]
                    m_ref[a, rows] = m_new

        live = dead_ref[pl.program_id(0), ii_ref[step] * nq + j] < 0.5
        for kind_id, subs in _SWEEPS.items():
            pl.when(jnp.logical_and(kind == kind_id, live))(lambda subs=subs: sweep(subs))

        @pl.when(kind == 2)
        def _():
            o_ref[...] = jnp.where(lo, acc_ref[0] / l_ref[0], acc_ref[1] / l_ref[1]).astype(BF16)
            ln_ref[:, :LANES] = m_ref[0] + jnp.log(l_ref[0])
            ln_ref[:, LANES:] = m_ref[1] + jnp.log(l_ref[1])

        if ride:
            ride.at_last_step((FOX_HEADS // 2, n_steps), ride_in, ride_out, more[2 * n_ride + 6:])

    blk = (tq, LANES)
    by_i = lambda col: (lambda hp, s, ii, jj, kk: (ii[s], col + hp))
    by_j = lambda col: (lambda hp, s, ii, jj, kk: (jj[s], col + hp))
    extra = ride.specs() if ride else _NO_RIDE
    grid_spec = pltpu.PrefetchScalarGridSpec(
        num_scalar_prefetch=3, grid=(FOX_HEADS // 2, n_steps),
        in_specs=[pl.BlockSpec((2 * tq, LANES), by_i(Q_COL)), pl.BlockSpec(blk, by_j(K_COL)),
                  pl.BlockSpec(blk, by_j(V_COL)),
                  pl.BlockSpec((None, 2, tq), lambda hp, s, ii, jj, kk: (hp, 0, jj[s])),
                  pl.BlockSpec(memory_space=pltpu.SMEM)] + extra.in_specs,
        out_specs=[pl.BlockSpec((2 * tq, LANES), by_i(0)), pl.BlockSpec((2 * tq, 2 * LANES), by_i(0))] + extra.out_specs,
        scratch_shapes=[pltpu.VMEM((2, 2 * tq, LANES), BF16), pltpu.VMEM((2, 2 * tq, LANES), F32),
                        pltpu.VMEM((2, 2 * tq, LANES), F32), pltpu.VMEM((2, 2 * tq, LANES), F32)] + extra.scratch)
    return pl.pallas_call(
        kern, grid_spec=grid_spec,
        out_shape=[jax.ShapeDtypeStruct((n_tok, 4 * LANES), BF16),
                   jax.ShapeDtypeStruct((n_tok, FOX_HEADS * LANES), F32)] + extra.out_shape,
        name="fox_fwd", compiler_params=_params(2))(ii, jj, kk, zm, zm, zm, c_pairs, dead, *extra.arrays)


def _fox_delta(d_out, out, tm):
    def body(i, ins, consts, outs, accs):
        do_ref, o_ref = ins
        dl_ref, = outs
        lane = lax.broadcasted_iota(jnp.int32, (tm, LANES), 1)
        lo = lane < HEAD_DIM
        for pair in range(FOX_HEADS // 2):
            cols = slice(pair * LANES, (pair + 1) * LANES)
            prod = do_ref[:, cols].astype(F32) * o_ref[:, cols].astype(F32)
            for a in range(2):
                dd = jnp.sum(jnp.where(lo if a == 0 else ~lo, prod, 0.0), axis=-1, keepdims=True)
                h = 2 * pair + a
                dl_ref[:, h * LANES:(h + 1) * LANES] = jnp.broadcast_to(dd, (tm, LANES))

    return _row_call(body, "fox_delta", d_out.shape[0], tm, [d_out, out], [], [(FOX_HEADS * LANES, F32)], [])[0]


def _fox_bwd(zm, c_pairs, dead, d_out, lnorm, delta, tq, ride=None):
    n_tok = zm.shape[0]
    nq = n_tok // tq
    ii, jj, kk = _fox_steps(nq)
    n_steps = len(ii)
    n_ride = len(ride.arrays) if ride else 0

    def kern(ii_ref, jj_ref, kk_ref, q_ref, k_ref, v_ref, ck_ref, dead_ref, do_ref, ln_ref, dl_ref, *more):
        ride_in, ride_out = more[:n_ride], more[n_ride + 5:2 * n_ride + 5]
        dq_ref, dk_ref, dv_ref, cs_ref, rs_ref = more[n_ride:n_ride + 5]
        qs_ref, qo_ref, dos_ref, dq_acc = more[2 * n_ride + 5:2 * n_ride + 9]
        step = pl.program_id(1)
        j, kind = jj_ref[step], kk_ref[step]
        lo = lax.broadcasted_iota(jnp.int32, (2 * tq, LANES), 1) < HEAD_DIM
        if ride:
            ride.at_first_step((FOX_HEADS // 2, n_steps), ride_in, ride_out, more[2 * n_ride + 9:])

        @pl.when(step == 0)
        def _():
            dk_ref[...] = jnp.zeros_like(dk_ref)
            dv_ref[...] = jnp.zeros_like(dv_ref)
            cs_ref[...] = jnp.zeros_like(cs_ref)

        @pl.when(j == 0)
        def _():
            q2, do2 = q_ref[...], do_ref[...]
            zq = jnp.zeros_like(q2)
            ones = jnp.ones((2 * tq, LANES), BF16)
            for a in range(2):
                half = lo if a == 0 else ~lo
                qa = jnp.where(half, q2, zq) * SCALE
                qs_ref[a] = qa
                qo_ref[a] = jnp.concatenate([qa, ones], axis=1)
                dos_ref[a] = jnp.where(half, do2, zq)
            dq_acc[...] = jnp.zeros(dq_acc.shape, F32)

        def sweep(subs):
            kv, vv = k_ref[...], v_ref[...]
            k_ones = jnp.concatenate([kv, jnp.ones((tq, LANES), BF16)], axis=1)
            dk, dv, sums = None, None, [None, None]
            for sub, diag in subs:
                rows = slice(sub * tq, (sub + 1) * tq)
                for a in range(2):
                    t = _dot_nt(qs_ref[a, rows], kv) - ck_ref[a:a + 1, :]
                    if diag:
                        t = _causal(t, tq, tq)
                    prob = jnp.exp(t - _lane_tile(ln_ref[rows, a * LANES:(a + 1) * LANES], tq))
                    dp = _dot_nt(dos_ref[a, rows], vv)
                    ds = (prob * (dp - _lane_tile(dl_ref[rows, a * LANES:(a + 1) * LANES], tq))).astype(BF16)
                    dq_acc[a, rows] += _dot(ds, k_ones)
                    dk_cs = _dot_tn(ds, qo_ref[a, rows])
                    dv_a = _dot_tn(prob.astype(BF16), dos_ref[a, rows])
                    dk = dk_cs[:, :LANES] if dk is None else dk + dk_cs[:, :LANES]
                    dv = dv_a if dv is None else dv + dv_a
                    sums[a] = dk_cs[:, LANES:] if sums[a] is None else sums[a] + dk_cs[:, LANES:]
            keys = pl.ds(pl.multiple_of(j * tq, tq), tq)
            dk_ref[keys, :] += dk
            cs_ref[keys, :] += jnp.where(lo[:tq], sums[0], sums[1])
            dv_ref[keys, :] += dv

        live = dead_ref[pl.program_id(0), ii_ref[step] * nq + j] < 0.5
        for kind_id, subs in _SWEEPS.items():
            pl.when(jnp.logical_and(kind == kind_id, live))(lambda subs=subs: sweep(subs))

        @pl.when(kind == 2)
        def _():
            dq_ref[...] = jnp.where(lo, dq_acc[0, :, :LANES], dq_acc[1, :, :LANES]) * SCALE
            rs_ref[...] = jnp.where(lo, dq_acc[0, :, LANES:], dq_acc[1, :, LANES:])

        if ride:
            ride.at_last_step((FOX_HEADS // 2, n_steps), ride_in, ride_out, more[2 * n_ride + 9:])

    blk = (tq, LANES)
    by_i = lambda col: (lambda hp, s, ii, jj, kk: (ii[s], col + hp))
    by_j = lambda col: (lambda hp, s, ii, jj, kk: (jj[s], col + hp))
    resident = pl.BlockSpec((2 * tq, LANES), by_i(0))
    stat = pl.BlockSpec((2 * tq, 2 * LANES), by_i(0))
    whole = pl.BlockSpec((n_tok, LANES), lambda hp, s, ii, jj, kk: (0, hp))
    extra = ride.specs() if ride else _NO_RIDE
    grid_spec = pltpu.PrefetchScalarGridSpec(
        num_scalar_prefetch=3, grid=(FOX_HEADS // 2, n_steps),
        in_specs=[pl.BlockSpec((2 * tq, LANES), by_i(Q_COL)), pl.BlockSpec(blk, by_j(K_COL)),
                  pl.BlockSpec(blk, by_j(V_COL)),
                  pl.BlockSpec((None, 2, tq), lambda hp, s, ii, jj, kk: (hp, 0, jj[s])),
                  pl.BlockSpec(memory_space=pltpu.SMEM), resident, stat, stat] + extra.in_specs,
        out_specs=[resident, whole, whole, whole, resident] + extra.out_specs,
        scratch_shapes=[pltpu.VMEM((2, 2 * tq, LANES), BF16), pltpu.VMEM((2, 2 * tq, 2 * LANES), BF16),
                        pltpu.VMEM((2, 2 * tq, LANES), BF16), pltpu.VMEM((2, 2 * tq, 2 * LANES), F32)] + extra.scratch)
    wide = jax.ShapeDtypeStruct((n_tok, 4 * LANES), F32)
    return pl.pallas_call(
        kern, grid_spec=grid_spec, out_shape=[wide] * 5 + extra.out_shape, name="fox_bwd",
        compiler_params=_params(2, FOX_BWD_VMEM))(ii, jj, kk, zm, zm, zm, c_pairs, dead, d_out, lnorm, delta,
                                                  *extra.arrays)


def _all_gather(shards):
    n_w = len(shards)

    def kern(*refs):
        x_refs, out_refs = refs[:n_w], refs[n_w:2 * n_w]
        send_sems, recv_sems, local_sems = refs[2 * n_w:]
        x, y, c = _my_pos()
        me, sibling = (x, y, c), (x, y, 1 - c)
        chips = [(1 - x, y), (x, 1 - y), (1 - x, 1 - y)]

        def slot(w, px, py, pc):
            return out_refs[w].at[4 * px + 2 * py + pc]

        def copy(w, k, block, to, src=None):
            return pltpu.make_async_remote_copy(
                src_ref=slot(w, *block) if src is None else src, dst_ref=slot(w, *block),
                send_sem=send_sems.at[7 * w + k], recv_sem=recv_sems.at[7 * w + k], device_id=to, device_id_type=MESH)

        local, started = [], []
        for w in range(n_w):
            mine = pltpu.make_async_copy(x_refs[w], slot(w, *me), local_sems.at[w])
            mine.start()
            local.append(mine)
            first = [copy(w, 0, me, sibling, src=x_refs[w])]
            first += [copy(w, 1 + k, me, (*chip, c), src=x_refs[w]) for k, chip in enumerate(chips)]
            for cp in first:
                cp.start()
            started += first
        for k, chip in enumerate(chips):
            for w in range(n_w):
                copy(w, 1 + k, (*chip, c), me).wait_recv()
                passed = copy(w, 4 + k, (*chip, c), sibling)
                passed.start()
                started.append(passed)
        for w in range(n_w):
            copy(w, 0, sibling, me).wait_recv()
            for k, chip in enumerate(chips):
                copy(w, 4 + k, (*chip, 1 - c), me).wait_recv()
        for cp in started:
            cp.wait_send()
        for cp in local:
            cp.wait()

    any_spec = pl.BlockSpec(memory_space=pl.ANY)
    return pl.pallas_call(
        kern, out_shape=[jax.ShapeDtypeStruct((N_DEV,) + s.shape, s.dtype) for s in shards],
        in_specs=[any_spec] * n_w, out_specs=[any_spec] * n_w,
        scratch_shapes=[pltpu.SemaphoreType.DMA((7 * n_w,)), pltpu.SemaphoreType.DMA((7 * n_w,)),
                        pltpu.SemaphoreType.DMA((n_w,))],
        name="weight_all_gather")(*shards)


def _small_exchange(small):
    def kern(s_ref, sall_ref, *sems):
        copies = _gather_copies([s_ref], [sall_ref], *sems)
        _start_copies(*copies)
        _finish_copies(*copies)

    any_spec = pl.BlockSpec(memory_space=pl.ANY)
    return pl.pallas_call(
        kern, out_shape=jax.ShapeDtypeStruct((N_DEV,) + small.shape, small.dtype), in_specs=[any_spec],
        out_specs=any_spec, scratch_shapes=_exchange_scratch(1), name="small_grad_exchange")(small)


ADAMW_BLOCK_BYTES = 2 * 1024 * 1024


def _adamw(parts, w, m, v, name):
    n_parts, n_rows, n_cols = parts.shape
    limit = max(8, ADAMW_BLOCK_BYTES // (n_parts * n_cols * parts.dtype.itemsize))
    tr = max(t for t in range(8, n_rows + 1, 8) if n_rows % t == 0 and t <= limit)

    def kern(p_ref, w_ref, m_ref, v_ref, g_out, d_out, m_out, v_out):
        g = p_ref[0].astype(F32)
        for k in range(1, n_parts):
            g = g + p_ref[k].astype(F32)
        m_new = ADAM_B1 * m_ref[...] + (1.0 - ADAM_B1) * g
        v_new = ADAM_B2 * v_ref[...] + (1.0 - ADAM_B2) * jnp.square(g)
        m_hat = m_new / (1.0 - ADAM_B1 ** ADAM_STEP)
        v_hat = v_new / (1.0 - ADAM_B2 ** ADAM_STEP)
        g_out[...] = g
        d_out[...] = -ADAM_LR * (m_hat / (jnp.sqrt(v_hat) + ADAM_EPS) + ADAM_WD * w_ref[...])
        m_out[...] = m_new
        v_out[...] = v_new

    row = pl.BlockSpec((tr, n_cols), lambda i: (i, 0))
    out = jax.ShapeDtypeStruct((n_rows, n_cols), F32)
    return pl.pallas_call(
        kern, grid=(n_rows // tr,),
        in_specs=[pl.BlockSpec((n_parts, tr, n_cols), lambda i: (0, i, 0)), row, row, row],
        out_specs=[row, row, row, row], out_shape=[out, out, out, out], name=name,
        compiler_params=_params(1))(parts, w, m, v)


SHARDED = {
    "w_in": ((D_MODEL, D_IN), 1), "w_br_swa": ((512, D_MODEL), 1), "w_br_fox": ((512, D_MODEL), 1),
    "w_mix_out": ((D_MODEL, D_MODEL), 0), "w_ff1": ((D_MODEL, D_FF), 1), "w_ff2": ((D_FF, D_MODEL), 0),
    "w_ple_gate": ((D_MODEL, D_MODEL), 0), "w_ple_proj": ((PLE_DIM, D_MODEL), 1),
}
W_IN_SHARD = D_IN // N_DEV
W_IN_PAD = 640
SMALL = ("g_mix", "g_mlp", "g_ple", "g_final", "b_forget", "swa_sinks")
SMALL_COLS = 1024


def _wire_shard(name, a):
    a = a.reshape(a.shape[-2:])
    return jnp.pad(a, ((0, 0), (0, W_IN_PAD - W_IN_SHARD))) if name == "w_in" else a


def _from_wire(name, a):
    return (a[:, :W_IN_SHARD] if name == "w_in" else a)[None]


def _w_all_from_wire(stacked):
    w_in = jnp.concatenate([stacked[d][:, :W_IN_SHARD] for d in range(N_DEV)], axis=1)
    fpad = jnp.zeros((D_MODEL, N_FPAD - FOX_HEADS), stacked.dtype)
    return jnp.concatenate([w_in[:, :N_MAIN + FOX_HEADS], fpad, w_in[:, N_MAIN + FOX_HEADS:]], axis=1)


def _dw_in_to_wire(dw_all):
    dw_in = jnp.concatenate([dw_all[:, :N_MAIN + FOX_HEADS], dw_all[:, N_MAIN + N_FPAD:]], axis=1)
    pad = jnp.zeros((D_MODEL, W_IN_PAD - W_IN_SHARD), dw_all.dtype)
    return jnp.stack([jnp.concatenate([dw_in[:, d * W_IN_SHARD:(d + 1) * W_IN_SHARD], pad], axis=1)
                      for d in range(N_DEV)])


def _pack_small(vals):
    rows = [jnp.pad(vals[n].reshape(-1), (0, SMALL_COLS - vals[n].size)) for n in SMALL]
    rows += [jnp.zeros((SMALL_COLS,), F32)] * (8 - len(SMALL))
    return jnp.stack(rows)


def _unpack_small(slab, like):
    return {n: slab[r, :like[n].size].reshape(like[n].shape) for r, n in enumerate(SMALL)}


def _local_step(x, p, tgt, w, small, tm, tq, ts, late_shards=None):
    n_tok = x.shape[0]
    row = lambda v: v.reshape(1, -1)
    g_mix, g_mlp, g_ple, g_fin = row(small["g_mix"]), row(small["g_mlp"]), row(small["g_ple"]), row(small["g_final"])
    sinks = small["swa_sinks"].reshape(-1)
    b_col = small["b_forget"].reshape(FOX_HEADS, 1)

    u1, zm, zfg, zf = _in_proj(x, g_mix, w["w_all"], tm)
    f_t = zf[:, :FOX_HEADS].T
    c_pairs = _decay_cumsum(f_t, b_col).reshape(FOX_HEADS // 2, 2, n_tok)
    attn_a, lse_a = _swa_fwd(zm, sinks)
    dead = _fox_dead_steps(zm, c_pairs, tq)
    if late_shards is None:
        attn_b, ln_b = _fox_fwd(zm, c_pairs, dead, tq)
    else:
        attn_b, ln_b, *late = _fox_fwd(zm, c_pairs, dead, tq, _gather_ride(list(late_shards.values())))
        w = {**w, **_gathered_to_local(dict(zip(late_shards, late)))}
    ya, yb, mixed, h1, u2 = _mix_fwd(attn_a, attn_b, zfg, x, w["w_br_swa"], w["w_br_fox"], w["w_mix_out"], g_mlp, tm)
    a, r, h2 = _ffn_fwd(u2, h1, w["w_ff1"], w["w_ff2"], tm // 2)
    dh3, dlg, dpp, u3, loss_acc, dgf = _head_fwd_bwd(h2, p, tgt, g_ple, w["w_ple_gate"], w["w_ple_proj"], g_fin, tm)

    dh2, dh2b, da, dgp = _ffn_bwd_a(dlg, dh3, h2, a, w["w_ple_gate"], g_ple, w["w_ff2"], tm // 2)
    dh1, dh1b, dgl, dya, dyb, daa, dab, dgm = _ffn_bwd_b(
        da, dh2, h1, ya, yb, zfg, w["w_ff1"], g_mlp, w["w_mix_out"], w["w_br_swa"], w["w_br_fox"], tm // 2)
    dq_a, dkp, dkc, dvp, dvc, dsk = _swa_bwd(zm, sinks, daa, attn_a, lse_a)
    delta_b = _fox_delta(dab, attn_b, tm)
    dw = {
        "w_br_swa": _matmul_tn(attn_a, dya, "dw_br_swa", ts, stack_cols=D_MODEL // N_DEV),
        "w_br_fox": _matmul_tn(attn_b, dyb, "dw_br_fox", ts, stack_cols=D_MODEL // N_DEV),
        "w_mix_out": _matmul_tn(mixed, dh1b, "dw_mix_out", ts),
        "w_ff1": _matmul_tn(u2, da, "dw_ff1", ts, stack_cols=D_FF // N_DEV),
        "w_ff2": _matmul_tn(r, dh2b, "dw_ff2", ts),
        "w_ple_gate": _matmul_tn(u3, dlg, "dw_ple_gate", ts),
        "w_ple_proj": _matmul_tn(p, dpp, "dw_ple_proj", ts, stack_cols=D_MODEL // N_DEV),
    }
    if late_shards is None:
        dq_b, dk_b, dv_b, cs, rs = _fox_bwd(zm, c_pairs, dead, dab, ln_b, delta_b, tq)
        late_parts = None
    else:
        wire = _local_to_wire(dw)
        dq_b, dk_b, dv_b, cs, rs, *parts = _fox_bwd(zm, c_pairs, dead, dab, ln_b, delta_b, tq,
                                                    _scatter_ride([wire[n] for n in late_shards]))
        late_parts = dict(zip(late_shards, parts))

    up = lambda t: jnp.concatenate([t[SWA_BLOCK:], jnp.zeros((SWA_BLOCK, LANES), F32)], axis=0)
    dk_a, dv_a = dkc + up(dkp), dvc + up(dvp)
    df_t, db = _decay_bwd(cs, rs, f_t, b_col)
    df = jnp.pad(df_t.T, ((0, 0), (0, N_FPAD - FOX_HEADS)))
    dz = jnp.concatenate([dq_a, dk_a.astype(BF16), dv_a.astype(BF16), dq_b.astype(BF16), dk_b.astype(BF16), dv_b.astype(BF16),
                          df.astype(BF16), dgl], axis=1)
    dw["w_all"] = _matmul_tn(u1, dz, "dw_in", ts)
    if late_shards is None:
        dx, dgx = _in_proj_bwd(dz, dh1, x, w["w_all"], g_mix, tm)
    else:
        dx, dgx, late_parts["w_in"] = _in_proj_bwd(dz, dh1, x, w["w_all"], g_mix, tm,
                                                   _scatter_ride([_dw_in_to_wire(dw["w_all"])]))
    dsmall = {"g_mix": dgx[0], "g_mlp": dgm[0], "g_ple": dgp[0], "g_final": dgf[0],
              "b_forget": db[:, 0], "swa_sinks": dsk[:, 0]}
    return loss_acc[0, 0], dx, dw, dsmall, late_parts


_ROWS = lambda t: t.reshape(-1, t.shape[-1])
_BY_ROWS = lambda t: t.reshape(N_DEV, t.shape[0] // N_DEV, t.shape[1])
_SAME = lambda t: t
LOCAL_LAYOUT = {
    "w_in": ("w_all", _w_all_from_wire, _dw_in_to_wire), "w_br_swa": ("w_br_swa", _SAME, _SAME),
    "w_br_fox": ("w_br_fox", _SAME, _SAME), "w_mix_out": ("w_mix_out", _ROWS, _BY_ROWS),
    "w_ff1": ("w_ff1", _SAME, _SAME), "w_ff2": ("w_ff2", _SAME, _BY_ROWS),
    "w_ple_gate": ("w_ple_gate", _ROWS, _BY_ROWS), "w_ple_proj": ("w_ple_proj", _SAME, _SAME),
}


def _gathered_to_local(g):
    return {LOCAL_LAYOUT[n][0]: LOCAL_LAYOUT[n][1](t) for n, t in g.items()}


def _local_to_wire(dw):
    names = {local: n for n, (local, _, _) in LOCAL_LAYOUT.items()}
    return {names[local]: LOCAL_LAYOUT[names[local]][2](t) for local, t in dw.items()}


def kernel(x, p, g_mix, w_in, b_forget, swa_sinks, w_br_swa, w_br_fox, w_mix_out, g_mlp, w_ff1, w_ff2, g_ple, w_ple_gate, w_ple_proj, g_final, loss_target, m_g_mix, m_w_in, m_b_forget, m_swa_sinks, m_w_br_swa, m_w_br_fox, m_w_mix_out, m_g_mlp, m_w_ff1, m_w_ff2, m_g_ple, m_w_ple_gate, m_w_ple_proj, m_g_final, v_g_mix, v_w_in, v_b_forget, v_swa_sinks, v_w_br_swa, v_w_br_fox, v_w_mix_out, v_g_mlp, v_w_ff1, v_w_ff2, v_g_ple, v_w_ple_gate, v_w_ple_proj, v_g_final):
    given = dict(g_mix=g_mix, w_in=w_in, b_forget=b_forget, swa_sinks=swa_sinks, w_br_swa=w_br_swa, w_br_fox=w_br_fox,
                 w_mix_out=w_mix_out, g_mlp=g_mlp, w_ff1=w_ff1, w_ff2=w_ff2, g_ple=g_ple, w_ple_gate=w_ple_gate,
                 w_ple_proj=w_ple_proj, g_final=g_final)
    mom = dict(g_mix=m_g_mix, w_in=m_w_in, b_forget=m_b_forget, swa_sinks=m_swa_sinks, w_br_swa=m_w_br_swa,
               w_br_fox=m_w_br_fox, w_mix_out=m_w_mix_out, g_mlp=m_g_mlp, w_ff1=m_w_ff1, w_ff2=m_w_ff2, g_ple=m_g_ple,
               w_ple_gate=m_w_ple_gate, w_ple_proj=m_w_ple_proj, g_final=m_g_final)
    vel = dict(g_mix=v_g_mix, w_in=v_w_in, b_forget=v_b_forget, swa_sinks=v_swa_sinks, w_br_swa=v_w_br_swa,
               w_br_fox=v_w_br_fox, w_mix_out=v_w_mix_out, g_mlp=v_g_mlp, w_ff1=v_w_ff1, w_ff2=v_w_ff2, g_ple=v_g_ple,
               w_ple_gate=v_w_ple_gate, w_ple_proj=v_w_ple_proj, g_final=v_g_final)
    names = list(given)
    sharded = list(SHARDED)

    w_wire = {n: _wire_shard(n, given[n]) for n in sharded}
    late = [n for n in sharded if n != "w_in"]
    gathered = _all_gather([w_wire["w_in"].astype(BF16)])
    local_w = _gathered_to_local({"w_in": gathered[0]})
    small = {n: given[n].reshape(-1) for n in SMALL}

    n_tok = x.shape[1]
    tile = min(512, n_tok // 4)
    loss_part, dx, dw, dsmall, parts = _local_step(
        x[0], p[0, 0], loss_target[0], local_w, small, tm=tile, tq=tile, ts=min(2048, n_tok // 4),
        late_shards={n: w_wire[n].astype(BF16) for n in late})
    loss = lax.psum(loss_part, AXES)

    small_all = _small_exchange(_pack_small(dsmall))

    res = {}
    for n in sharded:
        part = parts[n]
        flat = part.reshape(N_DEV, -1, part.shape[-1])
        outs = _adamw(flat, w_wire[n], _wire_shard(n, mom[n]), _wire_shard(n, vel[n]), "adamw_" + n)
        res[n] = [_from_wire(n, o) for o in outs]
    outs_s = _adamw(small_all, _pack_small(small), _pack_small({n: mom[n] for n in SMALL}),
                    _pack_small({n: vel[n] for n in SMALL}), "adamw_small")
    small_res = [_unpack_small(o, given) for o in outs_s]

    groups = [[res[n][k] if n in res else small_res[k][n] for n in names] for k in range(4)]
    return (loss, dx[None], *groups[0], *groups[1], *groups[2], *groups[3])
```

```python
import numpy as np
import jax
import jax.numpy as jnp
from jax import lax
from jax.experimental import pallas as pl
from jax.experimental.pallas import tpu as pltpu

F32 = jnp.float32
BF16 = jnp.bfloat16

D_MODEL = 1024
HEAD_DIM = 64
SWA_HEADS = 8
FOX_HEADS = 8
CHUNK_SHIFT = 6
SWA_BLOCK = 128
WINDOW_CHUNKS = 2
D_FF = 4096
PLE_DIM = 256
RMS_EPS = 1e-6
N_MAIN = 2304
N_FPAD = 128
N_GATE = 2048
N_ALL = N_MAIN + N_FPAD + N_GATE
D_IN = N_MAIN + FOX_HEADS + N_GATE
SCALE = HEAD_DIM ** -0.5
NEG = -1e30

ADAM_LR = 0.001
ADAM_B1 = 0.9
ADAM_B2 = 0.999
ADAM_EPS = 1e-08
ADAM_WD = 0.01
ADAM_STEP = 10

N_DEV = 8
LANES = 128
V7X_VMEM_BYTES = 64 * 1024 * 1024
VMEM_LIMIT = V7X_VMEM_BYTES * 3 // 4
FOX_BWD_VMEM = V7X_VMEM_BYTES * 7 // 8
MESH = pl.DeviceIdType.MESH
AXES = ("x", "y", "c")

_NT = (((1,), (1,)), ((), ()))
_TN = (((0,), (0,)), ((), ()))


def _params(n_grid, vmem_limit=VMEM_LIMIT):
    return pltpu.CompilerParams(dimension_semantics=("arbitrary",) * n_grid, vmem_limit_bytes=vmem_limit)


def _chunks(n, step):
    return [(s, min(step, n - s)) for s in range(0, n, step)]


def _sigmoid(x):
    return 1.0 / (1.0 + jnp.exp(-x))


def _dot(a, b):
    return jnp.dot(a, b, preferred_element_type=F32)


def _dot_nt(a, b):
    return lax.dot_general(a, b, _NT, preferred_element_type=F32)


def _dot_tn(a, b):
    return lax.dot_general(a, b, _TN, preferred_element_type=F32)


def _lane_concat(stacked_ref):
    return jnp.concatenate([stacked_ref[d] for d in range(N_DEV)], axis=1)


def _rms(h):
    return lax.rsqrt(jnp.mean(h * h, axis=-1, keepdims=True) + RMS_EPS)


def _rms_bwd(h, g, du):
    rs = _rms(h)
    n = h * rs
    dn = du * g
    dh = rs * (dn - n * jnp.mean(dn * n, axis=-1, keepdims=True))
    return dh, jnp.sum(du * n, axis=0, keepdims=True)


def _acc_rows(ref, i, row):
    @pl.when(i == 0)
    def _():
        ref[...] = jnp.zeros_like(ref)
    ref[...] += jnp.broadcast_to(row, ref.shape)


def _row_call(body, name, n_rows, tm, row_ins, const_ins, row_outs, acc_outs, ride=None, tile_outs=()):
    row_outs = list(row_outs)
    n_ri, n_ci, n_ro, n_ao = len(row_ins), len(const_ins), len(row_outs) + len(tile_outs), len(acc_outs)
    extra = ride if ride else _NO_RIDE
    n_ride = len(extra.arrays)
    grid = (n_rows // tm,)

    def kern(*refs):
        i = pl.program_id(0)
        ins, refs = refs[:n_ri + n_ci], refs[n_ri + n_ci:]
        ride_in, refs = refs[:n_ride], refs[n_ride:]
        outs, refs = refs[:n_ro + n_ao], refs[n_ro + n_ao:]
        ride_out, sems = refs[:n_ride], refs[n_ride:]
        if ride:
            ride.at_first_step(grid, ride_in, ride_out, sems)
        body(i, ins[:n_ri], ins[n_ri:], outs[:n_ro], outs[n_ro:])
        if ride:
            ride.at_last_step(grid, ride_in, ride_out, sems)

    def whole(a):
        zeros = (0,) * a.ndim
        return pl.BlockSpec(a.shape, lambda i: zeros, pipeline_mode=pl.Buffered(1))

    in_specs = [pl.BlockSpec((tm, a.shape[1]), lambda i: (i, 0)) for a in row_ins]
    in_specs += [whole(a) for a in const_ins] + extra.in_specs
    out_specs = [pl.BlockSpec((tm, c), lambda i: (i, 0)) for c, _ in row_outs]
    out_specs += [pl.BlockSpec((8, c), lambda i: (i, 0)) for c in tile_outs]
    out_specs += [pl.BlockSpec((8, c), lambda i: (0, 0)) for c in acc_outs] + extra.out_specs
    out_shape = [jax.ShapeDtypeStruct((n_rows, c), dt) for c, dt in row_outs]
    out_shape += [jax.ShapeDtypeStruct((8 * grid[0], c), F32) for c in tile_outs]
    out_shape += [jax.ShapeDtypeStruct((8, c), F32) for c in acc_outs] + extra.out_shape
    return pl.pallas_call(kern, grid=grid, in_specs=in_specs, out_specs=out_specs, out_shape=out_shape,
                          scratch_shapes=extra.scratch, name=name,
                          compiler_params=_params(1))(*row_ins, *const_ins, *extra.arrays)


def _in_proj(x, g_mix, w_all, tm):
    def body(i, ins, consts, outs, accs):
        x_ref, = ins
        g_ref, w_ref = consts
        u_ref, zm_ref, zfg_ref, zf_ref, nrm_ref = outs
        xv = x_ref[...]
        u = ((xv * _rms(xv)) * g_ref[...]).astype(BF16)
        u_ref[...] = u
        for s, n in _chunks(N_MAIN, 768):
            zm_ref[:, s:s + n] = _dot(u, w_ref[:, s:s + n]).astype(BF16)
        for s, n in _chunks(N_FPAD + N_GATE, 512):
            zfg_ref[:, s:s + n] = _dot(u, w_ref[:, N_MAIN + s:N_MAIN + s + n])
        zf_ref[...] = zfg_ref[:, :N_FPAD]
        lane = lax.broadcasted_iota(jnp.int32, (4 * LANES, LANES), 0)
        head = lax.broadcasted_iota(jnp.int32, (4 * LANES, LANES), 1)
        pick = (lane // HEAD_DIM == head).astype(F32)
        rows = []
        for col in (Q_COL, K_COL):
            t = zm_ref[:, col * LANES:(col + 4) * LANES].astype(F32)
            sq = jnp.dot(t * t, pick, precision=lax.Precision.HIGHEST, preferred_element_type=F32)
            rows.append(jnp.max(sq, axis=0, keepdims=True))
        nrm_ref[...] = jnp.concatenate(rows + [jnp.zeros((6, LANES), F32)], axis=0)

    *outs, nrm = _row_call(body, "in_proj", x.shape[0], tm, [x], [g_mix, w_all],
                           [(D_MODEL, BF16), (N_MAIN, BF16), (N_FPAD + N_GATE, F32), (N_FPAD, F32)], [],
                           tile_outs=[LANES])
    return (*outs, nrm)


def _mix_fwd(attn_a, attn_b, zfg, x, w_sa, w_fo, w_mo, g_mlp, tm):
    def body(i, ins, consts, outs, accs):
        aa_ref, ab_ref, zfg_ref, x_ref = ins
        wsa_ref, wfo_ref, wmo_ref, g_ref = consts
        ya_ref, yb_ref, mx_ref, h1_ref, u2_ref = outs
        ya = _dot(aa_ref[...], _lane_concat(wsa_ref))
        yb = _dot(ab_ref[...], _lane_concat(wfo_ref))
        g0 = _sigmoid(zfg_ref[:, N_FPAD:N_FPAD + D_MODEL])
        g1 = _sigmoid(zfg_ref[:, N_FPAD + D_MODEL:N_FPAD + 2 * D_MODEL])
        mixed = (g0 * ya + g1 * yb).astype(BF16)
        ya_ref[...] = ya.astype(BF16)
        yb_ref[...] = yb.astype(BF16)
        mx_ref[...] = mixed
        h1 = x_ref[...] + _dot(mixed, wmo_ref[...])
        h1_ref[...] = h1
        u2_ref[...] = ((h1 * _rms(h1)) * g_ref[...]).astype(BF16)

    return _row_call(body, "mix_fwd", x.shape[0], tm, [attn_a, attn_b, zfg, x], [w_sa, w_fo, w_mo, g_mlp],
                     [(D_MODEL, BF16), (D_MODEL, BF16), (D_MODEL, BF16), (D_MODEL, F32), (D_MODEL, BF16)], [])


def _ffn_fwd(u2, h1, w1s, w2s, tm):
    ch = D_FF // N_DEV

    def body(i, ins, consts, outs, accs):
        u_ref, h1_ref = ins
        w1_ref, w2_ref = consts
        a_ref, r_ref, h2_ref = outs
        u = u_ref[...]
        acc = h1_ref[...]
        for c in range(N_DEV):
            a = _dot(u, w1_ref[c])
            a_ref[:, c * ch:(c + 1) * ch] = a.astype(BF16)
            r = jnp.square(jnp.maximum(a, 0.0)).astype(BF16)
            r_ref[:, c * ch:(c + 1) * ch] = r
            acc = acc + _dot(r, w2_ref[c])
        h2_ref[...] = acc

    return _row_call(body, "ffn_fwd", u2.shape[0], tm, [u2, h1], [w1s, w2s],
                     [(D_FF, BF16), (D_FF, BF16), (D_MODEL, F32)], [])


def _head_fwd_bwd(h2, p, tgt, g_ple, w_pg, w_pp, g_fin, tm):
    def body(i, ins, consts, outs, accs):
        h2_ref, p_ref, t_ref = ins
        gp_ref, wpg_ref, wpp_ref, gf_ref = consts
        dh3_ref, dlg_ref, dpp_ref, u3_ref = outs
        loss_ref, dgf_ref = accs
        h2 = h2_ref[...]
        u3 = ((h2 * _rms(h2)) * gp_ref[...]).astype(BF16)
        u3_ref[...] = u3
        pg = _sigmoid(_dot(u3, wpg_ref[...]))
        pp = _dot(p_ref[...].astype(BF16), _lane_concat(wpp_ref))
        h3 = h2 + pg * pp
        rs3 = _rms(h3)
        n3 = h3 * rs3
        gf = gf_ref[...]
        err = n3 * gf - t_ref[...]
        row_loss = 0.5 * jnp.mean(err * err, axis=-1, keepdims=True)
        _acc_rows(loss_ref, i, jnp.broadcast_to(jnp.sum(row_loss, axis=0, keepdims=True), (1, LANES)))
        dy = err * (1.0 / D_MODEL)
        _acc_rows(dgf_ref, i, jnp.sum(dy * n3, axis=0, keepdims=True))
        dn = dy * gf
        dh3 = rs3 * (dn - n3 * jnp.mean(dn * n3, axis=-1, keepdims=True))
        dh3_ref[...] = dh3
        dpp_ref[...] = (dh3 * pg).astype(BF16)
        dlg_ref[...] = ((dh3 * pp) * pg * (1.0 - pg)).astype(BF16)

    return _row_call(body, "head_fwd_bwd", h2.shape[0], tm, [h2, p, tgt], [g_ple, w_pg, w_pp, g_fin],
                     [(D_MODEL, F32), (D_MODEL, BF16), (D_MODEL, BF16), (D_MODEL, BF16)], [LANES, D_MODEL])


def _ffn_bwd_a(dlg, dh3, h2, a, w_pg, g_ple, w2s, tm):
    ch = D_FF // N_DEV

    def body(i, ins, consts, outs, accs):
        dlg_ref, dh3_ref, h2_ref, a_ref = ins
        wpg_ref, gp_ref, w2_ref = consts
        dh2_ref, dh2b_ref, da_ref = outs
        dgp_ref, = accs
        du3 = _dot_nt(dlg_ref[...], wpg_ref[...])
        dh, dg = _rms_bwd(h2_ref[...], gp_ref[...], du3)
        _acc_rows(dgp_ref, i, dg)
        dh2 = dh3_ref[...] + dh
        dh2_ref[...] = dh2
        dh2b = dh2.astype(BF16)
        dh2b_ref[...] = dh2b
        for c in range(N_DEV):
            dr = _dot_nt(dh2b, w2_ref[c])
            av = a_ref[:, c * ch:(c + 1) * ch].astype(F32)
            da_ref[:, c * ch:(c + 1) * ch] = (dr * (2.0 * jnp.maximum(av, 0.0))).astype(BF16)

    return _row_call(body, "ffn_bwd_a", h2.shape[0], tm, [dlg, dh3, h2, a], [w_pg, g_ple, w2s],
                     [(D_MODEL, F32), (D_MODEL, BF16), (D_FF, BF16)], [D_MODEL])


def _ffn_bwd_b(da, dh2, h1, ya, yb, zfg, w1s, g_mlp, w_mo, w_sa, w_fo, tm):
    ch = D_FF // N_DEV

    def body(i, ins, consts, outs, accs):
        da_ref, dh2_ref, h1_ref, ya_ref, yb_ref, zfg_ref = ins
        w1_ref, gm_ref, wmo_ref, wsa_ref, wfo_ref = consts
        dh1_ref, dh1b_ref, dgl_ref, dya_ref, dyb_ref, daa_ref, dab_ref = outs
        dgm_ref, = accs
        du2 = _dot_nt(da_ref[:, 0:ch], w1_ref[0])
        for c in range(1, N_DEV):
            du2 = du2 + _dot_nt(da_ref[:, c * ch:(c + 1) * ch], w1_ref[c])
        dh, dg = _rms_bwd(h1_ref[...], gm_ref[...], du2)
        _acc_rows(dgm_ref, i, dg)
        dh1 = dh2_ref[...] + dh
        dh1_ref[...] = dh1
        dh1b = dh1.astype(BF16)
        dh1b_ref[...] = dh1b
        dmx = _dot_nt(dh1b, wmo_ref[...])
        g0 = _sigmoid(zfg_ref[:, N_FPAD:N_FPAD + D_MODEL])
        g1 = _sigmoid(zfg_ref[:, N_FPAD + D_MODEL:N_FPAD + 2 * D_MODEL])
        dya = (dmx * g0).astype(BF16)
        dyb = (dmx * g1).astype(BF16)
        dya_ref[...] = dya
        dyb_ref[...] = dyb
        dgl_ref[:, 0:D_MODEL] = ((dmx * ya_ref[...].astype(F32)) * g0 * (1.0 - g0)).astype(BF16)
        dgl_ref[:, D_MODEL:2 * D_MODEL] = ((dmx * yb_ref[...].astype(F32)) * g1 * (1.0 - g1)).astype(BF16)
        daa_ref[...] = _dot_nt(dya, _lane_concat(wsa_ref)).astype(BF16)
        dab_ref[...] = _dot_nt(dyb, _lane_concat(wfo_ref)).astype(BF16)

    half = D_MODEL // 2
    return _row_call(body, "ffn_bwd_b", h1.shape[0], tm, [da, dh2, h1, ya, yb, zfg],
                     [w1s, g_mlp, w_mo, w_sa, w_fo],
                     [(D_MODEL, F32), (D_MODEL, BF16), (N_GATE, BF16), (D_MODEL, BF16), (D_MODEL, BF16),
                      (half, BF16), (half, BF16)], [D_MODEL])


def _in_proj_bwd(dz, dh1, x, w_all, g_mix, tm, ride=None):
    def body(i, ins, consts, outs, accs):
        dz_ref, dh1_ref, x_ref = ins
        w_ref, g_ref = consts
        dx_ref, = outs
        dgx_ref, = accs
        du1 = _dot_nt(dz_ref[...], w_ref[...])
        dh, dg = _rms_bwd(x_ref[...], g_ref[...], du1)
        _acc_rows(dgx_ref, i, dg)
        dx_ref[...] = dh1_ref[...] + dh

    return _row_call(body, "in_proj_bwd", x.shape[0], tm, [dz, dh1, x], [w_all, g_mix],
                     [(D_MODEL, F32)], [D_MODEL], ride)


def _matmul_tn(a, b, name, ts, stack_cols=0):
    n_rows, ka = a.shape
    n = b.shape[1]
    tk = min(ka, 1024)
    tn = 896 if n % 1024 else 1024
    n_stack = tn // stack_cols if stack_cols else 0
    assert ka % tk == 0 and n % tn == 0 and n_rows % ts == 0 and (not stack_cols or tk == ka)
    n_steps = n_rows // ts

    def kern(a_ref, b_ref, o_ref, acc_ref):
        s = pl.program_id(2)

        @pl.when(s == 0)
        def _():
            acc_ref[...] = jnp.zeros_like(acc_ref)
        acc_ref[...] += _dot_tn(a_ref[...].astype(BF16), b_ref[...])

        @pl.when(s == n_steps - 1)
        def _():
            if stack_cols:
                for c in range(n_stack):
                    o_ref[c] = acc_ref[:, c * stack_cols:(c + 1) * stack_cols].astype(BF16)
            else:
                o_ref[...] = acc_ref[...].astype(BF16)

    if stack_cols:
        out_spec = pl.BlockSpec((n_stack, tk, stack_cols), lambda i, j, s: (j, 0, 0))
        out_shape = jax.ShapeDtypeStruct((n // stack_cols, ka, stack_cols), BF16)
    else:
        out_spec = pl.BlockSpec((tk, tn), lambda i, j, s: (i, j))
        out_shape = jax.ShapeDtypeStruct((ka, n), BF16)
    return pl.pallas_call(
        kern, grid=(ka // tk, n // tn, n_steps),
        in_specs=[pl.BlockSpec((ts, tk), lambda i, j, s: (s, i)), pl.BlockSpec((ts, tn), lambda i, j, s: (s, j))],
        out_specs=out_spec, out_shape=out_shape, scratch_shapes=[pltpu.VMEM((tk, tn), F32)], name=name,
        compiler_params=_params(3))(a, b)


SCAN_CHUNK = 512


def _decay_cumsum(f_t, b_col):
    n_tok = f_t.shape[1]
    ch = min(SCAN_CHUNK, n_tok)

    def kern(f_ref, b_ref, c_ref):
        r = lax.broadcasted_iota(jnp.int32, (ch, ch), 0)
        c = lax.broadcasted_iota(jnp.int32, (ch, ch), 1)
        tri = (r <= c).astype(F32)
        carry = jnp.zeros((8, 1), F32)
        for k in range(n_tok // ch):
            xv = f_ref[:, k * ch:(k + 1) * ch] + b_ref[...]
            lf = jnp.minimum(xv, 0.0) - jnp.log(1.0 + jnp.exp(-jnp.abs(xv)))
            cs = jnp.dot(lf, tri, precision=lax.Precision.HIGHEST, preferred_element_type=F32) + carry
            c_ref[:, k * ch:(k + 1) * ch] = cs
            carry = cs[:, ch - 1:ch]

    return pl.pallas_call(kern, out_shape=jax.ShapeDtypeStruct((8, n_tok), F32), name="decay_cumsum",
                          compiler_params=_params(0))(f_t, b_col)


def _decay_bwd(cs, rs, f_t, b_col):
    n_tok = f_t.shape[1]
    ch = min(SCAN_CHUNK, n_tok)
    n_ch = n_tok // ch

    def kern(cs_ref, rs_ref, f_ref, b_ref, df_ref, db_ref, carry_ref):
        k = pl.program_id(0)

        @pl.when(k == 0)
        def _():
            carry_ref[...] = jnp.zeros_like(carry_ref)
            db_ref[...] = jnp.zeros_like(db_ref)

        r = lax.broadcasted_iota(jnp.int32, (ch, ch), 0)
        c = lax.broadcasted_iota(jnp.int32, (ch, ch), 1)
        tri = (r >= c).astype(F32)
        head = lax.broadcasted_iota(jnp.int32, (8, 4 * LANES), 0)
        lane = lax.broadcasted_iota(jnp.int32, (8, 4 * LANES), 1)
        pick = (lane == HEAD_DIM * head).astype(F32)
        dc = lax.dot_general(pick, rs_ref[...] - cs_ref[...], _NT, precision=lax.Precision.HIGHEST,
                             preferred_element_type=F32)
        rc = jnp.dot(dc, tri, precision=lax.Precision.HIGHEST, preferred_element_type=F32) + carry_ref[:, 0:1]
        carry_ref[...] = jnp.broadcast_to(rc[:, 0:1], carry_ref.shape)
        df = rc / (1.0 + jnp.exp(f_ref[...] + b_ref[...]))
        df_ref[...] = df
        db_ref[...] += jnp.broadcast_to(jnp.sum(df, axis=1, keepdims=True), db_ref.shape)

    back = lambda k: n_ch - 1 - k
    wide = pl.BlockSpec((ch, 4 * LANES), lambda k: (back(k), 0))
    row = pl.BlockSpec((8, ch), lambda k: (0, back(k)))
    return pl.pallas_call(
        kern, grid=(n_ch,),
        in_specs=[wide, wide, row, pl.BlockSpec((8, 1), lambda k: (0, 0))],
        out_specs=[row, pl.BlockSpec((8, LANES), lambda k: (0, 0))],
        out_shape=[jax.ShapeDtypeStruct((8, n_tok), F32), jax.ShapeDtypeStruct((8, LANES), F32)],
        scratch_shapes=[pltpu.VMEM((8, LANES), F32)], name="decay_bwd", compiler_params=_params(1))(cs, rs, f_t, b_col)


def _swa_band_mask(n):
    row = lax.broadcasted_iota(jnp.int32, (SWA_BLOCK, 2 * SWA_BLOCK), 0) + SWA_BLOCK
    col = lax.broadcasted_iota(jnp.int32, (SWA_BLOCK, 2 * SWA_BLOCK), 1)
    cd = (row >> CHUNK_SHIFT) - (col >> CHUNK_SHIFT)
    first_real = jnp.where(n > 0, 0, SWA_BLOCK)
    ok = (cd >= 0) & (cd <= WINDOW_CHUNKS) & (col >= first_real)
    dist = jnp.abs(row - col).astype(F32)
    return ok, dist


def _swap_halves(t):
    return pltpu.roll(t.astype(F32), HEAD_DIM, axis=1).astype(t.dtype)


def _swa_specs():
    blk = SWA_BLOCK
    q = pl.BlockSpec((blk, 4 * LANES), lambda n: (n, 0))
    kp = pl.BlockSpec((blk, LANES), lambda n: (jnp.maximum(n - 1, 0), 4))
    kc = pl.BlockSpec((blk, LANES), lambda n: (n, 4))
    vp = pl.BlockSpec((blk, LANES), lambda n: (jnp.maximum(n - 1, 0), 5))
    vc = pl.BlockSpec((blk, LANES), lambda n: (n, 5))
    return q, kp, kc, vp, vc


SWA_GROUPS = ([h for h in range(SWA_HEADS) if h % 2 == h // 4], [h for h in range(SWA_HEADS) if h % 2 != h // 4])


def _stack_heads(ref, heads, lo, mask_halves):
    tiles = []
    for h in heads:
        t = ref[:, (h // 2) * LANES:(h // 2 + 1) * LANES]
        tiles.append(jnp.where(lo if h % 2 == 0 else ~lo, t, jnp.zeros_like(t)) if mask_halves else t)
    return jnp.concatenate(tiles, axis=0)


def _per_head_column(values, heads):
    return jnp.concatenate([jnp.full((SWA_BLOCK, 1), values(h), F32) for h in heads], axis=0)


def _swa_scores(q_ref, kx, heads, lo, ok, dist):
    qa = _stack_heads(q_ref, heads, lo, True) * SCALE
    s = _dot_nt(qa, kx) - _per_head_column(lambda h: 2.0 ** -(h + 1), heads) * dist
    return qa, jnp.where(ok, s, NEG)


def _swa_fwd(zm, sinks):
    n_tok = zm.shape[0]
    blk = SWA_BLOCK

    def kern(q_ref, kp_ref, kc_ref, vp_ref, vc_ref, sink_ref, o_ref, lse_ref):
        n = pl.program_id(0)
        ok, dist = _swa_band_mask(n)
        k2 = jnp.concatenate([kp_ref[...], kc_ref[...]], axis=0)
        v2 = jnp.concatenate([vp_ref[...], vc_ref[...]], axis=0)
        ksw, vsw = _swap_halves(k2), _swap_halves(v2)
        lane = lax.broadcasted_iota(jnp.int32, (blk, LANES), 1)
        lo = lane < HEAD_DIM
        lse_t = jnp.zeros((blk, LANES), F32)
        for pair in range(SWA_HEADS // 2):
            q2 = q_ref[:, pair * LANES:(pair + 1) * LANES]
            outs = []
            for a in range(2):
                h = 2 * pair + a
                qa = jnp.where(lo if a == 0 else ~lo, q2, jnp.zeros_like(q2)) * SCALE
                kx, vx = (k2, v2) if h in SWA_GROUPS[0] else (ksw, vsw)
                s = _dot_nt(qa, kx)
                s = jnp.where(ok, s - (2.0 ** -(h + 1)) * dist, NEG)
                sink = sink_ref[h]
                m = jnp.maximum(jnp.max(s, axis=-1, keepdims=True), sink)
                e = jnp.exp(s - m)
                l = jnp.sum(e, axis=-1, keepdims=True) + jnp.exp(sink - m)
                pn = (e * (1.0 / l)).astype(BF16)
                outs.append(_dot(pn, vx))
                lse_t = jnp.where(lane == h, m + jnp.log(l), lse_t)
            o_ref[:, pair * LANES:(pair + 1) * LANES] = jnp.where(lo, outs[0], outs[1]).astype(BF16)
        lse_ref[...] = lse_t

    q, kp, kc, vp, vc = _swa_specs()
    return pl.pallas_call(
        kern, grid=(n_tok // blk,),
        in_specs=[q, kp, kc, vp, vc, pl.BlockSpec(memory_space=pltpu.SMEM)],
        out_specs=[pl.BlockSpec((blk, 4 * LANES), lambda n: (n, 0)), pl.BlockSpec((blk, LANES), lambda n: (n, 0))],
        out_shape=[jax.ShapeDtypeStruct((n_tok, 4 * LANES), BF16), jax.ShapeDtypeStruct((n_tok, LANES), F32)],
        name="swa_fwd", compiler_params=_params(1))(zm, zm, zm, zm, zm, sinks)


def _swa_bwd(zm, sinks, d_out, out, lse):
    n_tok = zm.shape[0]
    blk = SWA_BLOCK

    def kern(q_ref, kp_ref, kc_ref, vp_ref, vc_ref, do_ref, o_ref, lse_ref, sink_ref,
             dq_ref, dkp_ref, dkc_ref, dvp_ref, dvc_ref, dsk_ref):
        n = pl.program_id(0)

        @pl.when(n == 0)
        def _():
            dsk_ref[...] = jnp.zeros_like(dsk_ref)

        ok, dist = _swa_band_mask(n)
        ok, dist = jnp.concatenate([ok] * 4, axis=0), jnp.concatenate([dist] * 4, axis=0)
        k2 = jnp.concatenate([kp_ref[...], kc_ref[...]], axis=0)
        v2 = jnp.concatenate([vp_ref[...], vc_ref[...]], axis=0)
        lane = lax.broadcasted_iota(jnp.int32, (blk, LANES), 1)
        lo = lane < HEAD_DIM
        lse_t = lse_ref[...]
        dqs, dkv = {}, []
        for heads, kx, vx in ((SWA_GROUPS[0], k2, v2), (SWA_GROUPS[1], _swap_halves(k2), _swap_halves(v2))):
            qa, s = _swa_scores(q_ref, kx, heads, lo, ok, dist)
            doa = _stack_heads(do_ref, heads, lo, True)
            lse_g = jnp.concatenate([lse_t[:, h:h + 1] for h in heads], axis=0)
            prob = jnp.exp(s - lse_g)
            dd = jnp.sum(doa.astype(F32) * _stack_heads(o_ref, heads, lo, False).astype(F32), axis=-1, keepdims=True)
            ds = (prob * (_dot_nt(doa, vx) - dd)).astype(BF16)
            sink_part = -jnp.exp(_per_head_column(lambda h: sink_ref[h], heads) - lse_g) * dd
            dq = _dot(ds, kx) * SCALE
            for r, h in enumerate(heads):
                dqs[h] = dq[r * blk:(r + 1) * blk]
                dsk_ref[h:h + 1, :] += jnp.broadcast_to(
                    jnp.sum(sink_part[r * blk:(r + 1) * blk], axis=0, keepdims=True), (1, LANES))
            dkv.append((_dot_tn(ds, qa), _dot_tn(prob.astype(BF16), doa)))
        for pair in range(SWA_HEADS // 2):
            dq_ref[:, pair * LANES:(pair + 1) * LANES] = jnp.where(lo, dqs[2 * pair], dqs[2 * pair + 1]).astype(BF16)
        dk = dkv[0][0] + pltpu.roll(dkv[1][0], HEAD_DIM, axis=1)
        dv = dkv[0][1] + pltpu.roll(dkv[1][1], HEAD_DIM, axis=1)
        dkp_ref[...] = dk[0:blk]
        dkc_ref[...] = dk[blk:2 * blk]
        dvp_ref[...] = dv[0:blk]
        dvc_ref[...] = dv[blk:2 * blk]

    q, kp, kc, vp, vc = _swa_specs()
    wide = pl.BlockSpec((blk, 4 * LANES), lambda n: (n, 0))
    narrow = pl.BlockSpec((blk, LANES), lambda n: (n, 0))
    part = jax.ShapeDtypeStruct((n_tok, LANES), F32)
    return pl.pallas_call(
        kern, grid=(n_tok // blk,),
        in_specs=[q, kp, kc, vp, vc, wide, wide, narrow, pl.BlockSpec(memory_space=pltpu.SMEM)],
        out_specs=[wide, narrow, narrow, narrow, narrow, pl.BlockSpec((8, LANES), lambda n: (0, 0))],
        out_shape=[jax.ShapeDtypeStruct((n_tok, 4 * LANES), BF16), part, part, part, part,
                   jax.ShapeDtypeStruct((8, LANES), F32)],
        name="swa_bwd", compiler_params=_params(1))(zm, zm, zm, zm, zm, d_out, out, lse, sinks)


def _my_pos():
    return lax.axis_index("x"), lax.axis_index("y"), lax.axis_index("c")


def _peer(k):
    x, y, c = _my_pos()
    px, py, pc = x ^ (k >> 2), y ^ ((k >> 1) & 1), c ^ (k & 1)
    return (px, py, pc), 4 * px + 2 * py + pc


def _gather_copies(x_refs, out_refs, send_sems, recv_sems, local_sems):
    x, y, c = _my_pos()
    my_id = 4 * x + 2 * y + c
    local = [pltpu.make_async_copy(x_refs[w], out_refs[w].at[my_id], local_sems.at[w]) for w in range(len(x_refs))]
    sends, arrivals = [], []
    for k in range(1, N_DEV):
        peer, peer_id = _peer(k)
        for w in range(len(x_refs)):
            sems = dict(send_sem=send_sems.at[7 * w + k - 1], recv_sem=recv_sems.at[7 * w + k - 1],
                        device_id=peer, device_id_type=MESH)
            sends.append(pltpu.make_async_remote_copy(src_ref=x_refs[w], dst_ref=out_refs[w].at[my_id], **sems))
            arrivals.append(pltpu.make_async_remote_copy(src_ref=x_refs[w], dst_ref=out_refs[w].at[peer_id], **sems))
    return local, sends, arrivals


def _scatter_copies(g_refs, part_refs, send_sems, recv_sems, local_sems):
    x, y, c = _my_pos()
    my_id = 4 * x + 2 * y + c
    local = [pltpu.make_async_copy(g_refs[w].at[my_id], part_refs[w].at[0], local_sems.at[w])
             for w in range(len(g_refs))]
    sends, arrivals = [], []
    for k in range(1, N_DEV):
        peer, peer_id = _peer(k)
        for w in range(len(g_refs)):
            sems = dict(send_sem=send_sems.at[7 * w + k - 1], recv_sem=recv_sems.at[7 * w + k - 1],
                        device_id=peer, device_id_type=MESH)
            sends.append(pltpu.make_async_remote_copy(src_ref=g_refs[w].at[peer_id], dst_ref=part_refs[w].at[k], **sems))
            arrivals.append(pltpu.make_async_remote_copy(src_ref=g_refs[w].at[my_id], dst_ref=part_refs[w].at[k], **sems))
    return local, sends, arrivals


def _start_copies(local, sends, arrivals):
    for cp in local + sends:
        cp.start()


def _finish_copies(local, sends, arrivals):
    for cp in arrivals:
        cp.wait_recv()
    for cp in sends:
        cp.wait_send()
    for cp in local:
        cp.wait()


def _exchange_scratch(n_arrays):
    return [pltpu.SemaphoreType.DMA((7 * n_arrays,)), pltpu.SemaphoreType.DMA((7 * n_arrays,)),
            pltpu.SemaphoreType.DMA((n_arrays,))]


class _Ride:
    def __init__(self, arrays, out_shape, copies):
        self.arrays, self.out_shape, self.copies = list(arrays), list(out_shape), copies
        any_spec = pl.BlockSpec(memory_space=pl.ANY)
        self.in_specs = [any_spec] * len(self.arrays)
        self.out_specs = [any_spec] * len(self.arrays)
        self.scratch = _exchange_scratch(len(self.arrays)) if self.arrays else []

    def specs(self):
        return self

    @staticmethod
    def _at(grid, last):
        hit = [pl.program_id(d) == (n - 1 if last else 0) for d, n in enumerate(grid)]
        return hit[0] if len(hit) == 1 else jnp.logical_and(*hit)

    def at_first_step(self, grid, in_refs, out_refs, sems):
        @pl.when(self._at(grid, False))
        def _():
            _start_copies(*self.copies(in_refs, out_refs, *sems))

    def at_last_step(self, grid, in_refs, out_refs, sems):
        @pl.when(self._at(grid, True))
        def _():
            _finish_copies(*self.copies(in_refs, out_refs, *sems))


_NO_RIDE = _Ride([], [], None)


def _gather_ride(shards):
    return _Ride(shards, [jax.ShapeDtypeStruct((N_DEV,) + s.shape, s.dtype) for s in shards], _gather_copies)


def _scatter_ride(grads):
    return _Ride(grads, [jax.ShapeDtypeStruct(g.shape, g.dtype) for g in grads], _scatter_copies)


Q_COL, K_COL, V_COL = 6, 10, 14


def _causal(t, tq, tk):
    row = lax.broadcasted_iota(jnp.int32, (tq, tk), 0)
    col = lax.broadcasted_iota(jnp.int32, (tq, tk), 1)
    return jnp.where(col <= row, t, NEG)


def _lane_tile(stat, width):
    return jnp.tile(stat, (1, width // LANES))


def _fox_steps(nq):
    steps = [(i2, j, 0 if j < 2 * i2 else 1 + j - 2 * i2) for i2 in range(nq // 2) for j in range(2 * i2 + 2)]
    return [np.asarray(col, np.int32) for col in zip(*steps)]


_SWEEPS = {0: [(0, False), (1, False)], 1: [(0, True), (1, False)], 2: [(1, True)]}


def _fox_dispatch(sweep, kind, dead_ref, head0, idx):
    dead0, dead1 = dead_ref[head0, idx] > 0.5, dead_ref[head0 + 1, idx] > 0.5
    live0, live1 = jnp.logical_not(dead0), jnp.logical_not(dead1)
    below = kind == 0
    pl.when(jnp.logical_and(below, jnp.logical_and(live0, live1)))(lambda: sweep(_SWEEPS[0], (0, 1)))
    pl.when(jnp.logical_and(below, jnp.logical_and(live0, dead1)))(lambda: sweep(_SWEEPS[0], (0,)))
    pl.when(jnp.logical_and(below, jnp.logical_and(dead0, live1)))(lambda: sweep(_SWEEPS[0], (1,)))
    pl.when(kind == 1)(lambda: sweep(_SWEEPS[1], (0, 1)))
    pl.when(kind == 2)(lambda: sweep(_SWEEPS[2], (0, 1)))


EXP_ZERO = 110.0
NORM_SLACK = 1.001


def _fox_dead_steps(nrm, c_pairs, tq):
    nq = nrm.shape[0] // 8
    norms = jnp.sqrt(nrm.reshape(nq, 8, LANES)[:, :2, :FOX_HEADS])
    qn, kn = norms[:, 0] * SCALE, norms[:, 1]
    cb = c_pairs.reshape(FOX_HEADS, nq, tq)
    c_max, c_min = jnp.max(cb, axis=-1).T, jnp.min(cb, axis=-1).T
    both = lambda t: jnp.max(t.reshape(nq // 2, 2, FOX_HEADS), axis=1)
    qn2, kn2, c_max2 = both(qn), both(kn), both(c_max)
    gap = qn2[:, None] * (kn[None] + kn2[:, None]) * NORM_SLACK + (c_max2[:, None] - c_min[None])
    below = jnp.arange(nq)[None, :] < 2 * jnp.arange(nq // 2)[:, None]
    dead = jnp.logical_and(gap < -EXP_ZERO, below[..., None])
    return dead.transpose(2, 0, 1).reshape(FOX_HEADS, -1).astype(F32)


def _fox_fwd(zm, c_pairs, dead, tq, ride=None):
    n_tok = zm.shape[0]
    nq = n_tok // tq
    ii, jj, kk = _fox_steps(nq)
    n_steps = len(ii)
    n_ride = len(ride.arrays) if ride else 0

    def kern(ii_ref, jj_ref, kk_ref, q_ref, k_ref, v_ref, ck_ref, dead_ref, *more):
        ride_in, (o_ref, ln_ref), ride_out = more[:n_ride], more[n_ride:n_ride + 2], more[n_ride + 2:2 * n_ride + 2]
        qs_ref, m_ref, l_ref, acc_ref = more[2 * n_ride + 2:2 * n_ride + 6]
        step = pl.program_id(1)
        j, kind = jj_ref[step], kk_ref[step]
        lo = lax.broadcasted_iota(jnp.int32, (2 * tq, LANES), 1) < HEAD_DIM
        if ride:
            ride.at_first_step((FOX_HEADS // 2, n_steps), ride_in, ride_out, more[2 * n_ride + 6:])

        @pl.when(j == 0)
        def _():
            q2 = q_ref[...]
            zq = jnp.zeros_like(q2)
            qs_ref[0] = jnp.where(lo, q2, zq) * SCALE
            qs_ref[1] = jnp.where(lo, zq, q2) * SCALE
            m_ref[...] = jnp.full(m_ref.shape, NEG, F32)
            l_ref[...] = jnp.zeros(l_ref.shape, F32)
            acc_ref[...] = jnp.zeros(acc_ref.shape, F32)

        def sweep(subs, heads):
            kv = k_ref[...]
            v_ones = jnp.concatenate([v_ref[...], jnp.ones((tq, LANES), BF16)], axis=1)
            for sub, diag in subs:
                rows = slice(sub * tq, (sub + 1) * tq)
                for a in heads:
                    t = _dot_nt(qs_ref[a, rows], kv) - ck_ref[a:a + 1, :]
                    if diag:
                        t = _causal(t, tq, tq)
                    m_old = m_ref[a, rows]
                    m_new = jnp.maximum(m_old, jnp.max(t, axis=-1, keepdims=True))
                    alpha = jnp.exp(m_old - m_new)
                    e = jnp.exp(t - _lane_tile(m_new, tq)).astype(BF16)
                    pv = _dot(e, v_ones)
                    acc_ref[a, rows] = alpha * acc_ref[a, rows] + pv[:, :LANES]
                    l_ref[a, rows] = alpha * l_ref[a, rows] + pv[:, LANES:]
                    m_ref[a, rows] = m_new

        _fox_dispatch(sweep, kind, dead_ref, 2 * pl.program_id(0), ii_ref[step] * nq + j)

        @pl.when(kind == 2)
        def _():
            o_ref[...] = jnp.where(lo, acc_ref[0] / l_ref[0], acc_ref[1] / l_ref[1]).astype(BF16)
            ln_ref[:, :LANES] = m_ref[0] + jnp.log(l_ref[0])
            ln_ref[:, LANES:] = m_ref[1] + jnp.log(l_ref[1])

        if ride:
            ride.at_last_step((FOX_HEADS // 2, n_steps), ride_in, ride_out, more[2 * n_ride + 6:])

    blk = (tq, LANES)
    by_i = lambda col: (lambda hp, s, ii, jj, kk: (ii[s], col + hp))
    by_j = lambda col: (lambda hp, s, ii, jj, kk: (jj[s], col + hp))
    extra = ride.specs() if ride else _NO_RIDE
    grid_spec = pltpu.PrefetchScalarGridSpec(
        num_scalar_prefetch=3, grid=(FOX_HEADS // 2, n_steps),
        in_specs=[pl.BlockSpec((2 * tq, LANES), by_i(Q_COL)), pl.BlockSpec(blk, by_j(K_COL)),
                  pl.BlockSpec(blk, by_j(V_COL)),
                  pl.BlockSpec((None, 2, tq), lambda hp, s, ii, jj, kk: (hp, 0, jj[s])),
                  pl.BlockSpec(memory_space=pltpu.SMEM)] + extra.in_specs,
        out_specs=[pl.BlockSpec((2 * tq, LANES), by_i(0)), pl.BlockSpec((2 * tq, 2 * LANES), by_i(0))] + extra.out_specs,
        scratch_shapes=[pltpu.VMEM((2, 2 * tq, LANES), BF16), pltpu.VMEM((2, 2 * tq, LANES), F32),
                        pltpu.VMEM((2, 2 * tq, LANES), F32), pltpu.VMEM((2, 2 * tq, LANES), F32)] + extra.scratch)
    return pl.pallas_call(
        kern, grid_spec=grid_spec,
        out_shape=[jax.ShapeDtypeStruct((n_tok, 4 * LANES), BF16),
                   jax.ShapeDtypeStruct((n_tok, FOX_HEADS * LANES), F32)] + extra.out_shape,
        name="fox_fwd", compiler_params=_params(2))(ii, jj, kk, zm, zm, zm, c_pairs, dead, *extra.arrays)


def _fox_delta(d_out, out, tm):
    def body(i, ins, consts, outs, accs):
        do_ref, o_ref = ins
        dl_ref, = outs
        lane = lax.broadcasted_iota(jnp.int32, (tm, LANES), 1)
        lo = lane < HEAD_DIM
        for pair in range(FOX_HEADS // 2):
            cols = slice(pair * LANES, (pair + 1) * LANES)
            prod = do_ref[:, cols].astype(F32) * o_ref[:, cols].astype(F32)
            for a in range(2):
                dd = jnp.sum(jnp.where(lo if a == 0 else ~lo, prod, 0.0), axis=-1, keepdims=True)
                h = 2 * pair + a
                dl_ref[:, h * LANES:(h + 1) * LANES] = jnp.broadcast_to(dd, (tm, LANES))

    return _row_call(body, "fox_delta", d_out.shape[0], tm, [d_out, out], [], [(FOX_HEADS * LANES, F32)], [])[0]


def _fox_bwd(zm, c_pairs, dead, d_out, lnorm, delta, tq, ride=None):
    n_tok = zm.shape[0]
    nq = n_tok // tq
    ii, jj, kk = _fox_steps(nq)
    n_steps = len(ii)
    n_ride = len(ride.arrays) if ride else 0

    def kern(ii_ref, jj_ref, kk_ref, q_ref, k_ref, v_ref, ck_ref, dead_ref, do_ref, ln_ref, dl_ref, *more):
        ride_in, ride_out = more[:n_ride], more[n_ride + 5:2 * n_ride + 5]
        dq_ref, dk_ref, dv_ref, cs_ref, rs_ref = more[n_ride:n_ride + 5]
        qs_ref, qo_ref, dos_ref, dq_acc = more[2 * n_ride + 5:2 * n_ride + 9]
        step = pl.program_id(1)
        j, kind = jj_ref[step], kk_ref[step]
        lo = lax.broadcasted_iota(jnp.int32, (2 * tq, LANES), 1) < HEAD_DIM
        if ride:
            ride.at_first_step((FOX_HEADS // 2, n_steps), ride_in, ride_out, more[2 * n_ride + 9:])

        @pl.when(step == 0)
        def _():
            dk_ref[...] = jnp.zeros_like(dk_ref)
            dv_ref[...] = jnp.zeros_like(dv_ref)
            cs_ref[...] = jnp.zeros_like(cs_ref)

        @pl.when(j == 0)
        def _():
            q2, do2 = q_ref[...], do_ref[...]
            zq = jnp.zeros_like(q2)
            ones = jnp.ones((2 * tq, LANES), BF16)
            for a in range(2):
                half = lo if a == 0 else ~lo
                qa = jnp.where(half, q2, zq) * SCALE
                qs_ref[a] = qa
                qo_ref[a] = jnp.concatenate([qa, ones], axis=1)
                dos_ref[a] = jnp.where(half, do2, zq)
            dq_acc[...] = jnp.zeros(dq_acc.shape, F32)

        def sweep(subs, heads):
            kv, vv = k_ref[...], v_ref[...]
            k_ones = jnp.concatenate([kv, jnp.ones((tq, LANES), BF16)], axis=1)
            dk, dv, sums = None, None, {}
            for sub, diag in subs:
                rows = slice(sub * tq, (sub + 1) * tq)
                for a in heads:
                    t = _dot_nt(qs_ref[a, rows], kv) - ck_ref[a:a + 1, :]
                    if diag:
                        t = _causal(t, tq, tq)
                    prob = jnp.exp(t - _lane_tile(ln_ref[rows, a * LANES:(a + 1) * LANES], tq))
                    dp = _dot_nt(dos_ref[a, rows], vv)
                    ds = (prob * (dp - _lane_tile(dl_ref[rows, a * LANES:(a + 1) * LANES], tq))).astype(BF16)
                    dq_acc[a, rows] += _dot(ds, k_ones)
                    dk_cs = _dot_tn(ds, qo_ref[a, rows])
                    dv_a = _dot_tn(prob.astype(BF16), dos_ref[a, rows])
                    dk = dk_cs[:, :LANES] if dk is None else dk + dk_cs[:, :LANES]
                    dv = dv_a if dv is None else dv + dv_a
                    sums[a] = dk_cs[:, LANES:] if a not in sums else sums[a] + dk_cs[:, LANES:]
            keys = pl.ds(pl.multiple_of(j * tq, tq), tq)
            dk_ref[keys, :] += dk
            cs_ref[keys, :] += jnp.where(lo[:tq], sums.get(0, 0.0), sums.get(1, 0.0))
            dv_ref[keys, :] += dv

        _fox_dispatch(sweep, kind, dead_ref, 2 * pl.program_id(0), ii_ref[step] * nq + j)

        @pl.when(kind == 2)
        def _():
            dq_ref[...] = jnp.where(lo, dq_acc[0, :, :LANES], dq_acc[1, :, :LANES]) * SCALE
            rs_ref[...] = jnp.where(lo, dq_acc[0, :, LANES:], dq_acc[1, :, LANES:])

        if ride:
            ride.at_last_step((FOX_HEADS // 2, n_steps), ride_in, ride_out, more[2 * n_ride + 9:])

    blk = (tq, LANES)
    by_i = lambda col: (lambda hp, s, ii, jj, kk: (ii[s], col + hp))
    by_j = lambda col: (lambda hp, s, ii, jj, kk: (jj[s], col + hp))
    resident = pl.BlockSpec((2 * tq, LANES), by_i(0))
    stat = pl.BlockSpec((2 * tq, 2 * LANES), by_i(0))
    whole = pl.BlockSpec((n_tok, LANES), lambda hp, s, ii, jj, kk: (0, hp))
    extra = ride.specs() if ride else _NO_RIDE
    grid_spec = pltpu.PrefetchScalarGridSpec(
        num_scalar_prefetch=3, grid=(FOX_HEADS // 2, n_steps),
        in_specs=[pl.BlockSpec((2 * tq, LANES), by_i(Q_COL)), pl.BlockSpec(blk, by_j(K_COL)),
                  pl.BlockSpec(blk, by_j(V_COL)),
                  pl.BlockSpec((None, 2, tq), lambda hp, s, ii, jj, kk: (hp, 0, jj[s])),
                  pl.BlockSpec(memory_space=pltpu.SMEM), resident, stat, stat] + extra.in_specs,
        out_specs=[resident, whole, whole, whole, resident] + extra.out_specs,
        scratch_shapes=[pltpu.VMEM((2, 2 * tq, LANES), BF16), pltpu.VMEM((2, 2 * tq, 2 * LANES), BF16),
                        pltpu.VMEM((2, 2 * tq, LANES), BF16), pltpu.VMEM((2, 2 * tq, 2 * LANES), F32)] + extra.scratch)
    wide = jax.ShapeDtypeStruct((n_tok, 4 * LANES), F32)
    return pl.pallas_call(
        kern, grid_spec=grid_spec, out_shape=[wide] * 5 + extra.out_shape, name="fox_bwd",
        compiler_params=_params(2, FOX_BWD_VMEM))(ii, jj, kk, zm, zm, zm, c_pairs, dead, d_out, lnorm, delta,
                                                  *extra.arrays)


def _all_gather(shards):
    n_w = len(shards)

    def kern(*refs):
        x_refs, out_refs = refs[:n_w], refs[n_w:2 * n_w]
        send_sems, recv_sems, local_sems = refs[2 * n_w:]
        x, y, c = _my_pos()
        me, sibling = (x, y, c), (x, y, 1 - c)
        chips = [(1 - x, y), (x, 1 - y), (1 - x, 1 - y)]

        def slot(w, px, py, pc):
            return out_refs[w].at[4 * px + 2 * py + pc]

        def copy(w, k, block, to, src=None):
            return pltpu.make_async_remote_copy(
                src_ref=slot(w, *block) if src is None else src, dst_ref=slot(w, *block),
                send_sem=send_sems.at[7 * w + k], recv_sem=recv_sems.at[7 * w + k], device_id=to, device_id_type=MESH)

        local, started = [], []
        for w in range(n_w):
            mine = pltpu.make_async_copy(x_refs[w], slot(w, *me), local_sems.at[w])
            mine.start()
            local.append(mine)
            first = [copy(w, 0, me, sibling, src=x_refs[w])]
            first += [copy(w, 1 + k, me, (*chip, c), src=x_refs[w]) for k, chip in enumerate(chips)]
            for cp in first:
                cp.start()
            started += first
        for k, chip in enumerate(chips):
            for w in range(n_w):
                copy(w, 1 + k, (*chip, c), me).wait_recv()
                passed = copy(w, 4 + k, (*chip, c), sibling)
                passed.start()
                started.append(passed)
        for w in range(n_w):
            copy(w, 0, sibling, me).wait_recv()
            for k, chip in enumerate(chips):
                copy(w, 4 + k, (*chip, 1 - c), me).wait_recv()
        for cp in started:
            cp.wait_send()
        for cp in local:
            cp.wait()

    any_spec = pl.BlockSpec(memory_space=pl.ANY)
    return pl.pallas_call(
        kern, out_shape=[jax.ShapeDtypeStruct((N_DEV,) + s.shape, s.dtype) for s in shards],
        in_specs=[any_spec] * n_w, out_specs=[any_spec] * n_w,
        scratch_shapes=[pltpu.SemaphoreType.DMA((7 * n_w,)), pltpu.SemaphoreType.DMA((7 * n_w,)),
                        pltpu.SemaphoreType.DMA((n_w,))],
        name="weight_all_gather")(*shards)


def _small_exchange(small):
    def kern(s_ref, sall_ref, *sems):
        copies = _gather_copies([s_ref], [sall_ref], *sems)
        _start_copies(*copies)
        _finish_copies(*copies)

    any_spec = pl.BlockSpec(memory_space=pl.ANY)
    return pl.pallas_call(
        kern, out_shape=jax.ShapeDtypeStruct((N_DEV,) + small.shape, small.dtype), in_specs=[any_spec],
        out_specs=any_spec, scratch_shapes=_exchange_scratch(1), name="small_grad_exchange")(small)


ADAMW_BLOCK_BYTES = 2 * 1024 * 1024


def _adamw(parts, w, m, v, name):
    n_parts, n_rows, n_cols = parts.shape
    limit = max(8, ADAMW_BLOCK_BYTES // (n_parts * n_cols * parts.dtype.itemsize))
    tr = max(t for t in range(8, n_rows + 1, 8) if n_rows % t == 0 and t <= limit)

    def kern(p_ref, w_ref, m_ref, v_ref, g_out, d_out, m_out, v_out):
        g = p_ref[0].astype(F32)
        for k in range(1, n_parts):
            g = g + p_ref[k].astype(F32)
        m_new = ADAM_B1 * m_ref[...] + (1.0 - ADAM_B1) * g
        v_new = ADAM_B2 * v_ref[...] + (1.0 - ADAM_B2) * jnp.square(g)
        m_hat = m_new / (1.0 - ADAM_B1 ** ADAM_STEP)
        v_hat = v_new / (1.0 - ADAM_B2 ** ADAM_STEP)
        g_out[...] = g
        d_out[...] = -ADAM_LR * (m_hat / (jnp.sqrt(v_hat) + ADAM_EPS) + ADAM_WD * w_ref[...])
        m_out[...] = m_new
        v_out[...] = v_new

    row = pl.BlockSpec((tr, n_cols), lambda i: (i, 0))
    out = jax.ShapeDtypeStruct((n_rows, n_cols), F32)
    return pl.pallas_call(
        kern, grid=(n_rows // tr,),
        in_specs=[pl.BlockSpec((n_parts, tr, n_cols), lambda i: (0, i, 0)), row, row, row],
        out_specs=[row, row, row, row], out_shape=[out, out, out, out], name=name,
        compiler_params=_params(1))(parts, w, m, v)


SHARDED = {
    "w_in": ((D_MODEL, D_IN), 1), "w_br_swa": ((512, D_MODEL), 1), "w_br_fox": ((512, D_MODEL), 1),
    "w_mix_out": ((D_MODEL, D_MODEL), 0), "w_ff1": ((D_MODEL, D_FF), 1), "w_ff2": ((D_FF, D_MODEL), 0),
    "w_ple_gate": ((D_MODEL, D_MODEL), 0), "w_ple_proj": ((PLE_DIM, D_MODEL), 1),
}
W_IN_SHARD = D_IN // N_DEV
W_IN_PAD = 640
SMALL = ("g_mix", "g_mlp", "g_ple", "g_final", "b_forget", "swa_sinks")
SMALL_COLS = 1024


def _wire_shard(name, a):
    a = a.reshape(a.shape[-2:])
    return jnp.pad(a, ((0, 0), (0, W_IN_PAD - W_IN_SHARD))) if name == "w_in" else a


def _from_wire(name, a):
    return (a[:, :W_IN_SHARD] if name == "w_in" else a)[None]


def _w_all_from_wire(stacked):
    w_in = jnp.concatenate([stacked[d][:, :W_IN_SHARD] for d in range(N_DEV)], axis=1)
    fpad = jnp.zeros((D_MODEL, N_FPAD - FOX_HEADS), stacked.dtype)
    return jnp.concatenate([w_in[:, :N_MAIN + FOX_HEADS], fpad, w_in[:, N_MAIN + FOX_HEADS:]], axis=1)


def _dw_in_to_wire(dw_all):
    dw_in = jnp.concatenate([dw_all[:, :N_MAIN + FOX_HEADS], dw_all[:, N_MAIN + N_FPAD:]], axis=1)
    pad = jnp.zeros((D_MODEL, W_IN_PAD - W_IN_SHARD), dw_all.dtype)
    return jnp.stack([jnp.concatenate([dw_in[:, d * W_IN_SHARD:(d + 1) * W_IN_SHARD], pad], axis=1)
                      for d in range(N_DEV)])


def _pack_small(vals):
    rows = [jnp.pad(vals[n].reshape(-1), (0, SMALL_COLS - vals[n].size)) for n in SMALL]
    rows += [jnp.zeros((SMALL_COLS,), F32)] * (8 - len(SMALL))
    return jnp.stack(rows)


def _unpack_small(slab, like):
    return {n: slab[r, :like[n].size].reshape(like[n].shape) for r, n in enumerate(SMALL)}


def _local_step(x, p, tgt, w, small, tm, tq, ts, late_shards=None):
    n_tok = x.shape[0]
    row = lambda v: v.reshape(1, -1)
    g_mix, g_mlp, g_ple, g_fin = row(small["g_mix"]), row(small["g_mlp"]), row(small["g_ple"]), row(small["g_final"])
    sinks = small["swa_sinks"].reshape(-1)
    b_col = small["b_forget"].reshape(FOX_HEADS, 1)

    assert tm == tq
    u1, zm, zfg, zf, nrm = _in_proj(x, g_mix, w["w_all"], tm)
    f_t = zf[:, :FOX_HEADS].T
    c_pairs = _decay_cumsum(f_t, b_col).reshape(FOX_HEADS // 2, 2, n_tok)
    attn_a, lse_a = _swa_fwd(zm, sinks)
    dead = _fox_dead_steps(nrm, c_pairs, tq)
    if late_shards is None:
        attn_b, ln_b = _fox_fwd(zm, c_pairs, dead, tq)
    else:
        attn_b, ln_b, *late = _fox_fwd(zm, c_pairs, dead, tq, _gather_ride(list(late_shards.values())))
        w = {**w, **_gathered_to_local(dict(zip(late_shards, late)))}
    ya, yb, mixed, h1, u2 = _mix_fwd(attn_a, attn_b, zfg, x, w["w_br_swa"], w["w_br_fox"], w["w_mix_out"], g_mlp, tm)
    a, r, h2 = _ffn_fwd(u2, h1, w["w_ff1"], w["w_ff2"], tm // 2)
    dh3, dlg, dpp, u3, loss_acc, dgf = _head_fwd_bwd(h2, p, tgt, g_ple, w["w_ple_gate"], w["w_ple_proj"], g_fin, tm)

    dh2, dh2b, da, dgp = _ffn_bwd_a(dlg, dh3, h2, a, w["w_ple_gate"], g_ple, w["w_ff2"], tm // 2)
    dh1, dh1b, dgl, dya, dyb, daa, dab, dgm = _ffn_bwd_b(
        da, dh2, h1, ya, yb, zfg, w["w_ff1"], g_mlp, w["w_mix_out"], w["w_br_swa"], w["w_br_fox"], tm // 2)
    dq_a, dkp, dkc, dvp, dvc, dsk = _swa_bwd(zm, sinks, daa, attn_a, lse_a)
    delta_b = _fox_delta(dab, attn_b, tm)
    dw = {
        "w_br_swa": _matmul_tn(attn_a, dya, "dw_br_swa", ts, stack_cols=D_MODEL // N_DEV),
        "w_br_fox": _matmul_tn(attn_b, dyb, "dw_br_fox", ts, stack_cols=D_MODEL // N_DEV),
        "w_mix_out": _matmul_tn(mixed, dh1b, "dw_mix_out", ts),
        "w_ff1": _matmul_tn(u2, da, "dw_ff1", ts, stack_cols=D_FF // N_DEV),
        "w_ff2": _matmul_tn(r, dh2b, "dw_ff2", ts),
        "w_ple_gate": _matmul_tn(u3, dlg, "dw_ple_gate", ts),
        "w_ple_proj": _matmul_tn(p, dpp, "dw_ple_proj", ts, stack_cols=D_MODEL // N_DEV),
    }
    if late_shards is None:
        dq_b, dk_b, dv_b, cs, rs = _fox_bwd(zm, c_pairs, dead, dab, ln_b, delta_b, tq)
        late_parts = None
    else:
        wire = _local_to_wire(dw)
        dq_b, dk_b, dv_b, cs, rs, *parts = _fox_bwd(zm, c_pairs, dead, dab, ln_b, delta_b, tq,
                                                    _scatter_ride([wire[n] for n in late_shards]))
        late_parts = dict(zip(late_shards, parts))

    up = lambda t: jnp.concatenate([t[SWA_BLOCK:], jnp.zeros((SWA_BLOCK, LANES), F32)], axis=0)
    dk_a, dv_a = dkc + up(dkp), dvc + up(dvp)
    df_t, db = _decay_bwd(cs, rs, f_t, b_col)
    df = jnp.pad(df_t.T, ((0, 0), (0, N_FPAD - FOX_HEADS)))
    dz = jnp.concatenate([dq_a, dk_a.astype(BF16), dv_a.astype(BF16), dq_b.astype(BF16), dk_b.astype(BF16), dv_b.astype(BF16),
                          df.astype(BF16), dgl], axis=1)
    dw["w_all"] = _matmul_tn(u1, dz, "dw_in", ts)
    if late_shards is None:
        dx, dgx = _in_proj_bwd(dz, dh1, x, w["w_all"], g_mix, tm)
    else:
        dx, dgx, late_parts["w_in"] = _in_proj_bwd(dz, dh1, x, w["w_all"], g_mix, tm,
                                                   _scatter_ride([_dw_in_to_wire(dw["w_all"])]))
    dsmall = {"g_mix": dgx[0], "g_mlp": dgm[0], "g_ple": dgp[0], "g_final": dgf[0],
              "b_forget": db[:, 0], "swa_sinks": dsk[:, 0]}
    return loss_acc[0, 0], dx, dw, dsmall, late_parts


_ROWS = lambda t: t.reshape(-1, t.shape[-1])
_BY_ROWS = lambda t: t.reshape(N_DEV, t.shape[0] // N_DEV, t.shape[1])
_SAME = lambda t: t
LOCAL_LAYOUT = {
    "w_in": ("w_all", _w_all_from_wire, _dw_in_to_wire), "w_br_swa": ("w_br_swa", _SAME, _SAME),
    "w_br_fox": ("w_br_fox", _SAME, _SAME), "w_mix_out": ("w_mix_out", _ROWS, _BY_ROWS),
    "w_ff1": ("w_ff1", _SAME, _SAME), "w_ff2": ("w_ff2", _SAME, _BY_ROWS),
    "w_ple_gate": ("w_ple_gate", _ROWS, _BY_ROWS), "w_ple_proj": ("w_ple_proj", _SAME, _SAME),
}


def _gathered_to_local(g):
    return {LOCAL_LAYOUT[n][0]: LOCAL_LAYOUT[n][1](t) for n, t in g.items()}


def _local_to_wire(dw):
    names = {local: n for n, (local, _, _) in LOCAL_LAYOUT.items()}
    return {names[local]: LOCAL_LAYOUT[names[local]][2](t) for local, t in dw.items()}


def kernel(x, p, g_mix, w_in, b_forget, swa_sinks, w_br_swa, w_br_fox, w_mix_out, g_mlp, w_ff1, w_ff2, g_ple, w_ple_gate, w_ple_proj, g_final, loss_target, m_g_mix, m_w_in, m_b_forget, m_swa_sinks, m_w_br_swa, m_w_br_fox, m_w_mix_out, m_g_mlp, m_w_ff1, m_w_ff2, m_g_ple, m_w_ple_gate, m_w_ple_proj, m_g_final, v_g_mix, v_w_in, v_b_forget, v_swa_sinks, v_w_br_swa, v_w_br_fox, v_w_mix_out, v_g_mlp, v_w_ff1, v_w_ff2, v_g_ple, v_w_ple_gate, v_w_ple_proj, v_g_final):
    given = dict(g_mix=g_mix, w_in=w_in, b_forget=b_forget, swa_sinks=swa_sinks, w_br_swa=w_br_swa, w_br_fox=w_br_fox,
                 w_mix_out=w_mix_out, g_mlp=g_mlp, w_ff1=w_ff1, w_ff2=w_ff2, g_ple=g_ple, w_ple_gate=w_ple_gate,
                 w_ple_proj=w_ple_proj, g_final=g_final)
    mom = dict(g_mix=m_g_mix, w_in=m_w_in, b_forget=m_b_forget, swa_sinks=m_swa_sinks, w_br_swa=m_w_br_swa,
               w_br_fox=m_w_br_fox, w_mix_out=m_w_mix_out, g_mlp=m_g_mlp, w_ff1=m_w_ff1, w_ff2=m_w_ff2, g_ple=m_g_ple,
               w_ple_gate=m_w_ple_gate, w_ple_proj=m_w_ple_proj, g_final=m_g_final)
    vel = dict(g_mix=v_g_mix, w_in=v_w_in, b_forget=v_b_forget, swa_sinks=v_swa_sinks, w_br_swa=v_w_br_swa,
               w_br_fox=v_w_br_fox, w_mix_out=v_w_mix_out, g_mlp=v_g_mlp, w_ff1=v_w_ff1, w_ff2=v_w_ff2, g_ple=v_g_ple,
               w_ple_gate=v_w_ple_gate, w_ple_proj=v_w_ple_proj, g_final=v_g_final)
    names = list(given)
    sharded = list(SHARDED)

    w_wire = {n: _wire_shard(n, given[n]) for n in sharded}
    late = [n for n in sharded if n != "w_in"]
    gathered = _all_gather([w_wire["w_in"].astype(BF16)])
    local_w = _gathered_to_local({"w_in": gathered[0]})
    small = {n: given[n].reshape(-1) for n in SMALL}

    n_tok = x.shape[1]
    tile = min(512, n_tok // 4)
    loss_part, dx, dw, dsmall, parts = _local_step(
        x[0], p[0, 0], loss_target[0], local_w, small, tm=tile, tq=tile, ts=min(2048, n_tok // 4),
        late_shards={n: w_wire[n].astype(BF16) for n in late})
    loss = lax.psum(loss_part, AXES)

    small_all = _small_exchange(_pack_small(dsmall))

    res = {}
    for n in sharded:
        part = parts[n]
        flat = part.reshape(N_DEV, -1, part.shape[-1])
        outs = _adamw(flat, w_wire[n], _wire_shard(n, mom[n]), _wire_shard(n, vel[n]), "adamw_" + n)
        res[n] = [_from_wire(n, o) for o in outs]
    outs_s = _adamw(small_all, _pack_small(small), _pack_small({n: mom[n] for n in SMALL}),
                    _pack_small({n: vel[n] for n in SMALL}), "adamw_small")
    small_res = [_unpack_small(o, given) for o in outs_s]

    groups = [[res[n][k] if n in res else small_res[k][n] for n in names] for k in range(4)]
    return (loss, dx[None], *groups[0], *groups[1], *groups[2], *groups[3])
```

```python
import numpy as np
import jax
import jax.numpy as jnp
from jax import lax
from jax.experimental import pallas as pl
from jax.experimental.pallas import tpu as pltpu

F32 = jnp.float32
BF16 = jnp.bfloat16

D_MODEL = 1024
HEAD_DIM = 64
SWA_HEADS = 8
FOX_HEADS = 8
CHUNK_SHIFT = 6
SWA_BLOCK = 128
WINDOW_CHUNKS = 2
D_FF = 4096
PLE_DIM = 256
RMS_EPS = 1e-6
N_MAIN = 2304
N_FPAD = 128
N_GATE = 2048
N_ALL = N_MAIN + N_FPAD + N_GATE
D_IN = N_MAIN + FOX_HEADS + N_GATE
SCALE = HEAD_DIM ** -0.5
NEG = -1e30

ADAM_LR = 0.001
ADAM_B1 = 0.9
ADAM_B2 = 0.999
ADAM_EPS = 1e-08
ADAM_WD = 0.01
ADAM_STEP = 10

N_DEV = 8
LANES = 128
V7X_VMEM_BYTES = 64 * 1024 * 1024
VMEM_LIMIT = V7X_VMEM_BYTES * 3 // 4
FOX_BWD_VMEM = V7X_VMEM_BYTES * 7 // 8
MESH = pl.DeviceIdType.MESH
AXES = ("x", "y", "c")

_NT = (((1,), (1,)), ((), ()))
_TN = (((0,), (0,)), ((), ()))


def _params(n_grid, vmem_limit=VMEM_LIMIT):
    return pltpu.CompilerParams(dimension_semantics=("arbitrary",) * n_grid, vmem_limit_bytes=vmem_limit)


def _chunks(n, step):
    return [(s, min(step, n - s)) for s in range(0, n, step)]


def _sigmoid(x):
    return 1.0 / (1.0 + jnp.exp(-x))


def _dot(a, b):
    return jnp.dot(a, b, preferred_element_type=F32)


def _dot_nt(a, b):
    return lax.dot_general(a, b, _NT, preferred_element_type=F32)


def _dot_tn(a, b):
    return lax.dot_general(a, b, _TN, preferred_element_type=F32)


def _lane_concat(stacked_ref):
    return jnp.concatenate([stacked_ref[d] for d in range(N_DEV)], axis=1)


def _rms(h):
    return lax.rsqrt(jnp.mean(h * h, axis=-1, keepdims=True) + RMS_EPS)


def _rms_bwd(h, g, du):
    rs = _rms(h)
    n = h * rs
    dn = du * g
    dh = rs * (dn - n * jnp.mean(dn * n, axis=-1, keepdims=True))
    return dh, jnp.sum(du * n, axis=0, keepdims=True)


def _acc_rows(ref, i, row):
    @pl.when(i == 0)
    def _():
        ref[...] = jnp.zeros_like(ref)
    ref[...] += jnp.broadcast_to(row, ref.shape)


def _row_call(body, name, n_rows, tm, row_ins, const_ins, row_outs, acc_outs, ride=None, tile_outs=()):
    row_outs = list(row_outs)
    n_ri, n_ci, n_ro, n_ao = len(row_ins), len(const_ins), len(row_outs) + len(tile_outs), len(acc_outs)
    extra = ride if ride else _NO_RIDE
    n_ride = len(extra.arrays)
    grid = (n_rows // tm,)

    def kern(*refs):
        i = pl.program_id(0)
        ins, refs = refs[:n_ri + n_ci], refs[n_ri + n_ci:]
        ride_in, refs = refs[:n_ride], refs[n_ride:]
        outs, refs = refs[:n_ro + n_ao], refs[n_ro + n_ao:]
        ride_out, sems = refs[:n_ride], refs[n_ride:]
        if ride:
            ride.at_first_step(grid, ride_in, ride_out, sems)
        body(i, ins[:n_ri], ins[n_ri:], outs[:n_ro], outs[n_ro:])
        if ride:
            ride.at_last_step(grid, ride_in, ride_out, sems)

    def whole(a):
        zeros = (0,) * a.ndim
        return pl.BlockSpec(a.shape, lambda i: zeros, pipeline_mode=pl.Buffered(1))

    in_specs = [pl.BlockSpec((tm, a.shape[1]), lambda i: (i, 0)) for a in row_ins]
    in_specs += [whole(a) for a in const_ins] + extra.in_specs
    out_specs = [pl.BlockSpec((tm, c), lambda i: (i, 0)) for c, _ in row_outs]
    out_specs += [pl.BlockSpec((8, c), lambda i: (i, 0)) for c in tile_outs]
    out_specs += [pl.BlockSpec((8, c), lambda i: (0, 0)) for c in acc_outs] + extra.out_specs
    out_shape = [jax.ShapeDtypeStruct((n_rows, c), dt) for c, dt in row_outs]
    out_shape += [jax.ShapeDtypeStruct((8 * grid[0], c), F32) for c in tile_outs]
    out_shape += [jax.ShapeDtypeStruct((8, c), F32) for c in acc_outs] + extra.out_shape
    return pl.pallas_call(kern, grid=grid, in_specs=in_specs, out_specs=out_specs, out_shape=out_shape,
                          scratch_shapes=extra.scratch, name=name,
                          compiler_params=_params(1))(*row_ins, *const_ins, *extra.arrays)


def _in_proj(x, g_mix, w_all, tm):
    def body(i, ins, consts, outs, accs):
        x_ref, = ins
        g_ref, w_ref = consts
        u_ref, zm_ref, zfg_ref, zf_ref, nrm_ref = outs
        xv = x_ref[...]
        u = ((xv * _rms(xv)) * g_ref[...]).astype(BF16)
        u_ref[...] = u
        for s, n in _chunks(N_MAIN, 768):
            zm_ref[:, s:s + n] = _dot(u, w_ref[:, s:s + n]).astype(BF16)
        for s, n in _chunks(N_FPAD + N_GATE, 512):
            zfg_ref[:, s:s + n] = _dot(u, w_ref[:, N_MAIN + s:N_MAIN + s + n])
        zf_ref[...] = zfg_ref[:, :N_FPAD]
        lane = lax.broadcasted_iota(jnp.int32, (4 * LANES, LANES), 0)
        head = lax.broadcasted_iota(jnp.int32, (4 * LANES, LANES), 1)
        pick = (lane // HEAD_DIM == head).astype(BF16)
        rows = []
        for col in (Q_COL, K_COL):
            t = zm_ref[:, col * LANES:(col + 4) * LANES].astype(F32)
            rows.append(jnp.max(_dot((t * t).astype(BF16), pick), axis=0, keepdims=True))
        nrm_ref[...] = jnp.concatenate(rows + [jnp.zeros((6, LANES), F32)], axis=0)

    *outs, nrm = _row_call(body, "in_proj", x.shape[0], tm, [x], [g_mix, w_all],
                           [(D_MODEL, BF16), (N_MAIN, BF16), (N_FPAD + N_GATE, F32), (N_FPAD, F32)], [],
                           tile_outs=[LANES])
    return (*outs, nrm)


def _mix_fwd(attn_a, attn_b, zfg, x, w_sa, w_fo, w_mo, g_mlp, tm):
    def body(i, ins, consts, outs, accs):
        aa_ref, ab_ref, zfg_ref, x_ref = ins
        wsa_ref, wfo_ref, wmo_ref, g_ref = consts
        ya_ref, yb_ref, mx_ref, h1_ref, u2_ref = outs
        ya = _dot(aa_ref[...], _lane_concat(wsa_ref))
        yb = _dot(ab_ref[...], _lane_concat(wfo_ref))
        g0 = _sigmoid(zfg_ref[:, N_FPAD:N_FPAD + D_MODEL])
        g1 = _sigmoid(zfg_ref[:, N_FPAD + D_MODEL:N_FPAD + 2 * D_MODEL])
        mixed = (g0 * ya + g1 * yb).astype(BF16)
        ya_ref[...] = ya.astype(BF16)
        yb_ref[...] = yb.astype(BF16)
        mx_ref[...] = mixed
        h1 = x_ref[...] + _dot(mixed, wmo_ref[...])
        h1_ref[...] = h1
        u2_ref[...] = ((h1 * _rms(h1)) * g_ref[...]).astype(BF16)

    return _row_call(body, "mix_fwd", x.shape[0], tm, [attn_a, attn_b, zfg, x], [w_sa, w_fo, w_mo, g_mlp],
                     [(D_MODEL, BF16), (D_MODEL, BF16), (D_MODEL, BF16), (D_MODEL, F32), (D_MODEL, BF16)], [])


def _ffn_fwd(u2, h1, w1s, w2s, tm):
    ch = D_FF // N_DEV

    def body(i, ins, consts, outs, accs):
        u_ref, h1_ref = ins
        w1_ref, w2_ref = consts
        a_ref, r_ref, h2_ref = outs
        u = u_ref[...]
        acc = h1_ref[...]
        for c in range(N_DEV):
            a = _dot(u, w1_ref[c])
            a_ref[:, c * ch:(c + 1) * ch] = a.astype(BF16)
            r = jnp.square(jnp.maximum(a, 0.0)).astype(BF16)
            r_ref[:, c * ch:(c + 1) * ch] = r
            acc = acc + _dot(r, w2_ref[c])
        h2_ref[...] = acc

    return _row_call(body, "ffn_fwd", u2.shape[0], tm, [u2, h1], [w1s, w2s],
                     [(D_FF, BF16), (D_FF, BF16), (D_MODEL, F32)], [])


def _head_fwd_bwd(h2, p, tgt, g_ple, w_pg, w_pp, g_fin, tm):
    def body(i, ins, consts, outs, accs):
        h2_ref, p_ref, t_ref = ins
        gp_ref, wpg_ref, wpp_ref, gf_ref = consts
        dh3_ref, dlg_ref, dpp_ref, u3_ref = outs
        loss_ref, dgf_ref = accs
        h2 = h2_ref[...]
        u3 = ((h2 * _rms(h2)) * gp_ref[...]).astype(BF16)
        u3_ref[...] = u3
        pg = _sigmoid(_dot(u3, wpg_ref[...]))
        pp = _dot(p_ref[...].astype(BF16), _lane_concat(wpp_ref))
        h3 = h2 + pg * pp
        rs3 = _rms(h3)
        n3 = h3 * rs3
        gf = gf_ref[...]
        err = n3 * gf - t_ref[...]
        row_loss = 0.5 * jnp.mean(err * err, axis=-1, keepdims=True)
        _acc_rows(loss_ref, i, jnp.broadcast_to(jnp.sum(row_loss, axis=0, keepdims=True), (1, LANES)))
        dy = err * (1.0 / D_MODEL)
        _acc_rows(dgf_ref, i, jnp.sum(dy * n3, axis=0, keepdims=True))
        dn = dy * gf
        dh3 = rs3 * (dn - n3 * jnp.mean(dn * n3, axis=-1, keepdims=True))
        dh3_ref[...] = dh3
        dpp_ref[...] = (dh3 * pg).astype(BF16)
        dlg_ref[...] = ((dh3 * pp) * pg * (1.0 - pg)).astype(BF16)

    return _row_call(body, "head_fwd_bwd", h2.shape[0], tm, [h2, p, tgt], [g_ple, w_pg, w_pp, g_fin],
                     [(D_MODEL, F32), (D_MODEL, BF16), (D_MODEL, BF16), (D_MODEL, BF16)], [LANES, D_MODEL])


def _ffn_bwd_a(dlg, dh3, h2, a, w_pg, g_ple, w2s, tm):
    ch = D_FF // N_DEV

    def body(i, ins, consts, outs, accs):
        dlg_ref, dh3_ref, h2_ref, a_ref = ins
        wpg_ref, gp_ref, w2_ref = consts
        dh2_ref, dh2b_ref, da_ref = outs
        dgp_ref, = accs
        du3 = _dot_nt(dlg_ref[...], wpg_ref[...])
        dh, dg = _rms_bwd(h2_ref[...], gp_ref[...], du3)
        _acc_rows(dgp_ref, i, dg)
        dh2 = dh3_ref[...] + dh
        dh2_ref[...] = dh2
        dh2b = dh2.astype(BF16)
        dh2b_ref[...] = dh2b
        for c in range(N_DEV):
            dr = _dot_nt(dh2b, w2_ref[c])
            av = a_ref[:, c * ch:(c + 1) * ch].astype(F32)
            da_ref[:, c * ch:(c + 1) * ch] = (dr * (2.0 * jnp.maximum(av, 0.0))).astype(BF16)

    return _row_call(body, "ffn_bwd_a", h2.shape[0], tm, [dlg, dh3, h2, a], [w_pg, g_ple, w2s],
                     [(D_MODEL, F32), (D_MODEL, BF16), (D_FF, BF16)], [D_MODEL])


def _ffn_bwd_b(da, dh2, h1, ya, yb, zfg, attn_b, w1s, g_mlp, w_mo, w_sa, w_fo, tm):
    ch = D_FF // N_DEV

    def body(i, ins, consts, outs, accs):
        da_ref, dh2_ref, h1_ref, ya_ref, yb_ref, zfg_ref, ob_ref = ins
        w1_ref, gm_ref, wmo_ref, wsa_ref, wfo_ref = consts
        dh1_ref, dh1b_ref, dgl_ref, dya_ref, dyb_ref, daa_ref, dab_ref, dl_ref = outs
        dgm_ref, = accs
        du2 = _dot_nt(da_ref[:, 0:ch], w1_ref[0])
        for c in range(1, N_DEV):
            du2 = du2 + _dot_nt(da_ref[:, c * ch:(c + 1) * ch], w1_ref[c])
        dh, dg = _rms_bwd(h1_ref[...], gm_ref[...], du2)
        _acc_rows(dgm_ref, i, dg)
        dh1 = dh2_ref[...] + dh
        dh1_ref[...] = dh1
        dh1b = dh1.astype(BF16)
        dh1b_ref[...] = dh1b
        dmx = _dot_nt(dh1b, wmo_ref[...])
        g0 = _sigmoid(zfg_ref[:, N_FPAD:N_FPAD + D_MODEL])
        g1 = _sigmoid(zfg_ref[:, N_FPAD + D_MODEL:N_FPAD + 2 * D_MODEL])
        dya = (dmx * g0).astype(BF16)
        dyb = (dmx * g1).astype(BF16)
        dya_ref[...] = dya
        dyb_ref[...] = dyb
        dgl_ref[:, 0:D_MODEL] = ((dmx * ya_ref[...].astype(F32)) * g0 * (1.0 - g0)).astype(BF16)
        dgl_ref[:, D_MODEL:2 * D_MODEL] = ((dmx * yb_ref[...].astype(F32)) * g1 * (1.0 - g1)).astype(BF16)
        daa_ref[...] = _dot_nt(dya, _lane_concat(wsa_ref)).astype(BF16)
        dab = _dot_nt(dyb, _lane_concat(wfo_ref)).astype(BF16)
        dab_ref[...] = dab
        half_in = lax.broadcasted_iota(jnp.int32, (LANES, 2 * LANES), 0) // HEAD_DIM
        half_out = lax.broadcasted_iota(jnp.int32, (LANES, 2 * LANES), 1) // LANES
        pick = (half_in == half_out).astype(BF16)
        for pair in range(FOX_HEADS // 2):
            cols = slice(pair * LANES, (pair + 1) * LANES)
            prod = dab[:, cols].astype(F32) * ob_ref[:, cols].astype(F32)
            hi = prod.astype(BF16)
            lo_part = (prod - hi.astype(F32)).astype(BF16)
            dl_ref[:, 2 * pair * LANES:(2 * pair + 2) * LANES] = _dot(hi, pick) + _dot(lo_part, pick)

    half = D_MODEL // 2
    return _row_call(body, "ffn_bwd_b", h1.shape[0], tm, [da, dh2, h1, ya, yb, zfg, attn_b],
                     [w1s, g_mlp, w_mo, w_sa, w_fo],
                     [(D_MODEL, F32), (D_MODEL, BF16), (N_GATE, BF16), (D_MODEL, BF16), (D_MODEL, BF16),
                      (half, BF16), (half, BF16), (FOX_HEADS * LANES, F32)], [D_MODEL])


def _in_proj_bwd(dz, dh1, x, w_all, g_mix, tm, ride=None):
    def body(i, ins, consts, outs, accs):
        dz_ref, dh1_ref, x_ref = ins
        w_ref, g_ref = consts
        dx_ref, = outs
        dgx_ref, = accs
        du1 = _dot_nt(dz_ref[...], w_ref[...])
        dh, dg = _rms_bwd(x_ref[...], g_ref[...], du1)
        _acc_rows(dgx_ref, i, dg)
        dx_ref[...] = dh1_ref[...] + dh

    return _row_call(body, "in_proj_bwd", x.shape[0], tm, [dz, dh1, x], [w_all, g_mix],
                     [(D_MODEL, F32)], [D_MODEL], ride)


def _matmul_tn(a, b, name, ts, stack_cols=0):
    n_rows, ka = a.shape
    n = b.shape[1]
    tk = min(ka, 1024)
    tn = 896 if n % 1024 else 1024
    n_stack = tn // stack_cols if stack_cols else 0
    assert ka % tk == 0 and n % tn == 0 and n_rows % ts == 0 and (not stack_cols or tk == ka)
    n_steps = n_rows // ts

    def kern(a_ref, b_ref, o_ref, acc_ref):
        s = pl.program_id(2)

        @pl.when(s == 0)
        def _():
            acc_ref[...] = jnp.zeros_like(acc_ref)
        acc_ref[...] += _dot_tn(a_ref[...].astype(BF16), b_ref[...])

        @pl.when(s == n_steps - 1)
        def _():
            if stack_cols:
                for c in range(n_stack):
                    o_ref[c] = acc_ref[:, c * stack_cols:(c + 1) * stack_cols].astype(BF16)
            else:
                o_ref[...] = acc_ref[...].astype(BF16)

    if stack_cols:
        out_spec = pl.BlockSpec((n_stack, tk, stack_cols), lambda i, j, s: (j, 0, 0))
        out_shape = jax.ShapeDtypeStruct((n // stack_cols, ka, stack_cols), BF16)
    else:
        out_spec = pl.BlockSpec((tk, tn), lambda i, j, s: (i, j))
        out_shape = jax.ShapeDtypeStruct((ka, n), BF16)
    return pl.pallas_call(
        kern, grid=(ka // tk, n // tn, n_steps),
        in_specs=[pl.BlockSpec((ts, tk), lambda i, j, s: (s, i)), pl.BlockSpec((ts, tn), lambda i, j, s: (s, j))],
        out_specs=out_spec, out_shape=out_shape, scratch_shapes=[pltpu.VMEM((tk, tn), F32)], name=name,
        compiler_params=_params(3))(a, b)


SCAN_CHUNK = 512


def _decay_cumsum(f_t, b_col):
    n_tok = f_t.shape[1]
    ch = min(SCAN_CHUNK, n_tok)

    def kern(f_ref, b_ref, c_ref):
        r = lax.broadcasted_iota(jnp.int32, (ch, ch), 0)
        c = lax.broadcasted_iota(jnp.int32, (ch, ch), 1)
        tri = (r <= c).astype(F32)
        carry = jnp.zeros((8, 1), F32)
        for k in range(n_tok // ch):
            xv = f_ref[:, k * ch:(k + 1) * ch] + b_ref[...]
            lf = jnp.minimum(xv, 0.0) - jnp.log(1.0 + jnp.exp(-jnp.abs(xv)))
            cs = jnp.dot(lf, tri, precision=lax.Precision.HIGHEST, preferred_element_type=F32) + carry
            c_ref[:, k * ch:(k + 1) * ch] = cs
            carry = cs[:, ch - 1:ch]

    return pl.pallas_call(kern, out_shape=jax.ShapeDtypeStruct((8, n_tok), F32), name="decay_cumsum",
                          compiler_params=_params(0))(f_t, b_col)


def _decay_bwd(cs, rs, f_t, b_col):
    n_tok = f_t.shape[1]
    ch = min(SCAN_CHUNK, n_tok)
    n_ch = n_tok // ch

    def kern(cs_ref, rs_ref, f_ref, b_ref, df_ref, db_ref, carry_ref):
        k = pl.program_id(0)

        @pl.when(k == 0)
        def _():
            carry_ref[...] = jnp.zeros_like(carry_ref)
            db_ref[...] = jnp.zeros_like(db_ref)

        r = lax.broadcasted_iota(jnp.int32, (ch, ch), 0)
        c = lax.broadcasted_iota(jnp.int32, (ch, ch), 1)
        tri = (r >= c).astype(F32)
        head = lax.broadcasted_iota(jnp.int32, (8, 4 * LANES), 0)
        lane = lax.broadcasted_iota(jnp.int32, (8, 4 * LANES), 1)
        pick = (lane == HEAD_DIM * head).astype(F32)
        dc = lax.dot_general(pick, rs_ref[...] - cs_ref[...], _NT, precision=lax.Precision.HIGHEST,
                             preferred_element_type=F32)
        rc = jnp.dot(dc, tri, precision=lax.Precision.HIGHEST, preferred_element_type=F32) + carry_ref[:, 0:1]
        carry_ref[...] = jnp.broadcast_to(rc[:, 0:1], carry_ref.shape)
        df = rc / (1.0 + jnp.exp(f_ref[...] + b_ref[...]))
        df_ref[...] = df
        db_ref[...] += jnp.broadcast_to(jnp.sum(df, axis=1, keepdims=True), db_ref.shape)

    back = lambda k: n_ch - 1 - k
    wide = pl.BlockSpec((ch, 4 * LANES), lambda k: (back(k), 0))
    row = pl.BlockSpec((8, ch), lambda k: (0, back(k)))
    return pl.pallas_call(
        kern, grid=(n_ch,),
        in_specs=[wide, wide, row, pl.BlockSpec((8, 1), lambda k: (0, 0))],
        out_specs=[row, pl.BlockSpec((8, LANES), lambda k: (0, 0))],
        out_shape=[jax.ShapeDtypeStruct((8, n_tok), F32), jax.ShapeDtypeStruct((8, LANES), F32)],
        scratch_shapes=[pltpu.VMEM((8, LANES), F32)], name="decay_bwd", compiler_params=_params(1))(cs, rs, f_t, b_col)


def _swa_bias_table():
    row = jnp.arange(SWA_BLOCK)[:, None] + SWA_BLOCK
    col = jnp.arange(2 * SWA_BLOCK)[None, :]
    cd = (row >> CHUNK_SHIFT) - (col >> CHUNK_SHIFT)
    band = (cd >= 0) & (cd <= WINDOW_CHUNKS)
    slopes = jnp.asarray([2.0 ** -(h + 1) for h in range(SWA_HEADS)], F32)
    bias = -slopes[:, None, None] * jnp.abs(row - col).astype(F32)[None]
    return jnp.stack([jnp.where(band & (col >= SWA_BLOCK), bias, NEG), jnp.where(band, bias, NEG)])


SWA_BIAS_SPEC = pl.BlockSpec((None, SWA_HEADS, SWA_BLOCK, 2 * SWA_BLOCK), lambda n: (jnp.minimum(n, 1), 0, 0, 0))


def _swap_halves(t):
    return pltpu.roll(t.astype(F32), HEAD_DIM, axis=1).astype(t.dtype)


def _swa_specs():
    blk = SWA_BLOCK
    q = pl.BlockSpec((blk, 4 * LANES), lambda n: (n, 0))
    kp = pl.BlockSpec((blk, LANES), lambda n: (jnp.maximum(n - 1, 0), 4))
    kc = pl.BlockSpec((blk, LANES), lambda n: (n, 4))
    vp = pl.BlockSpec((blk, LANES), lambda n: (jnp.maximum(n - 1, 0), 5))
    vc = pl.BlockSpec((blk, LANES), lambda n: (n, 5))
    return q, kp, kc, vp, vc


SWA_GROUPS = ([h for h in range(SWA_HEADS) if h % 2 == h // 4], [h for h in range(SWA_HEADS) if h % 2 != h // 4])


def _stack_heads(ref, heads, lo, mask_halves):
    tiles = []
    for h in heads:
        t = ref[:, (h // 2) * LANES:(h // 2 + 1) * LANES]
        tiles.append(jnp.where(lo if h % 2 == 0 else ~lo, t, jnp.zeros_like(t)) if mask_halves else t)
    return jnp.concatenate(tiles, axis=0)


def _per_head_column(values, heads):
    return jnp.concatenate([jnp.full((SWA_BLOCK, 1), values(h), F32) for h in heads], axis=0)


def _swa_scores(q_ref, kx, heads, lo, bias_ref):
    qa = _stack_heads(q_ref, heads, lo, True) * SCALE
    return qa, _dot_nt(qa, kx) + jnp.concatenate([bias_ref[h] for h in heads], axis=0)


def _swa_fwd(zm, sinks):
    n_tok = zm.shape[0]
    blk = SWA_BLOCK

    def kern(q_ref, kp_ref, kc_ref, vp_ref, vc_ref, bias_ref, sink_ref, o_ref, lse_ref):
        k2 = jnp.concatenate([kp_ref[...], kc_ref[...]], axis=0)
        v2 = jnp.concatenate([vp_ref[...], vc_ref[...]], axis=0)
        ksw, vsw = _swap_halves(k2), _swap_halves(v2)
        lane = lax.broadcasted_iota(jnp.int32, (blk, LANES), 1)
        lo = lane < HEAD_DIM
        lse_t = jnp.zeros((blk, LANES), F32)
        for pair in range(SWA_HEADS // 2):
            q2 = q_ref[:, pair * LANES:(pair + 1) * LANES]
            outs = []
            for a in range(2):
                h = 2 * pair + a
                qa = jnp.where(lo if a == 0 else ~lo, q2, jnp.zeros_like(q2)) * SCALE
                kx, vx = (k2, v2) if h in SWA_GROUPS[0] else (ksw, vsw)
                s = _dot_nt(qa, kx) + bias_ref[h]
                sink = sink_ref[h]
                m = jnp.maximum(jnp.max(s, axis=-1, keepdims=True), sink)
                e = jnp.exp(s - m)
                l = jnp.sum(e, axis=-1, keepdims=True) + jnp.exp(sink - m)
                pn = (e * (1.0 / l)).astype(BF16)
                outs.append(_dot(pn, vx))
                lse_t = jnp.where(lane == h, m + jnp.log(l), lse_t)
            o_ref[:, pair * LANES:(pair + 1) * LANES] = jnp.where(lo, outs[0], outs[1]).astype(BF16)
        lse_ref[...] = lse_t

    q, kp, kc, vp, vc = _swa_specs()
    return pl.pallas_call(
        kern, grid=(n_tok // blk,),
        in_specs=[q, kp, kc, vp, vc, SWA_BIAS_SPEC, pl.BlockSpec(memory_space=pltpu.SMEM)],
        out_specs=[pl.BlockSpec((blk, 4 * LANES), lambda n: (n, 0)), pl.BlockSpec((blk, LANES), lambda n: (n, 0))],
        out_shape=[jax.ShapeDtypeStruct((n_tok, 4 * LANES), BF16), jax.ShapeDtypeStruct((n_tok, LANES), F32)],
        name="swa_fwd", compiler_params=_params(1))(zm, zm, zm, zm, zm, _swa_bias_table(), sinks)


def _swa_bwd(zm, sinks, d_out, out, lse):
    n_tok = zm.shape[0]
    blk = SWA_BLOCK

    def kern(q_ref, kp_ref, kc_ref, vp_ref, vc_ref, bias_ref, do_ref, o_ref, lse_ref, sink_ref,
             dq_ref, dkp_ref, dkc_ref, dvp_ref, dvc_ref, dsk_ref):
        n = pl.program_id(0)

        @pl.when(n == 0)
        def _():
            dsk_ref[...] = jnp.zeros_like(dsk_ref)

        k2 = jnp.concatenate([kp_ref[...], kc_ref[...]], axis=0)
        v2 = jnp.concatenate([vp_ref[...], vc_ref[...]], axis=0)
        lane = lax.broadcasted_iota(jnp.int32, (blk, LANES), 1)
        lo = lane < HEAD_DIM
        lse_t = lse_ref[...]
        dqs, dkv = {}, []
        for heads, kx, vx in ((SWA_GROUPS[0], k2, v2), (SWA_GROUPS[1], _swap_halves(k2), _swap_halves(v2))):
            qa, s = _swa_scores(q_ref, kx, heads, lo, bias_ref)
            doa = _stack_heads(do_ref, heads, lo, True)
            lse_g = jnp.concatenate([lse_t[:, h:h + 1] for h in heads], axis=0)
            prob = jnp.exp(s - lse_g)
            dd = jnp.sum(doa.astype(F32) * _stack_heads(o_ref, heads, lo, False).astype(F32), axis=-1, keepdims=True)
            ds = (prob * (_dot_nt(doa, vx) - dd)).astype(BF16)
            sink_part = -jnp.exp(_per_head_column(lambda h: sink_ref[h], heads) - lse_g) * dd
            dq = _dot(ds, kx) * SCALE
            for r, h in enumerate(heads):
                dqs[h] = dq[r * blk:(r + 1) * blk]
                dsk_ref[h:h + 1, :] += jnp.broadcast_to(
                    jnp.sum(sink_part[r * blk:(r + 1) * blk], axis=0, keepdims=True), (1, LANES))
            dkv.append((_dot_tn(ds, qa), _dot_tn(prob.astype(BF16), doa)))
        for pair in range(SWA_HEADS // 2):
            dq_ref[:, pair * LANES:(pair + 1) * LANES] = jnp.where(lo, dqs[2 * pair], dqs[2 * pair + 1]).astype(BF16)
        dk = dkv[0][0] + pltpu.roll(dkv[1][0], HEAD_DIM, axis=1)
        dv = dkv[0][1] + pltpu.roll(dkv[1][1], HEAD_DIM, axis=1)
        dkp_ref[...] = dk[0:blk]
        dkc_ref[...] = dk[blk:2 * blk]
        dvp_ref[...] = dv[0:blk]
        dvc_ref[...] = dv[blk:2 * blk]

    q, kp, kc, vp, vc = _swa_specs()
    wide = pl.BlockSpec((blk, 4 * LANES), lambda n: (n, 0))
    narrow = pl.BlockSpec((blk, LANES), lambda n: (n, 0))
    part = jax.ShapeDtypeStruct((n_tok, LANES), F32)
    return pl.pallas_call(
        kern, grid=(n_tok // blk,),
        in_specs=[q, kp, kc, vp, vc, SWA_BIAS_SPEC, wide, wide, narrow, pl.BlockSpec(memory_space=pltpu.SMEM)],
        out_specs=[wide, narrow, narrow, narrow, narrow, pl.BlockSpec((8, LANES), lambda n: (0, 0))],
        out_shape=[jax.ShapeDtypeStruct((n_tok, 4 * LANES), BF16), part, part, part, part,
                   jax.ShapeDtypeStruct((8, LANES), F32)],
        name="swa_bwd", compiler_params=_params(1))(zm, zm, zm, zm, zm, _swa_bias_table(), d_out, out, lse, sinks)


def _my_pos():
    return lax.axis_index("x"), lax.axis_index("y"), lax.axis_index("c")


def _peer(k):
    x, y, c = _my_pos()
    px, py, pc = x ^ (k >> 2), y ^ ((k >> 1) & 1), c ^ (k & 1)
    return (px, py, pc), 4 * px + 2 * py + pc


def _gather_copies(x_refs, out_refs, send_sems, recv_sems, local_sems):
    x, y, c = _my_pos()
    my_id = 4 * x + 2 * y + c
    local = [pltpu.make_async_copy(x_refs[w], out_refs[w].at[my_id], local_sems.at[w]) for w in range(len(x_refs))]
    sends, arrivals = [], []
    for k in range(1, N_DEV):
        peer, peer_id = _peer(k)
        for w in range(len(x_refs)):
            sems = dict(send_sem=send_sems.at[7 * w + k - 1], recv_sem=recv_sems.at[7 * w + k - 1],
                        device_id=peer, device_id_type=MESH)
            sends.append(pltpu.make_async_remote_copy(src_ref=x_refs[w], dst_ref=out_refs[w].at[my_id], **sems))
            arrivals.append(pltpu.make_async_remote_copy(src_ref=x_refs[w], dst_ref=out_refs[w].at[peer_id], **sems))
    return local, sends, arrivals


def _scatter_copies(g_refs, part_refs, send_sems, recv_sems, local_sems):
    x, y, c = _my_pos()
    my_id = 4 * x + 2 * y + c
    local = [pltpu.make_async_copy(g_refs[w].at[my_id], part_refs[w].at[0], local_sems.at[w])
             for w in range(len(g_refs))]
    sends, arrivals = [], []
    for k in range(1, N_DEV):
        peer, peer_id = _peer(k)
        for w in range(len(g_refs)):
            sems = dict(send_sem=send_sems.at[7 * w + k - 1], recv_sem=recv_sems.at[7 * w + k - 1],
                        device_id=peer, device_id_type=MESH)
            sends.append(pltpu.make_async_remote_copy(src_ref=g_refs[w].at[peer_id], dst_ref=part_refs[w].at[k], **sems))
            arrivals.append(pltpu.make_async_remote_copy(src_ref=g_refs[w].at[my_id], dst_ref=part_refs[w].at[k], **sems))
    return local, sends, arrivals


def _start_copies(local, sends, arrivals):
    for cp in local + sends:
        cp.start()


def _finish_copies(local, sends, arrivals):
    for cp in arrivals:
        cp.wait_recv()
    for cp in sends:
        cp.wait_send()
    for cp in local:
        cp.wait()


def _exchange_scratch(n_arrays):
    return [pltpu.SemaphoreType.DMA((7 * n_arrays,)), pltpu.SemaphoreType.DMA((7 * n_arrays,)),
            pltpu.SemaphoreType.DMA((n_arrays,))]


class _Ride:
    def __init__(self, arrays, out_shape, copies):
        self.arrays, self.out_shape, self.copies = list(arrays), list(out_shape), copies
        any_spec = pl.BlockSpec(memory_space=pl.ANY)
        self.in_specs = [any_spec] * len(self.arrays)
        self.out_specs = [any_spec] * len(self.arrays)
        self.scratch = _exchange_scratch(len(self.arrays)) if self.arrays else []

    def specs(self):
        return self

    @staticmethod
    def _at(grid, last):
        hit = [pl.program_id(d) == (n - 1 if last else 0) for d, n in enumerate(grid)]
        return hit[0] if len(hit) == 1 else jnp.logical_and(*hit)

    def at_first_step(self, grid, in_refs, out_refs, sems):
        @pl.when(self._at(grid, False))
        def _():
            _start_copies(*self.copies(in_refs, out_refs, *sems))

    def at_last_step(self, grid, in_refs, out_refs, sems):
        @pl.when(self._at(grid, True))
        def _():
            _finish_copies(*self.copies(in_refs, out_refs, *sems))


_NO_RIDE = _Ride([], [], None)


def _gather_ride(shards):
    return _Ride(shards, [jax.ShapeDtypeStruct((N_DEV,) + s.shape, s.dtype) for s in shards], _gather_copies)


def _scatter_ride(grads):
    return _Ride(grads, [jax.ShapeDtypeStruct(g.shape, g.dtype) for g in grads], _scatter_copies)


Q_COL, K_COL, V_COL = 6, 10, 14


def _causal(t, tq, tk):
    row = lax.broadcasted_iota(jnp.int32, (tq, tk), 0)
    col = lax.broadcasted_iota(jnp.int32, (tq, tk), 1)
    return jnp.where(col <= row, t, NEG)


def _lane_tile(stat, width):
    return jnp.tile(stat, (1, width // LANES))


def _fox_steps(nq):
    steps = [(i2, j, 0 if j < 2 * i2 else 1 + j - 2 * i2) for i2 in range(nq // 2) for j in range(2 * i2 + 2)]
    return [np.asarray(col, np.int32) for col in zip(*steps)]


_SWEEPS = {0: [(0, False), (1, False)], 1: [(0, True), (1, False)], 2: [(1, True)]}


def _fox_dispatch(sweep, kind, dead_ref, head0, idx):
    dead0, dead1 = dead_ref[head0, idx] > 0.5, dead_ref[head0 + 1, idx] > 0.5
    live0, live1 = jnp.logical_not(dead0), jnp.logical_not(dead1)
    below = kind == 0
    pl.when(jnp.logical_and(below, jnp.logical_and(live0, live1)))(lambda: sweep(_SWEEPS[0], (0, 1)))
    pl.when(jnp.logical_and(below, jnp.logical_and(live0, dead1)))(lambda: sweep(_SWEEPS[0], (0,)))
    pl.when(jnp.logical_and(below, jnp.logical_and(dead0, live1)))(lambda: sweep(_SWEEPS[0], (1,)))
    pl.when(kind == 1)(lambda: sweep(_SWEEPS[1], (0, 1)))
    pl.when(kind == 2)(lambda: sweep(_SWEEPS[2], (0, 1)))


EXP_ZERO = 110.0
NORM_SLACK = 1.005


def _fox_dead_steps(nrm, c_pairs, tq):
    nq = nrm.shape[0] // 8
    norms = jnp.sqrt(nrm.reshape(nq, 8, LANES)[:, :2, :FOX_HEADS])
    qn, kn = norms[:, 0] * SCALE, norms[:, 1]
    cb = c_pairs.reshape(FOX_HEADS, nq, tq)
    c_max, c_min = jnp.max(cb, axis=-1).T, jnp.min(cb, axis=-1).T
    both = lambda t: jnp.max(t.reshape(nq // 2, 2, FOX_HEADS), axis=1)
    qn2, kn2, c_max2 = both(qn), both(kn), both(c_max)
    gap = qn2[:, None] * (kn[None] + kn2[:, None]) * NORM_SLACK + (c_max2[:, None] - c_min[None])
    below = jnp.arange(nq)[None, :] < 2 * jnp.arange(nq // 2)[:, None]
    dead = jnp.logical_and(gap < -EXP_ZERO, below[..., None])
    return dead.transpose(2, 0, 1).reshape(FOX_HEADS, -1).astype(F32)


def _fox_fwd(zm, c_pairs, dead, tq, ride=None):
    n_tok = zm.shape[0]
    nq = n_tok // tq
    ii, jj, kk = _fox_steps(nq)
    n_steps = len(ii)
    n_ride = len(ride.arrays) if ride else 0

    def kern(ii_ref, jj_ref, kk_ref, q_ref, k_ref, v_ref, ck_ref, dead_ref, *more):
        ride_in, (o_ref, ln_ref), ride_out = more[:n_ride], more[n_ride:n_ride + 2], more[n_ride + 2:2 * n_ride + 2]
        qs_ref, m_ref, l_ref, acc_ref = more[2 * n_ride + 2:2 * n_ride + 6]
        step = pl.program_id(1)
        j, kind = jj_ref[step], kk_ref[step]
        lo = lax.broadcasted_iota(jnp.int32, (2 * tq, LANES), 1) < HEAD_DIM
        if ride:
            ride.at_first_step((FOX_HEADS // 2, n_steps), ride_in, ride_out, more[2 * n_ride + 6:])

        @pl.when(j == 0)
        def _():
            q2 = q_ref[...]
            zq = jnp.zeros_like(q2)
            qs_ref[0] = jnp.where(lo, q2, zq) * SCALE
            qs_ref[1] = jnp.where(lo, zq, q2) * SCALE
            m_ref[...] = jnp.full(m_ref.shape, NEG, F32)
            l_ref[...] = jnp.zeros(l_ref.shape, F32)
            acc_ref[...] = jnp.zeros(acc_ref.shape, F32)

        def sweep(subs, heads):
            kv = k_ref[...]
            v_ones = jnp.concatenate([v_ref[...], jnp.ones((tq, LANES), BF16)], axis=1)
            for sub, diag in subs:
                rows = slice(sub * tq, (sub + 1) * tq)
                for a in heads:
                    t = _dot_nt(qs_ref[a, rows], kv) - ck_ref[a:a + 1, :]
                    if diag:
                        t = _causal(t, tq, tq)
                    m_old = m_ref[a, rows]
                    m_new = jnp.maximum(m_old, jnp.max(t, axis=-1, keepdims=True))
                    alpha = jnp.exp(m_old - m_new)
                    e = jnp.exp(t - _lane_tile(m_new, tq)).astype(BF16)
                    pv = _dot(e, v_ones)
                    acc_ref[a, rows] = alpha * acc_ref[a, rows] + pv[:, :LANES]
                    l_ref[a, rows] = alpha * l_ref[a, rows] + pv[:, LANES:]
                    m_ref[a, rows] = m_new

        _fox_dispatch(sweep, kind, dead_ref, 2 * pl.program_id(0), ii_ref[step] * nq + j)

        @pl.when(kind == 2)
        def _():
            o_ref[...] = jnp.where(lo, acc_ref[0] / l_ref[0], acc_ref[1] / l_ref[1]).astype(BF16)
            ln_ref[:, :LANES] = m_ref[0] + jnp.log(l_ref[0])
            ln_ref[:, LANES:] = m_ref[1] + jnp.log(l_ref[1])

        if ride:
            ride.at_last_step((FOX_HEADS // 2, n_steps), ride_in, ride_out, more[2 * n_ride + 6:])

    blk = (tq, LANES)
    by_i = lambda col: (lambda hp, s, ii, jj, kk: (ii[s], col + hp))
    by_j = lambda col: (lambda hp, s, ii, jj, kk: (jj[s], col + hp))
    extra = ride.specs() if ride else _NO_RIDE
    grid_spec = pltpu.PrefetchScalarGridSpec(
        num_scalar_prefetch=3, grid=(FOX_HEADS // 2, n_steps),
        in_specs=[pl.BlockSpec((2 * tq, LANES), by_i(Q_COL)), pl.BlockSpec(blk, by_j(K_COL)),
                  pl.BlockSpec(blk, by_j(V_COL)),
                  pl.BlockSpec((None, 2, tq), lambda hp, s, ii, jj, kk: (hp, 0, jj[s])),
                  pl.BlockSpec(memory_space=pltpu.SMEM)] + extra.in_specs,
        out_specs=[pl.BlockSpec((2 * tq, LANES), by_i(0)), pl.BlockSpec((2 * tq, 2 * LANES), by_i(0))] + extra.out_specs,
        scratch_shapes=[pltpu.VMEM((2, 2 * tq, LANES), BF16), pltpu.VMEM((2, 2 * tq, LANES), F32),
                        pltpu.VMEM((2, 2 * tq, LANES), F32), pltpu.VMEM((2, 2 * tq, LANES), F32)] + extra.scratch)
    return pl.pallas_call(
        kern, grid_spec=grid_spec,
        out_shape=[jax.ShapeDtypeStruct((n_tok, 4 * LANES), BF16),
                   jax.ShapeDtypeStruct((n_tok, FOX_HEADS * LANES), F32)] + extra.out_shape,
        name="fox_fwd", compiler_params=_params(2))(ii, jj, kk, zm, zm, zm, c_pairs, dead, *extra.arrays)


def _fox_bwd(zm, c_pairs, dead, d_out, lnorm, delta, tq, ride=None):
    n_tok = zm.shape[0]
    nq = n_tok // tq
    ii, jj, kk = _fox_steps(nq)
    n_steps = len(ii)
    n_ride = len(ride.arrays) if ride else 0

    def kern(ii_ref, jj_ref, kk_ref, q_ref, k_ref, v_ref, ck_ref, dead_ref, do_ref, ln_ref, dl_ref, *more):
        ride_in, ride_out = more[:n_ride], more[n_ride + 5:2 * n_ride + 5]
        dq_ref, dk_ref, dv_ref, cs_ref, rs_ref = more[n_ride:n_ride + 5]
        qs_ref, qo_ref, dos_ref, dq_acc = more[2 * n_ride + 5:2 * n_ride + 9]
        step = pl.program_id(1)
        j, kind = jj_ref[step], kk_ref[step]
        lo = lax.broadcasted_iota(jnp.int32, (2 * tq, LANES), 1) < HEAD_DIM
        if ride:
            ride.at_first_step((FOX_HEADS // 2, n_steps), ride_in, ride_out, more[2 * n_ride + 9:])

        @pl.when(step == 0)
        def _():
            dk_ref[...] = jnp.zeros_like(dk_ref)
            dv_ref[...] = jnp.zeros_like(dv_ref)
            cs_ref[...] = jnp.zeros_like(cs_ref)

        @pl.when(j == 0)
        def _():
            q2, do2 = q_ref[...], do_ref[...]
            zq = jnp.zeros_like(q2)
            ones = jnp.ones((2 * tq, LANES), BF16)
            for a in range(2):
                half = lo if a == 0 else ~lo
                qa = jnp.where(half, q2, zq) * SCALE
                qs_ref[a] = qa
                qo_ref[a] = jnp.concatenate([qa, ones], axis=1)
                dos_ref[a] = jnp.where(half, do2, zq)
            dq_acc[...] = jnp.zeros(dq_acc.shape, F32)

        def sweep(subs, heads):
            kv, vv = k_ref[...], v_ref[...]
            k_ones = jnp.concatenate([kv, jnp.ones((tq, LANES), BF16)], axis=1)
            dk, dv, sums = None, None, {}
            for sub, diag in subs:
                rows = slice(sub * tq, (sub + 1) * tq)
                for a in heads:
                    t = _dot_nt(qs_ref[a, rows], kv) - ck_ref[a:a + 1, :]
                    if diag:
                        t = _causal(t, tq, tq)
                    prob = jnp.exp(t - _lane_tile(ln_ref[rows, a * LANES:(a + 1) * LANES], tq))
                    dp = _dot_nt(dos_ref[a, rows], vv)
                    ds = (prob * (dp - _lane_tile(dl_ref[rows, a * LANES:(a + 1) * LANES], tq))).astype(BF16)
                    dq_acc[a, rows] += _dot(ds, k_ones)
                    dk_cs = _dot_tn(ds, qo_ref[a, rows])
                    dv_a = _dot_tn(prob.astype(BF16), dos_ref[a, rows])
                    dk = dk_cs[:, :LANES] if dk is None else dk + dk_cs[:, :LANES]
                    dv = dv_a if dv is None else dv + dv_a
                    sums[a] = dk_cs[:, LANES:] if a not in sums else sums[a] + dk_cs[:, LANES:]
            keys = pl.ds(pl.multiple_of(j * tq, tq), tq)
            dk_ref[keys, :] += dk
            cs_ref[keys, :] += jnp.where(lo[:tq], sums.get(0, 0.0), sums.get(1, 0.0))
            dv_ref[keys, :] += dv

        _fox_dispatch(sweep, kind, dead_ref, 2 * pl.program_id(0), ii_ref[step] * nq + j)

        @pl.when(kind == 2)
        def _():
            dq_ref[...] = jnp.where(lo, dq_acc[0, :, :LANES], dq_acc[1, :, :LANES]) * SCALE
            rs_ref[...] = jnp.where(lo, dq_acc[0, :, LANES:], dq_acc[1, :, LANES:])

        if ride:
            ride.at_last_step((FOX_HEADS // 2, n_steps), ride_in, ride_out, more[2 * n_ride + 9:])

    blk = (tq, LANES)
    by_i = lambda col: (lambda hp, s, ii, jj, kk: (ii[s], col + hp))
    by_j = lambda col: (lambda hp, s, ii, jj, kk: (jj[s], col + hp))
    resident = pl.BlockSpec((2 * tq, LANES), by_i(0))
    stat = pl.BlockSpec((2 * tq, 2 * LANES), by_i(0))
    whole = pl.BlockSpec((n_tok, LANES), lambda hp, s, ii, jj, kk: (0, hp))
    extra = ride.specs() if ride else _NO_RIDE
    grid_spec = pltpu.PrefetchScalarGridSpec(
        num_scalar_prefetch=3, grid=(FOX_HEADS // 2, n_steps),
        in_specs=[pl.BlockSpec((2 * tq, LANES), by_i(Q_COL)), pl.BlockSpec(blk, by_j(K_COL)),
                  pl.BlockSpec(blk, by_j(V_COL)),
                  pl.BlockSpec((None, 2, tq), lambda hp, s, ii, jj, kk: (hp, 0, jj[s])),
                  pl.BlockSpec(memory_space=pltpu.SMEM), resident, stat, stat] + extra.in_specs,
        out_specs=[resident, whole, whole, whole, resident] + extra.out_specs,
        scratch_shapes=[pltpu.VMEM((2, 2 * tq, LANES), BF16), pltpu.VMEM((2, 2 * tq, 2 * LANES), BF16),
                        pltpu.VMEM((2, 2 * tq, LANES), BF16), pltpu.VMEM((2, 2 * tq, 2 * LANES), F32)] + extra.scratch)
    wide = jax.ShapeDtypeStruct((n_tok, 4 * LANES), F32)
    return pl.pallas_call(
        kern, grid_spec=grid_spec, out_shape=[wide] * 5 + extra.out_shape, name="fox_bwd",
        compiler_params=_params(2, FOX_BWD_VMEM))(ii, jj, kk, zm, zm, zm, c_pairs, dead, d_out, lnorm, delta,
                                                  *extra.arrays)


def _all_gather(shards):
    n_w = len(shards)

    def kern(*refs):
        x_refs, out_refs = refs[:n_w], refs[n_w:2 * n_w]
        send_sems, recv_sems, local_sems = refs[2 * n_w:]
        x, y, c = _my_pos()
        me, sibling = (x, y, c), (x, y, 1 - c)
        chips = [(1 - x, y), (x, 1 - y), (1 - x, 1 - y)]

        def slot(w, px, py, pc):
            return out_refs[w].at[4 * px + 2 * py + pc]

        def copy(w, k, block, to, src=None):
            return pltpu.make_async_remote_copy(
                src_ref=slot(w, *block) if src is None else src, dst_ref=slot(w, *block),
                send_sem=send_sems.at[7 * w + k], recv_sem=recv_sems.at[7 * w + k], device_id=to, device_id_type=MESH)

        local, started = [], []
        for w in range(n_w):
            mine = pltpu.make_async_copy(x_refs[w], slot(w, *me), local_sems.at[w])
            mine.start()
            local.append(mine)
            first = [copy(w, 0, me, sibling, src=x_refs[w])]
            first += [copy(w, 1 + k, me, (*chip, c), src=x_refs[w]) for k, chip in enumerate(chips)]
            for cp in first:
                cp.start()
            started += first
        for k, chip in enumerate(chips):
            for w in range(n_w):
                copy(w, 1 + k, (*chip, c), me).wait_recv()
                passed = copy(w, 4 + k, (*chip, c), sibling)
                passed.start()
                started.append(passed)
        for w in range(n_w):
            copy(w, 0, sibling, me).wait_recv()
            for k, chip in enumerate(chips):
                copy(w, 4 + k, (*chip, 1 - c), me).wait_recv()
        for cp in started:
            cp.wait_send()
        for cp in local:
            cp.wait()

    any_spec = pl.BlockSpec(memory_space=pl.ANY)
    return pl.pallas_call(
        kern, out_shape=[jax.ShapeDtypeStruct((N_DEV,) + s.shape, s.dtype) for s in shards],
        in_specs=[any_spec] * n_w, out_specs=[any_spec] * n_w,
        scratch_shapes=[pltpu.SemaphoreType.DMA((7 * n_w,)), pltpu.SemaphoreType.DMA((7 * n_w,)),
                        pltpu.SemaphoreType.DMA((n_w,))],
        name="weight_all_gather")(*shards)


def _small_exchange(small):
    def kern(s_ref, sall_ref, *sems):
        copies = _gather_copies([s_ref], [sall_ref], *sems)
        _start_copies(*copies)
        _finish_copies(*copies)

    any_spec = pl.BlockSpec(memory_space=pl.ANY)
    return pl.pallas_call(
        kern, out_shape=jax.ShapeDtypeStruct((N_DEV,) + small.shape, small.dtype), in_specs=[any_spec],
        out_specs=any_spec, scratch_shapes=_exchange_scratch(1), name="small_grad_exchange")(small)


ADAMW_BLOCK_BYTES = 2 * 1024 * 1024


def _adamw(parts, w, m, v, name):
    n_parts, n_rows, n_cols = parts.shape
    limit = max(8, ADAMW_BLOCK_BYTES // (n_parts * n_cols * parts.dtype.itemsize))
    tr = max(t for t in range(8, n_rows + 1, 8) if n_rows % t == 0 and t <= limit)

    def kern(p_ref, w_ref, m_ref, v_ref, g_out, d_out, m_out, v_out):
        g = p_ref[0].astype(F32)
        for k in range(1, n_parts):
            g = g + p_ref[k].astype(F32)
        m_new = ADAM_B1 * m_ref[...] + (1.0 - ADAM_B1) * g
        v_new = ADAM_B2 * v_ref[...] + (1.0 - ADAM_B2) * jnp.square(g)
        m_hat = m_new / (1.0 - ADAM_B1 ** ADAM_STEP)
        v_hat = v_new / (1.0 - ADAM_B2 ** ADAM_STEP)
        g_out[...] = g
        d_out[...] = -ADAM_LR * (m_hat / (jnp.sqrt(v_hat) + ADAM_EPS) + ADAM_WD * w_ref[...])
        m_out[...] = m_new
        v_out[...] = v_new

    row = pl.BlockSpec((tr, n_cols), lambda i: (i, 0))
    out = jax.ShapeDtypeStruct((n_rows, n_cols), F32)
    return pl.pallas_call(
        kern, grid=(n_rows // tr,),
        in_specs=[pl.BlockSpec((n_parts, tr, n_cols), lambda i: (0, i, 0)), row, row, row],
        out_specs=[row, row, row, row], out_shape=[out, out, out, out], name=name,
        compiler_params=_params(1))(parts, w, m, v)


SHARDED = {
    "w_in": ((D_MODEL, D_IN), 1), "w_br_swa": ((512, D_MODEL), 1), "w_br_fox": ((512, D_MODEL), 1),
    "w_mix_out": ((D_MODEL, D_MODEL), 0), "w_ff1": ((D_MODEL, D_FF), 1), "w_ff2": ((D_FF, D_MODEL), 0),
    "w_ple_gate": ((D_MODEL, D_MODEL), 0), "w_ple_proj": ((PLE_DIM, D_MODEL), 1),
}
W_IN_SHARD = D_IN // N_DEV
W_IN_PAD = 640
SMALL = ("g_mix", "g_mlp", "g_ple", "g_final", "b_forget", "swa_sinks")
SMALL_COLS = 1024


def _wire_shard(name, a):
    a = a.reshape(a.shape[-2:])
    return jnp.pad(a, ((0, 0), (0, W_IN_PAD - W_IN_SHARD))) if name == "w_in" else a


def _from_wire(name, a):
    return (a[:, :W_IN_SHARD] if name == "w_in" else a)[None]


def _w_all_from_wire(stacked):
    w_in = jnp.concatenate([stacked[d][:, :W_IN_SHARD] for d in range(N_DEV)], axis=1)
    fpad = jnp.zeros((D_MODEL, N_FPAD - FOX_HEADS), stacked.dtype)
    return jnp.concatenate([w_in[:, :N_MAIN + FOX_HEADS], fpad, w_in[:, N_MAIN + FOX_HEADS:]], axis=1)


def _dw_in_to_wire(dw_all):
    dw_in = jnp.concatenate([dw_all[:, :N_MAIN + FOX_HEADS], dw_all[:, N_MAIN + N_FPAD:]], axis=1)
    pad = jnp.zeros((D_MODEL, W_IN_PAD - W_IN_SHARD), dw_all.dtype)
    return jnp.stack([jnp.concatenate([dw_in[:, d * W_IN_SHARD:(d + 1) * W_IN_SHARD], pad], axis=1)
                      for d in range(N_DEV)])


def _pack_small(vals):
    rows = [jnp.pad(vals[n].reshape(-1), (0, SMALL_COLS - vals[n].size)) for n in SMALL]
    rows += [jnp.zeros((SMALL_COLS,), F32)] * (8 - len(SMALL))
    return jnp.stack(rows)


def _unpack_small(slab, like):
    return {n: slab[r, :like[n].size].reshape(like[n].shape) for r, n in enumerate(SMALL)}


def _local_step(x, p, tgt, w, small, tm, tq, ts, late_shards=None):
    n_tok = x.shape[0]
    row = lambda v: v.reshape(1, -1)
    g_mix, g_mlp, g_ple, g_fin = row(small["g_mix"]), row(small["g_mlp"]), row(small["g_ple"]), row(small["g_final"])
    sinks = small["swa_sinks"].reshape(-1)
    b_col = small["b_forget"].reshape(FOX_HEADS, 1)

    assert tm == tq
    u1, zm, zfg, zf, nrm = _in_proj(x, g_mix, w["w_all"], tm)
    f_t = zf[:, :FOX_HEADS].T
    c_pairs = _decay_cumsum(f_t, b_col).reshape(FOX_HEADS // 2, 2, n_tok)
    attn_a, lse_a = _swa_fwd(zm, sinks)
    dead = _fox_dead_steps(nrm, c_pairs, tq)
    if late_shards is None:
        attn_b, ln_b = _fox_fwd(zm, c_pairs, dead, tq)
    else:
        attn_b, ln_b, *late = _fox_fwd(zm, c_pairs, dead, tq, _gather_ride(list(late_shards.values())))
        w = {**w, **_gathered_to_local(dict(zip(late_shards, late)))}
    ya, yb, mixed, h1, u2 = _mix_fwd(attn_a, attn_b, zfg, x, w["w_br_swa"], w["w_br_fox"], w["w_mix_out"], g_mlp, tm)
    a, r, h2 = _ffn_fwd(u2, h1, w["w_ff1"], w["w_ff2"], tm // 2)
    dh3, dlg, dpp, u3, loss_acc, dgf = _head_fwd_bwd(h2, p, tgt, g_ple, w["w_ple_gate"], w["w_ple_proj"], g_fin, tm)

    dh2, dh2b, da, dgp = _ffn_bwd_a(dlg, dh3, h2, a, w["w_ple_gate"], g_ple, w["w_ff2"], tm // 2)
    dh1, dh1b, dgl, dya, dyb, daa, dab, delta_b, dgm = _ffn_bwd_b(
        da, dh2, h1, ya, yb, zfg, attn_b, w["w_ff1"], g_mlp, w["w_mix_out"], w["w_br_swa"], w["w_br_fox"], tm // 2)
    dq_a, dkp, dkc, dvp, dvc, dsk = _swa_bwd(zm, sinks, daa, attn_a, lse_a)
    dw = {
        "w_br_swa": _matmul_tn(attn_a, dya, "dw_br_swa", ts, stack_cols=D_MODEL // N_DEV),
        "w_br_fox": _matmul_tn(attn_b, dyb, "dw_br_fox", ts, stack_cols=D_MODEL // N_DEV),
        "w_mix_out": _matmul_tn(mixed, dh1b, "dw_mix_out", ts),
        "w_ff1": _matmul_tn(u2, da, "dw_ff1", ts, stack_cols=D_FF // N_DEV),
        "w_ff2": _matmul_tn(r, dh2b, "dw_ff2", ts),
        "w_ple_gate": _matmul_tn(u3, dlg, "dw_ple_gate", ts),
        "w_ple_proj": _matmul_tn(p, dpp, "dw_ple_proj", ts, stack_cols=D_MODEL // N_DEV),
    }
    if late_shards is None:
        dq_b, dk_b, dv_b, cs, rs = _fox_bwd(zm, c_pairs, dead, dab, ln_b, delta_b, tq)
        late_parts = None
    else:
        wire = _local_to_wire(dw)
        dq_b, dk_b, dv_b, cs, rs, *parts = _fox_bwd(zm, c_pairs, dead, dab, ln_b, delta_b, tq,
                                                    _scatter_ride([wire[n] for n in late_shards]))
        late_parts = dict(zip(late_shards, parts))

    up = lambda t: jnp.concatenate([t[SWA_BLOCK:], jnp.zeros((SWA_BLOCK, LANES), F32)], axis=0)
    dk_a, dv_a = dkc + up(dkp), dvc + up(dvp)
    df_t, db = _decay_bwd(cs, rs, f_t, b_col)
    df = jnp.pad(df_t.T, ((0, 0), (0, N_FPAD - FOX_HEADS)))
    dz = jnp.concatenate([dq_a, dk_a.astype(BF16), dv_a.astype(BF16), dq_b.astype(BF16), dk_b.astype(BF16), dv_b.astype(BF16),
                          df.astype(BF16), dgl], axis=1)
    dw["w_all"] = _matmul_tn(u1, dz, "dw_in", ts)
    if late_shards is None:
        dx, dgx = _in_proj_bwd(dz, dh1, x, w["w_all"], g_mix, tm)
    else:
        dx, dgx, late_parts["w_in"] = _in_proj_bwd(dz, dh1, x, w["w_all"], g_mix, tm,
                                                   _scatter_ride([_dw_in_to_wire(dw["w_all"])]))
    dsmall = {"g_mix": dgx[0], "g_mlp": dgm[0], "g_ple": dgp[0], "g_final": dgf[0],
              "b_forget": db[:, 0], "swa_sinks": dsk[:, 0]}
    return loss_acc[0, 0], dx, dw, dsmall, late_parts


_ROWS = lambda t: t.reshape(-1, t.shape[-1])
_BY_ROWS = lambda t: t.reshape(N_DEV, t.shape[0] // N_DEV, t.shape[1])
_SAME = lambda t: t
LOCAL_LAYOUT = {
    "w_in": ("w_all", _w_all_from_wire, _dw_in_to_wire), "w_br_swa": ("w_br_swa", _SAME, _SAME),
    "w_br_fox": ("w_br_fox", _SAME, _SAME), "w_mix_out": ("w_mix_out", _ROWS, _BY_ROWS),
    "w_ff1": ("w_ff1", _SAME, _SAME), "w_ff2": ("w_ff2", _SAME, _BY_ROWS),
    "w_ple_gate": ("w_ple_gate", _ROWS, _BY_ROWS), "w_ple_proj": ("w_ple_proj", _SAME, _SAME),
}


def _gathered_to_local(g):
    return {LOCAL_LAYOUT[n][0]: LOCAL_LAYOUT[n][1](t) for n, t in g.items()}


def _local_to_wire(dw):
    names = {local: n for n, (local, _, _) in LOCAL_LAYOUT.items()}
    return {names[local]: LOCAL_LAYOUT[names[local]][2](t) for local, t in dw.items()}


def kernel(x, p, g_mix, w_in, b_forget, swa_sinks, w_br_swa, w_br_fox, w_mix_out, g_mlp, w_ff1, w_ff2, g_ple, w_ple_gate, w_ple_proj, g_final, loss_target, m_g_mix, m_w_in, m_b_forget, m_swa_sinks, m_w_br_swa, m_w_br_fox, m_w_mix_out, m_g_mlp, m_w_ff1, m_w_ff2, m_g_ple, m_w_ple_gate, m_w_ple_proj, m_g_final, v_g_mix, v_w_in, v_b_forget, v_swa_sinks, v_w_br_swa, v_w_br_fox, v_w_mix_out, v_g_mlp, v_w_ff1, v_w_ff2, v_g_ple, v_w_ple_gate, v_w_ple_proj, v_g_final):
    given = dict(g_mix=g_mix, w_in=w_in, b_forget=b_forget, swa_sinks=swa_sinks, w_br_swa=w_br_swa, w_br_fox=w_br_fox,
                 w_mix_out=w_mix_out, g_mlp=g_mlp, w_ff1=w_ff1, w_ff2=w_ff2, g_ple=g_ple, w_ple_gate=w_ple_gate,
                 w_ple_proj=w_ple_proj, g_final=g_final)
    mom = dict(g_mix=m_g_mix, w_in=m_w_in, b_forget=m_b_forget, swa_sinks=m_swa_sinks, w_br_swa=m_w_br_swa,
               w_br_fox=m_w_br_fox, w_mix_out=m_w_mix_out, g_mlp=m_g_mlp, w_ff1=m_w_ff1, w_ff2=m_w_ff2, g_ple=m_g_ple,
               w_ple_gate=m_w_ple_gate, w_ple_proj=m_w_ple_proj, g_final=m_g_final)
    vel = dict(g_mix=v_g_mix, w_in=v_w_in, b_forget=v_b_forget, swa_sinks=v_swa_sinks, w_br_swa=v_w_br_swa,
               w_br_fox=v_w_br_fox, w_mix_out=v_w_mix_out, g_mlp=v_g_mlp, w_ff1=v_w_ff1, w_ff2=v_w_ff2, g_ple=v_g_ple,
               w_ple_gate=v_w_ple_gate, w_ple_proj=v_w_ple_proj, g_final=v_g_final)
    names = list(given)
    sharded = list(SHARDED)

    w_wire = {n: _wire_shard(n, given[n]) for n in sharded}
    late = [n for n in sharded if n != "w_in"]
    gathered = _all_gather([w_wire["w_in"].astype(BF16)])
    local_w = _gathered_to_local({"w_in": gathered[0]})
    small = {n: given[n].reshape(-1) for n in SMALL}

    n_tok = x.shape[1]
    tile = min(512, n_tok // 4)
    loss_part, dx, dw, dsmall, parts = _local_step(
        x[0], p[0, 0], loss_target[0], local_w, small, tm=tile, tq=tile, ts=min(2048, n_tok // 4),
        late_shards={n: w_wire[n].astype(BF16) for n in late})
    loss = lax.psum(loss_part, AXES)

    small_all = _small_exchange(_pack_small(dsmall))

    res = {}
    for n in sharded:
        part = parts[n]
        flat = part.reshape(N_DEV, -1, part.shape[-1])
        outs = _adamw(flat, w_wire[n], _wire_shard(n, mom[n]), _wire_shard(n, vel[n]), "adamw_" + n)
        res[n] = [_from_wire(n, o) for o in outs]
    outs_s = _adamw(small_all, _pack_small(small), _pack_small({n: mom[n] for n in SMALL}),
                    _pack_small({n: vel[n] for n in SMALL}), "adamw_small")
    small_res = [_unpack_small(o, given) for o in outs_s]

    groups = [[res[n][k] if n in res else small_res[k][n] for n in names] for k in range(4)]
    return (loss, dx[None], *groups[0], *groups[1], *groups[2], *groups[3])
```

```python
import numpy as np
import jax
import jax.numpy as jnp
from jax import lax
from jax.experimental import pallas as pl
from jax.experimental.pallas import tpu as pltpu

F32 = jnp.float32
BF16 = jnp.bfloat16

D_MODEL = 1024
HEAD_DIM = 64
SWA_HEADS = 8
FOX_HEADS = 8
CHUNK_SHIFT = 6
SWA_BLOCK = 128
WINDOW_CHUNKS = 2
D_FF = 4096
PLE_DIM = 256
RMS_EPS = 1e-6
N_MAIN = 2304
N_FPAD = 128
N_GATE = 2048
N_ALL = N_MAIN + N_FPAD + N_GATE
D_IN = N_MAIN + FOX_HEADS + N_GATE
SCALE = HEAD_DIM ** -0.5
NEG = -1e30

ADAM_LR = 0.001
ADAM_B1 = 0.9
ADAM_B2 = 0.999
ADAM_EPS = 1e-08
ADAM_WD = 0.01
ADAM_STEP = 10

N_DEV = 8
LANES = 128
V7X_VMEM_BYTES = 64 * 1024 * 1024
VMEM_LIMIT = V7X_VMEM_BYTES * 3 // 4
FOX_BWD_VMEM = V7X_VMEM_BYTES * 7 // 8
MESH = pl.DeviceIdType.MESH

_NT = (((1,), (1,)), ((), ()))
_TN = (((0,), (0,)), ((), ()))


def _params(n_grid, vmem_limit=VMEM_LIMIT):
    return pltpu.CompilerParams(dimension_semantics=("arbitrary",) * n_grid, vmem_limit_bytes=vmem_limit)


def _chunks(n, step):
    return [(s, min(step, n - s)) for s in range(0, n, step)]


def _sigmoid(x):
    return 1.0 / (1.0 + jnp.exp(-x))


def _dot(a, b):
    return jnp.dot(a, b, preferred_element_type=F32)


def _dot_nt(a, b):
    return lax.dot_general(a, b, _NT, preferred_element_type=F32)


def _dot_tn(a, b):
    return lax.dot_general(a, b, _TN, preferred_element_type=F32)


def _lane_concat(stacked_ref):
    return jnp.concatenate([stacked_ref[d] for d in range(N_DEV)], axis=1)


def _rms(h):
    return lax.rsqrt(jnp.mean(h * h, axis=-1, keepdims=True) + RMS_EPS)


def _rms_bwd(h, g, du):
    rs = _rms(h)
    n = h * rs
    dn = du * g
    dh = rs * (dn - n * jnp.mean(dn * n, axis=-1, keepdims=True))
    return dh, jnp.sum(du * n, axis=0, keepdims=True)


def _acc_rows(ref, i, row):
    @pl.when(i == 0)
    def _():
        ref[...] = jnp.zeros_like(ref)
    ref[...] += jnp.broadcast_to(row, ref.shape)


def _row_call(body, name, n_rows, tm, row_ins, const_ins, row_outs, acc_outs, ride=None, tile_outs=()):
    row_outs = list(row_outs)
    n_ri, n_ci, n_ro, n_ao = len(row_ins), len(const_ins), len(row_outs) + len(tile_outs), len(acc_outs)
    extra = ride if ride else _NO_RIDE
    n_ride = len(extra.arrays)
    grid = (n_rows // tm,)

    def kern(*refs):
        i = pl.program_id(0)
        ins, refs = refs[:n_ri + n_ci], refs[n_ri + n_ci:]
        ride_in, refs = refs[:n_ride], refs[n_ride:]
        outs, refs = refs[:n_ro + n_ao], refs[n_ro + n_ao:]
        ride_out, sems = refs[:n_ride], refs[n_ride:]
        if ride:
            ride.at_first_step(grid, ride_in, ride_out, sems)
        body(i, ins[:n_ri], ins[n_ri:], outs[:n_ro], outs[n_ro:])
        if ride:
            ride.at_last_step(grid, ride_in, ride_out, sems)

    def whole(a):
        zeros = (0,) * a.ndim
        return pl.BlockSpec(a.shape, lambda i: zeros, pipeline_mode=pl.Buffered(1))

    in_specs = [pl.BlockSpec((tm, a.shape[1]), lambda i: (i, 0)) for a in row_ins]
    in_specs += [whole(a) for a in const_ins] + extra.in_specs
    out_specs = [pl.BlockSpec((tm, c), lambda i: (i, 0)) for c, _ in row_outs]
    out_specs += [pl.BlockSpec((8, c), lambda i: (i, 0)) for c in tile_outs]
    out_specs += [pl.BlockSpec((8, c), lambda i: (0, 0)) for c in acc_outs] + extra.out_specs
    out_shape = [jax.ShapeDtypeStruct((n_rows, c), dt) for c, dt in row_outs]
    out_shape += [jax.ShapeDtypeStruct((8 * grid[0], c), F32) for c in tile_outs]
    out_shape += [jax.ShapeDtypeStruct((8, c), F32) for c in acc_outs] + extra.out_shape
    return pl.pallas_call(kern, grid=grid, in_specs=in_specs, out_specs=out_specs, out_shape=out_shape,
                          scratch_shapes=extra.scratch, name=name,
                          compiler_params=_params(1))(*row_ins, *const_ins, *extra.arrays)


def _in_proj(x, g_mix, w_all, tm):
    def body(i, ins, consts, outs, accs):
        x_ref, = ins
        g_ref, w_ref = consts
        u_ref, zm_ref, zfg_ref, zf_ref, nrm_ref = outs
        xv = x_ref[...]
        u = ((xv * _rms(xv)) * g_ref[...]).astype(BF16)
        u_ref[...] = u
        for s, n in _chunks(N_MAIN, 768):
            zm_ref[:, s:s + n] = _dot(u, w_ref[:, s:s + n]).astype(BF16)
        for s, n in _chunks(N_FPAD + N_GATE, 512):
            zfg_ref[:, s:s + n] = _dot(u, w_ref[:, N_MAIN + s:N_MAIN + s + n])
        zf_ref[...] = zfg_ref[:, :N_FPAD]
        lane = lax.broadcasted_iota(jnp.int32, (4 * LANES, LANES), 0)
        head = lax.broadcasted_iota(jnp.int32, (4 * LANES, LANES), 1)
        pick = (lane // HEAD_DIM == head).astype(BF16)
        rows = []
        for col in (Q_COL, K_COL):
            t = zm_ref[:, col * LANES:(col + 4) * LANES].astype(F32)
            rows.append(jnp.max(_dot((t * t).astype(BF16), pick), axis=0, keepdims=True))
        nrm_ref[...] = jnp.concatenate(rows + [jnp.zeros((6, LANES), F32)], axis=0)

    *outs, nrm = _row_call(body, "in_proj", x.shape[0], tm, [x], [g_mix, w_all],
                           [(D_MODEL, BF16), (N_MAIN, BF16), (N_FPAD + N_GATE, F32), (N_FPAD, F32)], [],
                           tile_outs=[LANES])
    return (*outs, nrm)


def _mix_fwd(attn_a, attn_b, zfg, x, w_sa, w_fo, w_mo, g_mlp, tm):
    def body(i, ins, consts, outs, accs):
        aa_ref, ab_ref, zfg_ref, x_ref = ins
        wsa_ref, wfo_ref, wmo_ref, g_ref = consts
        ya_ref, yb_ref, mx_ref, h1_ref, u2_ref = outs
        ya = _dot(aa_ref[...], _lane_concat(wsa_ref))
        yb = _dot(ab_ref[...], _lane_concat(wfo_ref))
        g0 = _sigmoid(zfg_ref[:, N_FPAD:N_FPAD + D_MODEL])
        g1 = _sigmoid(zfg_ref[:, N_FPAD + D_MODEL:N_FPAD + 2 * D_MODEL])
        mixed = (g0 * ya + g1 * yb).astype(BF16)
        ya_ref[...] = ya.astype(BF16)
        yb_ref[...] = yb.astype(BF16)
        mx_ref[...] = mixed
        h1 = x_ref[...] + _dot(mixed, wmo_ref[...])
        h1_ref[...] = h1
        u2_ref[...] = ((h1 * _rms(h1)) * g_ref[...]).astype(BF16)

    return _row_call(body, "mix_fwd", x.shape[0], tm, [attn_a, attn_b, zfg, x], [w_sa, w_fo, w_mo, g_mlp],
                     [(D_MODEL, BF16), (D_MODEL, BF16), (D_MODEL, BF16), (D_MODEL, F32), (D_MODEL, BF16)], [])


def _ffn_fwd(u2, h1, w1s, w2s, tm):
    ch = D_FF // N_DEV

    def body(i, ins, consts, outs, accs):
        u_ref, h1_ref = ins
        w1_ref, w2_ref = consts
        a_ref, r_ref, h2_ref = outs
        u = u_ref[...]
        acc = h1_ref[...]
        for c in range(N_DEV):
            a = _dot(u, w1_ref[c])
            a_ref[:, c * ch:(c + 1) * ch] = a.astype(BF16)
            r = jnp.square(jnp.maximum(a, 0.0)).astype(BF16)
            r_ref[:, c * ch:(c + 1) * ch] = r
            acc = acc + _dot(r, w2_ref[c])
        h2_ref[...] = acc

    return _row_call(body, "ffn_fwd", u2.shape[0], tm, [u2, h1], [w1s, w2s],
                     [(D_FF, BF16), (D_FF, BF16), (D_MODEL, F32)], [])


def _head_fwd_bwd(h2, p, tgt, g_ple, w_pg, w_pp, g_fin, tm):
    def body(i, ins, consts, outs, accs):
        h2_ref, p_ref, t_ref = ins
        gp_ref, wpg_ref, wpp_ref, gf_ref = consts
        dh3_ref, dlg_ref, dpp_ref, u3_ref = outs
        loss_ref, dgf_ref = accs
        h2 = h2_ref[...]
        u3 = ((h2 * _rms(h2)) * gp_ref[...]).astype(BF16)
        u3_ref[...] = u3
        pg = _sigmoid(_dot(u3, wpg_ref[...]))
        pp = _dot(p_ref[...].astype(BF16), _lane_concat(wpp_ref))
        h3 = h2 + pg * pp
        rs3 = _rms(h3)
        n3 = h3 * rs3
        gf = gf_ref[...]
        err = n3 * gf - t_ref[...]
        row_loss = 0.5 * jnp.mean(err * err, axis=-1, keepdims=True)
        _acc_rows(loss_ref, i, jnp.broadcast_to(jnp.sum(row_loss, axis=0, keepdims=True), (1, LANES)))
        dy = err * (1.0 / D_MODEL)
        _acc_rows(dgf_ref, i, jnp.sum(dy * n3, axis=0, keepdims=True))
        dn = dy * gf
        dh3 = rs3 * (dn - n3 * jnp.mean(dn * n3, axis=-1, keepdims=True))
        dh3_ref[...] = dh3
        dpp_ref[...] = (dh3 * pg).astype(BF16)
        dlg_ref[...] = ((dh3 * pp) * pg * (1.0 - pg)).astype(BF16)

    return _row_call(body, "head_fwd_bwd", h2.shape[0], tm, [h2, p, tgt], [g_ple, w_pg, w_pp, g_fin],
                     [(D_MODEL, F32), (D_MODEL, BF16), (D_MODEL, BF16), (D_MODEL, BF16)], [LANES, D_MODEL])


def _ffn_bwd_a(dlg, dh3, h2, a, w_pg, g_ple, w2s, tm):
    ch = D_FF // N_DEV

    def body(i, ins, consts, outs, accs):
        dlg_ref, dh3_ref, h2_ref, a_ref = ins
        wpg_ref, gp_ref, w2_ref = consts
        dh2_ref, dh2b_ref, da_ref = outs
        dgp_ref, = accs
        du3 = _dot_nt(dlg_ref[...], wpg_ref[...])
        dh, dg = _rms_bwd(h2_ref[...], gp_ref[...], du3)
        _acc_rows(dgp_ref, i, dg)
        dh2 = dh3_ref[...] + dh
        dh2_ref[...] = dh2
        dh2b = dh2.astype(BF16)
        dh2b_ref[...] = dh2b
        for c in range(N_DEV):
            dr = _dot_nt(dh2b, w2_ref[c])
            av = a_ref[:, c * ch:(c + 1) * ch].astype(F32)
            da_ref[:, c * ch:(c + 1) * ch] = (dr * (2.0 * jnp.maximum(av, 0.0))).astype(BF16)

    return _row_call(body, "ffn_bwd_a", h2.shape[0], tm, [dlg, dh3, h2, a], [w_pg, g_ple, w2s],
                     [(D_MODEL, F32), (D_MODEL, BF16), (D_FF, BF16)], [D_MODEL])


def _ffn_bwd_b(da, dh2, h1, ya, yb, zfg, attn_b, w1s, g_mlp, w_mo, w_sa, w_fo, tm):
    ch = D_FF // N_DEV

    def body(i, ins, consts, outs, accs):
        da_ref, dh2_ref, h1_ref, ya_ref, yb_ref, zfg_ref, ob_ref = ins
        w1_ref, gm_ref, wmo_ref, wsa_ref, wfo_ref = consts
        dh1_ref, dh1b_ref, dgl_ref, dya_ref, dyb_ref, daa_ref, dab_ref, dl_ref = outs
        dgm_ref, = accs
        du2 = _dot_nt(da_ref[:, 0:ch], w1_ref[0])
        for c in range(1, N_DEV):
            du2 = du2 + _dot_nt(da_ref[:, c * ch:(c + 1) * ch], w1_ref[c])
        dh, dg = _rms_bwd(h1_ref[...], gm_ref[...], du2)
        _acc_rows(dgm_ref, i, dg)
        dh1 = dh2_ref[...] + dh
        dh1_ref[...] = dh1
        dh1b = dh1.astype(BF16)
        dh1b_ref[...] = dh1b
        dmx = _dot_nt(dh1b, wmo_ref[...])
        g0 = _sigmoid(zfg_ref[:, N_FPAD:N_FPAD + D_MODEL])
        g1 = _sigmoid(zfg_ref[:, N_FPAD + D_MODEL:N_FPAD + 2 * D_MODEL])
        dya = (dmx * g0).astype(BF16)
        dyb = (dmx * g1).astype(BF16)
        dya_ref[...] = dya
        dyb_ref[...] = dyb
        dgl_ref[:, 0:D_MODEL] = ((dmx * ya_ref[...].astype(F32)) * g0 * (1.0 - g0)).astype(BF16)
        dgl_ref[:, D_MODEL:2 * D_MODEL] = ((dmx * yb_ref[...].astype(F32)) * g1 * (1.0 - g1)).astype(BF16)
        daa_ref[...] = _dot_nt(dya, _lane_concat(wsa_ref)).astype(BF16)
        dab = _dot_nt(dyb, _lane_concat(wfo_ref)).astype(BF16)
        dab_ref[...] = dab
        half_in = lax.broadcasted_iota(jnp.int32, (LANES, 2 * LANES), 0) // HEAD_DIM
        half_out = lax.broadcasted_iota(jnp.int32, (LANES, 2 * LANES), 1) // LANES
        pick = (half_in == half_out).astype(BF16)
        for pair in range(FOX_HEADS // 2):
            cols = slice(pair * LANES, (pair + 1) * LANES)
            prod = dab[:, cols].astype(F32) * ob_ref[:, cols].astype(F32)
            hi = prod.astype(BF16)
            lo_part = (prod - hi.astype(F32)).astype(BF16)
            dl_ref[:, 2 * pair * LANES:(2 * pair + 2) * LANES] = _dot(hi, pick) + _dot(lo_part, pick)

    half = D_MODEL // 2
    return _row_call(body, "ffn_bwd_b", h1.shape[0], tm, [da, dh2, h1, ya, yb, zfg, attn_b],
                     [w1s, g_mlp, w_mo, w_sa, w_fo],
                     [(D_MODEL, F32), (D_MODEL, BF16), (N_GATE, BF16), (D_MODEL, BF16), (D_MODEL, BF16),
                      (half, BF16), (half, BF16), (FOX_HEADS * LANES, F32)], [D_MODEL])


def _in_proj_bwd(dz, dh1, x, w_all, g_mix, tm, ride=None):
    def body(i, ins, consts, outs, accs):
        dz_ref, dh1_ref, x_ref = ins
        w_ref, g_ref = consts
        dx_ref, = outs
        dgx_ref, = accs
        du1 = _dot_nt(dz_ref[...], w_ref[...])
        dh, dg = _rms_bwd(x_ref[...], g_ref[...], du1)
        _acc_rows(dgx_ref, i, dg)
        dx_ref[...] = dh1_ref[...] + dh

    return _row_call(body, "in_proj_bwd", x.shape[0], tm, [dz, dh1, x], [w_all, g_mix],
                     [(D_MODEL, F32)], [D_MODEL], ride)


def _matmul_tn(a, b, name, ts, stack_cols=0):
    n_rows, ka = a.shape
    n = b.shape[1]
    tk = min(ka, 1024)
    tn = 896 if n % 1024 else 1024
    n_stack = tn // stack_cols if stack_cols else 0
    assert ka % tk == 0 and n % tn == 0 and n_rows % ts == 0 and (not stack_cols or tk == ka)
    n_steps = n_rows // ts

    def kern(a_ref, b_ref, o_ref, acc_ref):
        s = pl.program_id(2)

        @pl.when(s == 0)
        def _():
            acc_ref[...] = jnp.zeros_like(acc_ref)
        acc_ref[...] += _dot_tn(a_ref[...].astype(BF16), b_ref[...])

        @pl.when(s == n_steps - 1)
        def _():
            if stack_cols:
                for c in range(n_stack):
                    o_ref[c] = acc_ref[:, c * stack_cols:(c + 1) * stack_cols].astype(BF16)
            else:
                o_ref[...] = acc_ref[...].astype(BF16)

    if stack_cols:
        out_spec = pl.BlockSpec((n_stack, tk, stack_cols), lambda i, j, s: (j, 0, 0))
        out_shape = jax.ShapeDtypeStruct((n // stack_cols, ka, stack_cols), BF16)
    else:
        out_spec = pl.BlockSpec((tk, tn), lambda i, j, s: (i, j))
        out_shape = jax.ShapeDtypeStruct((ka, n), BF16)
    return pl.pallas_call(
        kern, grid=(ka // tk, n // tn, n_steps),
        in_specs=[pl.BlockSpec((ts, tk), lambda i, j, s: (s, i)), pl.BlockSpec((ts, tn), lambda i, j, s: (s, j))],
        out_specs=out_spec, out_shape=out_shape, scratch_shapes=[pltpu.VMEM((tk, tn), F32)], name=name,
        compiler_params=_params(3))(a, b)


SCAN_CHUNK = 512


def _decay_cumsum(f_t, b_col):
    n_tok = f_t.shape[1]
    ch = min(SCAN_CHUNK, n_tok)

    def kern(f_ref, b_ref, c_ref):
        r = lax.broadcasted_iota(jnp.int32, (ch, ch), 0)
        c = lax.broadcasted_iota(jnp.int32, (ch, ch), 1)
        tri = (r <= c).astype(F32)
        carry = jnp.zeros((8, 1), F32)
        for k in range(n_tok // ch):
            xv = f_ref[:, k * ch:(k + 1) * ch] + b_ref[...]
            lf = jnp.minimum(xv, 0.0) - jnp.log(1.0 + jnp.exp(-jnp.abs(xv)))
            cs = jnp.dot(lf, tri, precision=lax.Precision.HIGHEST, preferred_element_type=F32) + carry
            c_ref[:, k * ch:(k + 1) * ch] = cs
            carry = cs[:, ch - 1:ch]

    return pl.pallas_call(kern, out_shape=jax.ShapeDtypeStruct((8, n_tok), F32), name="decay_cumsum",
                          compiler_params=_params(0))(f_t, b_col)


def _decay_bwd(cs, rs, f_t, b_col):
    n_tok = f_t.shape[1]
    ch = min(SCAN_CHUNK, n_tok)
    n_ch = n_tok // ch

    def kern(cs_ref, rs_ref, f_ref, b_ref, df_ref, db_ref, carry_ref):
        k = pl.program_id(0)

        @pl.when(k == 0)
        def _():
            carry_ref[...] = jnp.zeros_like(carry_ref)
            db_ref[...] = jnp.zeros_like(db_ref)

        r = lax.broadcasted_iota(jnp.int32, (ch, ch), 0)
        c = lax.broadcasted_iota(jnp.int32, (ch, ch), 1)
        tri = (r >= c).astype(F32)
        head = lax.broadcasted_iota(jnp.int32, (8, 4 * LANES), 0)
        lane = lax.broadcasted_iota(jnp.int32, (8, 4 * LANES), 1)
        pick = (lane == HEAD_DIM * head).astype(F32)
        dc = lax.dot_general(pick, rs_ref[...] - cs_ref[...], _NT, precision=lax.Precision.HIGHEST,
                             preferred_element_type=F32)
        rc = jnp.dot(dc, tri, precision=lax.Precision.HIGHEST, preferred_element_type=F32) + carry_ref[:, 0:1]
        carry_ref[...] = jnp.broadcast_to(rc[:, 0:1], carry_ref.shape)
        df = rc / (1.0 + jnp.exp(f_ref[...] + b_ref[...]))
        df_ref[...] = df
        db_ref[...] += jnp.broadcast_to(jnp.sum(df, axis=1, keepdims=True), db_ref.shape)

    back = lambda k: n_ch - 1 - k
    wide = pl.BlockSpec((ch, 4 * LANES), lambda k: (back(k), 0))
    row = pl.BlockSpec((8, ch), lambda k: (0, back(k)))
    return pl.pallas_call(
        kern, grid=(n_ch,),
        in_specs=[wide, wide, row, pl.BlockSpec((8, 1), lambda k: (0, 0))],
        out_specs=[row, pl.BlockSpec((8, LANES), lambda k: (0, 0))],
        out_shape=[jax.ShapeDtypeStruct((8, n_tok), F32), jax.ShapeDtypeStruct((8, LANES), F32)],
        scratch_shapes=[pltpu.VMEM((8, LANES), F32)], name="decay_bwd", compiler_params=_params(1))(cs, rs, f_t, b_col)


def _swa_bias_table():
    row = jnp.arange(SWA_BLOCK)[:, None] + SWA_BLOCK
    col = jnp.arange(2 * SWA_BLOCK)[None, :]
    cd = (row >> CHUNK_SHIFT) - (col >> CHUNK_SHIFT)
    band = (cd >= 0) & (cd <= WINDOW_CHUNKS)
    slopes = jnp.asarray([2.0 ** -(h + 1) for h in range(SWA_HEADS)], F32)
    bias = -slopes[:, None, None] * jnp.abs(row - col).astype(F32)[None]
    return jnp.stack([jnp.where(band & (col >= SWA_BLOCK), bias, NEG), jnp.where(band, bias, NEG)])


SWA_BIAS_SPEC = pl.BlockSpec((None, SWA_HEADS, SWA_BLOCK, 2 * SWA_BLOCK), lambda n: (jnp.minimum(n, 1), 0, 0, 0))


def _swap_halves(t):
    return pltpu.roll(t.astype(F32), HEAD_DIM, axis=1).astype(t.dtype)


def _swa_specs():
    blk = SWA_BLOCK
    q = pl.BlockSpec((blk, 4 * LANES), lambda n: (n, 0))
    kp = pl.BlockSpec((blk, LANES), lambda n: (jnp.maximum(n - 1, 0), 4))
    kc = pl.BlockSpec((blk, LANES), lambda n: (n, 4))
    vp = pl.BlockSpec((blk, LANES), lambda n: (jnp.maximum(n - 1, 0), 5))
    vc = pl.BlockSpec((blk, LANES), lambda n: (n, 5))
    return q, kp, kc, vp, vc


SWA_GROUPS = ([h for h in range(SWA_HEADS) if h % 2 == h // 4], [h for h in range(SWA_HEADS) if h % 2 != h // 4])


def _stack_heads(ref, heads, lo, mask_halves):
    tiles = []
    for h in heads:
        t = ref[:, (h // 2) * LANES:(h // 2 + 1) * LANES]
        tiles.append(jnp.where(lo if h % 2 == 0 else ~lo, t, jnp.zeros_like(t)) if mask_halves else t)
    return jnp.concatenate(tiles, axis=0)


def _per_head_column(values, heads):
    return jnp.concatenate([jnp.full((SWA_BLOCK, 1), values(h), F32) for h in heads], axis=0)


def _swa_scores(q_ref, kx, heads, lo, bias_ref):
    qa = _stack_heads(q_ref, heads, lo, True) * SCALE
    return qa, _dot_nt(qa, kx) + jnp.concatenate([bias_ref[h] for h in heads], axis=0)


def _swa_fwd(zm, sinks):
    n_tok = zm.shape[0]
    blk = SWA_BLOCK

    def kern(q_ref, kp_ref, kc_ref, vp_ref, vc_ref, bias_ref, sink_ref, o_ref, lse_ref):
        k2 = jnp.concatenate([kp_ref[...], kc_ref[...]], axis=0)
        v2 = jnp.concatenate([vp_ref[...], vc_ref[...]], axis=0)
        ksw, vsw = _swap_halves(k2), _swap_halves(v2)
        lane = lax.broadcasted_iota(jnp.int32, (blk, LANES), 1)
        lo = lane < HEAD_DIM
        lse_t = jnp.zeros((blk, LANES), F32)
        for pair in range(SWA_HEADS // 2):
            q2 = q_ref[:, pair * LANES:(pair + 1) * LANES]
            outs = []
            for a in range(2):
                h = 2 * pair + a
                qa = jnp.where(lo if a == 0 else ~lo, q2, jnp.zeros_like(q2)) * SCALE
                kx, vx = (k2, v2) if h in SWA_GROUPS[0] else (ksw, vsw)
                s = _dot_nt(qa, kx) + bias_ref[h]
                sink = sink_ref[h]
                m = jnp.maximum(jnp.max(s, axis=-1, keepdims=True), sink)
                e = jnp.exp(s - m)
                l = jnp.sum(e, axis=-1, keepdims=True) + jnp.exp(sink - m)
                pn = (e * (1.0 / l)).astype(BF16)
                outs.append(_dot(pn, vx))
                lse_t = jnp.where(lane == h, m + jnp.log(l), lse_t)
            o_ref[:, pair * LANES:(pair + 1) * LANES] = jnp.where(lo, outs[0], outs[1]).astype(BF16)
        lse_ref[...] = lse_t

    q, kp, kc, vp, vc = _swa_specs()
    return pl.pallas_call(
        kern, grid=(n_tok // blk,),
        in_specs=[q, kp, kc, vp, vc, SWA_BIAS_SPEC, pl.BlockSpec(memory_space=pltpu.SMEM)],
        out_specs=[pl.BlockSpec((blk, 4 * LANES), lambda n: (n, 0)), pl.BlockSpec((blk, LANES), lambda n: (n, 0))],
        out_shape=[jax.ShapeDtypeStruct((n_tok, 4 * LANES), BF16), jax.ShapeDtypeStruct((n_tok, LANES), F32)],
        name="swa_fwd", compiler_params=_params(1))(zm, zm, zm, zm, zm, _swa_bias_table(), sinks)


def _swa_bwd(zm, sinks, d_out, out, lse):
    n_tok = zm.shape[0]
    blk = SWA_BLOCK

    def kern(q_ref, kp_ref, kc_ref, vp_ref, vc_ref, bias_ref, do_ref, o_ref, lse_ref, sink_ref,
             dq_ref, dkp_ref, dkc_ref, dvp_ref, dvc_ref, dsk_ref):
        n = pl.program_id(0)

        @pl.when(n == 0)
        def _():
            dsk_ref[...] = jnp.zeros_like(dsk_ref)

        k2 = jnp.concatenate([kp_ref[...], kc_ref[...]], axis=0)
        v2 = jnp.concatenate([vp_ref[...], vc_ref[...]], axis=0)
        lane = lax.broadcasted_iota(jnp.int32, (blk, LANES), 1)
        lo = lane < HEAD_DIM
        lse_t = lse_ref[...]
        dqs, dkv = {}, []
        for heads, kx, vx in ((SWA_GROUPS[0], k2, v2), (SWA_GROUPS[1], _swap_halves(k2), _swap_halves(v2))):
            qa, s = _swa_scores(q_ref, kx, heads, lo, bias_ref)
            doa = _stack_heads(do_ref, heads, lo, True)
            lse_g = jnp.concatenate([lse_t[:, h:h + 1] for h in heads], axis=0)
            prob = jnp.exp(s - lse_g)
            dd = jnp.sum(doa.astype(F32) * _stack_heads(o_ref, heads, lo, False).astype(F32), axis=-1, keepdims=True)
            ds = (prob * (_dot_nt(doa, vx) - dd)).astype(BF16)
            sink_part = -jnp.exp(_per_head_column(lambda h: sink_ref[h], heads) - lse_g) * dd
            dq = _dot(ds, kx) * SCALE
            for r, h in enumerate(heads):
                dqs[h] = dq[r * blk:(r + 1) * blk]
                dsk_ref[h:h + 1, :] += jnp.broadcast_to(
                    jnp.sum(sink_part[r * blk:(r + 1) * blk], axis=0, keepdims=True), (1, LANES))
            dkv.append((_dot_tn(ds, qa), _dot_tn(prob.astype(BF16), doa)))
        for pair in range(SWA_HEADS // 2):
            dq_ref[:, pair * LANES:(pair + 1) * LANES] = jnp.where(lo, dqs[2 * pair], dqs[2 * pair + 1]).astype(BF16)
        dk = dkv[0][0] + pltpu.roll(dkv[1][0], HEAD_DIM, axis=1)
        dv = dkv[0][1] + pltpu.roll(dkv[1][1], HEAD_DIM, axis=1)
        dkp_ref[...] = dk[0:blk]
        dkc_ref[...] = dk[blk:2 * blk]
        dvp_ref[...] = dv[0:blk]
        dvc_ref[...] = dv[blk:2 * blk]

    q, kp, kc, vp, vc = _swa_specs()
    wide = pl.BlockSpec((blk, 4 * LANES), lambda n: (n, 0))
    narrow = pl.BlockSpec((blk, LANES), lambda n: (n, 0))
    part = jax.ShapeDtypeStruct((n_tok, LANES), F32)
    return pl.pallas_call(
        kern, grid=(n_tok // blk,),
        in_specs=[q, kp, kc, vp, vc, SWA_BIAS_SPEC, wide, wide, narrow, pl.BlockSpec(memory_space=pltpu.SMEM)],
        out_specs=[wide, narrow, narrow, narrow, narrow, pl.BlockSpec((8, LANES), lambda n: (0, 0))],
        out_shape=[jax.ShapeDtypeStruct((n_tok, 4 * LANES), BF16), part, part, part, part,
                   jax.ShapeDtypeStruct((8, LANES), F32)],
        name="swa_bwd", compiler_params=_params(1))(zm, zm, zm, zm, zm, _swa_bias_table(), d_out, out, lse, sinks)


def _my_pos():
    return lax.axis_index("x"), lax.axis_index("y"), lax.axis_index("c")


def _peer(k):
    x, y, c = _my_pos()
    px, py, pc = x ^ (k >> 2), y ^ ((k >> 1) & 1), c ^ (k & 1)
    return (px, py, pc), 4 * px + 2 * py + pc


def _gather_copies(x_refs, out_refs, send_sems, recv_sems, local_sems):
    x, y, c = _my_pos()
    my_id = 4 * x + 2 * y + c
    local = [pltpu.make_async_copy(x_refs[w], out_refs[w].at[my_id], local_sems.at[w]) for w in range(len(x_refs))]
    sends, arrivals = [], []
    for k in range(1, N_DEV):
        peer, peer_id = _peer(k)
        for w in range(len(x_refs)):
            sems = dict(send_sem=send_sems.at[7 * w + k - 1], recv_sem=recv_sems.at[7 * w + k - 1],
                        device_id=peer, device_id_type=MESH)
            sends.append(pltpu.make_async_remote_copy(src_ref=x_refs[w], dst_ref=out_refs[w].at[my_id], **sems))
            arrivals.append(pltpu.make_async_remote_copy(src_ref=x_refs[w], dst_ref=out_refs[w].at[peer_id], **sems))
    return local, sends, arrivals


def _scatter_copies(g_refs, part_refs, send_sems, recv_sems, local_sems):
    x, y, c = _my_pos()
    my_id = 4 * x + 2 * y + c
    local = [pltpu.make_async_copy(g_refs[w].at[my_id], part_refs[w].at[0], local_sems.at[w])
             for w in range(len(g_refs))]
    sends, arrivals = [], []
    for k in range(1, N_DEV):
        peer, peer_id = _peer(k)
        for w in range(len(g_refs)):
            sems = dict(send_sem=send_sems.at[7 * w + k - 1], recv_sem=recv_sems.at[7 * w + k - 1],
                        device_id=peer, device_id_type=MESH)
            sends.append(pltpu.make_async_remote_copy(src_ref=g_refs[w].at[peer_id], dst_ref=part_refs[w].at[k], **sems))
            arrivals.append(pltpu.make_async_remote_copy(src_ref=g_refs[w].at[my_id], dst_ref=part_refs[w].at[k], **sems))
    return local, sends, arrivals


def _start_copies(local, sends, arrivals):
    for cp in local + sends:
        cp.start()


def _finish_copies(local, sends, arrivals):
    for cp in arrivals:
        cp.wait_recv()
    for cp in sends:
        cp.wait_send()
    for cp in local:
        cp.wait()


def _exchange_scratch(n_arrays):
    return [pltpu.SemaphoreType.DMA((7 * n_arrays,)), pltpu.SemaphoreType.DMA((7 * n_arrays,)),
            pltpu.SemaphoreType.DMA((n_arrays,))]


class _Ride:
    def __init__(self, arrays, out_shape, copies):
        self.arrays, self.out_shape, self.copies = list(arrays), list(out_shape), copies
        any_spec = pl.BlockSpec(memory_space=pl.ANY)
        self.in_specs = [any_spec] * len(self.arrays)
        self.out_specs = [any_spec] * len(self.arrays)
        self.scratch = _exchange_scratch(len(self.arrays)) if self.arrays else []

    def specs(self):
        return self

    @staticmethod
    def _at(grid, last):
        hit = [pl.program_id(d) == (n - 1 if last else 0) for d, n in enumerate(grid)]
        return hit[0] if len(hit) == 1 else jnp.logical_and(*hit)

    def at_first_step(self, grid, in_refs, out_refs, sems):
        @pl.when(self._at(grid, False))
        def _():
            _start_copies(*self.copies(in_refs, out_refs, *sems))

    def at_last_step(self, grid, in_refs, out_refs, sems):
        @pl.when(self._at(grid, True))
        def _():
            _finish_copies(*self.copies(in_refs, out_refs, *sems))


_NO_RIDE = _Ride([], [], None)


def _gather_ride(shards):
    return _Ride(shards, [jax.ShapeDtypeStruct((N_DEV,) + s.shape, s.dtype) for s in shards], _gather_copies)


def _scatter_ride(grads):
    return _Ride(grads, [jax.ShapeDtypeStruct(g.shape, g.dtype) for g in grads], _scatter_copies)


Q_COL, K_COL, V_COL = 6, 10, 14


def _causal(t, first_row):
    row = lax.broadcasted_iota(jnp.int32, t.shape, 0) + first_row
    col = lax.broadcasted_iota(jnp.int32, t.shape, 1)
    return jnp.where(col <= row, t, NEG)


def _unit_pieces(sub, diag, tq):
    half = tq // 2
    if not diag or half % LANES:
        return [(slice(sub * tq, (sub + 1) * tq), tq, 0 if diag else None)]
    return [(slice(sub * tq + r * half, sub * tq + (r + 1) * half), (r + 1) * half, r * half) for r in range(2)]


def _pad_rows(t, n_rows):
    return t if t.shape[0] == n_rows else jnp.concatenate([t, jnp.zeros((n_rows - t.shape[0], t.shape[1]), t.dtype)])


def _lane_tile(stat, width):
    return jnp.tile(stat, (1, width // LANES))


def _fox_steps(nq):
    steps = [(i2, j, 0 if j < 2 * i2 else 1 + j - 2 * i2) for i2 in range(nq // 2) for j in range(2 * i2 + 2)]
    return [np.asarray(col, np.int32) for col in zip(*steps)]


_SWEEPS = {0: [(0, False), (1, False)], 1: [(0, True), (1, False)], 2: [(1, True)]}


def _fox_dispatch(sweep, kind, dead_ref, head0, idx):
    dead0, dead1 = dead_ref[head0, idx] > 0.5, dead_ref[head0 + 1, idx] > 0.5
    live0, live1 = jnp.logical_not(dead0), jnp.logical_not(dead1)
    below = kind == 0
    pl.when(jnp.logical_and(below, jnp.logical_and(live0, live1)))(lambda: sweep(_SWEEPS[0], (0, 1)))
    pl.when(jnp.logical_and(below, jnp.logical_and(live0, dead1)))(lambda: sweep(_SWEEPS[0], (0,)))
    pl.when(jnp.logical_and(below, jnp.logical_and(dead0, live1)))(lambda: sweep(_SWEEPS[0], (1,)))
    pl.when(kind == 1)(lambda: sweep(_SWEEPS[1], (0, 1)))
    pl.when(kind == 2)(lambda: sweep(_SWEEPS[2], (0, 1)))


EXP_ZERO = 110.0
NORM_SLACK = 1.005


def _fox_dead_steps(nrm, c_pairs, tq):
    nq = nrm.shape[0] // 8
    norms = jnp.sqrt(nrm.reshape(nq, 8, LANES)[:, :2, :FOX_HEADS])
    qn, kn = norms[:, 0] * SCALE, norms[:, 1]
    cb = c_pairs.reshape(FOX_HEADS, nq, tq)
    c_max, c_min = jnp.max(cb, axis=-1).T, jnp.min(cb, axis=-1).T
    both = lambda t: jnp.max(t.reshape(nq // 2, 2, FOX_HEADS), axis=1)
    qn2, kn2, c_max2 = both(qn), both(kn), both(c_max)
    gap = qn2[:, None] * (kn[None] + kn2[:, None]) * NORM_SLACK + (c_max2[:, None] - c_min[None])
    below = jnp.arange(nq)[None, :] < 2 * jnp.arange(nq // 2)[:, None]
    dead = jnp.logical_and(gap < -EXP_ZERO, below[..., None])
    return dead.transpose(2, 0, 1).reshape(FOX_HEADS, -1).astype(F32)


def _fox_fwd(zm, c_pairs, dead, tq, ride=None):
    n_tok = zm.shape[0]
    nq = n_tok // tq
    ii, jj, kk = _fox_steps(nq)
    n_steps = len(ii)
    n_ride = len(ride.arrays) if ride else 0

    def kern(ii_ref, jj_ref, kk_ref, q_ref, k_ref, v_ref, ck_ref, dead_ref, *more):
        ride_in, (o_ref, ln_ref), ride_out = more[:n_ride], more[n_ride:n_ride + 2], more[n_ride + 2:2 * n_ride + 2]
        qs_ref, m_ref, l_ref, acc_ref = more[2 * n_ride + 2:2 * n_ride + 6]
        step = pl.program_id(1)
        j, kind = jj_ref[step], kk_ref[step]
        lo = lax.broadcasted_iota(jnp.int32, (2 * tq, LANES), 1) < HEAD_DIM
        if ride:
            ride.at_first_step((FOX_HEADS // 2, n_steps), ride_in, ride_out, more[2 * n_ride + 6:])

        @pl.when(j == 0)
        def _():
            q2 = q_ref[...]
            zq = jnp.zeros_like(q2)
            qs_ref[0] = jnp.where(lo, q2, zq) * SCALE
            qs_ref[1] = jnp.where(lo, zq, q2) * SCALE
            m_ref[...] = jnp.full(m_ref.shape, NEG, F32)
            l_ref[...] = jnp.zeros(l_ref.shape, F32)
            acc_ref[...] = jnp.zeros(acc_ref.shape, F32)

        def sweep(subs, heads):
            kv = k_ref[...]
            v_ones = jnp.concatenate([v_ref[...], jnp.ones((tq, LANES), BF16)], axis=1)
            for sub, diag in subs:
                for rows, n_keys, first_row in _unit_pieces(sub, diag, tq):
                    for a in heads:
                        t = _dot_nt(qs_ref[a, rows], kv[:n_keys]) - ck_ref[a:a + 1, :n_keys]
                        if diag:
                            t = _causal(t, first_row)
                        m_old = m_ref[a, rows]
                        m_new = jnp.maximum(m_old, jnp.max(t, axis=-1, keepdims=True))
                        alpha = jnp.exp(m_old - m_new)
                        e = jnp.exp(t - _lane_tile(m_new, n_keys)).astype(BF16)
                        pv = _dot(e, v_ones[:n_keys])
                        acc_ref[a, rows] = alpha * acc_ref[a, rows] + pv[:, :LANES]
                        l_ref[a, rows] = alpha * l_ref[a, rows] + pv[:, LANES:]
                        m_ref[a, rows] = m_new

        _fox_dispatch(sweep, kind, dead_ref, 2 * pl.program_id(0), ii_ref[step] * nq + j)

        @pl.when(kind == 2)
        def _():
            o_ref[...] = jnp.where(lo, acc_ref[0] / l_ref[0], acc_ref[1] / l_ref[1]).astype(BF16)
            ln_ref[:, :LANES] = m_ref[0] + jnp.log(l_ref[0])
            ln_ref[:, LANES:] = m_ref[1] + jnp.log(l_ref[1])

        if ride:
            ride.at_last_step((FOX_HEADS // 2, n_steps), ride_in, ride_out, more[2 * n_ride + 6:])

    blk = (tq, LANES)
    by_i = lambda col: (lambda hp, s, ii, jj, kk: (ii[s], col + hp))
    by_j = lambda col: (lambda hp, s, ii, jj, kk: (jj[s], col + hp))
    extra = ride.specs() if ride else _NO_RIDE
    grid_spec = pltpu.PrefetchScalarGridSpec(
        num_scalar_prefetch=3, grid=(FOX_HEADS // 2, n_steps),
        in_specs=[pl.BlockSpec((2 * tq, LANES), by_i(Q_COL)), pl.BlockSpec(blk, by_j(K_COL)),
                  pl.BlockSpec(blk, by_j(V_COL)),
                  pl.BlockSpec((None, 2, tq), lambda hp, s, ii, jj, kk: (hp, 0, jj[s])),
                  pl.BlockSpec(memory_space=pltpu.SMEM)] + extra.in_specs,
        out_specs=[pl.BlockSpec((2 * tq, LANES), by_i(0)), pl.BlockSpec((2 * tq, 2 * LANES), by_i(0))] + extra.out_specs,
        scratch_shapes=[pltpu.VMEM((2, 2 * tq, LANES), BF16), pltpu.VMEM((2, 2 * tq, LANES), F32),
                        pltpu.VMEM((2, 2 * tq, LANES), F32), pltpu.VMEM((2, 2 * tq, LANES), F32)] + extra.scratch)
    return pl.pallas_call(
        kern, grid_spec=grid_spec,
        out_shape=[jax.ShapeDtypeStruct((n_tok, 4 * LANES), BF16),
                   jax.ShapeDtypeStruct((n_tok, FOX_HEADS * LANES), F32)] + extra.out_shape,
        name="fox_fwd", compiler_params=_params(2))(ii, jj, kk, zm, zm, zm, c_pairs, dead, *extra.arrays)


def _fox_bwd(zm, c_pairs, dead, d_out, lnorm, delta, tq, ride=None):
    n_tok = zm.shape[0]
    nq = n_tok // tq
    ii, jj, kk = _fox_steps(nq)
    n_steps = len(ii)
    n_ride = len(ride.arrays) if ride else 0

    def kern(ii_ref, jj_ref, kk_ref, q_ref, k_ref, v_ref, ck_ref, dead_ref, do_ref, ln_ref, dl_ref, *more):
        ride_in, ride_out = more[:n_ride], more[n_ride + 5:2 * n_ride + 5]
        dq_ref, dk_ref, dv_ref, cs_ref, rs_ref = more[n_ride:n_ride + 5]
        qs_ref, qo_ref, dos_ref, dq_acc = more[2 * n_ride + 5:2 * n_ride + 9]
        step = pl.program_id(1)
        j, kind = jj_ref[step], kk_ref[step]
        lo = lax.broadcasted_iota(jnp.int32, (2 * tq, LANES), 1) < HEAD_DIM
        if ride:
            ride.at_first_step((FOX_HEADS // 2, n_steps), ride_in, ride_out, more[2 * n_ride + 9:])

        @pl.when(step == 0)
        def _():
            dk_ref[...] = jnp.zeros_like(dk_ref)
            dv_ref[...] = jnp.zeros_like(dv_ref)
            cs_ref[...] = jnp.zeros_like(cs_ref)

        @pl.when(j == 0)
        def _():
            q2, do2 = q_ref[...], do_ref[...]
            zq = jnp.zeros_like(q2)
            ones = jnp.ones((2 * tq, LANES), BF16)
            for a in range(2):
                half = lo if a == 0 else ~lo
                qa = jnp.where(half, q2, zq) * SCALE
                qs_ref[a] = qa
                qo_ref[a] = jnp.concatenate([qa, ones], axis=1)
                dos_ref[a] = jnp.where(half, do2, zq)
            dq_acc[...] = jnp.zeros(dq_acc.shape, F32)

        def sweep(subs, heads):
            kv, vv = k_ref[...], v_ref[...]
            k_ones = jnp.concatenate([kv, jnp.ones((tq, LANES), BF16)], axis=1)
            dk, dv, sums = None, None, {}
            pieces = [piece for sub, diag in subs for piece in _unit_pieces(sub, diag, tq)]
            for rows, n_keys, first_row in pieces:
                for a in heads:
                    t = _dot_nt(qs_ref[a, rows], kv[:n_keys]) - ck_ref[a:a + 1, :n_keys]
                    if first_row is not None:
                        t = _causal(t, first_row)
                    prob = jnp.exp(t - _lane_tile(ln_ref[rows, a * LANES:(a + 1) * LANES], n_keys))
                    dp = _dot_nt(dos_ref[a, rows], vv[:n_keys])
                    ds = (prob * (dp - _lane_tile(dl_ref[rows, a * LANES:(a + 1) * LANES], n_keys))).astype(BF16)
                    dq_acc[a, rows] += _dot(ds, k_ones[:n_keys])
                    dk_cs = _pad_rows(_dot_tn(ds, qo_ref[a, rows]), tq)
                    dv_a = _pad_rows(_dot_tn(prob.astype(BF16), dos_ref[a, rows]), tq)
                    dk = dk_cs[:, :LANES] if dk is None else dk + dk_cs[:, :LANES]
                    dv = dv_a if dv is None else dv + dv_a
                    sums[a] = dk_cs[:, LANES:] if a not in sums else sums[a] + dk_cs[:, LANES:]
            keys = pl.ds(pl.multiple_of(j * tq, tq), tq)
            dk_ref[keys, :] += dk
            cs_ref[keys, :] += jnp.where(lo[:tq], sums.get(0, 0.0), sums.get(1, 0.0))
            dv_ref[keys, :] += dv

        _fox_dispatch(sweep, kind, dead_ref, 2 * pl.program_id(0), ii_ref[step] * nq + j)

        @pl.when(kind == 2)
        def _():
            dq_ref[...] = jnp.where(lo, dq_acc[0, :, :LANES], dq_acc[1, :, :LANES]) * SCALE
            rs_ref[...] = jnp.where(lo, dq_acc[0, :, LANES:], dq_acc[1, :, LANES:])

        if ride:
            ride.at_last_step((FOX_HEADS // 2, n_steps), ride_in, ride_out, more[2 * n_ride + 9:])

    blk = (tq, LANES)
    by_i = lambda col: (lambda hp, s, ii, jj, kk: (ii[s], col + hp))
    by_j = lambda col: (lambda hp, s, ii, jj, kk: (jj[s], col + hp))
    resident = pl.BlockSpec((2 * tq, LANES), by_i(0))
    stat = pl.BlockSpec((2 * tq, 2 * LANES), by_i(0))
    whole = pl.BlockSpec((n_tok, LANES), lambda hp, s, ii, jj, kk: (0, hp))
    extra = ride.specs() if ride else _NO_RIDE
    grid_spec = pltpu.PrefetchScalarGridSpec(
        num_scalar_prefetch=3, grid=(FOX_HEADS // 2, n_steps),
        in_specs=[pl.BlockSpec((2 * tq, LANES), by_i(Q_COL)), pl.BlockSpec(blk, by_j(K_COL)),
                  pl.BlockSpec(blk, by_j(V_COL)),
                  pl.BlockSpec((None, 2, tq), lambda hp, s, ii, jj, kk: (hp, 0, jj[s])),
                  pl.BlockSpec(memory_space=pltpu.SMEM), resident, stat, stat] + extra.in_specs,
        out_specs=[resident, whole, whole, whole, resident] + extra.out_specs,
        scratch_shapes=[pltpu.VMEM((2, 2 * tq, LANES), BF16), pltpu.VMEM((2, 2 * tq, 2 * LANES), BF16),
                        pltpu.VMEM((2, 2 * tq, LANES), BF16), pltpu.VMEM((2, 2 * tq, 2 * LANES), F32)] + extra.scratch)
    wide = jax.ShapeDtypeStruct((n_tok, 4 * LANES), F32)
    return pl.pallas_call(
        kern, grid_spec=grid_spec, out_shape=[wide] * 5 + extra.out_shape, name="fox_bwd",
        compiler_params=_params(2, FOX_BWD_VMEM))(ii, jj, kk, zm, zm, zm, c_pairs, dead, d_out, lnorm, delta,
                                                  *extra.arrays)


def _all_gather(shards):
    n_w = len(shards)

    def kern(*refs):
        x_refs, out_refs = refs[:n_w], refs[n_w:2 * n_w]
        send_sems, recv_sems, local_sems = refs[2 * n_w:]
        x, y, c = _my_pos()
        me, sibling = (x, y, c), (x, y, 1 - c)
        chips = [(1 - x, y), (x, 1 - y), (1 - x, 1 - y)]

        def slot(w, px, py, pc):
            return out_refs[w].at[4 * px + 2 * py + pc]

        def copy(w, k, block, to, src=None):
            return pltpu.make_async_remote_copy(
                src_ref=slot(w, *block) if src is None else src, dst_ref=slot(w, *block),
                send_sem=send_sems.at[7 * w + k], recv_sem=recv_sems.at[7 * w + k], device_id=to, device_id_type=MESH)

        local, started = [], []
        for w in range(n_w):
            mine = pltpu.make_async_copy(x_refs[w], slot(w, *me), local_sems.at[w])
            mine.start()
            local.append(mine)
            first = [copy(w, 0, me, sibling, src=x_refs[w])]
            first += [copy(w, 1 + k, me, (*chip, c), src=x_refs[w]) for k, chip in enumerate(chips)]
            for cp in first:
                cp.start()
            started += first
        for k, chip in enumerate(chips):
            for w in range(n_w):
                copy(w, 1 + k, (*chip, c), me).wait_recv()
                passed = copy(w, 4 + k, (*chip, c), sibling)
                passed.start()
                started.append(passed)
        for w in range(n_w):
            copy(w, 0, sibling, me).wait_recv()
            for k, chip in enumerate(chips):
                copy(w, 4 + k, (*chip, 1 - c), me).wait_recv()
        for cp in started:
            cp.wait_send()
        for cp in local:
            cp.wait()

    any_spec = pl.BlockSpec(memory_space=pl.ANY)
    return pl.pallas_call(
        kern, out_shape=[jax.ShapeDtypeStruct((N_DEV,) + s.shape, s.dtype) for s in shards],
        in_specs=[any_spec] * n_w, out_specs=[any_spec] * n_w,
        scratch_shapes=[pltpu.SemaphoreType.DMA((7 * n_w,)), pltpu.SemaphoreType.DMA((7 * n_w,)),
                        pltpu.SemaphoreType.DMA((n_w,))],
        name="weight_all_gather")(*shards)


def _small_exchange(small):
    def kern(s_ref, sall_ref, *sems):
        copies = _gather_copies([s_ref], [sall_ref], *sems)
        _start_copies(*copies)
        _finish_copies(*copies)

    any_spec = pl.BlockSpec(memory_space=pl.ANY)
    return pl.pallas_call(
        kern, out_shape=jax.ShapeDtypeStruct((N_DEV,) + small.shape, small.dtype), in_specs=[any_spec],
        out_specs=any_spec, scratch_shapes=_exchange_scratch(1), name="small_grad_exchange")(small)


ADAMW_BLOCK_BYTES = 2 * 1024 * 1024


def _adamw(parts, w, m, v, name):
    n_parts, n_rows, n_cols = parts.shape
    limit = max(8, ADAMW_BLOCK_BYTES // (n_parts * n_cols * parts.dtype.itemsize))
    tr = max(t for t in range(8, n_rows + 1, 8) if n_rows % t == 0 and t <= limit)

    def kern(p_ref, w_ref, m_ref, v_ref, g_out, d_out, m_out, v_out):
        g = p_ref[0].astype(F32)
        for k in range(1, n_parts):
            g = g + p_ref[k].astype(F32)
        m_new = ADAM_B1 * m_ref[...] + (1.0 - ADAM_B1) * g
        v_new = ADAM_B2 * v_ref[...] + (1.0 - ADAM_B2) * jnp.square(g)
        m_hat = m_new / (1.0 - ADAM_B1 ** ADAM_STEP)
        v_hat = v_new / (1.0 - ADAM_B2 ** ADAM_STEP)
        g_out[...] = g
        d_out[...] = -ADAM_LR * (m_hat / (jnp.sqrt(v_hat) + ADAM_EPS) + ADAM_WD * w_ref[...])
        m_out[...] = m_new
        v_out[...] = v_new

    row = pl.BlockSpec((tr, n_cols), lambda i: (i, 0))
    out = jax.ShapeDtypeStruct((n_rows, n_cols), F32)
    return pl.pallas_call(
        kern, grid=(n_rows // tr,),
        in_specs=[pl.BlockSpec((n_parts, tr, n_cols), lambda i: (0, i, 0)), row, row, row],
        out_specs=[row, row, row, row], out_shape=[out, out, out, out], name=name,
        compiler_params=_params(1))(parts, w, m, v)


SHARDED = {
    "w_in": ((D_MODEL, D_IN), 1), "w_br_swa": ((512, D_MODEL), 1), "w_br_fox": ((512, D_MODEL), 1),
    "w_mix_out": ((D_MODEL, D_MODEL), 0), "w_ff1": ((D_MODEL, D_FF), 1), "w_ff2": ((D_FF, D_MODEL), 0),
    "w_ple_gate": ((D_MODEL, D_MODEL), 0), "w_ple_proj": ((PLE_DIM, D_MODEL), 1),
}
W_IN_SHARD = D_IN // N_DEV
W_IN_PAD = 640
SMALL = ("g_mix", "g_mlp", "g_ple", "g_final", "b_forget", "swa_sinks")
SMALL_COLS = 1024


def _wire_shard(name, a):
    a = a.reshape(a.shape[-2:])
    return jnp.pad(a, ((0, 0), (0, W_IN_PAD - W_IN_SHARD))) if name == "w_in" else a


def _from_wire(name, a):
    return (a[:, :W_IN_SHARD] if name == "w_in" else a)[None]


def _w_all_from_wire(stacked):
    w_in = jnp.concatenate([stacked[d][:, :W_IN_SHARD] for d in range(N_DEV)], axis=1)
    fpad = jnp.zeros((D_MODEL, N_FPAD - FOX_HEADS), stacked.dtype)
    return jnp.concatenate([w_in[:, :N_MAIN + FOX_HEADS], fpad, w_in[:, N_MAIN + FOX_HEADS:]], axis=1)


def _dw_in_to_wire(dw_all):
    dw_in = jnp.concatenate([dw_all[:, :N_MAIN + FOX_HEADS], dw_all[:, N_MAIN + N_FPAD:]], axis=1)
    pad = jnp.zeros((D_MODEL, W_IN_PAD - W_IN_SHARD), dw_all.dtype)
    return jnp.stack([jnp.concatenate([dw_in[:, d * W_IN_SHARD:(d + 1) * W_IN_SHARD], pad], axis=1)
                      for d in range(N_DEV)])


def _pack_small(vals, scalar=None):
    rows = [jnp.pad(vals[n].reshape(-1), (0, SMALL_COLS - vals[n].size)) for n in SMALL]
    if scalar is not None:
        rows.append(jnp.pad(scalar.reshape(1), (0, SMALL_COLS - 1)))
    rows += [jnp.zeros((SMALL_COLS,), F32)] * (8 - len(rows))
    return jnp.stack(rows)


def _unpack_small(slab, like):
    return {n: slab[r, :like[n].size].reshape(like[n].shape) for r, n in enumerate(SMALL)}


def _local_step(x, p, tgt, w, small, tm, tq, ts, late_shards=None):
    n_tok = x.shape[0]
    row = lambda v: v.reshape(1, -1)
    g_mix, g_mlp, g_ple, g_fin = row(small["g_mix"]), row(small["g_mlp"]), row(small["g_ple"]), row(small["g_final"])
    sinks = small["swa_sinks"].reshape(-1)
    b_col = small["b_forget"].reshape(FOX_HEADS, 1)

    assert tm == tq
    u1, zm, zfg, zf, nrm = _in_proj(x, g_mix, w["w_all"], tm)
    f_t = zf[:, :FOX_HEADS].T
    c_pairs = _decay_cumsum(f_t, b_col).reshape(FOX_HEADS // 2, 2, n_tok)
    attn_a, lse_a = _swa_fwd(zm, sinks)
    dead = _fox_dead_steps(nrm, c_pairs, tq)
    if late_shards is None:
        attn_b, ln_b = _fox_fwd(zm, c_pairs, dead, tq)
    else:
        attn_b, ln_b, *late = _fox_fwd(zm, c_pairs, dead, tq, _gather_ride(list(late_shards.values())))
        w = {**w, **_gathered_to_local(dict(zip(late_shards, late)))}
    ya, yb, mixed, h1, u2 = _mix_fwd(attn_a, attn_b, zfg, x, w["w_br_swa"], w["w_br_fox"], w["w_mix_out"], g_mlp, tm)
    a, r, h2 = _ffn_fwd(u2, h1, w["w_ff1"], w["w_ff2"], tm // 2)
    dh3, dlg, dpp, u3, loss_acc, dgf = _head_fwd_bwd(h2, p, tgt, g_ple, w["w_ple_gate"], w["w_ple_proj"], g_fin, tm)

    dh2, dh2b, da, dgp = _ffn_bwd_a(dlg, dh3, h2, a, w["w_ple_gate"], g_ple, w["w_ff2"], tm // 2)
    dh1, dh1b, dgl, dya, dyb, daa, dab, delta_b, dgm = _ffn_bwd_b(
        da, dh2, h1, ya, yb, zfg, attn_b, w["w_ff1"], g_mlp, w["w_mix_out"], w["w_br_swa"], w["w_br_fox"], tm // 2)
    dq_a, dkp, dkc, dvp, dvc, dsk = _swa_bwd(zm, sinks, daa, attn_a, lse_a)
    dw = {
        "w_br_swa": _matmul_tn(attn_a, dya, "dw_br_swa", ts, stack_cols=D_MODEL // N_DEV),
        "w_br_fox": _matmul_tn(attn_b, dyb, "dw_br_fox", ts, stack_cols=D_MODEL // N_DEV),
        "w_mix_out": _matmul_tn(mixed, dh1b, "dw_mix_out", ts),
        "w_ff1": _matmul_tn(u2, da, "dw_ff1", ts, stack_cols=D_FF // N_DEV),
        "w_ff2": _matmul_tn(r, dh2b, "dw_ff2", ts),
        "w_ple_gate": _matmul_tn(u3, dlg, "dw_ple_gate", ts),
        "w_ple_proj": _matmul_tn(p, dpp, "dw_ple_proj", ts, stack_cols=D_MODEL // N_DEV),
    }
    if late_shards is None:
        dq_b, dk_b, dv_b, cs, rs = _fox_bwd(zm, c_pairs, dead, dab, ln_b, delta_b, tq)
        late_parts = None
    else:
        wire = _local_to_wire(dw)
        dq_b, dk_b, dv_b, cs, rs, *parts = _fox_bwd(zm, c_pairs, dead, dab, ln_b, delta_b, tq,
                                                    _scatter_ride([wire[n] for n in late_shards]))
        late_parts = dict(zip(late_shards, parts))

    up = lambda t: jnp.concatenate([t[SWA_BLOCK:], jnp.zeros((SWA_BLOCK, LANES), F32)], axis=0)
    dk_a, dv_a = dkc + up(dkp), dvc + up(dvp)
    df_t, db = _decay_bwd(cs, rs, f_t, b_col)
    df = jnp.pad(df_t.T, ((0, 0), (0, N_FPAD - FOX_HEADS)))
    dz = jnp.concatenate([dq_a, dk_a.astype(BF16), dv_a.astype(BF16), dq_b.astype(BF16), dk_b.astype(BF16), dv_b.astype(BF16),
                          df.astype(BF16), dgl], axis=1)
    dw["w_all"] = _matmul_tn(u1, dz, "dw_in", ts)
    if late_shards is None:
        dx, dgx = _in_proj_bwd(dz, dh1, x, w["w_all"], g_mix, tm)
    else:
        dx, dgx, late_parts["w_in"] = _in_proj_bwd(dz, dh1, x, w["w_all"], g_mix, tm,
                                                   _scatter_ride([_dw_in_to_wire(dw["w_all"])]))
    dsmall = {"g_mix": dgx[0], "g_mlp": dgm[0], "g_ple": dgp[0], "g_final": dgf[0],
              "b_forget": db[:, 0], "swa_sinks": dsk[:, 0]}
    return loss_acc[0, 0], dx, dw, dsmall, late_parts


_ROWS = lambda t: t.reshape(-1, t.shape[-1])
_BY_ROWS = lambda t: t.reshape(N_DEV, t.shape[0] // N_DEV, t.shape[1])
_SAME = lambda t: t
LOCAL_LAYOUT = {
    "w_in": ("w_all", _w_all_from_wire, _dw_in_to_wire), "w_br_swa": ("w_br_swa", _SAME, _SAME),
    "w_br_fox": ("w_br_fox", _SAME, _SAME), "w_mix_out": ("w_mix_out", _ROWS, _BY_ROWS),
    "w_ff1": ("w_ff1", _SAME, _SAME), "w_ff2": ("w_ff2", _SAME, _BY_ROWS),
    "w_ple_gate": ("w_ple_gate", _ROWS, _BY_ROWS), "w_ple_proj": ("w_ple_proj", _SAME, _SAME),
}


def _gathered_to_local(g):
    return {LOCAL_LAYOUT[n][0]: LOCAL_LAYOUT[n][1](t) for n, t in g.items()}


def _local_to_wire(dw):
    names = {local: n for n, (local, _, _) in LOCAL_LAYOUT.items()}
    return {names[local]: LOCAL_LAYOUT[names[local]][2](t) for local, t in dw.items()}


def kernel(x, p, g_mix, w_in, b_forget, swa_sinks, w_br_swa, w_br_fox, w_mix_out, g_mlp, w_ff1, w_ff2, g_ple, w_ple_gate, w_ple_proj, g_final, loss_target, m_g_mix, m_w_in, m_b_forget, m_swa_sinks, m_w_br_swa, m_w_br_fox, m_w_mix_out, m_g_mlp, m_w_ff1, m_w_ff2, m_g_ple, m_w_ple_gate, m_w_ple_proj, m_g_final, v_g_mix, v_w_in, v_b_forget, v_swa_sinks, v_w_br_swa, v_w_br_fox, v_w_mix_out, v_g_mlp, v_w_ff1, v_w_ff2, v_g_ple, v_w_ple_gate, v_w_ple_proj, v_g_final):
    given = dict(g_mix=g_mix, w_in=w_in, b_forget=b_forget, swa_sinks=swa_sinks, w_br_swa=w_br_swa, w_br_fox=w_br_fox,
                 w_mix_out=w_mix_out, g_mlp=g_mlp, w_ff1=w_ff1, w_ff2=w_ff2, g_ple=g_ple, w_ple_gate=w_ple_gate,
                 w_ple_proj=w_ple_proj, g_final=g_final)
    mom = dict(g_mix=m_g_mix, w_in=m_w_in, b_forget=m_b_forget, swa_sinks=m_swa_sinks, w_br_swa=m_w_br_swa,
               w_br_fox=m_w_br_fox, w_mix_out=m_w_mix_out, g_mlp=m_g_mlp, w_ff1=m_w_ff1, w_ff2=m_w_ff2, g_ple=m_g_ple,
               w_ple_gate=m_w_ple_gate, w_ple_proj=m_w_ple_proj, g_final=m_g_final)
    vel = dict(g_mix=v_g_mix, w_in=v_w_in, b_forget=v_b_forget, swa_sinks=v_swa_sinks, w_br_swa=v_w_br_swa,
               w_br_fox=v_w_br_fox, w_mix_out=v_w_mix_out, g_mlp=v_g_mlp, w_ff1=v_w_ff1, w_ff2=v_w_ff2, g_ple=v_g_ple,
               w_ple_gate=v_w_ple_gate, w_ple_proj=v_w_ple_proj, g_final=v_g_final)
    names = list(given)
    sharded = list(SHARDED)

    w_wire = {n: _wire_shard(n, given[n]) for n in sharded}
    late = [n for n in sharded if n != "w_in"]
    gathered = _all_gather([w_wire["w_in"].astype(BF16)])
    local_w = _gathered_to_local({"w_in": gathered[0]})
    small = {n: given[n].reshape(-1) for n in SMALL}

    n_tok = x.shape[1]
    tile = min(512, n_tok // 4)
    loss_part, dx, dw, dsmall, parts = _local_step(
        x[0], p[0, 0], loss_target[0], local_w, small, tm=tile, tq=tile, ts=min(2048, n_tok // 4),
        late_shards={n: w_wire[n].astype(BF16) for n in late})
    small_all = _small_exchange(_pack_small(dsmall, loss_part))

    res = {}
    for n in sharded:
        part = parts[n]
        flat = part.reshape(N_DEV, -1, part.shape[-1])
        outs = _adamw(flat, w_wire[n], _wire_shard(n, mom[n]), _wire_shard(n, vel[n]), "adamw_" + n)
        res[n] = [_from_wire(n, o) for o in outs]
    outs_s = _adamw(small_all, _pack_small(small), _pack_small({n: mom[n] for n in SMALL}),
                    _pack_small({n: vel[n] for n in SMALL}), "adamw_small")
    small_res = [_unpack_small(o, given) for o in outs_s]
    loss = outs_s[0][len(SMALL), 0]

    groups = [[res[n][k] if n in res else small_res[k][n] for n in names] for k in range(4)]
    return (loss, dx[None], *groups[0], *groups[1], *groups[2], *groups[3])
```

```python
import numpy as np
import jax
import jax.numpy as jnp
from jax import lax
from jax.experimental import pallas as pl
from jax.experimental.pallas import tpu as pltpu

F32 = jnp.float32
BF16 = jnp.bfloat16

D_MODEL = 1024
HEAD_DIM = 64
SWA_HEADS = 8
FOX_HEADS = 8
CHUNK_SHIFT = 6
SWA_BLOCK = 128
WINDOW_CHUNKS = 2
D_FF = 4096
PLE_DIM = 256
RMS_EPS = 1e-6
N_MAIN = 2304
N_FPAD = 128
N_GATE = 2048
N_ALL = N_MAIN + N_FPAD + N_GATE
D_IN = N_MAIN + FOX_HEADS + N_GATE
SCALE = HEAD_DIM ** -0.5
NEG = -1e30

ADAM_LR = 0.001
ADAM_B1 = 0.9
ADAM_B2 = 0.999
ADAM_EPS = 1e-08
ADAM_WD = 0.01
ADAM_STEP = 10

N_DEV = 8
LANES = 128
V7X_VMEM_BYTES = 64 * 1024 * 1024
VMEM_LIMIT = V7X_VMEM_BYTES * 3 // 4
FOX_BWD_VMEM = V7X_VMEM_BYTES * 7 // 8
MESH = pl.DeviceIdType.MESH

_NT = (((1,), (1,)), ((), ()))
_TN = (((0,), (0,)), ((), ()))


def _params(n_grid, vmem_limit=VMEM_LIMIT):
    return pltpu.CompilerParams(dimension_semantics=("arbitrary",) * n_grid, vmem_limit_bytes=vmem_limit)


def _chunks(n, step):
    return [(s, min(step, n - s)) for s in range(0, n, step)]


def _sigmoid(x):
    return 1.0 / (1.0 + jnp.exp(-x))


def _dot(a, b):
    return jnp.dot(a, b, preferred_element_type=F32)


def _dot_nt(a, b):
    return lax.dot_general(a, b, _NT, preferred_element_type=F32)


def _dot_tn(a, b):
    return lax.dot_general(a, b, _TN, preferred_element_type=F32)


def _lane_concat(stacked_ref):
    return jnp.concatenate([stacked_ref[d] for d in range(N_DEV)], axis=1)


def _rms(h):
    return lax.rsqrt(jnp.mean(h * h, axis=-1, keepdims=True) + RMS_EPS)


def _rms_bwd(h, g, du):
    rs = _rms(h)
    n = h * rs
    dn = du * g
    dh = rs * (dn - n * jnp.mean(dn * n, axis=-1, keepdims=True))
    return dh, jnp.sum(du * n, axis=0, keepdims=True)


def _acc_rows(ref, i, row):
    @pl.when(i == 0)
    def _():
        ref[...] = jnp.zeros_like(ref)
    ref[...] += jnp.broadcast_to(row, ref.shape)


def _row_call(body, name, n_rows, tm, row_ins, const_ins, row_outs, acc_outs, ride=None, tile_outs=()):
    row_outs = list(row_outs)
    n_ri, n_ci, n_ro, n_ao = len(row_ins), len(const_ins), len(row_outs) + len(tile_outs), len(acc_outs)
    extra = ride if ride else _NO_RIDE
    n_ride = len(extra.arrays)
    grid = (n_rows // tm,)

    def kern(*refs):
        i = pl.program_id(0)
        ins, refs = refs[:n_ri + n_ci], refs[n_ri + n_ci:]
        ride_in, refs = refs[:n_ride], refs[n_ride:]
        outs, refs = refs[:n_ro + n_ao], refs[n_ro + n_ao:]
        ride_out, sems = refs[:n_ride], refs[n_ride:]
        if ride:
            ride.at_first_step(grid, ride_in, ride_out, sems)
        body(i, ins[:n_ri], ins[n_ri:], outs[:n_ro], outs[n_ro:])
        if ride:
            ride.at_last_step(grid, ride_in, ride_out, sems)

    def whole(a):
        zeros = (0,) * a.ndim
        return pl.BlockSpec(a.shape, lambda i: zeros, pipeline_mode=pl.Buffered(1))

    in_specs = [pl.BlockSpec((tm, a.shape[1]), lambda i: (i, 0)) for a in row_ins]
    in_specs += [whole(a) for a in const_ins] + extra.in_specs
    out_specs = [pl.BlockSpec((tm, c), lambda i: (i, 0)) for c, _ in row_outs]
    out_specs += [pl.BlockSpec((8, c), lambda i: (i, 0)) for c in tile_outs]
    out_specs += [pl.BlockSpec((8, c), lambda i: (0, 0)) for c in acc_outs] + extra.out_specs
    out_shape = [jax.ShapeDtypeStruct((n_rows, c), dt) for c, dt in row_outs]
    out_shape += [jax.ShapeDtypeStruct((8 * grid[0], c), F32) for c in tile_outs]
    out_shape += [jax.ShapeDtypeStruct((8, c), F32) for c in acc_outs] + extra.out_shape
    return pl.pallas_call(kern, grid=grid, in_specs=in_specs, out_specs=out_specs, out_shape=out_shape,
                          scratch_shapes=extra.scratch, name=name,
                          compiler_params=_params(1))(*row_ins, *const_ins, *extra.arrays)


def _in_proj(x, g_mix, w_all, tm):
    def body(i, ins, consts, outs, accs):
        x_ref, = ins
        g_ref, w_ref = consts
        u_ref, zm_ref, zfg_ref, zf_ref, nrm_ref = outs
        xv = x_ref[...]
        u = ((xv * _rms(xv)) * g_ref[...]).astype(BF16)
        u_ref[...] = u
        for s, n in _chunks(N_MAIN, 768):
            zm_ref[:, s:s + n] = _dot(u, w_ref[:, s:s + n]).astype(BF16)
        for s, n in _chunks(N_FPAD + N_GATE, 512):
            zfg_ref[:, s:s + n] = _dot(u, w_ref[:, N_MAIN + s:N_MAIN + s + n])
        zf_ref[...] = zfg_ref[:, :N_FPAD]
        lane = lax.broadcasted_iota(jnp.int32, (4 * LANES, LANES), 0)
        head = lax.broadcasted_iota(jnp.int32, (4 * LANES, LANES), 1)
        pick = (lane // HEAD_DIM == head).astype(BF16)
        rows = []
        for col in (Q_COL, K_COL):
            t = zm_ref[:, col * LANES:(col + 4) * LANES].astype(F32)
            rows.append(jnp.max(_dot((t * t).astype(BF16), pick), axis=0, keepdims=True))
        nrm_ref[...] = jnp.concatenate(rows + [jnp.zeros((6, LANES), F32)], axis=0)

    *outs, nrm = _row_call(body, "in_proj", x.shape[0], tm, [x], [g_mix, w_all],
                           [(D_MODEL, BF16), (N_MAIN, BF16), (N_FPAD + N_GATE, F32), (N_FPAD, F32)], [],
                           tile_outs=[LANES])
    return (*outs, nrm)


def _mix_fwd(attn_a, attn_b, zfg, x, w_sa, w_fo, w_mo, g_mlp, tm):
    def body(i, ins, consts, outs, accs):
        aa_ref, ab_ref, zfg_ref, x_ref = ins
        wsa_ref, wfo_ref, wmo_ref, g_ref = consts
        ya_ref, yb_ref, mx_ref, h1_ref, u2_ref = outs
        ya = _dot(aa_ref[...], _lane_concat(wsa_ref))
        yb = _dot(ab_ref[...], _lane_concat(wfo_ref))
        g0 = _sigmoid(zfg_ref[:, N_FPAD:N_FPAD + D_MODEL])
        g1 = _sigmoid(zfg_ref[:, N_FPAD + D_MODEL:N_FPAD + 2 * D_MODEL])
        mixed = (g0 * ya + g1 * yb).astype(BF16)
        ya_ref[...] = ya.astype(BF16)
        yb_ref[...] = yb.astype(BF16)
        mx_ref[...] = mixed
        h1 = x_ref[...] + _dot(mixed, wmo_ref[...])
        h1_ref[...] = h1
        u2_ref[...] = ((h1 * _rms(h1)) * g_ref[...]).astype(BF16)

    return _row_call(body, "mix_fwd", x.shape[0], tm, [attn_a, attn_b, zfg, x], [w_sa, w_fo, w_mo, g_mlp],
                     [(D_MODEL, BF16), (D_MODEL, BF16), (D_MODEL, BF16), (D_MODEL, F32), (D_MODEL, BF16)], [])


def _ffn_fwd(u2, h1, w1s, w2s, tm):
    ch = D_FF // N_DEV

    def body(i, ins, consts, outs, accs):
        u_ref, h1_ref = ins
        w1_ref, w2_ref = consts
        a_ref, r_ref, h2_ref = outs
        u = u_ref[...]
        acc = h1_ref[...]
        for c in range(N_DEV):
            a = _dot(u, w1_ref[c])
            a_ref[:, c * ch:(c + 1) * ch] = a.astype(BF16)
            r = jnp.square(jnp.maximum(a, 0.0)).astype(BF16)
            r_ref[:, c * ch:(c + 1) * ch] = r
            acc = acc + _dot(r, w2_ref[c])
        h2_ref[...] = acc

    return _row_call(body, "ffn_fwd", u2.shape[0], tm, [u2, h1], [w1s, w2s],
                     [(D_FF, BF16), (D_FF, BF16), (D_MODEL, F32)], [])


def _head_fwd_bwd(h2, p, tgt, g_ple, w_pg, w_pp, g_fin, tm):
    def body(i, ins, consts, outs, accs):
        h2_ref, p_ref, t_ref = ins
        gp_ref, wpg_ref, wpp_ref, gf_ref = consts
        dh3_ref, dlg_ref, dpp_ref, u3_ref = outs
        loss_ref, dgf_ref = accs
        h2 = h2_ref[...]
        u3 = ((h2 * _rms(h2)) * gp_ref[...]).astype(BF16)
        u3_ref[...] = u3
        pg = _sigmoid(_dot(u3, wpg_ref[...]))
        pp = _dot(p_ref[...].astype(BF16), _lane_concat(wpp_ref))
        h3 = h2 + pg * pp
        rs3 = _rms(h3)
        n3 = h3 * rs3
        gf = gf_ref[...]
        err = n3 * gf - t_ref[...]
        row_loss = 0.5 * jnp.mean(err * err, axis=-1, keepdims=True)
        _acc_rows(loss_ref, i, jnp.broadcast_to(jnp.sum(row_loss, axis=0, keepdims=True), (1, LANES)))
        dy = err * (1.0 / D_MODEL)
        _acc_rows(dgf_ref, i, jnp.sum(dy * n3, axis=0, keepdims=True))
        dn = dy * gf
        dh3 = rs3 * (dn - n3 * jnp.mean(dn * n3, axis=-1, keepdims=True))
        dh3_ref[...] = dh3
        dpp_ref[...] = (dh3 * pg).astype(BF16)
        dlg_ref[...] = ((dh3 * pp) * pg * (1.0 - pg)).astype(BF16)

    return _row_call(body, "head_fwd_bwd", h2.shape[0], tm, [h2, p, tgt], [g_ple, w_pg, w_pp, g_fin],
                     [(D_MODEL, F32), (D_MODEL, BF16), (D_MODEL, BF16), (D_MODEL, BF16)], [LANES, D_MODEL])


def _ffn_bwd_a(dlg, dh3, h2, a, w_pg, g_ple, w2s, tm):
    ch = D_FF // N_DEV

    def body(i, ins, consts, outs, accs):
        dlg_ref, dh3_ref, h2_ref, a_ref = ins
        wpg_ref, gp_ref, w2_ref = consts
        dh2_ref, dh2b_ref, da_ref = outs
        dgp_ref, = accs
        du3 = _dot_nt(dlg_ref[...], wpg_ref[...])
        dh, dg = _rms_bwd(h2_ref[...], gp_ref[...], du3)
        _acc_rows(dgp_ref, i, dg)
        dh2 = dh3_ref[...] + dh
        dh2_ref[...] = dh2
        dh2b = dh2.astype(BF16)
        dh2b_ref[...] = dh2b
        for c in range(N_DEV):
            dr = _dot_nt(dh2b, w2_ref[c])
            av = a_ref[:, c * ch:(c + 1) * ch].astype(F32)
            da_ref[:, c * ch:(c + 1) * ch] = (dr * (2.0 * jnp.maximum(av, 0.0))).astype(BF16)

    return _row_call(body, "ffn_bwd_a", h2.shape[0], tm, [dlg, dh3, h2, a], [w_pg, g_ple, w2s],
                     [(D_MODEL, F32), (D_MODEL, BF16), (D_FF, BF16)], [D_MODEL])


def _ffn_bwd_b(da, dh2, h1, ya, yb, zfg, attn_b, w1s, g_mlp, w_mo, w_sa, w_fo, tm):
    ch = D_FF // N_DEV

    def body(i, ins, consts, outs, accs):
        da_ref, dh2_ref, h1_ref, ya_ref, yb_ref, zfg_ref, ob_ref = ins
        w1_ref, gm_ref, wmo_ref, wsa_ref, wfo_ref = consts
        dh1_ref, dh1b_ref, dgl_ref, dya_ref, dyb_ref, daa_ref, dab_ref, dl_ref = outs
        dgm_ref, = accs
        du2 = _dot_nt(da_ref[:, 0:ch], w1_ref[0])
        for c in range(1, N_DEV):
            du2 = du2 + _dot_nt(da_ref[:, c * ch:(c + 1) * ch], w1_ref[c])
        dh, dg = _rms_bwd(h1_ref[...], gm_ref[...], du2)
        _acc_rows(dgm_ref, i, dg)
        dh1 = dh2_ref[...] + dh
        dh1_ref[...] = dh1
        dh1b = dh1.astype(BF16)
        dh1b_ref[...] = dh1b
        dmx = _dot_nt(dh1b, wmo_ref[...])
        g0 = _sigmoid(zfg_ref[:, N_FPAD:N_FPAD + D_MODEL])
        g1 = _sigmoid(zfg_ref[:, N_FPAD + D_MODEL:N_FPAD + 2 * D_MODEL])
        dya = (dmx * g0).astype(BF16)
        dyb = (dmx * g1).astype(BF16)
        dya_ref[...] = dya
        dyb_ref[...] = dyb
        dgl_ref[:, 0:D_MODEL] = ((dmx * ya_ref[...].astype(F32)) * g0 * (1.0 - g0)).astype(BF16)
        dgl_ref[:, D_MODEL:2 * D_MODEL] = ((dmx * yb_ref[...].astype(F32)) * g1 * (1.0 - g1)).astype(BF16)
        daa_ref[...] = _dot_nt(dya, _lane_concat(wsa_ref)).astype(BF16)
        dab = _dot_nt(dyb, _lane_concat(wfo_ref)).astype(BF16)
        dab_ref[...] = dab
        half_in = lax.broadcasted_iota(jnp.int32, (LANES, 2 * LANES), 0) // HEAD_DIM
        half_out = lax.broadcasted_iota(jnp.int32, (LANES, 2 * LANES), 1) // LANES
        pick = (half_in == half_out).astype(BF16)
        for pair in range(FOX_HEADS // 2):
            cols = slice(pair * LANES, (pair + 1) * LANES)
            prod = dab[:, cols].astype(F32) * ob_ref[:, cols].astype(F32)
            hi = prod.astype(BF16)
            lo_part = (prod - hi.astype(F32)).astype(BF16)
            dl_ref[:, 2 * pair * LANES:(2 * pair + 2) * LANES] = _dot(hi, pick) + _dot(lo_part, pick)

    half = D_MODEL // 2
    return _row_call(body, "ffn_bwd_b", h1.shape[0], tm, [da, dh2, h1, ya, yb, zfg, attn_b],
                     [w1s, g_mlp, w_mo, w_sa, w_fo],
                     [(D_MODEL, F32), (D_MODEL, BF16), (N_GATE, BF16), (D_MODEL, BF16), (D_MODEL, BF16),
                      (half, BF16), (half, BF16), (FOX_HEADS * LANES, F32)], [D_MODEL])


def _in_proj_bwd(dz, dh1, x, w_all, g_mix, tm, ride=None):
    def body(i, ins, consts, outs, accs):
        dz_ref, dh1_ref, x_ref = ins
        w_ref, g_ref = consts
        dx_ref, = outs
        dgx_ref, = accs
        du1 = _dot_nt(dz_ref[...], w_ref[...])
        dh, dg = _rms_bwd(x_ref[...], g_ref[...], du1)
        _acc_rows(dgx_ref, i, dg)
        dx_ref[...] = dh1_ref[...] + dh

    return _row_call(body, "in_proj_bwd", x.shape[0], tm, [dz, dh1, x], [w_all, g_mix],
                     [(D_MODEL, F32)], [D_MODEL], ride)


def _matmul_tn(a, b, name, ts, stack_cols=0):
    n_rows, ka = a.shape
    n = b.shape[1]
    tk = min(ka, 1024)
    tn = 896 if n % 1024 else 1024
    n_stack = tn // stack_cols if stack_cols else 0
    assert ka % tk == 0 and n % tn == 0 and n_rows % ts == 0 and (not stack_cols or tk == ka)
    n_steps = n_rows // ts

    def kern(a_ref, b_ref, o_ref, acc_ref):
        s = pl.program_id(2)

        @pl.when(s == 0)
        def _():
            acc_ref[...] = jnp.zeros_like(acc_ref)
        acc_ref[...] += _dot_tn(a_ref[...].astype(BF16), b_ref[...])

        @pl.when(s == n_steps - 1)
        def _():
            if stack_cols:
                for c in range(n_stack):
                    o_ref[c] = acc_ref[:, c * stack_cols:(c + 1) * stack_cols].astype(BF16)
            else:
                o_ref[...] = acc_ref[...].astype(BF16)

    if stack_cols:
        out_spec = pl.BlockSpec((n_stack, tk, stack_cols), lambda i, j, s: (j, 0, 0))
        out_shape = jax.ShapeDtypeStruct((n // stack_cols, ka, stack_cols), BF16)
    else:
        out_spec = pl.BlockSpec((tk, tn), lambda i, j, s: (i, j))
        out_shape = jax.ShapeDtypeStruct((ka, n), BF16)
    return pl.pallas_call(
        kern, grid=(ka // tk, n // tn, n_steps),
        in_specs=[pl.BlockSpec((ts, tk), lambda i, j, s: (s, i)), pl.BlockSpec((ts, tn), lambda i, j, s: (s, j))],
        out_specs=out_spec, out_shape=out_shape, scratch_shapes=[pltpu.VMEM((tk, tn), F32)], name=name,
        compiler_params=_params(3))(a, b)


SCAN_CHUNK = 512


def _decay_cumsum(f_t, b_col):
    n_tok = f_t.shape[1]
    ch = min(SCAN_CHUNK, n_tok)

    def kern(f_ref, b_ref, c_ref):
        r = lax.broadcasted_iota(jnp.int32, (ch, ch), 0)
        c = lax.broadcasted_iota(jnp.int32, (ch, ch), 1)
        tri = (r <= c).astype(F32)
        carry = jnp.zeros((8, 1), F32)
        for k in range(n_tok // ch):
            xv = f_ref[:, k * ch:(k + 1) * ch] + b_ref[...]
            lf = jnp.minimum(xv, 0.0) - jnp.log(1.0 + jnp.exp(-jnp.abs(xv)))
            cs = jnp.dot(lf, tri, precision=lax.Precision.HIGHEST, preferred_element_type=F32) + carry
            c_ref[:, k * ch:(k + 1) * ch] = cs
            carry = cs[:, ch - 1:ch]

    return pl.pallas_call(kern, out_shape=jax.ShapeDtypeStruct((8, n_tok), F32), name="decay_cumsum",
                          compiler_params=_params(0))(f_t, b_col)


def _decay_bwd(cs, rs, f_t, b_col):
    n_tok = f_t.shape[1]
    ch = min(SCAN_CHUNK, n_tok)
    n_ch = n_tok // ch

    def kern(cs_ref, rs_ref, f_ref, b_ref, df_ref, db_ref, carry_ref):
        k = pl.program_id(0)

        @pl.when(k == 0)
        def _():
            carry_ref[...] = jnp.zeros_like(carry_ref)
            db_ref[...] = jnp.zeros_like(db_ref)

        r = lax.broadcasted_iota(jnp.int32, (ch, ch), 0)
        c = lax.broadcasted_iota(jnp.int32, (ch, ch), 1)
        tri = (r >= c).astype(F32)
        head = lax.broadcasted_iota(jnp.int32, (8, 4 * LANES), 0)
        lane = lax.broadcasted_iota(jnp.int32, (8, 4 * LANES), 1)
        pick = (lane == HEAD_DIM * head).astype(F32)
        dc = lax.dot_general(pick, rs_ref[...] - cs_ref[...], _NT, precision=lax.Precision.HIGHEST,
                             preferred_element_type=F32)
        rc = jnp.dot(dc, tri, precision=lax.Precision.HIGHEST, preferred_element_type=F32) + carry_ref[:, 0:1]
        carry_ref[...] = jnp.broadcast_to(rc[:, 0:1], carry_ref.shape)
        df = rc / (1.0 + jnp.exp(f_ref[...] + b_ref[...]))
        df_ref[...] = df
        db_ref[...] += jnp.broadcast_to(jnp.sum(df, axis=1, keepdims=True), db_ref.shape)

    back = lambda k: n_ch - 1 - k
    wide = pl.BlockSpec((ch, 4 * LANES), lambda k: (back(k), 0))
    row = pl.BlockSpec((8, ch), lambda k: (0, back(k)))
    return pl.pallas_call(
        kern, grid=(n_ch,),
        in_specs=[wide, wide, row, pl.BlockSpec((8, 1), lambda k: (0, 0))],
        out_specs=[row, pl.BlockSpec((8, LANES), lambda k: (0, 0))],
        out_shape=[jax.ShapeDtypeStruct((8, n_tok), F32), jax.ShapeDtypeStruct((8, LANES), F32)],
        scratch_shapes=[pltpu.VMEM((8, LANES), F32)], name="decay_bwd", compiler_params=_params(1))(cs, rs, f_t, b_col)


def _swa_bias_table():
    row = jnp.arange(SWA_BLOCK)[:, None] + SWA_BLOCK
    col = jnp.arange(2 * SWA_BLOCK)[None, :]
    cd = (row >> CHUNK_SHIFT) - (col >> CHUNK_SHIFT)
    band = (cd >= 0) & (cd <= WINDOW_CHUNKS)
    slopes = jnp.asarray([2.0 ** -(h + 1) for h in range(SWA_HEADS)], F32)
    bias = -slopes[:, None, None] * jnp.abs(row - col).astype(F32)[None]
    return jnp.stack([jnp.where(band & (col >= SWA_BLOCK), bias, NEG), jnp.where(band, bias, NEG)])


SWA_BIAS_SPEC = pl.BlockSpec((None, SWA_HEADS, SWA_BLOCK, 2 * SWA_BLOCK), lambda n: (jnp.minimum(n, 1), 0, 0, 0))


def _swap_halves(t):
    return pltpu.roll(t.astype(F32), HEAD_DIM, axis=1).astype(t.dtype)


def _swa_specs():
    blk = SWA_BLOCK
    q = pl.BlockSpec((blk, 4 * LANES), lambda n: (n, 0))
    kp = pl.BlockSpec((blk, LANES), lambda n: (jnp.maximum(n - 1, 0), 4))
    kc = pl.BlockSpec((blk, LANES), lambda n: (n, 4))
    vp = pl.BlockSpec((blk, LANES), lambda n: (jnp.maximum(n - 1, 0), 5))
    vc = pl.BlockSpec((blk, LANES), lambda n: (n, 5))
    return q, kp, kc, vp, vc


SWA_GROUPS = ([h for h in range(SWA_HEADS) if h % 2 == h // 4], [h for h in range(SWA_HEADS) if h % 2 != h // 4])


def _stack_heads(ref, heads, lo, mask_halves):
    tiles = []
    for h in heads:
        t = ref[:, (h // 2) * LANES:(h // 2 + 1) * LANES]
        tiles.append(jnp.where(lo if h % 2 == 0 else ~lo, t, jnp.zeros_like(t)) if mask_halves else t)
    return jnp.concatenate(tiles, axis=0)


def _per_head_column(values, heads):
    return jnp.concatenate([jnp.full((SWA_BLOCK, 1), values(h), F32) for h in heads], axis=0)


def _swa_scores(q_ref, kx, heads, lo, bias_ref):
    qa = _stack_heads(q_ref, heads, lo, True) * SCALE
    return qa, _dot_nt(qa, kx) + jnp.concatenate([bias_ref[h] for h in heads], axis=0)


def _swa_fwd(zm, sinks):
    n_tok = zm.shape[0]
    blk = SWA_BLOCK

    def kern(q_ref, kp_ref, kc_ref, vp_ref, vc_ref, bias_ref, sink_ref, o_ref, lse_ref):
        k2 = jnp.concatenate([kp_ref[...], kc_ref[...]], axis=0)
        v2 = jnp.concatenate([vp_ref[...], vc_ref[...]], axis=0)
        ksw, vsw = _swap_halves(k2), _swap_halves(v2)
        lane = lax.broadcasted_iota(jnp.int32, (blk, LANES), 1)
        lo = lane < HEAD_DIM
        lse_t = jnp.zeros((blk, LANES), F32)
        for pair in range(SWA_HEADS // 2):
            q2 = q_ref[:, pair * LANES:(pair + 1) * LANES]
            outs = []
            for a in range(2):
                h = 2 * pair + a
                qa = jnp.where(lo if a == 0 else ~lo, q2, jnp.zeros_like(q2)) * SCALE
                kx, vx = (k2, v2) if h in SWA_GROUPS[0] else (ksw, vsw)
                s = _dot_nt(qa, kx) + bias_ref[h]
                sink = sink_ref[h]
                m = jnp.maximum(jnp.max(s, axis=-1, keepdims=True), sink)
                e = jnp.exp(s - m)
                l = jnp.sum(e, axis=-1, keepdims=True) + jnp.exp(sink - m)
                pn = (e * (1.0 / l)).astype(BF16)
                outs.append(_dot(pn, vx))
                lse_t = jnp.where(lane == h, m + jnp.log(l), lse_t)
            o_ref[:, pair * LANES:(pair + 1) * LANES] = jnp.where(lo, outs[0], outs[1]).astype(BF16)
        lse_ref[...] = lse_t

    q, kp, kc, vp, vc = _swa_specs()
    return pl.pallas_call(
        kern, grid=(n_tok // blk,),
        in_specs=[q, kp, kc, vp, vc, SWA_BIAS_SPEC, pl.BlockSpec(memory_space=pltpu.SMEM)],
        out_specs=[pl.BlockSpec((blk, 4 * LANES), lambda n: (n, 0)), pl.BlockSpec((blk, LANES), lambda n: (n, 0))],
        out_shape=[jax.ShapeDtypeStruct((n_tok, 4 * LANES), BF16), jax.ShapeDtypeStruct((n_tok, LANES), F32)],
        name="swa_fwd", compiler_params=_params(1))(zm, zm, zm, zm, zm, _swa_bias_table(), sinks)


def _swa_bwd(zm, sinks, d_out, out, lse):
    n_tok = zm.shape[0]
    blk = SWA_BLOCK

    def kern(q_ref, kp_ref, kc_ref, vp_ref, vc_ref, bias_ref, do_ref, o_ref, lse_ref, sink_ref,
             dq_ref, dkp_ref, dkc_ref, dvp_ref, dvc_ref, dsk_ref):
        n = pl.program_id(0)

        @pl.when(n == 0)
        def _():
            dsk_ref[...] = jnp.zeros_like(dsk_ref)

        k2 = jnp.concatenate([kp_ref[...], kc_ref[...]], axis=0)
        v2 = jnp.concatenate([vp_ref[...], vc_ref[...]], axis=0)
        lane = lax.broadcasted_iota(jnp.int32, (blk, LANES), 1)
        lo = lane < HEAD_DIM
        lse_t = lse_ref[...]
        dqs, dkv = {}, []
        for heads, kx, vx in ((SWA_GROUPS[0], k2, v2), (SWA_GROUPS[1], _swap_halves(k2), _swap_halves(v2))):
            qa, s = _swa_scores(q_ref, kx, heads, lo, bias_ref)
            doa = _stack_heads(do_ref, heads, lo, True)
            lse_g = jnp.concatenate([lse_t[:, h:h + 1] for h in heads], axis=0)
            prob = jnp.exp(s - lse_g)
            dd = jnp.sum(doa.astype(F32) * _stack_heads(o_ref, heads, lo, False).astype(F32), axis=-1, keepdims=True)
            ds = (prob * (_dot_nt(doa, vx) - dd)).astype(BF16)
            sink_part = -jnp.exp(_per_head_column(lambda h: sink_ref[h], heads) - lse_g) * dd
            dq = _dot(ds, kx) * SCALE
            for r, h in enumerate(heads):
                dqs[h] = dq[r * blk:(r + 1) * blk]
                dsk_ref[h:h + 1, :] += jnp.broadcast_to(
                    jnp.sum(sink_part[r * blk:(r + 1) * blk], axis=0, keepdims=True), (1, LANES))
            dkv.append((_dot_tn(ds, qa), _dot_tn(prob.astype(BF16), doa)))
        for pair in range(SWA_HEADS // 2):
            dq_ref[:, pair * LANES:(pair + 1) * LANES] = jnp.where(lo, dqs[2 * pair], dqs[2 * pair + 1]).astype(BF16)
        dk = dkv[0][0] + pltpu.roll(dkv[1][0], HEAD_DIM, axis=1)
        dv = dkv[0][1] + pltpu.roll(dkv[1][1], HEAD_DIM, axis=1)
        dkp_ref[...] = dk[0:blk]
        dkc_ref[...] = dk[blk:2 * blk]
        dvp_ref[...] = dv[0:blk]
        dvc_ref[...] = dv[blk:2 * blk]

    q, kp, kc, vp, vc = _swa_specs()
    wide = pl.BlockSpec((blk, 4 * LANES), lambda n: (n, 0))
    narrow = pl.BlockSpec((blk, LANES), lambda n: (n, 0))
    part = jax.ShapeDtypeStruct((n_tok, LANES), F32)
    return pl.pallas_call(
        kern, grid=(n_tok // blk,),
        in_specs=[q, kp, kc, vp, vc, SWA_BIAS_SPEC, wide, wide, narrow, pl.BlockSpec(memory_space=pltpu.SMEM)],
        out_specs=[wide, narrow, narrow, narrow, narrow, pl.BlockSpec((8, LANES), lambda n: (0, 0))],
        out_shape=[jax.ShapeDtypeStruct((n_tok, 4 * LANES), BF16), part, part, part, part,
                   jax.ShapeDtypeStruct((8, LANES), F32)],
        name="swa_bwd", compiler_params=_params(1))(zm, zm, zm, zm, zm, _swa_bias_table(), d_out, out, lse, sinks)


def _my_pos():
    return lax.axis_index("x"), lax.axis_index("y"), lax.axis_index("c")


def _peer(k):
    x, y, c = _my_pos()
    px, py, pc = x ^ (k >> 2), y ^ ((k >> 1) & 1), c ^ (k & 1)
    return (px, py, pc), 4 * px + 2 * py + pc


def _gather_copies(x_refs, out_refs, send_sems, recv_sems, local_sems):
    x, y, c = _my_pos()
    my_id = 4 * x + 2 * y + c
    local = [pltpu.make_async_copy(x_refs[w], out_refs[w].at[my_id], local_sems.at[w]) for w in range(len(x_refs))]
    sends, arrivals = [], []
    for k in range(1, N_DEV):
        peer, peer_id = _peer(k)
        for w in range(len(x_refs)):
            sems = dict(send_sem=send_sems.at[7 * w + k - 1], recv_sem=recv_sems.at[7 * w + k - 1],
                        device_id=peer, device_id_type=MESH)
            sends.append(pltpu.make_async_remote_copy(src_ref=x_refs[w], dst_ref=out_refs[w].at[my_id], **sems))
            arrivals.append(pltpu.make_async_remote_copy(src_ref=x_refs[w], dst_ref=out_refs[w].at[peer_id], **sems))
    return local, sends, arrivals


def _scatter_copies(g_refs, part_refs, send_sems, recv_sems, local_sems):
    x, y, c = _my_pos()
    my_id = 4 * x + 2 * y + c
    local = [pltpu.make_async_copy(g_refs[w].at[my_id], part_refs[w].at[0], local_sems.at[w])
             for w in range(len(g_refs))]
    sends, arrivals = [], []
    for k in range(1, N_DEV):
        peer, peer_id = _peer(k)
        for w in range(len(g_refs)):
            sems = dict(send_sem=send_sems.at[7 * w + k - 1], recv_sem=recv_sems.at[7 * w + k - 1],
                        device_id=peer, device_id_type=MESH)
            sends.append(pltpu.make_async_remote_copy(src_ref=g_refs[w].at[peer_id], dst_ref=part_refs[w].at[k], **sems))
            arrivals.append(pltpu.make_async_remote_copy(src_ref=g_refs[w].at[my_id], dst_ref=part_refs[w].at[k], **sems))
    return local, sends, arrivals


def _start_copies(local, sends, arrivals):
    for cp in local + sends:
        cp.start()


def _finish_copies(local, sends, arrivals):
    for cp in arrivals:
        cp.wait_recv()
    for cp in sends:
        cp.wait_send()
    for cp in local:
        cp.wait()


def _exchange_scratch(n_arrays):
    return [pltpu.SemaphoreType.DMA((7 * n_arrays,)), pltpu.SemaphoreType.DMA((7 * n_arrays,)),
            pltpu.SemaphoreType.DMA((n_arrays,))]


class _Ride:
    def __init__(self, arrays, out_shape, copies):
        self.arrays, self.out_shape, self.copies = list(arrays), list(out_shape), copies
        any_spec = pl.BlockSpec(memory_space=pl.ANY)
        self.in_specs = [any_spec] * len(self.arrays)
        self.out_specs = [any_spec] * len(self.arrays)
        self.scratch = _exchange_scratch(len(self.arrays)) if self.arrays else []

    def specs(self):
        return self

    @staticmethod
    def _at(grid, last):
        hit = [pl.program_id(d) == (n - 1 if last else 0) for d, n in enumerate(grid)]
        return hit[0] if len(hit) == 1 else jnp.logical_and(*hit)

    def at_first_step(self, grid, in_refs, out_refs, sems):
        @pl.when(self._at(grid, False))
        def _():
            _start_copies(*self.copies(in_refs, out_refs, *sems))

    def at_last_step(self, grid, in_refs, out_refs, sems):
        @pl.when(self._at(grid, True))
        def _():
            _finish_copies(*self.copies(in_refs, out_refs, *sems))


_NO_RIDE = _Ride([], [], None)


def _gather_ride(shards):
    return _Ride(shards, [jax.ShapeDtypeStruct((N_DEV,) + s.shape, s.dtype) for s in shards], _gather_copies)


def _scatter_ride(grads):
    return _Ride(grads, [jax.ShapeDtypeStruct(g.shape, g.dtype) for g in grads], _scatter_copies)


Q_COL, K_COL, V_COL = 6, 10, 14


def _causal(t, tq, tk):
    row = lax.broadcasted_iota(jnp.int32, (tq, tk), 0)
    col = lax.broadcasted_iota(jnp.int32, (tq, tk), 1)
    return jnp.where(col <= row, t, NEG)


def _lane_tile(stat, width):
    return jnp.tile(stat, (1, width // LANES))


def _fox_steps(nq):
    steps = [(i2, j, 0 if j < 2 * i2 else 1 + j - 2 * i2) for i2 in range(nq // 2) for j in range(2 * i2 + 2)]
    return [np.asarray(col, np.int32) for col in zip(*steps)]


_SWEEPS = {0: [(0, False), (1, False)], 1: [(0, True), (1, False)], 2: [(1, True)]}


def _fox_dispatch(sweep, kind, dead_ref, head0, idx):
    dead0, dead1 = dead_ref[head0, idx] > 0.5, dead_ref[head0 + 1, idx] > 0.5
    live0, live1 = jnp.logical_not(dead0), jnp.logical_not(dead1)
    below = kind == 0
    pl.when(jnp.logical_and(below, jnp.logical_and(live0, live1)))(lambda: sweep(_SWEEPS[0], (0, 1)))
    pl.when(jnp.logical_and(below, jnp.logical_and(live0, dead1)))(lambda: sweep(_SWEEPS[0], (0,)))
    pl.when(jnp.logical_and(below, jnp.logical_and(dead0, live1)))(lambda: sweep(_SWEEPS[0], (1,)))
    pl.when(kind == 1)(lambda: sweep(_SWEEPS[1], (0, 1)))
    pl.when(kind == 2)(lambda: sweep(_SWEEPS[2], (0, 1)))


EXP_ZERO = 110.0
NORM_SLACK = 1.005


def _fox_dead_steps(nrm, c_pairs, tq):
    nq = nrm.shape[0] // 8
    norms = jnp.sqrt(nrm.reshape(nq, 8, LANES)[:, :2, :FOX_HEADS])
    qn, kn = norms[:, 0] * SCALE, norms[:, 1]
    cb = c_pairs.reshape(FOX_HEADS, nq, tq)
    c_max, c_min = jnp.max(cb, axis=-1).T, jnp.min(cb, axis=-1).T
    both = lambda t: jnp.max(t.reshape(nq // 2, 2, FOX_HEADS), axis=1)
    qn2, kn2, c_max2 = both(qn), both(kn), both(c_max)
    gap = qn2[:, None] * (kn[None] + kn2[:, None]) * NORM_SLACK + (c_max2[:, None] - c_min[None])
    below = jnp.arange(nq)[None, :] < 2 * jnp.arange(nq // 2)[:, None]
    dead = jnp.logical_and(gap < -EXP_ZERO, below[..., None])
    return dead.transpose(2, 0, 1).reshape(FOX_HEADS, -1).astype(F32)


def _fox_fwd(zm, c_pairs, dead, tq, ride=None):
    n_tok = zm.shape[0]
    nq = n_tok // tq
    ii, jj, kk = _fox_steps(nq)
    n_steps = len(ii)
    n_ride = len(ride.arrays) if ride else 0

    def kern(ii_ref, jj_ref, kk_ref, q_ref, k_ref, v_ref, ck_ref, dead_ref, *more):
        ride_in, (o_ref, ln_ref), ride_out = more[:n_ride], more[n_ride:n_ride + 2], more[n_ride + 2:2 * n_ride + 2]
        qs_ref, m_ref, l_ref, acc_ref = more[2 * n_ride + 2:2 * n_ride + 6]
        step = pl.program_id(1)
        j, kind = jj_ref[step], kk_ref[step]
        lo = lax.broadcasted_iota(jnp.int32, (2 * tq, LANES), 1) < HEAD_DIM
        if ride:
            ride.at_first_step((FOX_HEADS // 2, n_steps), ride_in, ride_out, more[2 * n_ride + 6:])

        @pl.when(j == 0)
        def _():
            q2 = q_ref[...]
            zq = jnp.zeros_like(q2)
            qs_ref[0] = jnp.where(lo, q2, zq) * SCALE
            qs_ref[1] = jnp.where(lo, zq, q2) * SCALE
            m_ref[...] = jnp.full(m_ref.shape, NEG, F32)
            l_ref[...] = jnp.zeros(l_ref.shape, F32)
            acc_ref[...] = jnp.zeros(acc_ref.shape, F32)

        def sweep(subs, heads):
            kv = k_ref[...]
            v_ones = jnp.concatenate([v_ref[...], jnp.ones((tq, LANES), BF16)], axis=1)
            for sub, diag in subs:
                rows = slice(sub * tq, (sub + 1) * tq)
                for a in heads:
                    t = _dot_nt(qs_ref[a, rows], kv) - ck_ref[a:a + 1, :]
                    if diag:
                        t = _causal(t, tq, tq)
                    m_old = m_ref[a, rows]
                    m_new = jnp.maximum(m_old, jnp.max(t, axis=-1, keepdims=True))
                    alpha = jnp.exp(m_old - m_new)
                    e = jnp.exp(t - _lane_tile(m_new, tq)).astype(BF16)
                    pv = _dot(e, v_ones)
                    acc_ref[a, rows] = alpha * acc_ref[a, rows] + pv[:, :LANES]
                    l_ref[a, rows] = alpha * l_ref[a, rows] + pv[:, LANES:]
                    m_ref[a, rows] = m_new

        _fox_dispatch(sweep, kind, dead_ref, 2 * pl.program_id(0), ii_ref[step] * nq + j)

        @pl.when(kind == 2)
        def _():
            o_ref[...] = jnp.where(lo, acc_ref[0] / l_ref[0], acc_ref[1] / l_ref[1]).astype(BF16)
            ln_ref[:, :LANES] = m_ref[0] + jnp.log(l_ref[0])
            ln_ref[:, LANES:] = m_ref[1] + jnp.log(l_ref[1])

        if ride:
            ride.at_last_step((FOX_HEADS // 2, n_steps), ride_in, ride_out, more[2 * n_ride + 6:])

    blk = (tq, LANES)
    by_i = lambda col: (lambda hp, s, ii, jj, kk: (ii[s], col + hp))
    by_j = lambda col: (lambda hp, s, ii, jj, kk: (jj[s], col + hp))
    extra = ride.specs() if ride else _NO_RIDE
    grid_spec = pltpu.PrefetchScalarGridSpec(
        num_scalar_prefetch=3, grid=(FOX_HEADS // 2, n_steps),
        in_specs=[pl.BlockSpec((2 * tq, LANES), by_i(Q_COL)), pl.BlockSpec(blk, by_j(K_COL)),
                  pl.BlockSpec(blk, by_j(V_COL)),
                  pl.BlockSpec((None, 2, tq), lambda hp, s, ii, jj, kk: (hp, 0, jj[s])),
                  pl.BlockSpec(memory_space=pltpu.SMEM)] + extra.in_specs,
        out_specs=[pl.BlockSpec((2 * tq, LANES), by_i(0)), pl.BlockSpec((2 * tq, 2 * LANES), by_i(0))] + extra.out_specs,
        scratch_shapes=[pltpu.VMEM((2, 2 * tq, LANES), BF16), pltpu.VMEM((2, 2 * tq, LANES), F32),
                        pltpu.VMEM((2, 2 * tq, LANES), F32), pltpu.VMEM((2, 2 * tq, LANES), F32)] + extra.scratch)
    return pl.pallas_call(
        kern, grid_spec=grid_spec,
        out_shape=[jax.ShapeDtypeStruct((n_tok, 4 * LANES), BF16),
                   jax.ShapeDtypeStruct((n_tok, FOX_HEADS * LANES), F32)] + extra.out_shape,
        name="fox_fwd", compiler_params=_params(2))(ii, jj, kk, zm, zm, zm, c_pairs, dead, *extra.arrays)


def _fox_bwd(zm, c_pairs, dead, d_out, lnorm, delta, tq, ride=None):
    n_tok = zm.shape[0]
    nq = n_tok // tq
    ii, jj, kk = _fox_steps(nq)
    n_steps = len(ii)
    n_ride = len(ride.arrays) if ride else 0

    def kern(ii_ref, jj_ref, kk_ref, q_ref, k_ref, v_ref, ck_ref, dead_ref, do_ref, ln_ref, dl_ref, *more):
        ride_in, ride_out = more[:n_ride], more[n_ride + 5:2 * n_ride + 5]
        dq_ref, dk_ref, dv_ref, cs_ref, rs_ref = more[n_ride:n_ride + 5]
        qs_ref, qo_ref, dos_ref, dq_acc = more[2 * n_ride + 5:2 * n_ride + 9]
        step = pl.program_id(1)
        j, kind = jj_ref[step], kk_ref[step]
        lo = lax.broadcasted_iota(jnp.int32, (2 * tq, LANES), 1) < HEAD_DIM
        if ride:
            ride.at_first_step((FOX_HEADS // 2, n_steps), ride_in, ride_out, more[2 * n_ride + 9:])

        @pl.when(step == 0)
        def _():
            dk_ref[...] = jnp.zeros_like(dk_ref)
            dv_ref[...] = jnp.zeros_like(dv_ref)
            cs_ref[...] = jnp.zeros_like(cs_ref)

        @pl.when(j == 0)
        def _():
            q2, do2 = q_ref[...], do_ref[...]
            zq = jnp.zeros_like(q2)
            ones = jnp.ones((2 * tq, LANES), BF16)
            for a in range(2):
                half = lo if a == 0 else ~lo
                qa = jnp.where(half, q2, zq) * SCALE
                qs_ref[a] = qa
                qo_ref[a] = jnp.concatenate([qa, ones], axis=1)
                dos_ref[a] = jnp.where(half, do2, zq)
            dq_acc[...] = jnp.zeros(dq_acc.shape, F32)

        def sweep(subs, heads):
            kv, vv = k_ref[...], v_ref[...]
            k_ones = jnp.concatenate([kv, jnp.ones((tq, LANES), BF16)], axis=1)
            dk, dv, sums = None, None, {}
            for sub, diag in subs:
                rows = slice(sub * tq, (sub + 1) * tq)
                for a in heads:
                    t = _dot_nt(qs_ref[a, rows], kv) - ck_ref[a:a + 1, :]
                    if diag:
                        t = _causal(t, tq, tq)
                    prob = jnp.exp(t - _lane_tile(ln_ref[rows, a * LANES:(a + 1) * LANES], tq))
                    dp = _dot_nt(dos_ref[a, rows], vv)
                    ds = (prob * (dp - _lane_tile(dl_ref[rows, a * LANES:(a + 1) * LANES], tq))).astype(BF16)
                    dq_acc[a, rows] += _dot(ds, k_ones)
                    dk_cs = _dot_tn(ds, qo_ref[a, rows])
                    dv_a = _dot_tn(prob.astype(BF16), dos_ref[a, rows])
                    dk = dk_cs[:, :LANES] if dk is None else dk + dk_cs[:, :LANES]
                    dv = dv_a if dv is None else dv + dv_a
                    sums[a] = dk_cs[:, LANES:] if a not in sums else sums[a] + dk_cs[:, LANES:]
            keys = pl.ds(pl.multiple_of(j * tq, tq), tq)
            dk_ref[keys, :] += dk
            cs_ref[keys, :] += jnp.where(lo[:tq], sums.get(0, 0.0), sums.get(1, 0.0))
            dv_ref[keys, :] += dv

        _fox_dispatch(sweep, kind, dead_ref, 2 * pl.program_id(0), ii_ref[step] * nq + j)

        @pl.when(kind == 2)
        def _():
            dq_ref[...] = jnp.where(lo, dq_acc[0, :, :LANES], dq_acc[1, :, :LANES]) * SCALE
            rs_ref[...] = jnp.where(lo, dq_acc[0, :, LANES:], dq_acc[1, :, LANES:])

        if ride:
            ride.at_last_step((FOX_HEADS // 2, n_steps), ride_in, ride_out, more[2 * n_ride + 9:])

    blk = (tq, LANES)
    by_i = lambda col: (lambda hp, s, ii, jj, kk: (ii[s], col + hp))
    by_j = lambda col: (lambda hp, s, ii, jj, kk: (jj[s], col + hp))
    resident = pl.BlockSpec((2 * tq, LANES), by_i(0))
    stat = pl.BlockSpec((2 * tq, 2 * LANES), by_i(0))
    whole = pl.BlockSpec((n_tok, LANES), lambda hp, s, ii, jj, kk: (0, hp))
    extra = ride.specs() if ride else _NO_RIDE
    grid_spec = pltpu.PrefetchScalarGridSpec(
        num_scalar_prefetch=3, grid=(FOX_HEADS // 2, n_steps),
        in_specs=[pl.BlockSpec((2 * tq, LANES), by_i(Q_COL)), pl.BlockSpec(blk, by_j(K_COL)),
                  pl.BlockSpec(blk, by_j(V_COL)),
                  pl.BlockSpec((None, 2, tq), lambda hp, s, ii, jj, kk: (hp, 0, jj[s])),
                  pl.BlockSpec(memory_space=pltpu.SMEM), resident, stat, stat] + extra.in_specs,
        out_specs=[resident, whole, whole, whole, resident] + extra.out_specs,
        scratch_shapes=[pltpu.VMEM((2, 2 * tq, LANES), BF16), pltpu.VMEM((2, 2 * tq, 2 * LANES), BF16),
                        pltpu.VMEM((2, 2 * tq, LANES), BF16), pltpu.VMEM((2, 2 * tq, 2 * LANES), F32)] + extra.scratch)
    wide = jax.ShapeDtypeStruct((n_tok, 4 * LANES), F32)
    return pl.pallas_call(
        kern, grid_spec=grid_spec, out_shape=[wide] * 5 + extra.out_shape, name="fox_bwd",
        compiler_params=_params(2, FOX_BWD_VMEM))(ii, jj, kk, zm, zm, zm, c_pairs, dead, d_out, lnorm, delta,
                                                  *extra.arrays)


def _all_gather(shards):
    n_w = len(shards)

    def kern(*refs):
        x_refs, out_refs = refs[:n_w], refs[n_w:2 * n_w]
        send_sems, recv_sems, local_sems = refs[2 * n_w:]
        x, y, c = _my_pos()
        me, sibling = (x, y, c), (x, y, 1 - c)
        chips = [(1 - x, y), (x, 1 - y), (1 - x, 1 - y)]

        def slot(w, px, py, pc):
            return out_refs[w].at[4 * px + 2 * py + pc]

        def copy(w, k, block, to, src=None):
            return pltpu.make_async_remote_copy(
                src_ref=slot(w, *block) if src is None else src, dst_ref=slot(w, *block),
                send_sem=send_sems.at[7 * w + k], recv_sem=recv_sems.at[7 * w + k], device_id=to, device_id_type=MESH)

        local, started = [], []
        for w in range(n_w):
            mine = pltpu.make_async_copy(x_refs[w], slot(w, *me), local_sems.at[w])
            mine.start()
            local.append(mine)
            first = [copy(w, 0, me, sibling, src=x_refs[w])]
            first += [copy(w, 1 + k, me, (*chip, c), src=x_refs[w]) for k, chip in enumerate(chips)]
            for cp in first:
                cp.start()
            started += first
        for k, chip in enumerate(chips):
            for w in range(n_w):
                copy(w, 1 + k, (*chip, c), me).wait_recv()
                passed = copy(w, 4 + k, (*chip, c), sibling)
                passed.start()
                started.append(passed)
        for w in range(n_w):
            copy(w, 0, sibling, me).wait_recv()
            for k, chip in enumerate(chips):
                copy(w, 4 + k, (*chip, 1 - c), me).wait_recv()
        for cp in started:
            cp.wait_send()
        for cp in local:
            cp.wait()

    any_spec = pl.BlockSpec(memory_space=pl.ANY)
    return pl.pallas_call(
        kern, out_shape=[jax.ShapeDtypeStruct((N_DEV,) + s.shape, s.dtype) for s in shards],
        in_specs=[any_spec] * n_w, out_specs=[any_spec] * n_w,
        scratch_shapes=[pltpu.SemaphoreType.DMA((7 * n_w,)), pltpu.SemaphoreType.DMA((7 * n_w,)),
                        pltpu.SemaphoreType.DMA((n_w,))],
        name="weight_all_gather")(*shards)


def _small_exchange(small):
    def kern(s_ref, sall_ref, *sems):
        copies = _gather_copies([s_ref], [sall_ref], *sems)
        _start_copies(*copies)
        _finish_copies(*copies)

    any_spec = pl.BlockSpec(memory_space=pl.ANY)
    return pl.pallas_call(
        kern, out_shape=jax.ShapeDtypeStruct((N_DEV,) + small.shape, small.dtype), in_specs=[any_spec],
        out_specs=any_spec, scratch_shapes=_exchange_scratch(1), name="small_grad_exchange")(small)


ADAMW_BLOCK_BYTES = 2 * 1024 * 1024


def _adamw(parts, w, m, v, name):
    n_parts, n_rows, n_cols = parts.shape
    limit = max(8, ADAMW_BLOCK_BYTES // (n_parts * n_cols * parts.dtype.itemsize))
    tr = max(t for t in range(8, n_rows + 1, 8) if n_rows % t == 0 and t <= limit)

    def kern(p_ref, w_ref, m_ref, v_ref, g_out, d_out, m_out, v_out):
        g = p_ref[0].astype(F32)
        for k in range(1, n_parts):
            g = g + p_ref[k].astype(F32)
        m_new = ADAM_B1 * m_ref[...] + (1.0 - ADAM_B1) * g
        v_new = ADAM_B2 * v_ref[...] + (1.0 - ADAM_B2) * jnp.square(g)
        m_hat = m_new / (1.0 - ADAM_B1 ** ADAM_STEP)
        v_hat = v_new / (1.0 - ADAM_B2 ** ADAM_STEP)
        g_out[...] = g
        d_out[...] = -ADAM_LR * (m_hat / (jnp.sqrt(v_hat) + ADAM_EPS) + ADAM_WD * w_ref[...])
        m_out[...] = m_new
        v_out[...] = v_new

    row = pl.BlockSpec((tr, n_cols), lambda i: (i, 0))
    out = jax.ShapeDtypeStruct((n_rows, n_cols), F32)
    return pl.pallas_call(
        kern, grid=(n_rows // tr,),
        in_specs=[pl.BlockSpec((n_parts, tr, n_cols), lambda i: (0, i, 0)), row, row, row],
        out_specs=[row, row, row, row], out_shape=[out, out, out, out], name=name,
        compiler_params=_params(1))(parts, w, m, v)


SHARDED = {
    "w_in": ((D_MODEL, D_IN), 1), "w_br_swa": ((512, D_MODEL), 1), "w_br_fox": ((512, D_MODEL), 1),
    "w_mix_out": ((D_MODEL, D_MODEL), 0), "w_ff1": ((D_MODEL, D_FF), 1), "w_ff2": ((D_FF, D_MODEL), 0),
    "w_ple_gate": ((D_MODEL, D_MODEL), 0), "w_ple_proj": ((PLE_DIM, D_MODEL), 1),
}
W_IN_SHARD = D_IN // N_DEV
W_IN_PAD = 640
SMALL = ("g_mix", "g_mlp", "g_ple", "g_final", "b_forget", "swa_sinks")
SMALL_COLS = 1024


def _wire_shard(name, a):
    a = a.reshape(a.shape[-2:])
    return jnp.pad(a, ((0, 0), (0, W_IN_PAD - W_IN_SHARD))) if name == "w_in" else a


def _from_wire(name, a):
    return (a[:, :W_IN_SHARD] if name == "w_in" else a)[None]


def _w_all_from_wire(stacked):
    w_in = jnp.concatenate([stacked[d][:, :W_IN_SHARD] for d in range(N_DEV)], axis=1)
    fpad = jnp.zeros((D_MODEL, N_FPAD - FOX_HEADS), stacked.dtype)
    return jnp.concatenate([w_in[:, :N_MAIN + FOX_HEADS], fpad, w_in[:, N_MAIN + FOX_HEADS:]], axis=1)


def _dw_in_to_wire(dw_all):
    dw_in = jnp.concatenate([dw_all[:, :N_MAIN + FOX_HEADS], dw_all[:, N_MAIN + N_FPAD:]], axis=1)
    pad = jnp.zeros((D_MODEL, W_IN_PAD - W_IN_SHARD), dw_all.dtype)
    return jnp.stack([jnp.concatenate([dw_in[:, d * W_IN_SHARD:(d + 1) * W_IN_SHARD], pad], axis=1)
                      for d in range(N_DEV)])


def _pack_small(vals, scalar=None):
    rows = [jnp.pad(vals[n].reshape(-1), (0, SMALL_COLS - vals[n].size)) for n in SMALL]
    if scalar is not None:
        rows.append(jnp.pad(scalar.reshape(1), (0, SMALL_COLS - 1)))
    rows += [jnp.zeros((SMALL_COLS,), F32)] * (8 - len(rows))
    return jnp.stack(rows)


def _unpack_small(slab, like):
    return {n: slab[r, :like[n].size].reshape(like[n].shape) for r, n in enumerate(SMALL)}


def _local_step(x, p, tgt, w, small, tm, tq, ts, late_shards=None):
    n_tok = x.shape[0]
    row = lambda v: v.reshape(1, -1)
    g_mix, g_mlp, g_ple, g_fin = row(small["g_mix"]), row(small["g_mlp"]), row(small["g_ple"]), row(small["g_final"])
    sinks = small["swa_sinks"].reshape(-1)
    b_col = small["b_forget"].reshape(FOX_HEADS, 1)

    assert tm == tq
    u1, zm, zfg, zf, nrm = _in_proj(x, g_mix, w["w_all"], tm)
    f_t = zf[:, :FOX_HEADS].T
    c_pairs = _decay_cumsum(f_t, b_col).reshape(FOX_HEADS // 2, 2, n_tok)
    attn_a, lse_a = _swa_fwd(zm, sinks)
    dead = _fox_dead_steps(nrm, c_pairs, tq)
    if late_shards is None:
        attn_b, ln_b = _fox_fwd(zm, c_pairs, dead, tq)
    else:
        attn_b, ln_b, *late = _fox_fwd(zm, c_pairs, dead, tq, _gather_ride(list(late_shards.values())))
        w = {**w, **_gathered_to_local(dict(zip(late_shards, late)))}
    ya, yb, mixed, h1, u2 = _mix_fwd(attn_a, attn_b, zfg, x, w["w_br_swa"], w["w_br_fox"], w["w_mix_out"], g_mlp, tm)
    a, r, h2 = _ffn_fwd(u2, h1, w["w_ff1"], w["w_ff2"], tm // 2)
    dh3, dlg, dpp, u3, loss_acc, dgf = _head_fwd_bwd(h2, p, tgt, g_ple, w["w_ple_gate"], w["w_ple_proj"], g_fin, tm)

    dh2, dh2b, da, dgp = _ffn_bwd_a(dlg, dh3, h2, a, w["w_ple_gate"], g_ple, w["w_ff2"], tm // 2)
    dh1, dh1b, dgl, dya, dyb, daa, dab, delta_b, dgm = _ffn_bwd_b(
        da, dh2, h1, ya, yb, zfg, attn_b, w["w_ff1"], g_mlp, w["w_mix_out"], w["w_br_swa"], w["w_br_fox"], tm // 2)
    dq_a, dkp, dkc, dvp, dvc, dsk = _swa_bwd(zm, sinks, daa, attn_a, lse_a)
    dw = {
        "w_br_swa": _matmul_tn(attn_a, dya, "dw_br_swa", ts, stack_cols=D_MODEL // N_DEV),
        "w_br_fox": _matmul_tn(attn_b, dyb, "dw_br_fox", ts, stack_cols=D_MODEL // N_DEV),
        "w_mix_out": _matmul_tn(mixed, dh1b, "dw_mix_out", ts),
        "w_ff1": _matmul_tn(u2, da, "dw_ff1", ts, stack_cols=D_FF // N_DEV),
        "w_ff2": _matmul_tn(r, dh2b, "dw_ff2", ts),
        "w_ple_gate": _matmul_tn(u3, dlg, "dw_ple_gate", ts),
        "w_ple_proj": _matmul_tn(p, dpp, "dw_ple_proj", ts, stack_cols=D_MODEL // N_DEV),
    }
    if late_shards is None:
        dq_b, dk_b, dv_b, cs, rs = _fox_bwd(zm, c_pairs, dead, dab, ln_b, delta_b, tq)
        late_parts = None
    else:
        wire = _local_to_wire(dw)
        dq_b, dk_b, dv_b, cs, rs, *parts = _fox_bwd(zm, c_pairs, dead, dab, ln_b, delta_b, tq,
                                                    _scatter_ride([wire[n] for n in late_shards]))
        late_parts = dict(zip(late_shards, parts))

    up = lambda t: jnp.concatenate([t[SWA_BLOCK:], jnp.zeros((SWA_BLOCK, LANES), F32)], axis=0)
    dk_a, dv_a = dkc + up(dkp), dvc + up(dvp)
    df_t, db = _decay_bwd(cs, rs, f_t, b_col)
    df = jnp.pad(df_t.T, ((0, 0), (0, N_FPAD - FOX_HEADS)))
    dz = jnp.concatenate([dq_a, dk_a.astype(BF16), dv_a.astype(BF16), dq_b.astype(BF16), dk_b.astype(BF16), dv_b.astype(BF16),
                          df.astype(BF16), dgl], axis=1)
    dw["w_all"] = _matmul_tn(u1, dz, "dw_in", ts)
    if late_shards is None:
        dx, dgx = _in_proj_bwd(dz, dh1, x, w["w_all"], g_mix, tm)
    else:
        dx, dgx, late_parts["w_in"] = _in_proj_bwd(dz, dh1, x, w["w_all"], g_mix, tm,
                                                   _scatter_ride([_dw_in_to_wire(dw["w_all"])]))
    dsmall = {"g_mix": dgx[0], "g_mlp": dgm[0], "g_ple": dgp[0], "g_final": dgf[0],
              "b_forget": db[:, 0], "swa_sinks": dsk[:, 0]}
    return loss_acc[0, 0], dx, dw, dsmall, late_parts


_ROWS = lambda t: t.reshape(-1, t.shape[-1])
_BY_ROWS = lambda t: t.reshape(N_DEV, t.shape[0] // N_DEV, t.shape[1])
_SAME = lambda t: t
LOCAL_LAYOUT = {
    "w_in": ("w_all", _w_all_from_wire, _dw_in_to_wire), "w_br_swa": ("w_br_swa", _SAME, _SAME),
    "w_br_fox": ("w_br_fox", _SAME, _SAME), "w_mix_out": ("w_mix_out", _ROWS, _BY_ROWS),
    "w_ff1": ("w_ff1", _SAME, _SAME), "w_ff2": ("w_ff2", _SAME, _BY_ROWS),
    "w_ple_gate": ("w_ple_gate", _ROWS, _BY_ROWS), "w_ple_proj": ("w_ple_proj", _SAME, _SAME),
}


def _gathered_to_local(g):
    return {LOCAL_LAYOUT[n][0]: LOCAL_LAYOUT[n][1](t) for n, t in g.items()}


def _local_to_wire(dw):
    names = {local: n for n, (local, _, _) in LOCAL_LAYOUT.items()}
    return {names[local]: LOCAL_LAYOUT[names[local]][2](t) for local, t in dw.items()}


def kernel(x, p, g_mix, w_in, b_forget, swa_sinks, w_br_swa, w_br_fox, w_mix_out, g_mlp, w_ff1, w_ff2, g_ple, w_ple_gate, w_ple_proj, g_final, loss_target, m_g_mix, m_w_in, m_b_forget, m_swa_sinks, m_w_br_swa, m_w_br_fox, m_w_mix_out, m_g_mlp, m_w_ff1, m_w_ff2, m_g_ple, m_w_ple_gate, m_w_ple_proj, m_g_final, v_g_mix, v_w_in, v_b_forget, v_swa_sinks, v_w_br_swa, v_w_br_fox, v_w_mix_out, v_g_mlp, v_w_ff1, v_w_ff2, v_g_ple, v_w_ple_gate, v_w_ple_proj, v_g_final):
    given = dict(g_mix=g_mix, w_in=w_in, b_forget=b_forget, swa_sinks=swa_sinks, w_br_swa=w_br_swa, w_br_fox=w_br_fox,
                 w_mix_out=w_mix_out, g_mlp=g_mlp, w_ff1=w_ff1, w_ff2=w_ff2, g_ple=g_ple, w_ple_gate=w_ple_gate,
                 w_ple_proj=w_ple_proj, g_final=g_final)
    mom = dict(g_mix=m_g_mix, w_in=m_w_in, b_forget=m_b_forget, swa_sinks=m_swa_sinks, w_br_swa=m_w_br_swa,
               w_br_fox=m_w_br_fox, w_mix_out=m_w_mix_out, g_mlp=m_g_mlp, w_ff1=m_w_ff1, w_ff2=m_w_ff2, g_ple=m_g_ple,
               w_ple_gate=m_w_ple_gate, w_ple_proj=m_w_ple_proj, g_final=m_g_final)
    vel = dict(g_mix=v_g_mix, w_in=v_w_in, b_forget=v_b_forget, swa_sinks=v_swa_sinks, w_br_swa=v_w_br_swa,
               w_br_fox=v_w_br_fox, w_mix_out=v_w_mix_out, g_mlp=v_g_mlp, w_ff1=v_w_ff1, w_ff2=v_w_ff2, g_ple=v_g_ple,
               w_ple_gate=v_w_ple_gate, w_ple_proj=v_w_ple_proj, g_final=v_g_final)
    names = list(given)
    sharded = list(SHARDED)

    w_wire = {n: _wire_shard(n, given[n]) for n in sharded}
    late = [n for n in sharded if n != "w_in"]
    gathered = _all_gather([w_wire["w_in"].astype(BF16)])
    local_w = _gathered_to_local({"w_in": gathered[0]})
    small = {n: given[n].reshape(-1) for n in SMALL}

    n_tok = x.shape[1]
    tile = min(512, n_tok // 4)
    loss_part, dx, dw, dsmall, parts = _local_step(
        x[0], p[0, 0], loss_target[0], local_w, small, tm=tile, tq=tile, ts=min(2048, n_tok // 4),
        late_shards={n: w_wire[n].astype(BF16) for n in late})
    small_all = _small_exchange(_pack_small(dsmall, loss_part))

    res = {}
    for n in sharded:
        part = parts[n]
        flat = part.reshape(N_DEV, -1, part.shape[-1])
        outs = _adamw(flat, w_wire[n], _wire_shard(n, mom[n]), _wire_shard(n, vel[n]), "adamw_" + n)
        res[n] = [_from_wire(n, o) for o in outs]
    outs_s = _adamw(small_all, _pack_small(small), _pack_small({n: mom[n] for n in SMALL}),
                    _pack_small({n: vel[n] for n in SMALL}), "adamw_small")
    small_res = [_unpack_small(o, given) for o in outs_s]
    loss = outs_s[0][len(SMALL), 0]

    groups = [[res[n][k] if n in res else small_res[k][n] for n in names] for k in range(4)]
    return (loss, dx[None], *groups[0], *groups[1], *groups[2], *groups[3])
```

```python
import numpy as np
import jax
import jax.numpy as jnp
from jax import lax
from jax.experimental import pallas as pl
from jax.experimental.pallas import tpu as pltpu

F32 = jnp.float32
BF16 = jnp.bfloat16

D_MODEL = 1024
HEAD_DIM = 64
SWA_HEADS = 8
FOX_HEADS = 8
CHUNK_SHIFT = 6
SWA_BLOCK = 128
WINDOW_CHUNKS = 2
D_FF = 4096
PLE_DIM = 256
RMS_EPS = 1e-6
N_MAIN = 2304
N_FPAD = 128
N_GATE = 2048
D_IN = N_MAIN + FOX_HEADS + N_GATE
SCALE = HEAD_DIM ** -0.5
NEG = -1e30

ADAM_LR = 0.001
ADAM_B1 = 0.9
ADAM_B2 = 0.999
ADAM_EPS = 1e-08
ADAM_WD = 0.01
ADAM_STEP = 10

N_DEV = 8
TOKEN_TILE = 512
DW_TOKENS_PER_STEP = 2048
LANES = 128
V7X_VMEM_BYTES = 64 * 1024 * 1024
VMEM_LIMIT = V7X_VMEM_BYTES * 3 // 4
FOX_BWD_VMEM = V7X_VMEM_BYTES * 7 // 8
MESH = pl.DeviceIdType.MESH

_NT = (((1,), (1,)), ((), ()))
_TN = (((0,), (0,)), ((), ()))


def _params(n_grid, vmem_limit=VMEM_LIMIT):
    return pltpu.CompilerParams(dimension_semantics=("arbitrary",) * n_grid, vmem_limit_bytes=vmem_limit)


def _chunks(n, step):
    return [(s, min(step, n - s)) for s in range(0, n, step)]


def _sigmoid(x):
    return 1.0 / (1.0 + jnp.exp(-x))


def _dot(a, b):
    return jnp.dot(a, b, preferred_element_type=F32)


def _dot_nt(a, b):
    return lax.dot_general(a, b, _NT, preferred_element_type=F32)


def _dot_tn(a, b):
    return lax.dot_general(a, b, _TN, preferred_element_type=F32)


def _lane_concat(stacked_ref):
    return jnp.concatenate([stacked_ref[d] for d in range(N_DEV)], axis=1)


def _rms(h):
    return lax.rsqrt(jnp.mean(h * h, axis=-1, keepdims=True) + RMS_EPS)


def _rms_bwd(h, g, du):
    rs = _rms(h)
    n = h * rs
    dn = du * g
    dh = rs * (dn - n * jnp.mean(dn * n, axis=-1, keepdims=True))
    return dh, jnp.sum(du * n, axis=0, keepdims=True)


def _acc_rows(ref, i, row):
    @pl.when(i == 0)
    def _():
        ref[...] = jnp.zeros_like(ref)
    ref[...] += jnp.broadcast_to(row, ref.shape)


def _row_call(body, name, n_rows, tm, row_ins, const_ins, row_outs, acc_outs, ride=None, tile_outs=()):
    row_outs = list(row_outs)
    n_ri, n_ci, n_ro, n_ao = len(row_ins), len(const_ins), len(row_outs) + len(tile_outs), len(acc_outs)
    extra = ride if ride else _NO_RIDE
    n_ride = len(extra.arrays)
    grid = (n_rows // tm,)

    def kern(*refs):
        i = pl.program_id(0)
        ins, refs = refs[:n_ri + n_ci], refs[n_ri + n_ci:]
        ride_in, refs = refs[:n_ride], refs[n_ride:]
        outs, refs = refs[:n_ro + n_ao], refs[n_ro + n_ao:]
        ride_out, sems = refs[:n_ride], refs[n_ride:]
        if ride:
            ride.at_first_step(grid, ride_in, ride_out, sems)
        body(i, ins[:n_ri], ins[n_ri:], outs[:n_ro], outs[n_ro:])
        if ride:
            ride.at_last_step(grid, ride_in, ride_out, sems)

    def whole(a):
        zeros = (0,) * a.ndim
        return pl.BlockSpec(a.shape, lambda i: zeros, pipeline_mode=pl.Buffered(1))

    in_specs = [pl.BlockSpec((tm, a.shape[1]), lambda i: (i, 0)) for a in row_ins]
    in_specs += [whole(a) for a in const_ins] + extra.in_specs
    out_specs = [pl.BlockSpec((tm, c), lambda i: (i, 0)) for c, _ in row_outs]
    out_specs += [pl.BlockSpec((8, c), lambda i: (i, 0)) for c in tile_outs]
    out_specs += [pl.BlockSpec((8, c), lambda i: (0, 0)) for c in acc_outs] + extra.out_specs
    out_shape = [jax.ShapeDtypeStruct((n_rows, c), dt) for c, dt in row_outs]
    out_shape += [jax.ShapeDtypeStruct((8 * grid[0], c), F32) for c in tile_outs]
    out_shape += [jax.ShapeDtypeStruct((8, c), F32) for c in acc_outs] + extra.out_shape
    return pl.pallas_call(kern, grid=grid, in_specs=in_specs, out_specs=out_specs, out_shape=out_shape,
                          scratch_shapes=extra.scratch, name=name,
                          compiler_params=_params(1))(*row_ins, *const_ins, *extra.arrays)


def _in_proj(x, g_mix, w_all, tm):
    def body(i, ins, consts, outs, accs):
        x_ref, = ins
        g_ref, w_ref = consts
        u_ref, zm_ref, zfg_ref, zf_ref, nrm_ref = outs
        xv = x_ref[...]
        u = ((xv * _rms(xv)) * g_ref[...]).astype(BF16)
        u_ref[...] = u
        for s, n in _chunks(N_MAIN, 768):
            zm_ref[:, s:s + n] = _dot(u, w_ref[:, s:s + n]).astype(BF16)
        for s, n in _chunks(N_FPAD + N_GATE, 512):
            zfg_ref[:, s:s + n] = _dot(u, w_ref[:, N_MAIN + s:N_MAIN + s + n])
        zf_ref[...] = zfg_ref[:, :N_FPAD]
        lane = lax.broadcasted_iota(jnp.int32, (4 * LANES, LANES), 0)
        head = lax.broadcasted_iota(jnp.int32, (4 * LANES, LANES), 1)
        pick = (lane // HEAD_DIM == head).astype(BF16)
        tq_, tk_ = (zm_ref[:, col * LANES:(col + 4) * LANES].astype(F32) for col in (Q_COL, K_COL))
        rows = [jnp.max(_dot((t * t).astype(BF16), pick), axis=0, keepdims=True) for t in (tq_, tk_)]
        rows.append(jnp.min(_dot((tq_ * tk_).astype(BF16), pick), axis=0, keepdims=True))
        nrm_ref[...] = jnp.concatenate(rows + [jnp.zeros((5, LANES), F32)], axis=0)

    *outs, nrm = _row_call(body, "in_proj", x.shape[0], tm, [x], [g_mix, w_all],
                           [(D_MODEL, BF16), (N_MAIN, BF16), (N_FPAD + N_GATE, F32), (N_FPAD, F32)], [],
                           tile_outs=[LANES])
    return (*outs, nrm)


def _mix_fwd(attn_a, attn_b, zfg, x, w_sa, w_fo, w_mo, g_mlp, tm):
    def body(i, ins, consts, outs, accs):
        aa_ref, ab_ref, zfg_ref, x_ref = ins
        wsa_ref, wfo_ref, wmo_ref, g_ref = consts
        ya_ref, yb_ref, mx_ref, h1_ref, u2_ref = outs
        ya = _dot(aa_ref[...], _lane_concat(wsa_ref))
        yb = _dot(ab_ref[...], _lane_concat(wfo_ref))
        g0 = _sigmoid(zfg_ref[:, N_FPAD:N_FPAD + D_MODEL])
        g1 = _sigmoid(zfg_ref[:, N_FPAD + D_MODEL:N_FPAD + 2 * D_MODEL])
        mixed = (g0 * ya + g1 * yb).astype(BF16)
        ya_ref[...] = ya.astype(BF16)
        yb_ref[...] = yb.astype(BF16)
        mx_ref[...] = mixed
        h1 = x_ref[...] + _dot(mixed, wmo_ref[...])
        h1_ref[...] = h1
        u2_ref[...] = ((h1 * _rms(h1)) * g_ref[...]).astype(BF16)

    return _row_call(body, "mix_fwd", x.shape[0], tm, [attn_a, attn_b, zfg, x], [w_sa, w_fo, w_mo, g_mlp],
                     [(D_MODEL, BF16), (D_MODEL, BF16), (D_MODEL, BF16), (D_MODEL, F32), (D_MODEL, BF16)], [])


def _ffn_fwd(u2, h1, w1s, w2s, tm):
    ch = D_FF // N_DEV

    def body(i, ins, consts, outs, accs):
        u_ref, h1_ref = ins
        w1_ref, w2_ref = consts
        a_ref, r_ref, h2_ref = outs
        u = u_ref[...]
        acc = h1_ref[...]
        for c in range(N_DEV):
            a = _dot(u, w1_ref[c])
            a_ref[:, c * ch:(c + 1) * ch] = a.astype(BF16)
            r = jnp.square(jnp.maximum(a, 0.0)).astype(BF16)
            r_ref[:, c * ch:(c + 1) * ch] = r
            acc = acc + _dot(r, w2_ref[c])
        h2_ref[...] = acc

    return _row_call(body, "ffn_fwd", u2.shape[0], tm, [u2, h1], [w1s, w2s],
                     [(D_FF, BF16), (D_FF, BF16), (D_MODEL, F32)], [])


def _head_fwd_bwd(h2, p, tgt, g_ple, w_pg, w_pp, g_fin, tm):
    def body(i, ins, consts, outs, accs):
        h2_ref, p_ref, t_ref = ins
        gp_ref, wpg_ref, wpp_ref, gf_ref = consts
        dh3_ref, dlg_ref, dpp_ref, u3_ref = outs
        loss_ref, dgf_ref = accs
        h2 = h2_ref[...]
        u3 = ((h2 * _rms(h2)) * gp_ref[...]).astype(BF16)
        u3_ref[...] = u3
        pg = _sigmoid(_dot(u3, wpg_ref[...]))
        pp = _dot(p_ref[...].astype(BF16), _lane_concat(wpp_ref))
        h3 = h2 + pg * pp
        rs3 = _rms(h3)
        n3 = h3 * rs3
        gf = gf_ref[...]
        err = n3 * gf - t_ref[...]
        row_loss = 0.5 * jnp.mean(err * err, axis=-1, keepdims=True)
        _acc_rows(loss_ref, i, jnp.broadcast_to(jnp.sum(row_loss, axis=0, keepdims=True), (1, LANES)))
        dy = err * (1.0 / D_MODEL)
        _acc_rows(dgf_ref, i, jnp.sum(dy * n3, axis=0, keepdims=True))
        dn = dy * gf
        dh3 = rs3 * (dn - n3 * jnp.mean(dn * n3, axis=-1, keepdims=True))
        dh3_ref[...] = dh3
        dpp_ref[...] = (dh3 * pg).astype(BF16)
        dlg_ref[...] = ((dh3 * pp) * pg * (1.0 - pg)).astype(BF16)

    return _row_call(body, "head_fwd_bwd", h2.shape[0], tm, [h2, p, tgt], [g_ple, w_pg, w_pp, g_fin],
                     [(D_MODEL, F32), (D_MODEL, BF16), (D_MODEL, BF16), (D_MODEL, BF16)], [LANES, D_MODEL])


def _ffn_bwd_a(dlg, dh3, h2, a, w_pg, g_ple, w2s, tm):
    ch = D_FF // N_DEV

    def body(i, ins, consts, outs, accs):
        dlg_ref, dh3_ref, h2_ref, a_ref = ins
        wpg_ref, gp_ref, w2_ref = consts
        dh2_ref, dh2b_ref, da_ref = outs
        dgp_ref, = accs
        du3 = _dot_nt(dlg_ref[...], wpg_ref[...])
        dh, dg = _rms_bwd(h2_ref[...], gp_ref[...], du3)
        _acc_rows(dgp_ref, i, dg)
        dh2 = dh3_ref[...] + dh
        dh2_ref[...] = dh2
        dh2b = dh2.astype(BF16)
        dh2b_ref[...] = dh2b
        for c in range(N_DEV):
            dr = _dot_nt(dh2b, w2_ref[c])
            av = a_ref[:, c * ch:(c + 1) * ch].astype(F32)
            da_ref[:, c * ch:(c + 1) * ch] = (dr * (2.0 * jnp.maximum(av, 0.0))).astype(BF16)

    return _row_call(body, "ffn_bwd_a", h2.shape[0], tm, [dlg, dh3, h2, a], [w_pg, g_ple, w2s],
                     [(D_MODEL, F32), (D_MODEL, BF16), (D_FF, BF16)], [D_MODEL])


def _ffn_bwd_b(da, dh2, h1, ya, yb, zfg, attn_b, w1s, g_mlp, w_mo, w_sa, w_fo, tm):
    ch = D_FF // N_DEV

    def body(i, ins, consts, outs, accs):
        da_ref, dh2_ref, h1_ref, ya_ref, yb_ref, zfg_ref, ob_ref = ins
        w1_ref, gm_ref, wmo_ref, wsa_ref, wfo_ref = consts
        dh1_ref, dh1b_ref, dgl_ref, dya_ref, dyb_ref, daa_ref, dab_ref, dl_ref = outs
        dgm_ref, = accs
        du2 = _dot_nt(da_ref[:, 0:ch], w1_ref[0])
        for c in range(1, N_DEV):
            du2 = du2 + _dot_nt(da_ref[:, c * ch:(c + 1) * ch], w1_ref[c])
        dh, dg = _rms_bwd(h1_ref[...], gm_ref[...], du2)
        _acc_rows(dgm_ref, i, dg)
        dh1 = dh2_ref[...] + dh
        dh1_ref[...] = dh1
        dh1b = dh1.astype(BF16)
        dh1b_ref[...] = dh1b
        dmx = _dot_nt(dh1b, wmo_ref[...])
        g0 = _sigmoid(zfg_ref[:, N_FPAD:N_FPAD + D_MODEL])
        g1 = _sigmoid(zfg_ref[:, N_FPAD + D_MODEL:N_FPAD + 2 * D_MODEL])
        dya = (dmx * g0).astype(BF16)
        dyb = (dmx * g1).astype(BF16)
        dya_ref[...] = dya
        dyb_ref[...] = dyb
        dgl_ref[:, 0:D_MODEL] = ((dmx * ya_ref[...].astype(F32)) * g0 * (1.0 - g0)).astype(BF16)
        dgl_ref[:, D_MODEL:2 * D_MODEL] = ((dmx * yb_ref[...].astype(F32)) * g1 * (1.0 - g1)).astype(BF16)
        daa_ref[...] = _dot_nt(dya, _lane_concat(wsa_ref)).astype(BF16)
        dab = _dot_nt(dyb, _lane_concat(wfo_ref)).astype(BF16)
        dab_ref[...] = dab
        half_in = lax.broadcasted_iota(jnp.int32, (LANES, 2 * LANES), 0) // HEAD_DIM
        half_out = lax.broadcasted_iota(jnp.int32, (LANES, 2 * LANES), 1) // LANES
        pick = (half_in == half_out).astype(BF16)
        for pair in range(FOX_HEADS // 2):
            cols = slice(pair * LANES, (pair + 1) * LANES)
            prod = dab[:, cols].astype(F32) * ob_ref[:, cols].astype(F32)
            hi = prod.astype(BF16)
            lo_part = (prod - hi.astype(F32)).astype(BF16)
            dl_ref[:, 2 * pair * LANES:(2 * pair + 2) * LANES] = _dot(hi, pick) + _dot(lo_part, pick)

    half = D_MODEL // 2
    return _row_call(body, "ffn_bwd_b", h1.shape[0], tm, [da, dh2, h1, ya, yb, zfg, attn_b],
                     [w1s, g_mlp, w_mo, w_sa, w_fo],
                     [(D_MODEL, F32), (D_MODEL, BF16), (N_GATE, BF16), (D_MODEL, BF16), (D_MODEL, BF16),
                      (half, BF16), (half, BF16), (FOX_HEADS * LANES, F32)], [D_MODEL])


def _in_proj_bwd(dz, dh1, x, w_all, g_mix, tm, ride=None):
    def body(i, ins, consts, outs, accs):
        dz_ref, dh1_ref, x_ref = ins
        w_ref, g_ref = consts
        dx_ref, = outs
        dgx_ref, = accs
        du1 = _dot_nt(dz_ref[...], w_ref[...])
        dh, dg = _rms_bwd(x_ref[...], g_ref[...], du1)
        _acc_rows(dgx_ref, i, dg)
        dx_ref[...] = dh1_ref[...] + dh

    return _row_call(body, "in_proj_bwd", x.shape[0], tm, [dz, dh1, x], [w_all, g_mix],
                     [(D_MODEL, F32)], [D_MODEL], ride)


def _matmul_tn(a, b, name, ts, stack_cols=0):
    n_rows, ka = a.shape
    n = b.shape[1]
    tk = min(ka, 1024)
    tn = 896 if n % 1024 else 1024
    n_stack = tn // stack_cols if stack_cols else 0
    assert ka % tk == 0 and n % tn == 0 and n_rows % ts == 0 and (not stack_cols or tk == ka)
    n_steps = n_rows // ts

    def kern(a_ref, b_ref, o_ref, acc_ref):
        s = pl.program_id(2)

        @pl.when(s == 0)
        def _():
            acc_ref[...] = jnp.zeros_like(acc_ref)
        acc_ref[...] += _dot_tn(a_ref[...].astype(BF16), b_ref[...])

        @pl.when(s == n_steps - 1)
        def _():
            if stack_cols:
                for c in range(n_stack):
                    o_ref[c] = acc_ref[:, c * stack_cols:(c + 1) * stack_cols].astype(BF16)
            else:
                o_ref[...] = acc_ref[...].astype(BF16)

    if stack_cols:
        out_spec = pl.BlockSpec((n_stack, tk, stack_cols), lambda i, j, s: (j, 0, 0))
        out_shape = jax.ShapeDtypeStruct((n // stack_cols, ka, stack_cols), BF16)
    else:
        out_spec = pl.BlockSpec((tk, tn), lambda i, j, s: (i, j))
        out_shape = jax.ShapeDtypeStruct((ka, n), BF16)
    return pl.pallas_call(
        kern, grid=(ka // tk, n // tn, n_steps),
        in_specs=[pl.BlockSpec((ts, tk), lambda i, j, s: (s, i)), pl.BlockSpec((ts, tn), lambda i, j, s: (s, j))],
        out_specs=out_spec, out_shape=out_shape, scratch_shapes=[pltpu.VMEM((tk, tn), F32)], name=name,
        compiler_params=_params(3))(a, b)


SCAN_CHUNK = 512


def _decay_cumsum(f_t, b_col):
    n_tok = f_t.shape[1]
    ch = min(SCAN_CHUNK, n_tok)

    def kern(f_ref, b_ref, c_ref):
        r = lax.broadcasted_iota(jnp.int32, (ch, ch), 0)
        c = lax.broadcasted_iota(jnp.int32, (ch, ch), 1)
        tri = (r <= c).astype(F32)
        carry = jnp.zeros((8, 1), F32)
        for k in range(n_tok // ch):
            xv = f_ref[:, k * ch:(k + 1) * ch] + b_ref[...]
            lf = jnp.minimum(xv, 0.0) - jnp.log(1.0 + jnp.exp(-jnp.abs(xv)))
            cs = jnp.dot(lf, tri, precision=lax.Precision.HIGHEST, preferred_element_type=F32) + carry
            c_ref[:, k * ch:(k + 1) * ch] = cs
            carry = cs[:, ch - 1:ch]

    return pl.pallas_call(kern, out_shape=jax.ShapeDtypeStruct((8, n_tok), F32), name="decay_cumsum",
                          compiler_params=_params(0))(f_t, b_col)


def _decay_bwd(cs, rs, f_t, b_col):
    n_tok = f_t.shape[1]
    ch = min(SCAN_CHUNK, n_tok)
    n_ch = n_tok // ch

    def kern(cs_ref, rs_ref, f_ref, b_ref, df_ref, db_ref, carry_ref):
        k = pl.program_id(0)

        @pl.when(k == 0)
        def _():
            carry_ref[...] = jnp.zeros_like(carry_ref)
            db_ref[...] = jnp.zeros_like(db_ref)

        r = lax.broadcasted_iota(jnp.int32, (ch, ch), 0)
        c = lax.broadcasted_iota(jnp.int32, (ch, ch), 1)
        tri = (r >= c).astype(F32)
        head = lax.broadcasted_iota(jnp.int32, (8, 4 * LANES), 0)
        lane = lax.broadcasted_iota(jnp.int32, (8, 4 * LANES), 1)
        pick = (lane == HEAD_DIM * head).astype(F32)
        dc = lax.dot_general(pick, rs_ref[...] - cs_ref[...], _NT, precision=lax.Precision.HIGHEST,
                             preferred_element_type=F32)
        rc = jnp.dot(dc, tri, precision=lax.Precision.HIGHEST, preferred_element_type=F32) + carry_ref[:, 0:1]
        carry_ref[...] = jnp.broadcast_to(rc[:, 0:1], carry_ref.shape)
        df = rc / (1.0 + jnp.exp(f_ref[...] + b_ref[...]))
        df_ref[...] = df
        db_ref[...] += jnp.broadcast_to(jnp.sum(df, axis=1, keepdims=True), db_ref.shape)

    back = lambda k: n_ch - 1 - k
    wide = pl.BlockSpec((ch, 4 * LANES), lambda k: (back(k), 0))
    row = pl.BlockSpec((8, ch), lambda k: (0, back(k)))
    return pl.pallas_call(
        kern, grid=(n_ch,),
        in_specs=[wide, wide, row, pl.BlockSpec((8, 1), lambda k: (0, 0))],
        out_specs=[row, pl.BlockSpec((8, LANES), lambda k: (0, 0))],
        out_shape=[jax.ShapeDtypeStruct((8, n_tok), F32), jax.ShapeDtypeStruct((8, LANES), F32)],
        scratch_shapes=[pltpu.VMEM((8, LANES), F32)], name="decay_bwd", compiler_params=_params(1))(cs, rs, f_t, b_col)


def _swa_bias_table():
    row = jnp.arange(SWA_BLOCK)[:, None] + SWA_BLOCK
    col = jnp.arange(2 * SWA_BLOCK)[None, :]
    cd = (row >> CHUNK_SHIFT) - (col >> CHUNK_SHIFT)
    band = (cd >= 0) & (cd <= WINDOW_CHUNKS)
    slopes = jnp.asarray([2.0 ** -(h + 1) for h in range(SWA_HEADS)], F32)
    bias = -slopes[:, None, None] * jnp.abs(row - col).astype(F32)[None]
    return jnp.stack([jnp.where(band & (col >= SWA_BLOCK), bias, NEG), jnp.where(band, bias, NEG)])


SWA_BIAS_SPEC = pl.BlockSpec((None, SWA_HEADS, SWA_BLOCK, 2 * SWA_BLOCK), lambda n: (jnp.minimum(n, 1), 0, 0, 0))


def _swap_halves(t):
    return pltpu.roll(t.astype(F32), HEAD_DIM, axis=1).astype(t.dtype)


def _swa_specs():
    blk = SWA_BLOCK
    q = pl.BlockSpec((blk, 4 * LANES), lambda n: (n, 0))
    kp = pl.BlockSpec((blk, LANES), lambda n: (jnp.maximum(n - 1, 0), 4))
    kc = pl.BlockSpec((blk, LANES), lambda n: (n, 4))
    vp = pl.BlockSpec((blk, LANES), lambda n: (jnp.maximum(n - 1, 0), 5))
    vc = pl.BlockSpec((blk, LANES), lambda n: (n, 5))
    return q, kp, kc, vp, vc


SWA_GROUPS = ([h for h in range(SWA_HEADS) if h % 2 == h // 4], [h for h in range(SWA_HEADS) if h % 2 != h // 4])


def _stack_heads(ref, heads, lo, mask_halves):
    tiles = []
    for h in heads:
        t = ref[:, (h // 2) * LANES:(h // 2 + 1) * LANES]
        tiles.append(jnp.where(lo if h % 2 == 0 else ~lo, t, jnp.zeros_like(t)) if mask_halves else t)
    return jnp.concatenate(tiles, axis=0)


def _per_head_column(values, heads):
    return jnp.concatenate([jnp.full((SWA_BLOCK, 1), values(h), F32) for h in heads], axis=0)


def _swa_scores(q_ref, kx, heads, lo, bias_ref):
    qa = _stack_heads(q_ref, heads, lo, True) * SCALE
    return qa, _dot_nt(qa, kx) + jnp.concatenate([bias_ref[h] for h in heads], axis=0)


def _swa_fwd(zm, sinks):
    n_tok = zm.shape[0]
    blk = SWA_BLOCK

    def kern(q_ref, kp_ref, kc_ref, vp_ref, vc_ref, bias_ref, sink_ref, o_ref, lse_ref):
        k2 = jnp.concatenate([kp_ref[...], kc_ref[...]], axis=0)
        v2 = jnp.concatenate([vp_ref[...], vc_ref[...]], axis=0)
        ksw, vsw = _swap_halves(k2), _swap_halves(v2)
        lane = lax.broadcasted_iota(jnp.int32, (blk, LANES), 1)
        lo = lane < HEAD_DIM
        lse_t = jnp.zeros((blk, LANES), F32)
        for pair in range(SWA_HEADS // 2):
            q2 = q_ref[:, pair * LANES:(pair + 1) * LANES]
            outs = []
            for a in range(2):
                h = 2 * pair + a
                qa = jnp.where(lo if a == 0 else ~lo, q2, jnp.zeros_like(q2)) * SCALE
                kx, vx = (k2, v2) if h in SWA_GROUPS[0] else (ksw, vsw)
                s = _dot_nt(qa, kx) + bias_ref[h]
                sink = sink_ref[h]
                m = jnp.maximum(jnp.max(s, axis=-1, keepdims=True), sink)
                e = jnp.exp(s - m)
                l = jnp.sum(e, axis=-1, keepdims=True) + jnp.exp(sink - m)
                pn = (e * (1.0 / l)).astype(BF16)
                outs.append(_dot(pn, vx))
                lse_t = jnp.where(lane == h, m + jnp.log(l), lse_t)
            o_ref[:, pair * LANES:(pair + 1) * LANES] = jnp.where(lo, outs[0], outs[1]).astype(BF16)
        lse_ref[...] = lse_t

    q, kp, kc, vp, vc = _swa_specs()
    return pl.pallas_call(
        kern, grid=(n_tok // blk,),
        in_specs=[q, kp, kc, vp, vc, SWA_BIAS_SPEC, pl.BlockSpec(memory_space=pltpu.SMEM)],
        out_specs=[pl.BlockSpec((blk, 4 * LANES), lambda n: (n, 0)), pl.BlockSpec((blk, LANES), lambda n: (n, 0))],
        out_shape=[jax.ShapeDtypeStruct((n_tok, 4 * LANES), BF16), jax.ShapeDtypeStruct((n_tok, LANES), F32)],
        name="swa_fwd", compiler_params=_params(1))(zm, zm, zm, zm, zm, _swa_bias_table(), sinks)


def _swa_bwd(zm, sinks, d_out, out, lse):
    n_tok = zm.shape[0]
    blk = SWA_BLOCK

    def kern(q_ref, kp_ref, kc_ref, vp_ref, vc_ref, bias_ref, do_ref, o_ref, lse_ref, sink_ref,
             dq_ref, dkp_ref, dkc_ref, dvp_ref, dvc_ref, dsk_ref):
        n = pl.program_id(0)

        @pl.when(n == 0)
        def _():
            dsk_ref[...] = jnp.zeros_like(dsk_ref)

        k2 = jnp.concatenate([kp_ref[...], kc_ref[...]], axis=0)
        v2 = jnp.concatenate([vp_ref[...], vc_ref[...]], axis=0)
        lane = lax.broadcasted_iota(jnp.int32, (blk, LANES), 1)
        lo = lane < HEAD_DIM
        lse_t = lse_ref[...]
        dqs, dkv = {}, []
        for heads, kx, vx in ((SWA_GROUPS[0], k2, v2), (SWA_GROUPS[1], _swap_halves(k2), _swap_halves(v2))):
            qa, s = _swa_scores(q_ref, kx, heads, lo, bias_ref)
            doa = _stack_heads(do_ref, heads, lo, True)
            lse_g = jnp.concatenate([lse_t[:, h:h + 1] for h in heads], axis=0)
            prob = jnp.exp(s - lse_g)
            dd = jnp.sum(doa.astype(F32) * _stack_heads(o_ref, heads, lo, False).astype(F32), axis=-1, keepdims=True)
            ds = (prob * (_dot_nt(doa, vx) - dd)).astype(BF16)
            sink_part = -jnp.exp(_per_head_column(lambda h: sink_ref[h], heads) - lse_g) * dd
            dq = _dot(ds, kx) * SCALE
            for r, h in enumerate(heads):
                dqs[h] = dq[r * blk:(r + 1) * blk]
                dsk_ref[h:h + 1, :] += jnp.broadcast_to(
                    jnp.sum(sink_part[r * blk:(r + 1) * blk], axis=0, keepdims=True), (1, LANES))
            dkv.append((_dot_tn(ds, qa), _dot_tn(prob.astype(BF16), doa)))
        for pair in range(SWA_HEADS // 2):
            dq_ref[:, pair * LANES:(pair + 1) * LANES] = jnp.where(lo, dqs[2 * pair], dqs[2 * pair + 1]).astype(BF16)
        dk = dkv[0][0] + pltpu.roll(dkv[1][0], HEAD_DIM, axis=1)
        dv = dkv[0][1] + pltpu.roll(dkv[1][1], HEAD_DIM, axis=1)
        dkp_ref[...] = dk[0:blk]
        dkc_ref[...] = dk[blk:2 * blk]
        dvp_ref[...] = dv[0:blk]
        dvc_ref[...] = dv[blk:2 * blk]

    q, kp, kc, vp, vc = _swa_specs()
    wide = pl.BlockSpec((blk, 4 * LANES), lambda n: (n, 0))
    narrow = pl.BlockSpec((blk, LANES), lambda n: (n, 0))
    part = jax.ShapeDtypeStruct((n_tok, LANES), F32)
    return pl.pallas_call(
        kern, grid=(n_tok // blk,),
        in_specs=[q, kp, kc, vp, vc, SWA_BIAS_SPEC, wide, wide, narrow, pl.BlockSpec(memory_space=pltpu.SMEM)],
        out_specs=[wide, narrow, narrow, narrow, narrow, pl.BlockSpec((8, LANES), lambda n: (0, 0))],
        out_shape=[jax.ShapeDtypeStruct((n_tok, 4 * LANES), BF16), part, part, part, part,
                   jax.ShapeDtypeStruct((8, LANES), F32)],
        name="swa_bwd", compiler_params=_params(1))(zm, zm, zm, zm, zm, _swa_bias_table(), d_out, out, lse, sinks)


def _my_pos():
    return lax.axis_index("x"), lax.axis_index("y"), lax.axis_index("c")


def _peer(k):
    x, y, c = _my_pos()
    px, py, pc = x ^ (k >> 2), y ^ ((k >> 1) & 1), c ^ (k & 1)
    return (px, py, pc), 4 * px + 2 * py + pc


def _gather_copies(x_refs, out_refs, send_sems, recv_sems, local_sems):
    x, y, c = _my_pos()
    my_id = 4 * x + 2 * y + c
    local = [pltpu.make_async_copy(x_refs[w], out_refs[w].at[my_id], local_sems.at[w]) for w in range(len(x_refs))]
    sends, arrivals = [], []
    for k in range(1, N_DEV):
        peer, peer_id = _peer(k)
        for w in range(len(x_refs)):
            sems = dict(send_sem=send_sems.at[7 * w + k - 1], recv_sem=recv_sems.at[7 * w + k - 1],
                        device_id=peer, device_id_type=MESH)
            sends.append(pltpu.make_async_remote_copy(src_ref=x_refs[w], dst_ref=out_refs[w].at[my_id], **sems))
            arrivals.append(pltpu.make_async_remote_copy(src_ref=x_refs[w], dst_ref=out_refs[w].at[peer_id], **sems))
    return local, sends, arrivals


def _scatter_copies(g_refs, part_refs, send_sems, recv_sems, local_sems):
    x, y, c = _my_pos()
    my_id = 4 * x + 2 * y + c
    local = [pltpu.make_async_copy(g_refs[w].at[my_id], part_refs[w].at[0], local_sems.at[w])
             for w in range(len(g_refs))]
    sends, arrivals = [], []
    for k in range(1, N_DEV):
        peer, peer_id = _peer(k)
        for w in range(len(g_refs)):
            sems = dict(send_sem=send_sems.at[7 * w + k - 1], recv_sem=recv_sems.at[7 * w + k - 1],
                        device_id=peer, device_id_type=MESH)
            sends.append(pltpu.make_async_remote_copy(src_ref=g_refs[w].at[peer_id], dst_ref=part_refs[w].at[k], **sems))
            arrivals.append(pltpu.make_async_remote_copy(src_ref=g_refs[w].at[my_id], dst_ref=part_refs[w].at[k], **sems))
    return local, sends, arrivals


def _start_copies(local, sends, arrivals):
    for cp in local + sends:
        cp.start()


def _finish_copies(local, sends, arrivals):
    for cp in arrivals:
        cp.wait_recv()
    for cp in sends:
        cp.wait_send()
    for cp in local:
        cp.wait()


def _exchange_scratch(n_arrays):
    return [pltpu.SemaphoreType.DMA((7 * n_arrays,)), pltpu.SemaphoreType.DMA((7 * n_arrays,)),
            pltpu.SemaphoreType.DMA((n_arrays,))]


class _Ride:
    def __init__(self, arrays, out_shape, copies):
        self.arrays, self.out_shape, self.copies = list(arrays), list(out_shape), copies
        any_spec = pl.BlockSpec(memory_space=pl.ANY)
        self.in_specs = [any_spec] * len(self.arrays)
        self.out_specs = [any_spec] * len(self.arrays)
        self.scratch = _exchange_scratch(len(self.arrays)) if self.arrays else []

    @staticmethod
    def _at(grid, last):
        hit = [pl.program_id(d) == (n - 1 if last else 0) for d, n in enumerate(grid)]
        return hit[0] if len(hit) == 1 else jnp.logical_and(*hit)

    def at_first_step(self, grid, in_refs, out_refs, sems):
        @pl.when(self._at(grid, False))
        def _():
            _start_copies(*self.copies(in_refs, out_refs, *sems))

    def at_last_step(self, grid, in_refs, out_refs, sems):
        @pl.when(self._at(grid, True))
        def _():
            _finish_copies(*self.copies(in_refs, out_refs, *sems))


_NO_RIDE = _Ride([], [], None)


def _gather_ride(shards):
    return _Ride(shards, [jax.ShapeDtypeStruct((N_DEV,) + s.shape, s.dtype) for s in shards], _gather_copies)


def _scatter_ride(grads):
    return _Ride(grads, [jax.ShapeDtypeStruct(g.shape, g.dtype) for g in grads], _scatter_copies)


Q_COL, K_COL, V_COL = 6, 10, 14


def _causal(t, tq, tk):
    row = lax.broadcasted_iota(jnp.int32, (tq, tk), 0)
    col = lax.broadcasted_iota(jnp.int32, (tq, tk), 1)
    return jnp.where(col <= row, t, NEG)


def _lane_tile(stat, width):
    return jnp.tile(stat, (1, width // LANES))


def _fox_steps(nq):
    steps = [(i2, j, 0 if j < 2 * i2 else 1 + j - 2 * i2) for i2 in range(nq // 2) for j in range(2 * i2 + 2)]
    return [np.asarray(col, np.int32) for col in zip(*steps)]


_SWEEPS = {0: [(0, False), (1, False)], 1: [(0, True), (1, False)], 2: [(1, True)]}


def _fox_dispatch(sweep, kind, dead_ref, head0, idx):
    dead0, dead1 = dead_ref[head0, idx] > 0.5, dead_ref[head0 + 1, idx] > 0.5
    live0, live1 = jnp.logical_not(dead0), jnp.logical_not(dead1)
    below = kind == 0
    pl.when(jnp.logical_and(below, jnp.logical_and(live0, live1)))(lambda: sweep(_SWEEPS[0], (0, 1)))
    pl.when(jnp.logical_and(below, jnp.logical_and(live0, dead1)))(lambda: sweep(_SWEEPS[0], (0,)))
    pl.when(jnp.logical_and(below, jnp.logical_and(dead0, live1)))(lambda: sweep(_SWEEPS[0], (1,)))
    pl.when(kind == 1)(lambda: sweep(_SWEEPS[1], (0, 1)))
    pl.when(kind == 2)(lambda: sweep(_SWEEPS[2], (0, 1)))


EXP_ZERO = 110.0
NORM_SLACK = 1.005


def _fox_dead_steps(nrm, c_pairs, tq):
    nq = nrm.shape[0] // 8
    stats = nrm.reshape(nq, 8, LANES)[:, :3, :FOX_HEADS]
    qn, kn, own = jnp.sqrt(stats[:, 0]) * SCALE, jnp.sqrt(stats[:, 1]), stats[:, 2] * SCALE
    cb = c_pairs.reshape(FOX_HEADS, nq, tq)
    c_max, c_min = jnp.max(cb, axis=-1).T, jnp.min(cb, axis=-1).T
    both = lambda t, pick: pick(t.reshape(nq // 2, 2, FOX_HEADS), axis=1)
    qn2, kn2, c_max2, own2 = both(qn, jnp.max), both(kn, jnp.max), both(c_max, jnp.max), both(own, jnp.min)
    row_max_floor = own2 - (NORM_SLACK - 1.0) * qn2 * kn2 - c_max2
    gap = qn2[:, None] * kn[None] * NORM_SLACK - c_min[None] - row_max_floor[:, None]
    below = jnp.arange(nq)[None, :] < 2 * jnp.arange(nq // 2)[:, None]
    dead = jnp.logical_and(gap < -EXP_ZERO, below[..., None])
    return dead.transpose(2, 0, 1).reshape(FOX_HEADS, -1).astype(F32)


def _fox_fwd(zm, c_pairs, dead, tq, ride=None):
    n_tok = zm.shape[0]
    nq = n_tok // tq
    ii, jj, kk = _fox_steps(nq)
    n_steps = len(ii)
    n_ride = len(ride.arrays) if ride else 0

    def kern(ii_ref, jj_ref, kk_ref, q_ref, k_ref, v_ref, ck_ref, dead_ref, *more):
        ride_in, (o_ref, ln_ref), ride_out = more[:n_ride], more[n_ride:n_ride + 2], more[n_ride + 2:2 * n_ride + 2]
        qs_ref, m_ref, l_ref, acc_ref = more[2 * n_ride + 2:2 * n_ride + 6]
        step = pl.program_id(1)
        j, kind = jj_ref[step], kk_ref[step]
        lo = lax.broadcasted_iota(jnp.int32, (2 * tq, LANES), 1) < HEAD_DIM
        if ride:
            ride.at_first_step((FOX_HEADS // 2, n_steps), ride_in, ride_out, more[2 * n_ride + 6:])

        @pl.when(j == 0)
        def _():
            q2 = q_ref[...]
            zq = jnp.zeros_like(q2)
            qs_ref[0] = jnp.where(lo, q2, zq) * SCALE
            qs_ref[1] = jnp.where(lo, zq, q2) * SCALE
            m_ref[...] = jnp.full(m_ref.shape, NEG, F32)
            l_ref[...] = jnp.zeros(l_ref.shape, F32)
            acc_ref[...] = jnp.zeros(acc_ref.shape, F32)

        def sweep(subs, heads):
            kv = k_ref[...]
            v_ones = jnp.concatenate([v_ref[...], jnp.ones((tq, LANES), BF16)], axis=1)
            for sub, diag in subs:
                rows = slice(sub * tq, (sub + 1) * tq)
                for a in heads:
                    t = _dot_nt(qs_ref[a, rows], kv) - ck_ref[a:a + 1, :]
                    if diag:
                        t = _causal(t, tq, tq)
                    m_old = m_ref[a, rows]
                    m_new = jnp.maximum(m_old, jnp.max(t, axis=-1, keepdims=True))
                    alpha = jnp.exp(m_old - m_new)
                    e = jnp.exp(t - _lane_tile(m_new, tq)).astype(BF16)
                    pv = _dot(e, v_ones)
                    acc_ref[a, rows] = alpha * acc_ref[a, rows] + pv[:, :LANES]
                    l_ref[a, rows] = alpha * l_ref[a, rows] + pv[:, LANES:]
                    m_ref[a, rows] = m_new

        _fox_dispatch(sweep, kind, dead_ref, 2 * pl.program_id(0), ii_ref[step] * nq + j)

        @pl.when(kind == 2)
        def _():
            o_ref[...] = jnp.where(lo, acc_ref[0] / l_ref[0], acc_ref[1] / l_ref[1]).astype(BF16)
            ln_ref[:, :LANES] = m_ref[0] + jnp.log(l_ref[0])
            ln_ref[:, LANES:] = m_ref[1] + jnp.log(l_ref[1])

        if ride:
            ride.at_last_step((FOX_HEADS // 2, n_steps), ride_in, ride_out, more[2 * n_ride + 6:])

    blk = (tq, LANES)
    by_i = lambda col: (lambda hp, s, ii, jj, kk: (ii[s], col + hp))
    by_j = lambda col: (lambda hp, s, ii, jj, kk: (jj[s], col + hp))
    extra = ride if ride else _NO_RIDE
    grid_spec = pltpu.PrefetchScalarGridSpec(
        num_scalar_prefetch=3, grid=(FOX_HEADS // 2, n_steps),
        in_specs=[pl.BlockSpec((2 * tq, LANES), by_i(Q_COL)), pl.BlockSpec(blk, by_j(K_COL)),
                  pl.BlockSpec(blk, by_j(V_COL)),
                  pl.BlockSpec((None, 2, tq), lambda hp, s, ii, jj, kk: (hp, 0, jj[s])),
                  pl.BlockSpec(memory_space=pltpu.SMEM)] + extra.in_specs,
        out_specs=[pl.BlockSpec((2 * tq, LANES), by_i(0)), pl.BlockSpec((2 * tq, 2 * LANES), by_i(0))] + extra.out_specs,
        scratch_shapes=[pltpu.VMEM((2, 2 * tq, LANES), BF16), pltpu.VMEM((2, 2 * tq, LANES), F32),
                        pltpu.VMEM((2, 2 * tq, LANES), F32), pltpu.VMEM((2, 2 * tq, LANES), F32)] + extra.scratch)
    return pl.pallas_call(
        kern, grid_spec=grid_spec,
        out_shape=[jax.ShapeDtypeStruct((n_tok, 4 * LANES), BF16),
                   jax.ShapeDtypeStruct((n_tok, FOX_HEADS * LANES), F32)] + extra.out_shape,
        name="fox_fwd", compiler_params=_params(2))(ii, jj, kk, zm, zm, zm, c_pairs, dead, *extra.arrays)


def _fox_bwd(zm, c_pairs, dead, d_out, lnorm, delta, tq, ride=None):
    n_tok = zm.shape[0]
    nq = n_tok // tq
    ii, jj, kk = _fox_steps(nq)
    n_steps = len(ii)
    n_ride = len(ride.arrays) if ride else 0

    def kern(ii_ref, jj_ref, kk_ref, q_ref, k_ref, v_ref, ck_ref, dead_ref, do_ref, ln_ref, dl_ref, *more):
        ride_in, ride_out = more[:n_ride], more[n_ride + 5:2 * n_ride + 5]
        dq_ref, dk_ref, dv_ref, cs_ref, rs_ref = more[n_ride:n_ride + 5]
        qs_ref, qo_ref, dos_ref, dq_acc = more[2 * n_ride + 5:2 * n_ride + 9]
        step = pl.program_id(1)
        j, kind = jj_ref[step], kk_ref[step]
        lo = lax.broadcasted_iota(jnp.int32, (2 * tq, LANES), 1) < HEAD_DIM
        if ride:
            ride.at_first_step((FOX_HEADS // 2, n_steps), ride_in, ride_out, more[2 * n_ride + 9:])

        @pl.when(step == 0)
        def _():
            dk_ref[...] = jnp.zeros_like(dk_ref)
            dv_ref[...] = jnp.zeros_like(dv_ref)
            cs_ref[...] = jnp.zeros_like(cs_ref)

        @pl.when(j == 0)
        def _():
            q2, do2 = q_ref[...], do_ref[...]
            zq = jnp.zeros_like(q2)
            ones = jnp.ones((2 * tq, LANES), BF16)
            for a in range(2):
                half = lo if a == 0 else ~lo
                qa = jnp.where(half, q2, zq) * SCALE
                qs_ref[a] = qa
                qo_ref[a] = jnp.concatenate([qa, ones], axis=1)
                dos_ref[a] = jnp.where(half, do2, zq)
            dq_acc[...] = jnp.zeros(dq_acc.shape, F32)

        def sweep(subs, heads):
            kv, vv = k_ref[...], v_ref[...]
            k_ones = jnp.concatenate([kv, jnp.ones((tq, LANES), BF16)], axis=1)
            dk, dv, sums = None, None, {}
            for sub, diag in subs:
                rows = slice(sub * tq, (sub + 1) * tq)
                for a in heads:
                    t = _dot_nt(qs_ref[a, rows], kv) - ck_ref[a:a + 1, :]
                    if diag:
                        t = _causal(t, tq, tq)
                    prob = jnp.exp(t - _lane_tile(ln_ref[rows, a * LANES:(a + 1) * LANES], tq))
                    dp = _dot_nt(dos_ref[a, rows], vv)
                    ds = (prob * (dp - _lane_tile(dl_ref[rows, a * LANES:(a + 1) * LANES], tq))).astype(BF16)
                    dq_acc[a, rows] += _dot(ds, k_ones)
                    dk_cs = _dot_tn(ds, qo_ref[a, rows])
                    dv_a = _dot_tn(prob.astype(BF16), dos_ref[a, rows])
                    dk = dk_cs[:, :LANES] if dk is None else dk + dk_cs[:, :LANES]
                    dv = dv_a if dv is None else dv + dv_a
                    sums[a] = dk_cs[:, LANES:] if a not in sums else sums[a] + dk_cs[:, LANES:]
            keys = pl.ds(pl.multiple_of(j * tq, tq), tq)
            dk_ref[keys, :] += dk
            cs_ref[keys, :] += jnp.where(lo[:tq], sums.get(0, 0.0), sums.get(1, 0.0))
            dv_ref[keys, :] += dv

        _fox_dispatch(sweep, kind, dead_ref, 2 * pl.program_id(0), ii_ref[step] * nq + j)

        @pl.when(kind == 2)
        def _():
            dq_ref[...] = jnp.where(lo, dq_acc[0, :, :LANES], dq_acc[1, :, :LANES]) * SCALE
            rs_ref[...] = jnp.where(lo, dq_acc[0, :, LANES:], dq_acc[1, :, LANES:])

        if ride:
            ride.at_last_step((FOX_HEADS // 2, n_steps), ride_in, ride_out, more[2 * n_ride + 9:])

    blk = (tq, LANES)
    by_i = lambda col: (lambda hp, s, ii, jj, kk: (ii[s], col + hp))
    by_j = lambda col: (lambda hp, s, ii, jj, kk: (jj[s], col + hp))
    resident = pl.BlockSpec((2 * tq, LANES), by_i(0))
    stat = pl.BlockSpec((2 * tq, 2 * LANES), by_i(0))
    whole = pl.BlockSpec((n_tok, LANES), lambda hp, s, ii, jj, kk: (0, hp))
    extra = ride if ride else _NO_RIDE
    grid_spec = pltpu.PrefetchScalarGridSpec(
        num_scalar_prefetch=3, grid=(FOX_HEADS // 2, n_steps),
        in_specs=[pl.BlockSpec((2 * tq, LANES), by_i(Q_COL)), pl.BlockSpec(blk, by_j(K_COL)),
                  pl.BlockSpec(blk, by_j(V_COL)),
                  pl.BlockSpec((None, 2, tq), lambda hp, s, ii, jj, kk: (hp, 0, jj[s])),
                  pl.BlockSpec(memory_space=pltpu.SMEM), resident, stat, stat] + extra.in_specs,
        out_specs=[resident, whole, whole, whole, resident] + extra.out_specs,
        scratch_shapes=[pltpu.VMEM((2, 2 * tq, LANES), BF16), pltpu.VMEM((2, 2 * tq, 2 * LANES), BF16),
                        pltpu.VMEM((2, 2 * tq, LANES), BF16), pltpu.VMEM((2, 2 * tq, 2 * LANES), F32)] + extra.scratch)
    wide = jax.ShapeDtypeStruct((n_tok, 4 * LANES), F32)
    return pl.pallas_call(
        kern, grid_spec=grid_spec, out_shape=[wide] * 5 + extra.out_shape, name="fox_bwd",
        compiler_params=_params(2, FOX_BWD_VMEM))(ii, jj, kk, zm, zm, zm, c_pairs, dead, d_out, lnorm, delta,
                                                  *extra.arrays)


def _all_gather(shards):
    n_w = len(shards)

    def kern(*refs):
        x_refs, out_refs = refs[:n_w], refs[n_w:2 * n_w]
        send_sems, recv_sems, local_sems = refs[2 * n_w:]
        x, y, c = _my_pos()
        me, sibling = (x, y, c), (x, y, 1 - c)
        chips = [(1 - x, y), (x, 1 - y), (1 - x, 1 - y)]

        def slot(w, px, py, pc):
            return out_refs[w].at[4 * px + 2 * py + pc]

        def copy(w, k, block, to, src=None):
            return pltpu.make_async_remote_copy(
                src_ref=slot(w, *block) if src is None else src, dst_ref=slot(w, *block),
                send_sem=send_sems.at[7 * w + k], recv_sem=recv_sems.at[7 * w + k], device_id=to, device_id_type=MESH)

        local, started = [], []
        for w in range(n_w):
            mine = pltpu.make_async_copy(x_refs[w], slot(w, *me), local_sems.at[w])
            mine.start()
            local.append(mine)
            first = [copy(w, 0, me, sibling, src=x_refs[w])]
            first += [copy(w, 1 + k, me, (*chip, c), src=x_refs[w]) for k, chip in enumerate(chips)]
            for cp in first:
                cp.start()
            started += first
        for k, chip in enumerate(chips):
            for w in range(n_w):
                copy(w, 1 + k, (*chip, c), me).wait_recv()
                passed = copy(w, 4 + k, (*chip, c), sibling)
                passed.start()
                started.append(passed)
        for w in range(n_w):
            copy(w, 0, sibling, me).wait_recv()
            for k, chip in enumerate(chips):
                copy(w, 4 + k, (*chip, 1 - c), me).wait_recv()
        for cp in started:
            cp.wait_send()
        for cp in local:
            cp.wait()

    any_spec = pl.BlockSpec(memory_space=pl.ANY)
    return pl.pallas_call(
        kern, out_shape=[jax.ShapeDtypeStruct((N_DEV,) + s.shape, s.dtype) for s in shards],
        in_specs=[any_spec] * n_w, out_specs=[any_spec] * n_w,
        scratch_shapes=[pltpu.SemaphoreType.DMA((7 * n_w,)), pltpu.SemaphoreType.DMA((7 * n_w,)),
                        pltpu.SemaphoreType.DMA((n_w,))],
        name="weight_all_gather")(*shards)


def _small_exchange(small):
    def kern(s_ref, sall_ref, *sems):
        copies = _gather_copies([s_ref], [sall_ref], *sems)
        _start_copies(*copies)
        _finish_copies(*copies)

    any_spec = pl.BlockSpec(memory_space=pl.ANY)
    return pl.pallas_call(
        kern, out_shape=jax.ShapeDtypeStruct((N_DEV,) + small.shape, small.dtype), in_specs=[any_spec],
        out_specs=any_spec, scratch_shapes=_exchange_scratch(1), name="small_grad_exchange")(small)


ADAMW_BLOCK_BYTES = 2 * 1024 * 1024


def _adamw(parts, w, m, v, name):
    n_parts, n_rows, n_cols = parts.shape
    limit = max(8, ADAMW_BLOCK_BYTES // (n_parts * n_cols * parts.dtype.itemsize))
    tr = max(t for t in range(8, n_rows + 1, 8) if n_rows % t == 0 and t <= limit)

    def kern(p_ref, w_ref, m_ref, v_ref, g_out, d_out, m_out, v_out):
        g = p_ref[0].astype(F32)
        for k in range(1, n_parts):
            g = g + p_ref[k].astype(F32)
        m_new = ADAM_B1 * m_ref[...] + (1.0 - ADAM_B1) * g
        v_new = ADAM_B2 * v_ref[...] + (1.0 - ADAM_B2) * jnp.square(g)
        m_hat = m_new / (1.0 - ADAM_B1 ** ADAM_STEP)
        v_hat = v_new / (1.0 - ADAM_B2 ** ADAM_STEP)
        g_out[...] = g
        d_out[...] = -ADAM_LR * (m_hat / (jnp.sqrt(v_hat) + ADAM_EPS) + ADAM_WD * w_ref[...])
        m_out[...] = m_new
        v_out[...] = v_new

    row = pl.BlockSpec((tr, n_cols), lambda i: (i, 0))
    out = jax.ShapeDtypeStruct((n_rows, n_cols), F32)
    return pl.pallas_call(
        kern, grid=(n_rows // tr,),
        in_specs=[pl.BlockSpec((n_parts, tr, n_cols), lambda i: (0, i, 0)), row, row, row],
        out_specs=[row, row, row, row], out_shape=[out, out, out, out], name=name,
        compiler_params=_params(1))(parts, w, m, v)


SHARDED = {
    "w_in": ((D_MODEL, D_IN), 1), "w_br_swa": ((512, D_MODEL), 1), "w_br_fox": ((512, D_MODEL), 1),
    "w_mix_out": ((D_MODEL, D_MODEL), 0), "w_ff1": ((D_MODEL, D_FF), 1), "w_ff2": ((D_FF, D_MODEL), 0),
    "w_ple_gate": ((D_MODEL, D_MODEL), 0), "w_ple_proj": ((PLE_DIM, D_MODEL), 1),
}
W_IN_SHARD = D_IN // N_DEV
W_IN_PAD = 640
SMALL = ("g_mix", "g_mlp", "g_ple", "g_final", "b_forget", "swa_sinks")
SMALL_COLS = 1024


def _wire_shard(name, a):
    a = a.reshape(a.shape[-2:])
    return jnp.pad(a, ((0, 0), (0, W_IN_PAD - W_IN_SHARD))) if name == "w_in" else a


def _from_wire(name, a):
    return (a[:, :W_IN_SHARD] if name == "w_in" else a)[None]


def _w_all_from_wire(stacked):
    w_in = jnp.concatenate([stacked[d][:, :W_IN_SHARD] for d in range(N_DEV)], axis=1)
    fpad = jnp.zeros((D_MODEL, N_FPAD - FOX_HEADS), stacked.dtype)
    return jnp.concatenate([w_in[:, :N_MAIN + FOX_HEADS], fpad, w_in[:, N_MAIN + FOX_HEADS:]], axis=1)


def _dw_in_to_wire(dw_all):
    dw_in = jnp.concatenate([dw_all[:, :N_MAIN + FOX_HEADS], dw_all[:, N_MAIN + N_FPAD:]], axis=1)
    pad = jnp.zeros((D_MODEL, W_IN_PAD - W_IN_SHARD), dw_all.dtype)
    return jnp.stack([jnp.concatenate([dw_in[:, d * W_IN_SHARD:(d + 1) * W_IN_SHARD], pad], axis=1)
                      for d in range(N_DEV)])


def _pack_small(vals, scalar=None):
    rows = [jnp.pad(vals[n].reshape(-1), (0, SMALL_COLS - vals[n].size)) for n in SMALL]
    if scalar is not None:
        rows.append(jnp.pad(scalar.reshape(1), (0, SMALL_COLS - 1)))
    rows += [jnp.zeros((SMALL_COLS,), F32)] * (8 - len(rows))
    return jnp.stack(rows)


def _unpack_small(slab, like):
    return {n: slab[r, :like[n].size].reshape(like[n].shape) for r, n in enumerate(SMALL)}


def _local_step(x, p, tgt, w, small, tm, tq, ts, late_shards=None):
    n_tok = x.shape[0]
    row = lambda v: v.reshape(1, -1)
    g_mix, g_mlp, g_ple, g_fin = row(small["g_mix"]), row(small["g_mlp"]), row(small["g_ple"]), row(small["g_final"])
    sinks = small["swa_sinks"].reshape(-1)
    b_col = small["b_forget"].reshape(FOX_HEADS, 1)

    assert tm == tq
    u1, zm, zfg, zf, nrm = _in_proj(x, g_mix, w["w_all"], tm)
    f_t = zf[:, :FOX_HEADS].T
    c_pairs = _decay_cumsum(f_t, b_col).reshape(FOX_HEADS // 2, 2, n_tok)
    attn_a, lse_a = _swa_fwd(zm, sinks)
    dead = _fox_dead_steps(nrm, c_pairs, tq)
    if late_shards is None:
        attn_b, ln_b = _fox_fwd(zm, c_pairs, dead, tq)
    else:
        attn_b, ln_b, *late = _fox_fwd(zm, c_pairs, dead, tq, _gather_ride(list(late_shards.values())))
        w = {**w, **_gathered_to_local(dict(zip(late_shards, late)))}
    ya, yb, mixed, h1, u2 = _mix_fwd(attn_a, attn_b, zfg, x, w["w_br_swa"], w["w_br_fox"], w["w_mix_out"], g_mlp, tm)
    a, r, h2 = _ffn_fwd(u2, h1, w["w_ff1"], w["w_ff2"], tm // 2)
    dh3, dlg, dpp, u3, loss_acc, dgf = _head_fwd_bwd(h2, p, tgt, g_ple, w["w_ple_gate"], w["w_ple_proj"], g_fin, tm)

    dh2, dh2b, da, dgp = _ffn_bwd_a(dlg, dh3, h2, a, w["w_ple_gate"], g_ple, w["w_ff2"], tm // 2)
    dh1, dh1b, dgl, dya, dyb, daa, dab, delta_b, dgm = _ffn_bwd_b(
        da, dh2, h1, ya, yb, zfg, attn_b, w["w_ff1"], g_mlp, w["w_mix_out"], w["w_br_swa"], w["w_br_fox"], tm // 2)
    dq_a, dkp, dkc, dvp, dvc, dsk = _swa_bwd(zm, sinks, daa, attn_a, lse_a)
    dw = {
        "w_br_swa": _matmul_tn(attn_a, dya, "dw_br_swa", ts, stack_cols=D_MODEL // N_DEV),
        "w_br_fox": _matmul_tn(attn_b, dyb, "dw_br_fox", ts, stack_cols=D_MODEL // N_DEV),
        "w_mix_out": _matmul_tn(mixed, dh1b, "dw_mix_out", ts),
        "w_ff1": _matmul_tn(u2, da, "dw_ff1", ts, stack_cols=D_FF // N_DEV),
        "w_ff2": _matmul_tn(r, dh2b, "dw_ff2", ts),
        "w_ple_gate": _matmul_tn(u3, dlg, "dw_ple_gate", ts),
        "w_ple_proj": _matmul_tn(p, dpp, "dw_ple_proj", ts, stack_cols=D_MODEL // N_DEV),
    }
    if late_shards is None:
        dq_b, dk_b, dv_b, cs, rs = _fox_bwd(zm, c_pairs, dead, dab, ln_b, delta_b, tq)
        late_parts = None
    else:
        wire = _local_to_wire(dw)
        dq_b, dk_b, dv_b, cs, rs, *parts = _fox_bwd(zm, c_pairs, dead, dab, ln_b, delta_b, tq,
                                                    _scatter_ride([wire[n] for n in late_shards]))
        late_parts = dict(zip(late_shards, parts))

    up = lambda t: jnp.concatenate([t[SWA_BLOCK:], jnp.zeros((SWA_BLOCK, LANES), F32)], axis=0)
    dk_a, dv_a = dkc + up(dkp), dvc + up(dvp)
    df_t, db = _decay_bwd(cs, rs, f_t, b_col)
    df = jnp.pad(df_t.T, ((0, 0), (0, N_FPAD - FOX_HEADS)))
    dz = jnp.concatenate([dq_a, dk_a.astype(BF16), dv_a.astype(BF16), dq_b.astype(BF16), dk_b.astype(BF16), dv_b.astype(BF16),
                          df.astype(BF16), dgl], axis=1)
    dw["w_all"] = _matmul_tn(u1, dz, "dw_in", ts)
    if late_shards is None:
        dx, dgx = _in_proj_bwd(dz, dh1, x, w["w_all"], g_mix, tm)
    else:
        dx, dgx, late_parts["w_in"] = _in_proj_bwd(dz, dh1, x, w["w_all"], g_mix, tm,
                                                   _scatter_ride([_dw_in_to_wire(dw["w_all"])]))
    dsmall = {"g_mix": dgx[0], "g_mlp": dgm[0], "g_ple": dgp[0], "g_final": dgf[0],
              "b_forget": db[:, 0], "swa_sinks": dsk[:, 0]}
    return loss_acc[0, 0], dx, dw, dsmall, late_parts


_ROWS = lambda t: t.reshape(-1, t.shape[-1])
_BY_ROWS = lambda t: t.reshape(N_DEV, t.shape[0] // N_DEV, t.shape[1])
_SAME = lambda t: t
LOCAL_LAYOUT = {
    "w_in": ("w_all", _w_all_from_wire, _dw_in_to_wire), "w_br_swa": ("w_br_swa", _SAME, _SAME),
    "w_br_fox": ("w_br_fox", _SAME, _SAME), "w_mix_out": ("w_mix_out", _ROWS, _BY_ROWS),
    "w_ff1": ("w_ff1", _SAME, _SAME), "w_ff2": ("w_ff2", _SAME, _BY_ROWS),
    "w_ple_gate": ("w_ple_gate", _ROWS, _BY_ROWS), "w_ple_proj": ("w_ple_proj", _SAME, _SAME),
}


def _gathered_to_local(g):
    return {LOCAL_LAYOUT[n][0]: LOCAL_LAYOUT[n][1](t) for n, t in g.items()}


def _local_to_wire(dw):
    names = {local: n for n, (local, _, _) in LOCAL_LAYOUT.items()}
    return {names[local]: LOCAL_LAYOUT[names[local]][2](t) for local, t in dw.items()}


def kernel(x, p, g_mix, w_in, b_forget, swa_sinks, w_br_swa, w_br_fox, w_mix_out, g_mlp, w_ff1, w_ff2, g_ple, w_ple_gate, w_ple_proj, g_final, loss_target, m_g_mix, m_w_in, m_b_forget, m_swa_sinks, m_w_br_swa, m_w_br_fox, m_w_mix_out, m_g_mlp, m_w_ff1, m_w_ff2, m_g_ple, m_w_ple_gate, m_w_ple_proj, m_g_final, v_g_mix, v_w_in, v_b_forget, v_swa_sinks, v_w_br_swa, v_w_br_fox, v_w_mix_out, v_g_mlp, v_w_ff1, v_w_ff2, v_g_ple, v_w_ple_gate, v_w_ple_proj, v_g_final):
    given = dict(g_mix=g_mix, w_in=w_in, b_forget=b_forget, swa_sinks=swa_sinks, w_br_swa=w_br_swa, w_br_fox=w_br_fox,
                 w_mix_out=w_mix_out, g_mlp=g_mlp, w_ff1=w_ff1, w_ff2=w_ff2, g_ple=g_ple, w_ple_gate=w_ple_gate,
                 w_ple_proj=w_ple_proj, g_final=g_final)
    mom = dict(g_mix=m_g_mix, w_in=m_w_in, b_forget=m_b_forget, swa_sinks=m_swa_sinks, w_br_swa=m_w_br_swa,
               w_br_fox=m_w_br_fox, w_mix_out=m_w_mix_out, g_mlp=m_g_mlp, w_ff1=m_w_ff1, w_ff2=m_w_ff2, g_ple=m_g_ple,
               w_ple_gate=m_w_ple_gate, w_ple_proj=m_w_ple_proj, g_final=m_g_final)
    vel = dict(g_mix=v_g_mix, w_in=v_w_in, b_forget=v_b_forget, swa_sinks=v_swa_sinks, w_br_swa=v_w_br_swa,
               w_br_fox=v_w_br_fox, w_mix_out=v_w_mix_out, g_mlp=v_g_mlp, w_ff1=v_w_ff1, w_ff2=v_w_ff2, g_ple=v_g_ple,
               w_ple_gate=v_w_ple_gate, w_ple_proj=v_w_ple_proj, g_final=v_g_final)
    names = list(given)
    sharded = list(SHARDED)

    w_wire = {n: _wire_shard(n, given[n]) for n in sharded}
    late = [n for n in sharded if n != "w_in"]
    gathered = _all_gather([w_wire["w_in"].astype(BF16)])
    local_w = _gathered_to_local({"w_in": gathered[0]})
    small = {n: given[n].reshape(-1) for n in SMALL}

    n_tok = x.shape[1]
    tile = min(TOKEN_TILE, n_tok // 4)
    loss_part, dx, dw, dsmall, parts = _local_step(
        x[0], p[0, 0], loss_target[0], local_w, small, tm=tile, tq=tile, ts=min(DW_TOKENS_PER_STEP, n_tok // 4),
        late_shards={n: w_wire[n].astype(BF16) for n in late})
    small_all = _small_exchange(_pack_small(dsmall, loss_part))

    res = {}
    for n in sharded:
        part = parts[n]
        flat = part.reshape(N_DEV, -1, part.shape[-1])
        outs = _adamw(flat, w_wire[n], _wire_shard(n, mom[n]), _wire_shard(n, vel[n]), "adamw_" + n)
        res[n] = [_from_wire(n, o) for o in outs]
    outs_s = _adamw(small_all, _pack_small(small), _pack_small({n: mom[n] for n in SMALL}),
                    _pack_small({n: vel[n] for n in SMALL}), "adamw_small")
    small_res = [_unpack_small(o, given) for o in outs_s]
    loss = outs_s[0][len(SMALL), 0]

    groups = [[res[n][k] if n in res else small_res[k][n] for n in names] for k in range(4)]
    return (loss, dx[None], *groups[0], *groups[1], *groups[2], *groups[3])
```

```python
import numpy as np
import jax
import jax.numpy as jnp
from jax import lax
from jax.experimental import pallas as pl
from jax.experimental.pallas import tpu as pltpu

F32 = jnp.float32
BF16 = jnp.bfloat16

D_MODEL = 1024
HEAD_DIM = 64
SWA_HEADS = 8
FOX_HEADS = 8
CHUNK_SHIFT = 6
SWA_BLOCK = 128
WINDOW_CHUNKS = 2
D_FF = 4096
PLE_DIM = 256
RMS_EPS = 1e-6
N_MAIN = 2304
N_FPAD = 128
N_GATE = 2048
D_IN = N_MAIN + FOX_HEADS + N_GATE
SCALE = HEAD_DIM ** -0.5
NEG = -1e30

ADAM_LR = 0.001
ADAM_B1 = 0.9
ADAM_B2 = 0.999
ADAM_EPS = 1e-08
ADAM_WD = 0.01
ADAM_STEP = 10

N_DEV = 8
TOKEN_TILE = 512
DW_TOKENS_PER_STEP = 2048
LANES = 128
V7X_VMEM_BYTES = 64 * 1024 * 1024
VMEM_LIMIT = V7X_VMEM_BYTES * 3 // 4
FOX_BWD_VMEM = V7X_VMEM_BYTES * 7 // 8
MESH = pl.DeviceIdType.MESH

_NT = (((1,), (1,)), ((), ()))
_TN = (((0,), (0,)), ((), ()))


def _params(n_grid, vmem_limit=VMEM_LIMIT):
    return pltpu.CompilerParams(dimension_semantics=("arbitrary",) * n_grid, vmem_limit_bytes=vmem_limit)


def _chunks(n, step):
    return [(s, min(step, n - s)) for s in range(0, n, step)]


def _sigmoid(x):
    return 1.0 / (1.0 + jnp.exp(-x))


def _dot(a, b):
    return jnp.dot(a, b, preferred_element_type=F32)


def _dot_nt(a, b):
    return lax.dot_general(a, b, _NT, preferred_element_type=F32)


def _dot_tn(a, b):
    return lax.dot_general(a, b, _TN, preferred_element_type=F32)


def _lane_concat(stacked_ref):
    return jnp.concatenate([stacked_ref[d] for d in range(N_DEV)], axis=1)


def _rms(h):
    return lax.rsqrt(jnp.mean(h * h, axis=-1, keepdims=True) + RMS_EPS)


def _rms_bwd(h, g, du):
    rs = _rms(h)
    n = h * rs
    dn = du * g
    dh = rs * (dn - n * jnp.mean(dn * n, axis=-1, keepdims=True))
    return dh, jnp.sum(du * n, axis=0, keepdims=True)


def _acc_rows(ref, i, row):
    @pl.when(i == 0)
    def _():
        ref[...] = jnp.zeros_like(ref)
    ref[...] += jnp.broadcast_to(row, ref.shape)


def _row_call(body, name, n_rows, tm, row_ins, const_ins, row_outs, acc_outs, ride=None, tile_outs=()):
    row_outs = list(row_outs)
    n_ri, n_ci, n_ro, n_ao = len(row_ins), len(const_ins), len(row_outs) + len(tile_outs), len(acc_outs)
    extra = ride if ride else _NO_RIDE
    n_ride = len(extra.arrays)
    grid = (n_rows // tm,)

    def kern(*refs):
        i = pl.program_id(0)
        ins, refs = refs[:n_ri + n_ci], refs[n_ri + n_ci:]
        ride_in, refs = refs[:n_ride], refs[n_ride:]
        outs, refs = refs[:n_ro + n_ao], refs[n_ro + n_ao:]
        ride_out, sems = refs[:n_ride], refs[n_ride:]
        if ride:
            ride.at_first_step(grid, ride_in, ride_out, sems)
        body(i, ins[:n_ri], ins[n_ri:], outs[:n_ro], outs[n_ro:])
        if ride:
            ride.at_last_step(grid, ride_in, ride_out, sems)

    def whole(a):
        zeros = (0,) * a.ndim
        return pl.BlockSpec(a.shape, lambda i: zeros, pipeline_mode=pl.Buffered(1))

    in_specs = [pl.BlockSpec((tm, a.shape[1]), lambda i: (i, 0)) for a in row_ins]
    in_specs += [whole(a) for a in const_ins] + extra.in_specs
    out_specs = [pl.BlockSpec((tm, c), lambda i: (i, 0)) for c, _ in row_outs]
    out_specs += [pl.BlockSpec((8, c), lambda i: (i, 0)) for c in tile_outs]
    out_specs += [pl.BlockSpec((8, c), lambda i: (0, 0)) for c in acc_outs] + extra.out_specs
    out_shape = [jax.ShapeDtypeStruct((n_rows, c), dt) for c, dt in row_outs]
    out_shape += [jax.ShapeDtypeStruct((8 * grid[0], c), F32) for c in tile_outs]
    out_shape += [jax.ShapeDtypeStruct((8, c), F32) for c in acc_outs] + extra.out_shape
    return pl.pallas_call(kern, grid=grid, in_specs=in_specs, out_specs=out_specs, out_shape=out_shape,
                          scratch_shapes=extra.scratch, name=name,
                          compiler_params=_params(1))(*row_ins, *const_ins, *extra.arrays)


def _in_proj(x, g_mix, w_all, tm):
    def body(i, ins, consts, outs, accs):
        x_ref, = ins
        g_ref, w_ref = consts
        u_ref, zm_ref, zfg_ref, zf_ref, nrm_ref = outs
        xv = x_ref[...]
        u = ((xv * _rms(xv)) * g_ref[...]).astype(BF16)
        u_ref[...] = u
        for s, n in _chunks(N_MAIN, 768):
            zm_ref[:, s:s + n] = _dot(u, w_ref[:, s:s + n]).astype(BF16)
        for s, n in _chunks(N_FPAD + N_GATE, 512):
            zfg_ref[:, s:s + n] = _dot(u, w_ref[:, N_MAIN + s:N_MAIN + s + n])
        zf_ref[...] = zfg_ref[:, :N_FPAD]
        lane = lax.broadcasted_iota(jnp.int32, (4 * LANES, LANES), 0)
        head = lax.broadcasted_iota(jnp.int32, (4 * LANES, LANES), 1)
        pick = (lane // HEAD_DIM == head).astype(BF16)
        tq_, tk_ = (zm_ref[:, col * LANES:(col + 4) * LANES].astype(F32) for col in (Q_COL, K_COL))
        rows = [jnp.max(_dot((t * t).astype(BF16), pick), axis=0, keepdims=True) for t in (tq_, tk_)]
        rows.append(jnp.min(_dot((tq_ * tk_).astype(BF16), pick), axis=0, keepdims=True))
        nrm_ref[...] = jnp.concatenate(rows + [jnp.zeros((5, LANES), F32)], axis=0)

    *outs, nrm = _row_call(body, "in_proj", x.shape[0], tm, [x], [g_mix, w_all],
                           [(D_MODEL, BF16), (N_MAIN, BF16), (N_FPAD + N_GATE, F32), (N_FPAD, F32)], [],
                           tile_outs=[LANES])
    return (*outs, nrm)


def _mix_ffn_fwd(attn_a, attn_b, zfg, x, w_sa, w_fo, w_mo, g_mlp, w1s, w2s, tm):
    ch = D_FF // N_DEV

    def body(i, ins, consts, outs, accs):
        aa_ref, ab_ref, zfg_ref, x_ref = ins
        wsa_ref, wfo_ref, wmo_ref, g_ref, w1_ref, w2_ref = consts
        ya_ref, yb_ref, mx_ref, h1_ref, u2_ref, a_ref, r_ref, h2_ref = outs
        ya = _dot(aa_ref[...], _lane_concat(wsa_ref))
        yb = _dot(ab_ref[...], _lane_concat(wfo_ref))
        g0 = _sigmoid(zfg_ref[:, N_FPAD:N_FPAD + D_MODEL])
        g1 = _sigmoid(zfg_ref[:, N_FPAD + D_MODEL:N_FPAD + 2 * D_MODEL])
        mixed = (g0 * ya + g1 * yb).astype(BF16)
        ya_ref[...] = ya.astype(BF16)
        yb_ref[...] = yb.astype(BF16)
        mx_ref[...] = mixed
        h1 = x_ref[...] + _dot(mixed, wmo_ref[...])
        h1_ref[...] = h1
        u = ((h1 * _rms(h1)) * g_ref[...]).astype(BF16)
        u2_ref[...] = u
        acc = h1
        for c in range(N_DEV):
            a = _dot(u, w1_ref[c])
            a_ref[:, c * ch:(c + 1) * ch] = a.astype(BF16)
            r = jnp.square(jnp.maximum(a, 0.0)).astype(BF16)
            r_ref[:, c * ch:(c + 1) * ch] = r
            acc = acc + _dot(r, w2_ref[c])
        h2_ref[...] = acc

    return _row_call(body, "mix_ffn_fwd", x.shape[0], tm, [attn_a, attn_b, zfg, x],
                     [w_sa, w_fo, w_mo, g_mlp, w1s, w2s],
                     [(D_MODEL, BF16), (D_MODEL, BF16), (D_MODEL, BF16), (D_MODEL, F32), (D_MODEL, BF16),
                      (D_FF, BF16), (D_FF, BF16), (D_MODEL, F32)], [])


def _head_fwd_bwd(h2, p, tgt, g_ple, w_pg, w_pp, g_fin, tm):
    def body(i, ins, consts, outs, accs):
        h2_ref, p_ref, t_ref = ins
        gp_ref, wpg_ref, wpp_ref, gf_ref = consts
        dh3_ref, dlg_ref, dpp_ref, u3_ref = outs
        loss_ref, dgf_ref = accs
        h2 = h2_ref[...]
        u3 = ((h2 * _rms(h2)) * gp_ref[...]).astype(BF16)
        u3_ref[...] = u3
        pg = _sigmoid(_dot(u3, wpg_ref[...]))
        pp = _dot(p_ref[...].astype(BF16), _lane_concat(wpp_ref))
        h3 = h2 + pg * pp
        rs3 = _rms(h3)
        n3 = h3 * rs3
        gf = gf_ref[...]
        err = n3 * gf - t_ref[...]
        row_loss = 0.5 * jnp.mean(err * err, axis=-1, keepdims=True)
        _acc_rows(loss_ref, i, jnp.broadcast_to(jnp.sum(row_loss, axis=0, keepdims=True), (1, LANES)))
        dy = err * (1.0 / D_MODEL)
        _acc_rows(dgf_ref, i, jnp.sum(dy * n3, axis=0, keepdims=True))
        dn = dy * gf
        dh3 = rs3 * (dn - n3 * jnp.mean(dn * n3, axis=-1, keepdims=True))
        dh3_ref[...] = dh3
        dpp_ref[...] = (dh3 * pg).astype(BF16)
        dlg_ref[...] = ((dh3 * pp) * pg * (1.0 - pg)).astype(BF16)

    return _row_call(body, "head_fwd_bwd", h2.shape[0], tm, [h2, p, tgt], [g_ple, w_pg, w_pp, g_fin],
                     [(D_MODEL, F32), (D_MODEL, BF16), (D_MODEL, BF16), (D_MODEL, BF16)], [LANES, D_MODEL])


def _ffn_bwd_a(dlg, dh3, h2, a, w_pg, g_ple, w2s, tm):
    ch = D_FF // N_DEV

    def body(i, ins, consts, outs, accs):
        dlg_ref, dh3_ref, h2_ref, a_ref = ins
        wpg_ref, gp_ref, w2_ref = consts
        dh2_ref, dh2b_ref, da_ref = outs
        dgp_ref, = accs
        du3 = _dot_nt(dlg_ref[...], wpg_ref[...])
        dh, dg = _rms_bwd(h2_ref[...], gp_ref[...], du3)
        _acc_rows(dgp_ref, i, dg)
        dh2 = dh3_ref[...] + dh
        dh2_ref[...] = dh2
        dh2b = dh2.astype(BF16)
        dh2b_ref[...] = dh2b
        for c in range(N_DEV):
            dr = _dot_nt(dh2b, w2_ref[c])
            av = a_ref[:, c * ch:(c + 1) * ch].astype(F32)
            da_ref[:, c * ch:(c + 1) * ch] = (dr * (2.0 * jnp.maximum(av, 0.0))).astype(BF16)

    return _row_call(body, "ffn_bwd_a", h2.shape[0], tm, [dlg, dh3, h2, a], [w_pg, g_ple, w2s],
                     [(D_MODEL, F32), (D_MODEL, BF16), (D_FF, BF16)], [D_MODEL])


def _ffn_bwd_b(da, dh2, h1, ya, yb, zfg, attn_b, w1s, g_mlp, w_mo, w_sa, w_fo, tm):
    ch = D_FF // N_DEV

    def body(i, ins, consts, outs, accs):
        da_ref, dh2_ref, h1_ref, ya_ref, yb_ref, zfg_ref, ob_ref = ins
        w1_ref, gm_ref, wmo_ref, wsa_ref, wfo_ref = consts
        dh1_ref, dh1b_ref, dgl_ref, dya_ref, dyb_ref, daa_ref, dab_ref, dl_ref = outs
        dgm_ref, = accs
        du2 = _dot_nt(da_ref[:, 0:ch], w1_ref[0])
        for c in range(1, N_DEV):
            du2 = du2 + _dot_nt(da_ref[:, c * ch:(c + 1) * ch], w1_ref[c])
        dh, dg = _rms_bwd(h1_ref[...], gm_ref[...], du2)
        _acc_rows(dgm_ref, i, dg)
        dh1 = dh2_ref[...] + dh
        dh1_ref[...] = dh1
        dh1b = dh1.astype(BF16)
        dh1b_ref[...] = dh1b
        dmx = _dot_nt(dh1b, wmo_ref[...])
        g0 = _sigmoid(zfg_ref[:, N_FPAD:N_FPAD + D_MODEL])
        g1 = _sigmoid(zfg_ref[:, N_FPAD + D_MODEL:N_FPAD + 2 * D_MODEL])
        dya = (dmx * g0).astype(BF16)
        dyb = (dmx * g1).astype(BF16)
        dya_ref[...] = dya
        dyb_ref[...] = dyb
        dgl_ref[:, 0:D_MODEL] = ((dmx * ya_ref[...].astype(F32)) * g0 * (1.0 - g0)).astype(BF16)
        dgl_ref[:, D_MODEL:2 * D_MODEL] = ((dmx * yb_ref[...].astype(F32)) * g1 * (1.0 - g1)).astype(BF16)
        daa_ref[...] = _dot_nt(dya, _lane_concat(wsa_ref)).astype(BF16)
        dab = _dot_nt(dyb, _lane_concat(wfo_ref)).astype(BF16)
        dab_ref[...] = dab
        half_in = lax.broadcasted_iota(jnp.int32, (LANES, 2 * LANES), 0) // HEAD_DIM
        half_out = lax.broadcasted_iota(jnp.int32, (LANES, 2 * LANES), 1) // LANES
        pick = (half_in == half_out).astype(BF16)
        for pair in range(FOX_HEADS // 2):
            cols = slice(pair * LANES, (pair + 1) * LANES)
            prod = dab[:, cols].astype(F32) * ob_ref[:, cols].astype(F32)
            hi = prod.astype(BF16)
            lo_part = (prod - hi.astype(F32)).astype(BF16)
            dl_ref[:, 2 * pair * LANES:(2 * pair + 2) * LANES] = _dot(hi, pick) + _dot(lo_part, pick)

    half = D_MODEL // 2
    return _row_call(body, "ffn_bwd_b", h1.shape[0], tm, [da, dh2, h1, ya, yb, zfg, attn_b],
                     [w1s, g_mlp, w_mo, w_sa, w_fo],
                     [(D_MODEL, F32), (D_MODEL, BF16), (N_GATE, BF16), (D_MODEL, BF16), (D_MODEL, BF16),
                      (half, BF16), (half, BF16), (FOX_HEADS * LANES, F32)], [D_MODEL])


def _in_proj_bwd(dz, dh1, x, w_all, g_mix, tm, ride=None):
    def body(i, ins, consts, outs, accs):
        dz_ref, dh1_ref, x_ref = ins
        w_ref, g_ref = consts
        dx_ref, = outs
        dgx_ref, = accs
        du1 = _dot_nt(dz_ref[...], w_ref[...])
        dh, dg = _rms_bwd(x_ref[...], g_ref[...], du1)
        _acc_rows(dgx_ref, i, dg)
        dx_ref[...] = dh1_ref[...] + dh

    return _row_call(body, "in_proj_bwd", x.shape[0], tm, [dz, dh1, x], [w_all, g_mix],
                     [(D_MODEL, F32)], [D_MODEL], ride)


def _matmul_tn(a, b, name, ts, stack_cols=0):
    n_rows, ka = a.shape
    n = b.shape[1]
    tk = min(ka, 1024)
    tn = 896 if n % 1024 else 1024
    n_stack = tn // stack_cols if stack_cols else 0
    assert ka % tk == 0 and n % tn == 0 and n_rows % ts == 0 and (not stack_cols or tk == ka)
    n_steps = n_rows // ts

    def kern(a_ref, b_ref, o_ref, acc_ref):
        s = pl.program_id(2)

        @pl.when(s == 0)
        def _():
            acc_ref[...] = jnp.zeros_like(acc_ref)
        acc_ref[...] += _dot_tn(a_ref[...].astype(BF16), b_ref[...])

        @pl.when(s == n_steps - 1)
        def _():
            if stack_cols:
                for c in range(n_stack):
                    o_ref[c] = acc_ref[:, c * stack_cols:(c + 1) * stack_cols].astype(BF16)
            else:
                o_ref[...] = acc_ref[...].astype(BF16)

    if stack_cols:
        out_spec = pl.BlockSpec((n_stack, tk, stack_cols), lambda i, j, s: (j, 0, 0))
        out_shape = jax.ShapeDtypeStruct((n // stack_cols, ka, stack_cols), BF16)
    else:
        out_spec = pl.BlockSpec((tk, tn), lambda i, j, s: (i, j))
        out_shape = jax.ShapeDtypeStruct((ka, n), BF16)
    return pl.pallas_call(
        kern, grid=(ka // tk, n // tn, n_steps),
        in_specs=[pl.BlockSpec((ts, tk), lambda i, j, s: (s, i)), pl.BlockSpec((ts, tn), lambda i, j, s: (s, j))],
        out_specs=out_spec, out_shape=out_shape, scratch_shapes=[pltpu.VMEM((tk, tn), F32)], name=name,
        compiler_params=_params(3))(a, b)


SCAN_CHUNK = 512


def _decay_cumsum(f_t, b_col):
    n_tok = f_t.shape[1]
    ch = min(SCAN_CHUNK, n_tok)

    def kern(f_ref, b_ref, c_ref):
        r = lax.broadcasted_iota(jnp.int32, (ch, ch), 0)
        c = lax.broadcasted_iota(jnp.int32, (ch, ch), 1)
        tri = (r <= c).astype(F32)
        carry = jnp.zeros((8, 1), F32)
        for k in range(n_tok // ch):
            xv = f_ref[:, k * ch:(k + 1) * ch] + b_ref[...]
            lf = jnp.minimum(xv, 0.0) - jnp.log(1.0 + jnp.exp(-jnp.abs(xv)))
            cs = jnp.dot(lf, tri, precision=lax.Precision.HIGHEST, preferred_element_type=F32) + carry
            c_ref[:, k * ch:(k + 1) * ch] = cs
            carry = cs[:, ch - 1:ch]

    return pl.pallas_call(kern, out_shape=jax.ShapeDtypeStruct((8, n_tok), F32), name="decay_cumsum",
                          compiler_params=_params(0))(f_t, b_col)


def _decay_bwd(cs, rs, f_t, b_col):
    n_tok = f_t.shape[1]
    ch = min(SCAN_CHUNK, n_tok)
    n_ch = n_tok // ch

    def kern(cs_ref, rs_ref, f_ref, b_ref, df_ref, db_ref, carry_ref):
        k = pl.program_id(0)

        @pl.when(k == 0)
        def _():
            carry_ref[...] = jnp.zeros_like(carry_ref)
            db_ref[...] = jnp.zeros_like(db_ref)

        r = lax.broadcasted_iota(jnp.int32, (ch, ch), 0)
        c = lax.broadcasted_iota(jnp.int32, (ch, ch), 1)
        tri = (r >= c).astype(F32)
        head = lax.broadcasted_iota(jnp.int32, (8, 4 * LANES), 0)
        lane = lax.broadcasted_iota(jnp.int32, (8, 4 * LANES), 1)
        pick = (lane == HEAD_DIM * head).astype(F32)
        dc = lax.dot_general(pick, rs_ref[...] - cs_ref[...], _NT, precision=lax.Precision.HIGHEST,
                             preferred_element_type=F32)
        rc = jnp.dot(dc, tri, precision=lax.Precision.HIGHEST, preferred_element_type=F32) + carry_ref[:, 0:1]
        carry_ref[...] = jnp.broadcast_to(rc[:, 0:1], carry_ref.shape)
        df = rc / (1.0 + jnp.exp(f_ref[...] + b_ref[...]))
        df_ref[...] = df
        db_ref[...] += jnp.broadcast_to(jnp.sum(df, axis=1, keepdims=True), db_ref.shape)

    back = lambda k: n_ch - 1 - k
    wide = pl.BlockSpec((ch, 4 * LANES), lambda k: (back(k), 0))
    row = pl.BlockSpec((8, ch), lambda k: (0, back(k)))
    return pl.pallas_call(
        kern, grid=(n_ch,),
        in_specs=[wide, wide, row, pl.BlockSpec((8, 1), lambda k: (0, 0))],
        out_specs=[row, pl.BlockSpec((8, LANES), lambda k: (0, 0))],
        out_shape=[jax.ShapeDtypeStruct((8, n_tok), F32), jax.ShapeDtypeStruct((8, LANES), F32)],
        scratch_shapes=[pltpu.VMEM((8, LANES), F32)], name="decay_bwd", compiler_params=_params(1))(cs, rs, f_t, b_col)


def _swa_bias_table():
    row = jnp.arange(SWA_BLOCK)[:, None] + SWA_BLOCK
    col = jnp.arange(2 * SWA_BLOCK)[None, :]
    cd = (row >> CHUNK_SHIFT) - (col >> CHUNK_SHIFT)
    band = (cd >= 0) & (cd <= WINDOW_CHUNKS)
    slopes = jnp.asarray([2.0 ** -(h + 1) for h in range(SWA_HEADS)], F32)
    bias = -slopes[:, None, None] * jnp.abs(row - col).astype(F32)[None]
    return jnp.stack([jnp.where(band & (col >= SWA_BLOCK), bias, NEG), jnp.where(band, bias, NEG)])


SWA_BIAS_SPEC = pl.BlockSpec((None, SWA_HEADS, SWA_BLOCK, 2 * SWA_BLOCK), lambda n: (jnp.minimum(n, 1), 0, 0, 0))


def _swap_halves(t):
    return pltpu.roll(t.astype(F32), HEAD_DIM, axis=1).astype(t.dtype)


def _swa_specs():
    blk = SWA_BLOCK
    q = pl.BlockSpec((blk, 4 * LANES), lambda n: (n, 0))
    kp = pl.BlockSpec((blk, LANES), lambda n: (jnp.maximum(n - 1, 0), 4))
    kc = pl.BlockSpec((blk, LANES), lambda n: (n, 4))
    vp = pl.BlockSpec((blk, LANES), lambda n: (jnp.maximum(n - 1, 0), 5))
    vc = pl.BlockSpec((blk, LANES), lambda n: (n, 5))
    return q, kp, kc, vp, vc


SWA_GROUPS = ([h for h in range(SWA_HEADS) if h % 2 == h // 4], [h for h in range(SWA_HEADS) if h % 2 != h // 4])


def _stack_heads(ref, heads, lo, mask_halves):
    tiles = []
    for h in heads:
        t = ref[:, (h // 2) * LANES:(h // 2 + 1) * LANES]
        tiles.append(jnp.where(lo if h % 2 == 0 else ~lo, t, jnp.zeros_like(t)) if mask_halves else t)
    return jnp.concatenate(tiles, axis=0)


def _per_head_column(values, heads):
    return jnp.concatenate([jnp.full((SWA_BLOCK, 1), values(h), F32) for h in heads], axis=0)


def _swa_scores(q_ref, kx, heads, lo, bias_ref):
    qa = _stack_heads(q_ref, heads, lo, True) * SCALE
    return qa, _dot_nt(qa, kx) + jnp.concatenate([bias_ref[h] for h in heads], axis=0)


def _swa_fwd(zm, sinks):
    n_tok = zm.shape[0]
    blk = SWA_BLOCK

    def kern(q_ref, kp_ref, kc_ref, vp_ref, vc_ref, bias_ref, sink_ref, o_ref, lse_ref):
        k2 = jnp.concatenate([kp_ref[...], kc_ref[...]], axis=0)
        v2 = jnp.concatenate([vp_ref[...], vc_ref[...]], axis=0)
        ksw, vsw = _swap_halves(k2), _swap_halves(v2)
        lane = lax.broadcasted_iota(jnp.int32, (blk, LANES), 1)
        lo = lane < HEAD_DIM
        lse_t = jnp.zeros((blk, LANES), F32)
        for pair in range(SWA_HEADS // 2):
            q2 = q_ref[:, pair * LANES:(pair + 1) * LANES]
            outs = []
            for a in range(2):
                h = 2 * pair + a
                qa = jnp.where(lo if a == 0 else ~lo, q2, jnp.zeros_like(q2)) * SCALE
                kx, vx = (k2, v2) if h in SWA_GROUPS[0] else (ksw, vsw)
                s = _dot_nt(qa, kx) + bias_ref[h]
                sink = sink_ref[h]
                m = jnp.maximum(jnp.max(s, axis=-1, keepdims=True), sink)
                e = jnp.exp(s - m)
                l = jnp.sum(e, axis=-1, keepdims=True) + jnp.exp(sink - m)
                pn = (e * (1.0 / l)).astype(BF16)
                outs.append(_dot(pn, vx))
                lse_t = jnp.where(lane == h, m + jnp.log(l), lse_t)
            o_ref[:, pair * LANES:(pair + 1) * LANES] = jnp.where(lo, outs[0], outs[1]).astype(BF16)
        lse_ref[...] = lse_t

    q, kp, kc, vp, vc = _swa_specs()
    return pl.pallas_call(
        kern, grid=(n_tok // blk,),
        in_specs=[q, kp, kc, vp, vc, SWA_BIAS_SPEC, pl.BlockSpec(memory_space=pltpu.SMEM)],
        out_specs=[pl.BlockSpec((blk, 4 * LANES), lambda n: (n, 0)), pl.BlockSpec((blk, LANES), lambda n: (n, 0))],
        out_shape=[jax.ShapeDtypeStruct((n_tok, 4 * LANES), BF16), jax.ShapeDtypeStruct((n_tok, LANES), F32)],
        name="swa_fwd", compiler_params=_params(1))(zm, zm, zm, zm, zm, _swa_bias_table(), sinks)


def _swa_bwd(zm, sinks, d_out, out, lse):
    n_tok = zm.shape[0]
    blk = SWA_BLOCK

    def kern(q_ref, kp_ref, kc_ref, vp_ref, vc_ref, bias_ref, do_ref, o_ref, lse_ref, sink_ref,
             dq_ref, dkp_ref, dkc_ref, dvp_ref, dvc_ref, dsk_ref):
        n = pl.program_id(0)

        @pl.when(n == 0)
        def _():
            dsk_ref[...] = jnp.zeros_like(dsk_ref)

        k2 = jnp.concatenate([kp_ref[...], kc_ref[...]], axis=0)
        v2 = jnp.concatenate([vp_ref[...], vc_ref[...]], axis=0)
        lane = lax.broadcasted_iota(jnp.int32, (blk, LANES), 1)
        lo = lane < HEAD_DIM
        lse_t = lse_ref[...]
        dqs, dkv = {}, []
        for heads, kx, vx in ((SWA_GROUPS[0], k2, v2), (SWA_GROUPS[1], _swap_halves(k2), _swap_halves(v2))):
            qa, s = _swa_scores(q_ref, kx, heads, lo, bias_ref)
            doa = _stack_heads(do_ref, heads, lo, True)
            lse_g = jnp.concatenate([lse_t[:, h:h + 1] for h in heads], axis=0)
            prob = jnp.exp(s - lse_g)
            dd = jnp.sum(doa.astype(F32) * _stack_heads(o_ref, heads, lo, False).astype(F32), axis=-1, keepdims=True)
            ds = (prob * (_dot_nt(doa, vx) - dd)).astype(BF16)
            sink_part = -jnp.exp(_per_head_column(lambda h: sink_ref[h], heads) - lse_g) * dd
            dq = _dot(ds, kx) * SCALE
            for r, h in enumerate(heads):
                dqs[h] = dq[r * blk:(r + 1) * blk]
                dsk_ref[h:h + 1, :] += jnp.broadcast_to(
                    jnp.sum(sink_part[r * blk:(r + 1) * blk], axis=0, keepdims=True), (1, LANES))
            dkv.append((_dot_tn(ds, qa), _dot_tn(prob.astype(BF16), doa)))
        for pair in range(SWA_HEADS // 2):
            dq_ref[:, pair * LANES:(pair + 1) * LANES] = jnp.where(lo, dqs[2 * pair], dqs[2 * pair + 1]).astype(BF16)
        dk = dkv[0][0] + pltpu.roll(dkv[1][0], HEAD_DIM, axis=1)
        dv = dkv[0][1] + pltpu.roll(dkv[1][1], HEAD_DIM, axis=1)
        dkp_ref[...] = dk[0:blk]
        dkc_ref[...] = dk[blk:2 * blk]
        dvp_ref[...] = dv[0:blk]
        dvc_ref[...] = dv[blk:2 * blk]

    q, kp, kc, vp, vc = _swa_specs()
    wide = pl.BlockSpec((blk, 4 * LANES), lambda n: (n, 0))
    narrow = pl.BlockSpec((blk, LANES), lambda n: (n, 0))
    part = jax.ShapeDtypeStruct((n_tok, LANES), F32)
    return pl.pallas_call(
        kern, grid=(n_tok // blk,),
        in_specs=[q, kp, kc, vp, vc, SWA_BIAS_SPEC, wide, wide, narrow, pl.BlockSpec(memory_space=pltpu.SMEM)],
        out_specs=[wide, narrow, narrow, narrow, narrow, pl.BlockSpec((8, LANES), lambda n: (0, 0))],
        out_shape=[jax.ShapeDtypeStruct((n_tok, 4 * LANES), BF16), part, part, part, part,
                   jax.ShapeDtypeStruct((8, LANES), F32)],
        name="swa_bwd", compiler_params=_params(1))(zm, zm, zm, zm, zm, _swa_bias_table(), d_out, out, lse, sinks)


def _my_pos():
    return lax.axis_index("x"), lax.axis_index("y"), lax.axis_index("c")


def _peer(k):
    x, y, c = _my_pos()
    px, py, pc = x ^ (k >> 2), y ^ ((k >> 1) & 1), c ^ (k & 1)
    return (px, py, pc), 4 * px + 2 * py + pc


def _gather_copies(x_refs, out_refs, send_sems, recv_sems, local_sems):
    x, y, c = _my_pos()
    my_id = 4 * x + 2 * y + c
    local = [pltpu.make_async_copy(x_refs[w], out_refs[w].at[my_id], local_sems.at[w]) for w in range(len(x_refs))]
    sends, arrivals = [], []
    for k in range(1, N_DEV):
        peer, peer_id = _peer(k)
        for w in range(len(x_refs)):
            sems = dict(send_sem=send_sems.at[7 * w + k - 1], recv_sem=recv_sems.at[7 * w + k - 1],
                        device_id=peer, device_id_type=MESH)
            sends.append(pltpu.make_async_remote_copy(src_ref=x_refs[w], dst_ref=out_refs[w].at[my_id], **sems))
            arrivals.append(pltpu.make_async_remote_copy(src_ref=x_refs[w], dst_ref=out_refs[w].at[peer_id], **sems))
    return local, sends, arrivals


def _scatter_copies(g_refs, part_refs, send_sems, recv_sems, local_sems):
    x, y, c = _my_pos()
    my_id = 4 * x + 2 * y + c
    local = [pltpu.make_async_copy(g_refs[w].at[my_id], part_refs[w].at[0], local_sems.at[w])
             for w in range(len(g_refs))]
    sends, arrivals = [], []
    for k in range(1, N_DEV):
        peer, peer_id = _peer(k)
        for w in range(len(g_refs)):
            sems = dict(send_sem=send_sems.at[7 * w + k - 1], recv_sem=recv_sems.at[7 * w + k - 1],
                        device_id=peer, device_id_type=MESH)
            sends.append(pltpu.make_async_remote_copy(src_ref=g_refs[w].at[peer_id], dst_ref=part_refs[w].at[k], **sems))
            arrivals.append(pltpu.make_async_remote_copy(src_ref=g_refs[w].at[my_id], dst_ref=part_refs[w].at[k], **sems))
    return local, sends, arrivals


def _start_copies(local, sends, arrivals):
    for cp in local + sends:
        cp.start()


def _finish_copies(local, sends, arrivals):
    for cp in arrivals:
        cp.wait_recv()
    for cp in sends:
        cp.wait_send()
    for cp in local:
        cp.wait()


def _exchange_scratch(n_arrays):
    return [pltpu.SemaphoreType.DMA((7 * n_arrays,)), pltpu.SemaphoreType.DMA((7 * n_arrays,)),
            pltpu.SemaphoreType.DMA((n_arrays,))]


class _Ride:
    def __init__(self, arrays, out_shape, copies):
        self.arrays, self.out_shape, self.copies = list(arrays), list(out_shape), copies
        any_spec = pl.BlockSpec(memory_space=pl.ANY)
        self.in_specs = [any_spec] * len(self.arrays)
        self.out_specs = [any_spec] * len(self.arrays)
        self.scratch = _exchange_scratch(len(self.arrays)) if self.arrays else []

    @staticmethod
    def _at(grid, last):
        hit = [pl.program_id(d) == (n - 1 if last else 0) for d, n in enumerate(grid)]
        return hit[0] if len(hit) == 1 else jnp.logical_and(*hit)

    def at_first_step(self, grid, in_refs, out_refs, sems):
        @pl.when(self._at(grid, False))
        def _():
            _start_copies(*self.copies(in_refs, out_refs, *sems))

    def at_last_step(self, grid, in_refs, out_refs, sems):
        @pl.when(self._at(grid, True))
        def _():
            _finish_copies(*self.copies(in_refs, out_refs, *sems))


_NO_RIDE = _Ride([], [], None)


def _gather_ride(shards):
    return _Ride(shards, [jax.ShapeDtypeStruct((N_DEV,) + s.shape, s.dtype) for s in shards], _gather_copies)


def _scatter_ride(grads):
    return _Ride(grads, [jax.ShapeDtypeStruct(g.shape, g.dtype) for g in grads], _scatter_copies)


Q_COL, K_COL, V_COL = 6, 10, 14


def _causal(t, tq, tk):
    row = lax.broadcasted_iota(jnp.int32, (tq, tk), 0)
    col = lax.broadcasted_iota(jnp.int32, (tq, tk), 1)
    return jnp.where(col <= row, t, NEG)


def _lane_tile(stat, width):
    return jnp.tile(stat, (1, width // LANES))


def _fox_steps(nq):
    steps = [(i2, j, 0 if j < 2 * i2 else 1 + j - 2 * i2) for i2 in range(nq // 2) for j in range(2 * i2 + 2)]
    return [np.asarray(col, np.int32) for col in zip(*steps)]


_SWEEPS = {0: [(0, False), (1, False)], 1: [(0, True), (1, False)], 2: [(1, True)]}


def _fox_dispatch(sweep, kind, dead_ref, head0, idx):
    dead0, dead1 = dead_ref[head0, idx] > 0.5, dead_ref[head0 + 1, idx] > 0.5
    live0, live1 = jnp.logical_not(dead0), jnp.logical_not(dead1)
    below = kind == 0
    pl.when(jnp.logical_and(below, jnp.logical_and(live0, live1)))(lambda: sweep(_SWEEPS[0], (0, 1)))
    pl.when(jnp.logical_and(below, jnp.logical_and(live0, dead1)))(lambda: sweep(_SWEEPS[0], (0,)))
    pl.when(jnp.logical_and(below, jnp.logical_and(dead0, live1)))(lambda: sweep(_SWEEPS[0], (1,)))
    pl.when(kind == 1)(lambda: sweep(_SWEEPS[1], (0, 1)))
    pl.when(kind == 2)(lambda: sweep(_SWEEPS[2], (0, 1)))


EXP_ZERO = 110.0
NORM_SLACK = 1.005


def _fox_dead_steps(nrm, c_pairs, tq):
    nq = nrm.shape[0] // 8
    stats = nrm.reshape(nq, 8, LANES)[:, :3, :FOX_HEADS]
    qn, kn, own = jnp.sqrt(stats[:, 0]) * SCALE, jnp.sqrt(stats[:, 1]), stats[:, 2] * SCALE
    cb = c_pairs.reshape(FOX_HEADS, nq, tq)
    c_max, c_min = jnp.max(cb, axis=-1).T, jnp.min(cb, axis=-1).T
    both = lambda t, pick: pick(t.reshape(nq // 2, 2, FOX_HEADS), axis=1)
    qn2, kn2, c_max2, own2 = both(qn, jnp.max), both(kn, jnp.max), both(c_max, jnp.max), both(own, jnp.min)
    row_max_floor = own2 - (NORM_SLACK - 1.0) * qn2 * kn2 - c_max2
    gap = qn2[:, None] * kn[None] * NORM_SLACK - c_min[None] - row_max_floor[:, None]
    below = jnp.arange(nq)[None, :] < 2 * jnp.arange(nq // 2)[:, None]
    dead = jnp.logical_and(gap < -EXP_ZERO, below[..., None])
    return dead.transpose(2, 0, 1).reshape(FOX_HEADS, -1).astype(F32)


def _fox_fwd(zm, c_pairs, dead, tq, ride=None):
    n_tok = zm.shape[0]
    nq = n_tok // tq
    ii, jj, kk = _fox_steps(nq)
    n_steps = len(ii)
    n_ride = len(ride.arrays) if ride else 0

    def kern(ii_ref, jj_ref, kk_ref, q_ref, k_ref, v_ref, ck_ref, dead_ref, *more):
        ride_in, (o_ref, ln_ref), ride_out = more[:n_ride], more[n_ride:n_ride + 2], more[n_ride + 2:2 * n_ride + 2]
        qs_ref, m_ref, l_ref, acc_ref = more[2 * n_ride + 2:2 * n_ride + 6]
        step = pl.program_id(1)
        j, kind = jj_ref[step], kk_ref[step]
        lo = lax.broadcasted_iota(jnp.int32, (2 * tq, LANES), 1) < HEAD_DIM
        if ride:
            ride.at_first_step((FOX_HEADS // 2, n_steps), ride_in, ride_out, more[2 * n_ride + 6:])

        @pl.when(j == 0)
        def _():
            q2 = q_ref[...]
            zq = jnp.zeros_like(q2)
            qs_ref[0] = jnp.where(lo, q2, zq) * SCALE
            qs_ref[1] = jnp.where(lo, zq, q2) * SCALE
            m_ref[...] = jnp.full(m_ref.shape, NEG, F32)
            l_ref[...] = jnp.zeros(l_ref.shape, F32)
            acc_ref[...] = jnp.zeros(acc_ref.shape, F32)

        def sweep(subs, heads):
            kv = k_ref[...]
            v_ones = jnp.concatenate([v_ref[...], jnp.ones((tq, LANES), BF16)], axis=1)
            for sub, diag in subs:
                rows = slice(sub * tq, (sub + 1) * tq)
                for a in heads:
                    t = _dot_nt(qs_ref[a, rows], kv) - ck_ref[a:a + 1, :]
                    if diag:
                        t = _causal(t, tq, tq)
                    m_old = m_ref[a, rows]
                    m_new = jnp.maximum(m_old, jnp.max(t, axis=-1, keepdims=True))
                    alpha = jnp.exp(m_old - m_new)
                    e = jnp.exp(t - _lane_tile(m_new, tq)).astype(BF16)
                    pv = _dot(e, v_ones)
                    acc_ref[a, rows] = alpha * acc_ref[a, rows] + pv[:, :LANES]
                    l_ref[a, rows] = alpha * l_ref[a, rows] + pv[:, LANES:]
                    m_ref[a, rows] = m_new

        _fox_dispatch(sweep, kind, dead_ref, 2 * pl.program_id(0), ii_ref[step] * nq + j)

        @pl.when(kind == 2)
        def _():
            o_ref[...] = jnp.where(lo, acc_ref[0] / l_ref[0], acc_ref[1] / l_ref[1]).astype(BF16)
            ln_ref[:, :LANES] = m_ref[0] + jnp.log(l_ref[0])
            ln_ref[:, LANES:] = m_ref[1] + jnp.log(l_ref[1])

        if ride:
            ride.at_last_step((FOX_HEADS // 2, n_steps), ride_in, ride_out, more[2 * n_ride + 6:])

    blk = (tq, LANES)
    by_i = lambda col: (lambda hp, s, ii, jj, kk: (ii[s], col + hp))
    by_j = lambda col: (lambda hp, s, ii, jj, kk: (jj[s], col + hp))
    extra = ride if ride else _NO_RIDE
    grid_spec = pltpu.PrefetchScalarGridSpec(
        num_scalar_prefetch=3, grid=(FOX_HEADS // 2, n_steps),
        in_specs=[pl.BlockSpec((2 * tq, LANES), by_i(Q_COL)), pl.BlockSpec(blk, by_j(K_COL)),
                  pl.BlockSpec(blk, by_j(V_COL)),
                  pl.BlockSpec((None, 2, tq), lambda hp, s, ii, jj, kk: (hp, 0, jj[s])),
                  pl.BlockSpec(memory_space=pltpu.SMEM)] + extra.in_specs,
        out_specs=[pl.BlockSpec((2 * tq, LANES), by_i(0)), pl.BlockSpec((2 * tq, 2 * LANES), by_i(0))] + extra.out_specs,
        scratch_shapes=[pltpu.VMEM((2, 2 * tq, LANES), BF16), pltpu.VMEM((2, 2 * tq, LANES), F32),
                        pltpu.VMEM((2, 2 * tq, LANES), F32), pltpu.VMEM((2, 2 * tq, LANES), F32)] + extra.scratch)
    return pl.pallas_call(
        kern, grid_spec=grid_spec,
        out_shape=[jax.ShapeDtypeStruct((n_tok, 4 * LANES), BF16),
                   jax.ShapeDtypeStruct((n_tok, FOX_HEADS * LANES), F32)] + extra.out_shape,
        name="fox_fwd", compiler_params=_params(2))(ii, jj, kk, zm, zm, zm, c_pairs, dead, *extra.arrays)


def _fox_bwd(zm, c_pairs, dead, d_out, lnorm, delta, tq, ride=None):
    n_tok = zm.shape[0]
    nq = n_tok // tq
    ii, jj, kk = _fox_steps(nq)
    n_steps = len(ii)
    n_ride = len(ride.arrays) if ride else 0

    def kern(ii_ref, jj_ref, kk_ref, q_ref, k_ref, v_ref, ck_ref, dead_ref, do_ref, ln_ref, dl_ref, *more):
        ride_in, ride_out = more[:n_ride], more[n_ride + 5:2 * n_ride + 5]
        dq_ref, dk_ref, dv_ref, cs_ref, rs_ref = more[n_ride:n_ride + 5]
        qs_ref, qo_ref, dos_ref, dq_acc = more[2 * n_ride + 5:2 * n_ride + 9]
        step = pl.program_id(1)
        j, kind = jj_ref[step], kk_ref[step]
        lo = lax.broadcasted_iota(jnp.int32, (2 * tq, LANES), 1) < HEAD_DIM
        if ride:
            ride.at_first_step((FOX_HEADS // 2, n_steps), ride_in, ride_out, more[2 * n_ride + 9:])

        @pl.when(step == 0)
        def _():
            dk_ref[...] = jnp.zeros_like(dk_ref)
            dv_ref[...] = jnp.zeros_like(dv_ref)
            cs_ref[...] = jnp.zeros_like(cs_ref)

        @pl.when(j == 0)
        def _():
            q2, do2 = q_ref[...], do_ref[...]
            zq = jnp.zeros_like(q2)
            ones = jnp.ones((2 * tq, LANES), BF16)
            for a in range(2):
                half = lo if a == 0 else ~lo
                qa = jnp.where(half, q2, zq) * SCALE
                qs_ref[a] = qa
                qo_ref[a] = jnp.concatenate([qa, ones], axis=1)
                dos_ref[a] = jnp.where(half, do2, zq)
            dq_acc[...] = jnp.zeros(dq_acc.shape, F32)

        def sweep(subs, heads):
            kv, vv = k_ref[...], v_ref[...]
            k_ones = jnp.concatenate([kv, jnp.ones((tq, LANES), BF16)], axis=1)
            dk, dv, sums = None, None, {}
            for sub, diag in subs:
                rows = slice(sub * tq, (sub + 1) * tq)
                for a in heads:
                    t = _dot_nt(qs_ref[a, rows], kv) - ck_ref[a:a + 1, :]
                    if diag:
                        t = _causal(t, tq, tq)
                    prob = jnp.exp(t - _lane_tile(ln_ref[rows, a * LANES:(a + 1) * LANES], tq))
                    dp = _dot_nt(dos_ref[a, rows], vv)
                    ds = (prob * (dp - _lane_tile(dl_ref[rows, a * LANES:(a + 1) * LANES], tq))).astype(BF16)
                    dq_acc[a, rows] += _dot(ds, k_ones)
                    dk_cs = _dot_tn(ds, qo_ref[a, rows])
                    dv_a = _dot_tn(prob.astype(BF16), dos_ref[a, rows])
                    dk = dk_cs[:, :LANES] if dk is None else dk + dk_cs[:, :LANES]
                    dv = dv_a if dv is None else dv + dv_a
                    sums[a] = dk_cs[:, LANES:] if a not in sums else sums[a] + dk_cs[:, LANES:]
            keys = pl.ds(pl.multiple_of(j * tq, tq), tq)
            dk_ref[keys, :] += dk
            cs_ref[keys, :] += jnp.where(lo[:tq], sums.get(0, 0.0), sums.get(1, 0.0))
            dv_ref[keys, :] += dv

        _fox_dispatch(sweep, kind, dead_ref, 2 * pl.program_id(0), ii_ref[step] * nq + j)

        @pl.when(kind == 2)
        def _():
            dq_ref[...] = jnp.where(lo, dq_acc[0, :, :LANES], dq_acc[1, :, :LANES]) * SCALE
            rs_ref[...] = jnp.where(lo, dq_acc[0, :, LANES:], dq_acc[1, :, LANES:])

        if ride:
            ride.at_last_step((FOX_HEADS // 2, n_steps), ride_in, ride_out, more[2 * n_ride + 9:])

    blk = (tq, LANES)
    by_i = lambda col: (lambda hp, s, ii, jj, kk: (ii[s], col + hp))
    by_j = lambda col: (lambda hp, s, ii, jj, kk: (jj[s], col + hp))
    resident = pl.BlockSpec((2 * tq, LANES), by_i(0))
    stat = pl.BlockSpec((2 * tq, 2 * LANES), by_i(0))
    whole = pl.BlockSpec((n_tok, LANES), lambda hp, s, ii, jj, kk: (0, hp))
    extra = ride if ride else _NO_RIDE
    grid_spec = pltpu.PrefetchScalarGridSpec(
        num_scalar_prefetch=3, grid=(FOX_HEADS // 2, n_steps),
        in_specs=[pl.BlockSpec((2 * tq, LANES), by_i(Q_COL)), pl.BlockSpec(blk, by_j(K_COL)),
                  pl.BlockSpec(blk, by_j(V_COL)),
                  pl.BlockSpec((None, 2, tq), lambda hp, s, ii, jj, kk: (hp, 0, jj[s])),
                  pl.BlockSpec(memory_space=pltpu.SMEM), resident, stat, stat] + extra.in_specs,
        out_specs=[resident, whole, whole, whole, resident] + extra.out_specs,
        scratch_shapes=[pltpu.VMEM((2, 2 * tq, LANES), BF16), pltpu.VMEM((2, 2 * tq, 2 * LANES), BF16),
                        pltpu.VMEM((2, 2 * tq, LANES), BF16), pltpu.VMEM((2, 2 * tq, 2 * LANES), F32)] + extra.scratch)
    wide = jax.ShapeDtypeStruct((n_tok, 4 * LANES), F32)
    return pl.pallas_call(
        kern, grid_spec=grid_spec, out_shape=[wide] * 5 + extra.out_shape, name="fox_bwd",
        compiler_params=_params(2, FOX_BWD_VMEM))(ii, jj, kk, zm, zm, zm, c_pairs, dead, d_out, lnorm, delta,
                                                  *extra.arrays)


def _all_gather(shards):
    n_w = len(shards)

    def kern(*refs):
        x_refs, out_refs = refs[:n_w], refs[n_w:2 * n_w]
        send_sems, recv_sems, local_sems = refs[2 * n_w:]
        x, y, c = _my_pos()
        me, sibling = (x, y, c), (x, y, 1 - c)
        chips = [(1 - x, y), (x, 1 - y), (1 - x, 1 - y)]

        def slot(w, px, py, pc):
            return out_refs[w].at[4 * px + 2 * py + pc]

        def copy(w, k, block, to, src=None):
            return pltpu.make_async_remote_copy(
                src_ref=slot(w, *block) if src is None else src, dst_ref=slot(w, *block),
                send_sem=send_sems.at[7 * w + k], recv_sem=recv_sems.at[7 * w + k], device_id=to, device_id_type=MESH)

        local, started = [], []
        for w in range(n_w):
            mine = pltpu.make_async_copy(x_refs[w], slot(w, *me), local_sems.at[w])
            mine.start()
            local.append(mine)
            first = [copy(w, 0, me, sibling, src=x_refs[w])]
            first += [copy(w, 1 + k, me, (*chip, c), src=x_refs[w]) for k, chip in enumerate(chips)]
            for cp in first:
                cp.start()
            started += first
        for k, chip in enumerate(chips):
            for w in range(n_w):
                copy(w, 1 + k, (*chip, c), me).wait_recv()
                passed = copy(w, 4 + k, (*chip, c), sibling)
                passed.start()
                started.append(passed)
        for w in range(n_w):
            copy(w, 0, sibling, me).wait_recv()
            for k, chip in enumerate(chips):
                copy(w, 4 + k, (*chip, 1 - c), me).wait_recv()
        for cp in started:
            cp.wait_send()
        for cp in local:
            cp.wait()

    any_spec = pl.BlockSpec(memory_space=pl.ANY)
    return pl.pallas_call(
        kern, out_shape=[jax.ShapeDtypeStruct((N_DEV,) + s.shape, s.dtype) for s in shards],
        in_specs=[any_spec] * n_w, out_specs=[any_spec] * n_w,
        scratch_shapes=[pltpu.SemaphoreType.DMA((7 * n_w,)), pltpu.SemaphoreType.DMA((7 * n_w,)),
                        pltpu.SemaphoreType.DMA((n_w,))],
        name="weight_all_gather")(*shards)


def _small_exchange(small):
    def kern(s_ref, sall_ref, *sems):
        copies = _gather_copies([s_ref], [sall_ref], *sems)
        _start_copies(*copies)
        _finish_copies(*copies)

    any_spec = pl.BlockSpec(memory_space=pl.ANY)
    return pl.pallas_call(
        kern, out_shape=jax.ShapeDtypeStruct((N_DEV,) + small.shape, small.dtype), in_specs=[any_spec],
        out_specs=any_spec, scratch_shapes=_exchange_scratch(1), name="small_grad_exchange")(small)


ADAMW_BLOCK_BYTES = 2 * 1024 * 1024


def _adamw(parts, w, m, v, name):
    n_parts, n_rows, n_cols = parts.shape
    limit = max(8, ADAMW_BLOCK_BYTES // (n_parts * n_cols * parts.dtype.itemsize))
    tr = max(t for t in range(8, n_rows + 1, 8) if n_rows % t == 0 and t <= limit)

    def kern(p_ref, w_ref, m_ref, v_ref, g_out, d_out, m_out, v_out):
        g = p_ref[0].astype(F32)
        for k in range(1, n_parts):
            g = g + p_ref[k].astype(F32)
        m_new = ADAM_B1 * m_ref[...] + (1.0 - ADAM_B1) * g
        v_new = ADAM_B2 * v_ref[...] + (1.0 - ADAM_B2) * jnp.square(g)
        m_hat = m_new / (1.0 - ADAM_B1 ** ADAM_STEP)
        v_hat = v_new / (1.0 - ADAM_B2 ** ADAM_STEP)
        g_out[...] = g
        d_out[...] = -ADAM_LR * (m_hat / (jnp.sqrt(v_hat) + ADAM_EPS) + ADAM_WD * w_ref[...])
        m_out[...] = m_new
        v_out[...] = v_new

    row = pl.BlockSpec((tr, n_cols), lambda i: (i, 0))
    out = jax.ShapeDtypeStruct((n_rows, n_cols), F32)
    return pl.pallas_call(
        kern, grid=(n_rows // tr,),
        in_specs=[pl.BlockSpec((n_parts, tr, n_cols), lambda i: (0, i, 0)), row, row, row],
        out_specs=[row, row, row, row], out_shape=[out, out, out, out], name=name,
        compiler_params=_params(1))(parts, w, m, v)


SHARDED = {
    "w_in": ((D_MODEL, D_IN), 1), "w_br_swa": ((512, D_MODEL), 1), "w_br_fox": ((512, D_MODEL), 1),
    "w_mix_out": ((D_MODEL, D_MODEL), 0), "w_ff1": ((D_MODEL, D_FF), 1), "w_ff2": ((D_FF, D_MODEL), 0),
    "w_ple_gate": ((D_MODEL, D_MODEL), 0), "w_ple_proj": ((PLE_DIM, D_MODEL), 1),
}
W_IN_SHARD = D_IN // N_DEV
W_IN_PAD = 640
SMALL = ("g_mix", "g_mlp", "g_ple", "g_final", "b_forget", "swa_sinks")
SMALL_COLS = 1024


def _wire_shard(name, a):
    a = a.reshape(a.shape[-2:])
    return jnp.pad(a, ((0, 0), (0, W_IN_PAD - W_IN_SHARD))) if name == "w_in" else a


def _from_wire(name, a):
    return (a[:, :W_IN_SHARD] if name == "w_in" else a)[None]


def _w_all_from_wire(stacked):
    w_in = jnp.concatenate([stacked[d][:, :W_IN_SHARD] for d in range(N_DEV)], axis=1)
    fpad = jnp.zeros((D_MODEL, N_FPAD - FOX_HEADS), stacked.dtype)
    return jnp.concatenate([w_in[:, :N_MAIN + FOX_HEADS], fpad, w_in[:, N_MAIN + FOX_HEADS:]], axis=1)


def _dw_in_to_wire(dw_all):
    dw_in = jnp.concatenate([dw_all[:, :N_MAIN + FOX_HEADS], dw_all[:, N_MAIN + N_FPAD:]], axis=1)
    pad = jnp.zeros((D_MODEL, W_IN_PAD - W_IN_SHARD), dw_all.dtype)
    return jnp.stack([jnp.concatenate([dw_in[:, d * W_IN_SHARD:(d + 1) * W_IN_SHARD], pad], axis=1)
                      for d in range(N_DEV)])


def _pack_small(vals, scalar=None):
    rows = [jnp.pad(vals[n].reshape(-1), (0, SMALL_COLS - vals[n].size)) for n in SMALL]
    if scalar is not None:
        rows.append(jnp.pad(scalar.reshape(1), (0, SMALL_COLS - 1)))
    rows += [jnp.zeros((SMALL_COLS,), F32)] * (8 - len(rows))
    return jnp.stack(rows)


def _unpack_small(slab, like):
    return {n: slab[r, :like[n].size].reshape(like[n].shape) for r, n in enumerate(SMALL)}


def _local_step(x, p, tgt, w, small, tm, tq, ts, late_shards=None):
    n_tok = x.shape[0]
    row = lambda v: v.reshape(1, -1)
    g_mix, g_mlp, g_ple, g_fin = row(small["g_mix"]), row(small["g_mlp"]), row(small["g_ple"]), row(small["g_final"])
    sinks = small["swa_sinks"].reshape(-1)
    b_col = small["b_forget"].reshape(FOX_HEADS, 1)

    assert tm == tq
    u1, zm, zfg, zf, nrm = _in_proj(x, g_mix, w["w_all"], tm)
    f_t = zf[:, :FOX_HEADS].T
    c_pairs = _decay_cumsum(f_t, b_col).reshape(FOX_HEADS // 2, 2, n_tok)
    attn_a, lse_a = _swa_fwd(zm, sinks)
    dead = _fox_dead_steps(nrm, c_pairs, tq)
    if late_shards is None:
        attn_b, ln_b = _fox_fwd(zm, c_pairs, dead, tq)
    else:
        attn_b, ln_b, *late = _fox_fwd(zm, c_pairs, dead, tq, _gather_ride(list(late_shards.values())))
        w = {**w, **_gathered_to_local(dict(zip(late_shards, late)))}
    ya, yb, mixed, h1, u2, a, r, h2 = _mix_ffn_fwd(attn_a, attn_b, zfg, x, w["w_br_swa"], w["w_br_fox"],
                                                   w["w_mix_out"], g_mlp, w["w_ff1"], w["w_ff2"], tm // 2)
    dh3, dlg, dpp, u3, loss_acc, dgf = _head_fwd_bwd(h2, p, tgt, g_ple, w["w_ple_gate"], w["w_ple_proj"], g_fin, tm)

    dh2, dh2b, da, dgp = _ffn_bwd_a(dlg, dh3, h2, a, w["w_ple_gate"], g_ple, w["w_ff2"], tm // 2)
    dh1, dh1b, dgl, dya, dyb, daa, dab, delta_b, dgm = _ffn_bwd_b(
        da, dh2, h1, ya, yb, zfg, attn_b, w["w_ff1"], g_mlp, w["w_mix_out"], w["w_br_swa"], w["w_br_fox"], tm // 2)
    dq_a, dkp, dkc, dvp, dvc, dsk = _swa_bwd(zm, sinks, daa, attn_a, lse_a)
    dw = {
        "w_br_swa": _matmul_tn(attn_a, dya, "dw_br_swa", ts, stack_cols=D_MODEL // N_DEV),
        "w_br_fox": _matmul_tn(attn_b, dyb, "dw_br_fox", ts, stack_cols=D_MODEL // N_DEV),
        "w_mix_out": _matmul_tn(mixed, dh1b, "dw_mix_out", ts),
        "w_ff1": _matmul_tn(u2, da, "dw_ff1", ts, stack_cols=D_FF // N_DEV),
        "w_ff2": _matmul_tn(r, dh2b, "dw_ff2", ts),
        "w_ple_gate": _matmul_tn(u3, dlg, "dw_ple_gate", ts),
        "w_ple_proj": _matmul_tn(p, dpp, "dw_ple_proj", ts, stack_cols=D_MODEL // N_DEV),
    }
    if late_shards is None:
        dq_b, dk_b, dv_b, cs, rs = _fox_bwd(zm, c_pairs, dead, dab, ln_b, delta_b, tq)
        late_parts = None
    else:
        wire = _local_to_wire(dw)
        dq_b, dk_b, dv_b, cs, rs, *parts = _fox_bwd(zm, c_pairs, dead, dab, ln_b, delta_b, tq,
                                                    _scatter_ride([wire[n] for n in late_shards]))
        late_parts = dict(zip(late_shards, parts))

    up = lambda t: jnp.concatenate([t[SWA_BLOCK:], jnp.zeros((SWA_BLOCK, LANES), F32)], axis=0)
    dk_a, dv_a = dkc + up(dkp), dvc + up(dvp)
    df_t, db = _decay_bwd(cs, rs, f_t, b_col)
    df = jnp.pad(df_t.T, ((0, 0), (0, N_FPAD - FOX_HEADS)))
    dz = jnp.concatenate([dq_a, dk_a.astype(BF16), dv_a.astype(BF16), dq_b.astype(BF16), dk_b.astype(BF16), dv_b.astype(BF16),
                          df.astype(BF16), dgl], axis=1)
    dw["w_all"] = _matmul_tn(u1, dz, "dw_in", ts)
    if late_shards is None:
        dx, dgx = _in_proj_bwd(dz, dh1, x, w["w_all"], g_mix, tm)
    else:
        dx, dgx, late_parts["w_in"] = _in_proj_bwd(dz, dh1, x, w["w_all"], g_mix, tm,
                                                   _scatter_ride([_dw_in_to_wire(dw["w_all"])]))
    dsmall = {"g_mix": dgx[0], "g_mlp": dgm[0], "g_ple": dgp[0], "g_final": dgf[0],
              "b_forget": db[:, 0], "swa_sinks": dsk[:, 0]}
    return loss_acc[0, 0], dx, dw, dsmall, late_parts


_ROWS = lambda t: t.reshape(-1, t.shape[-1])
_BY_ROWS = lambda t: t.reshape(N_DEV, t.shape[0] // N_DEV, t.shape[1])
_SAME = lambda t: t
LOCAL_LAYOUT = {
    "w_in": ("w_all", _w_all_from_wire, _dw_in_to_wire), "w_br_swa": ("w_br_swa", _SAME, _SAME),
    "w_br_fox": ("w_br_fox", _SAME, _SAME), "w_mix_out": ("w_mix_out", _ROWS, _BY_ROWS),
    "w_ff1": ("w_ff1", _SAME, _SAME), "w_ff2": ("w_ff2", _SAME, _BY_ROWS),
    "w_ple_gate": ("w_ple_gate", _ROWS, _BY_ROWS), "w_ple_proj": ("w_ple_proj", _SAME, _SAME),
}


def _gathered_to_local(g):
    return {LOCAL_LAYOUT[n][0]: LOCAL_LAYOUT[n][1](t) for n, t in g.items()}


def _local_to_wire(dw):
    names = {local: n for n, (local, _, _) in LOCAL_LAYOUT.items()}
    return {names[local]: LOCAL_LAYOUT[names[local]][2](t) for local, t in dw.items()}


def kernel(x, p, g_mix, w_in, b_forget, swa_sinks, w_br_swa, w_br_fox, w_mix_out, g_mlp, w_ff1, w_ff2, g_ple, w_ple_gate, w_ple_proj, g_final, loss_target, m_g_mix, m_w_in, m_b_forget, m_swa_sinks, m_w_br_swa, m_w_br_fox, m_w_mix_out, m_g_mlp, m_w_ff1, m_w_ff2, m_g_ple, m_w_ple_gate, m_w_ple_proj, m_g_final, v_g_mix, v_w_in, v_b_forget, v_swa_sinks, v_w_br_swa, v_w_br_fox, v_w_mix_out, v_g_mlp, v_w_ff1, v_w_ff2, v_g_ple, v_w_ple_gate, v_w_ple_proj, v_g_final):
    given = dict(g_mix=g_mix, w_in=w_in, b_forget=b_forget, swa_sinks=swa_sinks, w_br_swa=w_br_swa, w_br_fox=w_br_fox,
                 w_mix_out=w_mix_out, g_mlp=g_mlp, w_ff1=w_ff1, w_ff2=w_ff2, g_ple=g_ple, w_ple_gate=w_ple_gate,
                 w_ple_proj=w_ple_proj, g_final=g_final)
    mom = dict(g_mix=m_g_mix, w_in=m_w_in, b_forget=m_b_forget, swa_sinks=m_swa_sinks, w_br_swa=m_w_br_swa,
               w_br_fox=m_w_br_fox, w_mix_out=m_w_mix_out, g_mlp=m_g_mlp, w_ff1=m_w_ff1, w_ff2=m_w_ff2, g_ple=m_g_ple,
               w_ple_gate=m_w_ple_gate, w_ple_proj=m_w_ple_proj, g_final=m_g_final)
    vel = dict(g_mix=v_g_mix, w_in=v_w_in, b_forget=v_b_forget, swa_sinks=v_swa_sinks, w_br_swa=v_w_br_swa,
               w_br_fox=v_w_br_fox, w_mix_out=v_w_mix_out, g_mlp=v_g_mlp, w_ff1=v_w_ff1, w_ff2=v_w_ff2, g_ple=v_g_ple,
               w_ple_gate=v_w_ple_gate, w_ple_proj=v_w_ple_proj, g_final=v_g_final)
    names = list(given)
    sharded = list(SHARDED)

    w_wire = {n: _wire_shard(n, given[n]) for n in sharded}
    late = [n for n in sharded if n != "w_in"]
    gathered = _all_gather([w_wire["w_in"].astype(BF16)])
    local_w = _gathered_to_local({"w_in": gathered[0]})
    small = {n: given[n].reshape(-1) for n in SMALL}

    n_tok = x.shape[1]
    tile = min(TOKEN_TILE, n_tok // 4)
    loss_part, dx, dw, dsmall, parts = _local_step(
        x[0], p[0, 0], loss_target[0], local_w, small, tm=tile, tq=tile, ts=min(DW_TOKENS_PER_STEP, n_tok // 4),
        late_shards={n: w_wire[n].astype(BF16) for n in late})
    small_all = _small_exchange(_pack_small(dsmall, loss_part))

    res = {}
    for n in sharded:
        part = parts[n]
        flat = part.reshape(N_DEV, -1, part.shape[-1])
        outs = _adamw(flat, w_wire[n], _wire_shard(n, mom[n]), _wire_shard(n, vel[n]), "adamw_" + n)
        res[n] = [_from_wire(n, o) for o in outs]
    outs_s = _adamw(small_all, _pack_small(small), _pack_small({n: mom[n] for n in SMALL}),
                    _pack_small({n: vel[n] for n in SMALL}), "adamw_small")
    small_res = [_unpack_small(o, given) for o in outs_s]
    loss = outs_s[0][len(SMALL), 0]

    groups = [[res[n][k] if n in res else small_res[k][n] for n in names] for k in range(4)]
    return (loss, dx[None], *groups[0], *groups[1], *groups[2], *groups[3])
```

```python
import numpy as np
import jax
import jax.numpy as jnp
from jax import lax
from jax.experimental import pallas as pl
from jax.experimental.pallas import tpu as pltpu

F32 = jnp.float32
BF16 = jnp.bfloat16

D_MODEL = 1024
HEAD_DIM = 64
SWA_HEADS = 8
FOX_HEADS = 8
CHUNK_SHIFT = 6
SWA_BLOCK = 128
WINDOW_CHUNKS = 2
D_FF = 4096
PLE_DIM = 256
RMS_EPS = 1e-6
N_MAIN = 2304
N_FPAD = 128
N_GATE = 2048
D_IN = N_MAIN + FOX_HEADS + N_GATE
SCALE = HEAD_DIM ** -0.5
NEG = -1e30

ADAM_LR = 0.001
ADAM_B1 = 0.9
ADAM_B2 = 0.999
ADAM_EPS = 1e-08
ADAM_WD = 0.01
ADAM_STEP = 10

N_DEV = 8
TOKEN_TILE = 512
DW_TOKENS_PER_STEP = 2048
LANES = 128
V7X_VMEM_BYTES = 64 * 1024 * 1024
VMEM_LIMIT = V7X_VMEM_BYTES * 3 // 4
FOX_BWD_VMEM = V7X_VMEM_BYTES * 7 // 8
MESH = pl.DeviceIdType.MESH

_NT = (((1,), (1,)), ((), ()))
_TN = (((0,), (0,)), ((), ()))


def _params(n_grid, vmem_limit=VMEM_LIMIT):
    return pltpu.CompilerParams(dimension_semantics=("arbitrary",) * n_grid, vmem_limit_bytes=vmem_limit)


def _chunks(n, step):
    return [(s, min(step, n - s)) for s in range(0, n, step)]


def _sigmoid(x):
    return 1.0 / (1.0 + jnp.exp(-x))


def _dot(a, b):
    return jnp.dot(a, b, preferred_element_type=F32)


def _dot_nt(a, b):
    return lax.dot_general(a, b, _NT, preferred_element_type=F32)


def _dot_tn(a, b):
    return lax.dot_general(a, b, _TN, preferred_element_type=F32)


def _lane_concat(stacked_ref):
    return jnp.concatenate([stacked_ref[d] for d in range(N_DEV)], axis=1)


def _rms(h):
    return lax.rsqrt(jnp.mean(h * h, axis=-1, keepdims=True) + RMS_EPS)


def _rms_bwd(h, g, du):
    rs = _rms(h)
    n = h * rs
    dn = du * g
    dh = rs * (dn - n * jnp.mean(dn * n, axis=-1, keepdims=True))
    return dh, jnp.sum(du * n, axis=0, keepdims=True)


def _acc_rows(ref, i, row):
    @pl.when(i == 0)
    def _():
        ref[...] = jnp.zeros_like(ref)
    ref[...] += jnp.broadcast_to(row, ref.shape)


def _row_call(body, name, n_rows, tm, row_ins, const_ins, row_outs, acc_outs, ride=None, tile_outs=()):
    row_outs = list(row_outs)
    n_ri, n_ci, n_ro, n_ao = len(row_ins), len(const_ins), len(row_outs) + len(tile_outs), len(acc_outs)
    extra = ride if ride else _NO_RIDE
    n_ride = len(extra.arrays)
    grid = (n_rows // tm,)

    def kern(*refs):
        i = pl.program_id(0)
        ins, refs = refs[:n_ri + n_ci], refs[n_ri + n_ci:]
        ride_in, refs = refs[:n_ride], refs[n_ride:]
        outs, refs = refs[:n_ro + n_ao], refs[n_ro + n_ao:]
        ride_out, sems = refs[:n_ride], refs[n_ride:]
        if ride:
            ride.at_first_step(grid, ride_in, ride_out, sems)
        body(i, ins[:n_ri], ins[n_ri:], outs[:n_ro], outs[n_ro:])
        if ride:
            ride.at_last_step(grid, ride_in, ride_out, sems)

    def whole(a):
        zeros = (0,) * a.ndim
        return pl.BlockSpec(a.shape, lambda i: zeros, pipeline_mode=pl.Buffered(1))

    in_specs = [pl.BlockSpec((tm, a.shape[1]), lambda i: (i, 0)) for a in row_ins]
    in_specs += [whole(a) for a in const_ins] + extra.in_specs
    out_specs = [pl.BlockSpec((tm, c), lambda i: (i, 0)) for c, _ in row_outs]
    out_specs += [pl.BlockSpec((8, c), lambda i: (i, 0)) for c in tile_outs]
    out_specs += [pl.BlockSpec((8, c), lambda i: (0, 0)) for c in acc_outs] + extra.out_specs
    out_shape = [jax.ShapeDtypeStruct((n_rows, c), dt) for c, dt in row_outs]
    out_shape += [jax.ShapeDtypeStruct((8 * grid[0], c), F32) for c in tile_outs]
    out_shape += [jax.ShapeDtypeStruct((8, c), F32) for c in acc_outs] + extra.out_shape
    return pl.pallas_call(kern, grid=grid, in_specs=in_specs, out_specs=out_specs, out_shape=out_shape,
                          scratch_shapes=extra.scratch, name=name,
                          compiler_params=_params(1))(*row_ins, *const_ins, *extra.arrays)


def _in_proj(x, g_mix, w_all, tm):
    def body(i, ins, consts, outs, accs):
        x_ref, = ins
        g_ref, w_ref = consts
        u_ref, zm_ref, zfg_ref, zf_ref, nrm_ref = outs
        xv = x_ref[...]
        u = ((xv * _rms(xv)) * g_ref[...]).astype(BF16)
        u_ref[...] = u
        for s, n in _chunks(N_MAIN, 768):
            zm_ref[:, s:s + n] = _dot(u, w_ref[:, s:s + n]).astype(BF16)
        for s, n in _chunks(N_FPAD + N_GATE, 512):
            zfg_ref[:, s:s + n] = _dot(u, w_ref[:, N_MAIN + s:N_MAIN + s + n])
        zf_ref[...] = zfg_ref[:, :N_FPAD]
        lane = lax.broadcasted_iota(jnp.int32, (4 * LANES, LANES), 0)
        head = lax.broadcasted_iota(jnp.int32, (4 * LANES, LANES), 1)
        pick = (lane // HEAD_DIM == head).astype(BF16)
        tq_, tk_ = (zm_ref[:, col * LANES:(col + 4) * LANES].astype(F32) for col in (Q_COL, K_COL))
        rows = [jnp.max(_dot((t * t).astype(BF16), pick), axis=0, keepdims=True) for t in (tq_, tk_)]
        rows.append(jnp.min(_dot((tq_ * tk_).astype(BF16), pick), axis=0, keepdims=True))
        nrm_ref[...] = jnp.concatenate(rows + [jnp.zeros((5, LANES), F32)], axis=0)

    *outs, nrm = _row_call(body, "in_proj", x.shape[0], tm, [x], [g_mix, w_all],
                           [(D_MODEL, BF16), (N_MAIN, BF16), (N_FPAD + N_GATE, F32), (N_FPAD, F32)], [],
                           tile_outs=[LANES])
    return (*outs, nrm)


def _mix_ffn_fwd(attn_a, attn_b, zfg, x, w_sa, w_fo, w_mo, g_mlp, w1s, w2s, tm):
    ch = D_FF // N_DEV

    def body(i, ins, consts, outs, accs):
        aa_ref, ab_ref, zfg_ref, x_ref = ins
        wsa_ref, wfo_ref, wmo_ref, g_ref, w1_ref, w2_ref = consts
        ya_ref, yb_ref, mx_ref, h1_ref, u2_ref, a_ref, r_ref, h2_ref = outs
        ya = _dot(aa_ref[...], _lane_concat(wsa_ref))
        yb = _dot(ab_ref[...], _lane_concat(wfo_ref))
        g0 = _sigmoid(zfg_ref[:, N_FPAD:N_FPAD + D_MODEL])
        g1 = _sigmoid(zfg_ref[:, N_FPAD + D_MODEL:N_FPAD + 2 * D_MODEL])
        mixed = (g0 * ya + g1 * yb).astype(BF16)
        ya_ref[...] = ya.astype(BF16)
        yb_ref[...] = yb.astype(BF16)
        mx_ref[...] = mixed
        h1 = x_ref[...] + _dot(mixed, wmo_ref[...])
        h1_ref[...] = h1
        u = ((h1 * _rms(h1)) * g_ref[...]).astype(BF16)
        u2_ref[...] = u
        acc = h1
        for c in range(N_DEV):
            a = _dot(u, w1_ref[c])
            a_ref[:, c * ch:(c + 1) * ch] = a.astype(BF16)
            r = jnp.square(jnp.maximum(a, 0.0)).astype(BF16)
            r_ref[:, c * ch:(c + 1) * ch] = r
            acc = acc + _dot(r, w2_ref[c])
        h2_ref[...] = acc

    return _row_call(body, "mix_ffn_fwd", x.shape[0], tm, [attn_a, attn_b, zfg, x],
                     [w_sa, w_fo, w_mo, g_mlp, w1s, w2s],
                     [(D_MODEL, BF16), (D_MODEL, BF16), (D_MODEL, BF16), (D_MODEL, F32), (D_MODEL, BF16),
                      (D_FF, BF16), (D_FF, BF16), (D_MODEL, F32)], [])


def _head_ffn_bwd(h2, p, tgt, a, g_ple, w_pg, w_pp, g_fin, w2s, tm):
    ch = D_FF // N_DEV

    def body(i, ins, consts, outs, accs):
        h2_ref, p_ref, t_ref, a_ref = ins
        gp_ref, wpg_ref, wpp_ref, gf_ref, w2_ref = consts
        dlg_ref, dpp_ref, u3_ref, dh2_ref, dh2b_ref, da_ref = outs
        loss_ref, dgf_ref, dgp_ref = accs
        h2 = h2_ref[...]
        gp = gp_ref[...]
        u3 = ((h2 * _rms(h2)) * gp).astype(BF16)
        u3_ref[...] = u3
        pg = _sigmoid(_dot(u3, wpg_ref[...]))
        pp = _dot(p_ref[...].astype(BF16), _lane_concat(wpp_ref))
        h3 = h2 + pg * pp
        rs3 = _rms(h3)
        n3 = h3 * rs3
        gf = gf_ref[...]
        err = n3 * gf - t_ref[...]
        row_loss = 0.5 * jnp.mean(err * err, axis=-1, keepdims=True)
        _acc_rows(loss_ref, i, jnp.broadcast_to(jnp.sum(row_loss, axis=0, keepdims=True), (1, LANES)))
        dy = err * (1.0 / D_MODEL)
        _acc_rows(dgf_ref, i, jnp.sum(dy * n3, axis=0, keepdims=True))
        dn = dy * gf
        dh3 = rs3 * (dn - n3 * jnp.mean(dn * n3, axis=-1, keepdims=True))
        dpp_ref[...] = (dh3 * pg).astype(BF16)
        dlg = ((dh3 * pp) * pg * (1.0 - pg)).astype(BF16)
        dlg_ref[...] = dlg
        dh, dg = _rms_bwd(h2, gp, _dot_nt(dlg, wpg_ref[...]))
        _acc_rows(dgp_ref, i, dg)
        dh2 = dh3 + dh
        dh2_ref[...] = dh2
        dh2b = dh2.astype(BF16)
        dh2b_ref[...] = dh2b
        for c in range(N_DEV):
            dr = _dot_nt(dh2b, w2_ref[c])
            av = a_ref[:, c * ch:(c + 1) * ch].astype(F32)
            da_ref[:, c * ch:(c + 1) * ch] = (dr * (2.0 * jnp.maximum(av, 0.0))).astype(BF16)

    return _row_call(body, "head_ffn_bwd", h2.shape[0], tm, [h2, p, tgt, a], [g_ple, w_pg, w_pp, g_fin, w2s],
                     [(D_MODEL, BF16), (D_MODEL, BF16), (D_MODEL, BF16), (D_MODEL, F32), (D_MODEL, BF16),
                      (D_FF, BF16)], [LANES, D_MODEL, D_MODEL])


def _ffn_bwd_b(da, dh2, h1, ya, yb, zfg, attn_b, w1s, g_mlp, w_mo, w_sa, w_fo, tm):
    ch = D_FF // N_DEV

    def body(i, ins, consts, outs, accs):
        da_ref, dh2_ref, h1_ref, ya_ref, yb_ref, zfg_ref, ob_ref = ins
        w1_ref, gm_ref, wmo_ref, wsa_ref, wfo_ref = consts
        dh1_ref, dh1b_ref, dgl_ref, dya_ref, dyb_ref, daa_ref, dab_ref, dl_ref = outs
        dgm_ref, = accs
        du2 = _dot_nt(da_ref[:, 0:ch], w1_ref[0])
        for c in range(1, N_DEV):
            du2 = du2 + _dot_nt(da_ref[:, c * ch:(c + 1) * ch], w1_ref[c])
        dh, dg = _rms_bwd(h1_ref[...], gm_ref[...], du2)
        _acc_rows(dgm_ref, i, dg)
        dh1 = dh2_ref[...] + dh
        dh1_ref[...] = dh1
        dh1b = dh1.astype(BF16)
        dh1b_ref[...] = dh1b
        dmx = _dot_nt(dh1b, wmo_ref[...])
        g0 = _sigmoid(zfg_ref[:, N_FPAD:N_FPAD + D_MODEL])
        g1 = _sigmoid(zfg_ref[:, N_FPAD + D_MODEL:N_FPAD + 2 * D_MODEL])
        dya = (dmx * g0).astype(BF16)
        dyb = (dmx * g1).astype(BF16)
        dya_ref[...] = dya
        dyb_ref[...] = dyb
        dgl_ref[:, 0:D_MODEL] = ((dmx * ya_ref[...].astype(F32)) * g0 * (1.0 - g0)).astype(BF16)
        dgl_ref[:, D_MODEL:2 * D_MODEL] = ((dmx * yb_ref[...].astype(F32)) * g1 * (1.0 - g1)).astype(BF16)
        daa_ref[...] = _dot_nt(dya, _lane_concat(wsa_ref)).astype(BF16)
        dab = _dot_nt(dyb, _lane_concat(wfo_ref)).astype(BF16)
        dab_ref[...] = dab
        half_in = lax.broadcasted_iota(jnp.int32, (LANES, 2 * LANES), 0) // HEAD_DIM
        half_out = lax.broadcasted_iota(jnp.int32, (LANES, 2 * LANES), 1) // LANES
        pick = (half_in == half_out).astype(BF16)
        for pair in range(FOX_HEADS // 2):
            cols = slice(pair * LANES, (pair + 1) * LANES)
            prod = dab[:, cols].astype(F32) * ob_ref[:, cols].astype(F32)
            hi = prod.astype(BF16)
            lo_part = (prod - hi.astype(F32)).astype(BF16)
            dl_ref[:, 2 * pair * LANES:(2 * pair + 2) * LANES] = _dot(hi, pick) + _dot(lo_part, pick)

    half = D_MODEL // 2
    return _row_call(body, "ffn_bwd_b", h1.shape[0], tm, [da, dh2, h1, ya, yb, zfg, attn_b],
                     [w1s, g_mlp, w_mo, w_sa, w_fo],
                     [(D_MODEL, F32), (D_MODEL, BF16), (N_GATE, BF16), (D_MODEL, BF16), (D_MODEL, BF16),
                      (half, BF16), (half, BF16), (FOX_HEADS * LANES, F32)], [D_MODEL])


def _in_proj_bwd(dz, dh1, x, w_all, g_mix, tm, ride=None):
    def body(i, ins, consts, outs, accs):
        dz_ref, dh1_ref, x_ref = ins
        w_ref, g_ref = consts
        dx_ref, = outs
        dgx_ref, = accs
        du1 = _dot_nt(dz_ref[...], w_ref[...])
        dh, dg = _rms_bwd(x_ref[...], g_ref[...], du1)
        _acc_rows(dgx_ref, i, dg)
        dx_ref[...] = dh1_ref[...] + dh

    return _row_call(body, "in_proj_bwd", x.shape[0], tm, [dz, dh1, x], [w_all, g_mix],
                     [(D_MODEL, F32)], [D_MODEL], ride)


def _matmul_tn(a, b, name, ts, stack_cols=0):
    n_rows, ka = a.shape
    n = b.shape[1]
    tk = min(ka, 1024)
    tn = 896 if n % 1024 else 1024
    n_stack = tn // stack_cols if stack_cols else 0
    assert ka % tk == 0 and n % tn == 0 and n_rows % ts == 0 and (not stack_cols or tk == ka)
    n_steps = n_rows // ts

    def kern(a_ref, b_ref, o_ref, acc_ref):
        s = pl.program_id(2)

        @pl.when(s == 0)
        def _():
            acc_ref[...] = jnp.zeros_like(acc_ref)
        acc_ref[...] += _dot_tn(a_ref[...].astype(BF16), b_ref[...])

        @pl.when(s == n_steps - 1)
        def _():
            if stack_cols:
                for c in range(n_stack):
                    o_ref[c] = acc_ref[:, c * stack_cols:(c + 1) * stack_cols].astype(BF16)
            else:
                o_ref[...] = acc_ref[...].astype(BF16)

    if stack_cols:
        out_spec = pl.BlockSpec((n_stack, tk, stack_cols), lambda i, j, s: (j, 0, 0))
        out_shape = jax.ShapeDtypeStruct((n // stack_cols, ka, stack_cols), BF16)
    else:
        out_spec = pl.BlockSpec((tk, tn), lambda i, j, s: (i, j))
        out_shape = jax.ShapeDtypeStruct((ka, n), BF16)
    return pl.pallas_call(
        kern, grid=(ka // tk, n // tn, n_steps),
        in_specs=[pl.BlockSpec((ts, tk), lambda i, j, s: (s, i)), pl.BlockSpec((ts, tn), lambda i, j, s: (s, j))],
        out_specs=out_spec, out_shape=out_shape, scratch_shapes=[pltpu.VMEM((tk, tn), F32)], name=name,
        compiler_params=_params(3))(a, b)


SCAN_CHUNK = 512


def _decay_cumsum(f_t, b_col):
    n_tok = f_t.shape[1]
    ch = min(SCAN_CHUNK, n_tok)

    def kern(f_ref, b_ref, c_ref):
        r = lax.broadcasted_iota(jnp.int32, (ch, ch), 0)
        c = lax.broadcasted_iota(jnp.int32, (ch, ch), 1)
        tri = (r <= c).astype(F32)
        carry = jnp.zeros((8, 1), F32)
        for k in range(n_tok // ch):
            xv = f_ref[:, k * ch:(k + 1) * ch] + b_ref[...]
            lf = jnp.minimum(xv, 0.0) - jnp.log(1.0 + jnp.exp(-jnp.abs(xv)))
            cs = jnp.dot(lf, tri, precision=lax.Precision.HIGHEST, preferred_element_type=F32) + carry
            c_ref[:, k * ch:(k + 1) * ch] = cs
            carry = cs[:, ch - 1:ch]

    return pl.pallas_call(kern, out_shape=jax.ShapeDtypeStruct((8, n_tok), F32), name="decay_cumsum",
                          compiler_params=_params(0))(f_t, b_col)


def _decay_bwd(cs, rs, f_t, b_col):
    n_tok = f_t.shape[1]
    ch = min(SCAN_CHUNK, n_tok)
    n_ch = n_tok // ch

    def kern(cs_ref, rs_ref, f_ref, b_ref, df_ref, db_ref, carry_ref):
        k = pl.program_id(0)

        @pl.when(k == 0)
        def _():
            carry_ref[...] = jnp.zeros_like(carry_ref)
            db_ref[...] = jnp.zeros_like(db_ref)

        r = lax.broadcasted_iota(jnp.int32, (ch, ch), 0)
        c = lax.broadcasted_iota(jnp.int32, (ch, ch), 1)
        tri = (r >= c).astype(F32)
        head = lax.broadcasted_iota(jnp.int32, (8, 4 * LANES), 0)
        lane = lax.broadcasted_iota(jnp.int32, (8, 4 * LANES), 1)
        pick = (lane == HEAD_DIM * head).astype(F32)
        dc = lax.dot_general(pick, rs_ref[...] - cs_ref[...], _NT, precision=lax.Precision.HIGHEST,
                             preferred_element_type=F32)
        rc = jnp.dot(dc, tri, precision=lax.Precision.HIGHEST, preferred_element_type=F32) + carry_ref[:, 0:1]
        carry_ref[...] = jnp.broadcast_to(rc[:, 0:1], carry_ref.shape)
        df = rc / (1.0 + jnp.exp(f_ref[...] + b_ref[...]))
        df_ref[...] = df
        db_ref[...] += jnp.broadcast_to(jnp.sum(df, axis=1, keepdims=True), db_ref.shape)

    back = lambda k: n_ch - 1 - k
    wide = pl.BlockSpec((ch, 4 * LANES), lambda k: (back(k), 0))
    row = pl.BlockSpec((8, ch), lambda k: (0, back(k)))
    return pl.pallas_call(
        kern, grid=(n_ch,),
        in_specs=[wide, wide, row, pl.BlockSpec((8, 1), lambda k: (0, 0))],
        out_specs=[row, pl.BlockSpec((8, LANES), lambda k: (0, 0))],
        out_shape=[jax.ShapeDtypeStruct((8, n_tok), F32), jax.ShapeDtypeStruct((8, LANES), F32)],
        scratch_shapes=[pltpu.VMEM((8, LANES), F32)], name="decay_bwd", compiler_params=_params(1))(cs, rs, f_t, b_col)


def _swa_bias_table():
    row = jnp.arange(SWA_BLOCK)[:, None] + SWA_BLOCK
    col = jnp.arange(2 * SWA_BLOCK)[None, :]
    cd = (row >> CHUNK_SHIFT) - (col >> CHUNK_SHIFT)
    band = (cd >= 0) & (cd <= WINDOW_CHUNKS)
    slopes = jnp.asarray([2.0 ** -(h + 1) for h in range(SWA_HEADS)], F32)
    bias = -slopes[:, None, None] * jnp.abs(row - col).astype(F32)[None]
    return jnp.stack([jnp.where(band & (col >= SWA_BLOCK), bias, NEG), jnp.where(band, bias, NEG)])


SWA_BIAS_SPEC = pl.BlockSpec((None, SWA_HEADS, SWA_BLOCK, 2 * SWA_BLOCK), lambda n: (jnp.minimum(n, 1), 0, 0, 0))


def _swap_halves(t):
    return pltpu.roll(t.astype(F32), HEAD_DIM, axis=1).astype(t.dtype)


def _swa_specs():
    blk = SWA_BLOCK
    q = pl.BlockSpec((blk, 4 * LANES), lambda n: (n, 0))
    kp = pl.BlockSpec((blk, LANES), lambda n: (jnp.maximum(n - 1, 0), 4))
    kc = pl.BlockSpec((blk, LANES), lambda n: (n, 4))
    vp = pl.BlockSpec((blk, LANES), lambda n: (jnp.maximum(n - 1, 0), 5))
    vc = pl.BlockSpec((blk, LANES), lambda n: (n, 5))
    return q, kp, kc, vp, vc


SWA_GROUPS = ([h for h in range(SWA_HEADS) if h % 2 == h // 4], [h for h in range(SWA_HEADS) if h % 2 != h // 4])


def _stack_heads(ref, heads, lo, mask_halves):
    tiles = []
    for h in heads:
        t = ref[:, (h // 2) * LANES:(h // 2 + 1) * LANES]
        tiles.append(jnp.where(lo if h % 2 == 0 else ~lo, t, jnp.zeros_like(t)) if mask_halves else t)
    return jnp.concatenate(tiles, axis=0)


def _per_head_column(values, heads):
    return jnp.concatenate([jnp.full((SWA_BLOCK, 1), values(h), F32) for h in heads], axis=0)


def _swa_scores(q_ref, kx, heads, lo, bias_ref):
    qa = _stack_heads(q_ref, heads, lo, True) * SCALE
    return qa, _dot_nt(qa, kx) + jnp.concatenate([bias_ref[h] for h in heads], axis=0)


def _swa_fwd(zm, sinks):
    n_tok = zm.shape[0]
    blk = SWA_BLOCK

    def kern(q_ref, kp_ref, kc_ref, vp_ref, vc_ref, bias_ref, sink_ref, o_ref, lse_ref):
        k2 = jnp.concatenate([kp_ref[...], kc_ref[...]], axis=0)
        v2 = jnp.concatenate([vp_ref[...], vc_ref[...]], axis=0)
        ksw, vsw = _swap_halves(k2), _swap_halves(v2)
        lane = lax.broadcasted_iota(jnp.int32, (blk, LANES), 1)
        lo = lane < HEAD_DIM
        lse_t = jnp.zeros((blk, LANES), F32)
        for pair in range(SWA_HEADS // 2):
            q2 = q_ref[:, pair * LANES:(pair + 1) * LANES]
            outs = []
            for a in range(2):
                h = 2 * pair + a
                qa = jnp.where(lo if a == 0 else ~lo, q2, jnp.zeros_like(q2)) * SCALE
                kx, vx = (k2, v2) if h in SWA_GROUPS[0] else (ksw, vsw)
                s = _dot_nt(qa, kx) + bias_ref[h]
                sink = sink_ref[h]
                m = jnp.maximum(jnp.max(s, axis=-1, keepdims=True), sink)
                e = jnp.exp(s - m)
                l = jnp.sum(e, axis=-1, keepdims=True) + jnp.exp(sink - m)
                pn = (e * (1.0 / l)).astype(BF16)
                outs.append(_dot(pn, vx))
                lse_t = jnp.where(lane == h, m + jnp.log(l), lse_t)
            o_ref[:, pair * LANES:(pair + 1) * LANES] = jnp.where(lo, outs[0], outs[1]).astype(BF16)
        lse_ref[...] = lse_t

    q, kp, kc, vp, vc = _swa_specs()
    return pl.pallas_call(
        kern, grid=(n_tok // blk,),
        in_specs=[q, kp, kc, vp, vc, SWA_BIAS_SPEC, pl.BlockSpec(memory_space=pltpu.SMEM)],
        out_specs=[pl.BlockSpec((blk, 4 * LANES), lambda n: (n, 0)), pl.BlockSpec((blk, LANES), lambda n: (n, 0))],
        out_shape=[jax.ShapeDtypeStruct((n_tok, 4 * LANES), BF16), jax.ShapeDtypeStruct((n_tok, LANES), F32)],
        name="swa_fwd", compiler_params=_params(1))(zm, zm, zm, zm, zm, _swa_bias_table(), sinks)


def _swa_bwd(zm, sinks, d_out, out, lse):
    n_tok = zm.shape[0]
    blk = SWA_BLOCK

    def kern(q_ref, kp_ref, kc_ref, vp_ref, vc_ref, bias_ref, do_ref, o_ref, lse_ref, sink_ref,
             dq_ref, dkp_ref, dkc_ref, dvp_ref, dvc_ref, dsk_ref):
        n = pl.program_id(0)

        @pl.when(n == 0)
        def _():
            dsk_ref[...] = jnp.zeros_like(dsk_ref)

        k2 = jnp.concatenate([kp_ref[...], kc_ref[...]], axis=0)
        v2 = jnp.concatenate([vp_ref[...], vc_ref[...]], axis=0)
        lane = lax.broadcasted_iota(jnp.int32, (blk, LANES), 1)
        lo = lane < HEAD_DIM
        lse_t = lse_ref[...]
        dqs, dkv = {}, []
        for heads, kx, vx in ((SWA_GROUPS[0], k2, v2), (SWA_GROUPS[1], _swap_halves(k2), _swap_halves(v2))):
            qa, s = _swa_scores(q_ref, kx, heads, lo, bias_ref)
            doa = _stack_heads(do_ref, heads, lo, True)
            lse_g = jnp.concatenate([lse_t[:, h:h + 1] for h in heads], axis=0)
            prob = jnp.exp(s - lse_g)
            dd = jnp.sum(doa.astype(F32) * _stack_heads(o_ref, heads, lo, False).astype(F32), axis=-1, keepdims=True)
            ds = (prob * (_dot_nt(doa, vx) - dd)).astype(BF16)
            sink_part = -jnp.exp(_per_head_column(lambda h: sink_ref[h], heads) - lse_g) * dd
            dq = _dot(ds, kx) * SCALE
            for r, h in enumerate(heads):
                dqs[h] = dq[r * blk:(r + 1) * blk]
                dsk_ref[h:h + 1, :] += jnp.broadcast_to(
                    jnp.sum(sink_part[r * blk:(r + 1) * blk], axis=0, keepdims=True), (1, LANES))
            dkv.append((_dot_tn(ds, qa), _dot_tn(prob.astype(BF16), doa)))
        for pair in range(SWA_HEADS // 2):
            dq_ref[:, pair * LANES:(pair + 1) * LANES] = jnp.where(lo, dqs[2 * pair], dqs[2 * pair + 1]).astype(BF16)
        dk = dkv[0][0] + pltpu.roll(dkv[1][0], HEAD_DIM, axis=1)
        dv = dkv[0][1] + pltpu.roll(dkv[1][1], HEAD_DIM, axis=1)
        dkp_ref[...] = dk[0:blk]
        dkc_ref[...] = dk[blk:2 * blk]
        dvp_ref[...] = dv[0:blk]
        dvc_ref[...] = dv[blk:2 * blk]

    q, kp, kc, vp, vc = _swa_specs()
    wide = pl.BlockSpec((blk, 4 * LANES), lambda n: (n, 0))
    narrow = pl.BlockSpec((blk, LANES), lambda n: (n, 0))
    part = jax.ShapeDtypeStruct((n_tok, LANES), F32)
    return pl.pallas_call(
        kern, grid=(n_tok // blk,),
        in_specs=[q, kp, kc, vp, vc, SWA_BIAS_SPEC, wide, wide, narrow, pl.BlockSpec(memory_space=pltpu.SMEM)],
        out_specs=[wide, narrow, narrow, narrow, narrow, pl.BlockSpec((8, LANES), lambda n: (0, 0))],
        out_shape=[jax.ShapeDtypeStruct((n_tok, 4 * LANES), BF16), part, part, part, part,
                   jax.ShapeDtypeStruct((8, LANES), F32)],
        name="swa_bwd", compiler_params=_params(1))(zm, zm, zm, zm, zm, _swa_bias_table(), d_out, out, lse, sinks)


def _my_pos():
    return lax.axis_index("x"), lax.axis_index("y"), lax.axis_index("c")


def _peer(k):
    x, y, c = _my_pos()
    px, py, pc = x ^ (k >> 2), y ^ ((k >> 1) & 1), c ^ (k & 1)
    return (px, py, pc), 4 * px + 2 * py + pc


def _gather_copies(x_refs, out_refs, send_sems, recv_sems, local_sems):
    x, y, c = _my_pos()
    my_id = 4 * x + 2 * y + c
    local = [pltpu.make_async_copy(x_refs[w], out_refs[w].at[my_id], local_sems.at[w]) for w in range(len(x_refs))]
    sends, arrivals = [], []
    for k in range(1, N_DEV):
        peer, peer_id = _peer(k)
        for w in range(len(x_refs)):
            sems = dict(send_sem=send_sems.at[7 * w + k - 1], recv_sem=recv_sems.at[7 * w + k - 1],
                        device_id=peer, device_id_type=MESH)
            sends.append(pltpu.make_async_remote_copy(src_ref=x_refs[w], dst_ref=out_refs[w].at[my_id], **sems))
            arrivals.append(pltpu.make_async_remote_copy(src_ref=x_refs[w], dst_ref=out_refs[w].at[peer_id], **sems))
    return local, sends, arrivals


def _scatter_copies(g_refs, part_refs, send_sems, recv_sems, local_sems):
    x, y, c = _my_pos()
    my_id = 4 * x + 2 * y + c
    local = [pltpu.make_async_copy(g_refs[w].at[my_id], part_refs[w].at[0], local_sems.at[w])
             for w in range(len(g_refs))]
    sends, arrivals = [], []
    for k in range(1, N_DEV):
        peer, peer_id = _peer(k)
        for w in range(len(g_refs)):
            sems = dict(send_sem=send_sems.at[7 * w + k - 1], recv_sem=recv_sems.at[7 * w + k - 1],
                        device_id=peer, device_id_type=MESH)
            sends.append(pltpu.make_async_remote_copy(src_ref=g_refs[w].at[peer_id], dst_ref=part_refs[w].at[k], **sems))
            arrivals.append(pltpu.make_async_remote_copy(src_ref=g_refs[w].at[my_id], dst_ref=part_refs[w].at[k], **sems))
    return local, sends, arrivals


def _start_copies(local, sends, arrivals):
    for cp in local + sends:
        cp.start()


def _finish_copies(local, sends, arrivals):
    for cp in arrivals:
        cp.wait_recv()
    for cp in sends:
        cp.wait_send()
    for cp in local:
        cp.wait()


def _exchange_scratch(n_arrays):
    return [pltpu.SemaphoreType.DMA((7 * n_arrays,)), pltpu.SemaphoreType.DMA((7 * n_arrays,)),
            pltpu.SemaphoreType.DMA((n_arrays,))]


class _Ride:
    def __init__(self, arrays, out_shape, copies):
        self.arrays, self.out_shape, self.copies = list(arrays), list(out_shape), copies
        any_spec = pl.BlockSpec(memory_space=pl.ANY)
        self.in_specs = [any_spec] * len(self.arrays)
        self.out_specs = [any_spec] * len(self.arrays)
        self.scratch = _exchange_scratch(len(self.arrays)) if self.arrays else []

    @staticmethod
    def _at(grid, last):
        hit = [pl.program_id(d) == (n - 1 if last else 0) for d, n in enumerate(grid)]
        return hit[0] if len(hit) == 1 else jnp.logical_and(*hit)

    def at_first_step(self, grid, in_refs, out_refs, sems):
        @pl.when(self._at(grid, False))
        def _():
            _start_copies(*self.copies(in_refs, out_refs, *sems))

    def at_last_step(self, grid, in_refs, out_refs, sems):
        @pl.when(self._at(grid, True))
        def _():
            _finish_copies(*self.copies(in_refs, out_refs, *sems))


_NO_RIDE = _Ride([], [], None)


def _gather_ride(shards):
    return _Ride(shards, [jax.ShapeDtypeStruct((N_DEV,) + s.shape, s.dtype) for s in shards], _gather_copies)


def _scatter_ride(grads):
    return _Ride(grads, [jax.ShapeDtypeStruct(g.shape, g.dtype) for g in grads], _scatter_copies)


Q_COL, K_COL, V_COL = 6, 10, 14


def _causal(t, tq, tk):
    row = lax.broadcasted_iota(jnp.int32, (tq, tk), 0)
    col = lax.broadcasted_iota(jnp.int32, (tq, tk), 1)
    return jnp.where(col <= row, t, NEG)


def _lane_tile(stat, width):
    return jnp.tile(stat, (1, width // LANES))


def _fox_steps(nq):
    steps = [(i2, j, 0 if j < 2 * i2 else 1 + j - 2 * i2) for i2 in range(nq // 2) for j in range(2 * i2 + 2)]
    return [np.asarray(col, np.int32) for col in zip(*steps)]


_SWEEPS = {0: [(0, False), (1, False)], 1: [(0, True), (1, False)], 2: [(1, True)]}


def _fox_dispatch(sweep, kind, dead_ref, head0, idx):
    dead0, dead1 = dead_ref[head0, idx] > 0.5, dead_ref[head0 + 1, idx] > 0.5
    live0, live1 = jnp.logical_not(dead0), jnp.logical_not(dead1)
    below = kind == 0
    pl.when(jnp.logical_and(below, jnp.logical_and(live0, live1)))(lambda: sweep(_SWEEPS[0], (0, 1)))
    pl.when(jnp.logical_and(below, jnp.logical_and(live0, dead1)))(lambda: sweep(_SWEEPS[0], (0,)))
    pl.when(jnp.logical_and(below, jnp.logical_and(dead0, live1)))(lambda: sweep(_SWEEPS[0], (1,)))
    pl.when(kind == 1)(lambda: sweep(_SWEEPS[1], (0, 1)))
    pl.when(kind == 2)(lambda: sweep(_SWEEPS[2], (0, 1)))


EXP_ZERO = 110.0
NORM_SLACK = 1.005


def _fox_dead_steps(nrm, c_pairs, tq):
    nq = nrm.shape[0] // 8
    stats = nrm.reshape(nq, 8, LANES)[:, :3, :FOX_HEADS]
    qn, kn, own = jnp.sqrt(stats[:, 0]) * SCALE, jnp.sqrt(stats[:, 1]), stats[:, 2] * SCALE
    cb = c_pairs.reshape(FOX_HEADS, nq, tq)
    c_max, c_min = jnp.max(cb, axis=-1).T, jnp.min(cb, axis=-1).T
    both = lambda t, pick: pick(t.reshape(nq // 2, 2, FOX_HEADS), axis=1)
    qn2, kn2, c_max2, own2 = both(qn, jnp.max), both(kn, jnp.max), both(c_max, jnp.max), both(own, jnp.min)
    row_max_floor = own2 - (NORM_SLACK - 1.0) * qn2 * kn2 - c_max2
    gap = qn2[:, None] * kn[None] * NORM_SLACK - c_min[None] - row_max_floor[:, None]
    below = jnp.arange(nq)[None, :] < 2 * jnp.arange(nq // 2)[:, None]
    dead = jnp.logical_and(gap < -EXP_ZERO, below[..., None])
    return dead.transpose(2, 0, 1).reshape(FOX_HEADS, -1).astype(F32)


def _fox_fwd(zm, c_pairs, dead, tq, ride=None):
    n_tok = zm.shape[0]
    nq = n_tok // tq
    ii, jj, kk = _fox_steps(nq)
    n_steps = len(ii)
    n_ride = len(ride.arrays) if ride else 0

    def kern(ii_ref, jj_ref, kk_ref, q_ref, k_ref, v_ref, ck_ref, dead_ref, *more):
        ride_in, (o_ref, ln_ref), ride_out = more[:n_ride], more[n_ride:n_ride + 2], more[n_ride + 2:2 * n_ride + 2]
        qs_ref, m_ref, l_ref, acc_ref = more[2 * n_ride + 2:2 * n_ride + 6]
        step = pl.program_id(1)
        j, kind = jj_ref[step], kk_ref[step]
        lo = lax.broadcasted_iota(jnp.int32, (2 * tq, LANES), 1) < HEAD_DIM
        if ride:
            ride.at_first_step((FOX_HEADS // 2, n_steps), ride_in, ride_out, more[2 * n_ride + 6:])

        @pl.when(j == 0)
        def _():
            q2 = q_ref[...]
            zq = jnp.zeros_like(q2)
            qs_ref[0] = jnp.where(lo, q2, zq) * SCALE
            qs_ref[1] = jnp.where(lo, zq, q2) * SCALE
            m_ref[...] = jnp.full(m_ref.shape, NEG, F32)
            l_ref[...] = jnp.zeros(l_ref.shape, F32)
            acc_ref[...] = jnp.zeros(acc_ref.shape, F32)

        def sweep(subs, heads):
            kv = k_ref[...]
            v_ones = jnp.concatenate([v_ref[...], jnp.ones((tq, LANES), BF16)], axis=1)
            for sub, diag in subs:
                rows = slice(sub * tq, (sub + 1) * tq)
                for a in heads:
                    t = _dot_nt(qs_ref[a, rows], kv) - ck_ref[a:a + 1, :]
                    if diag:
                        t = _causal(t, tq, tq)
                    m_old = m_ref[a, rows]
                    m_new = jnp.maximum(m_old, jnp.max(t, axis=-1, keepdims=True))
                    alpha = jnp.exp(m_old - m_new)
                    e = jnp.exp(t - _lane_tile(m_new, tq)).astype(BF16)
                    pv = _dot(e, v_ones)
                    acc_ref[a, rows] = alpha * acc_ref[a, rows] + pv[:, :LANES]
                    l_ref[a, rows] = alpha * l_ref[a, rows] + pv[:, LANES:]
                    m_ref[a, rows] = m_new

        _fox_dispatch(sweep, kind, dead_ref, 2 * pl.program_id(0), ii_ref[step] * nq + j)

        @pl.when(kind == 2)
        def _():
            o_ref[...] = jnp.where(lo, acc_ref[0] / l_ref[0], acc_ref[1] / l_ref[1]).astype(BF16)
            ln_ref[:, :LANES] = m_ref[0] + jnp.log(l_ref[0])
            ln_ref[:, LANES:] = m_ref[1] + jnp.log(l_ref[1])

        if ride:
            ride.at_last_step((FOX_HEADS // 2, n_steps), ride_in, ride_out, more[2 * n_ride + 6:])

    blk = (tq, LANES)
    by_i = lambda col: (lambda hp, s, ii, jj, kk: (ii[s], col + hp))
    by_j = lambda col: (lambda hp, s, ii, jj, kk: (jj[s], col + hp))
    extra = ride if ride else _NO_RIDE
    grid_spec = pltpu.PrefetchScalarGridSpec(
        num_scalar_prefetch=3, grid=(FOX_HEADS // 2, n_steps),
        in_specs=[pl.BlockSpec((2 * tq, LANES), by_i(Q_COL)), pl.BlockSpec(blk, by_j(K_COL)),
                  pl.BlockSpec(blk, by_j(V_COL)),
                  pl.BlockSpec((None, 2, tq), lambda hp, s, ii, jj, kk: (hp, 0, jj[s])),
                  pl.BlockSpec(memory_space=pltpu.SMEM)] + extra.in_specs,
        out_specs=[pl.BlockSpec((2 * tq, LANES), by_i(0)), pl.BlockSpec((2 * tq, 2 * LANES), by_i(0))] + extra.out_specs,
        scratch_shapes=[pltpu.VMEM((2, 2 * tq, LANES), BF16), pltpu.VMEM((2, 2 * tq, LANES), F32),
                        pltpu.VMEM((2, 2 * tq, LANES), F32), pltpu.VMEM((2, 2 * tq, LANES), F32)] + extra.scratch)
    return pl.pallas_call(
        kern, grid_spec=grid_spec,
        out_shape=[jax.ShapeDtypeStruct((n_tok, 4 * LANES), BF16),
                   jax.ShapeDtypeStruct((n_tok, FOX_HEADS * LANES), F32)] + extra.out_shape,
        name="fox_fwd", compiler_params=_params(2))(ii, jj, kk, zm, zm, zm, c_pairs, dead, *extra.arrays)


def _fox_bwd(zm, c_pairs, dead, d_out, lnorm, delta, tq, ride=None):
    n_tok = zm.shape[0]
    nq = n_tok // tq
    ii, jj, kk = _fox_steps(nq)
    n_steps = len(ii)
    n_ride = len(ride.arrays) if ride else 0

    def kern(ii_ref, jj_ref, kk_ref, q_ref, k_ref, v_ref, ck_ref, dead_ref, do_ref, ln_ref, dl_ref, *more):
        ride_in, ride_out = more[:n_ride], more[n_ride + 5:2 * n_ride + 5]
        dq_ref, dk_ref, dv_ref, cs_ref, rs_ref = more[n_ride:n_ride + 5]
        qs_ref, qo_ref, dos_ref, dq_acc = more[2 * n_ride + 5:2 * n_ride + 9]
        step = pl.program_id(1)
        j, kind = jj_ref[step], kk_ref[step]
        lo = lax.broadcasted_iota(jnp.int32, (2 * tq, LANES), 1) < HEAD_DIM
        if ride:
            ride.at_first_step((FOX_HEADS // 2, n_steps), ride_in, ride_out, more[2 * n_ride + 9:])

        @pl.when(step == 0)
        def _():
            dk_ref[...] = jnp.zeros_like(dk_ref)
            dv_ref[...] = jnp.zeros_like(dv_ref)
            cs_ref[...] = jnp.zeros_like(cs_ref)

        @pl.when(j == 0)
        def _():
            q2, do2 = q_ref[...], do_ref[...]
            zq = jnp.zeros_like(q2)
            ones = jnp.ones((2 * tq, LANES), BF16)
            for a in range(2):
                half = lo if a == 0 else ~lo
                qa = jnp.where(half, q2, zq) * SCALE
                qs_ref[a] = qa
                qo_ref[a] = jnp.concatenate([qa, ones], axis=1)
                dos_ref[a] = jnp.where(half, do2, zq)
            dq_acc[...] = jnp.zeros(dq_acc.shape, F32)

        def sweep(subs, heads):
            kv, vv = k_ref[...], v_ref[...]
            k_ones = jnp.concatenate([kv, jnp.ones((tq, LANES), BF16)], axis=1)
            dk, dv, sums = None, None, {}
            for sub, diag in subs:
                rows = slice(sub * tq, (sub + 1) * tq)
                for a in heads:
                    t = _dot_nt(qs_ref[a, rows], kv) - ck_ref[a:a + 1, :]
                    if diag:
                        t = _causal(t, tq, tq)
                    prob = jnp.exp(t - _lane_tile(ln_ref[rows, a * LANES:(a + 1) * LANES], tq))
                    dp = _dot_nt(dos_ref[a, rows], vv)
                    ds = (prob * (dp - _lane_tile(dl_ref[rows, a * LANES:(a + 1) * LANES], tq))).astype(BF16)
                    dq_acc[a, rows] += _dot(ds, k_ones)
                    dk_cs = _dot_tn(ds, qo_ref[a, rows])
                    dv_a = _dot_tn(prob.astype(BF16), dos_ref[a, rows])
                    dk = dk_cs[:, :LANES] if dk is None else dk + dk_cs[:, :LANES]
                    dv = dv_a if dv is None else dv + dv_a
                    sums[a] = dk_cs[:, LANES:] if a not in sums else sums[a] + dk_cs[:, LANES:]
            keys = pl.ds(pl.multiple_of(j * tq, tq), tq)
            dk_ref[keys, :] += dk
            cs_ref[keys, :] += jnp.where(lo[:tq], sums.get(0, 0.0), sums.get(1, 0.0))
            dv_ref[keys, :] += dv

        _fox_dispatch(sweep, kind, dead_ref, 2 * pl.program_id(0), ii_ref[step] * nq + j)

        @pl.when(kind == 2)
        def _():
            dq_ref[...] = jnp.where(lo, dq_acc[0, :, :LANES], dq_acc[1, :, :LANES]) * SCALE
            rs_ref[...] = jnp.where(lo, dq_acc[0, :, LANES:], dq_acc[1, :, LANES:])

        if ride:
            ride.at_last_step((FOX_HEADS // 2, n_steps), ride_in, ride_out, more[2 * n_ride + 9:])

    blk = (tq, LANES)
    by_i = lambda col: (lambda hp, s, ii, jj, kk: (ii[s], col + hp))
    by_j = lambda col: (lambda hp, s, ii, jj, kk: (jj[s], col + hp))
    resident = pl.BlockSpec((2 * tq, LANES), by_i(0))
    stat = pl.BlockSpec((2 * tq, 2 * LANES), by_i(0))
    whole = pl.BlockSpec((n_tok, LANES), lambda hp, s, ii, jj, kk: (0, hp))
    extra = ride if ride else _NO_RIDE
    grid_spec = pltpu.PrefetchScalarGridSpec(
        num_scalar_prefetch=3, grid=(FOX_HEADS // 2, n_steps),
        in_specs=[pl.BlockSpec((2 * tq, LANES), by_i(Q_COL)), pl.BlockSpec(blk, by_j(K_COL)),
                  pl.BlockSpec(blk, by_j(V_COL)),
                  pl.BlockSpec((None, 2, tq), lambda hp, s, ii, jj, kk: (hp, 0, jj[s])),
                  pl.BlockSpec(memory_space=pltpu.SMEM), resident, stat, stat] + extra.in_specs,
        out_specs=[resident, whole, whole, whole, resident] + extra.out_specs,
        scratch_shapes=[pltpu.VMEM((2, 2 * tq, LANES), BF16), pltpu.VMEM((2, 2 * tq, 2 * LANES), BF16),
                        pltpu.VMEM((2, 2 * tq, LANES), BF16), pltpu.VMEM((2, 2 * tq, 2 * LANES), F32)] + extra.scratch)
    wide = jax.ShapeDtypeStruct((n_tok, 4 * LANES), F32)
    return pl.pallas_call(
        kern, grid_spec=grid_spec, out_shape=[wide] * 5 + extra.out_shape, name="fox_bwd",
        compiler_params=_params(2, FOX_BWD_VMEM))(ii, jj, kk, zm, zm, zm, c_pairs, dead, d_out, lnorm, delta,
                                                  *extra.arrays)


def _all_gather(shards):
    n_w = len(shards)

    def kern(*refs):
        x_refs, out_refs = refs[:n_w], refs[n_w:2 * n_w]
        send_sems, recv_sems, local_sems = refs[2 * n_w:]
        x, y, c = _my_pos()
        me, sibling = (x, y, c), (x, y, 1 - c)
        chips = [(1 - x, y), (x, 1 - y), (1 - x, 1 - y)]

        def slot(w, px, py, pc):
            return out_refs[w].at[4 * px + 2 * py + pc]

        def copy(w, k, block, to, src=None):
            return pltpu.make_async_remote_copy(
                src_ref=slot(w, *block) if src is None else src, dst_ref=slot(w, *block),
                send_sem=send_sems.at[7 * w + k], recv_sem=recv_sems.at[7 * w + k], device_id=to, device_id_type=MESH)

        local, started = [], []
        for w in range(n_w):
            mine = pltpu.make_async_copy(x_refs[w], slot(w, *me), local_sems.at[w])
            mine.start()
            local.append(mine)
            first = [copy(w, 0, me, sibling, src=x_refs[w])]
            first += [copy(w, 1 + k, me, (*chip, c), src=x_refs[w]) for k, chip in enumerate(chips)]
            for cp in first:
                cp.start()
            started += first
        for k, chip in enumerate(chips):
            for w in range(n_w):
                copy(w, 1 + k, (*chip, c), me).wait_recv()
                passed = copy(w, 4 + k, (*chip, c), sibling)
                passed.start()
                started.append(passed)
        for w in range(n_w):
            copy(w, 0, sibling, me).wait_recv()
            for k, chip in enumerate(chips):
                copy(w, 4 + k, (*chip, 1 - c), me).wait_recv()
        for cp in started:
            cp.wait_send()
        for cp in local:
            cp.wait()

    any_spec = pl.BlockSpec(memory_space=pl.ANY)
    return pl.pallas_call(
        kern, out_shape=[jax.ShapeDtypeStruct((N_DEV,) + s.shape, s.dtype) for s in shards],
        in_specs=[any_spec] * n_w, out_specs=[any_spec] * n_w,
        scratch_shapes=[pltpu.SemaphoreType.DMA((7 * n_w,)), pltpu.SemaphoreType.DMA((7 * n_w,)),
                        pltpu.SemaphoreType.DMA((n_w,))],
        name="weight_all_gather")(*shards)


def _small_exchange(small):
    def kern(s_ref, sall_ref, *sems):
        copies = _gather_copies([s_ref], [sall_ref], *sems)
        _start_copies(*copies)
        _finish_copies(*copies)

    any_spec = pl.BlockSpec(memory_space=pl.ANY)
    return pl.pallas_call(
        kern, out_shape=jax.ShapeDtypeStruct((N_DEV,) + small.shape, small.dtype), in_specs=[any_spec],
        out_specs=any_spec, scratch_shapes=_exchange_scratch(1), name="small_grad_exchange")(small)


ADAMW_BLOCK_BYTES = 2 * 1024 * 1024


def _adamw(parts, w, m, v, name):
    n_parts, n_rows, n_cols = parts.shape
    limit = max(8, ADAMW_BLOCK_BYTES // (n_parts * n_cols * parts.dtype.itemsize))
    tr = max(t for t in range(8, n_rows + 1, 8) if n_rows % t == 0 and t <= limit)

    def kern(p_ref, w_ref, m_ref, v_ref, g_out, d_out, m_out, v_out):
        g = p_ref[0].astype(F32)
        for k in range(1, n_parts):
            g = g + p_ref[k].astype(F32)
        m_new = ADAM_B1 * m_ref[...] + (1.0 - ADAM_B1) * g
        v_new = ADAM_B2 * v_ref[...] + (1.0 - ADAM_B2) * jnp.square(g)
        m_hat = m_new / (1.0 - ADAM_B1 ** ADAM_STEP)
        v_hat = v_new / (1.0 - ADAM_B2 ** ADAM_STEP)
        g_out[...] = g
        d_out[...] = -ADAM_LR * (m_hat / (jnp.sqrt(v_hat) + ADAM_EPS) + ADAM_WD * w_ref[...])
        m_out[...] = m_new
        v_out[...] = v_new

    row = pl.BlockSpec((tr, n_cols), lambda i: (i, 0))
    out = jax.ShapeDtypeStruct((n_rows, n_cols), F32)
    return pl.pallas_call(
        kern, grid=(n_rows // tr,),
        in_specs=[pl.BlockSpec((n_parts, tr, n_cols), lambda i: (0, i, 0)), row, row, row],
        out_specs=[row, row, row, row], out_shape=[out, out, out, out], name=name,
        compiler_params=_params(1))(parts, w, m, v)


SHARDED = {
    "w_in": ((D_MODEL, D_IN), 1), "w_br_swa": ((512, D_MODEL), 1), "w_br_fox": ((512, D_MODEL), 1),
    "w_mix_out": ((D_MODEL, D_MODEL), 0), "w_ff1": ((D_MODEL, D_FF), 1), "w_ff2": ((D_FF, D_MODEL), 0),
    "w_ple_gate": ((D_MODEL, D_MODEL), 0), "w_ple_proj": ((PLE_DIM, D_MODEL), 1),
}
W_IN_SHARD = D_IN // N_DEV
W_IN_PAD = 640
SMALL = ("g_mix", "g_mlp", "g_ple", "g_final", "b_forget", "swa_sinks")
SMALL_COLS = 1024


def _wire_shard(name, a):
    a = a.reshape(a.shape[-2:])
    return jnp.pad(a, ((0, 0), (0, W_IN_PAD - W_IN_SHARD))) if name == "w_in" else a


def _from_wire(name, a):
    return (a[:, :W_IN_SHARD] if name == "w_in" else a)[None]


def _w_all_from_wire(stacked):
    w_in = jnp.concatenate([stacked[d][:, :W_IN_SHARD] for d in range(N_DEV)], axis=1)
    fpad = jnp.zeros((D_MODEL, N_FPAD - FOX_HEADS), stacked.dtype)
    return jnp.concatenate([w_in[:, :N_MAIN + FOX_HEADS], fpad, w_in[:, N_MAIN + FOX_HEADS:]], axis=1)


def _dw_in_to_wire(dw_all):
    dw_in = jnp.concatenate([dw_all[:, :N_MAIN + FOX_HEADS], dw_all[:, N_MAIN + N_FPAD:]], axis=1)
    pad = jnp.zeros((D_MODEL, W_IN_PAD - W_IN_SHARD), dw_all.dtype)
    return jnp.stack([jnp.concatenate([dw_in[:, d * W_IN_SHARD:(d + 1) * W_IN_SHARD], pad], axis=1)
                      for d in range(N_DEV)])


def _pack_small(vals, scalar=None):
    rows = [jnp.pad(vals[n].reshape(-1), (0, SMALL_COLS - vals[n].size)) for n in SMALL]
    if scalar is not None:
        rows.append(jnp.pad(scalar.reshape(1), (0, SMALL_COLS - 1)))
    rows += [jnp.zeros((SMALL_COLS,), F32)] * (8 - len(rows))
    return jnp.stack(rows)


def _unpack_small(slab, like):
    return {n: slab[r, :like[n].size].reshape(like[n].shape) for r, n in enumerate(SMALL)}


def _local_step(x, p, tgt, w, small, tm, tq, ts, late_shards=None):
    n_tok = x.shape[0]
    row = lambda v: v.reshape(1, -1)
    g_mix, g_mlp, g_ple, g_fin = row(small["g_mix"]), row(small["g_mlp"]), row(small["g_ple"]), row(small["g_final"])
    sinks = small["swa_sinks"].reshape(-1)
    b_col = small["b_forget"].reshape(FOX_HEADS, 1)

    assert tm == tq
    u1, zm, zfg, zf, nrm = _in_proj(x, g_mix, w["w_all"], tm)
    f_t = zf[:, :FOX_HEADS].T
    c_pairs = _decay_cumsum(f_t, b_col).reshape(FOX_HEADS // 2, 2, n_tok)
    attn_a, lse_a = _swa_fwd(zm, sinks)
    dead = _fox_dead_steps(nrm, c_pairs, tq)
    if late_shards is None:
        attn_b, ln_b = _fox_fwd(zm, c_pairs, dead, tq)
    else:
        attn_b, ln_b, *late = _fox_fwd(zm, c_pairs, dead, tq, _gather_ride(list(late_shards.values())))
        w = {**w, **_gathered_to_local(dict(zip(late_shards, late)))}
    ya, yb, mixed, h1, u2, a, r, h2 = _mix_ffn_fwd(attn_a, attn_b, zfg, x, w["w_br_swa"], w["w_br_fox"],
                                                   w["w_mix_out"], g_mlp, w["w_ff1"], w["w_ff2"], tm // 2)

    dlg, dpp, u3, dh2, dh2b, da, loss_acc, dgf, dgp = _head_ffn_bwd(
        h2, p, tgt, a, g_ple, w["w_ple_gate"], w["w_ple_proj"], g_fin, w["w_ff2"], tm // 2)
    dh1, dh1b, dgl, dya, dyb, daa, dab, delta_b, dgm = _ffn_bwd_b(
        da, dh2, h1, ya, yb, zfg, attn_b, w["w_ff1"], g_mlp, w["w_mix_out"], w["w_br_swa"], w["w_br_fox"], tm // 2)
    dq_a, dkp, dkc, dvp, dvc, dsk = _swa_bwd(zm, sinks, daa, attn_a, lse_a)
    dw = {
        "w_br_swa": _matmul_tn(attn_a, dya, "dw_br_swa", ts, stack_cols=D_MODEL // N_DEV),
        "w_br_fox": _matmul_tn(attn_b, dyb, "dw_br_fox", ts, stack_cols=D_MODEL // N_DEV),
        "w_mix_out": _matmul_tn(mixed, dh1b, "dw_mix_out", ts),
        "w_ff1": _matmul_tn(u2, da, "dw_ff1", ts, stack_cols=D_FF // N_DEV),
        "w_ff2": _matmul_tn(r, dh2b, "dw_ff2", ts),
        "w_ple_gate": _matmul_tn(u3, dlg, "dw_ple_gate", ts),
        "w_ple_proj": _matmul_tn(p, dpp, "dw_ple_proj", ts, stack_cols=D_MODEL // N_DEV),
    }
    if late_shards is None:
        dq_b, dk_b, dv_b, cs, rs = _fox_bwd(zm, c_pairs, dead, dab, ln_b, delta_b, tq)
        late_parts = None
    else:
        wire = _local_to_wire(dw)
        dq_b, dk_b, dv_b, cs, rs, *parts = _fox_bwd(zm, c_pairs, dead, dab, ln_b, delta_b, tq,
                                                    _scatter_ride([wire[n] for n in late_shards]))
        late_parts = dict(zip(late_shards, parts))

    up = lambda t: jnp.concatenate([t[SWA_BLOCK:], jnp.zeros((SWA_BLOCK, LANES), F32)], axis=0)
    dk_a, dv_a = dkc + up(dkp), dvc + up(dvp)
    df_t, db = _decay_bwd(cs, rs, f_t, b_col)
    df = jnp.pad(df_t.T, ((0, 0), (0, N_FPAD - FOX_HEADS)))
    dz = jnp.concatenate([dq_a, dk_a.astype(BF16), dv_a.astype(BF16), dq_b.astype(BF16), dk_b.astype(BF16), dv_b.astype(BF16),
                          df.astype(BF16), dgl], axis=1)
    dw["w_all"] = _matmul_tn(u1, dz, "dw_in", ts)
    if late_shards is None:
        dx, dgx = _in_proj_bwd(dz, dh1, x, w["w_all"], g_mix, tm)
    else:
        dx, dgx, late_parts["w_in"] = _in_proj_bwd(dz, dh1, x, w["w_all"], g_mix, tm,
                                                   _scatter_ride([_dw_in_to_wire(dw["w_all"])]))
    dsmall = {"g_mix": dgx[0], "g_mlp": dgm[0], "g_ple": dgp[0], "g_final": dgf[0],
              "b_forget": db[:, 0], "swa_sinks": dsk[:, 0]}
    return loss_acc[0, 0], dx, dw, dsmall, late_parts


_ROWS = lambda t: t.reshape(-1, t.shape[-1])
_BY_ROWS = lambda t: t.reshape(N_DEV, t.shape[0] // N_DEV, t.shape[1])
_SAME = lambda t: t
LOCAL_LAYOUT = {
    "w_in": ("w_all", _w_all_from_wire, _dw_in_to_wire), "w_br_swa": ("w_br_swa", _SAME, _SAME),
    "w_br_fox": ("w_br_fox", _SAME, _SAME), "w_mix_out": ("w_mix_out", _ROWS, _BY_ROWS),
    "w_ff1": ("w_ff1", _SAME, _SAME), "w_ff2": ("w_ff2", _SAME, _BY_ROWS),
    "w_ple_gate": ("w_ple_gate", _ROWS, _BY_ROWS), "w_ple_proj": ("w_ple_proj", _SAME, _SAME),
}


def _gathered_to_local(g):
    return {LOCAL_LAYOUT[n][0]: LOCAL_LAYOUT[n][1](t) for n, t in g.items()}


def _local_to_wire(dw):
    names = {local: n for n, (local, _, _) in LOCAL_LAYOUT.items()}
    return {names[local]: LOCAL_LAYOUT[names[local]][2](t) for local, t in dw.items()}


def kernel(x, p, g_mix, w_in, b_forget, swa_sinks, w_br_swa, w_br_fox, w_mix_out, g_mlp, w_ff1, w_ff2, g_ple, w_ple_gate, w_ple_proj, g_final, loss_target, m_g_mix, m_w_in, m_b_forget, m_swa_sinks, m_w_br_swa, m_w_br_fox, m_w_mix_out, m_g_mlp, m_w_ff1, m_w_ff2, m_g_ple, m_w_ple_gate, m_w_ple_proj, m_g_final, v_g_mix, v_w_in, v_b_forget, v_swa_sinks, v_w_br_swa, v_w_br_fox, v_w_mix_out, v_g_mlp, v_w_ff1, v_w_ff2, v_g_ple, v_w_ple_gate, v_w_ple_proj, v_g_final):
    given = dict(g_mix=g_mix, w_in=w_in, b_forget=b_forget, swa_sinks=swa_sinks, w_br_swa=w_br_swa, w_br_fox=w_br_fox,
                 w_mix_out=w_mix_out, g_mlp=g_mlp, w_ff1=w_ff1, w_ff2=w_ff2, g_ple=g_ple, w_ple_gate=w_ple_gate,
                 w_ple_proj=w_ple_proj, g_final=g_final)
    mom = dict(g_mix=m_g_mix, w_in=m_w_in, b_forget=m_b_forget, swa_sinks=m_swa_sinks, w_br_swa=m_w_br_swa,
               w_br_fox=m_w_br_fox, w_mix_out=m_w_mix_out, g_mlp=m_g_mlp, w_ff1=m_w_ff1, w_ff2=m_w_ff2, g_ple=m_g_ple,
               w_ple_gate=m_w_ple_gate, w_ple_proj=m_w_ple_proj, g_final=m_g_final)
    vel = dict(g_mix=v_g_mix, w_in=v_w_in, b_forget=v_b_forget, swa_sinks=v_swa_sinks, w_br_swa=v_w_br_swa,
               w_br_fox=v_w_br_fox, w_mix_out=v_w_mix_out, g_mlp=v_g_mlp, w_ff1=v_w_ff1, w_ff2=v_w_ff2, g_ple=v_g_ple,
               w_ple_gate=v_w_ple_gate, w_ple_proj=v_w_ple_proj, g_final=v_g_final)
    names = list(given)
    sharded = list(SHARDED)

    w_wire = {n: _wire_shard(n, given[n]) for n in sharded}
    late = [n for n in sharded if n != "w_in"]
    gathered = _all_gather([w_wire["w_in"].astype(BF16)])
    local_w = _gathered_to_local({"w_in": gathered[0]})
    small = {n: given[n].reshape(-1) for n in SMALL}

    n_tok = x.shape[1]
    tile = min(TOKEN_TILE, n_tok // 4)
    loss_part, dx, dw, dsmall, parts = _local_step(
        x[0], p[0, 0], loss_target[0], local_w, small, tm=tile, tq=tile, ts=min(DW_TOKENS_PER_STEP, n_tok // 4),
        late_shards={n: w_wire[n].astype(BF16) for n in late})
    small_all = _small_exchange(_pack_small(dsmall, loss_part))

    res = {}
    for n in sharded:
        part = parts[n]
        flat = part.reshape(N_DEV, -1, part.shape[-1])
        outs = _adamw(flat, w_wire[n], _wire_shard(n, mom[n]), _wire_shard(n, vel[n]), "adamw_" + n)
        res[n] = [_from_wire(n, o) for o in outs]
    outs_s = _adamw(small_all, _pack_small(small), _pack_small({n: mom[n] for n in SMALL}),
                    _pack_small({n: vel[n] for n in SMALL}), "adamw_small")
    small_res = [_unpack_small(o, given) for o in outs_s]
    loss = outs_s[0][len(SMALL), 0]

    groups = [[res[n][k] if n in res else small_res[k][n] for n in names] for k in range(4)]
    return (loss, dx[None], *groups[0], *groups[1], *groups[2], *groups[3])
```

```python
import numpy as np
import jax
import jax.numpy as jnp
from jax import lax
from jax.experimental import pallas as pl
from jax.experimental.pallas import tpu as pltpu

F32 = jnp.float32
BF16 = jnp.bfloat16

D_MODEL = 1024
HEAD_DIM = 64
SWA_HEADS = 8
FOX_HEADS = 8
CHUNK_SHIFT = 6
SWA_BLOCK = 128
WINDOW_CHUNKS = 2
D_FF = 4096
PLE_DIM = 256
RMS_EPS = 1e-6
N_MAIN = 2304
N_FPAD = 128
N_GATE = 2048
D_IN = N_MAIN + FOX_HEADS + N_GATE
SCALE = HEAD_DIM ** -0.5
NEG = -1e30

ADAM_LR = 0.001
ADAM_B1 = 0.9
ADAM_B2 = 0.999
ADAM_EPS = 1e-08
ADAM_WD = 0.01
ADAM_STEP = 10

N_DEV = 8
TOKEN_TILE = 512
DW_TOKENS_PER_STEP = 2048
LANES = 128
V7X_VMEM_BYTES = 64 * 1024 * 1024
VMEM_LIMIT = V7X_VMEM_BYTES * 3 // 4
FOX_BWD_VMEM = V7X_VMEM_BYTES * 7 // 8
MESH = pl.DeviceIdType.MESH

_NT = (((1,), (1,)), ((), ()))
_TN = (((0,), (0,)), ((), ()))


def _params(n_grid, vmem_limit=VMEM_LIMIT):
    return pltpu.CompilerParams(dimension_semantics=("arbitrary",) * n_grid, vmem_limit_bytes=vmem_limit)


def _chunks(n, step):
    return [(s, min(step, n - s)) for s in range(0, n, step)]


def _sigmoid(x):
    return 1.0 / (1.0 + jnp.exp(-x))


def _dot(a, b):
    return jnp.dot(a, b, preferred_element_type=F32)


def _dot_nt(a, b):
    return lax.dot_general(a, b, _NT, preferred_element_type=F32)


def _dot_tn(a, b):
    return lax.dot_general(a, b, _TN, preferred_element_type=F32)


def _lane_concat(stacked_ref):
    return jnp.concatenate([stacked_ref[d] for d in range(N_DEV)], axis=1)


def _rms(h):
    return lax.rsqrt(jnp.mean(h * h, axis=-1, keepdims=True) + RMS_EPS)


def _rms_bwd(h, g, du):
    rs = _rms(h)
    n = h * rs
    dn = du * g
    dh = rs * (dn - n * jnp.mean(dn * n, axis=-1, keepdims=True))
    return dh, jnp.sum(du * n, axis=0, keepdims=True)


def _acc_rows(ref, i, row):
    @pl.when(i == 0)
    def _():
        ref[...] = jnp.zeros_like(ref)
    ref[...] += jnp.broadcast_to(row, ref.shape)


def _row_call(body, name, n_rows, tm, row_ins, const_ins, row_outs, acc_outs, ride=None, tile_outs=()):
    row_outs = list(row_outs)
    n_ri, n_ci, n_ro, n_ao = len(row_ins), len(const_ins), len(row_outs) + len(tile_outs), len(acc_outs)
    extra = ride if ride else _NO_RIDE
    n_ride = len(extra.arrays)
    grid = (n_rows // tm,)

    def kern(*refs):
        i = pl.program_id(0)
        ins, refs = refs[:n_ri + n_ci], refs[n_ri + n_ci:]
        ride_in, refs = refs[:n_ride], refs[n_ride:]
        outs, refs = refs[:n_ro + n_ao], refs[n_ro + n_ao:]
        ride_out, sems = refs[:n_ride], refs[n_ride:]
        if ride:
            ride.at_first_step(grid, ride_in, ride_out, sems)
        body(i, ins[:n_ri], ins[n_ri:], outs[:n_ro], outs[n_ro:])
        if ride:
            ride.at_last_step(grid, ride_in, ride_out, sems)

    def whole(a):
        zeros = (0,) * a.ndim
        return pl.BlockSpec(a.shape, lambda i: zeros, pipeline_mode=pl.Buffered(1))

    in_specs = [pl.BlockSpec((tm, a.shape[1]), lambda i: (i, 0)) for a in row_ins]
    in_specs += [whole(a) for a in const_ins] + extra.in_specs
    out_specs = [pl.BlockSpec((tm, c), lambda i: (i, 0)) for c, _ in row_outs]
    out_specs += [pl.BlockSpec((8, c), lambda i: (i, 0)) for c in tile_outs]
    out_specs += [pl.BlockSpec((8, c), lambda i: (0, 0)) for c in acc_outs] + extra.out_specs
    out_shape = [jax.ShapeDtypeStruct((n_rows, c), dt) for c, dt in row_outs]
    out_shape += [jax.ShapeDtypeStruct((8 * grid[0], c), F32) for c in tile_outs]
    out_shape += [jax.ShapeDtypeStruct((8, c), F32) for c in acc_outs] + extra.out_shape
    return pl.pallas_call(kern, grid=grid, in_specs=in_specs, out_specs=out_specs, out_shape=out_shape,
                          scratch_shapes=extra.scratch, name=name,
                          compiler_params=_params(1))(*row_ins, *const_ins, *extra.arrays)


def _in_proj(x, g_mix, w_all, tm):
    def body(i, ins, consts, outs, accs):
        x_ref, = ins
        g_ref, w_ref = consts
        u_ref, zm_ref, zfg_ref, zf_ref, nrm_ref = outs
        xv = x_ref[...]
        u = ((xv * _rms(xv)) * g_ref[...]).astype(BF16)
        u_ref[...] = u
        for s, n in _chunks(N_MAIN, 768):
            zm_ref[:, s:s + n] = _dot(u, w_ref[:, s:s + n]).astype(BF16)
        for s, n in _chunks(N_FPAD + N_GATE, 512):
            zfg_ref[:, s:s + n] = _dot(u, w_ref[:, N_MAIN + s:N_MAIN + s + n])
        zf_ref[...] = zfg_ref[:, :N_FPAD]
        lane = lax.broadcasted_iota(jnp.int32, (4 * LANES, LANES), 0)
        head = lax.broadcasted_iota(jnp.int32, (4 * LANES, LANES), 1)
        pick = (lane // HEAD_DIM == head).astype(BF16)
        tq_, tk_ = (zm_ref[:, col * LANES:(col + 4) * LANES].astype(F32) for col in (Q_COL, K_COL))
        rows = [jnp.max(_dot((t * t).astype(BF16), pick), axis=0, keepdims=True) for t in (tq_, tk_)]
        rows.append(jnp.min(_dot((tq_ * tk_).astype(BF16), pick), axis=0, keepdims=True))
        nrm_ref[...] = jnp.concatenate(rows + [jnp.zeros((5, LANES), F32)], axis=0)

    *outs, nrm = _row_call(body, "in_proj", x.shape[0], tm, [x], [g_mix, w_all],
                           [(D_MODEL, BF16), (N_MAIN, BF16), (N_FPAD + N_GATE, F32), (N_FPAD, F32)], [],
                           tile_outs=[LANES])
    return (*outs, nrm)


def _mix_ffn_fwd(attn_a, attn_b, zfg, x, w_sa, w_fo, w_mo, g_mlp, w1s, w2s, tm):
    ch = D_FF // N_DEV

    def body(i, ins, consts, outs, accs):
        aa_ref, ab_ref, zfg_ref, x_ref = ins
        wsa_ref, wfo_ref, wmo_ref, g_ref, w1_ref, w2_ref = consts
        ya_ref, yb_ref, mx_ref, h1_ref, u2_ref, a_ref, r_ref, h2_ref = outs
        ya = _dot(aa_ref[...], _lane_concat(wsa_ref))
        yb = _dot(ab_ref[...], _lane_concat(wfo_ref))
        g0 = _sigmoid(zfg_ref[:, N_FPAD:N_FPAD + D_MODEL])
        g1 = _sigmoid(zfg_ref[:, N_FPAD + D_MODEL:N_FPAD + 2 * D_MODEL])
        mixed = (g0 * ya + g1 * yb).astype(BF16)
        ya_ref[...] = ya.astype(BF16)
        yb_ref[...] = yb.astype(BF16)
        mx_ref[...] = mixed
        h1 = x_ref[...] + _dot(mixed, wmo_ref[...])
        h1_ref[...] = h1
        u = ((h1 * _rms(h1)) * g_ref[...]).astype(BF16)
        u2_ref[...] = u
        acc = h1
        for c in range(N_DEV):
            a = _dot(u, w1_ref[c])
            a_ref[:, c * ch:(c + 1) * ch] = a.astype(BF16)
            r = jnp.square(jnp.maximum(a, 0.0)).astype(BF16)
            r_ref[:, c * ch:(c + 1) * ch] = r
            acc = acc + _dot(r, w2_ref[c])
        h2_ref[...] = acc

    return _row_call(body, "mix_ffn_fwd", x.shape[0], tm, [attn_a, attn_b, zfg, x],
                     [w_sa, w_fo, w_mo, g_mlp, w1s, w2s],
                     [(D_MODEL, BF16), (D_MODEL, BF16), (D_MODEL, BF16), (D_MODEL, F32), (D_MODEL, BF16),
                      (D_FF, BF16), (D_FF, BF16), (D_MODEL, F32)], [])


def _head_ffn_bwd(h2, p, tgt, a, g_ple, w_pg, w_pp, g_fin, w2s, tm):
    ch = D_FF // N_DEV

    def body(i, ins, consts, outs, accs):
        h2_ref, p_ref, t_ref, a_ref = ins
        gp_ref, wpg_ref, wpp_ref, gf_ref, w2_ref = consts
        dlg_ref, dpp_ref, u3_ref, dh2_ref, dh2b_ref, da_ref = outs
        loss_ref, dgf_ref, dgp_ref = accs
        h2 = h2_ref[...]
        gp = gp_ref[...]
        u3 = ((h2 * _rms(h2)) * gp).astype(BF16)
        u3_ref[...] = u3
        pg = _sigmoid(_dot(u3, wpg_ref[...]))
        pp = _dot(p_ref[...].astype(BF16), _lane_concat(wpp_ref))
        h3 = h2 + pg * pp
        rs3 = _rms(h3)
        n3 = h3 * rs3
        gf = gf_ref[...]
        err = n3 * gf - t_ref[...]
        row_loss = 0.5 * jnp.mean(err * err, axis=-1, keepdims=True)
        _acc_rows(loss_ref, i, jnp.broadcast_to(jnp.sum(row_loss, axis=0, keepdims=True), (1, LANES)))
        dy = err * (1.0 / D_MODEL)
        _acc_rows(dgf_ref, i, jnp.sum(dy * n3, axis=0, keepdims=True))
        dn = dy * gf
        dh3 = rs3 * (dn - n3 * jnp.mean(dn * n3, axis=-1, keepdims=True))
        dpp_ref[...] = (dh3 * pg).astype(BF16)
        dlg = ((dh3 * pp) * pg * (1.0 - pg)).astype(BF16)
        dlg_ref[...] = dlg
        dh, dg = _rms_bwd(h2, gp, _dot_nt(dlg, wpg_ref[...]))
        _acc_rows(dgp_ref, i, dg)
        dh2 = dh3 + dh
        dh2_ref[...] = dh2
        dh2b = dh2.astype(BF16)
        dh2b_ref[...] = dh2b
        for c in range(N_DEV):
            dr = _dot_nt(dh2b, w2_ref[c])
            av = a_ref[:, c * ch:(c + 1) * ch].astype(F32)
            da_ref[:, c * ch:(c + 1) * ch] = (dr * (2.0 * jnp.maximum(av, 0.0))).astype(BF16)

    return _row_call(body, "head_ffn_bwd", h2.shape[0], tm, [h2, p, tgt, a], [g_ple, w_pg, w_pp, g_fin, w2s],
                     [(D_MODEL, BF16), (D_MODEL, BF16), (D_MODEL, BF16), (D_MODEL, F32), (D_MODEL, BF16),
                      (D_FF, BF16)], [LANES, D_MODEL, D_MODEL])


def _ffn_bwd_b(da, dh2, h1, ya, yb, zfg, attn_b, w1s, g_mlp, w_mo, w_sa, w_fo, tm):
    ch = D_FF // N_DEV

    def body(i, ins, consts, outs, accs):
        da_ref, dh2_ref, h1_ref, ya_ref, yb_ref, zfg_ref, ob_ref = ins
        w1_ref, gm_ref, wmo_ref, wsa_ref, wfo_ref = consts
        dh1_ref, dh1b_ref, dgl_ref, dya_ref, dyb_ref, daa_ref, dab_ref, dl_ref = outs
        dgm_ref, = accs
        du2 = _dot_nt(da_ref[:, 0:ch], w1_ref[0])
        for c in range(1, N_DEV):
            du2 = du2 + _dot_nt(da_ref[:, c * ch:(c + 1) * ch], w1_ref[c])
        dh, dg = _rms_bwd(h1_ref[...], gm_ref[...], du2)
        _acc_rows(dgm_ref, i, dg)
        dh1 = dh2_ref[...] + dh
        dh1_ref[...] = dh1
        dh1b = dh1.astype(BF16)
        dh1b_ref[...] = dh1b
        dmx = _dot_nt(dh1b, wmo_ref[...])
        g0 = _sigmoid(zfg_ref[:, N_FPAD:N_FPAD + D_MODEL])
        g1 = _sigmoid(zfg_ref[:, N_FPAD + D_MODEL:N_FPAD + 2 * D_MODEL])
        dya = (dmx * g0).astype(BF16)
        dyb = (dmx * g1).astype(BF16)
        dya_ref[...] = dya
        dyb_ref[...] = dyb
        dgl_ref[:, 0:D_MODEL] = ((dmx * ya_ref[...].astype(F32)) * g0 * (1.0 - g0)).astype(BF16)
        dgl_ref[:, D_MODEL:2 * D_MODEL] = ((dmx * yb_ref[...].astype(F32)) * g1 * (1.0 - g1)).astype(BF16)
        daa_ref[...] = _dot_nt(dya, _lane_concat(wsa_ref)).astype(BF16)
        dab = _dot_nt(dyb, _lane_concat(wfo_ref)).astype(BF16)
        dab_ref[...] = dab
        half_in = lax.broadcasted_iota(jnp.int32, (LANES, 2 * LANES), 0) // HEAD_DIM
        half_out = lax.broadcasted_iota(jnp.int32, (LANES, 2 * LANES), 1) // LANES
        pick = (half_in == half_out).astype(BF16)
        for pair in range(FOX_HEADS // 2):
            cols = slice(pair * LANES, (pair + 1) * LANES)
            prod = dab[:, cols].astype(F32) * ob_ref[:, cols].astype(F32)
            hi = prod.astype(BF16)
            lo_part = (prod - hi.astype(F32)).astype(BF16)
            dl_ref[:, 2 * pair * LANES:(2 * pair + 2) * LANES] = _dot(hi, pick) + _dot(lo_part, pick)

    half = D_MODEL // 2
    return _row_call(body, "ffn_bwd_b", h1.shape[0], tm, [da, dh2, h1, ya, yb, zfg, attn_b],
                     [w1s, g_mlp, w_mo, w_sa, w_fo],
                     [(D_MODEL, F32), (D_MODEL, BF16), (N_GATE, BF16), (D_MODEL, BF16), (D_MODEL, BF16),
                      (half, BF16), (half, BF16), (FOX_HEADS * LANES, F32)], [D_MODEL])


def _in_proj_bwd(dz, dh1, x, w_all, g_mix, tm, ride=None):
    def body(i, ins, consts, outs, accs):
        dz_ref, dh1_ref, x_ref = ins
        w_ref, g_ref = consts
        dx_ref, = outs
        dgx_ref, = accs
        du1 = _dot_nt(dz_ref[...], w_ref[...])
        dh, dg = _rms_bwd(x_ref[...], g_ref[...], du1)
        _acc_rows(dgx_ref, i, dg)
        dx_ref[...] = dh1_ref[...] + dh

    return _row_call(body, "in_proj_bwd", x.shape[0], tm, [dz, dh1, x], [w_all, g_mix],
                     [(D_MODEL, F32)], [D_MODEL], ride)


def _matmul_tn(a, b, name, ts, stack_cols=0):
    n_rows, ka = a.shape
    n = b.shape[1]
    tk = min(ka, 1024)
    tn = 896 if n % 1024 else 1024
    n_stack = tn // stack_cols if stack_cols else 0
    assert ka % tk == 0 and n % tn == 0 and n_rows % ts == 0 and (not stack_cols or tk == ka)
    n_steps = n_rows // ts

    def kern(a_ref, b_ref, o_ref, acc_ref):
        s = pl.program_id(2)

        @pl.when(s == 0)
        def _():
            acc_ref[...] = jnp.zeros_like(acc_ref)
        acc_ref[...] += _dot_tn(a_ref[...].astype(BF16), b_ref[...])

        @pl.when(s == n_steps - 1)
        def _():
            if stack_cols:
                for c in range(n_stack):
                    o_ref[c] = acc_ref[:, c * stack_cols:(c + 1) * stack_cols].astype(BF16)
            else:
                o_ref[...] = acc_ref[...].astype(BF16)

    if stack_cols:
        out_spec = pl.BlockSpec((n_stack, tk, stack_cols), lambda i, j, s: (j, 0, 0))
        out_shape = jax.ShapeDtypeStruct((n // stack_cols, ka, stack_cols), BF16)
    else:
        out_spec = pl.BlockSpec((tk, tn), lambda i, j, s: (i, j))
        out_shape = jax.ShapeDtypeStruct((ka, n), BF16)
    return pl.pallas_call(
        kern, grid=(ka // tk, n // tn, n_steps),
        in_specs=[pl.BlockSpec((ts, tk), lambda i, j, s: (s, i)), pl.BlockSpec((ts, tn), lambda i, j, s: (s, j))],
        out_specs=out_spec, out_shape=out_shape, scratch_shapes=[pltpu.VMEM((tk, tn), F32)], name=name,
        compiler_params=_params(3))(a, b)


SCAN_CHUNK = 512
BWD_SCAN_CHUNK = 1024


def _decay_cumsum(f_t, b_col):
    n_tok = f_t.shape[1]
    ch = min(SCAN_CHUNK, n_tok)

    def kern(f_ref, b_ref, c_ref):
        r = lax.broadcasted_iota(jnp.int32, (ch, ch), 0)
        c = lax.broadcasted_iota(jnp.int32, (ch, ch), 1)
        tri = (r <= c).astype(F32)
        carry = jnp.zeros((8, 1), F32)
        for k in range(n_tok // ch):
            xv = f_ref[:, k * ch:(k + 1) * ch] + b_ref[...]
            lf = jnp.minimum(xv, 0.0) - jnp.log(1.0 + jnp.exp(-jnp.abs(xv)))
            cs = jnp.dot(lf, tri, precision=lax.Precision.HIGHEST, preferred_element_type=F32) + carry
            c_ref[:, k * ch:(k + 1) * ch] = cs
            carry = cs[:, ch - 1:ch]

    return pl.pallas_call(kern, out_shape=jax.ShapeDtypeStruct((8, n_tok), F32), name="decay_cumsum",
                          compiler_params=_params(0))(f_t, b_col)


def _decay_bwd(cs, rs, f_t, b_col):
    n_tok = f_t.shape[1]
    ch = min(BWD_SCAN_CHUNK, n_tok)
    n_ch = n_tok // ch

    def kern(cs_ref, rs_ref, f_ref, b_ref, df_ref, db_ref, carry_ref):
        k = pl.program_id(0)

        @pl.when(k == 0)
        def _():
            carry_ref[...] = jnp.zeros_like(carry_ref)
            db_ref[...] = jnp.zeros_like(db_ref)

        r = lax.broadcasted_iota(jnp.int32, (ch, ch), 0)
        c = lax.broadcasted_iota(jnp.int32, (ch, ch), 1)
        tri = (r >= c).astype(F32)
        head = lax.broadcasted_iota(jnp.int32, (8, 4 * LANES), 0)
        lane = lax.broadcasted_iota(jnp.int32, (8, 4 * LANES), 1)
        pick = (lane == HEAD_DIM * head).astype(F32)
        dc = lax.dot_general(pick, rs_ref[...] - cs_ref[...], _NT, precision=lax.Precision.HIGHEST,
                             preferred_element_type=F32)
        rc = jnp.dot(dc, tri, precision=lax.Precision.HIGHEST, preferred_element_type=F32) + carry_ref[:, 0:1]
        carry_ref[...] = jnp.broadcast_to(rc[:, 0:1], carry_ref.shape)
        df = rc / (1.0 + jnp.exp(f_ref[...] + b_ref[...]))
        df_ref[...] = df
        db_ref[...] += jnp.broadcast_to(jnp.sum(df, axis=1, keepdims=True), db_ref.shape)

    back = lambda k: n_ch - 1 - k
    wide = pl.BlockSpec((ch, 4 * LANES), lambda k: (back(k), 0))
    row = pl.BlockSpec((8, ch), lambda k: (0, back(k)))
    return pl.pallas_call(
        kern, grid=(n_ch,),
        in_specs=[wide, wide, row, pl.BlockSpec((8, 1), lambda k: (0, 0))],
        out_specs=[row, pl.BlockSpec((8, LANES), lambda k: (0, 0))],
        out_shape=[jax.ShapeDtypeStruct((8, n_tok), F32), jax.ShapeDtypeStruct((8, LANES), F32)],
        scratch_shapes=[pltpu.VMEM((8, LANES), F32)], name="decay_bwd", compiler_params=_params(1))(cs, rs, f_t, b_col)


def _swa_bias_table():
    row = jnp.arange(SWA_BLOCK)[:, None] + SWA_BLOCK
    col = jnp.arange(2 * SWA_BLOCK)[None, :]
    cd = (row >> CHUNK_SHIFT) - (col >> CHUNK_SHIFT)
    band = (cd >= 0) & (cd <= WINDOW_CHUNKS)
    slopes = jnp.asarray([2.0 ** -(h + 1) for h in range(SWA_HEADS)], F32)
    bias = -slopes[:, None, None] * jnp.abs(row - col).astype(F32)[None]
    return jnp.stack([jnp.where(band & (col >= SWA_BLOCK), bias, NEG), jnp.where(band, bias, NEG)])


SWA_BIAS_SPEC = pl.BlockSpec((None, SWA_HEADS, SWA_BLOCK, 2 * SWA_BLOCK), lambda n: (jnp.minimum(n, 1), 0, 0, 0))


def _swap_halves(t):
    return pltpu.roll(t.astype(F32), HEAD_DIM, axis=1).astype(t.dtype)


def _swa_specs():
    blk = SWA_BLOCK
    q = pl.BlockSpec((blk, 4 * LANES), lambda n: (n, 0))
    kp = pl.BlockSpec((blk, LANES), lambda n: (jnp.maximum(n - 1, 0), 4))
    kc = pl.BlockSpec((blk, LANES), lambda n: (n, 4))
    vp = pl.BlockSpec((blk, LANES), lambda n: (jnp.maximum(n - 1, 0), 5))
    vc = pl.BlockSpec((blk, LANES), lambda n: (n, 5))
    return q, kp, kc, vp, vc


SWA_GROUPS = ([h for h in range(SWA_HEADS) if h % 2 == h // 4], [h for h in range(SWA_HEADS) if h % 2 != h // 4])


def _stack_heads(ref, heads, lo, mask_halves):
    tiles = []
    for h in heads:
        t = ref[:, (h // 2) * LANES:(h // 2 + 1) * LANES]
        tiles.append(jnp.where(lo if h % 2 == 0 else ~lo, t, jnp.zeros_like(t)) if mask_halves else t)
    return jnp.concatenate(tiles, axis=0)


def _per_head_column(values, heads):
    return jnp.concatenate([jnp.full((SWA_BLOCK, 1), values(h), F32) for h in heads], axis=0)


def _swa_scores(q_ref, kx, heads, lo, bias_ref):
    qa = _stack_heads(q_ref, heads, lo, True) * SCALE
    return qa, _dot_nt(qa, kx) + jnp.concatenate([bias_ref[h] for h in heads], axis=0)


def _swa_fwd(zm, sinks):
    n_tok = zm.shape[0]
    blk = SWA_BLOCK

    def kern(q_ref, kp_ref, kc_ref, vp_ref, vc_ref, bias_ref, sink_ref, o_ref, lse_ref):
        k2 = jnp.concatenate([kp_ref[...], kc_ref[...]], axis=0)
        v2 = jnp.concatenate([vp_ref[...], vc_ref[...]], axis=0)
        ksw, vsw = _swap_halves(k2), _swap_halves(v2)
        lane = lax.broadcasted_iota(jnp.int32, (blk, LANES), 1)
        lo = lane < HEAD_DIM
        lse_t = jnp.zeros((blk, LANES), F32)
        for pair in range(SWA_HEADS // 2):
            q2 = q_ref[:, pair * LANES:(pair + 1) * LANES]
            outs = []
            for a in range(2):
                h = 2 * pair + a
                qa = jnp.where(lo if a == 0 else ~lo, q2, jnp.zeros_like(q2)) * SCALE
                kx, vx = (k2, v2) if h in SWA_GROUPS[0] else (ksw, vsw)
                s = _dot_nt(qa, kx) + bias_ref[h]
                sink = sink_ref[h]
                m = jnp.maximum(jnp.max(s, axis=-1, keepdims=True), sink)
                e = jnp.exp(s - m)
                l = jnp.sum(e, axis=-1, keepdims=True) + jnp.exp(sink - m)
                pn = (e * (1.0 / l)).astype(BF16)
                outs.append(_dot(pn, vx))
                lse_t = jnp.where(lane == h, m + jnp.log(l), lse_t)
            o_ref[:, pair * LANES:(pair + 1) * LANES] = jnp.where(lo, outs[0], outs[1]).astype(BF16)
        lse_ref[...] = lse_t

    q, kp, kc, vp, vc = _swa_specs()
    return pl.pallas_call(
        kern, grid=(n_tok // blk,),
        in_specs=[q, kp, kc, vp, vc, SWA_BIAS_SPEC, pl.BlockSpec(memory_space=pltpu.SMEM)],
        out_specs=[pl.BlockSpec((blk, 4 * LANES), lambda n: (n, 0)), pl.BlockSpec((blk, LANES), lambda n: (n, 0))],
        out_shape=[jax.ShapeDtypeStruct((n_tok, 4 * LANES), BF16), jax.ShapeDtypeStruct((n_tok, LANES), F32)],
        name="swa_fwd", compiler_params=_params(1))(zm, zm, zm, zm, zm, _swa_bias_table(), sinks)


def _swa_bwd(zm, sinks, d_out, out, lse):
    n_tok = zm.shape[0]
    blk = SWA_BLOCK

    def kern(q_ref, kp_ref, kc_ref, vp_ref, vc_ref, bias_ref, do_ref, o_ref, lse_ref, sink_ref,
             dq_ref, dkp_ref, dkc_ref, dvp_ref, dvc_ref, dsk_ref):
        n = pl.program_id(0)

        @pl.when(n == 0)
        def _():
            dsk_ref[...] = jnp.zeros_like(dsk_ref)

        k2 = jnp.concatenate([kp_ref[...], kc_ref[...]], axis=0)
        v2 = jnp.concatenate([vp_ref[...], vc_ref[...]], axis=0)
        lane = lax.broadcasted_iota(jnp.int32, (blk, LANES), 1)
        lo = lane < HEAD_DIM
        lse_t = lse_ref[...]
        dqs, dkv = {}, []
        for heads, kx, vx in ((SWA_GROUPS[0], k2, v2), (SWA_GROUPS[1], _swap_halves(k2), _swap_halves(v2))):
            qa, s = _swa_scores(q_ref, kx, heads, lo, bias_ref)
            doa = _stack_heads(do_ref, heads, lo, True)
            lse_g = jnp.concatenate([lse_t[:, h:h + 1] for h in heads], axis=0)
            prob = jnp.exp(s - lse_g)
            dd = jnp.sum(doa.astype(F32) * _stack_heads(o_ref, heads, lo, False).astype(F32), axis=-1, keepdims=True)
            ds = (prob * (_dot_nt(doa, vx) - dd)).astype(BF16)
            sink_part = -jnp.exp(_per_head_column(lambda h: sink_ref[h], heads) - lse_g) * dd
            dq = _dot(ds, kx) * SCALE
            for r, h in enumerate(heads):
                dqs[h] = dq[r * blk:(r + 1) * blk]
                dsk_ref[h:h + 1, :] += jnp.broadcast_to(
                    jnp.sum(sink_part[r * blk:(r + 1) * blk], axis=0, keepdims=True), (1, LANES))
            dkv.append((_dot_tn(ds, qa), _dot_tn(prob.astype(BF16), doa)))
        for pair in range(SWA_HEADS // 2):
            dq_ref[:, pair * LANES:(pair + 1) * LANES] = jnp.where(lo, dqs[2 * pair], dqs[2 * pair + 1]).astype(BF16)
        dk = dkv[0][0] + pltpu.roll(dkv[1][0], HEAD_DIM, axis=1)
        dv = dkv[0][1] + pltpu.roll(dkv[1][1], HEAD_DIM, axis=1)
        dkp_ref[...] = dk[0:blk]
        dkc_ref[...] = dk[blk:2 * blk]
        dvp_ref[...] = dv[0:blk]
        dvc_ref[...] = dv[blk:2 * blk]

    q, kp, kc, vp, vc = _swa_specs()
    wide = pl.BlockSpec((blk, 4 * LANES), lambda n: (n, 0))
    narrow = pl.BlockSpec((blk, LANES), lambda n: (n, 0))
    part = jax.ShapeDtypeStruct((n_tok, LANES), F32)
    return pl.pallas_call(
        kern, grid=(n_tok // blk,),
        in_specs=[q, kp, kc, vp, vc, SWA_BIAS_SPEC, wide, wide, narrow, pl.BlockSpec(memory_space=pltpu.SMEM)],
        out_specs=[wide, narrow, narrow, narrow, narrow, pl.BlockSpec((8, LANES), lambda n: (0, 0))],
        out_shape=[jax.ShapeDtypeStruct((n_tok, 4 * LANES), BF16), part, part, part, part,
                   jax.ShapeDtypeStruct((8, LANES), F32)],
        name="swa_bwd", compiler_params=_params(1))(zm, zm, zm, zm, zm, _swa_bias_table(), d_out, out, lse, sinks)


def _my_pos():
    return lax.axis_index("x"), lax.axis_index("y"), lax.axis_index("c")


def _peer(k):
    x, y, c = _my_pos()
    px, py, pc = x ^ (k >> 2), y ^ ((k >> 1) & 1), c ^ (k & 1)
    return (px, py, pc), 4 * px + 2 * py + pc


def _gather_copies(x_refs, out_refs, send_sems, recv_sems, local_sems):
    x, y, c = _my_pos()
    my_id = 4 * x + 2 * y + c
    local = [pltpu.make_async_copy(x_refs[w], out_refs[w].at[my_id], local_sems.at[w]) for w in range(len(x_refs))]
    sends, arrivals = [], []
    for k in range(1, N_DEV):
        peer, peer_id = _peer(k)
        for w in range(len(x_refs)):
            sems = dict(send_sem=send_sems.at[7 * w + k - 1], recv_sem=recv_sems.at[7 * w + k - 1],
                        device_id=peer, device_id_type=MESH)
            sends.append(pltpu.make_async_remote_copy(src_ref=x_refs[w], dst_ref=out_refs[w].at[my_id], **sems))
            arrivals.append(pltpu.make_async_remote_copy(src_ref=x_refs[w], dst_ref=out_refs[w].at[peer_id], **sems))
    return local, sends, arrivals


def _scatter_copies(g_refs, part_refs, send_sems, recv_sems, local_sems):
    x, y, c = _my_pos()
    my_id = 4 * x + 2 * y + c
    local = [pltpu.make_async_copy(g_refs[w].at[my_id], part_refs[w].at[0], local_sems.at[w])
             for w in range(len(g_refs))]
    sends, arrivals = [], []
    for k in range(1, N_DEV):
        peer, peer_id = _peer(k)
        for w in range(len(g_refs)):
            sems = dict(send_sem=send_sems.at[7 * w + k - 1], recv_sem=recv_sems.at[7 * w + k - 1],
                        device_id=peer, device_id_type=MESH)
            sends.append(pltpu.make_async_remote_copy(src_ref=g_refs[w].at[peer_id], dst_ref=part_refs[w].at[k], **sems))
            arrivals.append(pltpu.make_async_remote_copy(src_ref=g_refs[w].at[my_id], dst_ref=part_refs[w].at[k], **sems))
    return local, sends, arrivals


def _start_copies(local, sends, arrivals):
    for cp in local + sends:
        cp.start()


def _finish_copies(local, sends, arrivals):
    for cp in arrivals:
        cp.wait_recv()
    for cp in sends:
        cp.wait_send()
    for cp in local:
        cp.wait()


def _exchange_scratch(n_arrays):
    return [pltpu.SemaphoreType.DMA((7 * n_arrays,)), pltpu.SemaphoreType.DMA((7 * n_arrays,)),
            pltpu.SemaphoreType.DMA((n_arrays,))]


class _Ride:
    def __init__(self, arrays, out_shape, copies):
        self.arrays, self.out_shape, self.copies = list(arrays), list(out_shape), copies
        any_spec = pl.BlockSpec(memory_space=pl.ANY)
        self.in_specs = [any_spec] * len(self.arrays)
        self.out_specs = [any_spec] * len(self.arrays)
        self.scratch = _exchange_scratch(len(self.arrays)) if self.arrays else []

    @staticmethod
    def _at(grid, last):
        hit = [pl.program_id(d) == (n - 1 if last else 0) for d, n in enumerate(grid)]
        return hit[0] if len(hit) == 1 else jnp.logical_and(*hit)

    def at_first_step(self, grid, in_refs, out_refs, sems):
        @pl.when(self._at(grid, False))
        def _():
            _start_copies(*self.copies(in_refs, out_refs, *sems))

    def at_last_step(self, grid, in_refs, out_refs, sems):
        @pl.when(self._at(grid, True))
        def _():
            _finish_copies(*self.copies(in_refs, out_refs, *sems))


_NO_RIDE = _Ride([], [], None)


def _gather_ride(shards):
    return _Ride(shards, [jax.ShapeDtypeStruct((N_DEV,) + s.shape, s.dtype) for s in shards], _gather_copies)


def _scatter_ride(grads):
    return _Ride(grads, [jax.ShapeDtypeStruct(g.shape, g.dtype) for g in grads], _scatter_copies)


Q_COL, K_COL, V_COL = 6, 10, 14


def _causal(t, tq, tk):
    row = lax.broadcasted_iota(jnp.int32, (tq, tk), 0)
    col = lax.broadcasted_iota(jnp.int32, (tq, tk), 1)
    return jnp.where(col <= row, t, NEG)


def _lane_tile(stat, width):
    return jnp.tile(stat, (1, width // LANES))


def _fox_steps(nq):
    steps = [(i2, j, 0 if j < 2 * i2 else 1 + j - 2 * i2) for i2 in range(nq // 2) for j in range(2 * i2 + 2)]
    return [np.asarray(col, np.int32) for col in zip(*steps)]


_SWEEPS = {0: [(0, False), (1, False)], 1: [(0, True), (1, False)], 2: [(1, True)]}


def _fox_dispatch(sweep, kind, dead_ref, head0, idx):
    dead0, dead1 = dead_ref[head0, idx] > 0.5, dead_ref[head0 + 1, idx] > 0.5
    live0, live1 = jnp.logical_not(dead0), jnp.logical_not(dead1)
    below = kind == 0
    pl.when(jnp.logical_and(below, jnp.logical_and(live0, live1)))(lambda: sweep(_SWEEPS[0], (0, 1)))
    pl.when(jnp.logical_and(below, jnp.logical_and(live0, dead1)))(lambda: sweep(_SWEEPS[0], (0,)))
    pl.when(jnp.logical_and(below, jnp.logical_and(dead0, live1)))(lambda: sweep(_SWEEPS[0], (1,)))
    pl.when(kind == 1)(lambda: sweep(_SWEEPS[1], (0, 1)))
    pl.when(kind == 2)(lambda: sweep(_SWEEPS[2], (0, 1)))


EXP_ZERO = 110.0
NORM_SLACK = 1.005


def _fox_dead_steps(nrm, c_pairs, tq):
    nq = nrm.shape[0] // 8
    stats = nrm.reshape(nq, 8, LANES)[:, :3, :FOX_HEADS]
    qn, kn, own = jnp.sqrt(stats[:, 0]) * SCALE, jnp.sqrt(stats[:, 1]), stats[:, 2] * SCALE
    cb = c_pairs.reshape(FOX_HEADS, nq, tq)
    c_max, c_min = jnp.max(cb, axis=-1).T, jnp.min(cb, axis=-1).T
    both = lambda t, pick: pick(t.reshape(nq // 2, 2, FOX_HEADS), axis=1)
    qn2, kn2, c_max2, own2 = both(qn, jnp.max), both(kn, jnp.max), both(c_max, jnp.max), both(own, jnp.min)
    row_max_floor = own2 - (NORM_SLACK - 1.0) * qn2 * kn2 - c_max2
    gap = qn2[:, None] * kn[None] * NORM_SLACK - c_min[None] - row_max_floor[:, None]
    below = jnp.arange(nq)[None, :] < 2 * jnp.arange(nq // 2)[:, None]
    dead = jnp.logical_and(gap < -EXP_ZERO, below[..., None])
    return dead.transpose(2, 0, 1).reshape(FOX_HEADS, -1).astype(F32)


def _fox_fwd(zm, c_pairs, dead, tq, ride=None):
    n_tok = zm.shape[0]
    nq = n_tok // tq
    ii, jj, kk = _fox_steps(nq)
    n_steps = len(ii)
    n_ride = len(ride.arrays) if ride else 0

    def kern(ii_ref, jj_ref, kk_ref, q_ref, k_ref, v_ref, ck_ref, dead_ref, *more):
        ride_in, (o_ref, ln_ref), ride_out = more[:n_ride], more[n_ride:n_ride + 2], more[n_ride + 2:2 * n_ride + 2]
        qs_ref, m_ref, l_ref, acc_ref = more[2 * n_ride + 2:2 * n_ride + 6]
        step = pl.program_id(1)
        j, kind = jj_ref[step], kk_ref[step]
        lo = lax.broadcasted_iota(jnp.int32, (2 * tq, LANES), 1) < HEAD_DIM
        if ride:
            ride.at_first_step((FOX_HEADS // 2, n_steps), ride_in, ride_out, more[2 * n_ride + 6:])

        @pl.when(j == 0)
        def _():
            q2 = q_ref[...]
            zq = jnp.zeros_like(q2)
            qs_ref[0] = jnp.where(lo, q2, zq) * SCALE
            qs_ref[1] = jnp.where(lo, zq, q2) * SCALE
            m_ref[...] = jnp.full(m_ref.shape, NEG, F32)
            l_ref[...] = jnp.zeros(l_ref.shape, F32)
            acc_ref[...] = jnp.zeros(acc_ref.shape, F32)

        def sweep(subs, heads):
            kv = k_ref[...]
            v_ones = jnp.concatenate([v_ref[...], jnp.ones((tq, LANES), BF16)], axis=1)
            for sub, diag in subs:
                rows = slice(sub * tq, (sub + 1) * tq)
                for a in heads:
                    t = _dot_nt(qs_ref[a, rows], kv) - ck_ref[a:a + 1, :]
                    if diag:
                        t = _causal(t, tq, tq)
                    m_old = m_ref[a, rows]
                    m_new = jnp.maximum(m_old, jnp.max(t, axis=-1, keepdims=True))
                    alpha = jnp.exp(m_old - m_new)
                    e = jnp.exp(t - _lane_tile(m_new, tq)).astype(BF16)
                    pv = _dot(e, v_ones)
                    acc_ref[a, rows] = alpha * acc_ref[a, rows] + pv[:, :LANES]
                    l_ref[a, rows] = alpha * l_ref[a, rows] + pv[:, LANES:]
                    m_ref[a, rows] = m_new

        _fox_dispatch(sweep, kind, dead_ref, 2 * pl.program_id(0), ii_ref[step] * nq + j)

        @pl.when(kind == 2)
        def _():
            o_ref[...] = jnp.where(lo, acc_ref[0] / l_ref[0], acc_ref[1] / l_ref[1]).astype(BF16)
            ln_ref[:, :LANES] = m_ref[0] + jnp.log(l_ref[0])
            ln_ref[:, LANES:] = m_ref[1] + jnp.log(l_ref[1])

        if ride:
            ride.at_last_step((FOX_HEADS // 2, n_steps), ride_in, ride_out, more[2 * n_ride + 6:])

    blk = (tq, LANES)
    by_i = lambda col: (lambda hp, s, ii, jj, kk: (ii[s], col + hp))
    by_j = lambda col: (lambda hp, s, ii, jj, kk: (jj[s], col + hp))
    extra = ride if ride else _NO_RIDE
    grid_spec = pltpu.PrefetchScalarGridSpec(
        num_scalar_prefetch=3, grid=(FOX_HEADS // 2, n_steps),
        in_specs=[pl.BlockSpec((2 * tq, LANES), by_i(Q_COL)), pl.BlockSpec(blk, by_j(K_COL)),
                  pl.BlockSpec(blk, by_j(V_COL)),
                  pl.BlockSpec((None, 2, tq), lambda hp, s, ii, jj, kk: (hp, 0, jj[s])),
                  pl.BlockSpec(memory_space=pltpu.SMEM)] + extra.in_specs,
        out_specs=[pl.BlockSpec((2 * tq, LANES), by_i(0)), pl.BlockSpec((2 * tq, 2 * LANES), by_i(0))] + extra.out_specs,
        scratch_shapes=[pltpu.VMEM((2, 2 * tq, LANES), BF16), pltpu.VMEM((2, 2 * tq, LANES), F32),
                        pltpu.VMEM((2, 2 * tq, LANES), F32), pltpu.VMEM((2, 2 * tq, LANES), F32)] + extra.scratch)
    return pl.pallas_call(
        kern, grid_spec=grid_spec,
        out_shape=[jax.ShapeDtypeStruct((n_tok, 4 * LANES), BF16),
                   jax.ShapeDtypeStruct((n_tok, FOX_HEADS * LANES), F32)] + extra.out_shape,
        name="fox_fwd", compiler_params=_params(2))(ii, jj, kk, zm, zm, zm, c_pairs, dead, *extra.arrays)


def _fox_bwd(zm, c_pairs, dead, d_out, lnorm, delta, tq, ride=None):
    n_tok = zm.shape[0]
    nq = n_tok // tq
    ii, jj, kk = _fox_steps(nq)
    n_steps = len(ii)
    n_ride = len(ride.arrays) if ride else 0

    def kern(ii_ref, jj_ref, kk_ref, q_ref, k_ref, v_ref, ck_ref, dead_ref, do_ref, ln_ref, dl_ref, *more):
        ride_in, ride_out = more[:n_ride], more[n_ride + 5:2 * n_ride + 5]
        dq_ref, dk_out, dv_out, cs_ref, rs_ref = more[n_ride:n_ride + 5]
        qs_ref, qo_ref, dos_ref, dq_acc, dk_ref, dv_ref = more[2 * n_ride + 5:2 * n_ride + 11]
        step = pl.program_id(1)
        j, kind = jj_ref[step], kk_ref[step]
        lo = lax.broadcasted_iota(jnp.int32, (2 * tq, LANES), 1) < HEAD_DIM
        if ride:
            ride.at_first_step((FOX_HEADS // 2, n_steps), ride_in, ride_out, more[2 * n_ride + 11:])

        @pl.when(step == 0)
        def _():
            dk_ref[...] = jnp.zeros_like(dk_ref)
            dv_ref[...] = jnp.zeros_like(dv_ref)
            cs_ref[...] = jnp.zeros_like(cs_ref)

        @pl.when(j == 0)
        def _():
            q2, do2 = q_ref[...], do_ref[...]
            zq = jnp.zeros_like(q2)
            ones = jnp.ones((2 * tq, LANES), BF16)
            for a in range(2):
                half = lo if a == 0 else ~lo
                qa = jnp.where(half, q2, zq) * SCALE
                qs_ref[a] = qa
                qo_ref[a] = jnp.concatenate([qa, ones], axis=1)
                dos_ref[a] = jnp.where(half, do2, zq)
            dq_acc[...] = jnp.zeros(dq_acc.shape, F32)

        def sweep(subs, heads):
            kv, vv = k_ref[...], v_ref[...]
            k_ones = jnp.concatenate([kv, jnp.ones((tq, LANES), BF16)], axis=1)
            dk, dv, sums = None, None, {}
            for sub, diag in subs:
                rows = slice(sub * tq, (sub + 1) * tq)
                for a in heads:
                    t = _dot_nt(qs_ref[a, rows], kv) - ck_ref[a:a + 1, :]
                    if diag:
                        t = _causal(t, tq, tq)
                    prob = jnp.exp(t - _lane_tile(ln_ref[rows, a * LANES:(a + 1) * LANES], tq))
                    dp = _dot_nt(dos_ref[a, rows], vv)
                    ds = (prob * (dp - _lane_tile(dl_ref[rows, a * LANES:(a + 1) * LANES], tq))).astype(BF16)
                    dq_acc[a, rows] += _dot(ds, k_ones)
                    dk_cs = _dot_tn(ds, qo_ref[a, rows])
                    dv_a = _dot_tn(prob.astype(BF16), dos_ref[a, rows])
                    dk = dk_cs[:, :LANES] if dk is None else dk + dk_cs[:, :LANES]
                    dv = dv_a if dv is None else dv + dv_a
                    sums[a] = dk_cs[:, LANES:] if a not in sums else sums[a] + dk_cs[:, LANES:]
            keys = pl.ds(pl.multiple_of(j * tq, tq), tq)
            dk_ref[keys, :] += dk
            cs_ref[keys, :] += jnp.where(lo[:tq], sums.get(0, 0.0), sums.get(1, 0.0))
            dv_ref[keys, :] += dv

        _fox_dispatch(sweep, kind, dead_ref, 2 * pl.program_id(0), ii_ref[step] * nq + j)

        @pl.when(kind == 2)
        def _():
            dq_ref[...] = (jnp.where(lo, dq_acc[0, :, :LANES], dq_acc[1, :, :LANES]) * SCALE).astype(BF16)
            rs_ref[...] = jnp.where(lo, dq_acc[0, :, LANES:], dq_acc[1, :, LANES:])

        @pl.when(step == n_steps - 1)
        def _():
            dk_out[...] = dk_ref[...].astype(BF16)
            dv_out[...] = dv_ref[...].astype(BF16)

        if ride:
            ride.at_last_step((FOX_HEADS // 2, n_steps), ride_in, ride_out, more[2 * n_ride + 11:])

    blk = (tq, LANES)
    by_i = lambda col: (lambda hp, s, ii, jj, kk: (ii[s], col + hp))
    by_j = lambda col: (lambda hp, s, ii, jj, kk: (jj[s], col + hp))
    resident = pl.BlockSpec((2 * tq, LANES), by_i(0))
    stat = pl.BlockSpec((2 * tq, 2 * LANES), by_i(0))
    whole = pl.BlockSpec((n_tok, LANES), lambda hp, s, ii, jj, kk: (0, hp))
    extra = ride if ride else _NO_RIDE
    grid_spec = pltpu.PrefetchScalarGridSpec(
        num_scalar_prefetch=3, grid=(FOX_HEADS // 2, n_steps),
        in_specs=[pl.BlockSpec((2 * tq, LANES), by_i(Q_COL)), pl.BlockSpec(blk, by_j(K_COL)),
                  pl.BlockSpec(blk, by_j(V_COL)),
                  pl.BlockSpec((None, 2, tq), lambda hp, s, ii, jj, kk: (hp, 0, jj[s])),
                  pl.BlockSpec(memory_space=pltpu.SMEM), resident, stat, stat] + extra.in_specs,
        out_specs=[resident, whole, whole, whole, resident] + extra.out_specs,
        scratch_shapes=[pltpu.VMEM((2, 2 * tq, LANES), BF16), pltpu.VMEM((2, 2 * tq, 2 * LANES), BF16),
                        pltpu.VMEM((2, 2 * tq, LANES), BF16), pltpu.VMEM((2, 2 * tq, 2 * LANES), F32),
                        pltpu.VMEM((n_tok, LANES), F32), pltpu.VMEM((n_tok, LANES), F32)] + extra.scratch)
    wide = lambda dt: jax.ShapeDtypeStruct((n_tok, 4 * LANES), dt)
    return pl.pallas_call(
        kern, grid_spec=grid_spec, name="fox_bwd",
        out_shape=[wide(BF16), wide(BF16), wide(BF16), wide(F32), wide(F32)] + extra.out_shape,
        compiler_params=_params(2, FOX_BWD_VMEM))(ii, jj, kk, zm, zm, zm, c_pairs, dead, d_out, lnorm, delta,
                                                  *extra.arrays)


def _all_gather(shards):
    n_w = len(shards)

    def kern(*refs):
        x_refs, out_refs = refs[:n_w], refs[n_w:2 * n_w]
        send_sems, recv_sems, local_sems = refs[2 * n_w:]
        x, y, c = _my_pos()
        me, sibling = (x, y, c), (x, y, 1 - c)
        chips = [(1 - x, y), (x, 1 - y), (1 - x, 1 - y)]

        def slot(w, px, py, pc):
            return out_refs[w].at[4 * px + 2 * py + pc]

        def copy(w, k, block, to, src=None):
            return pltpu.make_async_remote_copy(
                src_ref=slot(w, *block) if src is None else src, dst_ref=slot(w, *block),
                send_sem=send_sems.at[7 * w + k], recv_sem=recv_sems.at[7 * w + k], device_id=to, device_id_type=MESH)

        local, started = [], []
        for w in range(n_w):
            mine = pltpu.make_async_copy(x_refs[w], slot(w, *me), local_sems.at[w])
            mine.start()
            local.append(mine)
            first = [copy(w, 0, me, sibling, src=x_refs[w])]
            first += [copy(w, 1 + k, me, (*chip, c), src=x_refs[w]) for k, chip in enumerate(chips)]
            for cp in first:
                cp.start()
            started += first
        for k, chip in enumerate(chips):
            for w in range(n_w):
                copy(w, 1 + k, (*chip, c), me).wait_recv()
                passed = copy(w, 4 + k, (*chip, c), sibling)
                passed.start()
                started.append(passed)
        for w in range(n_w):
            copy(w, 0, sibling, me).wait_recv()
            for k, chip in enumerate(chips):
                copy(w, 4 + k, (*chip, 1 - c), me).wait_recv()
        for cp in started:
            cp.wait_send()
        for cp in local:
            cp.wait()

    any_spec = pl.BlockSpec(memory_space=pl.ANY)
    return pl.pallas_call(
        kern, out_shape=[jax.ShapeDtypeStruct((N_DEV,) + s.shape, s.dtype) for s in shards],
        in_specs=[any_spec] * n_w, out_specs=[any_spec] * n_w,
        scratch_shapes=[pltpu.SemaphoreType.DMA((7 * n_w,)), pltpu.SemaphoreType.DMA((7 * n_w,)),
                        pltpu.SemaphoreType.DMA((n_w,))],
        name="weight_all_gather")(*shards)


def _small_exchange(small):
    def kern(s_ref, sall_ref, *sems):
        copies = _gather_copies([s_ref], [sall_ref], *sems)
        _start_copies(*copies)
        _finish_copies(*copies)

    any_spec = pl.BlockSpec(memory_space=pl.ANY)
    return pl.pallas_call(
        kern, out_shape=jax.ShapeDtypeStruct((N_DEV,) + small.shape, small.dtype), in_specs=[any_spec],
        out_specs=any_spec, scratch_shapes=_exchange_scratch(1), name="small_grad_exchange")(small)


ADAMW_BLOCK_BYTES = 2 * 1024 * 1024


def _adamw(parts, w, m, v, name):
    n_parts, n_rows, n_cols = parts.shape
    limit = max(8, ADAMW_BLOCK_BYTES // (n_parts * n_cols * parts.dtype.itemsize))
    tr = max(t for t in range(8, n_rows + 1, 8) if n_rows % t == 0 and t <= limit)

    def kern(p_ref, w_ref, m_ref, v_ref, g_out, d_out, m_out, v_out):
        g = p_ref[0].astype(F32)
        for k in range(1, n_parts):
            g = g + p_ref[k].astype(F32)
        m_new = ADAM_B1 * m_ref[...] + (1.0 - ADAM_B1) * g
        v_new = ADAM_B2 * v_ref[...] + (1.0 - ADAM_B2) * jnp.square(g)
        m_hat = m_new / (1.0 - ADAM_B1 ** ADAM_STEP)
        v_hat = v_new / (1.0 - ADAM_B2 ** ADAM_STEP)
        g_out[...] = g
        d_out[...] = -ADAM_LR * (m_hat / (jnp.sqrt(v_hat) + ADAM_EPS) + ADAM_WD * w_ref[...])
        m_out[...] = m_new
        v_out[...] = v_new

    row = pl.BlockSpec((tr, n_cols), lambda i: (i, 0))
    out = jax.ShapeDtypeStruct((n_rows, n_cols), F32)
    return pl.pallas_call(
        kern, grid=(n_rows // tr,),
        in_specs=[pl.BlockSpec((n_parts, tr, n_cols), lambda i: (0, i, 0)), row, row, row],
        out_specs=[row, row, row, row], out_shape=[out, out, out, out], name=name,
        compiler_params=_params(1))(parts, w, m, v)


SHARDED = {
    "w_in": ((D_MODEL, D_IN), 1), "w_br_swa": ((512, D_MODEL), 1), "w_br_fox": ((512, D_MODEL), 1),
    "w_mix_out": ((D_MODEL, D_MODEL), 0), "w_ff1": ((D_MODEL, D_FF), 1), "w_ff2": ((D_FF, D_MODEL), 0),
    "w_ple_gate": ((D_MODEL, D_MODEL), 0), "w_ple_proj": ((PLE_DIM, D_MODEL), 1),
}
W_IN_SHARD = D_IN // N_DEV
W_IN_PAD = 640
SMALL = ("g_mix", "g_mlp", "g_ple", "g_final", "b_forget", "swa_sinks")
SMALL_COLS = 1024


def _wire_shard(name, a):
    a = a.reshape(a.shape[-2:])
    return jnp.pad(a, ((0, 0), (0, W_IN_PAD - W_IN_SHARD))) if name == "w_in" else a


def _from_wire(name, a):
    return (a[:, :W_IN_SHARD] if name == "w_in" else a)[None]


def _w_all_from_wire(stacked):
    w_in = jnp.concatenate([stacked[d][:, :W_IN_SHARD] for d in range(N_DEV)], axis=1)
    fpad = jnp.zeros((D_MODEL, N_FPAD - FOX_HEADS), stacked.dtype)
    return jnp.concatenate([w_in[:, :N_MAIN + FOX_HEADS], fpad, w_in[:, N_MAIN + FOX_HEADS:]], axis=1)


def _dw_in_to_wire(dw_all):
    dw_in = jnp.concatenate([dw_all[:, :N_MAIN + FOX_HEADS], dw_all[:, N_MAIN + N_FPAD:]], axis=1)
    pad = jnp.zeros((D_MODEL, W_IN_PAD - W_IN_SHARD), dw_all.dtype)
    return jnp.stack([jnp.concatenate([dw_in[:, d * W_IN_SHARD:(d + 1) * W_IN_SHARD], pad], axis=1)
                      for d in range(N_DEV)])


def _pack_small(vals, scalar=None):
    rows = [jnp.pad(vals[n].reshape(-1), (0, SMALL_COLS - vals[n].size)) for n in SMALL]
    if scalar is not None:
        rows.append(jnp.pad(scalar.reshape(1), (0, SMALL_COLS - 1)))
    rows += [jnp.zeros((SMALL_COLS,), F32)] * (8 - len(rows))
    return jnp.stack(rows)


def _unpack_small(slab, like):
    return {n: slab[r, :like[n].size].reshape(like[n].shape) for r, n in enumerate(SMALL)}


def _local_step(x, p, tgt, w, small, tm, tq, ts, late_shards=None):
    n_tok = x.shape[0]
    row = lambda v: v.reshape(1, -1)
    g_mix, g_mlp, g_ple, g_fin = row(small["g_mix"]), row(small["g_mlp"]), row(small["g_ple"]), row(small["g_final"])
    sinks = small["swa_sinks"].reshape(-1)
    b_col = small["b_forget"].reshape(FOX_HEADS, 1)

    assert tm == tq
    u1, zm, zfg, zf, nrm = _in_proj(x, g_mix, w["w_all"], tm)
    f_t = zf[:, :FOX_HEADS].T
    c_pairs = _decay_cumsum(f_t, b_col).reshape(FOX_HEADS // 2, 2, n_tok)
    attn_a, lse_a = _swa_fwd(zm, sinks)
    dead = _fox_dead_steps(nrm, c_pairs, tq)
    if late_shards is None:
        attn_b, ln_b = _fox_fwd(zm, c_pairs, dead, tq)
    else:
        attn_b, ln_b, *late = _fox_fwd(zm, c_pairs, dead, tq, _gather_ride(list(late_shards.values())))
        w = {**w, **_gathered_to_local(dict(zip(late_shards, late)))}
    ya, yb, mixed, h1, u2, a, r, h2 = _mix_ffn_fwd(attn_a, attn_b, zfg, x, w["w_br_swa"], w["w_br_fox"],
                                                   w["w_mix_out"], g_mlp, w["w_ff1"], w["w_ff2"], tm // 2)

    dlg, dpp, u3, dh2, dh2b, da, loss_acc, dgf, dgp = _head_ffn_bwd(
        h2, p, tgt, a, g_ple, w["w_ple_gate"], w["w_ple_proj"], g_fin, w["w_ff2"], tm // 2)
    dh1, dh1b, dgl, dya, dyb, daa, dab, delta_b, dgm = _ffn_bwd_b(
        da, dh2, h1, ya, yb, zfg, attn_b, w["w_ff1"], g_mlp, w["w_mix_out"], w["w_br_swa"], w["w_br_fox"], tm // 2)
    dq_a, dkp, dkc, dvp, dvc, dsk = _swa_bwd(zm, sinks, daa, attn_a, lse_a)
    dw = {
        "w_br_swa": _matmul_tn(attn_a, dya, "dw_br_swa", ts, stack_cols=D_MODEL // N_DEV),
        "w_br_fox": _matmul_tn(attn_b, dyb, "dw_br_fox", ts, stack_cols=D_MODEL // N_DEV),
        "w_mix_out": _matmul_tn(mixed, dh1b, "dw_mix_out", ts),
        "w_ff1": _matmul_tn(u2, da, "dw_ff1", ts, stack_cols=D_FF // N_DEV),
        "w_ff2": _matmul_tn(r, dh2b, "dw_ff2", ts),
        "w_ple_gate": _matmul_tn(u3, dlg, "dw_ple_gate", ts),
        "w_ple_proj": _matmul_tn(p, dpp, "dw_ple_proj", ts, stack_cols=D_MODEL // N_DEV),
    }
    if late_shards is None:
        dq_b, dk_b, dv_b, cs, rs = _fox_bwd(zm, c_pairs, dead, dab, ln_b, delta_b, tq)
        late_parts = None
    else:
        wire = _local_to_wire(dw)
        dq_b, dk_b, dv_b, cs, rs, *parts = _fox_bwd(zm, c_pairs, dead, dab, ln_b, delta_b, tq,
                                                    _scatter_ride([wire[n] for n in late_shards]))
        late_parts = dict(zip(late_shards, parts))

    up = lambda t: jnp.concatenate([t[SWA_BLOCK:], jnp.zeros((SWA_BLOCK, LANES), F32)], axis=0)
    dk_a, dv_a = dkc + up(dkp), dvc + up(dvp)
    df_t, db = _decay_bwd(cs, rs, f_t, b_col)
    df = jnp.pad(df_t.T, ((0, 0), (0, N_FPAD - FOX_HEADS)))
    dz = jnp.concatenate([dq_a, dk_a.astype(BF16), dv_a.astype(BF16), dq_b, dk_b, dv_b,
                          df.astype(BF16), dgl], axis=1)
    dw["w_all"] = _matmul_tn(u1, dz, "dw_in", ts)
    if late_shards is None:
        dx, dgx = _in_proj_bwd(dz, dh1, x, w["w_all"], g_mix, tm)
    else:
        dx, dgx, late_parts["w_in"] = _in_proj_bwd(dz, dh1, x, w["w_all"], g_mix, tm,
                                                   _scatter_ride([_dw_in_to_wire(dw["w_all"])]))
    dsmall = {"g_mix": dgx[0], "g_mlp": dgm[0], "g_ple": dgp[0], "g_final": dgf[0],
              "b_forget": db[:, 0], "swa_sinks": dsk[:, 0]}
    return loss_acc[0, 0], dx, dw, dsmall, late_parts


_ROWS = lambda t: t.reshape(-1, t.shape[-1])
_BY_ROWS = lambda t: t.reshape(N_DEV, t.shape[0] // N_DEV, t.shape[1])
_SAME = lambda t: t
LOCAL_LAYOUT = {
    "w_in": ("w_all", _w_all_from_wire, _dw_in_to_wire), "w_br_swa": ("w_br_swa", _SAME, _SAME),
    "w_br_fox": ("w_br_fox", _SAME, _SAME), "w_mix_out": ("w_mix_out", _ROWS, _BY_ROWS),
    "w_ff1": ("w_ff1", _SAME, _SAME), "w_ff2": ("w_ff2", _SAME, _BY_ROWS),
    "w_ple_gate": ("w_ple_gate", _ROWS, _BY_ROWS), "w_ple_proj": ("w_ple_proj", _SAME, _SAME),
}


def _gathered_to_local(g):
    return {LOCAL_LAYOUT[n][0]: LOCAL_LAYOUT[n][1](t) for n, t in g.items()}


def _local_to_wire(dw):
    names = {local: n for n, (local, _, _) in LOCAL_LAYOUT.items()}
    return {names[local]: LOCAL_LAYOUT[names[local]][2](t) for local, t in dw.items()}


def kernel(x, p, g_mix, w_in, b_forget, swa_sinks, w_br_swa, w_br_fox, w_mix_out, g_mlp, w_ff1, w_ff2, g_ple, w_ple_gate, w_ple_proj, g_final, loss_target, m_g_mix, m_w_in, m_b_forget, m_swa_sinks, m_w_br_swa, m_w_br_fox, m_w_mix_out, m_g_mlp, m_w_ff1, m_w_ff2, m_g_ple, m_w_ple_gate, m_w_ple_proj, m_g_final, v_g_mix, v_w_in, v_b_forget, v_swa_sinks, v_w_br_swa, v_w_br_fox, v_w_mix_out, v_g_mlp, v_w_ff1, v_w_ff2, v_g_ple, v_w_ple_gate, v_w_ple_proj, v_g_final):
    given = dict(g_mix=g_mix, w_in=w_in, b_forget=b_forget, swa_sinks=swa_sinks, w_br_swa=w_br_swa, w_br_fox=w_br_fox,
                 w_mix_out=w_mix_out, g_mlp=g_mlp, w_ff1=w_ff1, w_ff2=w_ff2, g_ple=g_ple, w_ple_gate=w_ple_gate,
                 w_ple_proj=w_ple_proj, g_final=g_final)
    mom = dict(g_mix=m_g_mix, w_in=m_w_in, b_forget=m_b_forget, swa_sinks=m_swa_sinks, w_br_swa=m_w_br_swa,
               w_br_fox=m_w_br_fox, w_mix_out=m_w_mix_out, g_mlp=m_g_mlp, w_ff1=m_w_ff1, w_ff2=m_w_ff2, g_ple=m_g_ple,
               w_ple_gate=m_w_ple_gate, w_ple_proj=m_w_ple_proj, g_final=m_g_final)
    vel = dict(g_mix=v_g_mix, w_in=v_w_in, b_forget=v_b_forget, swa_sinks=v_swa_sinks, w_br_swa=v_w_br_swa,
               w_br_fox=v_w_br_fox, w_mix_out=v_w_mix_out, g_mlp=v_g_mlp, w_ff1=v_w_ff1, w_ff2=v_w_ff2, g_ple=v_g_ple,
               w_ple_gate=v_w_ple_gate, w_ple_proj=v_w_ple_proj, g_final=v_g_final)
    names = list(given)
    sharded = list(SHARDED)

    w_wire = {n: _wire_shard(n, given[n]) for n in sharded}
    late = [n for n in sharded if n != "w_in"]
    gathered = _all_gather([w_wire["w_in"].astype(BF16)])
    local_w = _gathered_to_local({"w_in": gathered[0]})
    small = {n: given[n].reshape(-1) for n in SMALL}

    n_tok = x.shape[1]
    tile = min(TOKEN_TILE, n_tok // 4)
    loss_part, dx, dw, dsmall, parts = _local_step(
        x[0], p[0, 0], loss_target[0], local_w, small, tm=tile, tq=tile, ts=min(DW_TOKENS_PER_STEP, n_tok // 4),
        late_shards={n: w_wire[n].astype(BF16) for n in late})
    small_all = _small_exchange(_pack_small(dsmall, loss_part))

    res = {}
    for n in sharded:
        part = parts[n]
        flat = part.reshape(N_DEV, -1, part.shape[-1])
        outs = _adamw(flat, w_wire[n], _wire_shard(n, mom[n]), _wire_shard(n, vel[n]), "adamw_" + n)
        res[n] = [_from_wire(n, o) for o in outs]
    outs_s = _adamw(small_all, _pack_small(small), _pack_small({n: mom[n] for n in SMALL}),
                    _pack_small({n: vel[n] for n in SMALL}), "adamw_small")
    small_res = [_unpack_small(o, given) for o in outs_s]
    loss = outs_s[0][len(SMALL), 0]

    groups = [[res[n][k] if n in res else small_res[k][n] for n in names] for k in range(4)]
    return (loss, dx[None], *groups[0], *groups[1], *groups[2], *groups[3])
```

```python
import numpy as np
import jax
import jax.numpy as jnp
from jax import lax
from jax.experimental import pallas as pl
from jax.experimental.pallas import tpu as pltpu

F32 = jnp.float32
BF16 = jnp.bfloat16

D_MODEL = 1024
HEAD_DIM = 64
SWA_HEADS = 8
FOX_HEADS = 8
CHUNK_SHIFT = 6
SWA_BLOCK = 128
WINDOW_CHUNKS = 2
D_FF = 4096
PLE_DIM = 256
RMS_EPS = 1e-6
N_MAIN = 2304
N_FPAD = 128
N_GATE = 2048
D_IN = N_MAIN + FOX_HEADS + N_GATE
SCALE = HEAD_DIM ** -0.5
NEG = -1e30

ADAM_LR = 0.001
ADAM_B1 = 0.9
ADAM_B2 = 0.999
ADAM_EPS = 1e-08
ADAM_WD = 0.01
ADAM_STEP = 10

N_DEV = 8
TOKEN_TILE = 512
DW_TOKENS_PER_STEP = 2048
LANES = 128
V7X_VMEM_BYTES = 64 * 1024 * 1024
VMEM_LIMIT = V7X_VMEM_BYTES * 3 // 4
FOX_BWD_VMEM = V7X_VMEM_BYTES * 7 // 8
MESH = pl.DeviceIdType.MESH

_NT = (((1,), (1,)), ((), ()))
_TN = (((0,), (0,)), ((), ()))


def _params(n_grid, vmem_limit=VMEM_LIMIT):
    return pltpu.CompilerParams(dimension_semantics=("arbitrary",) * n_grid, vmem_limit_bytes=vmem_limit)


def _chunks(n, step):
    return [(s, min(step, n - s)) for s in range(0, n, step)]


def _sigmoid(x):
    return 1.0 / (1.0 + jnp.exp(-x))


def _dot(a, b):
    return jnp.dot(a, b, preferred_element_type=F32)


def _dot_nt(a, b):
    return lax.dot_general(a, b, _NT, preferred_element_type=F32)


def _dot_tn(a, b):
    return lax.dot_general(a, b, _TN, preferred_element_type=F32)


def _lane_concat(stacked_ref):
    return jnp.concatenate([stacked_ref[d] for d in range(N_DEV)], axis=1)


def _rms(h):
    return lax.rsqrt(jnp.mean(h * h, axis=-1, keepdims=True) + RMS_EPS)


def _rms_bwd(h, g, du):
    rs = _rms(h)
    n = h * rs
    dn = du * g
    dh = rs * (dn - n * jnp.mean(dn * n, axis=-1, keepdims=True))
    return dh, jnp.sum(du * n, axis=0, keepdims=True)


def _acc_rows(ref, i, row):
    @pl.when(i == 0)
    def _():
        ref[...] = jnp.zeros_like(ref)
    ref[...] += jnp.broadcast_to(row, ref.shape)


def _row_call(body, name, n_rows, tm, row_ins, const_ins, row_outs, acc_outs, ride=None, tile_outs=()):
    row_outs = list(row_outs)
    n_ri, n_ci, n_ro, n_ao = len(row_ins), len(const_ins), len(row_outs) + len(tile_outs), len(acc_outs)
    extra = ride if ride else _NO_RIDE
    n_ride = len(extra.arrays)
    grid = (n_rows // tm,)

    def kern(*refs):
        i = pl.program_id(0)
        ins, refs = refs[:n_ri + n_ci], refs[n_ri + n_ci:]
        ride_in, refs = refs[:n_ride], refs[n_ride:]
        outs, refs = refs[:n_ro + n_ao], refs[n_ro + n_ao:]
        ride_out, sems = refs[:n_ride], refs[n_ride:]
        if ride:
            ride.at_first_step(grid, ride_in, ride_out, sems)
        body(i, ins[:n_ri], ins[n_ri:], outs[:n_ro], outs[n_ro:])
        if ride:
            ride.at_last_step(grid, ride_in, ride_out, sems)

    def whole(a):
        zeros = (0,) * a.ndim
        return pl.BlockSpec(a.shape, lambda i: zeros, pipeline_mode=pl.Buffered(1))

    in_specs = [pl.BlockSpec((tm, a.shape[1]), lambda i: (i, 0)) for a in row_ins]
    in_specs += [whole(a) for a in const_ins] + extra.in_specs
    out_specs = [pl.BlockSpec((tm, c), lambda i: (i, 0)) for c, _ in row_outs]
    out_specs += [pl.BlockSpec((8, c), lambda i: (i, 0)) for c in tile_outs]
    out_specs += [pl.BlockSpec((8, c), lambda i: (0, 0)) for c in acc_outs] + extra.out_specs
    out_shape = [jax.ShapeDtypeStruct((n_rows, c), dt) for c, dt in row_outs]
    out_shape += [jax.ShapeDtypeStruct((8 * grid[0], c), F32) for c in tile_outs]
    out_shape += [jax.ShapeDtypeStruct((8, c), F32) for c in acc_outs] + extra.out_shape
    return pl.pallas_call(kern, grid=grid, in_specs=in_specs, out_specs=out_specs, out_shape=out_shape,
                          scratch_shapes=extra.scratch, name=name,
                          compiler_params=_params(1))(*row_ins, *const_ins, *extra.arrays)


def _in_proj(x, g_mix, w_all, tm):
    def body(i, ins, consts, outs, accs):
        x_ref, = ins
        g_ref, w_ref = consts
        u_ref, zm_ref, zfg_ref, zf_ref, nrm_ref = outs
        xv = x_ref[...]
        u = ((xv * _rms(xv)) * g_ref[...]).astype(BF16)
        u_ref[...] = u
        for s, n in _chunks(N_MAIN, 768):
            zm_ref[:, s:s + n] = _dot(u, w_ref[:, s:s + n]).astype(BF16)
        for s, n in _chunks(N_FPAD + N_GATE, 512):
            zfg_ref[:, s:s + n] = _dot(u, w_ref[:, N_MAIN + s:N_MAIN + s + n])
        zf_ref[...] = zfg_ref[:, :N_FPAD]
        lane = lax.broadcasted_iota(jnp.int32, (4 * LANES, LANES), 0)
        head = lax.broadcasted_iota(jnp.int32, (4 * LANES, LANES), 1)
        pick = (lane // HEAD_DIM == head).astype(BF16)
        tq_, tk_ = (zm_ref[:, col * LANES:(col + 4) * LANES].astype(F32) for col in (Q_COL, K_COL))
        rows = [jnp.max(_dot((t * t).astype(BF16), pick), axis=0, keepdims=True) for t in (tq_, tk_)]
        rows.append(jnp.min(_dot((tq_ * tk_).astype(BF16), pick), axis=0, keepdims=True))
        nrm_ref[...] = jnp.concatenate(rows + [jnp.zeros((5, LANES), F32)], axis=0)

    *outs, nrm = _row_call(body, "in_proj", x.shape[0], tm, [x], [g_mix, w_all],
                           [(D_MODEL, BF16), (N_MAIN, BF16), (N_FPAD + N_GATE, F32), (N_FPAD, F32)], [],
                           tile_outs=[LANES])
    return (*outs, nrm)


def _mix_ffn_fwd(attn_a, attn_b, zfg, x, w_sa, w_fo, w_mo, g_mlp, w1s, w2s, tm):
    ch = D_FF // N_DEV

    def body(i, ins, consts, outs, accs):
        aa_ref, ab_ref, zfg_ref, x_ref = ins
        wsa_ref, wfo_ref, wmo_ref, g_ref, w1_ref, w2_ref = consts
        ya_ref, yb_ref, mx_ref, h1_ref, u2_ref, a_ref, r_ref, h2_ref = outs
        ya = _dot(aa_ref[...], _lane_concat(wsa_ref))
        yb = _dot(ab_ref[...], _lane_concat(wfo_ref))
        g0 = _sigmoid(zfg_ref[:, N_FPAD:N_FPAD + D_MODEL])
        g1 = _sigmoid(zfg_ref[:, N_FPAD + D_MODEL:N_FPAD + 2 * D_MODEL])
        mixed = (g0 * ya + g1 * yb).astype(BF16)
        ya_ref[...] = ya.astype(BF16)
        yb_ref[...] = yb.astype(BF16)
        mx_ref[...] = mixed
        h1 = x_ref[...] + _dot(mixed, wmo_ref[...])
        h1_ref[...] = h1
        u = ((h1 * _rms(h1)) * g_ref[...]).astype(BF16)
        u2_ref[...] = u
        acc = h1
        for c in range(N_DEV):
            a = _dot(u, w1_ref[c])
            a_ref[:, c * ch:(c + 1) * ch] = a.astype(BF16)
            r = jnp.square(jnp.maximum(a, 0.0)).astype(BF16)
            r_ref[:, c * ch:(c + 1) * ch] = r
            acc = acc + _dot(r, w2_ref[c])
        h2_ref[...] = acc

    return _row_call(body, "mix_ffn_fwd", x.shape[0], tm, [attn_a, attn_b, zfg, x],
                     [w_sa, w_fo, w_mo, g_mlp, w1s, w2s],
                     [(D_MODEL, BF16), (D_MODEL, BF16), (D_MODEL, BF16), (D_MODEL, F32), (D_MODEL, BF16),
                      (D_FF, BF16), (D_FF, BF16), (D_MODEL, F32)], [])


def _head_ffn_bwd(h2, p, tgt, a, g_ple, w_pg, w_pp, g_fin, w2s, tm):
    ch = D_FF // N_DEV

    def body(i, ins, consts, outs, accs):
        h2_ref, p_ref, t_ref, a_ref = ins
        gp_ref, wpg_ref, wpp_ref, gf_ref, w2_ref = consts
        dlg_ref, dpp_ref, u3_ref, dh2_ref, dh2b_ref, da_ref = outs
        loss_ref, dgf_ref, dgp_ref = accs
        h2 = h2_ref[...]
        gp = gp_ref[...]
        u3 = ((h2 * _rms(h2)) * gp).astype(BF16)
        u3_ref[...] = u3
        pg = _sigmoid(_dot(u3, wpg_ref[...]))
        pp = _dot(p_ref[...].astype(BF16), _lane_concat(wpp_ref))
        h3 = h2 + pg * pp
        rs3 = _rms(h3)
        n3 = h3 * rs3
        gf = gf_ref[...]
        err = n3 * gf - t_ref[...]
        row_loss = 0.5 * jnp.mean(err * err, axis=-1, keepdims=True)
        _acc_rows(loss_ref, i, jnp.broadcast_to(jnp.sum(row_loss, axis=0, keepdims=True), (1, LANES)))
        dy = err * (1.0 / D_MODEL)
        _acc_rows(dgf_ref, i, jnp.sum(dy * n3, axis=0, keepdims=True))
        dn = dy * gf
        dh3 = rs3 * (dn - n3 * jnp.mean(dn * n3, axis=-1, keepdims=True))
        dpp_ref[...] = (dh3 * pg).astype(BF16)
        dlg = ((dh3 * pp) * pg * (1.0 - pg)).astype(BF16)
        dlg_ref[...] = dlg
        dh, dg = _rms_bwd(h2, gp, _dot_nt(dlg, wpg_ref[...]))
        _acc_rows(dgp_ref, i, dg)
        dh2 = dh3 + dh
        dh2_ref[...] = dh2
        dh2b = dh2.astype(BF16)
        dh2b_ref[...] = dh2b
        for c in range(N_DEV):
            dr = _dot_nt(dh2b, w2_ref[c])
            av = a_ref[:, c * ch:(c + 1) * ch].astype(F32)
            da_ref[:, c * ch:(c + 1) * ch] = (dr * (2.0 * jnp.maximum(av, 0.0))).astype(BF16)

    return _row_call(body, "head_ffn_bwd", h2.shape[0], tm, [h2, p, tgt, a], [g_ple, w_pg, w_pp, g_fin, w2s],
                     [(D_MODEL, BF16), (D_MODEL, BF16), (D_MODEL, BF16), (D_MODEL, F32), (D_MODEL, BF16),
                      (D_FF, BF16)], [LANES, D_MODEL, D_MODEL])


def _ffn_bwd_b(da, dh2, h1, ya, yb, zfg, attn_b, w1s, g_mlp, w_mo, w_sa, w_fo, tm):
    ch = D_FF // N_DEV

    def body(i, ins, consts, outs, accs):
        da_ref, dh2_ref, h1_ref, ya_ref, yb_ref, zfg_ref, ob_ref = ins
        w1_ref, gm_ref, wmo_ref, wsa_ref, wfo_ref = consts
        dh1_ref, dh1b_ref, dgl_ref, dya_ref, dyb_ref, daa_ref, dab_ref, dl_ref = outs
        dgm_ref, = accs
        du2 = _dot_nt(da_ref[:, 0:ch], w1_ref[0])
        for c in range(1, N_DEV):
            du2 = du2 + _dot_nt(da_ref[:, c * ch:(c + 1) * ch], w1_ref[c])
        dh, dg = _rms_bwd(h1_ref[...], gm_ref[...], du2)
        _acc_rows(dgm_ref, i, dg)
        dh1 = dh2_ref[...] + dh
        dh1_ref[...] = dh1
        dh1b = dh1.astype(BF16)
        dh1b_ref[...] = dh1b
        dmx = _dot_nt(dh1b, wmo_ref[...])
        g0 = _sigmoid(zfg_ref[:, N_FPAD:N_FPAD + D_MODEL])
        g1 = _sigmoid(zfg_ref[:, N_FPAD + D_MODEL:N_FPAD + 2 * D_MODEL])
        dya = (dmx * g0).astype(BF16)
        dyb = (dmx * g1).astype(BF16)
        dya_ref[...] = dya
        dyb_ref[...] = dyb
        dgl_ref[:, 0:D_MODEL] = ((dmx * ya_ref[...].astype(F32)) * g0 * (1.0 - g0)).astype(BF16)
        dgl_ref[:, D_MODEL:2 * D_MODEL] = ((dmx * yb_ref[...].astype(F32)) * g1 * (1.0 - g1)).astype(BF16)
        daa_ref[...] = _dot_nt(dya, _lane_concat(wsa_ref)).astype(BF16)
        dab = _dot_nt(dyb, _lane_concat(wfo_ref)).astype(BF16)
        dab_ref[...] = dab
        half_in = lax.broadcasted_iota(jnp.int32, (LANES, 2 * LANES), 0) // HEAD_DIM
        half_out = lax.broadcasted_iota(jnp.int32, (LANES, 2 * LANES), 1) // LANES
        pick = (half_in == half_out).astype(BF16)
        for pair in range(FOX_HEADS // 2):
            cols = slice(pair * LANES, (pair + 1) * LANES)
            prod = dab[:, cols].astype(F32) * ob_ref[:, cols].astype(F32)
            hi = prod.astype(BF16)
            lo_part = (prod - hi.astype(F32)).astype(BF16)
            dl_ref[:, 2 * pair * LANES:(2 * pair + 2) * LANES] = _dot(hi, pick) + _dot(lo_part, pick)

    half = D_MODEL // 2
    return _row_call(body, "ffn_bwd_b", h1.shape[0], tm, [da, dh2, h1, ya, yb, zfg, attn_b],
                     [w1s, g_mlp, w_mo, w_sa, w_fo],
                     [(D_MODEL, F32), (D_MODEL, BF16), (N_GATE, BF16), (D_MODEL, BF16), (D_MODEL, BF16),
                      (half, BF16), (half, BF16), (FOX_HEADS * LANES, F32)], [D_MODEL])


def _in_proj_bwd(dz, dh1, x, w_all, g_mix, tm, ride=None):
    def body(i, ins, consts, outs, accs):
        dz_ref, dh1_ref, x_ref = ins
        w_ref, g_ref = consts
        dx_ref, = outs
        dgx_ref, = accs
        du1 = _dot_nt(dz_ref[...], w_ref[...])
        dh, dg = _rms_bwd(x_ref[...], g_ref[...], du1)
        _acc_rows(dgx_ref, i, dg)
        dx_ref[...] = dh1_ref[...] + dh

    return _row_call(body, "in_proj_bwd", x.shape[0], tm, [dz, dh1, x], [w_all, g_mix],
                     [(D_MODEL, F32)], [D_MODEL], ride)


def _matmul_tn(a, b, name, ts, stack_cols=0):
    n_rows, ka = a.shape
    n = b.shape[1]
    tk = min(ka, 1024)
    tn = 896 if n % 1024 else 1024
    n_stack = tn // stack_cols if stack_cols else 0
    assert ka % tk == 0 and n % tn == 0 and n_rows % ts == 0 and (not stack_cols or tk == ka)
    n_steps = n_rows // ts

    def kern(a_ref, b_ref, o_ref, acc_ref):
        s = pl.program_id(2)

        @pl.when(s == 0)
        def _():
            acc_ref[...] = jnp.zeros_like(acc_ref)
        acc_ref[...] += _dot_tn(a_ref[...].astype(BF16), b_ref[...])

        @pl.when(s == n_steps - 1)
        def _():
            if stack_cols:
                for c in range(n_stack):
                    o_ref[c] = acc_ref[:, c * stack_cols:(c + 1) * stack_cols].astype(BF16)
            else:
                o_ref[...] = acc_ref[...].astype(BF16)

    if stack_cols:
        out_spec = pl.BlockSpec((n_stack, tk, stack_cols), lambda i, j, s: (j, 0, 0))
        out_shape = jax.ShapeDtypeStruct((n // stack_cols, ka, stack_cols), BF16)
    else:
        out_spec = pl.BlockSpec((tk, tn), lambda i, j, s: (i, j))
        out_shape = jax.ShapeDtypeStruct((ka, n), BF16)
    return pl.pallas_call(
        kern, grid=(ka // tk, n // tn, n_steps),
        in_specs=[pl.BlockSpec((ts, tk), lambda i, j, s: (s, i)), pl.BlockSpec((ts, tn), lambda i, j, s: (s, j))],
        out_specs=out_spec, out_shape=out_shape, scratch_shapes=[pltpu.VMEM((tk, tn), F32)], name=name,
        compiler_params=_params(3))(a, b)


SCAN_CHUNK = 512
BWD_SCAN_CHUNK = 1024


def _decay_cumsum(f_t, b_col):
    n_tok = f_t.shape[1]
    ch = min(SCAN_CHUNK, n_tok)

    def kern(f_ref, b_ref, c_ref):
        r = lax.broadcasted_iota(jnp.int32, (ch, ch), 0)
        c = lax.broadcasted_iota(jnp.int32, (ch, ch), 1)
        tri = (r <= c).astype(F32)
        carry = jnp.zeros((8, 1), F32)
        for k in range(n_tok // ch):
            xv = f_ref[:, k * ch:(k + 1) * ch] + b_ref[...]
            lf = jnp.minimum(xv, 0.0) - jnp.log(1.0 + jnp.exp(-jnp.abs(xv)))
            cs = jnp.dot(lf, tri, precision=lax.Precision.HIGHEST, preferred_element_type=F32) + carry
            c_ref[:, k * ch:(k + 1) * ch] = cs
            carry = cs[:, ch - 1:ch]

    return pl.pallas_call(kern, out_shape=jax.ShapeDtypeStruct((8, n_tok), F32), name="decay_cumsum",
                          compiler_params=_params(0))(f_t, b_col)


def _decay_bwd(cs, rs, f_t, b_col):
    n_tok = f_t.shape[1]
    ch = min(BWD_SCAN_CHUNK, n_tok)
    n_ch = n_tok // ch

    def kern(cs_ref, rs_ref, f_ref, b_ref, df_ref, db_ref, carry_ref):
        k = pl.program_id(0)

        @pl.when(k == 0)
        def _():
            carry_ref[...] = jnp.zeros_like(carry_ref)
            db_ref[...] = jnp.zeros_like(db_ref)

        r = lax.broadcasted_iota(jnp.int32, (ch, ch), 0)
        c = lax.broadcasted_iota(jnp.int32, (ch, ch), 1)
        tri = (r >= c).astype(F32)
        head = lax.broadcasted_iota(jnp.int32, (8, 4 * LANES), 0)
        lane = lax.broadcasted_iota(jnp.int32, (8, 4 * LANES), 1)
        pick = (lane == HEAD_DIM * head).astype(F32)
        dc = lax.dot_general(pick, rs_ref[...] - cs_ref[...], _NT, precision=lax.Precision.HIGHEST,
                             preferred_element_type=F32)
        rc = jnp.dot(dc, tri, precision=lax.Precision.HIGHEST, preferred_element_type=F32) + carry_ref[:, 0:1]
        carry_ref[...] = jnp.broadcast_to(rc[:, 0:1], carry_ref.shape)
        df = rc / (1.0 + jnp.exp(f_ref[...] + b_ref[...]))
        df_ref[...] = df
        db_ref[...] += jnp.broadcast_to(jnp.sum(df, axis=1, keepdims=True), db_ref.shape)

    back = lambda k: n_ch - 1 - k
    wide = pl.BlockSpec((ch, 4 * LANES), lambda k: (back(k), 0))
    row = pl.BlockSpec((8, ch), lambda k: (0, back(k)))
    return pl.pallas_call(
        kern, grid=(n_ch,),
        in_specs=[wide, wide, row, pl.BlockSpec((8, 1), lambda k: (0, 0))],
        out_specs=[row, pl.BlockSpec((8, LANES), lambda k: (0, 0))],
        out_shape=[jax.ShapeDtypeStruct((8, n_tok), F32), jax.ShapeDtypeStruct((8, LANES), F32)],
        scratch_shapes=[pltpu.VMEM((8, LANES), F32)], name="decay_bwd", compiler_params=_params(1))(cs, rs, f_t, b_col)


def _swa_bias_table():
    row = jnp.arange(SWA_BLOCK)[:, None] + SWA_BLOCK
    col = jnp.arange(2 * SWA_BLOCK)[None, :]
    cd = (row >> CHUNK_SHIFT) - (col >> CHUNK_SHIFT)
    band = (cd >= 0) & (cd <= WINDOW_CHUNKS)
    slopes = jnp.asarray([2.0 ** -(h + 1) for h in range(SWA_HEADS)], F32)
    bias = -slopes[:, None, None] * jnp.abs(row - col).astype(F32)[None]
    return jnp.stack([jnp.where(band & (col >= SWA_BLOCK), bias, NEG), jnp.where(band, bias, NEG)])


SWA_PER_STEP = 2


def _swap_halves(t):
    return pltpu.roll(t.astype(F32), HEAD_DIM, axis=1).astype(t.dtype)


def _swa_specs():
    blk, rows = SWA_BLOCK, SWA_PER_STEP * SWA_BLOCK
    q = pl.BlockSpec((rows, 4 * LANES), lambda n: (n, 0))
    before = lambda col: pl.BlockSpec((blk, LANES), lambda n: (jnp.maximum(SWA_PER_STEP * n - 1, 0), col))
    own = lambda col: pl.BlockSpec((rows, LANES), lambda n: (n, col))
    bias = pl.BlockSpec((2, SWA_HEADS, blk, 2 * blk), lambda n: (0, 0, 0, 0))
    return [q, before(4), own(4), before(5), own(5), bias]


def _swa_band(before_ref, own_ref):
    both = jnp.concatenate([before_ref[...], own_ref[...]], axis=0)
    return both, _swap_halves(both)


def _swa_bias(bias_ref, n, b):
    return bias_ref.at[jnp.minimum(n, 1)] if b == 0 else bias_ref.at[1]


SWA_GROUPS = ([h for h in range(SWA_HEADS) if h % 2 == h // 4], [h for h in range(SWA_HEADS) if h % 2 != h // 4])


def _stack_heads(ref, rows, heads, lo, mask_halves):
    tiles = []
    for h in heads:
        t = ref[rows, (h // 2) * LANES:(h // 2 + 1) * LANES]
        tiles.append(jnp.where(lo if h % 2 == 0 else ~lo, t, jnp.zeros_like(t)) if mask_halves else t)
    return jnp.concatenate(tiles, axis=0)


def _per_head_column(values, heads):
    return jnp.concatenate([jnp.full((SWA_BLOCK, 1), values(h), F32) for h in heads], axis=0)


def _swa_scores(q_ref, rows, kx, heads, lo, bias):
    qa = _stack_heads(q_ref, rows, heads, lo, True) * SCALE
    return qa, _dot_nt(qa, kx) + jnp.concatenate([bias[h] for h in heads], axis=0)


def _swa_fwd(zm, sinks):
    n_tok = zm.shape[0]
    blk, step_rows = SWA_BLOCK, SWA_PER_STEP * SWA_BLOCK

    def kern(q_ref, kp_ref, kc_ref, vp_ref, vc_ref, bias_ref, sink_ref, o_ref, lse_ref):
        n = pl.program_id(0)
        (k_all, k_all_sw), (v_all, v_all_sw) = _swa_band(kp_ref, kc_ref), _swa_band(vp_ref, vc_ref)
        lane = lax.broadcasted_iota(jnp.int32, (blk, LANES), 1)
        lo = lane < HEAD_DIM
        for b in range(SWA_PER_STEP):
            rows, band = slice(b * blk, (b + 1) * blk), slice(b * blk, (b + 2) * blk)
            bias = _swa_bias(bias_ref, n, b)
            lse_t = jnp.zeros((blk, LANES), F32)
            for pair in range(SWA_HEADS // 2):
                q2 = q_ref[rows, pair * LANES:(pair + 1) * LANES]
                outs = []
                for a in range(2):
                    h = 2 * pair + a
                    qa = jnp.where(lo if a == 0 else ~lo, q2, jnp.zeros_like(q2)) * SCALE
                    kx, vx = (k_all[band], v_all[band]) if h in SWA_GROUPS[0] else (k_all_sw[band], v_all_sw[band])
                    s = _dot_nt(qa, kx) + bias[h]
                    sink = sink_ref[h]
                    m = jnp.maximum(jnp.max(s, axis=-1, keepdims=True), sink)
                    e = jnp.exp(s - m)
                    l = jnp.sum(e, axis=-1, keepdims=True) + jnp.exp(sink - m)
                    pn = (e * (1.0 / l)).astype(BF16)
                    outs.append(_dot(pn, vx))
                    lse_t = jnp.where(lane == h, m + jnp.log(l), lse_t)
                o_ref[rows, pair * LANES:(pair + 1) * LANES] = jnp.where(lo, outs[0], outs[1]).astype(BF16)
            lse_ref[rows, :] = lse_t

    return pl.pallas_call(
        kern, grid=(n_tok // step_rows,),
        in_specs=_swa_specs() + [pl.BlockSpec(memory_space=pltpu.SMEM)],
        out_specs=[pl.BlockSpec((step_rows, 4 * LANES), lambda n: (n, 0)),
                   pl.BlockSpec((step_rows, LANES), lambda n: (n, 0))],
        out_shape=[jax.ShapeDtypeStruct((n_tok, 4 * LANES), BF16), jax.ShapeDtypeStruct((n_tok, LANES), F32)],
        name="swa_fwd", compiler_params=_params(1))(zm, zm, zm, zm, zm, _swa_bias_table(), sinks)


def _swa_bwd(zm, sinks, d_out, out, lse):
    n_tok = zm.shape[0]
    blk, step_rows = SWA_BLOCK, SWA_PER_STEP * SWA_BLOCK

    def kern(q_ref, kp_ref, kc_ref, vp_ref, vc_ref, bias_ref, do_ref, o_ref, lse_ref, sink_ref,
             dq_ref, dkp_ref, dkc_ref, dvp_ref, dvc_ref, dsk_ref):
        n = pl.program_id(0)

        @pl.when(n == 0)
        def _():
            dsk_ref[...] = jnp.zeros_like(dsk_ref)

        bands = (_swa_band(kp_ref, kc_ref), _swa_band(vp_ref, vc_ref))
        lane = lax.broadcasted_iota(jnp.int32, (blk, LANES), 1)
        lo = lane < HEAD_DIM
        for b in range(SWA_PER_STEP):
            rows, band = slice(b * blk, (b + 1) * blk), slice(b * blk, (b + 2) * blk)
            bias = _swa_bias(bias_ref, n, b)
            lse_t = lse_ref[rows, :]
            dqs, dkv = {}, []
            for g, heads in enumerate(SWA_GROUPS):
                kx, vx = bands[0][g][band], bands[1][g][band]
                qa, s = _swa_scores(q_ref, rows, kx, heads, lo, bias)
                doa = _stack_heads(do_ref, rows, heads, lo, True)
                lse_g = jnp.concatenate([lse_t[:, h:h + 1] for h in heads], axis=0)
                prob = jnp.exp(s - lse_g)
                o_g = _stack_heads(o_ref, rows, heads, lo, False)
                dd = jnp.sum(doa.astype(F32) * o_g.astype(F32), axis=-1, keepdims=True)
                ds = (prob * (_dot_nt(doa, vx) - dd)).astype(BF16)
                sink_part = -jnp.exp(_per_head_column(lambda h: sink_ref[h], heads) - lse_g) * dd
                dq = _dot(ds, kx) * SCALE
                for r, h in enumerate(heads):
                    dqs[h] = dq[r * blk:(r + 1) * blk]
                    dsk_ref[h:h + 1, :] += jnp.broadcast_to(
                        jnp.sum(sink_part[r * blk:(r + 1) * blk], axis=0, keepdims=True), (1, LANES))
                dkv.append((_dot_tn(ds, qa), _dot_tn(prob.astype(BF16), doa)))
            for pair in range(SWA_HEADS // 2):
                dq_ref[rows, pair * LANES:(pair + 1) * LANES] = jnp.where(
                    lo, dqs[2 * pair], dqs[2 * pair + 1]).astype(BF16)
            dk = dkv[0][0] + pltpu.roll(dkv[1][0], HEAD_DIM, axis=1)
            dv = dkv[0][1] + pltpu.roll(dkv[1][1], HEAD_DIM, axis=1)
            dkp_ref[rows, :] = dk[0:blk]
            dkc_ref[rows, :] = dk[blk:2 * blk]
            dvp_ref[rows, :] = dv[0:blk]
            dvc_ref[rows, :] = dv[blk:2 * blk]

    wide = pl.BlockSpec((step_rows, 4 * LANES), lambda n: (n, 0))
    narrow = pl.BlockSpec((step_rows, LANES), lambda n: (n, 0))
    part = jax.ShapeDtypeStruct((n_tok, LANES), F32)
    return pl.pallas_call(
        kern, grid=(n_tok // step_rows,),
        in_specs=_swa_specs() + [wide, wide, narrow, pl.BlockSpec(memory_space=pltpu.SMEM)],
        out_specs=[wide, narrow, narrow, narrow, narrow, pl.BlockSpec((8, LANES), lambda n: (0, 0))],
        out_shape=[jax.ShapeDtypeStruct((n_tok, 4 * LANES), BF16), part, part, part, part,
                   jax.ShapeDtypeStruct((8, LANES), F32)],
        name="swa_bwd", compiler_params=_params(1))(zm, zm, zm, zm, zm, _swa_bias_table(), d_out, out, lse, sinks)


def _my_pos():
    return lax.axis_index("x"), lax.axis_index("y"), lax.axis_index("c")


def _peer(k):
    x, y, c = _my_pos()
    px, py, pc = x ^ (k >> 2), y ^ ((k >> 1) & 1), c ^ (k & 1)
    return (px, py, pc), 4 * px + 2 * py + pc


def _gather_copies(x_refs, out_refs, send_sems, recv_sems, local_sems):
    x, y, c = _my_pos()
    my_id = 4 * x + 2 * y + c
    local = [pltpu.make_async_copy(x_refs[w], out_refs[w].at[my_id], local_sems.at[w]) for w in range(len(x_refs))]
    sends, arrivals = [], []
    for k in range(1, N_DEV):
        peer, peer_id = _peer(k)
        for w in range(len(x_refs)):
            sems = dict(send_sem=send_sems.at[7 * w + k - 1], recv_sem=recv_sems.at[7 * w + k - 1],
                        device_id=peer, device_id_type=MESH)
            sends.append(pltpu.make_async_remote_copy(src_ref=x_refs[w], dst_ref=out_refs[w].at[my_id], **sems))
            arrivals.append(pltpu.make_async_remote_copy(src_ref=x_refs[w], dst_ref=out_refs[w].at[peer_id], **sems))
    return local, sends, arrivals


def _scatter_copies(g_refs, part_refs, send_sems, recv_sems, local_sems):
    x, y, c = _my_pos()
    my_id = 4 * x + 2 * y + c
    local = [pltpu.make_async_copy(g_refs[w].at[my_id], part_refs[w].at[0], local_sems.at[w])
             for w in range(len(g_refs))]
    sends, arrivals = [], []
    for k in range(1, N_DEV):
        peer, peer_id = _peer(k)
        for w in range(len(g_refs)):
            sems = dict(send_sem=send_sems.at[7 * w + k - 1], recv_sem=recv_sems.at[7 * w + k - 1],
                        device_id=peer, device_id_type=MESH)
            sends.append(pltpu.make_async_remote_copy(src_ref=g_refs[w].at[peer_id], dst_ref=part_refs[w].at[k], **sems))
            arrivals.append(pltpu.make_async_remote_copy(src_ref=g_refs[w].at[my_id], dst_ref=part_refs[w].at[k], **sems))
    return local, sends, arrivals


def _start_copies(local, sends, arrivals):
    for cp in local + sends:
        cp.start()


def _finish_copies(local, sends, arrivals):
    for cp in arrivals:
        cp.wait_recv()
    for cp in sends:
        cp.wait_send()
    for cp in local:
        cp.wait()


def _exchange_scratch(n_arrays):
    return [pltpu.SemaphoreType.DMA((7 * n_arrays,)), pltpu.SemaphoreType.DMA((7 * n_arrays,)),
            pltpu.SemaphoreType.DMA((n_arrays,))]


class _Ride:
    def __init__(self, arrays, out_shape, copies):
        self.arrays, self.out_shape, self.copies = list(arrays), list(out_shape), copies
        any_spec = pl.BlockSpec(memory_space=pl.ANY)
        self.in_specs = [any_spec] * len(self.arrays)
        self.out_specs = [any_spec] * len(self.arrays)
        self.scratch = _exchange_scratch(len(self.arrays)) if self.arrays else []

    @staticmethod
    def _at(grid, last):
        hit = [pl.program_id(d) == (n - 1 if last else 0) for d, n in enumerate(grid)]
        return hit[0] if len(hit) == 1 else jnp.logical_and(*hit)

    def at_first_step(self, grid, in_refs, out_refs, sems):
        @pl.when(self._at(grid, False))
        def _():
            _start_copies(*self.copies(in_refs, out_refs, *sems))

    def at_last_step(self, grid, in_refs, out_refs, sems):
        @pl.when(self._at(grid, True))
        def _():
            _finish_copies(*self.copies(in_refs, out_refs, *sems))


_NO_RIDE = _Ride([], [], None)


def _gather_ride(shards):
    return _Ride(shards, [jax.ShapeDtypeStruct((N_DEV,) + s.shape, s.dtype) for s in shards], _gather_copies)


def _scatter_ride(grads):
    return _Ride(grads, [jax.ShapeDtypeStruct(g.shape, g.dtype) for g in grads], _scatter_copies)


Q_COL, K_COL, V_COL = 6, 10, 14


def _causal(t, tq, tk):
    row = lax.broadcasted_iota(jnp.int32, (tq, tk), 0)
    col = lax.broadcasted_iota(jnp.int32, (tq, tk), 1)
    return jnp.where(col <= row, t, NEG)


def _lane_tile(stat, width):
    return jnp.tile(stat, (1, width // LANES))


def _fox_steps(nq):
    steps = [(i2, j, 0 if j < 2 * i2 else 1 + j - 2 * i2) for i2 in range(nq // 2) for j in range(2 * i2 + 2)]
    return [np.asarray(col, np.int32) for col in zip(*steps)]


_SWEEPS = {0: [(0, False), (1, False)], 1: [(0, True), (1, False)], 2: [(1, True)]}


def _fox_dispatch(sweep, kind, dead_ref, head0, idx):
    dead0, dead1 = dead_ref[head0, idx] > 0.5, dead_ref[head0 + 1, idx] > 0.5
    live0, live1 = jnp.logical_not(dead0), jnp.logical_not(dead1)
    below = kind == 0
    pl.when(jnp.logical_and(below, jnp.logical_and(live0, live1)))(lambda: sweep(_SWEEPS[0], (0, 1)))
    pl.when(jnp.logical_and(below, jnp.logical_and(live0, dead1)))(lambda: sweep(_SWEEPS[0], (0,)))
    pl.when(jnp.logical_and(below, jnp.logical_and(dead0, live1)))(lambda: sweep(_SWEEPS[0], (1,)))
    pl.when(kind == 1)(lambda: sweep(_SWEEPS[1], (0, 1)))
    pl.when(kind == 2)(lambda: sweep(_SWEEPS[2], (0, 1)))


EXP_ZERO = 110.0
NORM_SLACK = 1.005


def _fox_dead_steps(nrm, c_pairs, tq):
    nq = nrm.shape[0] // 8
    stats = nrm.reshape(nq, 8, LANES)[:, :3, :FOX_HEADS]
    qn, kn, own = jnp.sqrt(stats[:, 0]) * SCALE, jnp.sqrt(stats[:, 1]), stats[:, 2] * SCALE
    cb = c_pairs.reshape(FOX_HEADS, nq, tq)
    c_max, c_min = jnp.max(cb, axis=-1).T, jnp.min(cb, axis=-1).T
    both = lambda t, pick: pick(t.reshape(nq // 2, 2, FOX_HEADS), axis=1)
    qn2, kn2, c_max2, own2 = both(qn, jnp.max), both(kn, jnp.max), both(c_max, jnp.max), both(own, jnp.min)
    row_max_floor = own2 - (NORM_SLACK - 1.0) * qn2 * kn2 - c_max2
    gap = qn2[:, None] * kn[None] * NORM_SLACK - c_min[None] - row_max_floor[:, None]
    below = jnp.arange(nq)[None, :] < 2 * jnp.arange(nq // 2)[:, None]
    dead = jnp.logical_and(gap < -EXP_ZERO, below[..., None])
    return dead.transpose(2, 0, 1).reshape(FOX_HEADS, -1).astype(F32)


def _fox_fwd(zm, c_pairs, dead, tq, ride=None):
    n_tok = zm.shape[0]
    nq = n_tok // tq
    ii, jj, kk = _fox_steps(nq)
    n_steps = len(ii)
    n_ride = len(ride.arrays) if ride else 0

    def kern(ii_ref, jj_ref, kk_ref, q_ref, k_ref, v_ref, ck_ref, dead_ref, *more):
        ride_in, (o_ref, ln_ref), ride_out = more[:n_ride], more[n_ride:n_ride + 2], more[n_ride + 2:2 * n_ride + 2]
        qs_ref, m_ref, l_ref, acc_ref = more[2 * n_ride + 2:2 * n_ride + 6]
        step = pl.program_id(1)
        j, kind = jj_ref[step], kk_ref[step]
        lo = lax.broadcasted_iota(jnp.int32, (2 * tq, LANES), 1) < HEAD_DIM
        if ride:
            ride.at_first_step((FOX_HEADS // 2, n_steps), ride_in, ride_out, more[2 * n_ride + 6:])

        @pl.when(j == 0)
        def _():
            q2 = q_ref[...]
            zq = jnp.zeros_like(q2)
            qs_ref[0] = jnp.where(lo, q2, zq) * SCALE
            qs_ref[1] = jnp.where(lo, zq, q2) * SCALE
            m_ref[...] = jnp.full(m_ref.shape, NEG, F32)
            l_ref[...] = jnp.zeros(l_ref.shape, F32)
            acc_ref[...] = jnp.zeros(acc_ref.shape, F32)

        def sweep(subs, heads):
            kv = k_ref[...]
            v_ones = jnp.concatenate([v_ref[...], jnp.ones((tq, LANES), BF16)], axis=1)
            for sub, diag in subs:
                rows = slice(sub * tq, (sub + 1) * tq)
                for a in heads:
                    t = _dot_nt(qs_ref[a, rows], kv) - ck_ref[a:a + 1, :]
                    if diag:
                        t = _causal(t, tq, tq)
                    m_old = m_ref[a, rows]
                    m_new = jnp.maximum(m_old, jnp.max(t, axis=-1, keepdims=True))
                    alpha = jnp.exp(m_old - m_new)
                    e = jnp.exp(t - _lane_tile(m_new, tq)).astype(BF16)
                    pv = _dot(e, v_ones)
                    acc_ref[a, rows] = alpha * acc_ref[a, rows] + pv[:, :LANES]
                    l_ref[a, rows] = alpha * l_ref[a, rows] + pv[:, LANES:]
                    m_ref[a, rows] = m_new

        _fox_dispatch(sweep, kind, dead_ref, 2 * pl.program_id(0), ii_ref[step] * nq + j)

        @pl.when(kind == 2)
        def _():
            o_ref[...] = jnp.where(lo, acc_ref[0] / l_ref[0], acc_ref[1] / l_ref[1]).astype(BF16)
            ln_ref[:, :LANES] = m_ref[0] + jnp.log(l_ref[0])
            ln_ref[:, LANES:] = m_ref[1] + jnp.log(l_ref[1])

        if ride:
            ride.at_last_step((FOX_HEADS // 2, n_steps), ride_in, ride_out, more[2 * n_ride + 6:])

    blk = (tq, LANES)
    by_i = lambda col: (lambda hp, s, ii, jj, kk: (ii[s], col + hp))
    by_j = lambda col: (lambda hp, s, ii, jj, kk: (jj[s], col + hp))
    extra = ride if ride else _NO_RIDE
    grid_spec = pltpu.PrefetchScalarGridSpec(
        num_scalar_prefetch=3, grid=(FOX_HEADS // 2, n_steps),
        in_specs=[pl.BlockSpec((2 * tq, LANES), by_i(Q_COL)), pl.BlockSpec(blk, by_j(K_COL)),
                  pl.BlockSpec(blk, by_j(V_COL)),
                  pl.BlockSpec((None, 2, tq), lambda hp, s, ii, jj, kk: (hp, 0, jj[s])),
                  pl.BlockSpec(memory_space=pltpu.SMEM)] + extra.in_specs,
        out_specs=[pl.BlockSpec((2 * tq, LANES), by_i(0)), pl.BlockSpec((2 * tq, 2 * LANES), by_i(0))] + extra.out_specs,
        scratch_shapes=[pltpu.VMEM((2, 2 * tq, LANES), BF16), pltpu.VMEM((2, 2 * tq, LANES), F32),
                        pltpu.VMEM((2, 2 * tq, LANES), F32), pltpu.VMEM((2, 2 * tq, LANES), F32)] + extra.scratch)
    return pl.pallas_call(
        kern, grid_spec=grid_spec,
        out_shape=[jax.ShapeDtypeStruct((n_tok, 4 * LANES), BF16),
                   jax.ShapeDtypeStruct((n_tok, FOX_HEADS * LANES), F32)] + extra.out_shape,
        name="fox_fwd", compiler_params=_params(2))(ii, jj, kk, zm, zm, zm, c_pairs, dead, *extra.arrays)


def _fox_bwd(zm, c_pairs, dead, d_out, lnorm, delta, tq, ride=None):
    n_tok = zm.shape[0]
    nq = n_tok // tq
    ii, jj, kk = _fox_steps(nq)
    n_steps = len(ii)
    n_ride = len(ride.arrays) if ride else 0

    def kern(ii_ref, jj_ref, kk_ref, q_ref, k_ref, v_ref, ck_ref, dead_ref, do_ref, ln_ref, dl_ref, *more):
        ride_in, ride_out = more[:n_ride], more[n_ride + 5:2 * n_ride + 5]
        dq_ref, dk_out, dv_out, cs_ref, rs_ref = more[n_ride:n_ride + 5]
        qs_ref, qo_ref, dos_ref, dq_acc, dk_ref, dv_ref = more[2 * n_ride + 5:2 * n_ride + 11]
        step = pl.program_id(1)
        j, kind = jj_ref[step], kk_ref[step]
        lo = lax.broadcasted_iota(jnp.int32, (2 * tq, LANES), 1) < HEAD_DIM
        if ride:
            ride.at_first_step((FOX_HEADS // 2, n_steps), ride_in, ride_out, more[2 * n_ride + 11:])

        @pl.when(step == 0)
        def _():
            dk_ref[...] = jnp.zeros_like(dk_ref)
            dv_ref[...] = jnp.zeros_like(dv_ref)
            cs_ref[...] = jnp.zeros_like(cs_ref)

        @pl.when(j == 0)
        def _():
            q2, do2 = q_ref[...], do_ref[...]
            zq = jnp.zeros_like(q2)
            ones = jnp.ones((2 * tq, LANES), BF16)
            for a in range(2):
                half = lo if a == 0 else ~lo
                qa = jnp.where(half, q2, zq) * SCALE
                qs_ref[a] = qa
                qo_ref[a] = jnp.concatenate([qa, ones], axis=1)
                dos_ref[a] = jnp.where(half, do2, zq)
            dq_acc[...] = jnp.zeros(dq_acc.shape, F32)

        def sweep(subs, heads):
            kv, vv = k_ref[...], v_ref[...]
            k_ones = jnp.concatenate([kv, jnp.ones((tq, LANES), BF16)], axis=1)
            dk, dv, sums = None, None, {}
            for sub, diag in subs:
                rows = slice(sub * tq, (sub + 1) * tq)
                for a in heads:
                    t = _dot_nt(qs_ref[a, rows], kv) - ck_ref[a:a + 1, :]
                    if diag:
                        t = _causal(t, tq, tq)
                    prob = jnp.exp(t - _lane_tile(ln_ref[rows, a * LANES:(a + 1) * LANES], tq))
                    dp = _dot_nt(dos_ref[a, rows], vv)
                    ds = (prob * (dp - _lane_tile(dl_ref[rows, a * LANES:(a + 1) * LANES], tq))).astype(BF16)
                    dq_acc[a, rows] += _dot(ds, k_ones)
                    dk_cs = _dot_tn(ds, qo_ref[a, rows])
                    dv_a = _dot_tn(prob.astype(BF16), dos_ref[a, rows])
                    dk = dk_cs[:, :LANES] if dk is None else dk + dk_cs[:, :LANES]
                    dv = dv_a if dv is None else dv + dv_a
                    sums[a] = dk_cs[:, LANES:] if a not in sums else sums[a] + dk_cs[:, LANES:]
            keys = pl.ds(pl.multiple_of(j * tq, tq), tq)
            dk_ref[keys, :] += dk
            cs_ref[keys, :] += jnp.where(lo[:tq], sums.get(0, 0.0), sums.get(1, 0.0))
            dv_ref[keys, :] += dv

        _fox_dispatch(sweep, kind, dead_ref, 2 * pl.program_id(0), ii_ref[step] * nq + j)

        @pl.when(kind == 2)
        def _():
            dq_ref[...] = (jnp.where(lo, dq_acc[0, :, :LANES], dq_acc[1, :, :LANES]) * SCALE).astype(BF16)
            rs_ref[...] = jnp.where(lo, dq_acc[0, :, LANES:], dq_acc[1, :, LANES:])

        @pl.when(step == n_steps - 1)
        def _():
            dk_out[...] = dk_ref[...].astype(BF16)
            dv_out[...] = dv_ref[...].astype(BF16)

        if ride:
            ride.at_last_step((FOX_HEADS // 2, n_steps), ride_in, ride_out, more[2 * n_ride + 11:])

    blk = (tq, LANES)
    by_i = lambda col: (lambda hp, s, ii, jj, kk: (ii[s], col + hp))
    by_j = lambda col: (lambda hp, s, ii, jj, kk: (jj[s], col + hp))
    resident = pl.BlockSpec((2 * tq, LANES), by_i(0))
    stat = pl.BlockSpec((2 * tq, 2 * LANES), by_i(0))
    whole = pl.BlockSpec((n_tok, LANES), lambda hp, s, ii, jj, kk: (0, hp))
    extra = ride if ride else _NO_RIDE
    grid_spec = pltpu.PrefetchScalarGridSpec(
        num_scalar_prefetch=3, grid=(FOX_HEADS // 2, n_steps),
        in_specs=[pl.BlockSpec((2 * tq, LANES), by_i(Q_COL)), pl.BlockSpec(blk, by_j(K_COL)),
                  pl.BlockSpec(blk, by_j(V_COL)),
                  pl.BlockSpec((None, 2, tq), lambda hp, s, ii, jj, kk: (hp, 0, jj[s])),
                  pl.BlockSpec(memory_space=pltpu.SMEM), resident, stat, stat] + extra.in_specs,
        out_specs=[resident, whole, whole, whole, resident] + extra.out_specs,
        scratch_shapes=[pltpu.VMEM((2, 2 * tq, LANES), BF16), pltpu.VMEM((2, 2 * tq, 2 * LANES), BF16),
                        pltpu.VMEM((2, 2 * tq, LANES), BF16), pltpu.VMEM((2, 2 * tq, 2 * LANES), F32),
                        pltpu.VMEM((n_tok, LANES), F32), pltpu.VMEM((n_tok, LANES), F32)] + extra.scratch)
    wide = lambda dt: jax.ShapeDtypeStruct((n_tok, 4 * LANES), dt)
    return pl.pallas_call(
        kern, grid_spec=grid_spec, name="fox_bwd",
        out_shape=[wide(BF16), wide(BF16), wide(BF16), wide(F32), wide(F32)] + extra.out_shape,
        compiler_params=_params(2, FOX_BWD_VMEM))(ii, jj, kk, zm, zm, zm, c_pairs, dead, d_out, lnorm, delta,
                                                  *extra.arrays)


def _all_gather(shards):
    n_w = len(shards)

    def kern(*refs):
        x_refs, out_refs = refs[:n_w], refs[n_w:2 * n_w]
        send_sems, recv_sems, local_sems = refs[2 * n_w:]
        x, y, c = _my_pos()
        me, sibling = (x, y, c), (x, y, 1 - c)
        chips = [(1 - x, y), (x, 1 - y), (1 - x, 1 - y)]

        def slot(w, px, py, pc):
            return out_refs[w].at[4 * px + 2 * py + pc]

        def copy(w, k, block, to, src=None):
            return pltpu.make_async_remote_copy(
                src_ref=slot(w, *block) if src is None else src, dst_ref=slot(w, *block),
                send_sem=send_sems.at[7 * w + k], recv_sem=recv_sems.at[7 * w + k], device_id=to, device_id_type=MESH)

        local, started = [], []
        for w in range(n_w):
            mine = pltpu.make_async_copy(x_refs[w], slot(w, *me), local_sems.at[w])
            mine.start()
            local.append(mine)
            first = [copy(w, 0, me, sibling, src=x_refs[w])]
            first += [copy(w, 1 + k, me, (*chip, c), src=x_refs[w]) for k, chip in enumerate(chips)]
            for cp in first:
                cp.start()
            started += first
        for k, chip in enumerate(chips):
            for w in range(n_w):
                copy(w, 1 + k, (*chip, c), me).wait_recv()
                passed = copy(w, 4 + k, (*chip, c), sibling)
                passed.start()
                started.append(passed)
        for w in range(n_w):
            copy(w, 0, sibling, me).wait_recv()
            for k, chip in enumerate(chips):
                copy(w, 4 + k, (*chip, 1 - c), me).wait_recv()
        for cp in started:
            cp.wait_send()
        for cp in local:
            cp.wait()

    any_spec = pl.BlockSpec(memory_space=pl.ANY)
    return pl.pallas_call(
        kern, out_shape=[jax.ShapeDtypeStruct((N_DEV,) + s.shape, s.dtype) for s in shards],
        in_specs=[any_spec] * n_w, out_specs=[any_spec] * n_w,
        scratch_shapes=[pltpu.SemaphoreType.DMA((7 * n_w,)), pltpu.SemaphoreType.DMA((7 * n_w,)),
                        pltpu.SemaphoreType.DMA((n_w,))],
        name="weight_all_gather")(*shards)


def _small_exchange(small):
    def kern(s_ref, sall_ref, *sems):
        copies = _gather_copies([s_ref], [sall_ref], *sems)
        _start_copies(*copies)
        _finish_copies(*copies)

    any_spec = pl.BlockSpec(memory_space=pl.ANY)
    return pl.pallas_call(
        kern, out_shape=jax.ShapeDtypeStruct((N_DEV,) + small.shape, small.dtype), in_specs=[any_spec],
        out_specs=any_spec, scratch_shapes=_exchange_scratch(1), name="small_grad_exchange")(small)


ADAMW_BLOCK_BYTES = 2 * 1024 * 1024


def _adamw(parts, w, m, v, name):
    n_parts, n_rows, n_cols = parts.shape
    limit = max(8, ADAMW_BLOCK_BYTES // (n_parts * n_cols * parts.dtype.itemsize))
    tr = max(t for t in range(8, n_rows + 1, 8) if n_rows % t == 0 and t <= limit)

    def kern(p_ref, w_ref, m_ref, v_ref, g_out, d_out, m_out, v_out):
        g = p_ref[0].astype(F32)
        for k in range(1, n_parts):
            g = g + p_ref[k].astype(F32)
        m_new = ADAM_B1 * m_ref[...] + (1.0 - ADAM_B1) * g
        v_new = ADAM_B2 * v_ref[...] + (1.0 - ADAM_B2) * jnp.square(g)
        m_hat = m_new / (1.0 - ADAM_B1 ** ADAM_STEP)
        v_hat = v_new / (1.0 - ADAM_B2 ** ADAM_STEP)
        g_out[...] = g
        d_out[...] = -ADAM_LR * (m_hat / (jnp.sqrt(v_hat) + ADAM_EPS) + ADAM_WD * w_ref[...])
        m_out[...] = m_new
        v_out[...] = v_new

    row = pl.BlockSpec((tr, n_cols), lambda i: (i, 0))
    out = jax.ShapeDtypeStruct((n_rows, n_cols), F32)
    return pl.pallas_call(
        kern, grid=(n_rows // tr,),
        in_specs=[pl.BlockSpec((n_parts, tr, n_cols), lambda i: (0, i, 0)), row, row, row],
        out_specs=[row, row, row, row], out_shape=[out, out, out, out], name=name,
        compiler_params=_params(1))(parts, w, m, v)


SHARDED = {
    "w_in": ((D_MODEL, D_IN), 1), "w_br_swa": ((512, D_MODEL), 1), "w_br_fox": ((512, D_MODEL), 1),
    "w_mix_out": ((D_MODEL, D_MODEL), 0), "w_ff1": ((D_MODEL, D_FF), 1), "w_ff2": ((D_FF, D_MODEL), 0),
    "w_ple_gate": ((D_MODEL, D_MODEL), 0), "w_ple_proj": ((PLE_DIM, D_MODEL), 1),
}
W_IN_SHARD = D_IN // N_DEV
W_IN_PAD = 640
SMALL = ("g_mix", "g_mlp", "g_ple", "g_final", "b_forget", "swa_sinks")
SMALL_COLS = 1024


def _wire_shard(name, a):
    a = a.reshape(a.shape[-2:])
    return jnp.pad(a, ((0, 0), (0, W_IN_PAD - W_IN_SHARD))) if name == "w_in" else a


def _from_wire(name, a):
    return (a[:, :W_IN_SHARD] if name == "w_in" else a)[None]


def _w_all_from_wire(stacked):
    w_in = jnp.concatenate([stacked[d][:, :W_IN_SHARD] for d in range(N_DEV)], axis=1)
    fpad = jnp.zeros((D_MODEL, N_FPAD - FOX_HEADS), stacked.dtype)
    return jnp.concatenate([w_in[:, :N_MAIN + FOX_HEADS], fpad, w_in[:, N_MAIN + FOX_HEADS:]], axis=1)


def _dw_in_to_wire(dw_all):
    dw_in = jnp.concatenate([dw_all[:, :N_MAIN + FOX_HEADS], dw_all[:, N_MAIN + N_FPAD:]], axis=1)
    pad = jnp.zeros((D_MODEL, W_IN_PAD - W_IN_SHARD), dw_all.dtype)
    return jnp.stack([jnp.concatenate([dw_in[:, d * W_IN_SHARD:(d + 1) * W_IN_SHARD], pad], axis=1)
                      for d in range(N_DEV)])


def _pack_small(vals, scalar=None):
    rows = [jnp.pad(vals[n].reshape(-1), (0, SMALL_COLS - vals[n].size)) for n in SMALL]
    if scalar is not None:
        rows.append(jnp.pad(scalar.reshape(1), (0, SMALL_COLS - 1)))
    rows += [jnp.zeros((SMALL_COLS,), F32)] * (8 - len(rows))
    return jnp.stack(rows)


def _unpack_small(slab, like):
    return {n: slab[r, :like[n].size].reshape(like[n].shape) for r, n in enumerate(SMALL)}


def _local_step(x, p, tgt, w, small, tm, tq, ts, late_shards=None):
    n_tok = x.shape[0]
    row = lambda v: v.reshape(1, -1)
    g_mix, g_mlp, g_ple, g_fin = row(small["g_mix"]), row(small["g_mlp"]), row(small["g_ple"]), row(small["g_final"])
    sinks = small["swa_sinks"].reshape(-1)
    b_col = small["b_forget"].reshape(FOX_HEADS, 1)

    assert tm == tq
    u1, zm, zfg, zf, nrm = _in_proj(x, g_mix, w["w_all"], tm)
    f_t = zf[:, :FOX_HEADS].T
    c_pairs = _decay_cumsum(f_t, b_col).reshape(FOX_HEADS // 2, 2, n_tok)
    attn_a, lse_a = _swa_fwd(zm, sinks)
    dead = _fox_dead_steps(nrm, c_pairs, tq)
    if late_shards is None:
        attn_b, ln_b = _fox_fwd(zm, c_pairs, dead, tq)
    else:
        attn_b, ln_b, *late = _fox_fwd(zm, c_pairs, dead, tq, _gather_ride(list(late_shards.values())))
        w = {**w, **_gathered_to_local(dict(zip(late_shards, late)))}
    ya, yb, mixed, h1, u2, a, r, h2 = _mix_ffn_fwd(attn_a, attn_b, zfg, x, w["w_br_swa"], w["w_br_fox"],
                                                   w["w_mix_out"], g_mlp, w["w_ff1"], w["w_ff2"], tm // 2)

    dlg, dpp, u3, dh2, dh2b, da, loss_acc, dgf, dgp = _head_ffn_bwd(
        h2, p, tgt, a, g_ple, w["w_ple_gate"], w["w_ple_proj"], g_fin, w["w_ff2"], tm // 2)
    dh1, dh1b, dgl, dya, dyb, daa, dab, delta_b, dgm = _ffn_bwd_b(
        da, dh2, h1, ya, yb, zfg, attn_b, w["w_ff1"], g_mlp, w["w_mix_out"], w["w_br_swa"], w["w_br_fox"], tm // 2)
    dq_a, dkp, dkc, dvp, dvc, dsk = _swa_bwd(zm, sinks, daa, attn_a, lse_a)
    dw = {
        "w_br_swa": _matmul_tn(attn_a, dya, "dw_br_swa", ts, stack_cols=D_MODEL // N_DEV),
        "w_br_fox": _matmul_tn(attn_b, dyb, "dw_br_fox", ts, stack_cols=D_MODEL // N_DEV),
        "w_mix_out": _matmul_tn(mixed, dh1b, "dw_mix_out", ts),
        "w_ff1": _matmul_tn(u2, da, "dw_ff1", ts, stack_cols=D_FF // N_DEV),
        "w_ff2": _matmul_tn(r, dh2b, "dw_ff2", ts),
        "w_ple_gate": _matmul_tn(u3, dlg, "dw_ple_gate", ts),
        "w_ple_proj": _matmul_tn(p, dpp, "dw_ple_proj", ts, stack_cols=D_MODEL // N_DEV),
    }
    if late_shards is None:
        dq_b, dk_b, dv_b, cs, rs = _fox_bwd(zm, c_pairs, dead, dab, ln_b, delta_b, tq)
        late_parts = None
    else:
        wire = _local_to_wire(dw)
        dq_b, dk_b, dv_b, cs, rs, *parts = _fox_bwd(zm, c_pairs, dead, dab, ln_b, delta_b, tq,
                                                    _scatter_ride([wire[n] for n in late_shards]))
        late_parts = dict(zip(late_shards, parts))

    up = lambda t: jnp.concatenate([t[SWA_BLOCK:], jnp.zeros((SWA_BLOCK, LANES), F32)], axis=0)
    dk_a, dv_a = dkc + up(dkp), dvc + up(dvp)
    df_t, db = _decay_bwd(cs, rs, f_t, b_col)
    df = jnp.pad(df_t.T, ((0, 0), (0, N_FPAD - FOX_HEADS)))
    dz = jnp.concatenate([dq_a, dk_a.astype(BF16), dv_a.astype(BF16), dq_b, dk_b, dv_b,
                          df.astype(BF16), dgl], axis=1)
    dw["w_all"] = _matmul_tn(u1, dz, "dw_in", ts)
    if late_shards is None:
        dx, dgx = _in_proj_bwd(dz, dh1, x, w["w_all"], g_mix, tm)
    else:
        dx, dgx, late_parts["w_in"] = _in_proj_bwd(dz, dh1, x, w["w_all"], g_mix, tm,
                                                   _scatter_ride([_dw_in_to_wire(dw["w_all"])]))
    dsmall = {"g_mix": dgx[0], "g_mlp": dgm[0], "g_ple": dgp[0], "g_final": dgf[0],
              "b_forget": db[:, 0], "swa_sinks": dsk[:, 0]}
    return loss_acc[0, 0], dx, dw, dsmall, late_parts


_ROWS = lambda t: t.reshape(-1, t.shape[-1])
_BY_ROWS = lambda t: t.reshape(N_DEV, t.shape[0] // N_DEV, t.shape[1])
_SAME = lambda t: t
LOCAL_LAYOUT = {
    "w_in": ("w_all", _w_all_from_wire, _dw_in_to_wire), "w_br_swa": ("w_br_swa", _SAME, _SAME),
    "w_br_fox": ("w_br_fox", _SAME, _SAME), "w_mix_out": ("w_mix_out", _ROWS, _BY_ROWS),
    "w_ff1": ("w_ff1", _SAME, _SAME), "w_ff2": ("w_ff2", _SAME, _BY_ROWS),
    "w_ple_gate": ("w_ple_gate", _ROWS, _BY_ROWS), "w_ple_proj": ("w_ple_proj", _SAME, _SAME),
}


def _gathered_to_local(g):
    return {LOCAL_LAYOUT[n][0]: LOCAL_LAYOUT[n][1](t) for n, t in g.items()}


def _local_to_wire(dw):
    names = {local: n for n, (local, _, _) in LOCAL_LAYOUT.items()}
    return {names[local]: LOCAL_LAYOUT[names[local]][2](t) for local, t in dw.items()}


def kernel(x, p, g_mix, w_in, b_forget, swa_sinks, w_br_swa, w_br_fox, w_mix_out, g_mlp, w_ff1, w_ff2, g_ple, w_ple_gate, w_ple_proj, g_final, loss_target, m_g_mix, m_w_in, m_b_forget, m_swa_sinks, m_w_br_swa, m_w_br_fox, m_w_mix_out, m_g_mlp, m_w_ff1, m_w_ff2, m_g_ple, m_w_ple_gate, m_w_ple_proj, m_g_final, v_g_mix, v_w_in, v_b_forget, v_swa_sinks, v_w_br_swa, v_w_br_fox, v_w_mix_out, v_g_mlp, v_w_ff1, v_w_ff2, v_g_ple, v_w_ple_gate, v_w_ple_proj, v_g_final):
    given = dict(g_mix=g_mix, w_in=w_in, b_forget=b_forget, swa_sinks=swa_sinks, w_br_swa=w_br_swa, w_br_fox=w_br_fox,
                 w_mix_out=w_mix_out, g_mlp=g_mlp, w_ff1=w_ff1, w_ff2=w_ff2, g_ple=g_ple, w_ple_gate=w_ple_gate,
                 w_ple_proj=w_ple_proj, g_final=g_final)
    mom = dict(g_mix=m_g_mix, w_in=m_w_in, b_forget=m_b_forget, swa_sinks=m_swa_sinks, w_br_swa=m_w_br_swa,
               w_br_fox=m_w_br_fox, w_mix_out=m_w_mix_out, g_mlp=m_g_mlp, w_ff1=m_w_ff1, w_ff2=m_w_ff2, g_ple=m_g_ple,
               w_ple_gate=m_w_ple_gate, w_ple_proj=m_w_ple_proj, g_final=m_g_final)
    vel = dict(g_mix=v_g_mix, w_in=v_w_in, b_forget=v_b_forget, swa_sinks=v_swa_sinks, w_br_swa=v_w_br_swa,
               w_br_fox=v_w_br_fox, w_mix_out=v_w_mix_out, g_mlp=v_g_mlp, w_ff1=v_w_ff1, w_ff2=v_w_ff2, g_ple=v_g_ple,
               w_ple_gate=v_w_ple_gate, w_ple_proj=v_w_ple_proj, g_final=v_g_final)
    names = list(given)
    sharded = list(SHARDED)

    w_wire = {n: _wire_shard(n, given[n]) for n in sharded}
    late = [n for n in sharded if n != "w_in"]
    gathered = _all_gather([w_wire["w_in"].astype(BF16)])
    local_w = _gathered_to_local({"w_in": gathered[0]})
    small = {n: given[n].reshape(-1) for n in SMALL}

    n_tok = x.shape[1]
    tile = min(TOKEN_TILE, n_tok // 4)
    loss_part, dx, dw, dsmall, parts = _local_step(
        x[0], p[0, 0], loss_target[0], local_w, small, tm=tile, tq=tile, ts=min(DW_TOKENS_PER_STEP, n_tok // 4),
        late_shards={n: w_wire[n].astype(BF16) for n in late})
    small_all = _small_exchange(_pack_small(dsmall, loss_part))

    res = {}
    for n in sharded:
        part = parts[n]
        flat = part.reshape(N_DEV, -1, part.shape[-1])
        outs = _adamw(flat, w_wire[n], _wire_shard(n, mom[n]), _wire_shard(n, vel[n]), "adamw_" + n)
        res[n] = [_from_wire(n, o) for o in outs]
    outs_s = _adamw(small_all, _pack_small(small), _pack_small({n: mom[n] for n in SMALL}),
                    _pack_small({n: vel[n] for n in SMALL}), "adamw_small")
    small_res = [_unpack_small(o, given) for o in outs_s]
    loss = outs_s[0][len(SMALL), 0]

    groups = [[res[n][k] if n in res else small_res[k][n] for n in names] for k in range(4)]
    return (loss, dx[None], *groups[0], *groups[1], *groups[2], *groups[3])
```

```python
import numpy as np
import jax
import jax.numpy as jnp
from jax import lax
from jax.experimental import pallas as pl
from jax.experimental.pallas import tpu as pltpu

F32 = jnp.float32
BF16 = jnp.bfloat16

D_MODEL = 1024
HEAD_DIM = 64
SWA_HEADS = 8
FOX_HEADS = 8
CHUNK_SHIFT = 6
SWA_BLOCK = 128
WINDOW_CHUNKS = 2
D_FF = 4096
PLE_DIM = 256
RMS_EPS = 1e-6
N_MAIN = 2304
N_FPAD = 128
N_GATE = 2048
D_IN = N_MAIN + FOX_HEADS + N_GATE
SCALE = HEAD_DIM ** -0.5
NEG = -1e30

ADAM_LR = 0.001
ADAM_B1 = 0.9
ADAM_B2 = 0.999
ADAM_EPS = 1e-08
ADAM_WD = 0.01
ADAM_STEP = 10

N_DEV = 8
TOKEN_TILE = 512
DW_TOKENS_PER_STEP = 2048
LANES = 128
V7X_VMEM_BYTES = 64 * 1024 * 1024
VMEM_LIMIT = V7X_VMEM_BYTES * 3 // 4
FOX_BWD_VMEM = V7X_VMEM_BYTES * 7 // 8
MESH = pl.DeviceIdType.MESH

_NT = (((1,), (1,)), ((), ()))
_TN = (((0,), (0,)), ((), ()))


def _params(n_grid, vmem_limit=VMEM_LIMIT):
    return pltpu.CompilerParams(dimension_semantics=("arbitrary",) * n_grid, vmem_limit_bytes=vmem_limit)


def _chunks(n, step):
    return [(s, min(step, n - s)) for s in range(0, n, step)]


def _sigmoid(x):
    return 1.0 / (1.0 + jnp.exp(-x))


def _dot(a, b):
    return jnp.dot(a, b, preferred_element_type=F32)


def _dot_nt(a, b):
    return lax.dot_general(a, b, _NT, preferred_element_type=F32)


def _dot_tn(a, b):
    return lax.dot_general(a, b, _TN, preferred_element_type=F32)


def _lane_concat(stacked_ref):
    return jnp.concatenate([stacked_ref[d] for d in range(N_DEV)], axis=1)


def _rms(h):
    return lax.rsqrt(jnp.mean(h * h, axis=-1, keepdims=True) + RMS_EPS)


def _rms_bwd(h, g, du):
    rs = _rms(h)
    n = h * rs
    dn = du * g
    dh = rs * (dn - n * jnp.mean(dn * n, axis=-1, keepdims=True))
    return dh, jnp.sum(du * n, axis=0, keepdims=True)


def _acc_rows(ref, i, row):
    @pl.when(i == 0)
    def _():
        ref[...] = jnp.zeros_like(ref)
    ref[...] += jnp.broadcast_to(row, ref.shape)


def _row_call(body, name, n_rows, tm, row_ins, const_ins, row_outs, acc_outs, ride=None, tile_outs=()):
    row_outs = list(row_outs)
    n_ri, n_ci, n_ro, n_ao = len(row_ins), len(const_ins), len(row_outs) + len(tile_outs), len(acc_outs)
    extra = ride if ride else _NO_RIDE
    n_ride = len(extra.arrays)
    grid = (n_rows // tm,)

    def kern(*refs):
        i = pl.program_id(0)
        ins, refs = refs[:n_ri + n_ci], refs[n_ri + n_ci:]
        ride_in, refs = refs[:n_ride], refs[n_ride:]
        outs, refs = refs[:n_ro + n_ao], refs[n_ro + n_ao:]
        ride_out, sems = refs[:n_ride], refs[n_ride:]
        if ride:
            ride.at_first_step(grid, ride_in, ride_out, sems)
        body(i, ins[:n_ri], ins[n_ri:], outs[:n_ro], outs[n_ro:])
        if ride:
            ride.at_last_step(grid, ride_in, ride_out, sems)

    def whole(a):
        zeros = (0,) * a.ndim
        return pl.BlockSpec(a.shape, lambda i: zeros, pipeline_mode=pl.Buffered(1))

    in_specs = [pl.BlockSpec((tm, a.shape[1]), lambda i: (i, 0)) for a in row_ins]
    in_specs += [whole(a) for a in const_ins] + extra.in_specs
    out_specs = [pl.BlockSpec((tm, c), lambda i: (i, 0)) for c, _ in row_outs]
    out_specs += [pl.BlockSpec((8, c), lambda i: (i, 0)) for c in tile_outs]
    out_specs += [pl.BlockSpec((8, c), lambda i: (0, 0)) for c in acc_outs] + extra.out_specs
    out_shape = [jax.ShapeDtypeStruct((n_rows, c), dt) for c, dt in row_outs]
    out_shape += [jax.ShapeDtypeStruct((8 * grid[0], c), F32) for c in tile_outs]
    out_shape += [jax.ShapeDtypeStruct((8, c), F32) for c in acc_outs] + extra.out_shape
    return pl.pallas_call(kern, grid=grid, in_specs=in_specs, out_specs=out_specs, out_shape=out_shape,
                          scratch_shapes=extra.scratch, name=name,
                          compiler_params=_params(1))(*row_ins, *const_ins, *extra.arrays)


def _in_proj(x, g_mix, w_all, tm):
    def body(i, ins, consts, outs, accs):
        x_ref, = ins
        g_ref, w_ref = consts
        u_ref, zm_ref, zfg_ref, zf_ref, nrm_ref = outs
        xv = x_ref[...]
        u = ((xv * _rms(xv)) * g_ref[...]).astype(BF16)
        u_ref[...] = u
        for s, n in _chunks(N_MAIN, 768):
            zm_ref[:, s:s + n] = _dot(u, w_ref[:, s:s + n]).astype(BF16)
        for s, n in _chunks(N_FPAD + N_GATE, 512):
            zfg_ref[:, s:s + n] = _dot(u, w_ref[:, N_MAIN + s:N_MAIN + s + n])
        zf_ref[...] = zfg_ref[:, :N_FPAD]
        lane = lax.broadcasted_iota(jnp.int32, (4 * LANES, LANES), 0)
        head = lax.broadcasted_iota(jnp.int32, (4 * LANES, LANES), 1)
        pick = (lane // HEAD_DIM == head).astype(BF16)
        tq_, tk_ = (zm_ref[:, col * LANES:(col + 4) * LANES].astype(F32) for col in (Q_COL, K_COL))
        rows = [jnp.max(_dot((t * t).astype(BF16), pick), axis=0, keepdims=True) for t in (tq_, tk_)]
        rows.append(jnp.min(_dot((tq_ * tk_).astype(BF16), pick), axis=0, keepdims=True))
        nrm_ref[...] = jnp.concatenate(rows + [jnp.zeros((5, LANES), F32)], axis=0)

    *outs, nrm = _row_call(body, "in_proj", x.shape[0], tm, [x], [g_mix, w_all],
                           [(D_MODEL, BF16), (N_MAIN, BF16), (N_FPAD + N_GATE, F32), (N_FPAD, F32)], [],
                           tile_outs=[LANES])
    return (*outs, nrm)


def _mix_ffn_fwd(attn_a, attn_b, zfg, x, w_sa, w_fo, w_mo, g_mlp, w1s, w2s, tm):
    ch = D_FF // N_DEV

    def body(i, ins, consts, outs, accs):
        aa_ref, ab_ref, zfg_ref, x_ref = ins
        wsa_ref, wfo_ref, wmo_ref, g_ref, w1_ref, w2_ref = consts
        ya_ref, yb_ref, mx_ref, h1_ref, u2_ref, a_ref, r_ref, h2_ref = outs
        ya = _dot(aa_ref[...], _lane_concat(wsa_ref))
        yb = _dot(ab_ref[...], _lane_concat(wfo_ref))
        g0 = _sigmoid(zfg_ref[:, N_FPAD:N_FPAD + D_MODEL])
        g1 = _sigmoid(zfg_ref[:, N_FPAD + D_MODEL:N_FPAD + 2 * D_MODEL])
        mixed = (g0 * ya + g1 * yb).astype(BF16)
        ya_ref[...] = ya.astype(BF16)
        yb_ref[...] = yb.astype(BF16)
        mx_ref[...] = mixed
        h1 = x_ref[...] + _dot(mixed, wmo_ref[...])
        h1_ref[...] = h1
        u = ((h1 * _rms(h1)) * g_ref[...]).astype(BF16)
        u2_ref[...] = u
        acc = h1
        for c in range(N_DEV):
            a = _dot(u, w1_ref[c])
            a_ref[:, c * ch:(c + 1) * ch] = a.astype(BF16)
            r = jnp.square(jnp.maximum(a, 0.0)).astype(BF16)
            r_ref[:, c * ch:(c + 1) * ch] = r
            acc = acc + _dot(r, w2_ref[c])
        h2_ref[...] = acc

    return _row_call(body, "mix_ffn_fwd", x.shape[0], tm, [attn_a, attn_b, zfg, x],
                     [w_sa, w_fo, w_mo, g_mlp, w1s, w2s],
                     [(D_MODEL, BF16), (D_MODEL, BF16), (D_MODEL, BF16), (D_MODEL, F32), (D_MODEL, BF16),
                      (D_FF, BF16), (D_FF, BF16), (D_MODEL, F32)], [])


def _head_ffn_bwd(h2, p, tgt, a, g_ple, w_pg, w_pp, g_fin, w2s, tm):
    ch = D_FF // N_DEV

    def body(i, ins, consts, outs, accs):
        h2_ref, p_ref, t_ref, a_ref = ins
        gp_ref, wpg_ref, wpp_ref, gf_ref, w2_ref = consts
        dlg_ref, dpp_ref, u3_ref, dh2_ref, dh2b_ref, da_ref = outs
        loss_ref, dgf_ref, dgp_ref = accs
        h2 = h2_ref[...]
        gp = gp_ref[...]
        u3 = ((h2 * _rms(h2)) * gp).astype(BF16)
        u3_ref[...] = u3
        pg = _sigmoid(_dot(u3, wpg_ref[...]))
        pp = _dot(p_ref[...].astype(BF16), _lane_concat(wpp_ref))
        h3 = h2 + pg * pp
        rs3 = _rms(h3)
        n3 = h3 * rs3
        gf = gf_ref[...]
        err = n3 * gf - t_ref[...]
        row_loss = 0.5 * jnp.mean(err * err, axis=-1, keepdims=True)
        _acc_rows(loss_ref, i, jnp.broadcast_to(jnp.sum(row_loss, axis=0, keepdims=True), (1, LANES)))
        dy = err * (1.0 / D_MODEL)
        _acc_rows(dgf_ref, i, jnp.sum(dy * n3, axis=0, keepdims=True))
        dn = dy * gf
        dh3 = rs3 * (dn - n3 * jnp.mean(dn * n3, axis=-1, keepdims=True))
        dpp_ref[...] = (dh3 * pg).astype(BF16)
        dlg = ((dh3 * pp) * pg * (1.0 - pg)).astype(BF16)
        dlg_ref[...] = dlg
        dh, dg = _rms_bwd(h2, gp, _dot_nt(dlg, wpg_ref[...]))
        _acc_rows(dgp_ref, i, dg)
        dh2 = dh3 + dh
        dh2_ref[...] = dh2
        dh2b = dh2.astype(BF16)
        dh2b_ref[...] = dh2b
        for c in range(N_DEV):
            dr = _dot_nt(dh2b, w2_ref[c])
            av = a_ref[:, c * ch:(c + 1) * ch].astype(F32)
            da_ref[:, c * ch:(c + 1) * ch] = (dr * (2.0 * jnp.maximum(av, 0.0))).astype(BF16)

    return _row_call(body, "head_ffn_bwd", h2.shape[0], tm, [h2, p, tgt, a], [g_ple, w_pg, w_pp, g_fin, w2s],
                     [(D_MODEL, BF16), (D_MODEL, BF16), (D_MODEL, BF16), (D_MODEL, F32), (D_MODEL, BF16),
                      (D_FF, BF16)], [LANES, D_MODEL, D_MODEL])


def _ffn_bwd_b(da, dh2, h1, ya, yb, zfg, attn_b, w1s, g_mlp, w_mo, w_sa, w_fo, tm):
    ch = D_FF // N_DEV

    def body(i, ins, consts, outs, accs):
        da_ref, dh2_ref, h1_ref, ya_ref, yb_ref, zfg_ref, ob_ref = ins
        w1_ref, gm_ref, wmo_ref, wsa_ref, wfo_ref = consts
        dh1_ref, dh1b_ref, dgl_ref, dya_ref, dyb_ref, daa_ref, dab_ref, dl_ref = outs
        dgm_ref, = accs
        du2 = _dot_nt(da_ref[:, 0:ch], w1_ref[0])
        for c in range(1, N_DEV):
            du2 = du2 + _dot_nt(da_ref[:, c * ch:(c + 1) * ch], w1_ref[c])
        dh, dg = _rms_bwd(h1_ref[...], gm_ref[...], du2)
        _acc_rows(dgm_ref, i, dg)
        dh1 = dh2_ref[...] + dh
        dh1_ref[...] = dh1
        dh1b = dh1.astype(BF16)
        dh1b_ref[...] = dh1b
        dmx = _dot_nt(dh1b, wmo_ref[...])
        g0 = _sigmoid(zfg_ref[:, N_FPAD:N_FPAD + D_MODEL])
        g1 = _sigmoid(zfg_ref[:, N_FPAD + D_MODEL:N_FPAD + 2 * D_MODEL])
        dya = (dmx * g0).astype(BF16)
        dyb = (dmx * g1).astype(BF16)
        dya_ref[...] = dya
        dyb_ref[...] = dyb
        dgl_ref[:, 0:D_MODEL] = ((dmx * ya_ref[...].astype(F32)) * g0 * (1.0 - g0)).astype(BF16)
        dgl_ref[:, D_MODEL:2 * D_MODEL] = ((dmx * yb_ref[...].astype(F32)) * g1 * (1.0 - g1)).astype(BF16)
        daa_ref[...] = _dot_nt(dya, _lane_concat(wsa_ref)).astype(BF16)
        dab = _dot_nt(dyb, _lane_concat(wfo_ref)).astype(BF16)
        dab_ref[...] = dab
        half_in = lax.broadcasted_iota(jnp.int32, (LANES, 2 * LANES), 0) // HEAD_DIM
        half_out = lax.broadcasted_iota(jnp.int32, (LANES, 2 * LANES), 1) // LANES
        pick = (half_in == half_out).astype(BF16)
        for pair in range(FOX_HEADS // 2):
            cols = slice(pair * LANES, (pair + 1) * LANES)
            prod = dab[:, cols].astype(F32) * ob_ref[:, cols].astype(F32)
            hi = prod.astype(BF16)
            lo_part = (prod - hi.astype(F32)).astype(BF16)
            dl_ref[:, 2 * pair * LANES:(2 * pair + 2) * LANES] = _dot(hi, pick) + _dot(lo_part, pick)

    half = D_MODEL // 2
    return _row_call(body, "ffn_bwd_b", h1.shape[0], tm, [da, dh2, h1, ya, yb, zfg, attn_b],
                     [w1s, g_mlp, w_mo, w_sa, w_fo],
                     [(D_MODEL, F32), (D_MODEL, BF16), (N_GATE, BF16), (D_MODEL, BF16), (D_MODEL, BF16),
                      (half, BF16), (half, BF16), (FOX_HEADS * LANES, F32)], [D_MODEL])


def _in_proj_bwd(dz, dh1, x, w_all, g_mix, tm, ride=None):
    def body(i, ins, consts, outs, accs):
        dz_ref, dh1_ref, x_ref = ins
        w_ref, g_ref = consts
        dx_ref, = outs
        dgx_ref, = accs
        du1 = _dot_nt(dz_ref[...], w_ref[...])
        dh, dg = _rms_bwd(x_ref[...], g_ref[...], du1)
        _acc_rows(dgx_ref, i, dg)
        dx_ref[...] = dh1_ref[...] + dh

    return _row_call(body, "in_proj_bwd", x.shape[0], tm, [dz, dh1, x], [w_all, g_mix],
                     [(D_MODEL, F32)], [D_MODEL], ride)


def _matmul_tn(a, b, name, ts, stack_cols=0):
    n_rows, ka = a.shape
    n = b.shape[1]
    tk = min(ka, 1024)
    tn = 896 if n % 1024 else 1024
    n_stack = tn // stack_cols if stack_cols else 0
    assert ka % tk == 0 and n % tn == 0 and n_rows % ts == 0 and (not stack_cols or tk == ka)
    n_steps = n_rows // ts

    def kern(a_ref, b_ref, o_ref, acc_ref):
        s = pl.program_id(2)

        @pl.when(s == 0)
        def _():
            acc_ref[...] = jnp.zeros_like(acc_ref)
        acc_ref[...] += _dot_tn(a_ref[...].astype(BF16), b_ref[...])

        @pl.when(s == n_steps - 1)
        def _():
            if stack_cols:
                for c in range(n_stack):
                    o_ref[c] = acc_ref[:, c * stack_cols:(c + 1) * stack_cols].astype(BF16)
            else:
                o_ref[...] = acc_ref[...].astype(BF16)

    if stack_cols:
        out_spec = pl.BlockSpec((n_stack, tk, stack_cols), lambda i, j, s: (j, 0, 0))
        out_shape = jax.ShapeDtypeStruct((n // stack_cols, ka, stack_cols), BF16)
    else:
        out_spec = pl.BlockSpec((tk, tn), lambda i, j, s: (i, j))
        out_shape = jax.ShapeDtypeStruct((ka, n), BF16)
    return pl.pallas_call(
        kern, grid=(ka // tk, n // tn, n_steps),
        in_specs=[pl.BlockSpec((ts, tk), lambda i, j, s: (s, i)), pl.BlockSpec((ts, tn), lambda i, j, s: (s, j))],
        out_specs=out_spec, out_shape=out_shape, scratch_shapes=[pltpu.VMEM((tk, tn), F32)], name=name,
        compiler_params=_params(3))(a, b)


SCAN_CHUNK = 512
BWD_SCAN_CHUNK = 1024


def _decay_cumsum(f_t, b_col):
    n_tok = f_t.shape[1]
    ch = min(SCAN_CHUNK, n_tok)

    def kern(f_ref, b_ref, c_ref):
        r = lax.broadcasted_iota(jnp.int32, (ch, ch), 0)
        c = lax.broadcasted_iota(jnp.int32, (ch, ch), 1)
        tri = (r <= c).astype(F32)
        carry = jnp.zeros((8, 1), F32)
        for k in range(n_tok // ch):
            xv = f_ref[:, k * ch:(k + 1) * ch] + b_ref[...]
            lf = jnp.minimum(xv, 0.0) - jnp.log(1.0 + jnp.exp(-jnp.abs(xv)))
            cs = jnp.dot(lf, tri, precision=lax.Precision.HIGHEST, preferred_element_type=F32) + carry
            c_ref[:, k * ch:(k + 1) * ch] = cs
            carry = cs[:, ch - 1:ch]

    return pl.pallas_call(kern, out_shape=jax.ShapeDtypeStruct((8, n_tok), F32), name="decay_cumsum",
                          compiler_params=_params(0))(f_t, b_col)


def _decay_bwd(cs, rs, f_t, b_col):
    n_tok = f_t.shape[1]
    ch = min(BWD_SCAN_CHUNK, n_tok)
    n_ch = n_tok // ch

    def kern(cs_ref, rs_ref, f_ref, b_ref, df_ref, db_ref, carry_ref):
        k = pl.program_id(0)

        @pl.when(k == 0)
        def _():
            carry_ref[...] = jnp.zeros_like(carry_ref)
            db_ref[...] = jnp.zeros_like(db_ref)

        r = lax.broadcasted_iota(jnp.int32, (ch, ch), 0)
        c = lax.broadcasted_iota(jnp.int32, (ch, ch), 1)
        tri = (r >= c).astype(F32)
        head = lax.broadcasted_iota(jnp.int32, (8, 4 * LANES), 0)
        lane = lax.broadcasted_iota(jnp.int32, (8, 4 * LANES), 1)
        pick = (lane == HEAD_DIM * head).astype(F32)
        dc = lax.dot_general(pick, rs_ref[...] - cs_ref[...], _NT, precision=lax.Precision.HIGHEST,
                             preferred_element_type=F32)
        rc = jnp.dot(dc, tri, precision=lax.Precision.HIGHEST, preferred_element_type=F32) + carry_ref[:, 0:1]
        carry_ref[...] = jnp.broadcast_to(rc[:, 0:1], carry_ref.shape)
        df = rc / (1.0 + jnp.exp(f_ref[...] + b_ref[...]))
        df_ref[...] = df
        db_ref[...] += jnp.broadcast_to(jnp.sum(df, axis=1, keepdims=True), db_ref.shape)

    back = lambda k: n_ch - 1 - k
    wide = pl.BlockSpec((ch, 4 * LANES), lambda k: (back(k), 0))
    row = pl.BlockSpec((8, ch), lambda k: (0, back(k)))
    return pl.pallas_call(
        kern, grid=(n_ch,),
        in_specs=[wide, wide, row, pl.BlockSpec((8, 1), lambda k: (0, 0))],
        out_specs=[row, pl.BlockSpec((8, LANES), lambda k: (0, 0))],
        out_shape=[jax.ShapeDtypeStruct((8, n_tok), F32), jax.ShapeDtypeStruct((8, LANES), F32)],
        scratch_shapes=[pltpu.VMEM((8, LANES), F32)], name="decay_bwd", compiler_params=_params(1))(cs, rs, f_t, b_col)


def _swa_bias_table():
    row = jnp.arange(SWA_BLOCK)[:, None] + SWA_BLOCK
    col = jnp.arange(2 * SWA_BLOCK)[None, :]
    cd = (row >> CHUNK_SHIFT) - (col >> CHUNK_SHIFT)
    band = (cd >= 0) & (cd <= WINDOW_CHUNKS)
    slopes = jnp.asarray([2.0 ** -(h + 1) for h in range(SWA_HEADS)], F32)
    bias = -slopes[:, None, None] * jnp.abs(row - col).astype(F32)[None]
    return jnp.stack([jnp.where(band & (col >= SWA_BLOCK), bias, NEG), jnp.where(band, bias, NEG)])


SWA_PER_STEP = 4


def _swap_halves(t):
    return pltpu.roll(t.astype(F32), HEAD_DIM, axis=1).astype(t.dtype)


def _swa_specs():
    blk, rows = SWA_BLOCK, SWA_PER_STEP * SWA_BLOCK
    q = pl.BlockSpec((rows, 4 * LANES), lambda n: (n, 0))
    before = lambda col: pl.BlockSpec((blk, LANES), lambda n: (jnp.maximum(SWA_PER_STEP * n - 1, 0), col))
    own = lambda col: pl.BlockSpec((rows, LANES), lambda n: (n, col))
    bias = pl.BlockSpec((2, SWA_HEADS, blk, 2 * blk), lambda n: (0, 0, 0, 0))
    return [q, before(4), own(4), before(5), own(5), bias]


def _swa_band(before_ref, own_ref):
    both = jnp.concatenate([before_ref[...], own_ref[...]], axis=0)
    return both, _swap_halves(both)


def _swa_bias(bias_ref, n, b):
    return bias_ref.at[jnp.minimum(n, 1)] if b == 0 else bias_ref.at[1]


SWA_GROUPS = ([h for h in range(SWA_HEADS) if h % 2 == h // 4], [h for h in range(SWA_HEADS) if h % 2 != h // 4])


def _stack_heads(ref, rows, heads, lo, mask_halves):
    tiles = []
    for h in heads:
        t = ref[rows, (h // 2) * LANES:(h // 2 + 1) * LANES]
        tiles.append(jnp.where(lo if h % 2 == 0 else ~lo, t, jnp.zeros_like(t)) if mask_halves else t)
    return jnp.concatenate(tiles, axis=0)


def _per_head_column(values, heads):
    return jnp.concatenate([jnp.full((SWA_BLOCK, 1), values(h), F32) for h in heads], axis=0)


def _swa_scores(q_ref, rows, kx, heads, lo, bias):
    qa = _stack_heads(q_ref, rows, heads, lo, True) * SCALE
    return qa, _dot_nt(qa, kx) + jnp.concatenate([bias[h] for h in heads], axis=0)


def _swa_fwd(zm, sinks):
    n_tok = zm.shape[0]
    blk, step_rows = SWA_BLOCK, SWA_PER_STEP * SWA_BLOCK

    def kern(q_ref, kp_ref, kc_ref, vp_ref, vc_ref, bias_ref, sink_ref, o_ref, lse_ref):
        n = pl.program_id(0)
        (k_all, k_all_sw), (v_all, v_all_sw) = _swa_band(kp_ref, kc_ref), _swa_band(vp_ref, vc_ref)
        lane = lax.broadcasted_iota(jnp.int32, (blk, LANES), 1)
        lo = lane < HEAD_DIM
        for b in range(SWA_PER_STEP):
            rows, band = slice(b * blk, (b + 1) * blk), slice(b * blk, (b + 2) * blk)
            bias = _swa_bias(bias_ref, n, b)
            lse_t = jnp.zeros((blk, LANES), F32)
            for pair in range(SWA_HEADS // 2):
                q2 = q_ref[rows, pair * LANES:(pair + 1) * LANES]
                outs = []
                for a in range(2):
                    h = 2 * pair + a
                    qa = jnp.where(lo if a == 0 else ~lo, q2, jnp.zeros_like(q2)) * SCALE
                    kx, vx = (k_all[band], v_all[band]) if h in SWA_GROUPS[0] else (k_all_sw[band], v_all_sw[band])
                    s = _dot_nt(qa, kx) + bias[h]
                    sink = sink_ref[h]
                    m = jnp.maximum(jnp.max(s, axis=-1, keepdims=True), sink)
                    e = jnp.exp(s - m)
                    l = jnp.sum(e, axis=-1, keepdims=True) + jnp.exp(sink - m)
                    pn = (e * (1.0 / l)).astype(BF16)
                    outs.append(_dot(pn, vx))
                    lse_t = jnp.where(lane == h, m + jnp.log(l), lse_t)
                o_ref[rows, pair * LANES:(pair + 1) * LANES] = jnp.where(lo, outs[0], outs[1]).astype(BF16)
            lse_ref[rows, :] = lse_t

    return pl.pallas_call(
        kern, grid=(n_tok // step_rows,),
        in_specs=_swa_specs() + [pl.BlockSpec(memory_space=pltpu.SMEM)],
        out_specs=[pl.BlockSpec((step_rows, 4 * LANES), lambda n: (n, 0)),
                   pl.BlockSpec((step_rows, LANES), lambda n: (n, 0))],
        out_shape=[jax.ShapeDtypeStruct((n_tok, 4 * LANES), BF16), jax.ShapeDtypeStruct((n_tok, LANES), F32)],
        name="swa_fwd", compiler_params=_params(1))(zm, zm, zm, zm, zm, _swa_bias_table(), sinks)


def _swa_bwd(zm, sinks, d_out, out, lse):
    n_tok = zm.shape[0]
    blk, step_rows = SWA_BLOCK, SWA_PER_STEP * SWA_BLOCK

    def kern(q_ref, kp_ref, kc_ref, vp_ref, vc_ref, bias_ref, do_ref, o_ref, lse_ref, sink_ref,
             dq_ref, dkp_ref, dkc_ref, dvp_ref, dvc_ref, dsk_ref):
        n = pl.program_id(0)

        @pl.when(n == 0)
        def _():
            dsk_ref[...] = jnp.zeros_like(dsk_ref)

        bands = (_swa_band(kp_ref, kc_ref), _swa_band(vp_ref, vc_ref))
        lane = lax.broadcasted_iota(jnp.int32, (blk, LANES), 1)
        lo = lane < HEAD_DIM
        for b in range(SWA_PER_STEP):
            rows, band = slice(b * blk, (b + 1) * blk), slice(b * blk, (b + 2) * blk)
            bias = _swa_bias(bias_ref, n, b)
            lse_t = lse_ref[rows, :]
            dqs, dkv = {}, []
            for g, heads in enumerate(SWA_GROUPS):
                kx, vx = bands[0][g][band], bands[1][g][band]
                qa, s = _swa_scores(q_ref, rows, kx, heads, lo, bias)
                doa = _stack_heads(do_ref, rows, heads, lo, True)
                lse_g = jnp.concatenate([lse_t[:, h:h + 1] for h in heads], axis=0)
                prob = jnp.exp(s - lse_g)
                o_g = _stack_heads(o_ref, rows, heads, lo, False)
                dd = jnp.sum(doa.astype(F32) * o_g.astype(F32), axis=-1, keepdims=True)
                ds = (prob * (_dot_nt(doa, vx) - dd)).astype(BF16)
                sink_part = -jnp.exp(_per_head_column(lambda h: sink_ref[h], heads) - lse_g) * dd
                dq = _dot(ds, kx) * SCALE
                for r, h in enumerate(heads):
                    dqs[h] = dq[r * blk:(r + 1) * blk]
                    dsk_ref[h:h + 1, :] += jnp.broadcast_to(
                        jnp.sum(sink_part[r * blk:(r + 1) * blk], axis=0, keepdims=True), (1, LANES))
                dkv.append((_dot_tn(ds, qa), _dot_tn(prob.astype(BF16), doa)))
            for pair in range(SWA_HEADS // 2):
                dq_ref[rows, pair * LANES:(pair + 1) * LANES] = jnp.where(
                    lo, dqs[2 * pair], dqs[2 * pair + 1]).astype(BF16)
            dk = dkv[0][0] + pltpu.roll(dkv[1][0], HEAD_DIM, axis=1)
            dv = dkv[0][1] + pltpu.roll(dkv[1][1], HEAD_DIM, axis=1)
            dkp_ref[rows, :] = dk[0:blk]
            dkc_ref[rows, :] = dk[blk:2 * blk]
            dvp_ref[rows, :] = dv[0:blk]
            dvc_ref[rows, :] = dv[blk:2 * blk]

    wide = pl.BlockSpec((step_rows, 4 * LANES), lambda n: (n, 0))
    narrow = pl.BlockSpec((step_rows, LANES), lambda n: (n, 0))
    part = jax.ShapeDtypeStruct((n_tok, LANES), F32)
    return pl.pallas_call(
        kern, grid=(n_tok // step_rows,),
        in_specs=_swa_specs() + [wide, wide, narrow, pl.BlockSpec(memory_space=pltpu.SMEM)],
        out_specs=[wide, narrow, narrow, narrow, narrow, pl.BlockSpec((8, LANES), lambda n: (0, 0))],
        out_shape=[jax.ShapeDtypeStruct((n_tok, 4 * LANES), BF16), part, part, part, part,
                   jax.ShapeDtypeStruct((8, LANES), F32)],
        name="swa_bwd", compiler_params=_params(1))(zm, zm, zm, zm, zm, _swa_bias_table(), d_out, out, lse, sinks)


def _my_pos():
    return lax.axis_index("x"), lax.axis_index("y"), lax.axis_index("c")


def _peer(k):
    x, y, c = _my_pos()
    px, py, pc = x ^ (k >> 2), y ^ ((k >> 1) & 1), c ^ (k & 1)
    return (px, py, pc), 4 * px + 2 * py + pc


def _gather_copies(x_refs, out_refs, send_sems, recv_sems, local_sems):
    x, y, c = _my_pos()
    my_id = 4 * x + 2 * y + c
    local = [pltpu.make_async_copy(x_refs[w], out_refs[w].at[my_id], local_sems.at[w]) for w in range(len(x_refs))]
    sends, arrivals = [], []
    for k in range(1, N_DEV):
        peer, peer_id = _peer(k)
        for w in range(len(x_refs)):
            sems = dict(send_sem=send_sems.at[7 * w + k - 1], recv_sem=recv_sems.at[7 * w + k - 1],
                        device_id=peer, device_id_type=MESH)
            sends.append(pltpu.make_async_remote_copy(src_ref=x_refs[w], dst_ref=out_refs[w].at[my_id], **sems))
            arrivals.append(pltpu.make_async_remote_copy(src_ref=x_refs[w], dst_ref=out_refs[w].at[peer_id], **sems))
    return local, sends, arrivals


def _scatter_copies(g_refs, part_refs, send_sems, recv_sems, local_sems):
    x, y, c = _my_pos()
    my_id = 4 * x + 2 * y + c
    local = [pltpu.make_async_copy(g_refs[w].at[my_id], part_refs[w].at[0], local_sems.at[w])
             for w in range(len(g_refs))]
    sends, arrivals = [], []
    for k in range(1, N_DEV):
        peer, peer_id = _peer(k)
        for w in range(len(g_refs)):
            sems = dict(send_sem=send_sems.at[7 * w + k - 1], recv_sem=recv_sems.at[7 * w + k - 1],
                        device_id=peer, device_id_type=MESH)
            sends.append(pltpu.make_async_remote_copy(src_ref=g_refs[w].at[peer_id], dst_ref=part_refs[w].at[k], **sems))
            arrivals.append(pltpu.make_async_remote_copy(src_ref=g_refs[w].at[my_id], dst_ref=part_refs[w].at[k], **sems))
    return local, sends, arrivals


def _start_copies(local, sends, arrivals):
    for cp in local + sends:
        cp.start()


def _finish_copies(local, sends, arrivals):
    for cp in arrivals:
        cp.wait_recv()
    for cp in sends:
        cp.wait_send()
    for cp in local:
        cp.wait()


def _exchange_scratch(n_arrays):
    return [pltpu.SemaphoreType.DMA((7 * n_arrays,)), pltpu.SemaphoreType.DMA((7 * n_arrays,)),
            pltpu.SemaphoreType.DMA((n_arrays,))]


class _Ride:
    def __init__(self, arrays, out_shape, copies):
        self.arrays, self.out_shape, self.copies = list(arrays), list(out_shape), copies
        any_spec = pl.BlockSpec(memory_space=pl.ANY)
        self.in_specs = [any_spec] * len(self.arrays)
        self.out_specs = [any_spec] * len(self.arrays)
        self.scratch = _exchange_scratch(len(self.arrays)) if self.arrays else []

    @staticmethod
    def _at(grid, last):
        hit = [pl.program_id(d) == (n - 1 if last else 0) for d, n in enumerate(grid)]
        return hit[0] if len(hit) == 1 else jnp.logical_and(*hit)

    def at_first_step(self, grid, in_refs, out_refs, sems):
        @pl.when(self._at(grid, False))
        def _():
            _start_copies(*self.copies(in_refs, out_refs, *sems))

    def at_last_step(self, grid, in_refs, out_refs, sems):
        @pl.when(self._at(grid, True))
        def _():
            _finish_copies(*self.copies(in_refs, out_refs, *sems))


_NO_RIDE = _Ride([], [], None)


def _gather_ride(shards):
    return _Ride(shards, [jax.ShapeDtypeStruct((N_DEV,) + s.shape, s.dtype) for s in shards], _gather_copies)


def _scatter_ride(grads):
    return _Ride(grads, [jax.ShapeDtypeStruct(g.shape, g.dtype) for g in grads], _scatter_copies)


Q_COL, K_COL, V_COL = 6, 10, 14


def _causal(t, tq, tk):
    row = lax.broadcasted_iota(jnp.int32, (tq, tk), 0)
    col = lax.broadcasted_iota(jnp.int32, (tq, tk), 1)
    return jnp.where(col <= row, t, NEG)


def _lane_tile(stat, width):
    return jnp.tile(stat, (1, width // LANES))


def _fox_steps(nq):
    steps = [(i2, j, 0 if j < 2 * i2 else 1 + j - 2 * i2) for i2 in range(nq // 2) for j in range(2 * i2 + 2)]
    return [np.asarray(col, np.int32) for col in zip(*steps)]


_SWEEPS = {0: [(0, False), (1, False)], 1: [(0, True), (1, False)], 2: [(1, True)]}


def _fox_dispatch(sweep, kind, dead_ref, head0, idx):
    dead0, dead1 = dead_ref[head0, idx] > 0.5, dead_ref[head0 + 1, idx] > 0.5
    live0, live1 = jnp.logical_not(dead0), jnp.logical_not(dead1)
    below = kind == 0
    pl.when(jnp.logical_and(below, jnp.logical_and(live0, live1)))(lambda: sweep(_SWEEPS[0], (0, 1)))
    pl.when(jnp.logical_and(below, jnp.logical_and(live0, dead1)))(lambda: sweep(_SWEEPS[0], (0,)))
    pl.when(jnp.logical_and(below, jnp.logical_and(dead0, live1)))(lambda: sweep(_SWEEPS[0], (1,)))
    pl.when(kind == 1)(lambda: sweep(_SWEEPS[1], (0, 1)))
    pl.when(kind == 2)(lambda: sweep(_SWEEPS[2], (0, 1)))


EXP_ZERO = 104.5
NORM_SLACK = 1.005


def _fox_dead_steps(nrm, c_pairs, tq):
    nq = nrm.shape[0] // 8
    stats = nrm.reshape(nq, 8, LANES)[:, :3, :FOX_HEADS]
    qn, kn, own = jnp.sqrt(stats[:, 0]) * SCALE, jnp.sqrt(stats[:, 1]), stats[:, 2] * SCALE
    cb = c_pairs.reshape(FOX_HEADS, nq, tq)
    c_max, c_min = jnp.max(cb, axis=-1).T, jnp.min(cb, axis=-1).T
    both = lambda t, pick: pick(t.reshape(nq // 2, 2, FOX_HEADS), axis=1)
    qn2, kn2, c_max2, own2 = both(qn, jnp.max), both(kn, jnp.max), both(c_max, jnp.max), both(own, jnp.min)
    row_max_floor = own2 - (NORM_SLACK - 1.0) * qn2 * kn2 - c_max2
    gap = qn2[:, None] * kn[None] * NORM_SLACK - c_min[None] - row_max_floor[:, None]
    below = jnp.arange(nq)[None, :] < 2 * jnp.arange(nq // 2)[:, None]
    dead = jnp.logical_and(gap < -EXP_ZERO, below[..., None])
    return dead.transpose(2, 0, 1).reshape(FOX_HEADS, -1).astype(F32)


def _fox_fwd(zm, c_pairs, dead, tq, ride=None):
    n_tok = zm.shape[0]
    nq = n_tok // tq
    ii, jj, kk = _fox_steps(nq)
    n_steps = len(ii)
    n_ride = len(ride.arrays) if ride else 0

    def kern(ii_ref, jj_ref, kk_ref, q_ref, k_ref, v_ref, ck_ref, dead_ref, *more):
        ride_in, (o_ref, ln_ref), ride_out = more[:n_ride], more[n_ride:n_ride + 2], more[n_ride + 2:2 * n_ride + 2]
        qs_ref, m_ref, l_ref, acc_ref = more[2 * n_ride + 2:2 * n_ride + 6]
        step = pl.program_id(1)
        j, kind = jj_ref[step], kk_ref[step]
        lo = lax.broadcasted_iota(jnp.int32, (2 * tq, LANES), 1) < HEAD_DIM
        if ride:
            ride.at_first_step((FOX_HEADS // 2, n_steps), ride_in, ride_out, more[2 * n_ride + 6:])

        @pl.when(j == 0)
        def _():
            q2 = q_ref[...]
            zq = jnp.zeros_like(q2)
            qs_ref[0] = jnp.where(lo, q2, zq) * SCALE
            qs_ref[1] = jnp.where(lo, zq, q2) * SCALE
            m_ref[...] = jnp.full(m_ref.shape, NEG, F32)
            l_ref[...] = jnp.zeros(l_ref.shape, F32)
            acc_ref[...] = jnp.zeros(acc_ref.shape, F32)

        def sweep(subs, heads):
            kv = k_ref[...]
            v_ones = jnp.concatenate([v_ref[...], jnp.ones((tq, LANES), BF16)], axis=1)
            for sub, diag in subs:
                rows = slice(sub * tq, (sub + 1) * tq)
                for a in heads:
                    t = _dot_nt(qs_ref[a, rows], kv) - ck_ref[a:a + 1, :]
                    if diag:
                        t = _causal(t, tq, tq)
                    m_old = m_ref[a, rows]
                    m_new = jnp.maximum(m_old, jnp.max(t, axis=-1, keepdims=True))
                    alpha = jnp.exp(m_old - m_new)
                    e = jnp.exp(t - _lane_tile(m_new, tq)).astype(BF16)
                    pv = _dot(e, v_ones)
                    acc_ref[a, rows] = alpha * acc_ref[a, rows] + pv[:, :LANES]
                    l_ref[a, rows] = alpha * l_ref[a, rows] + pv[:, LANES:]
                    m_ref[a, rows] = m_new

        _fox_dispatch(sweep, kind, dead_ref, 2 * pl.program_id(0), ii_ref[step] * nq + j)

        @pl.when(kind == 2)
        def _():
            o_ref[...] = jnp.where(lo, acc_ref[0] / l_ref[0], acc_ref[1] / l_ref[1]).astype(BF16)
            ln_ref[:, :LANES] = m_ref[0] + jnp.log(l_ref[0])
            ln_ref[:, LANES:] = m_ref[1] + jnp.log(l_ref[1])

        if ride:
            ride.at_last_step((FOX_HEADS // 2, n_steps), ride_in, ride_out, more[2 * n_ride + 6:])

    blk = (tq, LANES)
    by_i = lambda col: (lambda hp, s, ii, jj, kk: (ii[s], col + hp))
    by_j = lambda col: (lambda hp, s, ii, jj, kk: (jj[s], col + hp))
    extra = ride if ride else _NO_RIDE
    grid_spec = pltpu.PrefetchScalarGridSpec(
        num_scalar_prefetch=3, grid=(FOX_HEADS // 2, n_steps),
        in_specs=[pl.BlockSpec((2 * tq, LANES), by_i(Q_COL)), pl.BlockSpec(blk, by_j(K_COL)),
                  pl.BlockSpec(blk, by_j(V_COL)),
                  pl.BlockSpec((None, 2, tq), lambda hp, s, ii, jj, kk: (hp, 0, jj[s])),
                  pl.BlockSpec(memory_space=pltpu.SMEM)] + extra.in_specs,
        out_specs=[pl.BlockSpec((2 * tq, LANES), by_i(0)), pl.BlockSpec((2 * tq, 2 * LANES), by_i(0))] + extra.out_specs,
        scratch_shapes=[pltpu.VMEM((2, 2 * tq, LANES), BF16), pltpu.VMEM((2, 2 * tq, LANES), F32),
                        pltpu.VMEM((2, 2 * tq, LANES), F32), pltpu.VMEM((2, 2 * tq, LANES), F32)] + extra.scratch)
    return pl.pallas_call(
        kern, grid_spec=grid_spec,
        out_shape=[jax.ShapeDtypeStruct((n_tok, 4 * LANES), BF16),
                   jax.ShapeDtypeStruct((n_tok, FOX_HEADS * LANES), F32)] + extra.out_shape,
        name="fox_fwd", compiler_params=_params(2))(ii, jj, kk, zm, zm, zm, c_pairs, dead, *extra.arrays)


def _fox_bwd(zm, c_pairs, dead, d_out, lnorm, delta, tq, ride=None):
    n_tok = zm.shape[0]
    nq = n_tok // tq
    ii, jj, kk = _fox_steps(nq)
    n_steps = len(ii)
    n_ride = len(ride.arrays) if ride else 0

    def kern(ii_ref, jj_ref, kk_ref, q_ref, k_ref, v_ref, ck_ref, dead_ref, do_ref, ln_ref, dl_ref, *more):
        ride_in, ride_out = more[:n_ride], more[n_ride + 5:2 * n_ride + 5]
        dq_ref, dk_out, dv_out, cs_ref, rs_ref = more[n_ride:n_ride + 5]
        qs_ref, qo_ref, dos_ref, dq_acc, dk_ref, dv_ref = more[2 * n_ride + 5:2 * n_ride + 11]
        step = pl.program_id(1)
        j, kind = jj_ref[step], kk_ref[step]
        lo = lax.broadcasted_iota(jnp.int32, (2 * tq, LANES), 1) < HEAD_DIM
        if ride:
            ride.at_first_step((FOX_HEADS // 2, n_steps), ride_in, ride_out, more[2 * n_ride + 11:])

        @pl.when(step == 0)
        def _():
            dk_ref[...] = jnp.zeros_like(dk_ref)
            dv_ref[...] = jnp.zeros_like(dv_ref)
            cs_ref[...] = jnp.zeros_like(cs_ref)

        @pl.when(j == 0)
        def _():
            q2, do2 = q_ref[...], do_ref[...]
            zq = jnp.zeros_like(q2)
            ones = jnp.ones((2 * tq, LANES), BF16)
            for a in range(2):
                half = lo if a == 0 else ~lo
                qa = jnp.where(half, q2, zq) * SCALE
                qs_ref[a] = qa
                qo_ref[a] = jnp.concatenate([qa, ones], axis=1)
                dos_ref[a] = jnp.where(half, do2, zq)
            dq_acc[...] = jnp.zeros(dq_acc.shape, F32)

        def sweep(subs, heads):
            kv, vv = k_ref[...], v_ref[...]
            k_ones = jnp.concatenate([kv, jnp.ones((tq, LANES), BF16)], axis=1)
            dk, dv, sums = None, None, {}
            for sub, diag in subs:
                rows = slice(sub * tq, (sub + 1) * tq)
                for a in heads:
                    t = _dot_nt(qs_ref[a, rows], kv) - ck_ref[a:a + 1, :]
                    if diag:
                        t = _causal(t, tq, tq)
                    prob = jnp.exp(t - _lane_tile(ln_ref[rows, a * LANES:(a + 1) * LANES], tq))
                    dp = _dot_nt(dos_ref[a, rows], vv)
                    ds = (prob * (dp - _lane_tile(dl_ref[rows, a * LANES:(a + 1) * LANES], tq))).astype(BF16)
                    dq_acc[a, rows] += _dot(ds, k_ones)
                    dk_cs = _dot_tn(ds, qo_ref[a, rows])
                    dv_a = _dot_tn(prob.astype(BF16), dos_ref[a, rows])
                    dk = dk_cs[:, :LANES] if dk is None else dk + dk_cs[:, :LANES]
                    dv = dv_a if dv is None else dv + dv_a
                    sums[a] = dk_cs[:, LANES:] if a not in sums else sums[a] + dk_cs[:, LANES:]
            keys = pl.ds(pl.multiple_of(j * tq, tq), tq)
            dk_ref[keys, :] += dk
            cs_ref[keys, :] += jnp.where(lo[:tq], sums.get(0, 0.0), sums.get(1, 0.0))
            dv_ref[keys, :] += dv

        _fox_dispatch(sweep, kind, dead_ref, 2 * pl.program_id(0), ii_ref[step] * nq + j)

        @pl.when(kind == 2)
        def _():
            dq_ref[...] = (jnp.where(lo, dq_acc[0, :, :LANES], dq_acc[1, :, :LANES]) * SCALE).astype(BF16)
            rs_ref[...] = jnp.where(lo, dq_acc[0, :, LANES:], dq_acc[1, :, LANES:])

        @pl.when(step == n_steps - 1)
        def _():
            dk_out[...] = dk_ref[...].astype(BF16)
            dv_out[...] = dv_ref[...].astype(BF16)

        if ride:
            ride.at_last_step((FOX_HEADS // 2, n_steps), ride_in, ride_out, more[2 * n_ride + 11:])

    blk = (tq, LANES)
    by_i = lambda col: (lambda hp, s, ii, jj, kk: (ii[s], col + hp))
    by_j = lambda col: (lambda hp, s, ii, jj, kk: (jj[s], col + hp))
    resident = pl.BlockSpec((2 * tq, LANES), by_i(0))
    stat = pl.BlockSpec((2 * tq, 2 * LANES), by_i(0))
    whole = pl.BlockSpec((n_tok, LANES), lambda hp, s, ii, jj, kk: (0, hp))
    extra = ride if ride else _NO_RIDE
    grid_spec = pltpu.PrefetchScalarGridSpec(
        num_scalar_prefetch=3, grid=(FOX_HEADS // 2, n_steps),
        in_specs=[pl.BlockSpec((2 * tq, LANES), by_i(Q_COL)), pl.BlockSpec(blk, by_j(K_COL)),
                  pl.BlockSpec(blk, by_j(V_COL)),
                  pl.BlockSpec((None, 2, tq), lambda hp, s, ii, jj, kk: (hp, 0, jj[s])),
                  pl.BlockSpec(memory_space=pltpu.SMEM), resident, stat, stat] + extra.in_specs,
        out_specs=[resident, whole, whole, whole, resident] + extra.out_specs,
        scratch_shapes=[pltpu.VMEM((2, 2 * tq, LANES), BF16), pltpu.VMEM((2, 2 * tq, 2 * LANES), BF16),
                        pltpu.VMEM((2, 2 * tq, LANES), BF16), pltpu.VMEM((2, 2 * tq, 2 * LANES), F32),
                        pltpu.VMEM((n_tok, LANES), F32), pltpu.VMEM((n_tok, LANES), F32)] + extra.scratch)
    wide = lambda dt: jax.ShapeDtypeStruct((n_tok, 4 * LANES), dt)
    return pl.pallas_call(
        kern, grid_spec=grid_spec, name="fox_bwd",
        out_shape=[wide(BF16), wide(BF16), wide(BF16), wide(F32), wide(F32)] + extra.out_shape,
        compiler_params=_params(2, FOX_BWD_VMEM))(ii, jj, kk, zm, zm, zm, c_pairs, dead, d_out, lnorm, delta,
                                                  *extra.arrays)


def _all_gather(shards):
    n_w = len(shards)

    def kern(*refs):
        x_refs, out_refs = refs[:n_w], refs[n_w:2 * n_w]
        send_sems, recv_sems, local_sems = refs[2 * n_w:]
        x, y, c = _my_pos()
        me, sibling = (x, y, c), (x, y, 1 - c)
        chips = [(1 - x, y), (x, 1 - y), (1 - x, 1 - y)]

        def slot(w, px, py, pc):
            return out_refs[w].at[4 * px + 2 * py + pc]

        def copy(w, k, block, to, src=None):
            return pltpu.make_async_remote_copy(
                src_ref=slot(w, *block) if src is None else src, dst_ref=slot(w, *block),
                send_sem=send_sems.at[7 * w + k], recv_sem=recv_sems.at[7 * w + k], device_id=to, device_id_type=MESH)

        local, started = [], []
        for w in range(n_w):
            mine = pltpu.make_async_copy(x_refs[w], slot(w, *me), local_sems.at[w])
            mine.start()
            local.append(mine)
            first = [copy(w, 0, me, sibling, src=x_refs[w])]
            first += [copy(w, 1 + k, me, (*chip, c), src=x_refs[w]) for k, chip in enumerate(chips)]
            for cp in first:
                cp.start()
            started += first
        for k, chip in enumerate(chips):
            for w in range(n_w):
                copy(w, 1 + k, (*chip, c), me).wait_recv()
                passed = copy(w, 4 + k, (*chip, c), sibling)
                passed.start()
                started.append(passed)
        for w in range(n_w):
            copy(w, 0, sibling, me).wait_recv()
            for k, chip in enumerate(chips):
                copy(w, 4 + k, (*chip, 1 - c), me).wait_recv()
        for cp in started:
            cp.wait_send()
        for cp in local:
            cp.wait()

    any_spec = pl.BlockSpec(memory_space=pl.ANY)
    return pl.pallas_call(
        kern, out_shape=[jax.ShapeDtypeStruct((N_DEV,) + s.shape, s.dtype) for s in shards],
        in_specs=[any_spec] * n_w, out_specs=[any_spec] * n_w,
        scratch_shapes=[pltpu.SemaphoreType.DMA((7 * n_w,)), pltpu.SemaphoreType.DMA((7 * n_w,)),
                        pltpu.SemaphoreType.DMA((n_w,))],
        name="weight_all_gather")(*shards)


def _small_exchange(small):
    def kern(s_ref, sall_ref, *sems):
        copies = _gather_copies([s_ref], [sall_ref], *sems)
        _start_copies(*copies)
        _finish_copies(*copies)

    any_spec = pl.BlockSpec(memory_space=pl.ANY)
    return pl.pallas_call(
        kern, out_shape=jax.ShapeDtypeStruct((N_DEV,) + small.shape, small.dtype), in_specs=[any_spec],
        out_specs=any_spec, scratch_shapes=_exchange_scratch(1), name="small_grad_exchange")(small)


ADAMW_BLOCK_BYTES = 2 * 1024 * 1024


def _adamw(parts, w, m, v, name):
    n_parts, n_rows, n_cols = parts.shape
    limit = max(8, ADAMW_BLOCK_BYTES // (n_parts * n_cols * parts.dtype.itemsize))
    tr = max(t for t in range(8, n_rows + 1, 8) if n_rows % t == 0 and t <= limit)

    def kern(p_ref, w_ref, m_ref, v_ref, g_out, d_out, m_out, v_out):
        g = p_ref[0].astype(F32)
        for k in range(1, n_parts):
            g = g + p_ref[k].astype(F32)
        m_new = ADAM_B1 * m_ref[...] + (1.0 - ADAM_B1) * g
        v_new = ADAM_B2 * v_ref[...] + (1.0 - ADAM_B2) * jnp.square(g)
        m_hat = m_new / (1.0 - ADAM_B1 ** ADAM_STEP)
        v_hat = v_new / (1.0 - ADAM_B2 ** ADAM_STEP)
        g_out[...] = g
        d_out[...] = -ADAM_LR * (m_hat / (jnp.sqrt(v_hat) + ADAM_EPS) + ADAM_WD * w_ref[...])
        m_out[...] = m_new
        v_out[...] = v_new

    row = pl.BlockSpec((tr, n_cols), lambda i: (i, 0))
    out = jax.ShapeDtypeStruct((n_rows, n_cols), F32)
    return pl.pallas_call(
        kern, grid=(n_rows // tr,),
        in_specs=[pl.BlockSpec((n_parts, tr, n_cols), lambda i: (0, i, 0)), row, row, row],
        out_specs=[row, row, row, row], out_shape=[out, out, out, out], name=name,
        compiler_params=_params(1))(parts, w, m, v)


SHARDED = {
    "w_in": ((D_MODEL, D_IN), 1), "w_br_swa": ((512, D_MODEL), 1), "w_br_fox": ((512, D_MODEL), 1),
    "w_mix_out": ((D_MODEL, D_MODEL), 0), "w_ff1": ((D_MODEL, D_FF), 1), "w_ff2": ((D_FF, D_MODEL), 0),
    "w_ple_gate": ((D_MODEL, D_MODEL), 0), "w_ple_proj": ((PLE_DIM, D_MODEL), 1),
}
W_IN_SHARD = D_IN // N_DEV
W_IN_PAD = 640
SMALL = ("g_mix", "g_mlp", "g_ple", "g_final", "b_forget", "swa_sinks")
SMALL_COLS = 1024


def _wire_shard(name, a):
    a = a.reshape(a.shape[-2:])
    return jnp.pad(a, ((0, 0), (0, W_IN_PAD - W_IN_SHARD))) if name == "w_in" else a


def _from_wire(name, a):
    return (a[:, :W_IN_SHARD] if name == "w_in" else a)[None]


def _w_all_from_wire(stacked):
    w_in = jnp.concatenate([stacked[d][:, :W_IN_SHARD] for d in range(N_DEV)], axis=1)
    fpad = jnp.zeros((D_MODEL, N_FPAD - FOX_HEADS), stacked.dtype)
    return jnp.concatenate([w_in[:, :N_MAIN + FOX_HEADS], fpad, w_in[:, N_MAIN + FOX_HEADS:]], axis=1)


def _dw_in_to_wire(dw_all):
    dw_in = jnp.concatenate([dw_all[:, :N_MAIN + FOX_HEADS], dw_all[:, N_MAIN + N_FPAD:]], axis=1)
    pad = jnp.zeros((D_MODEL, W_IN_PAD - W_IN_SHARD), dw_all.dtype)
    return jnp.stack([jnp.concatenate([dw_in[:, d * W_IN_SHARD:(d + 1) * W_IN_SHARD], pad], axis=1)
                      for d in range(N_DEV)])


def _pack_small(vals, scalar=None):
    rows = [jnp.pad(vals[n].reshape(-1), (0, SMALL_COLS - vals[n].size)) for n in SMALL]
    if scalar is not None:
        rows.append(jnp.pad(scalar.reshape(1), (0, SMALL_COLS - 1)))
    rows += [jnp.zeros((SMALL_COLS,), F32)] * (8 - len(rows))
    return jnp.stack(rows)


def _unpack_small(slab, like):
    return {n: slab[r, :like[n].size].reshape(like[n].shape) for r, n in enumerate(SMALL)}


def _local_step(x, p, tgt, w, small, tm, tq, ts, late_shards=None):
    n_tok = x.shape[0]
    row = lambda v: v.reshape(1, -1)
    g_mix, g_mlp, g_ple, g_fin = row(small["g_mix"]), row(small["g_mlp"]), row(small["g_ple"]), row(small["g_final"])
    sinks = small["swa_sinks"].reshape(-1)
    b_col = small["b_forget"].reshape(FOX_HEADS, 1)

    assert tm == tq
    u1, zm, zfg, zf, nrm = _in_proj(x, g_mix, w["w_all"], tm)
    f_t = zf[:, :FOX_HEADS].T
    c_pairs = _decay_cumsum(f_t, b_col).reshape(FOX_HEADS // 2, 2, n_tok)
    attn_a, lse_a = _swa_fwd(zm, sinks)
    dead = _fox_dead_steps(nrm, c_pairs, tq)
    if late_shards is None:
        attn_b, ln_b = _fox_fwd(zm, c_pairs, dead, tq)
    else:
        attn_b, ln_b, *late = _fox_fwd(zm, c_pairs, dead, tq, _gather_ride(list(late_shards.values())))
        w = {**w, **_gathered_to_local(dict(zip(late_shards, late)))}
    ya, yb, mixed, h1, u2, a, r, h2 = _mix_ffn_fwd(attn_a, attn_b, zfg, x, w["w_br_swa"], w["w_br_fox"],
                                                   w["w_mix_out"], g_mlp, w["w_ff1"], w["w_ff2"], tm // 2)

    dlg, dpp, u3, dh2, dh2b, da, loss_acc, dgf, dgp = _head_ffn_bwd(
        h2, p, tgt, a, g_ple, w["w_ple_gate"], w["w_ple_proj"], g_fin, w["w_ff2"], tm // 2)
    dh1, dh1b, dgl, dya, dyb, daa, dab, delta_b, dgm = _ffn_bwd_b(
        da, dh2, h1, ya, yb, zfg, attn_b, w["w_ff1"], g_mlp, w["w_mix_out"], w["w_br_swa"], w["w_br_fox"], tm // 2)
    dq_a, dkp, dkc, dvp, dvc, dsk = _swa_bwd(zm, sinks, daa, attn_a, lse_a)
    dw = {
        "w_br_swa": _matmul_tn(attn_a, dya, "dw_br_swa", ts, stack_cols=D_MODEL // N_DEV),
        "w_br_fox": _matmul_tn(attn_b, dyb, "dw_br_fox", ts, stack_cols=D_MODEL // N_DEV),
        "w_mix_out": _matmul_tn(mixed, dh1b, "dw_mix_out", ts),
        "w_ff1": _matmul_tn(u2, da, "dw_ff1", ts, stack_cols=D_FF // N_DEV),
        "w_ff2": _matmul_tn(r, dh2b, "dw_ff2", ts),
        "w_ple_gate": _matmul_tn(u3, dlg, "dw_ple_gate", ts),
        "w_ple_proj": _matmul_tn(p, dpp, "dw_ple_proj", ts, stack_cols=D_MODEL // N_DEV),
    }
    if late_shards is None:
        dq_b, dk_b, dv_b, cs, rs = _fox_bwd(zm, c_pairs, dead, dab, ln_b, delta_b, tq)
        late_parts = None
    else:
        wire = _local_to_wire(dw)
        dq_b, dk_b, dv_b, cs, rs, *parts = _fox_bwd(zm, c_pairs, dead, dab, ln_b, delta_b, tq,
                                                    _scatter_ride([wire[n] for n in late_shards]))
        late_parts = dict(zip(late_shards, parts))

    up = lambda t: jnp.concatenate([t[SWA_BLOCK:], jnp.zeros((SWA_BLOCK, LANES), F32)], axis=0)
    dk_a, dv_a = dkc + up(dkp), dvc + up(dvp)
    df_t, db = _decay_bwd(cs, rs, f_t, b_col)
    df = jnp.pad(df_t.T, ((0, 0), (0, N_FPAD - FOX_HEADS)))
    dz = jnp.concatenate([dq_a, dk_a.astype(BF16), dv_a.astype(BF16), dq_b, dk_b, dv_b,
                          df.astype(BF16), dgl], axis=1)
    dw["w_all"] = _matmul_tn(u1, dz, "dw_in", ts)
    if late_shards is None:
        dx, dgx = _in_proj_bwd(dz, dh1, x, w["w_all"], g_mix, tm)
    else:
        dx, dgx, late_parts["w_in"] = _in_proj_bwd(dz, dh1, x, w["w_all"], g_mix, tm,
                                                   _scatter_ride([_dw_in_to_wire(dw["w_all"])]))
    dsmall = {"g_mix": dgx[0], "g_mlp": dgm[0], "g_ple": dgp[0], "g_final": dgf[0],
              "b_forget": db[:, 0], "swa_sinks": dsk[:, 0]}
    return loss_acc[0, 0], dx, dw, dsmall, late_parts


_ROWS = lambda t: t.reshape(-1, t.shape[-1])
_BY_ROWS = lambda t: t.reshape(N_DEV, t.shape[0] // N_DEV, t.shape[1])
_SAME = lambda t: t
LOCAL_LAYOUT = {
    "w_in": ("w_all", _w_all_from_wire, _dw_in_to_wire), "w_br_swa": ("w_br_swa", _SAME, _SAME),
    "w_br_fox": ("w_br_fox", _SAME, _SAME), "w_mix_out": ("w_mix_out", _ROWS, _BY_ROWS),
    "w_ff1": ("w_ff1", _SAME, _SAME), "w_ff2": ("w_ff2", _SAME, _BY_ROWS),
    "w_ple_gate": ("w_ple_gate", _ROWS, _BY_ROWS), "w_ple_proj": ("w_ple_proj", _SAME, _SAME),
}


def _gathered_to_local(g):
    return {LOCAL_LAYOUT[n][0]: LOCAL_LAYOUT[n][1](t) for n, t in g.items()}


def _local_to_wire(dw):
    names = {local: n for n, (local, _, _) in LOCAL_LAYOUT.items()}
    return {names[local]: LOCAL_LAYOUT[names[local]][2](t) for local, t in dw.items()}


def kernel(x, p, g_mix, w_in, b_forget, swa_sinks, w_br_swa, w_br_fox, w_mix_out, g_mlp, w_ff1, w_ff2, g_ple, w_ple_gate, w_ple_proj, g_final, loss_target, m_g_mix, m_w_in, m_b_forget, m_swa_sinks, m_w_br_swa, m_w_br_fox, m_w_mix_out, m_g_mlp, m_w_ff1, m_w_ff2, m_g_ple, m_w_ple_gate, m_w_ple_proj, m_g_final, v_g_mix, v_w_in, v_b_forget, v_swa_sinks, v_w_br_swa, v_w_br_fox, v_w_mix_out, v_g_mlp, v_w_ff1, v_w_ff2, v_g_ple, v_w_ple_gate, v_w_ple_proj, v_g_final):
    given = dict(g_mix=g_mix, w_in=w_in, b_forget=b_forget, swa_sinks=swa_sinks, w_br_swa=w_br_swa, w_br_fox=w_br_fox,
                 w_mix_out=w_mix_out, g_mlp=g_mlp, w_ff1=w_ff1, w_ff2=w_ff2, g_ple=g_ple, w_ple_gate=w_ple_gate,
                 w_ple_proj=w_ple_proj, g_final=g_final)
    mom = dict(g_mix=m_g_mix, w_in=m_w_in, b_forget=m_b_forget, swa_sinks=m_swa_sinks, w_br_swa=m_w_br_swa,
               w_br_fox=m_w_br_fox, w_mix_out=m_w_mix_out, g_mlp=m_g_mlp, w_ff1=m_w_ff1, w_ff2=m_w_ff2, g_ple=m_g_ple,
               w_ple_gate=m_w_ple_gate, w_ple_proj=m_w_ple_proj, g_final=m_g_final)
    vel = dict(g_mix=v_g_mix, w_in=v_w_in, b_forget=v_b_forget, swa_sinks=v_swa_sinks, w_br_swa=v_w_br_swa,
               w_br_fox=v_w_br_fox, w_mix_out=v_w_mix_out, g_mlp=v_g_mlp, w_ff1=v_w_ff1, w_ff2=v_w_ff2, g_ple=v_g_ple,
               w_ple_gate=v_w_ple_gate, w_ple_proj=v_w_ple_proj, g_final=v_g_final)
    names = list(given)
    sharded = list(SHARDED)

    w_wire = {n: _wire_shard(n, given[n]) for n in sharded}
    late = [n for n in sharded if n != "w_in"]
    gathered = _all_gather([w_wire["w_in"].astype(BF16)])
    local_w = _gathered_to_local({"w_in": gathered[0]})
    small = {n: given[n].reshape(-1) for n in SMALL}

    n_tok = x.shape[1]
    tile = min(TOKEN_TILE, n_tok // 4)
    loss_part, dx, dw, dsmall, parts = _local_step(
        x[0], p[0, 0], loss_target[0], local_w, small, tm=tile, tq=tile, ts=min(DW_TOKENS_PER_STEP, n_tok // 4),
        late_shards={n: w_wire[n].astype(BF16) for n in late})
    small_all = _small_exchange(_pack_small(dsmall, loss_part))

    res = {}
    for n in sharded:
        part = parts[n]
        flat = part.reshape(N_DEV, -1, part.shape[-1])
        outs = _adamw(flat, w_wire[n], _wire_shard(n, mom[n]), _wire_shard(n, vel[n]), "adamw_" + n)
        res[n] = [_from_wire(n, o) for o in outs]
    outs_s = _adamw(small_all, _pack_small(small), _pack_small({n: mom[n] for n in SMALL}),
                    _pack_small({n: vel[n] for n in SMALL}), "adamw_small")
    small_res = [_unpack_small(o, given) for o in outs_s]
    loss = outs_s[0][len(SMALL), 0]

    groups = [[res[n][k] if n in res else small_res[k][n] for n in names] for k in range(4)]
    return (loss, dx[None], *groups[0], *groups[1], *groups[2], *groups[3])
```

```python
import numpy as np
import jax
import jax.numpy as jnp
from jax import lax
from jax.experimental import pallas as pl
from jax.experimental.pallas import tpu as pltpu

F32 = jnp.float32
BF16 = jnp.bfloat16

D_MODEL = 1024
HEAD_DIM = 64
SWA_HEADS = 8
FOX_HEADS = 8
CHUNK_SHIFT = 6
SWA_BLOCK = 128
WINDOW_CHUNKS = 2
D_FF = 4096
PLE_DIM = 256
RMS_EPS = 1e-6
N_MAIN = 2304
N_FPAD = 128
N_GATE = 2048
D_IN = N_MAIN + FOX_HEADS + N_GATE
SCALE = HEAD_DIM ** -0.5
NEG = -1e30

ADAM_LR = 0.001
ADAM_B1 = 0.9
ADAM_B2 = 0.999
ADAM_EPS = 1e-08
ADAM_WD = 0.01
ADAM_STEP = 10

N_DEV = 8
TOKEN_TILE = 512
DW_TOKENS_PER_STEP = 2048
LANES = 128
V7X_VMEM_BYTES = 64 * 1024 * 1024
VMEM_LIMIT = V7X_VMEM_BYTES * 3 // 4
FOX_BWD_VMEM = V7X_VMEM_BYTES * 7 // 8
MESH = pl.DeviceIdType.MESH

_NT = (((1,), (1,)), ((), ()))
_TN = (((0,), (0,)), ((), ()))


def _params(n_grid, vmem_limit=VMEM_LIMIT):
    return pltpu.CompilerParams(dimension_semantics=("arbitrary",) * n_grid, vmem_limit_bytes=vmem_limit)


def _chunks(n, step):
    return [(s, min(step, n - s)) for s in range(0, n, step)]


def _sigmoid(x):
    return 1.0 / (1.0 + jnp.exp(-x))


def _dot(a, b):
    return jnp.dot(a, b, preferred_element_type=F32)


def _dot_nt(a, b):
    return lax.dot_general(a, b, _NT, preferred_element_type=F32)


def _dot_tn(a, b):
    return lax.dot_general(a, b, _TN, preferred_element_type=F32)


def _lane_concat(stacked_ref):
    return jnp.concatenate([stacked_ref[d] for d in range(N_DEV)], axis=1)


def _rms(h):
    return lax.rsqrt(jnp.mean(h * h, axis=-1, keepdims=True) + RMS_EPS)


def _rms_bwd(h, g, du):
    rs = _rms(h)
    n = h * rs
    dn = du * g
    dh = rs * (dn - n * jnp.mean(dn * n, axis=-1, keepdims=True))
    return dh, jnp.sum(du * n, axis=0, keepdims=True)


def _acc_rows(ref, i, row):
    @pl.when(i == 0)
    def _():
        ref[...] = jnp.zeros_like(ref)
    ref[...] += jnp.broadcast_to(row, ref.shape)


def _row_call(body, name, n_rows, tm, row_ins, const_ins, row_outs, acc_outs, ride=None, tile_outs=()):
    row_outs = list(row_outs)
    n_ri, n_ci, n_ro, n_ao = len(row_ins), len(const_ins), len(row_outs) + len(tile_outs), len(acc_outs)
    extra = ride if ride else _NO_RIDE
    n_ride = len(extra.arrays)
    grid = (n_rows // tm,)

    def kern(*refs):
        i = pl.program_id(0)
        ins, refs = refs[:n_ri + n_ci], refs[n_ri + n_ci:]
        ride_in, refs = refs[:n_ride], refs[n_ride:]
        outs, refs = refs[:n_ro + n_ao], refs[n_ro + n_ao:]
        ride_out, sems = refs[:n_ride], refs[n_ride:]
        if ride:
            ride.at_first_step(grid, ride_in, ride_out, sems)
        body(i, ins[:n_ri], ins[n_ri:], outs[:n_ro], outs[n_ro:])
        if ride:
            ride.at_last_step(grid, ride_in, ride_out, sems)

    def whole(a):
        zeros = (0,) * a.ndim
        return pl.BlockSpec(a.shape, lambda i: zeros, pipeline_mode=pl.Buffered(1))

    in_specs = [pl.BlockSpec((tm, a.shape[1]), lambda i: (i, 0)) for a in row_ins]
    in_specs += [whole(a) for a in const_ins] + extra.in_specs
    out_specs = [pl.BlockSpec((tm, c), lambda i: (i, 0)) for c, _ in row_outs]
    out_specs += [pl.BlockSpec((8, c), lambda i: (i, 0)) for c in tile_outs]
    out_specs += [pl.BlockSpec((8, c), lambda i: (0, 0)) for c in acc_outs] + extra.out_specs
    out_shape = [jax.ShapeDtypeStruct((n_rows, c), dt) for c, dt in row_outs]
    out_shape += [jax.ShapeDtypeStruct((8 * grid[0], c), F32) for c in tile_outs]
    out_shape += [jax.ShapeDtypeStruct((8, c), F32) for c in acc_outs] + extra.out_shape
    return pl.pallas_call(kern, grid=grid, in_specs=in_specs, out_specs=out_specs, out_shape=out_shape,
                          scratch_shapes=extra.scratch, name=name,
                          compiler_params=_params(1))(*row_ins, *const_ins, *extra.arrays)


def _in_proj(x, g_mix, w_all, tm):
    def body(i, ins, consts, outs, accs):
        x_ref, = ins
        g_ref, w_ref = consts
        u_ref, zm_ref, zfg_ref, zf_ref, nrm_ref = outs
        xv = x_ref[...]
        u = ((xv * _rms(xv)) * g_ref[...]).astype(BF16)
        u_ref[...] = u
        for s, n in _chunks(N_MAIN, 768):
            zm_ref[:, s:s + n] = _dot(u, w_ref[:, s:s + n]).astype(BF16)
        for s, n in _chunks(N_FPAD + N_GATE, 512):
            zfg_ref[:, s:s + n] = _dot(u, w_ref[:, N_MAIN + s:N_MAIN + s + n])
        zf_ref[...] = zfg_ref[:, :N_FPAD]
        lane = lax.broadcasted_iota(jnp.int32, (4 * LANES, LANES), 0)
        head = lax.broadcasted_iota(jnp.int32, (4 * LANES, LANES), 1)
        pick = (lane // HEAD_DIM == head).astype(BF16)
        tq_, tk_ = (zm_ref[:, col * LANES:(col + 4) * LANES].astype(F32) for col in (Q_COL, K_COL))
        rows = [jnp.max(_dot((t * t).astype(BF16), pick), axis=0, keepdims=True) for t in (tq_, tk_)]
        rows.append(jnp.min(_dot((tq_ * tk_).astype(BF16), pick), axis=0, keepdims=True))
        nrm_ref[...] = jnp.concatenate(rows + [jnp.zeros((5, LANES), F32)], axis=0)

    *outs, nrm = _row_call(body, "in_proj", x.shape[0], tm, [x], [g_mix, w_all],
                           [(D_MODEL, BF16), (N_MAIN, BF16), (N_FPAD + N_GATE, F32), (N_FPAD, F32)], [],
                           tile_outs=[LANES])
    return (*outs, nrm)


def _mix_ffn_fwd(attn_a, attn_b, zfg, x, w_sa, w_fo, w_mo, g_mlp, w1s, w2s, tm):
    ch = D_FF // N_DEV

    def body(i, ins, consts, outs, accs):
        aa_ref, ab_ref, zfg_ref, x_ref = ins
        wsa_ref, wfo_ref, wmo_ref, g_ref, w1_ref, w2_ref = consts
        ya_ref, yb_ref, mx_ref, h1_ref, u2_ref, a_ref, r_ref, h2_ref = outs
        ya = _dot(aa_ref[...], _lane_concat(wsa_ref))
        yb = _dot(ab_ref[...], _lane_concat(wfo_ref))
        g0 = _sigmoid(zfg_ref[:, N_FPAD:N_FPAD + D_MODEL])
        g1 = _sigmoid(zfg_ref[:, N_FPAD + D_MODEL:N_FPAD + 2 * D_MODEL])
        mixed = (g0 * ya + g1 * yb).astype(BF16)
        ya_ref[...] = ya.astype(BF16)
        yb_ref[...] = yb.astype(BF16)
        mx_ref[...] = mixed
        h1 = x_ref[...] + _dot(mixed, wmo_ref[...])
        h1_ref[...] = h1
        u = ((h1 * _rms(h1)) * g_ref[...]).astype(BF16)
        u2_ref[...] = u
        acc = h1
        for c in range(N_DEV):
            a = _dot(u, w1_ref[c])
            a_ref[:, c * ch:(c + 1) * ch] = a.astype(BF16)
            r = jnp.square(jnp.maximum(a, 0.0)).astype(BF16)
            r_ref[:, c * ch:(c + 1) * ch] = r
            acc = acc + _dot(r, w2_ref[c])
        h2_ref[...] = acc

    return _row_call(body, "mix_ffn_fwd", x.shape[0], tm, [attn_a, attn_b, zfg, x],
                     [w_sa, w_fo, w_mo, g_mlp, w1s, w2s],
                     [(D_MODEL, BF16), (D_MODEL, BF16), (D_MODEL, BF16), (D_MODEL, F32), (D_MODEL, BF16),
                      (D_FF, BF16), (D_FF, BF16), (D_MODEL, F32)], [])


def _head_ffn_bwd(h2, p, tgt, a, g_ple, w_pg, w_pp, g_fin, w2s, tm):
    ch = D_FF // N_DEV

    def body(i, ins, consts, outs, accs):
        h2_ref, p_ref, t_ref, a_ref = ins
        gp_ref, wpg_ref, wpp_ref, gf_ref, w2_ref = consts
        dlg_ref, dpp_ref, u3_ref, dh2_ref, dh2b_ref, da_ref = outs
        loss_ref, dgf_ref, dgp_ref = accs
        h2 = h2_ref[...]
        gp = gp_ref[...]
        u3 = ((h2 * _rms(h2)) * gp).astype(BF16)
        u3_ref[...] = u3
        pg = _sigmoid(_dot(u3, wpg_ref[...]))
        pp = _dot(p_ref[...].astype(BF16), _lane_concat(wpp_ref))
        h3 = h2 + pg * pp
        rs3 = _rms(h3)
        n3 = h3 * rs3
        gf = gf_ref[...]
        err = n3 * gf - t_ref[...]
        row_loss = 0.5 * jnp.mean(err * err, axis=-1, keepdims=True)
        _acc_rows(loss_ref, i, jnp.broadcast_to(jnp.sum(row_loss, axis=0, keepdims=True), (1, LANES)))
        dy = err * (1.0 / D_MODEL)
        _acc_rows(dgf_ref, i, jnp.sum(dy * n3, axis=0, keepdims=True))
        dn = dy * gf
        dh3 = rs3 * (dn - n3 * jnp.mean(dn * n3, axis=-1, keepdims=True))
        dpp_ref[...] = (dh3 * pg).astype(BF16)
        dlg = ((dh3 * pp) * pg * (1.0 - pg)).astype(BF16)
        dlg_ref[...] = dlg
        dh, dg = _rms_bwd(h2, gp, _dot_nt(dlg, wpg_ref[...]))
        _acc_rows(dgp_ref, i, dg)
        dh2 = dh3 + dh
        dh2_ref[...] = dh2
        dh2b = dh2.astype(BF16)
        dh2b_ref[...] = dh2b
        for c in range(N_DEV):
            dr = _dot_nt(dh2b, w2_ref[c])
            av = a_ref[:, c * ch:(c + 1) * ch].astype(F32)
            da_ref[:, c * ch:(c + 1) * ch] = (dr * (2.0 * jnp.maximum(av, 0.0))).astype(BF16)

    return _row_call(body, "head_ffn_bwd", h2.shape[0], tm, [h2, p, tgt, a], [g_ple, w_pg, w_pp, g_fin, w2s],
                     [(D_MODEL, BF16), (D_MODEL, BF16), (D_MODEL, BF16), (D_MODEL, F32), (D_MODEL, BF16),
                      (D_FF, BF16)], [LANES, D_MODEL, D_MODEL])


def _ffn_bwd_b(da, dh2, h1, ya, yb, zfg, attn_b, w1s, g_mlp, w_mo, w_sa, w_fo, tm):
    ch = D_FF // N_DEV

    def body(i, ins, consts, outs, accs):
        da_ref, dh2_ref, h1_ref, ya_ref, yb_ref, zfg_ref, ob_ref = ins
        w1_ref, gm_ref, wmo_ref, wsa_ref, wfo_ref = consts
        dh1_ref, dh1b_ref, dgl_ref, dya_ref, dyb_ref, daa_ref, dab_ref, dl_ref = outs
        dgm_ref, = accs
        du2 = _dot_nt(da_ref[:, 0:ch], w1_ref[0])
        for c in range(1, N_DEV):
            du2 = du2 + _dot_nt(da_ref[:, c * ch:(c + 1) * ch], w1_ref[c])
        dh, dg = _rms_bwd(h1_ref[...], gm_ref[...], du2)
        _acc_rows(dgm_ref, i, dg)
        dh1 = dh2_ref[...] + dh
        dh1_ref[...] = dh1
        dh1b = dh1.astype(BF16)
        dh1b_ref[...] = dh1b
        dmx = _dot_nt(dh1b, wmo_ref[...])
        g0 = _sigmoid(zfg_ref[:, N_FPAD:N_FPAD + D_MODEL])
        g1 = _sigmoid(zfg_ref[:, N_FPAD + D_MODEL:N_FPAD + 2 * D_MODEL])
        dya = (dmx * g0).astype(BF16)
        dyb = (dmx * g1).astype(BF16)
        dya_ref[...] = dya
        dyb_ref[...] = dyb
        dgl_ref[:, 0:D_MODEL] = ((dmx * ya_ref[...].astype(F32)) * g0 * (1.0 - g0)).astype(BF16)
        dgl_ref[:, D_MODEL:2 * D_MODEL] = ((dmx * yb_ref[...].astype(F32)) * g1 * (1.0 - g1)).astype(BF16)
        daa_ref[...] = _dot_nt(dya, _lane_concat(wsa_ref)).astype(BF16)
        dab = _dot_nt(dyb, _lane_concat(wfo_ref)).astype(BF16)
        dab_ref[...] = dab
        half_in = lax.broadcasted_iota(jnp.int32, (LANES, 2 * LANES), 0) // HEAD_DIM
        half_out = lax.broadcasted_iota(jnp.int32, (LANES, 2 * LANES), 1) // LANES
        pick = (half_in == half_out).astype(BF16)
        for pair in range(FOX_HEADS // 2):
            cols = slice(pair * LANES, (pair + 1) * LANES)
            prod = dab[:, cols].astype(F32) * ob_ref[:, cols].astype(F32)
            hi = prod.astype(BF16)
            lo_part = (prod - hi.astype(F32)).astype(BF16)
            dl_ref[:, 2 * pair * LANES:(2 * pair + 2) * LANES] = _dot(hi, pick) + _dot(lo_part, pick)

    half = D_MODEL // 2
    return _row_call(body, "ffn_bwd_b", h1.shape[0], tm, [da, dh2, h1, ya, yb, zfg, attn_b],
                     [w1s, g_mlp, w_mo, w_sa, w_fo],
                     [(D_MODEL, F32), (D_MODEL, BF16), (N_GATE, BF16), (D_MODEL, BF16), (D_MODEL, BF16),
                      (half, BF16), (half, BF16), (FOX_HEADS * LANES, F32)], [D_MODEL])


def _in_proj_bwd(dz, dh1, x, w_all, g_mix, tm, ride=None):
    def body(i, ins, consts, outs, accs):
        dz_ref, dh1_ref, x_ref = ins
        w_ref, g_ref = consts
        dx_ref, = outs
        dgx_ref, = accs
        du1 = _dot_nt(dz_ref[...], w_ref[...])
        dh, dg = _rms_bwd(x_ref[...], g_ref[...], du1)
        _acc_rows(dgx_ref, i, dg)
        dx_ref[...] = dh1_ref[...] + dh

    return _row_call(body, "in_proj_bwd", x.shape[0], tm, [dz, dh1, x], [w_all, g_mix],
                     [(D_MODEL, F32)], [D_MODEL], ride)


def _matmul_tn(a, b, name, ts, stack_cols=0):
    n_rows, ka = a.shape
    n = b.shape[1]
    tk = min(ka, 1024)
    tn = 896 if n % 1024 else 1024
    n_stack = tn // stack_cols if stack_cols else 0
    assert ka % tk == 0 and n % tn == 0 and n_rows % ts == 0 and (not stack_cols or tk == ka)
    n_steps = n_rows // ts

    def kern(a_ref, b_ref, o_ref, acc_ref):
        s = pl.program_id(2)

        @pl.when(s == 0)
        def _():
            acc_ref[...] = jnp.zeros_like(acc_ref)
        acc_ref[...] += _dot_tn(a_ref[...].astype(BF16), b_ref[...])

        @pl.when(s == n_steps - 1)
        def _():
            if stack_cols:
                for c in range(n_stack):
                    o_ref[c] = acc_ref[:, c * stack_cols:(c + 1) * stack_cols].astype(BF16)
            else:
                o_ref[...] = acc_ref[...].astype(BF16)

    if stack_cols:
        out_spec = pl.BlockSpec((n_stack, tk, stack_cols), lambda i, j, s: (j, 0, 0))
        out_shape = jax.ShapeDtypeStruct((n // stack_cols, ka, stack_cols), BF16)
    else:
        out_spec = pl.BlockSpec((tk, tn), lambda i, j, s: (i, j))
        out_shape = jax.ShapeDtypeStruct((ka, n), BF16)
    return pl.pallas_call(
        kern, grid=(ka // tk, n // tn, n_steps),
        in_specs=[pl.BlockSpec((ts, tk), lambda i, j, s: (s, i)), pl.BlockSpec((ts, tn), lambda i, j, s: (s, j))],
        out_specs=out_spec, out_shape=out_shape, scratch_shapes=[pltpu.VMEM((tk, tn), F32)], name=name,
        compiler_params=_params(3))(a, b)


SCAN_CHUNK = 512
BWD_SCAN_CHUNK = 1024


def _decay_cumsum(f_t, b_col):
    n_tok = f_t.shape[1]
    ch = min(SCAN_CHUNK, n_tok)

    def kern(f_ref, b_ref, c_ref):
        r = lax.broadcasted_iota(jnp.int32, (ch, ch), 0)
        c = lax.broadcasted_iota(jnp.int32, (ch, ch), 1)
        tri = (r <= c).astype(F32)
        carry = jnp.zeros((8, 1), F32)
        for k in range(n_tok // ch):
            xv = f_ref[:, k * ch:(k + 1) * ch] + b_ref[...]
            lf = jnp.minimum(xv, 0.0) - jnp.log(1.0 + jnp.exp(-jnp.abs(xv)))
            cs = jnp.dot(lf, tri, precision=lax.Precision.HIGHEST, preferred_element_type=F32) + carry
            c_ref[:, k * ch:(k + 1) * ch] = cs
            carry = cs[:, ch - 1:ch]

    return pl.pallas_call(kern, out_shape=jax.ShapeDtypeStruct((8, n_tok), F32), name="decay_cumsum",
                          compiler_params=_params(0))(f_t, b_col)


def _decay_bwd(cs, rs, f_t, b_col):
    n_tok = f_t.shape[1]
    ch = min(BWD_SCAN_CHUNK, n_tok)
    n_ch = n_tok // ch

    def kern(cs_ref, rs_ref, f_ref, b_ref, df_ref, db_ref, carry_ref):
        k = pl.program_id(0)

        @pl.when(k == 0)
        def _():
            carry_ref[...] = jnp.zeros_like(carry_ref)
            db_ref[...] = jnp.zeros_like(db_ref)

        r = lax.broadcasted_iota(jnp.int32, (ch, ch), 0)
        c = lax.broadcasted_iota(jnp.int32, (ch, ch), 1)
        tri = (r >= c).astype(F32)
        head = lax.broadcasted_iota(jnp.int32, (8, 4 * LANES), 0)
        lane = lax.broadcasted_iota(jnp.int32, (8, 4 * LANES), 1)
        pick = (lane == HEAD_DIM * head).astype(F32)
        dc = lax.dot_general(pick, rs_ref[...] - cs_ref[...], _NT, precision=lax.Precision.HIGHEST,
                             preferred_element_type=F32)
        rc = jnp.dot(dc, tri, precision=lax.Precision.HIGHEST, preferred_element_type=F32) + carry_ref[:, 0:1]
        carry_ref[...] = jnp.broadcast_to(rc[:, 0:1], carry_ref.shape)
        df = rc / (1.0 + jnp.exp(f_ref[...] + b_ref[...]))
        df_ref[...] = df
        db_ref[...] += jnp.broadcast_to(jnp.sum(df, axis=1, keepdims=True), db_ref.shape)

    back = lambda k: n_ch - 1 - k
    wide = pl.BlockSpec((ch, 4 * LANES), lambda k: (back(k), 0))
    row = pl.BlockSpec((8, ch), lambda k: (0, back(k)))
    return pl.pallas_call(
        kern, grid=(n_ch,),
        in_specs=[wide, wide, row, pl.BlockSpec((8, 1), lambda k: (0, 0))],
        out_specs=[row, pl.BlockSpec((8, LANES), lambda k: (0, 0))],
        out_shape=[jax.ShapeDtypeStruct((8, n_tok), F32), jax.ShapeDtypeStruct((8, LANES), F32)],
        scratch_shapes=[pltpu.VMEM((8, LANES), F32)], name="decay_bwd", compiler_params=_params(1))(cs, rs, f_t, b_col)


def _swa_bias_table():
    row = jnp.arange(SWA_BLOCK)[:, None] + SWA_BLOCK
    col = jnp.arange(2 * SWA_BLOCK)[None, :]
    cd = (row >> CHUNK_SHIFT) - (col >> CHUNK_SHIFT)
    band = (cd >= 0) & (cd <= WINDOW_CHUNKS)
    slopes = jnp.asarray([2.0 ** -(h + 1) for h in range(SWA_HEADS)], F32)
    bias = -slopes[:, None, None] * jnp.abs(row - col).astype(F32)[None]
    return jnp.stack([jnp.where(band & (col >= SWA_BLOCK), bias, NEG), jnp.where(band, bias, NEG)])


SWA_PER_STEP = 4


def _swap_halves(t):
    return pltpu.roll(t.astype(F32), HEAD_DIM, axis=1).astype(t.dtype)


def _swa_specs():
    blk, rows = SWA_BLOCK, SWA_PER_STEP * SWA_BLOCK
    q = pl.BlockSpec((rows, 4 * LANES), lambda n: (n, 0))
    before = lambda col: pl.BlockSpec((blk, LANES), lambda n: (jnp.maximum(SWA_PER_STEP * n - 1, 0), col))
    own = lambda col: pl.BlockSpec((rows, LANES), lambda n: (n, col))
    bias = pl.BlockSpec((2, SWA_HEADS, blk, 2 * blk), lambda n: (0, 0, 0, 0))
    return [q, before(4), own(4), before(5), own(5), bias]


def _swa_band(before_ref, own_ref):
    both = jnp.concatenate([before_ref[...], own_ref[...]], axis=0)
    return both, _swap_halves(both)


def _swa_bias(bias_ref, n, b):
    return bias_ref.at[jnp.minimum(n, 1)] if b == 0 else bias_ref.at[1]


SWA_GROUPS = ([h for h in range(SWA_HEADS) if h % 2 == h // 4], [h for h in range(SWA_HEADS) if h % 2 != h // 4])


def _stack_heads(ref, rows, heads, lo, mask_halves):
    tiles = []
    for h in heads:
        t = ref[rows, (h // 2) * LANES:(h // 2 + 1) * LANES]
        tiles.append(jnp.where(lo if h % 2 == 0 else ~lo, t, jnp.zeros_like(t)) if mask_halves else t)
    return jnp.concatenate(tiles, axis=0)


def _per_head_column(values, heads):
    return jnp.concatenate([jnp.full((SWA_BLOCK, 1), values(h), F32) for h in heads], axis=0)


def _swa_scores(q_ref, rows, kx, heads, lo, bias):
    qa = _stack_heads(q_ref, rows, heads, lo, True) * SCALE
    return qa, _dot_nt(qa, kx) + jnp.concatenate([bias[h] for h in heads], axis=0)


def _swa_fwd(zm, sinks):
    n_tok = zm.shape[0]
    blk, step_rows = SWA_BLOCK, SWA_PER_STEP * SWA_BLOCK

    def kern(q_ref, kp_ref, kc_ref, vp_ref, vc_ref, bias_ref, sink_ref, o_ref, lse_ref):
        n = pl.program_id(0)
        (k_all, k_all_sw), (v_all, v_all_sw) = _swa_band(kp_ref, kc_ref), _swa_band(vp_ref, vc_ref)
        lane = lax.broadcasted_iota(jnp.int32, (blk, LANES), 1)
        lo = lane < HEAD_DIM
        for b in range(SWA_PER_STEP):
            rows, band = slice(b * blk, (b + 1) * blk), slice(b * blk, (b + 2) * blk)
            bias = _swa_bias(bias_ref, n, b)
            lse_t = jnp.zeros((blk, LANES), F32)
            for pair in range(SWA_HEADS // 2):
                q2 = q_ref[rows, pair * LANES:(pair + 1) * LANES]
                outs = []
                for a in range(2):
                    h = 2 * pair + a
                    qa = jnp.where(lo if a == 0 else ~lo, q2, jnp.zeros_like(q2)) * SCALE
                    kx, vx = (k_all[band], v_all[band]) if h in SWA_GROUPS[0] else (k_all_sw[band], v_all_sw[band])
                    s = _dot_nt(qa, kx) + bias[h]
                    sink = sink_ref[h]
                    m = jnp.maximum(jnp.max(s, axis=-1, keepdims=True), sink)
                    e = jnp.exp(s - m)
                    l = jnp.sum(e, axis=-1, keepdims=True) + jnp.exp(sink - m)
                    pn = (e * (1.0 / l)).astype(BF16)
                    outs.append(_dot(pn, vx))
                    lse_t = jnp.where(lane == h, m + jnp.log(l), lse_t)
                o_ref[rows, pair * LANES:(pair + 1) * LANES] = jnp.where(lo, outs[0], outs[1]).astype(BF16)
            lse_ref[rows, :] = lse_t

    return pl.pallas_call(
        kern, grid=(n_tok // step_rows,),
        in_specs=_swa_specs() + [pl.BlockSpec(memory_space=pltpu.SMEM)],
        out_specs=[pl.BlockSpec((step_rows, 4 * LANES), lambda n: (n, 0)),
                   pl.BlockSpec((step_rows, LANES), lambda n: (n, 0))],
        out_shape=[jax.ShapeDtypeStruct((n_tok, 4 * LANES), BF16), jax.ShapeDtypeStruct((n_tok, LANES), F32)],
        name="swa_fwd", compiler_params=_params(1))(zm, zm, zm, zm, zm, _swa_bias_table(), sinks)


def _swa_bwd(zm, sinks, d_out, out, lse):
    n_tok = zm.shape[0]
    blk, step_rows = SWA_BLOCK, SWA_PER_STEP * SWA_BLOCK

    def kern(q_ref, kp_ref, kc_ref, vp_ref, vc_ref, bias_ref, do_ref, o_ref, lse_ref, sink_ref,
             dq_ref, dkp_ref, dkc_ref, dvp_ref, dvc_ref, dsk_ref):
        n = pl.program_id(0)

        @pl.when(n == 0)
        def _():
            dsk_ref[...] = jnp.zeros_like(dsk_ref)

        bands = (_swa_band(kp_ref, kc_ref), _swa_band(vp_ref, vc_ref))
        lane = lax.broadcasted_iota(jnp.int32, (blk, LANES), 1)
        lo = lane < HEAD_DIM
        for b in range(SWA_PER_STEP):
            rows, band = slice(b * blk, (b + 1) * blk), slice(b * blk, (b + 2) * blk)
            bias = _swa_bias(bias_ref, n, b)
            lse_t = lse_ref[rows, :]
            dqs, dkv = {}, []
            for g, heads in enumerate(SWA_GROUPS):
                kx, vx = bands[0][g][band], bands[1][g][band]
                qa, s = _swa_scores(q_ref, rows, kx, heads, lo, bias)
                doa = _stack_heads(do_ref, rows, heads, lo, True)
                lse_g = jnp.concatenate([lse_t[:, h:h + 1] for h in heads], axis=0)
                prob = jnp.exp(s - lse_g)
                o_g = _stack_heads(o_ref, rows, heads, lo, False)
                dd = jnp.sum(doa.astype(F32) * o_g.astype(F32), axis=-1, keepdims=True)
                ds = (prob * (_dot_nt(doa, vx) - dd)).astype(BF16)
                sink_part = -jnp.exp(_per_head_column(lambda h: sink_ref[h], heads) - lse_g) * dd
                dq = _dot(ds, kx) * SCALE
                for r, h in enumerate(heads):
                    dqs[h] = dq[r * blk:(r + 1) * blk]
                    dsk_ref[h:h + 1, :] += jnp.broadcast_to(
                        jnp.sum(sink_part[r * blk:(r + 1) * blk], axis=0, keepdims=True), (1, LANES))
                dkv.append((_dot_tn(ds, qa), _dot_tn(prob.astype(BF16), doa)))
            for pair in range(SWA_HEADS // 2):
                dq_ref[rows, pair * LANES:(pair + 1) * LANES] = jnp.where(
                    lo, dqs[2 * pair], dqs[2 * pair + 1]).astype(BF16)
            dk = dkv[0][0] + pltpu.roll(dkv[1][0], HEAD_DIM, axis=1)
            dv = dkv[0][1] + pltpu.roll(dkv[1][1], HEAD_DIM, axis=1)
            dkp_ref[rows, :] = dk[0:blk]
            dkc_ref[rows, :] = dk[blk:2 * blk]
            dvp_ref[rows, :] = dv[0:blk]
            dvc_ref[rows, :] = dv[blk:2 * blk]

    wide = pl.BlockSpec((step_rows, 4 * LANES), lambda n: (n, 0))
    narrow = pl.BlockSpec((step_rows, LANES), lambda n: (n, 0))
    part = jax.ShapeDtypeStruct((n_tok, LANES), F32)
    return pl.pallas_call(
        kern, grid=(n_tok // step_rows,),
        in_specs=_swa_specs() + [wide, wide, narrow, pl.BlockSpec(memory_space=pltpu.SMEM)],
        out_specs=[wide, narrow, narrow, narrow, narrow, pl.BlockSpec((8, LANES), lambda n: (0, 0))],
        out_shape=[jax.ShapeDtypeStruct((n_tok, 4 * LANES), BF16), part, part, part, part,
                   jax.ShapeDtypeStruct((8, LANES), F32)],
        name="swa_bwd", compiler_params=_params(1))(zm, zm, zm, zm, zm, _swa_bias_table(), d_out, out, lse, sinks)


def _my_pos():
    return lax.axis_index("x"), lax.axis_index("y"), lax.axis_index("c")


def _peer(k):
    x, y, c = _my_pos()
    px, py, pc = x ^ (k >> 2), y ^ ((k >> 1) & 1), c ^ (k & 1)
    return (px, py, pc), 4 * px + 2 * py + pc


def _gather_copies(x_refs, out_refs, send_sems, recv_sems, local_sems):
    x, y, c = _my_pos()
    my_id = 4 * x + 2 * y + c
    local = [pltpu.make_async_copy(x_refs[w], out_refs[w].at[my_id], local_sems.at[w]) for w in range(len(x_refs))]
    sends, arrivals = [], []
    for k in range(1, N_DEV):
        peer, peer_id = _peer(k)
        for w in range(len(x_refs)):
            sems = dict(send_sem=send_sems.at[7 * w + k - 1], recv_sem=recv_sems.at[7 * w + k - 1],
                        device_id=peer, device_id_type=MESH)
            sends.append(pltpu.make_async_remote_copy(src_ref=x_refs[w], dst_ref=out_refs[w].at[my_id], **sems))
            arrivals.append(pltpu.make_async_remote_copy(src_ref=x_refs[w], dst_ref=out_refs[w].at[peer_id], **sems))
    return local, sends, arrivals


def _scatter_copies(g_refs, part_refs, send_sems, recv_sems, local_sems):
    x, y, c = _my_pos()
    my_id = 4 * x + 2 * y + c
    local = [pltpu.make_async_copy(g_refs[w].at[my_id], part_refs[w].at[0], local_sems.at[w])
             for w in range(len(g_refs))]
    sends, arrivals = [], []
    for k in range(1, N_DEV):
        peer, peer_id = _peer(k)
        for w in range(len(g_refs)):
            sems = dict(send_sem=send_sems.at[7 * w + k - 1], recv_sem=recv_sems.at[7 * w + k - 1],
                        device_id=peer, device_id_type=MESH)
            sends.append(pltpu.make_async_remote_copy(src_ref=g_refs[w].at[peer_id], dst_ref=part_refs[w].at[k], **sems))
            arrivals.append(pltpu.make_async_remote_copy(src_ref=g_refs[w].at[my_id], dst_ref=part_refs[w].at[k], **sems))
    return local, sends, arrivals


def _start_copies(local, sends, arrivals):
    for cp in local + sends:
        cp.start()


def _finish_copies(local, sends, arrivals):
    for cp in arrivals:
        cp.wait_recv()
    for cp in sends:
        cp.wait_send()
    for cp in local:
        cp.wait()


def _exchange_scratch(n_arrays):
    return [pltpu.SemaphoreType.DMA((7 * n_arrays,)), pltpu.SemaphoreType.DMA((7 * n_arrays,)),
            pltpu.SemaphoreType.DMA((n_arrays,))]


class _Ride:
    def __init__(self, arrays, out_shape, copies):
        self.arrays, self.out_shape, self.copies = list(arrays), list(out_shape), copies
        any_spec = pl.BlockSpec(memory_space=pl.ANY)
        self.in_specs = [any_spec] * len(self.arrays)
        self.out_specs = [any_spec] * len(self.arrays)
        self.scratch = _exchange_scratch(len(self.arrays)) if self.arrays else []

    @staticmethod
    def _at(grid, last):
        hit = [pl.program_id(d) == (n - 1 if last else 0) for d, n in enumerate(grid)]
        return hit[0] if len(hit) == 1 else jnp.logical_and(*hit)

    def at_first_step(self, grid, in_refs, out_refs, sems):
        @pl.when(self._at(grid, False))
        def _():
            _start_copies(*self.copies(in_refs, out_refs, *sems))

    def at_last_step(self, grid, in_refs, out_refs, sems):
        @pl.when(self._at(grid, True))
        def _():
            _finish_copies(*self.copies(in_refs, out_refs, *sems))


_NO_RIDE = _Ride([], [], None)


def _gather_ride(shards):
    return _Ride(shards, [jax.ShapeDtypeStruct((N_DEV,) + s.shape, s.dtype) for s in shards], _gather_copies)


def _scatter_ride(grads):
    return _Ride(grads, [jax.ShapeDtypeStruct(g.shape, g.dtype) for g in grads], _scatter_copies)


Q_COL, K_COL, V_COL = 6, 10, 14


def _lane_tile(stat, width):
    return jnp.tile(stat, (1, width // LANES))


def _fox_steps(nq):
    steps = [(i2, j2, int(j2 == i2)) for i2 in range(nq // 2) for j2 in range(i2 + 1)]
    return [np.asarray(col, np.int32) for col in zip(*steps)]


_BELOW = [(0, 0, 2, None), (1, 0, 2, None)]
_BELOW_NEAR = [(0, 1, 2, None), (1, 1, 2, None)]
_DIAGONAL = [(0, 0, 1, 0), (1, 0, 2, 1)]


def _causal(t, first_row):
    row = lax.broadcasted_iota(jnp.int32, t.shape, 0) + first_row
    col = lax.broadcasted_iota(jnp.int32, t.shape, 1)
    return jnp.where(col <= row, t, NEG)


def _fox_dispatch(sweep, last, dead_ref, head0, idx):
    below = last == 0
    takes_all, takes_near = [], []
    for h in (head0, head0 + 1):
        far_live, near_live = dead_ref[h, idx] < 0.5, dead_ref[h, idx + 1] < 0.5
        takes_all.append(far_live)
        takes_near.append(jnp.logical_and(jnp.logical_not(far_live), near_live))
    joint = jnp.logical_and(takes_all[0], takes_all[1])
    pl.when(jnp.logical_and(below, joint))(lambda: sweep(_BELOW, (0, 1)))
    apart = jnp.logical_and(below, jnp.logical_not(joint))
    for a in range(2):
        pl.when(jnp.logical_and(apart, takes_all[a]))(lambda a=a: sweep(_BELOW, (a,)))
        pl.when(jnp.logical_and(below, takes_near[a]))(lambda a=a: sweep(_BELOW_NEAR, (a,)))
    pl.when(last == 1)(lambda: sweep(_DIAGONAL, (0, 1)))


EXP_ZERO = 104.5
NORM_SLACK = 1.005


def _fox_dead_steps(nrm, c_pairs, tq):
    nq = nrm.shape[0] // 8
    stats = nrm.reshape(nq, 8, LANES)[:, :3, :FOX_HEADS]
    qn, kn, own = jnp.sqrt(stats[:, 0]) * SCALE, jnp.sqrt(stats[:, 1]), stats[:, 2] * SCALE
    cb = c_pairs.reshape(FOX_HEADS, nq, tq)
    c_max, c_min = jnp.max(cb, axis=-1).T, jnp.min(cb, axis=-1).T
    both = lambda t, pick: pick(t.reshape(nq // 2, 2, FOX_HEADS), axis=1)
    qn2, kn2, c_max2, own2 = both(qn, jnp.max), both(kn, jnp.max), both(c_max, jnp.max), both(own, jnp.min)
    row_max_floor = own2 - (NORM_SLACK - 1.0) * qn2 * kn2 - c_max2
    gap = qn2[:, None] * kn[None] * NORM_SLACK - c_min[None] - row_max_floor[:, None]
    below = jnp.arange(nq)[None, :] < 2 * jnp.arange(nq // 2)[:, None]
    dead = jnp.logical_and(gap < -EXP_ZERO, below[..., None])
    return dead.transpose(2, 0, 1).reshape(FOX_HEADS, -1).astype(F32)


def _fox_fwd(zm, c_pairs, dead, tq, ride=None):
    n_tok = zm.shape[0]
    nq = n_tok // tq
    ii, jj, kk = _fox_steps(nq)
    n_steps = len(ii)
    n_ride = len(ride.arrays) if ride else 0

    def kern(ii_ref, jj_ref, kk_ref, q_ref, k_ref, v_ref, ck_ref, dead_ref, *more):
        ride_in, (o_ref, ln_ref), ride_out = more[:n_ride], more[n_ride:n_ride + 2], more[n_ride + 2:2 * n_ride + 2]
        qs_ref, m_ref, l_ref, acc_ref = more[2 * n_ride + 2:2 * n_ride + 6]
        step = pl.program_id(1)
        j, last = jj_ref[step], kk_ref[step]
        lo = lax.broadcasted_iota(jnp.int32, (2 * tq, LANES), 1) < HEAD_DIM
        if ride:
            ride.at_first_step((FOX_HEADS // 2, n_steps), ride_in, ride_out, more[2 * n_ride + 6:])

        @pl.when(j == 0)
        def _():
            q2 = q_ref[...]
            zq = jnp.zeros_like(q2)
            qs_ref[0] = jnp.where(lo, q2, zq) * SCALE
            qs_ref[1] = jnp.where(lo, zq, q2) * SCALE
            m_ref[...] = jnp.full(m_ref.shape, NEG, F32)
            l_ref[...] = jnp.zeros(l_ref.shape, F32)
            acc_ref[...] = jnp.zeros(acc_ref.shape, F32)

        def sweep(tiles, heads):
            v_ones = jnp.concatenate([v_ref[...], jnp.ones((2 * tq, LANES), BF16)], axis=1)
            for sub, k0, k1, diagonal in tiles:
                rows, keys = slice(sub * tq, (sub + 1) * tq), slice(k0 * tq, k1 * tq)
                for a in heads:
                    t = _dot_nt(qs_ref[a, rows], k_ref[keys, :]) - ck_ref[a:a + 1, keys]
                    if diagonal is not None:
                        t = _causal(t, diagonal * tq)
                    m_old = m_ref[a, rows]
                    m_new = jnp.maximum(m_old, jnp.max(t, axis=-1, keepdims=True))
                    alpha = jnp.exp(m_old - m_new)
                    e = jnp.exp(t - _lane_tile(m_new, (k1 - k0) * tq)).astype(BF16)
                    pv = _dot(e, v_ones[keys])
                    acc_ref[a, rows] = alpha * acc_ref[a, rows] + pv[:, :LANES]
                    l_ref[a, rows] = alpha * l_ref[a, rows] + pv[:, LANES:]
                    m_ref[a, rows] = m_new

        _fox_dispatch(sweep, last, dead_ref, 2 * pl.program_id(0), ii_ref[step] * nq + 2 * j)

        @pl.when(last == 1)
        def _():
            o_ref[...] = jnp.where(lo, acc_ref[0] / l_ref[0], acc_ref[1] / l_ref[1]).astype(BF16)
            ln_ref[:, :LANES] = m_ref[0] + jnp.log(l_ref[0])
            ln_ref[:, LANES:] = m_ref[1] + jnp.log(l_ref[1])

        if ride:
            ride.at_last_step((FOX_HEADS // 2, n_steps), ride_in, ride_out, more[2 * n_ride + 6:])

    blk = (2 * tq, LANES)
    by_i = lambda col: (lambda hp, s, ii, jj, kk: (ii[s], col + hp))
    by_j = lambda col: (lambda hp, s, ii, jj, kk: (jj[s], col + hp))
    extra = ride if ride else _NO_RIDE
    grid_spec = pltpu.PrefetchScalarGridSpec(
        num_scalar_prefetch=3, grid=(FOX_HEADS // 2, n_steps),
        in_specs=[pl.BlockSpec(blk, by_i(Q_COL)), pl.BlockSpec(blk, by_j(K_COL)), pl.BlockSpec(blk, by_j(V_COL)),
                  pl.BlockSpec((None, 2, 2 * tq), lambda hp, s, ii, jj, kk: (hp, 0, jj[s])),
                  pl.BlockSpec(memory_space=pltpu.SMEM)] + extra.in_specs,
        out_specs=[pl.BlockSpec(blk, by_i(0)), pl.BlockSpec((2 * tq, 2 * LANES), by_i(0))] + extra.out_specs,
        scratch_shapes=[pltpu.VMEM((2, 2 * tq, LANES), BF16), pltpu.VMEM((2, 2 * tq, LANES), F32),
                        pltpu.VMEM((2, 2 * tq, LANES), F32), pltpu.VMEM((2, 2 * tq, LANES), F32)] + extra.scratch)
    return pl.pallas_call(
        kern, grid_spec=grid_spec,
        out_shape=[jax.ShapeDtypeStruct((n_tok, 4 * LANES), BF16),
                   jax.ShapeDtypeStruct((n_tok, FOX_HEADS * LANES), F32)] + extra.out_shape,
        name="fox_fwd", compiler_params=_params(2))(ii, jj, kk, zm, zm, zm, c_pairs, dead, *extra.arrays)


def _fox_bwd(zm, c_pairs, dead, d_out, lnorm, delta, tq, ride=None):
    n_tok = zm.shape[0]
    nq = n_tok // tq
    ii, jj, kk = _fox_steps(nq)
    n_steps = len(ii)
    n_ride = len(ride.arrays) if ride else 0

    def kern(ii_ref, jj_ref, kk_ref, q_ref, k_ref, v_ref, ck_ref, dead_ref, do_ref, ln_ref, dl_ref, *more):
        ride_in, ride_out = more[:n_ride], more[n_ride + 5:2 * n_ride + 5]
        dq_ref, dk_out, dv_out, cs_ref, rs_ref = more[n_ride:n_ride + 5]
        qs_ref, qo_ref, dos_ref, dq_acc, dk_ref, dv_ref = more[2 * n_ride + 5:2 * n_ride + 11]
        step = pl.program_id(1)
        j, last = jj_ref[step], kk_ref[step]
        lo = lax.broadcasted_iota(jnp.int32, (2 * tq, LANES), 1) < HEAD_DIM
        if ride:
            ride.at_first_step((FOX_HEADS // 2, n_steps), ride_in, ride_out, more[2 * n_ride + 11:])

        @pl.when(step == 0)
        def _():
            dk_ref[...] = jnp.zeros_like(dk_ref)
            dv_ref[...] = jnp.zeros_like(dv_ref)
            cs_ref[...] = jnp.zeros_like(cs_ref)

        @pl.when(j == 0)
        def _():
            q2, do2 = q_ref[...], do_ref[...]
            zq = jnp.zeros_like(q2)
            ones = jnp.ones((2 * tq, LANES), BF16)
            for a in range(2):
                half = lo if a == 0 else ~lo
                qa = jnp.where(half, q2, zq) * SCALE
                qs_ref[a] = qa
                qo_ref[a] = jnp.concatenate([qa, ones], axis=1)
                dos_ref[a] = jnp.where(half, do2, zq)
            dq_acc[...] = jnp.zeros(dq_acc.shape, F32)

        def sweep(tiles, heads):
            k_ones = jnp.concatenate([k_ref[...], jnp.ones((2 * tq, LANES), BF16)], axis=1)
            sums = {}
            for sub, k0, k1, diagonal in tiles:
                rows, keys, n_keys = slice(sub * tq, (sub + 1) * tq), slice(k0 * tq, k1 * tq), (k1 - k0) * tq
                part = sums.setdefault((k0, k1), [0.0, 0.0, 0.0, 0.0])
                for a in heads:
                    t = _dot_nt(qs_ref[a, rows], k_ref[keys, :]) - ck_ref[a:a + 1, keys]
                    if diagonal is not None:
                        t = _causal(t, diagonal * tq)
                    prob = jnp.exp(t - _lane_tile(ln_ref[rows, a * LANES:(a + 1) * LANES], n_keys))
                    dp = _dot_nt(dos_ref[a, rows], v_ref[keys, :])
                    ds = (prob * (dp - _lane_tile(dl_ref[rows, a * LANES:(a + 1) * LANES], n_keys))).astype(BF16)
                    dq_acc[a, rows] += _dot(ds, k_ones[keys])
                    dk_cs = _dot_tn(ds, qo_ref[a, rows])
                    part[0] = part[0] + dk_cs[:, :LANES]
                    part[1] = part[1] + _dot_tn(prob.astype(BF16), dos_ref[a, rows])
                    part[2 + a] = part[2 + a] + dk_cs[:, LANES:]
            for (k0, k1), (dk, dv, cs0, cs1) in sums.items():
                keys = pl.ds(pl.multiple_of((2 * j + k0) * tq, tq), (k1 - k0) * tq)
                dk_ref[keys, :] += dk
                cs_ref[keys, :] += jnp.where(lo[:(k1 - k0) * tq], cs0, cs1)
                dv_ref[keys, :] += dv

        _fox_dispatch(sweep, last, dead_ref, 2 * pl.program_id(0), ii_ref[step] * nq + 2 * j)

        @pl.when(last == 1)
        def _():
            dq_ref[...] = (jnp.where(lo, dq_acc[0, :, :LANES], dq_acc[1, :, :LANES]) * SCALE).astype(BF16)
            rs_ref[...] = jnp.where(lo, dq_acc[0, :, LANES:], dq_acc[1, :, LANES:])

        @pl.when(step == n_steps - 1)
        def _():
            dk_out[...] = dk_ref[...].astype(BF16)
            dv_out[...] = dv_ref[...].astype(BF16)

        if ride:
            ride.at_last_step((FOX_HEADS // 2, n_steps), ride_in, ride_out, more[2 * n_ride + 11:])

    blk = (2 * tq, LANES)
    by_i = lambda col: (lambda hp, s, ii, jj, kk: (ii[s], col + hp))
    by_j = lambda col: (lambda hp, s, ii, jj, kk: (jj[s], col + hp))
    resident = pl.BlockSpec(blk, by_i(0))
    stat = pl.BlockSpec((2 * tq, 2 * LANES), by_i(0))
    whole = pl.BlockSpec((n_tok, LANES), lambda hp, s, ii, jj, kk: (0, hp))
    extra = ride if ride else _NO_RIDE
    grid_spec = pltpu.PrefetchScalarGridSpec(
        num_scalar_prefetch=3, grid=(FOX_HEADS // 2, n_steps),
        in_specs=[pl.BlockSpec(blk, by_i(Q_COL)), pl.BlockSpec(blk, by_j(K_COL)), pl.BlockSpec(blk, by_j(V_COL)),
                  pl.BlockSpec((None, 2, 2 * tq), lambda hp, s, ii, jj, kk: (hp, 0, jj[s])),
                  pl.BlockSpec(memory_space=pltpu.SMEM), resident, stat, stat] + extra.in_specs,
        out_specs=[resident, whole, whole, whole, resident] + extra.out_specs,
        scratch_shapes=[pltpu.VMEM((2, 2 * tq, LANES), BF16), pltpu.VMEM((2, 2 * tq, 2 * LANES), BF16),
                        pltpu.VMEM((2, 2 * tq, LANES), BF16), pltpu.VMEM((2, 2 * tq, 2 * LANES), F32),
                        pltpu.VMEM((n_tok, LANES), F32), pltpu.VMEM((n_tok, LANES), F32)] + extra.scratch)
    wide = lambda dt: jax.ShapeDtypeStruct((n_tok, 4 * LANES), dt)
    return pl.pallas_call(
        kern, grid_spec=grid_spec, name="fox_bwd",
        out_shape=[wide(BF16), wide(BF16), wide(BF16), wide(F32), wide(F32)] + extra.out_shape,
        compiler_params=_params(2, FOX_BWD_VMEM))(ii, jj, kk, zm, zm, zm, c_pairs, dead, d_out, lnorm, delta,
                                                  *extra.arrays)


def _all_gather(shards):
    n_w = len(shards)

    def kern(*refs):
        x_refs, out_refs = refs[:n_w], refs[n_w:2 * n_w]
        send_sems, recv_sems, local_sems = refs[2 * n_w:]
        x, y, c = _my_pos()
        me, sibling = (x, y, c), (x, y, 1 - c)
        chips = [(1 - x, y), (x, 1 - y), (1 - x, 1 - y)]

        def slot(w, px, py, pc):
            return out_refs[w].at[4 * px + 2 * py + pc]

        def copy(w, k, block, to, src=None):
            return pltpu.make_async_remote_copy(
                src_ref=slot(w, *block) if src is None else src, dst_ref=slot(w, *block),
                send_sem=send_sems.at[7 * w + k], recv_sem=recv_sems.at[7 * w + k], device_id=to, device_id_type=MESH)

        local, started = [], []
        for w in range(n_w):
            mine = pltpu.make_async_copy(x_refs[w], slot(w, *me), local_sems.at[w])
            mine.start()
            local.append(mine)
            first = [copy(w, 0, me, sibling, src=x_refs[w])]
            first += [copy(w, 1 + k, me, (*chip, c), src=x_refs[w]) for k, chip in enumerate(chips)]
            for cp in first:
                cp.start()
            started += first
        for k, chip in enumerate(chips):
            for w in range(n_w):
                copy(w, 1 + k, (*chip, c), me).wait_recv()
                passed = copy(w, 4 + k, (*chip, c), sibling)
                passed.start()
                started.append(passed)
        for w in range(n_w):
            copy(w, 0, sibling, me).wait_recv()
            for k, chip in enumerate(chips):
                copy(w, 4 + k, (*chip, 1 - c), me).wait_recv()
        for cp in started:
            cp.wait_send()
        for cp in local:
            cp.wait()

    any_spec = pl.BlockSpec(memory_space=pl.ANY)
    return pl.pallas_call(
        kern, out_shape=[jax.ShapeDtypeStruct((N_DEV,) + s.shape, s.dtype) for s in shards],
        in_specs=[any_spec] * n_w, out_specs=[any_spec] * n_w,
        scratch_shapes=[pltpu.SemaphoreType.DMA((7 * n_w,)), pltpu.SemaphoreType.DMA((7 * n_w,)),
                        pltpu.SemaphoreType.DMA((n_w,))],
        name="weight_all_gather")(*shards)


def _small_exchange(small):
    def kern(s_ref, sall_ref, *sems):
        copies = _gather_copies([s_ref], [sall_ref], *sems)
        _start_copies(*copies)
        _finish_copies(*copies)

    any_spec = pl.BlockSpec(memory_space=pl.ANY)
    return pl.pallas_call(
        kern, out_shape=jax.ShapeDtypeStruct((N_DEV,) + small.shape, small.dtype), in_specs=[any_spec],
        out_specs=any_spec, scratch_shapes=_exchange_scratch(1), name="small_grad_exchange")(small)


ADAMW_BLOCK_BYTES = 2 * 1024 * 1024


def _adamw(parts, w, m, v, name):
    n_parts, n_rows, n_cols = parts.shape
    limit = max(8, ADAMW_BLOCK_BYTES // (n_parts * n_cols * parts.dtype.itemsize))
    tr = max(t for t in range(8, n_rows + 1, 8) if n_rows % t == 0 and t <= limit)

    def kern(p_ref, w_ref, m_ref, v_ref, g_out, d_out, m_out, v_out):
        g = p_ref[0].astype(F32)
        for k in range(1, n_parts):
            g = g + p_ref[k].astype(F32)
        m_new = ADAM_B1 * m_ref[...] + (1.0 - ADAM_B1) * g
        v_new = ADAM_B2 * v_ref[...] + (1.0 - ADAM_B2) * jnp.square(g)
        m_hat = m_new / (1.0 - ADAM_B1 ** ADAM_STEP)
        v_hat = v_new / (1.0 - ADAM_B2 ** ADAM_STEP)
        g_out[...] = g
        d_out[...] = -ADAM_LR * (m_hat / (jnp.sqrt(v_hat) + ADAM_EPS) + ADAM_WD * w_ref[...])
        m_out[...] = m_new
        v_out[...] = v_new

    row = pl.BlockSpec((tr, n_cols), lambda i: (i, 0))
    out = jax.ShapeDtypeStruct((n_rows, n_cols), F32)
    return pl.pallas_call(
        kern, grid=(n_rows // tr,),
        in_specs=[pl.BlockSpec((n_parts, tr, n_cols), lambda i: (0, i, 0)), row, row, row],
        out_specs=[row, row, row, row], out_shape=[out, out, out, out], name=name,
        compiler_params=_params(1))(parts, w, m, v)


SHARDED = {
    "w_in": ((D_MODEL, D_IN), 1), "w_br_swa": ((512, D_MODEL), 1), "w_br_fox": ((512, D_MODEL), 1),
    "w_mix_out": ((D_MODEL, D_MODEL), 0), "w_ff1": ((D_MODEL, D_FF), 1), "w_ff2": ((D_FF, D_MODEL), 0),
    "w_ple_gate": ((D_MODEL, D_MODEL), 0), "w_ple_proj": ((PLE_DIM, D_MODEL), 1),
}
W_IN_SHARD = D_IN // N_DEV
W_IN_PAD = 640
SMALL = ("g_mix", "g_mlp", "g_ple", "g_final", "b_forget", "swa_sinks")
SMALL_COLS = 1024


def _wire_shard(name, a):
    a = a.reshape(a.shape[-2:])
    return jnp.pad(a, ((0, 0), (0, W_IN_PAD - W_IN_SHARD))) if name == "w_in" else a


def _from_wire(name, a):
    return (a[:, :W_IN_SHARD] if name == "w_in" else a)[None]


def _w_all_from_wire(stacked):
    w_in = jnp.concatenate([stacked[d][:, :W_IN_SHARD] for d in range(N_DEV)], axis=1)
    fpad = jnp.zeros((D_MODEL, N_FPAD - FOX_HEADS), stacked.dtype)
    return jnp.concatenate([w_in[:, :N_MAIN + FOX_HEADS], fpad, w_in[:, N_MAIN + FOX_HEADS:]], axis=1)


def _dw_in_to_wire(dw_all):
    dw_in = jnp.concatenate([dw_all[:, :N_MAIN + FOX_HEADS], dw_all[:, N_MAIN + N_FPAD:]], axis=1)
    pad = jnp.zeros((D_MODEL, W_IN_PAD - W_IN_SHARD), dw_all.dtype)
    return jnp.stack([jnp.concatenate([dw_in[:, d * W_IN_SHARD:(d + 1) * W_IN_SHARD], pad], axis=1)
                      for d in range(N_DEV)])


def _pack_small(vals, scalar=None):
    rows = [jnp.pad(vals[n].reshape(-1), (0, SMALL_COLS - vals[n].size)) for n in SMALL]
    if scalar is not None:
        rows.append(jnp.pad(scalar.reshape(1), (0, SMALL_COLS - 1)))
    rows += [jnp.zeros((SMALL_COLS,), F32)] * (8 - len(rows))
    return jnp.stack(rows)


def _unpack_small(slab, like):
    return {n: slab[r, :like[n].size].reshape(like[n].shape) for r, n in enumerate(SMALL)}


def _local_step(x, p, tgt, w, small, tm, tq, ts, late_shards=None):
    n_tok = x.shape[0]
    row = lambda v: v.reshape(1, -1)
    g_mix, g_mlp, g_ple, g_fin = row(small["g_mix"]), row(small["g_mlp"]), row(small["g_ple"]), row(small["g_final"])
    sinks = small["swa_sinks"].reshape(-1)
    b_col = small["b_forget"].reshape(FOX_HEADS, 1)

    assert tm == tq
    u1, zm, zfg, zf, nrm = _in_proj(x, g_mix, w["w_all"], tm)
    f_t = zf[:, :FOX_HEADS].T
    c_pairs = _decay_cumsum(f_t, b_col).reshape(FOX_HEADS // 2, 2, n_tok)
    attn_a, lse_a = _swa_fwd(zm, sinks)
    dead = _fox_dead_steps(nrm, c_pairs, tq)
    if late_shards is None:
        attn_b, ln_b = _fox_fwd(zm, c_pairs, dead, tq)
    else:
        attn_b, ln_b, *late = _fox_fwd(zm, c_pairs, dead, tq, _gather_ride(list(late_shards.values())))
        w = {**w, **_gathered_to_local(dict(zip(late_shards, late)))}
    ya, yb, mixed, h1, u2, a, r, h2 = _mix_ffn_fwd(attn_a, attn_b, zfg, x, w["w_br_swa"], w["w_br_fox"],
                                                   w["w_mix_out"], g_mlp, w["w_ff1"], w["w_ff2"], tm // 2)

    dlg, dpp, u3, dh2, dh2b, da, loss_acc, dgf, dgp = _head_ffn_bwd(
        h2, p, tgt, a, g_ple, w["w_ple_gate"], w["w_ple_proj"], g_fin, w["w_ff2"], tm // 2)
    dh1, dh1b, dgl, dya, dyb, daa, dab, delta_b, dgm = _ffn_bwd_b(
        da, dh2, h1, ya, yb, zfg, attn_b, w["w_ff1"], g_mlp, w["w_mix_out"], w["w_br_swa"], w["w_br_fox"], tm // 2)
    dq_a, dkp, dkc, dvp, dvc, dsk = _swa_bwd(zm, sinks, daa, attn_a, lse_a)
    dw = {
        "w_br_swa": _matmul_tn(attn_a, dya, "dw_br_swa", ts, stack_cols=D_MODEL // N_DEV),
        "w_br_fox": _matmul_tn(attn_b, dyb, "dw_br_fox", ts, stack_cols=D_MODEL // N_DEV),
        "w_mix_out": _matmul_tn(mixed, dh1b, "dw_mix_out", ts),
        "w_ff1": _matmul_tn(u2, da, "dw_ff1", ts, stack_cols=D_FF // N_DEV),
        "w_ff2": _matmul_tn(r, dh2b, "dw_ff2", ts),
        "w_ple_gate": _matmul_tn(u3, dlg, "dw_ple_gate", ts),
        "w_ple_proj": _matmul_tn(p, dpp, "dw_ple_proj", ts, stack_cols=D_MODEL // N_DEV),
    }
    if late_shards is None:
        dq_b, dk_b, dv_b, cs, rs = _fox_bwd(zm, c_pairs, dead, dab, ln_b, delta_b, tq)
        late_parts = None
    else:
        wire = _local_to_wire(dw)
        dq_b, dk_b, dv_b, cs, rs, *parts = _fox_bwd(zm, c_pairs, dead, dab, ln_b, delta_b, tq,
                                                    _scatter_ride([wire[n] for n in late_shards]))
        late_parts = dict(zip(late_shards, parts))

    up = lambda t: jnp.concatenate([t[SWA_BLOCK:], jnp.zeros((SWA_BLOCK, LANES), F32)], axis=0)
    dk_a, dv_a = dkc + up(dkp), dvc + up(dvp)
    df_t, db = _decay_bwd(cs, rs, f_t, b_col)
    df = jnp.pad(df_t.T, ((0, 0), (0, N_FPAD - FOX_HEADS)))
    dz = jnp.concatenate([dq_a, dk_a.astype(BF16), dv_a.astype(BF16), dq_b, dk_b, dv_b,
                          df.astype(BF16), dgl], axis=1)
    dw["w_all"] = _matmul_tn(u1, dz, "dw_in", ts)
    if late_shards is None:
        dx, dgx = _in_proj_bwd(dz, dh1, x, w["w_all"], g_mix, tm)
    else:
        dx, dgx, late_parts["w_in"] = _in_proj_bwd(dz, dh1, x, w["w_all"], g_mix, tm,
                                                   _scatter_ride([_dw_in_to_wire(dw["w_all"])]))
    dsmall = {"g_mix": dgx[0], "g_mlp": dgm[0], "g_ple": dgp[0], "g_final": dgf[0],
              "b_forget": db[:, 0], "swa_sinks": dsk[:, 0]}
    return loss_acc[0, 0], dx, dw, dsmall, late_parts


_ROWS = lambda t: t.reshape(-1, t.shape[-1])
_BY_ROWS = lambda t: t.reshape(N_DEV, t.shape[0] // N_DEV, t.shape[1])
_SAME = lambda t: t
LOCAL_LAYOUT = {
    "w_in": ("w_all", _w_all_from_wire, _dw_in_to_wire), "w_br_swa": ("w_br_swa", _SAME, _SAME),
    "w_br_fox": ("w_br_fox", _SAME, _SAME), "w_mix_out": ("w_mix_out", _ROWS, _BY_ROWS),
    "w_ff1": ("w_ff1", _SAME, _SAME), "w_ff2": ("w_ff2", _SAME, _BY_ROWS),
    "w_ple_gate": ("w_ple_gate", _ROWS, _BY_ROWS), "w_ple_proj": ("w_ple_proj", _SAME, _SAME),
}


def _gathered_to_local(g):
    return {LOCAL_LAYOUT[n][0]: LOCAL_LAYOUT[n][1](t) for n, t in g.items()}


def _local_to_wire(dw):
    names = {local: n for n, (local, _, _) in LOCAL_LAYOUT.items()}
    return {names[local]: LOCAL_LAYOUT[names[local]][2](t) for local, t in dw.items()}


def kernel(x, p, g_mix, w_in, b_forget, swa_sinks, w_br_swa, w_br_fox, w_mix_out, g_mlp, w_ff1, w_ff2, g_ple, w_ple_gate, w_ple_proj, g_final, loss_target, m_g_mix, m_w_in, m_b_forget, m_swa_sinks, m_w_br_swa, m_w_br_fox, m_w_mix_out, m_g_mlp, m_w_ff1, m_w_ff2, m_g_ple, m_w_ple_gate, m_w_ple_proj, m_g_final, v_g_mix, v_w_in, v_b_forget, v_swa_sinks, v_w_br_swa, v_w_br_fox, v_w_mix_out, v_g_mlp, v_w_ff1, v_w_ff2, v_g_ple, v_w_ple_gate, v_w_ple_proj, v_g_final):
    given = dict(g_mix=g_mix, w_in=w_in, b_forget=b_forget, swa_sinks=swa_sinks, w_br_swa=w_br_swa, w_br_fox=w_br_fox,
                 w_mix_out=w_mix_out, g_mlp=g_mlp, w_ff1=w_ff1, w_ff2=w_ff2, g_ple=g_ple, w_ple_gate=w_ple_gate,
                 w_ple_proj=w_ple_proj, g_final=g_final)
    mom = dict(g_mix=m_g_mix, w_in=m_w_in, b_forget=m_b_forget, swa_sinks=m_swa_sinks, w_br_swa=m_w_br_swa,
               w_br_fox=m_w_br_fox, w_mix_out=m_w_mix_out, g_mlp=m_g_mlp, w_ff1=m_w_ff1, w_ff2=m_w_ff2, g_ple=m_g_ple,
               w_ple_gate=m_w_ple_gate, w_ple_proj=m_w_ple_proj, g_final=m_g_final)
    vel = dict(g_mix=v_g_mix, w_in=v_w_in, b_forget=v_b_forget, swa_sinks=v_swa_sinks, w_br_swa=v_w_br_swa,
               w_br_fox=v_w_br_fox, w_mix_out=v_w_mix_out, g_mlp=v_g_mlp, w_ff1=v_w_ff1, w_ff2=v_w_ff2, g_ple=v_g_ple,
               w_ple_gate=v_w_ple_gate, w_ple_proj=v_w_ple_proj, g_final=v_g_final)
    names = list(given)
    sharded = list(SHARDED)

    w_wire = {n: _wire_shard(n, given[n]) for n in sharded}
    late = [n for n in sharded if n != "w_in"]
    gathered = _all_gather([w_wire["w_in"].astype(BF16)])
    local_w = _gathered_to_local({"w_in": gathered[0]})
    small = {n: given[n].reshape(-1) for n in SMALL}

    n_tok = x.shape[1]
    tile = min(TOKEN_TILE, n_tok // 4)
    loss_part, dx, dw, dsmall, parts = _local_step(
        x[0], p[0, 0], loss_target[0], local_w, small, tm=tile, tq=tile, ts=min(DW_TOKENS_PER_STEP, n_tok // 4),
        late_shards={n: w_wire[n].astype(BF16) for n in late})
    small_all = _small_exchange(_pack_small(dsmall, loss_part))

    res = {}
    for n in sharded:
        part = parts[n]
        flat = part.reshape(N_DEV, -1, part.shape[-1])
        outs = _adamw(flat, w_wire[n], _wire_shard(n, mom[n]), _wire_shard(n, vel[n]), "adamw_" + n)
        res[n] = [_from_wire(n, o) for o in outs]
    outs_s = _adamw(small_all, _pack_small(small), _pack_small({n: mom[n] for n in SMALL}),
                    _pack_small({n: vel[n] for n in SMALL}), "adamw_small")
    small_res = [_unpack_small(o, given) for o in outs_s]
    loss = outs_s[0][len(SMALL), 0]

    groups = [[res[n][k] if n in res else small_res[k][n] for n in names] for k in range(4)]
    return (loss, dx[None], *groups[0], *groups[1], *groups[2], *groups[3])
```

```python
import numpy as np
import jax
import jax.numpy as jnp
from jax import lax
from jax.experimental import pallas as pl
from jax.experimental.pallas import tpu as pltpu

F32 = jnp.float32
BF16 = jnp.bfloat16

D_MODEL = 1024
HEAD_DIM = 64
SWA_HEADS = 8
FOX_HEADS = 8
CHUNK_SHIFT = 6
SWA_BLOCK = 128
WINDOW_CHUNKS = 2
D_FF = 4096
PLE_DIM = 256
RMS_EPS = 1e-6
N_MAIN = 2304
N_FPAD = 128
N_GATE = 2048
D_IN = N_MAIN + FOX_HEADS + N_GATE
SCALE = HEAD_DIM ** -0.5
NEG = -1e30

ADAM_LR = 0.001
ADAM_B1 = 0.9
ADAM_B2 = 0.999
ADAM_EPS = 1e-08
ADAM_WD = 0.01
ADAM_STEP = 10

N_DEV = 8
TOKEN_TILE = 512
DW_TOKENS_PER_STEP = 2048
LANES = 128
V7X_VMEM_BYTES = 64 * 1024 * 1024
VMEM_LIMIT = V7X_VMEM_BYTES * 3 // 4
FOX_BWD_VMEM = V7X_VMEM_BYTES * 7 // 8
MESH = pl.DeviceIdType.MESH

_NT = (((1,), (1,)), ((), ()))
_TN = (((0,), (0,)), ((), ()))


def _params(n_grid, vmem_limit=VMEM_LIMIT):
    return pltpu.CompilerParams(dimension_semantics=("arbitrary",) * n_grid, vmem_limit_bytes=vmem_limit)


def _chunks(n, step):
    return [(s, min(step, n - s)) for s in range(0, n, step)]


def _sigmoid(x):
    return 1.0 / (1.0 + jnp.exp(-x))


def _dot(a, b):
    return jnp.dot(a, b, preferred_element_type=F32)


def _dot_nt(a, b):
    return lax.dot_general(a, b, _NT, preferred_element_type=F32)


def _dot_tn(a, b):
    return lax.dot_general(a, b, _TN, preferred_element_type=F32)


def _lane_concat(stacked_ref):
    return jnp.concatenate([stacked_ref[d] for d in range(N_DEV)], axis=1)


def _rms(h):
    return lax.rsqrt(jnp.mean(h * h, axis=-1, keepdims=True) + RMS_EPS)


def _rms_bwd(h, g, du):
    rs = _rms(h)
    n = h * rs
    dn = du * g
    dh = rs * (dn - n * jnp.mean(dn * n, axis=-1, keepdims=True))
    return dh, jnp.sum(du * n, axis=0, keepdims=True)


def _acc_rows(ref, i, row):
    @pl.when(i == 0)
    def _():
        ref[...] = jnp.zeros_like(ref)
    ref[...] += jnp.broadcast_to(row, ref.shape)


def _row_call(body, name, n_rows, tm, row_ins, const_ins, row_outs, acc_outs, ride=None, tile_outs=()):
    row_outs = list(row_outs)
    n_ri, n_ci, n_ro, n_ao = len(row_ins), len(const_ins), len(row_outs) + len(tile_outs), len(acc_outs)
    extra = ride if ride else _NO_RIDE
    n_ride = len(extra.arrays)
    grid = (n_rows // tm,)

    def kern(*refs):
        i = pl.program_id(0)
        ins, refs = refs[:n_ri + n_ci], refs[n_ri + n_ci:]
        ride_in, refs = refs[:n_ride], refs[n_ride:]
        outs, refs = refs[:n_ro + n_ao], refs[n_ro + n_ao:]
        ride_out, sems = refs[:n_ride], refs[n_ride:]
        if ride:
            ride.at_first_step(grid, ride_in, ride_out, sems)
        body(i, ins[:n_ri], ins[n_ri:], outs[:n_ro], outs[n_ro:])
        if ride:
            ride.at_last_step(grid, ride_in, ride_out, sems)

    def whole(a):
        zeros = (0,) * a.ndim
        return pl.BlockSpec(a.shape, lambda i: zeros, pipeline_mode=pl.Buffered(1))

    in_specs = [pl.BlockSpec((tm, a.shape[1]), lambda i: (i, 0)) for a in row_ins]
    in_specs += [whole(a) for a in const_ins] + extra.in_specs
    out_specs = [pl.BlockSpec((tm, c), lambda i: (i, 0)) for c, _ in row_outs]
    out_specs += [pl.BlockSpec((8, c), lambda i: (i, 0)) for c in tile_outs]
    out_specs += [pl.BlockSpec((8, c), lambda i: (0, 0)) for c in acc_outs] + extra.out_specs
    out_shape = [jax.ShapeDtypeStruct((n_rows, c), dt) for c, dt in row_outs]
    out_shape += [jax.ShapeDtypeStruct((8 * grid[0], c), F32) for c in tile_outs]
    out_shape += [jax.ShapeDtypeStruct((8, c), F32) for c in acc_outs] + extra.out_shape
    return pl.pallas_call(kern, grid=grid, in_specs=in_specs, out_specs=out_specs, out_shape=out_shape,
                          scratch_shapes=extra.scratch, name=name,
                          compiler_params=_params(1))(*row_ins, *const_ins, *extra.arrays)


def _in_proj(x, g_mix, w_all, tm):
    def body(i, ins, consts, outs, accs):
        x_ref, = ins
        g_ref, w_ref = consts
        u_ref, zm_ref, zfg_ref, zf_ref, nrm_ref = outs
        xv = x_ref[...]
        u = ((xv * _rms(xv)) * g_ref[...]).astype(BF16)
        u_ref[...] = u
        for s, n in _chunks(N_MAIN, 768):
            zm_ref[:, s:s + n] = _dot(u, w_ref[:, s:s + n]).astype(BF16)
        for s, n in _chunks(N_FPAD + N_GATE, 512):
            zfg_ref[:, s:s + n] = _dot(u, w_ref[:, N_MAIN + s:N_MAIN + s + n])
        zf_ref[...] = zfg_ref[:, :N_FPAD]
        lane = lax.broadcasted_iota(jnp.int32, (4 * LANES, LANES), 0)
        head = lax.broadcasted_iota(jnp.int32, (4 * LANES, LANES), 1)
        pick = (lane // HEAD_DIM == head).astype(BF16)
        tq_, tk_ = (zm_ref[:, col * LANES:(col + 4) * LANES].astype(F32) for col in (Q_COL, K_COL))
        rows = [jnp.max(_dot((t * t).astype(BF16), pick), axis=0, keepdims=True) for t in (tq_, tk_)]
        rows.append(jnp.min(_dot((tq_ * tk_).astype(BF16), pick), axis=0, keepdims=True))
        nrm_ref[...] = jnp.concatenate(rows + [jnp.zeros((5, LANES), F32)], axis=0)

    *outs, nrm = _row_call(body, "in_proj", x.shape[0], tm, [x], [g_mix, w_all],
                           [(D_MODEL, BF16), (N_MAIN, BF16), (N_FPAD + N_GATE, F32), (N_FPAD, F32)], [],
                           tile_outs=[LANES])
    return (*outs, nrm)


def _mix_ffn_fwd(attn_a, attn_b, zfg, x, w_sa, w_fo, w_mo, g_mlp, w1s, w2s, tm):
    ch = D_FF // N_DEV

    def body(i, ins, consts, outs, accs):
        aa_ref, ab_ref, zfg_ref, x_ref = ins
        wsa_ref, wfo_ref, wmo_ref, g_ref, w1_ref, w2_ref = consts
        ya_ref, yb_ref, mx_ref, h1_ref, u2_ref, a_ref, r_ref, h2_ref = outs
        ya = _dot(aa_ref[...], _lane_concat(wsa_ref))
        yb = _dot(ab_ref[...], _lane_concat(wfo_ref))
        g0 = _sigmoid(zfg_ref[:, N_FPAD:N_FPAD + D_MODEL])
        g1 = _sigmoid(zfg_ref[:, N_FPAD + D_MODEL:N_FPAD + 2 * D_MODEL])
        mixed = (g0 * ya + g1 * yb).astype(BF16)
        ya_ref[...] = ya.astype(BF16)
        yb_ref[...] = yb.astype(BF16)
        mx_ref[...] = mixed
        h1 = x_ref[...] + _dot(mixed, wmo_ref[...])
        h1_ref[...] = h1
        u = ((h1 * _rms(h1)) * g_ref[...]).astype(BF16)
        u2_ref[...] = u
        acc = h1
        for c in range(N_DEV):
            a = _dot(u, w1_ref[c])
            a_ref[:, c * ch:(c + 1) * ch] = a.astype(BF16)
            r = jnp.square(jnp.maximum(a, 0.0)).astype(BF16)
            r_ref[:, c * ch:(c + 1) * ch] = r
            acc = acc + _dot(r, w2_ref[c])
        h2_ref[...] = acc

    return _row_call(body, "mix_ffn_fwd", x.shape[0], tm, [attn_a, attn_b, zfg, x],
                     [w_sa, w_fo, w_mo, g_mlp, w1s, w2s],
                     [(D_MODEL, BF16), (D_MODEL, BF16), (D_MODEL, BF16), (D_MODEL, F32), (D_MODEL, BF16),
                      (D_FF, BF16), (D_FF, BF16), (D_MODEL, F32)], [])


def _head_ffn_bwd(h2, p, tgt, a, g_ple, w_pg, w_pp, g_fin, w2s, tm):
    ch = D_FF // N_DEV

    def body(i, ins, consts, outs, accs):
        h2_ref, p_ref, t_ref, a_ref = ins
        gp_ref, wpg_ref, wpp_ref, gf_ref, w2_ref = consts
        dlg_ref, dpp_ref, u3_ref, dh2_ref, dh2b_ref, da_ref = outs
        loss_ref, dgf_ref, dgp_ref = accs
        h2 = h2_ref[...]
        gp = gp_ref[...]
        u3 = ((h2 * _rms(h2)) * gp).astype(BF16)
        u3_ref[...] = u3
        pg = _sigmoid(_dot(u3, wpg_ref[...]))
        pp = _dot(p_ref[...].astype(BF16), _lane_concat(wpp_ref))
        h3 = h2 + pg * pp
        rs3 = _rms(h3)
        n3 = h3 * rs3
        gf = gf_ref[...]
        err = n3 * gf - t_ref[...]
        row_loss = 0.5 * jnp.mean(err * err, axis=-1, keepdims=True)
        _acc_rows(loss_ref, i, jnp.broadcast_to(jnp.sum(row_loss, axis=0, keepdims=True), (1, LANES)))
        dy = err * (1.0 / D_MODEL)
        _acc_rows(dgf_ref, i, jnp.sum(dy * n3, axis=0, keepdims=True))
        dn = dy * gf
        dh3 = rs3 * (dn - n3 * jnp.mean(dn * n3, axis=-1, keepdims=True))
        dpp_ref[...] = (dh3 * pg).astype(BF16)
        dlg = ((dh3 * pp) * pg * (1.0 - pg)).astype(BF16)
        dlg_ref[...] = dlg
        dh, dg = _rms_bwd(h2, gp, _dot_nt(dlg, wpg_ref[...]))
        _acc_rows(dgp_ref, i, dg)
        dh2 = dh3 + dh
        dh2_ref[...] = dh2
        dh2b = dh2.astype(BF16)
        dh2b_ref[...] = dh2b
        for c in range(N_DEV):
            dr = _dot_nt(dh2b, w2_ref[c])
            av = a_ref[:, c * ch:(c + 1) * ch].astype(F32)
            da_ref[:, c * ch:(c + 1) * ch] = (dr * (2.0 * jnp.maximum(av, 0.0))).astype(BF16)

    return _row_call(body, "head_ffn_bwd", h2.shape[0], tm, [h2, p, tgt, a], [g_ple, w_pg, w_pp, g_fin, w2s],
                     [(D_MODEL, BF16), (D_MODEL, BF16), (D_MODEL, BF16), (D_MODEL, F32), (D_MODEL, BF16),
                      (D_FF, BF16)], [LANES, D_MODEL, D_MODEL])


def _ffn_bwd_b(da, dh2, h1, ya, yb, zfg, attn_b, w1s, g_mlp, w_mo, w_sa, w_fo, tm):
    ch = D_FF // N_DEV

    def body(i, ins, consts, outs, accs):
        da_ref, dh2_ref, h1_ref, ya_ref, yb_ref, zfg_ref, ob_ref = ins
        w1_ref, gm_ref, wmo_ref, wsa_ref, wfo_ref = consts
        dh1_ref, dh1b_ref, dgl_ref, dya_ref, dyb_ref, daa_ref, dab_ref, dl_ref = outs
        dgm_ref, = accs
        du2 = _dot_nt(da_ref[:, 0:ch], w1_ref[0])
        for c in range(1, N_DEV):
            du2 = du2 + _dot_nt(da_ref[:, c * ch:(c + 1) * ch], w1_ref[c])
        dh, dg = _rms_bwd(h1_ref[...], gm_ref[...], du2)
        _acc_rows(dgm_ref, i, dg)
        dh1 = dh2_ref[...] + dh
        dh1_ref[...] = dh1
        dh1b = dh1.astype(BF16)
        dh1b_ref[...] = dh1b
        dmx = _dot_nt(dh1b, wmo_ref[...])
        g0 = _sigmoid(zfg_ref[:, N_FPAD:N_FPAD + D_MODEL])
        g1 = _sigmoid(zfg_ref[:, N_FPAD + D_MODEL:N_FPAD + 2 * D_MODEL])
        dya = (dmx * g0).astype(BF16)
        dyb = (dmx * g1).astype(BF16)
        dya_ref[...] = dya
        dyb_ref[...] = dyb
        dgl_ref[:, 0:D_MODEL] = ((dmx * ya_ref[...].astype(F32)) * g0 * (1.0 - g0)).astype(BF16)
        dgl_ref[:, D_MODEL:2 * D_MODEL] = ((dmx * yb_ref[...].astype(F32)) * g1 * (1.0 - g1)).astype(BF16)
        daa_ref[...] = _dot_nt(dya, _lane_concat(wsa_ref)).astype(BF16)
        dab = _dot_nt(dyb, _lane_concat(wfo_ref)).astype(BF16)
        dab_ref[...] = dab
        half_in = lax.broadcasted_iota(jnp.int32, (LANES, 2 * LANES), 0) // HEAD_DIM
        half_out = lax.broadcasted_iota(jnp.int32, (LANES, 2 * LANES), 1) // LANES
        pick = (half_in == half_out).astype(BF16)
        for pair in range(FOX_HEADS // 2):
            cols = slice(pair * LANES, (pair + 1) * LANES)
            prod = dab[:, cols].astype(F32) * ob_ref[:, cols].astype(F32)
            hi = prod.astype(BF16)
            lo_part = (prod - hi.astype(F32)).astype(BF16)
            dl_ref[:, 2 * pair * LANES:(2 * pair + 2) * LANES] = _dot(hi, pick) + _dot(lo_part, pick)

    half = D_MODEL // 2
    return _row_call(body, "ffn_bwd_b", h1.shape[0], tm, [da, dh2, h1, ya, yb, zfg, attn_b],
                     [w1s, g_mlp, w_mo, w_sa, w_fo],
                     [(D_MODEL, F32), (D_MODEL, BF16), (N_GATE, BF16), (D_MODEL, BF16), (D_MODEL, BF16),
                      (half, BF16), (half, BF16), (FOX_HEADS * LANES, F32)], [D_MODEL])


def _in_proj_bwd(dz, dh1, x, w_all, g_mix, tm, ride=None):
    def body(i, ins, consts, outs, accs):
        dz_ref, dh1_ref, x_ref = ins
        w_ref, g_ref = consts
        dx_ref, = outs
        dgx_ref, = accs
        du1 = _dot_nt(dz_ref[...], w_ref[...])
        dh, dg = _rms_bwd(x_ref[...], g_ref[...], du1)
        _acc_rows(dgx_ref, i, dg)
        dx_ref[...] = dh1_ref[...] + dh

    return _row_call(body, "in_proj_bwd", x.shape[0], tm, [dz, dh1, x], [w_all, g_mix],
                     [(D_MODEL, F32)], [D_MODEL], ride)


def _matmul_tn(a, b, name, ts, stack_cols=0):
    n_rows, ka = a.shape
    n = b.shape[1]
    tk = min(ka, 1024)
    tn = 896 if n % 1024 else 1024
    n_stack = tn // stack_cols if stack_cols else 0
    assert ka % tk == 0 and n % tn == 0 and n_rows % ts == 0 and (not stack_cols or tk == ka)
    n_steps = n_rows // ts

    def kern(a_ref, b_ref, o_ref, acc_ref):
        s = pl.program_id(2)

        @pl.when(s == 0)
        def _():
            acc_ref[...] = jnp.zeros_like(acc_ref)
        acc_ref[...] += _dot_tn(a_ref[...].astype(BF16), b_ref[...])

        @pl.when(s == n_steps - 1)
        def _():
            if stack_cols:
                for c in range(n_stack):
                    o_ref[c] = acc_ref[:, c * stack_cols:(c + 1) * stack_cols].astype(BF16)
            else:
                o_ref[...] = acc_ref[...].astype(BF16)

    if stack_cols:
        out_spec = pl.BlockSpec((n_stack, tk, stack_cols), lambda i, j, s: (j, 0, 0))
        out_shape = jax.ShapeDtypeStruct((n // stack_cols, ka, stack_cols), BF16)
    else:
        out_spec = pl.BlockSpec((tk, tn), lambda i, j, s: (i, j))
        out_shape = jax.ShapeDtypeStruct((ka, n), BF16)
    return pl.pallas_call(
        kern, grid=(ka // tk, n // tn, n_steps),
        in_specs=[pl.BlockSpec((ts, tk), lambda i, j, s: (s, i)), pl.BlockSpec((ts, tn), lambda i, j, s: (s, j))],
        out_specs=out_spec, out_shape=out_shape, scratch_shapes=[pltpu.VMEM((tk, tn), F32)], name=name,
        compiler_params=_params(3))(a, b)


SCAN_CHUNK = 512
BWD_SCAN_CHUNK = 1024


def _decay_cumsum(f_t, b_col):
    n_tok = f_t.shape[1]
    ch = min(SCAN_CHUNK, n_tok)

    def kern(f_ref, b_ref, c_ref):
        r = lax.broadcasted_iota(jnp.int32, (ch, ch), 0)
        c = lax.broadcasted_iota(jnp.int32, (ch, ch), 1)
        tri = (r <= c).astype(F32)
        carry = jnp.zeros((8, 1), F32)
        for k in range(n_tok // ch):
            xv = f_ref[:, k * ch:(k + 1) * ch] + b_ref[...]
            lf = jnp.minimum(xv, 0.0) - jnp.log(1.0 + jnp.exp(-jnp.abs(xv)))
            cs = jnp.dot(lf, tri, precision=lax.Precision.HIGHEST, preferred_element_type=F32) + carry
            c_ref[:, k * ch:(k + 1) * ch] = cs
            carry = cs[:, ch - 1:ch]

    return pl.pallas_call(kern, out_shape=jax.ShapeDtypeStruct((8, n_tok), F32), name="decay_cumsum",
                          compiler_params=_params(0))(f_t, b_col)


def _decay_bwd(cs, rs, f_t, b_col):
    n_tok = f_t.shape[1]
    ch = min(BWD_SCAN_CHUNK, n_tok)
    n_ch = n_tok // ch

    def kern(cs_ref, rs_ref, f_ref, b_ref, df_ref, db_ref, carry_ref):
        k = pl.program_id(0)

        @pl.when(k == 0)
        def _():
            carry_ref[...] = jnp.zeros_like(carry_ref)
            db_ref[...] = jnp.zeros_like(db_ref)

        r = lax.broadcasted_iota(jnp.int32, (ch, ch), 0)
        c = lax.broadcasted_iota(jnp.int32, (ch, ch), 1)
        tri = (r >= c).astype(F32)
        head = lax.broadcasted_iota(jnp.int32, (8, 4 * LANES), 0)
        lane = lax.broadcasted_iota(jnp.int32, (8, 4 * LANES), 1)
        pick = (lane == HEAD_DIM * head).astype(F32)
        dc = lax.dot_general(pick, rs_ref[...] - cs_ref[...], _NT, precision=lax.Precision.HIGHEST,
                             preferred_element_type=F32)
        rc = jnp.dot(dc, tri, precision=lax.Precision.HIGHEST, preferred_element_type=F32) + carry_ref[:, 0:1]
        carry_ref[...] = jnp.broadcast_to(rc[:, 0:1], carry_ref.shape)
        df = rc / (1.0 + jnp.exp(f_ref[...] + b_ref[...]))
        df_ref[...] = df
        db_ref[...] += jnp.broadcast_to(jnp.sum(df, axis=1, keepdims=True), db_ref.shape)

    back = lambda k: n_ch - 1 - k
    wide = pl.BlockSpec((ch, 4 * LANES), lambda k: (back(k), 0))
    row = pl.BlockSpec((8, ch), lambda k: (0, back(k)))
    return pl.pallas_call(
        kern, grid=(n_ch,),
        in_specs=[wide, wide, row, pl.BlockSpec((8, 1), lambda k: (0, 0))],
        out_specs=[row, pl.BlockSpec((8, LANES), lambda k: (0, 0))],
        out_shape=[jax.ShapeDtypeStruct((8, n_tok), F32), jax.ShapeDtypeStruct((8, LANES), F32)],
        scratch_shapes=[pltpu.VMEM((8, LANES), F32)], name="decay_bwd", compiler_params=_params(1))(cs, rs, f_t, b_col)


def _swa_bias_table():
    row = jnp.arange(SWA_BLOCK)[:, None] + SWA_BLOCK
    col = jnp.arange(2 * SWA_BLOCK)[None, :]
    cd = (row >> CHUNK_SHIFT) - (col >> CHUNK_SHIFT)
    band = (cd >= 0) & (cd <= WINDOW_CHUNKS)
    slopes = jnp.asarray([2.0 ** -(h + 1) for h in range(SWA_HEADS)], F32)
    bias = -slopes[:, None, None] * jnp.abs(row - col).astype(F32)[None]
    return jnp.stack([jnp.where(band & (col >= SWA_BLOCK), bias, NEG), jnp.where(band, bias, NEG)])


SWA_PER_STEP = 4


def _swap_halves(t):
    return pltpu.roll(t.astype(F32), HEAD_DIM, axis=1).astype(t.dtype)


def _swa_specs():
    blk, rows = SWA_BLOCK, SWA_PER_STEP * SWA_BLOCK
    q = pl.BlockSpec((rows, 4 * LANES), lambda n: (n, 0))
    before = lambda col: pl.BlockSpec((blk, LANES), lambda n: (jnp.maximum(SWA_PER_STEP * n - 1, 0), col))
    own = lambda col: pl.BlockSpec((rows, LANES), lambda n: (n, col))
    bias = pl.BlockSpec((2, SWA_HEADS, blk, 2 * blk), lambda n: (0, 0, 0, 0))
    return [q, before(4), own(4), before(5), own(5), bias]


def _swa_band(before_ref, own_ref):
    both = jnp.concatenate([before_ref[...], own_ref[...]], axis=0)
    return both, _swap_halves(both)


def _swa_bias(bias_ref, n, b):
    return bias_ref.at[jnp.minimum(n, 1)] if b == 0 else bias_ref.at[1]


SWA_GROUPS = ([h for h in range(SWA_HEADS) if h % 2 == h // 4], [h for h in range(SWA_HEADS) if h % 2 != h // 4])


def _stack_heads(ref, rows, heads, lo, mask_halves):
    tiles = []
    for h in heads:
        t = ref[rows, (h // 2) * LANES:(h // 2 + 1) * LANES]
        tiles.append(jnp.where(lo if h % 2 == 0 else ~lo, t, jnp.zeros_like(t)) if mask_halves else t)
    return jnp.concatenate(tiles, axis=0)


def _per_head_column(values, heads):
    return jnp.concatenate([jnp.full((SWA_BLOCK, 1), values(h), F32) for h in heads], axis=0)


def _swa_scores(q_ref, rows, kx, heads, lo, bias):
    qa = _stack_heads(q_ref, rows, heads, lo, True) * SCALE
    return qa, _dot_nt(qa, kx) + jnp.concatenate([bias[h] for h in heads], axis=0)


def _swa_fwd(zm, sinks):
    n_tok = zm.shape[0]
    blk, step_rows = SWA_BLOCK, SWA_PER_STEP * SWA_BLOCK

    def kern(q_ref, kp_ref, kc_ref, vp_ref, vc_ref, bias_ref, sink_ref, o_ref, lse_ref):
        n = pl.program_id(0)
        (k_all, k_all_sw), (v_all, v_all_sw) = _swa_band(kp_ref, kc_ref), _swa_band(vp_ref, vc_ref)
        lane = lax.broadcasted_iota(jnp.int32, (blk, LANES), 1)
        lo = lane < HEAD_DIM
        for b in range(SWA_PER_STEP):
            rows, band = slice(b * blk, (b + 1) * blk), slice(b * blk, (b + 2) * blk)
            bias = _swa_bias(bias_ref, n, b)
            lse_t = jnp.zeros((blk, LANES), F32)
            for pair in range(SWA_HEADS // 2):
                q2 = q_ref[rows, pair * LANES:(pair + 1) * LANES]
                outs = []
                for a in range(2):
                    h = 2 * pair + a
                    qa = jnp.where(lo if a == 0 else ~lo, q2, jnp.zeros_like(q2)) * SCALE
                    kx, vx = (k_all[band], v_all[band]) if h in SWA_GROUPS[0] else (k_all_sw[band], v_all_sw[band])
                    s = _dot_nt(qa, kx) + bias[h]
                    sink = sink_ref[h]
                    m = jnp.maximum(jnp.max(s, axis=-1, keepdims=True), sink)
                    e = jnp.exp(s - m)
                    l = jnp.sum(e, axis=-1, keepdims=True) + jnp.exp(sink - m)
                    pn = (e * (1.0 / l)).astype(BF16)
                    outs.append(_dot(pn, vx))
                    lse_t = jnp.where(lane == h, m + jnp.log(l), lse_t)
                o_ref[rows, pair * LANES:(pair + 1) * LANES] = jnp.where(lo, outs[0], outs[1]).astype(BF16)
            lse_ref[rows, :] = lse_t

    return pl.pallas_call(
        kern, grid=(n_tok // step_rows,),
        in_specs=_swa_specs() + [pl.BlockSpec(memory_space=pltpu.SMEM)],
        out_specs=[pl.BlockSpec((step_rows, 4 * LANES), lambda n: (n, 0)),
                   pl.BlockSpec((step_rows, LANES), lambda n: (n, 0))],
        out_shape=[jax.ShapeDtypeStruct((n_tok, 4 * LANES), BF16), jax.ShapeDtypeStruct((n_tok, LANES), F32)],
        name="swa_fwd", compiler_params=_params(1))(zm, zm, zm, zm, zm, _swa_bias_table(), sinks)


def _swa_bwd(zm, sinks, d_out, out, lse):
    n_tok = zm.shape[0]
    blk, step_rows = SWA_BLOCK, SWA_PER_STEP * SWA_BLOCK

    def kern(q_ref, kp_ref, kc_ref, vp_ref, vc_ref, bias_ref, do_ref, o_ref, lse_ref, sink_ref,
             dq_ref, dkp_ref, dkc_ref, dvp_ref, dvc_ref, dsk_ref):
        n = pl.program_id(0)

        @pl.when(n == 0)
        def _():
            dsk_ref[...] = jnp.zeros_like(dsk_ref)

        bands = (_swa_band(kp_ref, kc_ref), _swa_band(vp_ref, vc_ref))
        lane = lax.broadcasted_iota(jnp.int32, (blk, LANES), 1)
        lo = lane < HEAD_DIM
        for b in range(SWA_PER_STEP):
            rows, band = slice(b * blk, (b + 1) * blk), slice(b * blk, (b + 2) * blk)
            bias = _swa_bias(bias_ref, n, b)
            lse_t = lse_ref[rows, :]
            dqs, dkv = {}, []
            for g, heads in enumerate(SWA_GROUPS):
                kx, vx = bands[0][g][band], bands[1][g][band]
                qa, s = _swa_scores(q_ref, rows, kx, heads, lo, bias)
                doa = _stack_heads(do_ref, rows, heads, lo, True)
                lse_g = jnp.concatenate([lse_t[:, h:h + 1] for h in heads], axis=0)
                prob = jnp.exp(s - lse_g)
                o_g = _stack_heads(o_ref, rows, heads, lo, False)
                dd = jnp.sum(doa.astype(F32) * o_g.astype(F32), axis=-1, keepdims=True)
                ds = (prob * (_dot_nt(doa, vx) - dd)).astype(BF16)
                sink_part = -jnp.exp(_per_head_column(lambda h: sink_ref[h], heads) - lse_g) * dd
                dq = _dot(ds, kx) * SCALE
                for r, h in enumerate(heads):
                    dqs[h] = dq[r * blk:(r + 1) * blk]
                    dsk_ref[h:h + 1, :] += jnp.broadcast_to(
                        jnp.sum(sink_part[r * blk:(r + 1) * blk], axis=0, keepdims=True), (1, LANES))
                dkv.append((_dot_tn(ds, qa), _dot_tn(prob.astype(BF16), doa)))
            for pair in range(SWA_HEADS // 2):
                dq_ref[rows, pair * LANES:(pair + 1) * LANES] = jnp.where(
                    lo, dqs[2 * pair], dqs[2 * pair + 1]).astype(BF16)
            dk = dkv[0][0] + pltpu.roll(dkv[1][0], HEAD_DIM, axis=1)
            dv = dkv[0][1] + pltpu.roll(dkv[1][1], HEAD_DIM, axis=1)
            dkp_ref[rows, :] = dk[0:blk]
            dkc_ref[rows, :] = dk[blk:2 * blk]
            dvp_ref[rows, :] = dv[0:blk]
            dvc_ref[rows, :] = dv[blk:2 * blk]

    wide = pl.BlockSpec((step_rows, 4 * LANES), lambda n: (n, 0))
    narrow = pl.BlockSpec((step_rows, LANES), lambda n: (n, 0))
    part = jax.ShapeDtypeStruct((n_tok, LANES), F32)
    return pl.pallas_call(
        kern, grid=(n_tok // step_rows,),
        in_specs=_swa_specs() + [wide, wide, narrow, pl.BlockSpec(memory_space=pltpu.SMEM)],
        out_specs=[wide, narrow, narrow, narrow, narrow, pl.BlockSpec((8, LANES), lambda n: (0, 0))],
        out_shape=[jax.ShapeDtypeStruct((n_tok, 4 * LANES), BF16), part, part, part, part,
                   jax.ShapeDtypeStruct((8, LANES), F32)],
        name="swa_bwd", compiler_params=_params(1))(zm, zm, zm, zm, zm, _swa_bias_table(), d_out, out, lse, sinks)


def _my_pos():
    return lax.axis_index("x"), lax.axis_index("y"), lax.axis_index("c")


def _peer(k):
    x, y, c = _my_pos()
    px, py, pc = x ^ (k >> 2), y ^ ((k >> 1) & 1), c ^ (k & 1)
    return (px, py, pc), 4 * px + 2 * py + pc


def _gather_copies(x_refs, out_refs, send_sems, recv_sems, local_sems):
    x, y, c = _my_pos()
    my_id = 4 * x + 2 * y + c
    local = [pltpu.make_async_copy(x_refs[w], out_refs[w].at[my_id], local_sems.at[w]) for w in range(len(x_refs))]
    sends, arrivals = [], []
    for k in range(1, N_DEV):
        peer, peer_id = _peer(k)
        for w in range(len(x_refs)):
            sems = dict(send_sem=send_sems.at[7 * w + k - 1], recv_sem=recv_sems.at[7 * w + k - 1],
                        device_id=peer, device_id_type=MESH)
            sends.append(pltpu.make_async_remote_copy(src_ref=x_refs[w], dst_ref=out_refs[w].at[my_id], **sems))
            arrivals.append(pltpu.make_async_remote_copy(src_ref=x_refs[w], dst_ref=out_refs[w].at[peer_id], **sems))
    return local, sends, arrivals


def _scatter_copies(g_refs, part_refs, send_sems, recv_sems, local_sems):
    x, y, c = _my_pos()
    my_id = 4 * x + 2 * y + c
    local = [pltpu.make_async_copy(g_refs[w].at[my_id], part_refs[w].at[0], local_sems.at[w])
             for w in range(len(g_refs))]
    sends, arrivals = [], []
    for k in range(1, N_DEV):
        peer, peer_id = _peer(k)
        for w in range(len(g_refs)):
            sems = dict(send_sem=send_sems.at[7 * w + k - 1], recv_sem=recv_sems.at[7 * w + k - 1],
                        device_id=peer, device_id_type=MESH)
            sends.append(pltpu.make_async_remote_copy(src_ref=g_refs[w].at[peer_id], dst_ref=part_refs[w].at[k], **sems))
            arrivals.append(pltpu.make_async_remote_copy(src_ref=g_refs[w].at[my_id], dst_ref=part_refs[w].at[k], **sems))
    return local, sends, arrivals


def _start_copies(local, sends, arrivals):
    for cp in local + sends:
        cp.start()


def _finish_copies(local, sends, arrivals):
    for cp in arrivals:
        cp.wait_recv()
    for cp in sends:
        cp.wait_send()
    for cp in local:
        cp.wait()


def _exchange_scratch(n_arrays):
    return [pltpu.SemaphoreType.DMA((7 * n_arrays,)), pltpu.SemaphoreType.DMA((7 * n_arrays,)),
            pltpu.SemaphoreType.DMA((n_arrays,))]


class _Ride:
    def __init__(self, arrays, out_shape, copies):
        self.arrays, self.out_shape, self.copies = list(arrays), list(out_shape), copies
        any_spec = pl.BlockSpec(memory_space=pl.ANY)
        self.in_specs = [any_spec] * len(self.arrays)
        self.out_specs = [any_spec] * len(self.arrays)
        self.scratch = _exchange_scratch(len(self.arrays)) if self.arrays else []

    @staticmethod
    def _at(grid, last):
        hit = [pl.program_id(d) == (n - 1 if last else 0) for d, n in enumerate(grid)]
        return hit[0] if len(hit) == 1 else jnp.logical_and(*hit)

    def at_first_step(self, grid, in_refs, out_refs, sems):
        @pl.when(self._at(grid, False))
        def _():
            _start_copies(*self.copies(in_refs, out_refs, *sems))

    def at_last_step(self, grid, in_refs, out_refs, sems):
        @pl.when(self._at(grid, True))
        def _():
            _finish_copies(*self.copies(in_refs, out_refs, *sems))


_NO_RIDE = _Ride([], [], None)


def _gather_ride(shards):
    return _Ride(shards, [jax.ShapeDtypeStruct((N_DEV,) + s.shape, s.dtype) for s in shards], _gather_copies)


def _scatter_ride(grads):
    return _Ride(grads, [jax.ShapeDtypeStruct(g.shape, g.dtype) for g in grads], _scatter_copies)


Q_COL, K_COL, V_COL = 6, 10, 14


def _low_half(rows):
    return lax.broadcasted_iota(jnp.int32, (rows, LANES), 1) < HEAD_DIM


def _lane_tile(stat, width):
    return jnp.tile(stat, (1, width // LANES))


FOX_KEY_BLOCKS = 4


def _fox_steps(nq):
    kb = FOX_KEY_BLOCKS
    steps = [(i2, j, 0 if j < (2 * i2) // kb else 1 + (2 * i2) % kb)
             for i2 in range(nq // 2) for j in range((2 * i2) // kb + 1)]
    return [np.asarray(col, np.int32) for col in zip(*steps)]


def _fox_tiles(kind, near_only=False):
    kb = FOX_KEY_BLOCKS
    if kind == 0:
        return [(sub, kb // 2 if near_only else 0, kb, None) for sub in range(2)]
    return [(sub, 0, kind + sub, kind - 1 + sub) for sub in range(2)]


def _causal(t, first_row):
    row = lax.broadcasted_iota(jnp.int32, t.shape, 0) + first_row
    col = lax.broadcasted_iota(jnp.int32, t.shape, 1)
    return jnp.where(col <= row, t, NEG)


def _fox_dispatch(sweep, kind, dead_ref, head0, idx):
    kb = FOX_KEY_BLOCKS
    below = kind == 0

    def all_dead(h, blocks):
        dead = dead_ref[h, idx + blocks[0]] > 0.5
        for b in blocks[1:]:
            dead = jnp.logical_and(dead, dead_ref[h, idx + b] > 0.5)
        return dead

    takes_all, takes_near = [], []
    for h in (head0, head0 + 1):
        far_dead = all_dead(h, list(range(kb // 2)))
        takes_all.append(jnp.logical_not(far_dead))
        takes_near.append(jnp.logical_and(far_dead, jnp.logical_not(all_dead(h, list(range(kb // 2, kb))))))
    joint = jnp.logical_and(takes_all[0], takes_all[1])
    pl.when(jnp.logical_and(below, joint))(lambda: sweep(_fox_tiles(0), (0, 1)))
    apart = jnp.logical_and(below, jnp.logical_not(joint))
    for a in range(2):
        pl.when(jnp.logical_and(apart, takes_all[a]))(lambda a=a: sweep(_fox_tiles(0), (a,)))
        pl.when(jnp.logical_and(below, takes_near[a]))(lambda a=a: sweep(_fox_tiles(0, True), (a,)))
    for p in range(0, kb, 2):
        pl.when(kind == 1 + p)(lambda p=p: sweep(_fox_tiles(1 + p), (0, 1)))


EXP_ZERO = 104.5
NORM_SLACK = 1.005


def _fox_dead_steps(nrm, c_pairs, tq):
    nq = nrm.shape[0] // 8
    stats = nrm.reshape(nq, 8, LANES)[:, :3, :FOX_HEADS]
    qn, kn, own = jnp.sqrt(stats[:, 0]) * SCALE, jnp.sqrt(stats[:, 1]), stats[:, 2] * SCALE
    cb = c_pairs.reshape(FOX_HEADS, nq, tq)
    c_max, c_min = jnp.max(cb, axis=-1).T, jnp.min(cb, axis=-1).T
    both = lambda t, pick: pick(t.reshape(nq // 2, 2, FOX_HEADS), axis=1)
    qn2, kn2, c_max2, own2 = both(qn, jnp.max), both(kn, jnp.max), both(c_max, jnp.max), both(own, jnp.min)
    row_max_floor = own2 - (NORM_SLACK - 1.0) * qn2 * kn2 - c_max2
    gap = qn2[:, None] * kn[None] * NORM_SLACK - c_min[None] - row_max_floor[:, None]
    below = jnp.arange(nq)[None, :] < 2 * jnp.arange(nq // 2)[:, None]
    dead = jnp.logical_and(gap < -EXP_ZERO, below[..., None])
    return dead.transpose(2, 0, 1).reshape(FOX_HEADS, -1).astype(F32)


def _fox_fwd(zm, c_pairs, dead, tq, ride=None):
    n_tok = zm.shape[0]
    nq = n_tok // tq
    ii, jj, kk = _fox_steps(nq)
    n_steps = len(ii)
    n_ride = len(ride.arrays) if ride else 0

    def kern(ii_ref, jj_ref, kk_ref, q_ref, k_ref, v_ref, ck_ref, dead_ref, *more):
        ride_in, (o_ref, ln_ref), ride_out = more[:n_ride], more[n_ride:n_ride + 2], more[n_ride + 2:2 * n_ride + 2]
        qs_ref, m_ref, l_ref, acc_ref = more[2 * n_ride + 2:2 * n_ride + 6]
        step = pl.program_id(1)
        j, kind = jj_ref[step], kk_ref[step]
        lo = lax.broadcasted_iota(jnp.int32, (2 * tq, LANES), 1) < HEAD_DIM
        if ride:
            ride.at_first_step((FOX_HEADS // 2, n_steps), ride_in, ride_out, more[2 * n_ride + 6:])

        @pl.when(j == 0)
        def _():
            q2 = q_ref[...]
            zq = jnp.zeros_like(q2)
            qs_ref[0] = jnp.where(lo, q2, zq) * SCALE
            qs_ref[1] = jnp.where(lo, zq, q2) * SCALE
            m_ref[...] = jnp.full(m_ref.shape, NEG, F32)
            l_ref[...] = jnp.zeros(l_ref.shape, F32)
            acc_ref[...] = jnp.zeros(acc_ref.shape, F32)

        def sweep(tiles, heads):
            v_ones = jnp.concatenate([v_ref[...], jnp.ones((FOX_KEY_BLOCKS * tq, LANES), BF16)], axis=1)
            for sub, k0, k1, diagonal in tiles:
                rows, keys = slice(sub * tq, (sub + 1) * tq), slice(k0 * tq, k1 * tq)
                for a in heads:
                    t = _dot_nt(qs_ref[a, rows], k_ref[keys, :]) - ck_ref[a:a + 1, keys]
                    if diagonal is not None:
                        t = _causal(t, diagonal * tq)
                    m_old = m_ref[a, rows]
                    m_new = jnp.maximum(m_old, jnp.max(t, axis=-1, keepdims=True))
                    alpha = jnp.exp(m_old - m_new)
                    e = jnp.exp(t - _lane_tile(m_new, (k1 - k0) * tq)).astype(BF16)
                    pv = _dot(e, v_ones[keys])
                    acc_ref[a, rows] = alpha * acc_ref[a, rows] + pv[:, :LANES]
                    l_ref[a, rows] = alpha * l_ref[a, rows] + pv[:, LANES:]
                    m_ref[a, rows] = m_new

        _fox_dispatch(sweep, kind, dead_ref, 2 * pl.program_id(0), ii_ref[step] * nq + FOX_KEY_BLOCKS * j)

        @pl.when(kind != 0)
        def _():
            o_ref[...] = jnp.where(lo, acc_ref[0] / l_ref[0], acc_ref[1] / l_ref[1]).astype(BF16)
            ln_ref[:, :LANES] = m_ref[0] + jnp.log(l_ref[0])
            ln_ref[:, LANES:] = m_ref[1] + jnp.log(l_ref[1])

        if ride:
            ride.at_last_step((FOX_HEADS // 2, n_steps), ride_in, ride_out, more[2 * n_ride + 6:])

    blk, kblk = (2 * tq, LANES), (FOX_KEY_BLOCKS * tq, LANES)
    by_i = lambda col: (lambda hp, s, ii, jj, kk: (ii[s], col + hp))
    by_j = lambda col: (lambda hp, s, ii, jj, kk: (jj[s], col + hp))
    extra = ride if ride else _NO_RIDE
    grid_spec = pltpu.PrefetchScalarGridSpec(
        num_scalar_prefetch=3, grid=(FOX_HEADS // 2, n_steps),
        in_specs=[pl.BlockSpec(blk, by_i(Q_COL)), pl.BlockSpec(kblk, by_j(K_COL)), pl.BlockSpec(kblk, by_j(V_COL)),
                  pl.BlockSpec((None, 2, kblk[0]), lambda hp, s, ii, jj, kk: (hp, 0, jj[s])),
                  pl.BlockSpec(memory_space=pltpu.SMEM)] + extra.in_specs,
        out_specs=[pl.BlockSpec(blk, by_i(0)), pl.BlockSpec((2 * tq, 2 * LANES), by_i(0))] + extra.out_specs,
        scratch_shapes=[pltpu.VMEM((2, 2 * tq, LANES), BF16), pltpu.VMEM((2, 2 * tq, LANES), F32),
                        pltpu.VMEM((2, 2 * tq, LANES), F32), pltpu.VMEM((2, 2 * tq, LANES), F32)] + extra.scratch)
    return pl.pallas_call(
        kern, grid_spec=grid_spec,
        out_shape=[jax.ShapeDtypeStruct((n_tok, 4 * LANES), BF16),
                   jax.ShapeDtypeStruct((n_tok, FOX_HEADS * LANES), F32)] + extra.out_shape,
        name="fox_fwd", compiler_params=_params(2))(ii, jj, kk, zm, zm, zm, c_pairs, dead, *extra.arrays)


def _fox_bwd(zm, c_pairs, dead, d_out, lnorm, delta, tq, ride=None):
    n_tok = zm.shape[0]
    nq = n_tok // tq
    ii, jj, kk = _fox_steps(nq)
    n_steps = len(ii)
    n_ride = len(ride.arrays) if ride else 0

    def kern(ii_ref, jj_ref, kk_ref, q_ref, k_ref, v_ref, ck_ref, dead_ref, do_ref, ln_ref, dl_ref, *more):
        ride_in, ride_out = more[:n_ride], more[n_ride + 5:2 * n_ride + 5]
        dq_ref, dk_out, dv_out, cs_ref, rs_ref = more[n_ride:n_ride + 5]
        qs_ref, qo_ref, dos_ref, dq_acc, dk_ref, dv_ref = more[2 * n_ride + 5:2 * n_ride + 11]
        step = pl.program_id(1)
        j, kind = jj_ref[step], kk_ref[step]
        lo = lax.broadcasted_iota(jnp.int32, (2 * tq, LANES), 1) < HEAD_DIM
        if ride:
            ride.at_first_step((FOX_HEADS // 2, n_steps), ride_in, ride_out, more[2 * n_ride + 11:])

        @pl.when(step == 0)
        def _():
            dk_ref[...] = jnp.zeros_like(dk_ref)
            dv_ref[...] = jnp.zeros_like(dv_ref)
            cs_ref[...] = jnp.zeros_like(cs_ref)

        @pl.when(j == 0)
        def _():
            q2, do2 = q_ref[...], do_ref[...]
            zq = jnp.zeros_like(q2)
            ones = jnp.ones((2 * tq, LANES), BF16)
            for a in range(2):
                half = lo if a == 0 else ~lo
                qa = jnp.where(half, q2, zq) * SCALE
                qs_ref[a] = qa
                qo_ref[a] = jnp.concatenate([qa, ones], axis=1)
                dos_ref[a] = jnp.where(half, do2, zq)
            dq_acc[...] = jnp.zeros(dq_acc.shape, F32)

        def sweep(tiles, heads):
            k_ones = jnp.concatenate([k_ref[...], jnp.ones((FOX_KEY_BLOCKS * tq, LANES), BF16)], axis=1)
            sums = {}
            for sub, k0, k1, diagonal in tiles:
                rows, keys, n_keys = slice(sub * tq, (sub + 1) * tq), slice(k0 * tq, k1 * tq), (k1 - k0) * tq
                part = sums.setdefault((k0, k1), [0.0, 0.0, 0.0, 0.0])
                for a in heads:
                    t = _dot_nt(qs_ref[a, rows], k_ref[keys, :]) - ck_ref[a:a + 1, keys]
                    if diagonal is not None:
                        t = _causal(t, diagonal * tq)
                    prob = jnp.exp(t - _lane_tile(ln_ref[rows, a * LANES:(a + 1) * LANES], n_keys))
                    dp = _dot_nt(dos_ref[a, rows], v_ref[keys, :])
                    ds = (prob * (dp - _lane_tile(dl_ref[rows, a * LANES:(a + 1) * LANES], n_keys))).astype(BF16)
                    dq_acc[a, rows] += _dot(ds, k_ones[keys])
                    dk_cs = _dot_tn(ds, qo_ref[a, rows])
                    part[0] = part[0] + dk_cs[:, :LANES]
                    part[1] = part[1] + _dot_tn(prob.astype(BF16), dos_ref[a, rows])
                    part[2 + a] = part[2 + a] + dk_cs[:, LANES:]
            for (k0, k1), (dk, dv, cs0, cs1) in sums.items():
                keys = pl.ds(pl.multiple_of((FOX_KEY_BLOCKS * j + k0) * tq, tq), (k1 - k0) * tq)
                dk_ref[keys, :] += dk
                cs_ref[keys, :] += jnp.where(_low_half((k1 - k0) * tq), cs0, cs1)
                dv_ref[keys, :] += dv

        _fox_dispatch(sweep, kind, dead_ref, 2 * pl.program_id(0), ii_ref[step] * nq + FOX_KEY_BLOCKS * j)

        @pl.when(kind != 0)
        def _():
            dq_ref[...] = (jnp.where(lo, dq_acc[0, :, :LANES], dq_acc[1, :, :LANES]) * SCALE).astype(BF16)
            rs_ref[...] = jnp.where(lo, dq_acc[0, :, LANES:], dq_acc[1, :, LANES:])

        @pl.when(step == n_steps - 1)
        def _():
            dk_out[...] = dk_ref[...].astype(BF16)
            dv_out[...] = dv_ref[...].astype(BF16)

        if ride:
            ride.at_last_step((FOX_HEADS // 2, n_steps), ride_in, ride_out, more[2 * n_ride + 11:])

    blk, kblk = (2 * tq, LANES), (FOX_KEY_BLOCKS * tq, LANES)
    by_i = lambda col: (lambda hp, s, ii, jj, kk: (ii[s], col + hp))
    by_j = lambda col: (lambda hp, s, ii, jj, kk: (jj[s], col + hp))
    resident = pl.BlockSpec(blk, by_i(0))
    stat = pl.BlockSpec((2 * tq, 2 * LANES), by_i(0))
    whole = pl.BlockSpec((n_tok, LANES), lambda hp, s, ii, jj, kk: (0, hp))
    extra = ride if ride else _NO_RIDE
    grid_spec = pltpu.PrefetchScalarGridSpec(
        num_scalar_prefetch=3, grid=(FOX_HEADS // 2, n_steps),
        in_specs=[pl.BlockSpec(blk, by_i(Q_COL)), pl.BlockSpec(kblk, by_j(K_COL)), pl.BlockSpec(kblk, by_j(V_COL)),
                  pl.BlockSpec((None, 2, kblk[0]), lambda hp, s, ii, jj, kk: (hp, 0, jj[s])),
                  pl.BlockSpec(memory_space=pltpu.SMEM), resident, stat, stat] + extra.in_specs,
        out_specs=[resident, whole, whole, whole, resident] + extra.out_specs,
        scratch_shapes=[pltpu.VMEM((2, 2 * tq, LANES), BF16), pltpu.VMEM((2, 2 * tq, 2 * LANES), BF16),
                        pltpu.VMEM((2, 2 * tq, LANES), BF16), pltpu.VMEM((2, 2 * tq, 2 * LANES), F32),
                        pltpu.VMEM((n_tok, LANES), F32), pltpu.VMEM((n_tok, LANES), F32)] + extra.scratch)
    wide = lambda dt: jax.ShapeDtypeStruct((n_tok, 4 * LANES), dt)
    return pl.pallas_call(
        kern, grid_spec=grid_spec, name="fox_bwd",
        out_shape=[wide(BF16), wide(BF16), wide(BF16), wide(F32), wide(F32)] + extra.out_shape,
        compiler_params=_params(2, FOX_BWD_VMEM))(ii, jj, kk, zm, zm, zm, c_pairs, dead, d_out, lnorm, delta,
                                                  *extra.arrays)


def _all_gather(shards):
    n_w = len(shards)

    def kern(*refs):
        x_refs, out_refs = refs[:n_w], refs[n_w:2 * n_w]
        send_sems, recv_sems, local_sems = refs[2 * n_w:]
        x, y, c = _my_pos()
        me, sibling = (x, y, c), (x, y, 1 - c)
        chips = [(1 - x, y), (x, 1 - y), (1 - x, 1 - y)]

        def slot(w, px, py, pc):
            return out_refs[w].at[4 * px + 2 * py + pc]

        def copy(w, k, block, to, src=None):
            return pltpu.make_async_remote_copy(
                src_ref=slot(w, *block) if src is None else src, dst_ref=slot(w, *block),
                send_sem=send_sems.at[7 * w + k], recv_sem=recv_sems.at[7 * w + k], device_id=to, device_id_type=MESH)

        local, started = [], []
        for w in range(n_w):
            mine = pltpu.make_async_copy(x_refs[w], slot(w, *me), local_sems.at[w])
            mine.start()
            local.append(mine)
            first = [copy(w, 0, me, sibling, src=x_refs[w])]
            first += [copy(w, 1 + k, me, (*chip, c), src=x_refs[w]) for k, chip in enumerate(chips)]
            for cp in first:
                cp.start()
            started += first
        for k, chip in enumerate(chips):
            for w in range(n_w):
                copy(w, 1 + k, (*chip, c), me).wait_recv()
                passed = copy(w, 4 + k, (*chip, c), sibling)
                passed.start()
                started.append(passed)
        for w in range(n_w):
            copy(w, 0, sibling, me).wait_recv()
            for k, chip in enumerate(chips):
                copy(w, 4 + k, (*chip, 1 - c), me).wait_recv()
        for cp in started:
            cp.wait_send()
        for cp in local:
            cp.wait()

    any_spec = pl.BlockSpec(memory_space=pl.ANY)
    return pl.pallas_call(
        kern, out_shape=[jax.ShapeDtypeStruct((N_DEV,) + s.shape, s.dtype) for s in shards],
        in_specs=[any_spec] * n_w, out_specs=[any_spec] * n_w,
        scratch_shapes=[pltpu.SemaphoreType.DMA((7 * n_w,)), pltpu.SemaphoreType.DMA((7 * n_w,)),
                        pltpu.SemaphoreType.DMA((n_w,))],
        name="weight_all_gather")(*shards)


def _small_exchange(small):
    def kern(s_ref, sall_ref, *sems):
        copies = _gather_copies([s_ref], [sall_ref], *sems)
        _start_copies(*copies)
        _finish_copies(*copies)

    any_spec = pl.BlockSpec(memory_space=pl.ANY)
    return pl.pallas_call(
        kern, out_shape=jax.ShapeDtypeStruct((N_DEV,) + small.shape, small.dtype), in_specs=[any_spec],
        out_specs=any_spec, scratch_shapes=_exchange_scratch(1), name="small_grad_exchange")(small)


ADAMW_BLOCK_BYTES = 2 * 1024 * 1024


def _adamw(parts, w, m, v, name):
    n_parts, n_rows, n_cols = parts.shape
    limit = max(8, ADAMW_BLOCK_BYTES // (n_parts * n_cols * parts.dtype.itemsize))
    tr = max(t for t in range(8, n_rows + 1, 8) if n_rows % t == 0 and t <= limit)

    def kern(p_ref, w_ref, m_ref, v_ref, g_out, d_out, m_out, v_out):
        g = p_ref[0].astype(F32)
        for k in range(1, n_parts):
            g = g + p_ref[k].astype(F32)
        m_new = ADAM_B1 * m_ref[...] + (1.0 - ADAM_B1) * g
        v_new = ADAM_B2 * v_ref[...] + (1.0 - ADAM_B2) * jnp.square(g)
        m_hat = m_new / (1.0 - ADAM_B1 ** ADAM_STEP)
        v_hat = v_new / (1.0 - ADAM_B2 ** ADAM_STEP)
        g_out[...] = g
        d_out[...] = -ADAM_LR * (m_hat / (jnp.sqrt(v_hat) + ADAM_EPS) + ADAM_WD * w_ref[...])
        m_out[...] = m_new
        v_out[...] = v_new

    row = pl.BlockSpec((tr, n_cols), lambda i: (i, 0))
    out = jax.ShapeDtypeStruct((n_rows, n_cols), F32)
    return pl.pallas_call(
        kern, grid=(n_rows // tr,),
        in_specs=[pl.BlockSpec((n_parts, tr, n_cols), lambda i: (0, i, 0)), row, row, row],
        out_specs=[row, row, row, row], out_shape=[out, out, out, out], name=name,
        compiler_params=_params(1))(parts, w, m, v)


SHARDED = {
    "w_in": ((D_MODEL, D_IN), 1), "w_br_swa": ((512, D_MODEL), 1), "w_br_fox": ((512, D_MODEL), 1),
    "w_mix_out": ((D_MODEL, D_MODEL), 0), "w_ff1": ((D_MODEL, D_FF), 1), "w_ff2": ((D_FF, D_MODEL), 0),
    "w_ple_gate": ((D_MODEL, D_MODEL), 0), "w_ple_proj": ((PLE_DIM, D_MODEL), 1),
}
W_IN_SHARD = D_IN // N_DEV
W_IN_PAD = 640
SMALL = ("g_mix", "g_mlp", "g_ple", "g_final", "b_forget", "swa_sinks")
SMALL_COLS = 1024


def _wire_shard(name, a):
    a = a.reshape(a.shape[-2:])
    return jnp.pad(a, ((0, 0), (0, W_IN_PAD - W_IN_SHARD))) if name == "w_in" else a


def _from_wire(name, a):
    return (a[:, :W_IN_SHARD] if name == "w_in" else a)[None]


def _w_all_from_wire(stacked):
    w_in = jnp.concatenate([stacked[d][:, :W_IN_SHARD] for d in range(N_DEV)], axis=1)
    fpad = jnp.zeros((D_MODEL, N_FPAD - FOX_HEADS), stacked.dtype)
    return jnp.concatenate([w_in[:, :N_MAIN + FOX_HEADS], fpad, w_in[:, N_MAIN + FOX_HEADS:]], axis=1)


def _dw_in_to_wire(dw_all):
    dw_in = jnp.concatenate([dw_all[:, :N_MAIN + FOX_HEADS], dw_all[:, N_MAIN + N_FPAD:]], axis=1)
    pad = jnp.zeros((D_MODEL, W_IN_PAD - W_IN_SHARD), dw_all.dtype)
    return jnp.stack([jnp.concatenate([dw_in[:, d * W_IN_SHARD:(d + 1) * W_IN_SHARD], pad], axis=1)
                      for d in range(N_DEV)])


def _pack_small(vals, scalar=None):
    rows = [jnp.pad(vals[n].reshape(-1), (0, SMALL_COLS - vals[n].size)) for n in SMALL]
    if scalar is not None:
        rows.append(jnp.pad(scalar.reshape(1), (0, SMALL_COLS - 1)))
    rows += [jnp.zeros((SMALL_COLS,), F32)] * (8 - len(rows))
    return jnp.stack(rows)


def _unpack_small(slab, like):
    return {n: slab[r, :like[n].size].reshape(like[n].shape) for r, n in enumerate(SMALL)}


def _local_step(x, p, tgt, w, small, tm, tq, ts, late_shards=None):
    n_tok = x.shape[0]
    row = lambda v: v.reshape(1, -1)
    g_mix, g_mlp, g_ple, g_fin = row(small["g_mix"]), row(small["g_mlp"]), row(small["g_ple"]), row(small["g_final"])
    sinks = small["swa_sinks"].reshape(-1)
    b_col = small["b_forget"].reshape(FOX_HEADS, 1)

    assert tm == tq
    u1, zm, zfg, zf, nrm = _in_proj(x, g_mix, w["w_all"], tm)
    f_t = zf[:, :FOX_HEADS].T
    c_pairs = _decay_cumsum(f_t, b_col).reshape(FOX_HEADS // 2, 2, n_tok)
    attn_a, lse_a = _swa_fwd(zm, sinks)
    dead = _fox_dead_steps(nrm, c_pairs, tq)
    if late_shards is None:
        attn_b, ln_b = _fox_fwd(zm, c_pairs, dead, tq)
    else:
        attn_b, ln_b, *late = _fox_fwd(zm, c_pairs, dead, tq, _gather_ride(list(late_shards.values())))
        w = {**w, **_gathered_to_local(dict(zip(late_shards, late)))}
    ya, yb, mixed, h1, u2, a, r, h2 = _mix_ffn_fwd(attn_a, attn_b, zfg, x, w["w_br_swa"], w["w_br_fox"],
                                                   w["w_mix_out"], g_mlp, w["w_ff1"], w["w_ff2"], tm // 2)

    dlg, dpp, u3, dh2, dh2b, da, loss_acc, dgf, dgp = _head_ffn_bwd(
        h2, p, tgt, a, g_ple, w["w_ple_gate"], w["w_ple_proj"], g_fin, w["w_ff2"], tm // 2)
    dh1, dh1b, dgl, dya, dyb, daa, dab, delta_b, dgm = _ffn_bwd_b(
        da, dh2, h1, ya, yb, zfg, attn_b, w["w_ff1"], g_mlp, w["w_mix_out"], w["w_br_swa"], w["w_br_fox"], tm // 2)
    dq_a, dkp, dkc, dvp, dvc, dsk = _swa_bwd(zm, sinks, daa, attn_a, lse_a)
    dw = {
        "w_br_swa": _matmul_tn(attn_a, dya, "dw_br_swa", ts, stack_cols=D_MODEL // N_DEV),
        "w_br_fox": _matmul_tn(attn_b, dyb, "dw_br_fox", ts, stack_cols=D_MODEL // N_DEV),
        "w_mix_out": _matmul_tn(mixed, dh1b, "dw_mix_out", ts),
        "w_ff1": _matmul_tn(u2, da, "dw_ff1", ts, stack_cols=D_FF // N_DEV),
        "w_ff2": _matmul_tn(r, dh2b, "dw_ff2", ts),
        "w_ple_gate": _matmul_tn(u3, dlg, "dw_ple_gate", ts),
        "w_ple_proj": _matmul_tn(p, dpp, "dw_ple_proj", ts, stack_cols=D_MODEL // N_DEV),
    }
    if late_shards is None:
        dq_b, dk_b, dv_b, cs, rs = _fox_bwd(zm, c_pairs, dead, dab, ln_b, delta_b, tq)
        late_parts = None
    else:
        wire = _local_to_wire(dw)
        dq_b, dk_b, dv_b, cs, rs, *parts = _fox_bwd(zm, c_pairs, dead, dab, ln_b, delta_b, tq,
                                                    _scatter_ride([wire[n] for n in late_shards]))
        late_parts = dict(zip(late_shards, parts))

    up = lambda t: jnp.concatenate([t[SWA_BLOCK:], jnp.zeros((SWA_BLOCK, LANES), F32)], axis=0)
    dk_a, dv_a = dkc + up(dkp), dvc + up(dvp)
    df_t, db = _decay_bwd(cs, rs, f_t, b_col)
    df = jnp.pad(df_t.T, ((0, 0), (0, N_FPAD - FOX_HEADS)))
    dz = jnp.concatenate([dq_a, dk_a.astype(BF16), dv_a.astype(BF16), dq_b, dk_b, dv_b,
                          df.astype(BF16), dgl], axis=1)
    dw["w_all"] = _matmul_tn(u1, dz, "dw_in", ts)
    if late_shards is None:
        dx, dgx = _in_proj_bwd(dz, dh1, x, w["w_all"], g_mix, tm)
    else:
        dx, dgx, late_parts["w_in"] = _in_proj_bwd(dz, dh1, x, w["w_all"], g_mix, tm,
                                                   _scatter_ride([_dw_in_to_wire(dw["w_all"])]))
    dsmall = {"g_mix": dgx[0], "g_mlp": dgm[0], "g_ple": dgp[0], "g_final": dgf[0],
              "b_forget": db[:, 0], "swa_sinks": dsk[:, 0]}
    return loss_acc[0, 0], dx, dw, dsmall, late_parts


_ROWS = lambda t: t.reshape(-1, t.shape[-1])
_BY_ROWS = lambda t: t.reshape(N_DEV, t.shape[0] // N_DEV, t.shape[1])
_SAME = lambda t: t
LOCAL_LAYOUT = {
    "w_in": ("w_all", _w_all_from_wire, _dw_in_to_wire), "w_br_swa": ("w_br_swa", _SAME, _SAME),
    "w_br_fox": ("w_br_fox", _SAME, _SAME), "w_mix_out": ("w_mix_out", _ROWS, _BY_ROWS),
    "w_ff1": ("w_ff1", _SAME, _SAME), "w_ff2": ("w_ff2", _SAME, _BY_ROWS),
    "w_ple_gate": ("w_ple_gate", _ROWS, _BY_ROWS), "w_ple_proj": ("w_ple_proj", _SAME, _SAME),
}


def _gathered_to_local(g):
    return {LOCAL_LAYOUT[n][0]: LOCAL_LAYOUT[n][1](t) for n, t in g.items()}


def _local_to_wire(dw):
    names = {local: n for n, (local, _, _) in LOCAL_LAYOUT.items()}
    return {names[local]: LOCAL_LAYOUT[names[local]][2](t) for local, t in dw.items()}


def kernel(x, p, g_mix, w_in, b_forget, swa_sinks, w_br_swa, w_br_fox, w_mix_out, g_mlp, w_ff1, w_ff2, g_ple, w_ple_gate, w_ple_proj, g_final, loss_target, m_g_mix, m_w_in, m_b_forget, m_swa_sinks, m_w_br_swa, m_w_br_fox, m_w_mix_out, m_g_mlp, m_w_ff1, m_w_ff2, m_g_ple, m_w_ple_gate, m_w_ple_proj, m_g_final, v_g_mix, v_w_in, v_b_forget, v_swa_sinks, v_w_br_swa, v_w_br_fox, v_w_mix_out, v_g_mlp, v_w_ff1, v_w_ff2, v_g_ple, v_w_ple_gate, v_w_ple_proj, v_g_final):
    given = dict(g_mix=g_mix, w_in=w_in, b_forget=b_forget, swa_sinks=swa_sinks, w_br_swa=w_br_swa, w_br_fox=w_br_fox,
                 w_mix_out=w_mix_out, g_mlp=g_mlp, w_ff1=w_ff1, w_ff2=w_ff2, g_ple=g_ple, w_ple_gate=w_ple_gate,
                 w_ple_proj=w_ple_proj, g_final=g_final)
    mom = dict(g_mix=m_g_mix, w_in=m_w_in, b_forget=m_b_forget, swa_sinks=m_swa_sinks, w_br_swa=m_w_br_swa,
               w_br_fox=m_w_br_fox, w_mix_out=m_w_mix_out, g_mlp=m_g_mlp, w_ff1=m_w_ff1, w_ff2=m_w_ff2, g_ple=m_g_ple,
               w_ple_gate=m_w_ple_gate, w_ple_proj=m_w_ple_proj, g_final=m_g_final)
    vel = dict(g_mix=v_g_mix, w_in=v_w_in, b_forget=v_b_forget, swa_sinks=v_swa_sinks, w_br_swa=v_w_br_swa,
               w_br_fox=v_w_br_fox, w_mix_out=v_w_mix_out, g_mlp=v_g_mlp, w_ff1=v_w_ff1, w_ff2=v_w_ff2, g_ple=v_g_ple,
               w_ple_gate=v_w_ple_gate, w_ple_proj=v_w_ple_proj, g_final=v_g_final)
    names = list(given)
    sharded = list(SHARDED)

    w_wire = {n: _wire_shard(n, given[n]) for n in sharded}
    late = [n for n in sharded if n != "w_in"]
    gathered = _all_gather([w_wire["w_in"].astype(BF16)])
    local_w = _gathered_to_local({"w_in": gathered[0]})
    small = {n: given[n].reshape(-1) for n in SMALL}

    n_tok = x.shape[1]
    tile = min(TOKEN_TILE, n_tok // 4)
    loss_part, dx, dw, dsmall, parts = _local_step(
        x[0], p[0, 0], loss_target[0], local_w, small, tm=tile, tq=tile, ts=min(DW_TOKENS_PER_STEP, n_tok // 4),
        late_shards={n: w_wire[n].astype(BF16) for n in late})
    small_all = _small_exchange(_pack_small(dsmall, loss_part))

    res = {}
    for n in sharded:
        part = parts[n]
        flat = part.reshape(N_DEV, -1, part.shape[-1])
        outs = _adamw(flat, w_wire[n], _wire_shard(n, mom[n]), _wire_shard(n, vel[n]), "adamw_" + n)
        res[n] = [_from_wire(n, o) for o in outs]
    outs_s = _adamw(small_all, _pack_small(small), _pack_small({n: mom[n] for n in SMALL}),
                    _pack_small({n: vel[n] for n in SMALL}), "adamw_small")
    small_res = [_unpack_small(o, given) for o in outs_s]
    loss = outs_s[0][len(SMALL), 0]

    groups = [[res[n][k] if n in res else small_res[k][n] for n in names] for k in range(4)]
    return (loss, dx[None], *groups[0], *groups[1], *groups[2], *groups[3])
```

```python
import numpy as np
import jax
import jax.numpy as jnp
from jax import lax
from jax.experimental import pallas as pl
from jax.experimental.pallas import tpu as pltpu

F32 = jnp.float32
BF16 = jnp.bfloat16

D_MODEL = 1024
HEAD_DIM = 64
SWA_HEADS = 8
FOX_HEADS = 8
CHUNK_SHIFT = 6
SWA_BLOCK = 128
WINDOW_CHUNKS = 2
D_FF = 4096
PLE_DIM = 256
RMS_EPS = 1e-6
N_MAIN = 2304
N_FPAD = 128
N_GATE = 2048
D_IN = N_MAIN + FOX_HEADS + N_GATE
SCALE = HEAD_DIM ** -0.5
NEG = -1e30

ADAM_LR = 0.001
ADAM_B1 = 0.9
ADAM_B2 = 0.999
ADAM_EPS = 1e-08
ADAM_WD = 0.01
ADAM_STEP = 10

N_DEV = 8
TOKEN_TILE = 512
DW_TOKENS_PER_STEP = 2048
LANES = 128
V7X_VMEM_BYTES = 64 * 1024 * 1024
VMEM_LIMIT = V7X_VMEM_BYTES * 3 // 4
FOX_BWD_VMEM = V7X_VMEM_BYTES * 7 // 8
MESH = pl.DeviceIdType.MESH

_NT = (((1,), (1,)), ((), ()))
_TN = (((0,), (0,)), ((), ()))


def _params(n_grid, vmem_limit=VMEM_LIMIT):
    return pltpu.CompilerParams(dimension_semantics=("arbitrary",) * n_grid, vmem_limit_bytes=vmem_limit)


def _chunks(n, step):
    return [(s, min(step, n - s)) for s in range(0, n, step)]


def _sigmoid(x):
    return 1.0 / (1.0 + jnp.exp(-x))


def _dot(a, b):
    return jnp.dot(a, b, preferred_element_type=F32)


def _dot_nt(a, b):
    return lax.dot_general(a, b, _NT, preferred_element_type=F32)


def _dot_tn(a, b):
    return lax.dot_general(a, b, _TN, preferred_element_type=F32)


def _lane_concat(stacked_ref):
    return jnp.concatenate([stacked_ref[d] for d in range(N_DEV)], axis=1)


def _rms(h):
    return lax.rsqrt(jnp.mean(h * h, axis=-1, keepdims=True) + RMS_EPS)


def _rms_bwd(h, g, du):
    rs = _rms(h)
    n = h * rs
    dn = du * g
    dh = rs * (dn - n * jnp.mean(dn * n, axis=-1, keepdims=True))
    return dh, jnp.sum(du * n, axis=0, keepdims=True)


def _acc_rows(ref, i, row):
    @pl.when(i == 0)
    def _():
        ref[...] = jnp.zeros_like(ref)
    ref[...] += jnp.broadcast_to(row, ref.shape)


def _row_call(body, name, n_rows, tm, row_ins, const_ins, row_outs, acc_outs, ride=None, tile_outs=()):
    row_outs = list(row_outs)
    n_ri, n_ci, n_ro, n_ao = len(row_ins), len(const_ins), len(row_outs) + len(tile_outs), len(acc_outs)
    extra = ride if ride else _NO_RIDE
    n_ride = len(extra.arrays)
    grid = (n_rows // tm,)

    def kern(*refs):
        i = pl.program_id(0)
        ins, refs = refs[:n_ri + n_ci], refs[n_ri + n_ci:]
        ride_in, refs = refs[:n_ride], refs[n_ride:]
        outs, refs = refs[:n_ro + n_ao], refs[n_ro + n_ao:]
        ride_out, sems = refs[:n_ride], refs[n_ride:]
        if ride:
            ride.at_first_step(grid, ride_in, ride_out, sems)
        body(i, ins[:n_ri], ins[n_ri:], outs[:n_ro], outs[n_ro:])
        if ride:
            ride.at_last_step(grid, ride_in, ride_out, sems)

    def whole(a):
        zeros = (0,) * a.ndim
        return pl.BlockSpec(a.shape, lambda i: zeros, pipeline_mode=pl.Buffered(1))

    in_specs = [pl.BlockSpec((tm, a.shape[1]), lambda i: (i, 0)) for a in row_ins]
    in_specs += [whole(a) for a in const_ins] + extra.in_specs
    out_specs = [pl.BlockSpec((tm, c), lambda i: (i, 0)) for c, _ in row_outs]
    out_specs += [pl.BlockSpec((8, c), lambda i: (i, 0)) for c in tile_outs]
    out_specs += [pl.BlockSpec((8, c), lambda i: (0, 0)) for c in acc_outs] + extra.out_specs
    out_shape = [jax.ShapeDtypeStruct((n_rows, c), dt) for c, dt in row_outs]
    out_shape += [jax.ShapeDtypeStruct((8 * grid[0], c), F32) for c in tile_outs]
    out_shape += [jax.ShapeDtypeStruct((8, c), F32) for c in acc_outs] + extra.out_shape
    return pl.pallas_call(kern, grid=grid, in_specs=in_specs, out_specs=out_specs, out_shape=out_shape,
                          scratch_shapes=extra.scratch, name=name,
                          compiler_params=_params(1))(*row_ins, *const_ins, *extra.arrays)


def _in_proj(x, g_mix, w_all, tm):
    def body(i, ins, consts, outs, accs):
        x_ref, = ins
        g_ref, w_ref = consts
        u_ref, zm_ref, zfg_ref, zf_ref, nrm_ref = outs
        xv = x_ref[...]
        u = ((xv * _rms(xv)) * g_ref[...]).astype(BF16)
        u_ref[...] = u
        for s, n in _chunks(N_MAIN, 768):
            zm_ref[:, s:s + n] = _dot(u, w_ref[:, s:s + n]).astype(BF16)
        for s, n in _chunks(N_FPAD + N_GATE, 512):
            zfg_ref[:, s:s + n] = _dot(u, w_ref[:, N_MAIN + s:N_MAIN + s + n])
        zf_ref[...] = zfg_ref[:, :N_FPAD]
        lane = lax.broadcasted_iota(jnp.int32, (4 * LANES, LANES), 0)
        head = lax.broadcasted_iota(jnp.int32, (4 * LANES, LANES), 1)
        pick = (lane // HEAD_DIM == head).astype(BF16)
        tq_, tk_ = (zm_ref[:, col * LANES:(col + 4) * LANES].astype(F32) for col in (Q_COL, K_COL))
        rows = [jnp.max(_dot((t * t).astype(BF16), pick), axis=0, keepdims=True) for t in (tq_, tk_)]
        rows.append(jnp.min(_dot((tq_ * tk_).astype(BF16), pick), axis=0, keepdims=True))
        nrm_ref[...] = jnp.concatenate(rows + [jnp.zeros((5, LANES), F32)], axis=0)

    *outs, nrm = _row_call(body, "in_proj", x.shape[0], tm, [x], [g_mix, w_all],
                           [(D_MODEL, BF16), (N_MAIN, BF16), (N_FPAD + N_GATE, F32), (N_FPAD, F32)], [],
                           tile_outs=[LANES])
    return (*outs, nrm)


def _mix_ffn_fwd(attn_a, attn_b, zfg, x, w_sa, w_fo, w_mo, g_mlp, w1s, w2s, tm):
    ch = D_FF // N_DEV

    def body(i, ins, consts, outs, accs):
        aa_ref, ab_ref, zfg_ref, x_ref = ins
        wsa_ref, wfo_ref, wmo_ref, g_ref, w1_ref, w2_ref = consts
        ya_ref, yb_ref, mx_ref, h1_ref, u2_ref, a_ref, r_ref, h2_ref = outs
        ya = _dot(aa_ref[...], _lane_concat(wsa_ref))
        yb = _dot(ab_ref[...], _lane_concat(wfo_ref))
        g0 = _sigmoid(zfg_ref[:, N_FPAD:N_FPAD + D_MODEL])
        g1 = _sigmoid(zfg_ref[:, N_FPAD + D_MODEL:N_FPAD + 2 * D_MODEL])
        mixed = (g0 * ya + g1 * yb).astype(BF16)
        ya_ref[...] = ya.astype(BF16)
        yb_ref[...] = yb.astype(BF16)
        mx_ref[...] = mixed
        h1 = x_ref[...] + _dot(mixed, wmo_ref[...])
        h1_ref[...] = h1
        u = ((h1 * _rms(h1)) * g_ref[...]).astype(BF16)
        u2_ref[...] = u
        acc = h1
        for c in range(N_DEV):
            a = _dot(u, w1_ref[c])
            a_ref[:, c * ch:(c + 1) * ch] = a.astype(BF16)
            r = jnp.square(jnp.maximum(a, 0.0)).astype(BF16)
            r_ref[:, c * ch:(c + 1) * ch] = r
            acc = acc + _dot(r, w2_ref[c])
        h2_ref[...] = acc

    return _row_call(body, "mix_ffn_fwd", x.shape[0], tm, [attn_a, attn_b, zfg, x],
                     [w_sa, w_fo, w_mo, g_mlp, w1s, w2s],
                     [(D_MODEL, BF16), (D_MODEL, BF16), (D_MODEL, BF16), (D_MODEL, F32), (D_MODEL, BF16),
                      (D_FF, BF16), (D_FF, BF16), (D_MODEL, F32)], [])


def _head_ffn_bwd(h2, p, tgt, a, g_ple, w_pg, w_pp, g_fin, w2s, tm):
    ch = D_FF // N_DEV

    def body(i, ins, consts, outs, accs):
        h2_ref, p_ref, t_ref, a_ref = ins
        gp_ref, wpg_ref, wpp_ref, gf_ref, w2_ref = consts
        dlg_ref, dpp_ref, u3_ref, dh2_ref, dh2b_ref, da_ref = outs
        loss_ref, dgf_ref, dgp_ref = accs
        h2 = h2_ref[...]
        gp = gp_ref[...]
        u3 = ((h2 * _rms(h2)) * gp).astype(BF16)
        u3_ref[...] = u3
        pg = _sigmoid(_dot(u3, wpg_ref[...]))
        pp = _dot(p_ref[...].astype(BF16), _lane_concat(wpp_ref))
        h3 = h2 + pg * pp
        rs3 = _rms(h3)
        n3 = h3 * rs3
        gf = gf_ref[...]
        err = n3 * gf - t_ref[...]
        row_loss = 0.5 * jnp.mean(err * err, axis=-1, keepdims=True)
        _acc_rows(loss_ref, i, jnp.broadcast_to(jnp.sum(row_loss, axis=0, keepdims=True), (1, LANES)))
        dy = err * (1.0 / D_MODEL)
        _acc_rows(dgf_ref, i, jnp.sum(dy * n3, axis=0, keepdims=True))
        dn = dy * gf
        dh3 = rs3 * (dn - n3 * jnp.mean(dn * n3, axis=-1, keepdims=True))
        dpp_ref[...] = (dh3 * pg).astype(BF16)
        dlg = ((dh3 * pp) * pg * (1.0 - pg)).astype(BF16)
        dlg_ref[...] = dlg
        dh, dg = _rms_bwd(h2, gp, _dot_nt(dlg, wpg_ref[...]))
        _acc_rows(dgp_ref, i, dg)
        dh2 = dh3 + dh
        dh2_ref[...] = dh2
        dh2b = dh2.astype(BF16)
        dh2b_ref[...] = dh2b
        for c in range(N_DEV):
            dr = _dot_nt(dh2b, w2_ref[c])
            av = a_ref[:, c * ch:(c + 1) * ch].astype(F32)
            da_ref[:, c * ch:(c + 1) * ch] = (dr * (2.0 * jnp.maximum(av, 0.0))).astype(BF16)

    return _row_call(body, "head_ffn_bwd", h2.shape[0], tm, [h2, p, tgt, a], [g_ple, w_pg, w_pp, g_fin, w2s],
                     [(D_MODEL, BF16), (D_MODEL, BF16), (D_MODEL, BF16), (D_MODEL, F32), (D_MODEL, BF16),
                      (D_FF, BF16)], [LANES, D_MODEL, D_MODEL])


def _ffn_bwd_b(da, dh2, h1, ya, yb, zfg, attn_b, w1s, g_mlp, w_mo, w_sa, w_fo, tm):
    ch = D_FF // N_DEV

    def body(i, ins, consts, outs, accs):
        da_ref, dh2_ref, h1_ref, ya_ref, yb_ref, zfg_ref, ob_ref = ins
        w1_ref, gm_ref, wmo_ref, wsa_ref, wfo_ref = consts
        dh1_ref, dh1b_ref, dgl_ref, dya_ref, dyb_ref, daa_ref, dab_ref, dl_ref = outs
        dgm_ref, = accs
        du2 = _dot_nt(da_ref[:, 0:ch], w1_ref[0])
        for c in range(1, N_DEV):
            du2 = du2 + _dot_nt(da_ref[:, c * ch:(c + 1) * ch], w1_ref[c])
        dh, dg = _rms_bwd(h1_ref[...], gm_ref[...], du2)
        _acc_rows(dgm_ref, i, dg)
        dh1 = dh2_ref[...] + dh
        dh1_ref[...] = dh1
        dh1b = dh1.astype(BF16)
        dh1b_ref[...] = dh1b
        dmx = _dot_nt(dh1b, wmo_ref[...])
        g0 = _sigmoid(zfg_ref[:, N_FPAD:N_FPAD + D_MODEL])
        g1 = _sigmoid(zfg_ref[:, N_FPAD + D_MODEL:N_FPAD + 2 * D_MODEL])
        dya = (dmx * g0).astype(BF16)
        dyb = (dmx * g1).astype(BF16)
        dya_ref[...] = dya
        dyb_ref[...] = dyb
        dgl_ref[:, 0:D_MODEL] = ((dmx * ya_ref[...].astype(F32)) * g0 * (1.0 - g0)).astype(BF16)
        dgl_ref[:, D_MODEL:2 * D_MODEL] = ((dmx * yb_ref[...].astype(F32)) * g1 * (1.0 - g1)).astype(BF16)
        daa_ref[...] = _dot_nt(dya, _lane_concat(wsa_ref)).astype(BF16)
        dab = _dot_nt(dyb, _lane_concat(wfo_ref)).astype(BF16)
        dab_ref[...] = dab
        half_in = lax.broadcasted_iota(jnp.int32, (LANES, 2 * LANES), 0) // HEAD_DIM
        half_out = lax.broadcasted_iota(jnp.int32, (LANES, 2 * LANES), 1) // LANES
        pick = (half_in == half_out).astype(BF16)
        for pair in range(FOX_HEADS // 2):
            cols = slice(pair * LANES, (pair + 1) * LANES)
            prod = dab[:, cols].astype(F32) * ob_ref[:, cols].astype(F32)
            hi = prod.astype(BF16)
            lo_part = (prod - hi.astype(F32)).astype(BF16)
            dl_ref[:, 2 * pair * LANES:(2 * pair + 2) * LANES] = _dot(hi, pick) + _dot(lo_part, pick)

    half = D_MODEL // 2
    return _row_call(body, "ffn_bwd_b", h1.shape[0], tm, [da, dh2, h1, ya, yb, zfg, attn_b],
                     [w1s, g_mlp, w_mo, w_sa, w_fo],
                     [(D_MODEL, F32), (D_MODEL, BF16), (N_GATE, BF16), (D_MODEL, BF16), (D_MODEL, BF16),
                      (half, BF16), (half, BF16), (FOX_HEADS * LANES, F32)], [D_MODEL])


def _in_proj_bwd(dz, dh1, x, w_all, g_mix, tm, ride=None):
    def body(i, ins, consts, outs, accs):
        dz_ref, dh1_ref, x_ref = ins
        w_ref, g_ref = consts
        dx_ref, = outs
        dgx_ref, = accs
        du1 = _dot_nt(dz_ref[...], w_ref[...])
        dh, dg = _rms_bwd(x_ref[...], g_ref[...], du1)
        _acc_rows(dgx_ref, i, dg)
        dx_ref[...] = dh1_ref[...] + dh

    return _row_call(body, "in_proj_bwd", x.shape[0], tm, [dz, dh1, x], [w_all, g_mix],
                     [(D_MODEL, F32)], [D_MODEL], ride)


def _matmul_tn(a, b, name, ts, stack_cols=0):
    n_rows, ka = a.shape
    n = b.shape[1]
    tk = min(ka, 1024)
    tn = 896 if n % 1024 else 1024
    n_stack = tn // stack_cols if stack_cols else 0
    assert ka % tk == 0 and n % tn == 0 and n_rows % ts == 0 and (not stack_cols or tk == ka)
    n_steps = n_rows // ts

    def kern(a_ref, b_ref, o_ref, acc_ref):
        s = pl.program_id(2)

        @pl.when(s == 0)
        def _():
            acc_ref[...] = jnp.zeros_like(acc_ref)
        acc_ref[...] += _dot_tn(a_ref[...].astype(BF16), b_ref[...])

        @pl.when(s == n_steps - 1)
        def _():
            if stack_cols:
                for c in range(n_stack):
                    o_ref[c] = acc_ref[:, c * stack_cols:(c + 1) * stack_cols].astype(BF16)
            else:
                o_ref[...] = acc_ref[...].astype(BF16)

    if stack_cols:
        out_spec = pl.BlockSpec((n_stack, tk, stack_cols), lambda i, j, s: (j, 0, 0))
        out_shape = jax.ShapeDtypeStruct((n // stack_cols, ka, stack_cols), BF16)
    else:
        out_spec = pl.BlockSpec((tk, tn), lambda i, j, s: (i, j))
        out_shape = jax.ShapeDtypeStruct((ka, n), BF16)
    return pl.pallas_call(
        kern, grid=(ka // tk, n // tn, n_steps),
        in_specs=[pl.BlockSpec((ts, tk), lambda i, j, s: (s, i)), pl.BlockSpec((ts, tn), lambda i, j, s: (s, j))],
        out_specs=out_spec, out_shape=out_shape, scratch_shapes=[pltpu.VMEM((tk, tn), F32)], name=name,
        compiler_params=_params(3))(a, b)


SCAN_CHUNK = 512
BWD_SCAN_CHUNK = 1024


def _decay_cumsum(f_t, b_col):
    n_tok = f_t.shape[1]
    ch = min(SCAN_CHUNK, n_tok)

    def kern(f_ref, b_ref, c_ref):
        r = lax.broadcasted_iota(jnp.int32, (ch, ch), 0)
        c = lax.broadcasted_iota(jnp.int32, (ch, ch), 1)
        tri = (r <= c).astype(F32)
        carry = jnp.zeros((8, 1), F32)
        for k in range(n_tok // ch):
            xv = f_ref[:, k * ch:(k + 1) * ch] + b_ref[...]
            lf = jnp.minimum(xv, 0.0) - jnp.log(1.0 + jnp.exp(-jnp.abs(xv)))
            cs = jnp.dot(lf, tri, precision=lax.Precision.HIGHEST, preferred_element_type=F32) + carry
            c_ref[:, k * ch:(k + 1) * ch] = cs
            carry = cs[:, ch - 1:ch]

    return pl.pallas_call(kern, out_shape=jax.ShapeDtypeStruct((8, n_tok), F32), name="decay_cumsum",
                          compiler_params=_params(0))(f_t, b_col)


def _decay_bwd(cs, rs, f_t, b_col):
    n_tok = f_t.shape[1]
    ch = min(BWD_SCAN_CHUNK, n_tok)
    n_ch = n_tok // ch

    def kern(cs_ref, rs_ref, f_ref, b_ref, df_ref, db_ref, carry_ref):
        k = pl.program_id(0)

        @pl.when(k == 0)
        def _():
            carry_ref[...] = jnp.zeros_like(carry_ref)
            db_ref[...] = jnp.zeros_like(db_ref)

        r = lax.broadcasted_iota(jnp.int32, (ch, ch), 0)
        c = lax.broadcasted_iota(jnp.int32, (ch, ch), 1)
        tri = (r >= c).astype(F32)
        head = lax.broadcasted_iota(jnp.int32, (8, 4 * LANES), 0)
        lane = lax.broadcasted_iota(jnp.int32, (8, 4 * LANES), 1)
        pick = (lane == HEAD_DIM * head).astype(F32)
        dc = lax.dot_general(pick, rs_ref[...] - cs_ref[...], _NT, precision=lax.Precision.HIGHEST,
                             preferred_element_type=F32)
        rc = jnp.dot(dc, tri, precision=lax.Precision.HIGHEST, preferred_element_type=F32) + carry_ref[:, 0:1]
        carry_ref[...] = jnp.broadcast_to(rc[:, 0:1], carry_ref.shape)
        df = rc / (1.0 + jnp.exp(f_ref[...] + b_ref[...]))
        df_ref[...] = df
        db_ref[...] += jnp.broadcast_to(jnp.sum(df, axis=1, keepdims=True), db_ref.shape)

    back = lambda k: n_ch - 1 - k
    wide = pl.BlockSpec((ch, 4 * LANES), lambda k: (back(k), 0))
    row = pl.BlockSpec((8, ch), lambda k: (0, back(k)))
    return pl.pallas_call(
        kern, grid=(n_ch,),
        in_specs=[wide, wide, row, pl.BlockSpec((8, 1), lambda k: (0, 0))],
        out_specs=[row, pl.BlockSpec((8, LANES), lambda k: (0, 0))],
        out_shape=[jax.ShapeDtypeStruct((8, n_tok), F32), jax.ShapeDtypeStruct((8, LANES), F32)],
        scratch_shapes=[pltpu.VMEM((8, LANES), F32)], name="decay_bwd", compiler_params=_params(1))(cs, rs, f_t, b_col)


def _swa_bias_table():
    row = jnp.arange(SWA_BLOCK)[:, None] + SWA_BLOCK
    col = jnp.arange(2 * SWA_BLOCK)[None, :]
    cd = (row >> CHUNK_SHIFT) - (col >> CHUNK_SHIFT)
    band = (cd >= 0) & (cd <= WINDOW_CHUNKS)
    slopes = jnp.asarray([2.0 ** -(h + 1) for h in range(SWA_HEADS)], F32)
    bias = -slopes[:, None, None] * jnp.abs(row - col).astype(F32)[None]
    return jnp.stack([jnp.where(band & (col >= SWA_BLOCK), bias, NEG), jnp.where(band, bias, NEG)])


SWA_PER_STEP = 4


def _swap_halves(t):
    return pltpu.roll(t.astype(F32), HEAD_DIM, axis=1).astype(t.dtype)


def _swa_specs():
    blk, rows = SWA_BLOCK, SWA_PER_STEP * SWA_BLOCK
    q = pl.BlockSpec((rows, 4 * LANES), lambda n: (n, 0))
    before = lambda col: pl.BlockSpec((blk, LANES), lambda n: (jnp.maximum(SWA_PER_STEP * n - 1, 0), col))
    own = lambda col: pl.BlockSpec((rows, LANES), lambda n: (n, col))
    bias = pl.BlockSpec((2, SWA_HEADS, blk, 2 * blk), lambda n: (0, 0, 0, 0))
    return [q, before(4), own(4), before(5), own(5), bias]


def _swa_band(before_ref, own_ref):
    both = jnp.concatenate([before_ref[...], own_ref[...]], axis=0)
    return both, _swap_halves(both)


def _swa_bias(bias_ref, n, b):
    return bias_ref.at[jnp.minimum(n, 1)] if b == 0 else bias_ref.at[1]


SWA_GROUPS = ([h for h in range(SWA_HEADS) if h % 2 == h // 4], [h for h in range(SWA_HEADS) if h % 2 != h // 4])


def _stack_heads(ref, rows, heads, lo, mask_halves):
    tiles = []
    for h in heads:
        t = ref[rows, (h // 2) * LANES:(h // 2 + 1) * LANES]
        tiles.append(jnp.where(lo if h % 2 == 0 else ~lo, t, jnp.zeros_like(t)) if mask_halves else t)
    return jnp.concatenate(tiles, axis=0)


def _per_head_column(values, heads):
    return jnp.concatenate([jnp.full((SWA_BLOCK, 1), values(h), F32) for h in heads], axis=0)


def _swa_scores(q_ref, rows, kx, heads, lo, bias):
    qa = _stack_heads(q_ref, rows, heads, lo, True) * SCALE
    return qa, _dot_nt(qa, kx) + jnp.concatenate([bias[h] for h in heads], axis=0)


def _swa_fwd(zm, sinks):
    n_tok = zm.shape[0]
    blk, step_rows = SWA_BLOCK, SWA_PER_STEP * SWA_BLOCK

    def kern(q_ref, kp_ref, kc_ref, vp_ref, vc_ref, bias_ref, sink_ref, o_ref, lse_ref):
        n = pl.program_id(0)
        (k_all, k_all_sw), (v_all, v_all_sw) = _swa_band(kp_ref, kc_ref), _swa_band(vp_ref, vc_ref)
        lane = lax.broadcasted_iota(jnp.int32, (blk, LANES), 1)
        lo = lane < HEAD_DIM
        for b in range(SWA_PER_STEP):
            rows, band = slice(b * blk, (b + 1) * blk), slice(b * blk, (b + 2) * blk)
            bias = _swa_bias(bias_ref, n, b)
            lse_t = jnp.zeros((blk, LANES), F32)
            for pair in range(SWA_HEADS // 2):
                q2 = q_ref[rows, pair * LANES:(pair + 1) * LANES]
                outs = []
                for a in range(2):
                    h = 2 * pair + a
                    qa = jnp.where(lo if a == 0 else ~lo, q2, jnp.zeros_like(q2)) * SCALE
                    kx, vx = (k_all[band], v_all[band]) if h in SWA_GROUPS[0] else (k_all_sw[band], v_all_sw[band])
                    s = _dot_nt(qa, kx) + bias[h]
                    sink = sink_ref[h]
                    m = jnp.maximum(jnp.max(s, axis=-1, keepdims=True), sink)
                    e = jnp.exp(s - m)
                    l = jnp.sum(e, axis=-1, keepdims=True) + jnp.exp(sink - m)
                    pn = (e * (1.0 / l)).astype(BF16)
                    outs.append(_dot(pn, vx))
                    lse_t = jnp.where(lane == h, m + jnp.log(l), lse_t)
                o_ref[rows, pair * LANES:(pair + 1) * LANES] = jnp.where(lo, outs[0], outs[1]).astype(BF16)
            lse_ref[rows, :] = lse_t

    return pl.pallas_call(
        kern, grid=(n_tok // step_rows,),
        in_specs=_swa_specs() + [pl.BlockSpec(memory_space=pltpu.SMEM)],
        out_specs=[pl.BlockSpec((step_rows, 4 * LANES), lambda n: (n, 0)),
                   pl.BlockSpec((step_rows, LANES), lambda n: (n, 0))],
        out_shape=[jax.ShapeDtypeStruct((n_tok, 4 * LANES), BF16), jax.ShapeDtypeStruct((n_tok, LANES), F32)],
        name="swa_fwd", compiler_params=_params(1))(zm, zm, zm, zm, zm, _swa_bias_table(), sinks)


def _swa_bwd(zm, sinks, d_out, out, lse):
    n_tok = zm.shape[0]
    blk, step_rows = SWA_BLOCK, SWA_PER_STEP * SWA_BLOCK

    def kern(q_ref, kp_ref, kc_ref, vp_ref, vc_ref, bias_ref, do_ref, o_ref, lse_ref, sink_ref,
             dq_ref, dkp_ref, dkc_ref, dvp_ref, dvc_ref, dsk_ref):
        n = pl.program_id(0)

        @pl.when(n == 0)
        def _():
            dsk_ref[...] = jnp.zeros_like(dsk_ref)

        bands = (_swa_band(kp_ref, kc_ref), _swa_band(vp_ref, vc_ref))
        lane = lax.broadcasted_iota(jnp.int32, (blk, LANES), 1)
        lo = lane < HEAD_DIM
        for b in range(SWA_PER_STEP):
            rows, band = slice(b * blk, (b + 1) * blk), slice(b * blk, (b + 2) * blk)
            bias = _swa_bias(bias_ref, n, b)
            lse_t = lse_ref[rows, :]
            dqs, dkv = {}, []
            for g, heads in enumerate(SWA_GROUPS):
                kx, vx = bands[0][g][band], bands[1][g][band]
                qa, s = _swa_scores(q_ref, rows, kx, heads, lo, bias)
                doa = _stack_heads(do_ref, rows, heads, lo, True)
                lse_g = jnp.concatenate([lse_t[:, h:h + 1] for h in heads], axis=0)
                prob = jnp.exp(s - lse_g)
                o_g = _stack_heads(o_ref, rows, heads, lo, False)
                dd = jnp.sum(doa.astype(F32) * o_g.astype(F32), axis=-1, keepdims=True)
                ds = (prob * (_dot_nt(doa, vx) - dd)).astype(BF16)
                sink_part = -jnp.exp(_per_head_column(lambda h: sink_ref[h], heads) - lse_g) * dd
                dq = _dot(ds, kx) * SCALE
                for r, h in enumerate(heads):
                    dqs[h] = dq[r * blk:(r + 1) * blk]
                    dsk_ref[h:h + 1, :] += jnp.broadcast_to(
                        jnp.sum(sink_part[r * blk:(r + 1) * blk], axis=0, keepdims=True), (1, LANES))
                dkv.append((_dot_tn(ds, qa), _dot_tn(prob.astype(BF16), doa)))
            for pair in range(SWA_HEADS // 2):
                dq_ref[rows, pair * LANES:(pair + 1) * LANES] = jnp.where(
                    lo, dqs[2 * pair], dqs[2 * pair + 1]).astype(BF16)
            dk = dkv[0][0] + pltpu.roll(dkv[1][0], HEAD_DIM, axis=1)
            dv = dkv[0][1] + pltpu.roll(dkv[1][1], HEAD_DIM, axis=1)
            dkp_ref[rows, :] = dk[0:blk]
            dkc_ref[rows, :] = dk[blk:2 * blk]
            dvp_ref[rows, :] = dv[0:blk]
            dvc_ref[rows, :] = dv[blk:2 * blk]

    wide = pl.BlockSpec((step_rows, 4 * LANES), lambda n: (n, 0))
    narrow = pl.BlockSpec((step_rows, LANES), lambda n: (n, 0))
    part = jax.ShapeDtypeStruct((n_tok, LANES), F32)
    return pl.pallas_call(
        kern, grid=(n_tok // step_rows,),
        in_specs=_swa_specs() + [wide, wide, narrow, pl.BlockSpec(memory_space=pltpu.SMEM)],
        out_specs=[wide, narrow, narrow, narrow, narrow, pl.BlockSpec((8, LANES), lambda n: (0, 0))],
        out_shape=[jax.ShapeDtypeStruct((n_tok, 4 * LANES), BF16), part, part, part, part,
                   jax.ShapeDtypeStruct((8, LANES), F32)],
        name="swa_bwd", compiler_params=_params(1))(zm, zm, zm, zm, zm, _swa_bias_table(), d_out, out, lse, sinks)


def _my_pos():
    return lax.axis_index("x"), lax.axis_index("y"), lax.axis_index("c")


def _peer(k):
    x, y, c = _my_pos()
    px, py, pc = x ^ (k >> 2), y ^ ((k >> 1) & 1), c ^ (k & 1)
    return (px, py, pc), 4 * px + 2 * py + pc


def _gather_copies(x_refs, out_refs, send_sems, recv_sems, local_sems):
    x, y, c = _my_pos()
    my_id = 4 * x + 2 * y + c
    local = [pltpu.make_async_copy(x_refs[w], out_refs[w].at[my_id], local_sems.at[w]) for w in range(len(x_refs))]
    sends, arrivals = [], []
    for k in range(1, N_DEV):
        peer, peer_id = _peer(k)
        for w in range(len(x_refs)):
            sems = dict(send_sem=send_sems.at[7 * w + k - 1], recv_sem=recv_sems.at[7 * w + k - 1],
                        device_id=peer, device_id_type=MESH)
            sends.append(pltpu.make_async_remote_copy(src_ref=x_refs[w], dst_ref=out_refs[w].at[my_id], **sems))
            arrivals.append(pltpu.make_async_remote_copy(src_ref=x_refs[w], dst_ref=out_refs[w].at[peer_id], **sems))
    return local, sends, arrivals


def _scatter_copies(g_refs, part_refs, send_sems, recv_sems, local_sems):
    x, y, c = _my_pos()
    my_id = 4 * x + 2 * y + c
    local = [pltpu.make_async_copy(g_refs[w].at[my_id], part_refs[w].at[0], local_sems.at[w])
             for w in range(len(g_refs))]
    sends, arrivals = [], []
    for k in range(1, N_DEV):
        peer, peer_id = _peer(k)
        for w in range(len(g_refs)):
            sems = dict(send_sem=send_sems.at[7 * w + k - 1], recv_sem=recv_sems.at[7 * w + k - 1],
                        device_id=peer, device_id_type=MESH)
            sends.append(pltpu.make_async_remote_copy(src_ref=g_refs[w].at[peer_id], dst_ref=part_refs[w].at[k], **sems))
            arrivals.append(pltpu.make_async_remote_copy(src_ref=g_refs[w].at[my_id], dst_ref=part_refs[w].at[k], **sems))
    return local, sends, arrivals


def _start_copies(local, sends, arrivals):
    for cp in local + sends:
        cp.start()


def _finish_copies(local, sends, arrivals):
    for cp in arrivals:
        cp.wait_recv()
    for cp in sends:
        cp.wait_send()
    for cp in local:
        cp.wait()


def _exchange_scratch(n_arrays):
    return [pltpu.SemaphoreType.DMA((7 * n_arrays,)), pltpu.SemaphoreType.DMA((7 * n_arrays,)),
            pltpu.SemaphoreType.DMA((n_arrays,))]


class _Ride:
    def __init__(self, arrays, out_shape, copies):
        self.arrays, self.out_shape, self.copies = list(arrays), list(out_shape), copies
        any_spec = pl.BlockSpec(memory_space=pl.ANY)
        self.in_specs = [any_spec] * len(self.arrays)
        self.out_specs = [any_spec] * len(self.arrays)
        self.scratch = _exchange_scratch(len(self.arrays)) if self.arrays else []

    @staticmethod
    def _at(grid, last):
        hit = [pl.program_id(d) == (n - 1 if last else 0) for d, n in enumerate(grid)]
        return hit[0] if len(hit) == 1 else jnp.logical_and(*hit)

    def at_first_step(self, grid, in_refs, out_refs, sems):
        @pl.when(self._at(grid, False))
        def _():
            _start_copies(*self.copies(in_refs, out_refs, *sems))

    def at_last_step(self, grid, in_refs, out_refs, sems):
        @pl.when(self._at(grid, True))
        def _():
            _finish_copies(*self.copies(in_refs, out_refs, *sems))


_NO_RIDE = _Ride([], [], None)


def _gather_ride(shards):
    return _Ride(shards, [jax.ShapeDtypeStruct((N_DEV,) + s.shape, s.dtype) for s in shards], _gather_copies)


def _scatter_ride(grads):
    return _Ride(grads, [jax.ShapeDtypeStruct(g.shape, g.dtype) for g in grads], _scatter_copies)


Q_COL, K_COL, V_COL = 6, 10, 14


def _low_half(rows):
    return lax.broadcasted_iota(jnp.int32, (rows, LANES), 1) < HEAD_DIM


def _lane_tile(stat, width):
    return jnp.tile(stat, (1, width // LANES))


FOX_KEY_BLOCKS = 2


def _fox_steps(nq):
    kb = FOX_KEY_BLOCKS
    steps = [(i2, j, 0 if j < (2 * i2) // kb else 1 + (2 * i2) % kb)
             for i2 in range(nq // 2) for j in range((2 * i2) // kb + 1)]
    return [np.asarray(col, np.int32) for col in zip(*steps)]


def _fox_tiles(kind, near_only=False):
    kb = FOX_KEY_BLOCKS
    if kind == 0:
        return [(sub, kb // 2 if near_only else 0, kb, None) for sub in range(2)]
    return [(sub, 0, kind + sub, kind - 1 + sub) for sub in range(2)]


def _causal(t, first_row):
    row = lax.broadcasted_iota(jnp.int32, t.shape, 0) + first_row
    col = lax.broadcasted_iota(jnp.int32, t.shape, 1)
    return jnp.where(col <= row, t, NEG)


def _fox_dispatch(sweep, kind, dead_ref, head0, idx):
    kb = FOX_KEY_BLOCKS
    below = kind == 0

    def all_dead(h, blocks):
        dead = dead_ref[h, idx + blocks[0]] > 0.5
        for b in blocks[1:]:
            dead = jnp.logical_and(dead, dead_ref[h, idx + b] > 0.5)
        return dead

    takes_all, takes_near = [], []
    for h in (head0, head0 + 1):
        far_dead = all_dead(h, list(range(kb // 2)))
        takes_all.append(jnp.logical_not(far_dead))
        takes_near.append(jnp.logical_and(far_dead, jnp.logical_not(all_dead(h, list(range(kb // 2, kb))))))
    joint = jnp.logical_and(takes_all[0], takes_all[1])
    pl.when(jnp.logical_and(below, joint))(lambda: sweep(_fox_tiles(0), (0, 1)))
    apart = jnp.logical_and(below, jnp.logical_not(joint))
    for a in range(2):
        pl.when(jnp.logical_and(apart, takes_all[a]))(lambda a=a: sweep(_fox_tiles(0), (a,)))
        pl.when(jnp.logical_and(below, takes_near[a]))(lambda a=a: sweep(_fox_tiles(0, True), (a,)))
    for p in range(0, kb, 2):
        pl.when(kind == 1 + p)(lambda p=p: sweep(_fox_tiles(1 + p), (0, 1)))


EXP_ZERO = 104.5
NORM_SLACK = 1.005


def _fox_dead_steps(nrm, c_pairs, tq):
    nq = nrm.shape[0] // 8
    stats = nrm.reshape(nq, 8, LANES)[:, :3, :FOX_HEADS]
    qn, kn, own = jnp.sqrt(stats[:, 0]) * SCALE, jnp.sqrt(stats[:, 1]), stats[:, 2] * SCALE
    cb = c_pairs.reshape(FOX_HEADS, nq, tq)
    c_max, c_min = jnp.max(cb, axis=-1).T, jnp.min(cb, axis=-1).T
    both = lambda t, pick: pick(t.reshape(nq // 2, 2, FOX_HEADS), axis=1)
    qn2, kn2, c_max2, own2 = both(qn, jnp.max), both(kn, jnp.max), both(c_max, jnp.max), both(own, jnp.min)
    row_max_floor = own2 - (NORM_SLACK - 1.0) * qn2 * kn2 - c_max2
    gap = qn2[:, None] * kn[None] * NORM_SLACK - c_min[None] - row_max_floor[:, None]
    below = jnp.arange(nq)[None, :] < 2 * jnp.arange(nq // 2)[:, None]
    dead = jnp.logical_and(gap < -EXP_ZERO, below[..., None])
    return dead.transpose(2, 0, 1).reshape(FOX_HEADS, -1).astype(F32)


def _fox_fwd(zm, c_pairs, dead, tq, ride=None):
    n_tok = zm.shape[0]
    nq = n_tok // tq
    ii, jj, kk = _fox_steps(nq)
    n_steps = len(ii)
    n_ride = len(ride.arrays) if ride else 0

    def kern(ii_ref, jj_ref, kk_ref, q_ref, k_ref, v_ref, ck_ref, dead_ref, *more):
        ride_in, (o_ref, ln_ref), ride_out = more[:n_ride], more[n_ride:n_ride + 2], more[n_ride + 2:2 * n_ride + 2]
        qs_ref, m_ref, l_ref, acc_ref = more[2 * n_ride + 2:2 * n_ride + 6]
        step = pl.program_id(1)
        j, kind = jj_ref[step], kk_ref[step]
        lo = lax.broadcasted_iota(jnp.int32, (2 * tq, LANES), 1) < HEAD_DIM
        if ride:
            ride.at_first_step((FOX_HEADS // 2, n_steps), ride_in, ride_out, more[2 * n_ride + 6:])

        @pl.when(j == 0)
        def _():
            q2 = q_ref[...]
            zq = jnp.zeros_like(q2)
            qs_ref[0] = jnp.where(lo, q2, zq) * SCALE
            qs_ref[1] = jnp.where(lo, zq, q2) * SCALE
            m_ref[...] = jnp.full(m_ref.shape, NEG, F32)
            l_ref[...] = jnp.zeros(l_ref.shape, F32)
            acc_ref[...] = jnp.zeros(acc_ref.shape, F32)

        def sweep(tiles, heads):
            v_ones = jnp.concatenate([v_ref[...], jnp.ones((FOX_KEY_BLOCKS * tq, LANES), BF16)], axis=1)
            for sub, k0, k1, diagonal in tiles:
                rows, keys = slice(sub * tq, (sub + 1) * tq), slice(k0 * tq, k1 * tq)
                for a in heads:
                    t = _dot_nt(qs_ref[a, rows], k_ref[keys, :]) - ck_ref[a:a + 1, keys]
                    if diagonal is not None:
                        t = _causal(t, diagonal * tq)
                    m_old = m_ref[a, rows]
                    m_new = jnp.maximum(m_old, jnp.max(t, axis=-1, keepdims=True))
                    alpha = jnp.exp(m_old - m_new)
                    e = jnp.exp(t - _lane_tile(m_new, (k1 - k0) * tq)).astype(BF16)
                    pv = _dot(e, v_ones[keys])
                    acc_ref[a, rows] = alpha * acc_ref[a, rows] + pv[:, :LANES]
                    l_ref[a, rows] = alpha * l_ref[a, rows] + pv[:, LANES:]
                    m_ref[a, rows] = m_new

        _fox_dispatch(sweep, kind, dead_ref, 2 * pl.program_id(0), ii_ref[step] * nq + FOX_KEY_BLOCKS * j)

        @pl.when(kind != 0)
        def _():
            o_ref[...] = jnp.where(lo, acc_ref[0] / l_ref[0], acc_ref[1] / l_ref[1]).astype(BF16)
            ln_ref[:, :LANES] = m_ref[0] + jnp.log(l_ref[0])
            ln_ref[:, LANES:] = m_ref[1] + jnp.log(l_ref[1])

        if ride:
            ride.at_last_step((FOX_HEADS // 2, n_steps), ride_in, ride_out, more[2 * n_ride + 6:])

    blk, kblk = (2 * tq, LANES), (FOX_KEY_BLOCKS * tq, LANES)
    by_i = lambda col: (lambda hp, s, ii, jj, kk: (ii[s], col + hp))
    by_j = lambda col: (lambda hp, s, ii, jj, kk: (jj[s], col + hp))
    extra = ride if ride else _NO_RIDE
    grid_spec = pltpu.PrefetchScalarGridSpec(
        num_scalar_prefetch=3, grid=(FOX_HEADS // 2, n_steps),
        in_specs=[pl.BlockSpec(blk, by_i(Q_COL)), pl.BlockSpec(kblk, by_j(K_COL)), pl.BlockSpec(kblk, by_j(V_COL)),
                  pl.BlockSpec((None, 2, kblk[0]), lambda hp, s, ii, jj, kk: (hp, 0, jj[s])),
                  pl.BlockSpec(memory_space=pltpu.SMEM)] + extra.in_specs,
        out_specs=[pl.BlockSpec(blk, by_i(0)), pl.BlockSpec((2 * tq, 2 * LANES), by_i(0))] + extra.out_specs,
        scratch_shapes=[pltpu.VMEM((2, 2 * tq, LANES), BF16), pltpu.VMEM((2, 2 * tq, LANES), F32),
                        pltpu.VMEM((2, 2 * tq, LANES), F32), pltpu.VMEM((2, 2 * tq, LANES), F32)] + extra.scratch)
    return pl.pallas_call(
        kern, grid_spec=grid_spec,
        out_shape=[jax.ShapeDtypeStruct((n_tok, 4 * LANES), BF16),
                   jax.ShapeDtypeStruct((n_tok, FOX_HEADS * LANES), F32)] + extra.out_shape,
        name="fox_fwd", compiler_params=_params(2))(ii, jj, kk, zm, zm, zm, c_pairs, dead, *extra.arrays)


def _fox_bwd(zm, c_pairs, dead, d_out, lnorm, delta, tq, ride=None):
    n_tok = zm.shape[0]
    nq = n_tok // tq
    ii, jj, kk = _fox_steps(nq)
    n_steps = len(ii)
    n_ride = len(ride.arrays) if ride else 0

    def kern(ii_ref, jj_ref, kk_ref, q_ref, k_ref, v_ref, ck_ref, dead_ref, do_ref, ln_ref, dl_ref, *more):
        ride_in, ride_out = more[:n_ride], more[n_ride + 5:2 * n_ride + 5]
        dq_ref, dk_out, dv_out, cs_ref, rs_ref = more[n_ride:n_ride + 5]
        qs_ref, qo_ref, dos_ref, dq_acc, dk_ref, dv_ref = more[2 * n_ride + 5:2 * n_ride + 11]
        step = pl.program_id(1)
        j, kind = jj_ref[step], kk_ref[step]
        lo = lax.broadcasted_iota(jnp.int32, (2 * tq, LANES), 1) < HEAD_DIM
        if ride:
            ride.at_first_step((FOX_HEADS // 2, n_steps), ride_in, ride_out, more[2 * n_ride + 11:])

        @pl.when(step == 0)
        def _():
            dk_ref[...] = jnp.zeros_like(dk_ref)
            dv_ref[...] = jnp.zeros_like(dv_ref)
            cs_ref[...] = jnp.zeros_like(cs_ref)

        @pl.when(j == 0)
        def _():
            q2, do2 = q_ref[...], do_ref[...]
            zq = jnp.zeros_like(q2)
            ones = jnp.ones((2 * tq, LANES), BF16)
            for a in range(2):
                half = lo if a == 0 else ~lo
                qa = jnp.where(half, q2, zq) * SCALE
                qs_ref[a] = qa
                qo_ref[a] = jnp.concatenate([qa, ones], axis=1)
                dos_ref[a] = jnp.where(half, do2, zq)
            dq_acc[...] = jnp.zeros(dq_acc.shape, F32)

        def sweep(tiles, heads):
            k_ones = jnp.concatenate([k_ref[...], jnp.ones((FOX_KEY_BLOCKS * tq, LANES), BF16)], axis=1)
            sums = {}
            for sub, k0, k1, diagonal in tiles:
                rows, keys, n_keys = slice(sub * tq, (sub + 1) * tq), slice(k0 * tq, k1 * tq), (k1 - k0) * tq
                part = sums.setdefault((k0, k1), [0.0, 0.0, 0.0, 0.0])
                for a in heads:
                    t = _dot_nt(qs_ref[a, rows], k_ref[keys, :]) - ck_ref[a:a + 1, keys]
                    if diagonal is not None:
                        t = _causal(t, diagonal * tq)
                    prob = jnp.exp(t - _lane_tile(ln_ref[rows, a * LANES:(a + 1) * LANES], n_keys))
                    dp = _dot_nt(dos_ref[a, rows], v_ref[keys, :])
                    ds = (prob * (dp - _lane_tile(dl_ref[rows, a * LANES:(a + 1) * LANES], n_keys))).astype(BF16)
                    dq_acc[a, rows] += _dot(ds, k_ones[keys])
                    dk_cs = _dot_tn(ds, qo_ref[a, rows])
                    part[0] = part[0] + dk_cs[:, :LANES]
                    part[1] = part[1] + _dot_tn(prob.astype(BF16), dos_ref[a, rows])
                    part[2 + a] = part[2 + a] + dk_cs[:, LANES:]
            for (k0, k1), (dk, dv, cs0, cs1) in sums.items():
                keys = pl.ds(pl.multiple_of((FOX_KEY_BLOCKS * j + k0) * tq, tq), (k1 - k0) * tq)
                dk_ref[keys, :] += dk
                cs_ref[keys, :] += jnp.where(_low_half((k1 - k0) * tq), cs0, cs1)
                dv_ref[keys, :] += dv

        _fox_dispatch(sweep, kind, dead_ref, 2 * pl.program_id(0), ii_ref[step] * nq + FOX_KEY_BLOCKS * j)

        @pl.when(kind != 0)
        def _():
            dq_ref[...] = (jnp.where(lo, dq_acc[0, :, :LANES], dq_acc[1, :, :LANES]) * SCALE).astype(BF16)
            rs_ref[...] = jnp.where(lo, dq_acc[0, :, LANES:], dq_acc[1, :, LANES:])

        @pl.when(step == n_steps - 1)
        def _():
            dk_out[...] = dk_ref[...].astype(BF16)
            dv_out[...] = dv_ref[...].astype(BF16)

        if ride:
            ride.at_last_step((FOX_HEADS // 2, n_steps), ride_in, ride_out, more[2 * n_ride + 11:])

    blk, kblk = (2 * tq, LANES), (FOX_KEY_BLOCKS * tq, LANES)
    by_i = lambda col: (lambda hp, s, ii, jj, kk: (ii[s], col + hp))
    by_j = lambda col: (lambda hp, s, ii, jj, kk: (jj[s], col + hp))
    resident = pl.BlockSpec(blk, by_i(0))
    stat = pl.BlockSpec((2 * tq, 2 * LANES), by_i(0))
    whole = pl.BlockSpec((n_tok, LANES), lambda hp, s, ii, jj, kk: (0, hp))
    extra = ride if ride else _NO_RIDE
    grid_spec = pltpu.PrefetchScalarGridSpec(
        num_scalar_prefetch=3, grid=(FOX_HEADS // 2, n_steps),
        in_specs=[pl.BlockSpec(blk, by_i(Q_COL)), pl.BlockSpec(kblk, by_j(K_COL)), pl.BlockSpec(kblk, by_j(V_COL)),
                  pl.BlockSpec((None, 2, kblk[0]), lambda hp, s, ii, jj, kk: (hp, 0, jj[s])),
                  pl.BlockSpec(memory_space=pltpu.SMEM), resident, stat, stat] + extra.in_specs,
        out_specs=[resident, whole, whole, whole, resident] + extra.out_specs,
        scratch_shapes=[pltpu.VMEM((2, 2 * tq, LANES), BF16), pltpu.VMEM((2, 2 * tq, 2 * LANES), BF16),
                        pltpu.VMEM((2, 2 * tq, LANES), BF16), pltpu.VMEM((2, 2 * tq, 2 * LANES), F32),
                        pltpu.VMEM((n_tok, LANES), F32), pltpu.VMEM((n_tok, LANES), F32)] + extra.scratch)
    wide = lambda dt: jax.ShapeDtypeStruct((n_tok, 4 * LANES), dt)
    return pl.pallas_call(
        kern, grid_spec=grid_spec, name="fox_bwd",
        out_shape=[wide(BF16), wide(BF16), wide(BF16), wide(F32), wide(F32)] + extra.out_shape,
        compiler_params=_params(2, FOX_BWD_VMEM))(ii, jj, kk, zm, zm, zm, c_pairs, dead, d_out, lnorm, delta,
                                                  *extra.arrays)


def _all_gather(shards):
    n_w = len(shards)

    def kern(*refs):
        x_refs, out_refs = refs[:n_w], refs[n_w:2 * n_w]
        send_sems, recv_sems, local_sems = refs[2 * n_w:]
        x, y, c = _my_pos()
        me, sibling = (x, y, c), (x, y, 1 - c)
        chips = [(1 - x, y), (x, 1 - y), (1 - x, 1 - y)]

        def slot(w, px, py, pc):
            return out_refs[w].at[4 * px + 2 * py + pc]

        def copy(w, k, block, to, src=None):
            return pltpu.make_async_remote_copy(
                src_ref=slot(w, *block) if src is None else src, dst_ref=slot(w, *block),
                send_sem=send_sems.at[7 * w + k], recv_sem=recv_sems.at[7 * w + k], device_id=to, device_id_type=MESH)

        local, started = [], []
        for w in range(n_w):
            mine = pltpu.make_async_copy(x_refs[w], slot(w, *me), local_sems.at[w])
            mine.start()
            local.append(mine)
            first = [copy(w, 0, me, sibling, src=x_refs[w])]
            first += [copy(w, 1 + k, me, (*chip, c), src=x_refs[w]) for k, chip in enumerate(chips)]
            for cp in first:
                cp.start()
            started += first
        for k, chip in enumerate(chips):
            for w in range(n_w):
                copy(w, 1 + k, (*chip, c), me).wait_recv()
                passed = copy(w, 4 + k, (*chip, c), sibling)
                passed.start()
                started.append(passed)
        for w in range(n_w):
            copy(w, 0, sibling, me).wait_recv()
            for k, chip in enumerate(chips):
                copy(w, 4 + k, (*chip, 1 - c), me).wait_recv()
        for cp in started:
            cp.wait_send()
        for cp in local:
            cp.wait()

    any_spec = pl.BlockSpec(memory_space=pl.ANY)
    return pl.pallas_call(
        kern, out_shape=[jax.ShapeDtypeStruct((N_DEV,) + s.shape, s.dtype) for s in shards],
        in_specs=[any_spec] * n_w, out_specs=[any_spec] * n_w,
        scratch_shapes=[pltpu.SemaphoreType.DMA((7 * n_w,)), pltpu.SemaphoreType.DMA((7 * n_w,)),
                        pltpu.SemaphoreType.DMA((n_w,))],
        name="weight_all_gather")(*shards)


def _small_exchange(small):
    def kern(s_ref, sall_ref, *sems):
        copies = _gather_copies([s_ref], [sall_ref], *sems)
        _start_copies(*copies)
        _finish_copies(*copies)

    any_spec = pl.BlockSpec(memory_space=pl.ANY)
    return pl.pallas_call(
        kern, out_shape=jax.ShapeDtypeStruct((N_DEV,) + small.shape, small.dtype), in_specs=[any_spec],
        out_specs=any_spec, scratch_shapes=_exchange_scratch(1), name="small_grad_exchange")(small)


ADAMW_BLOCK_BYTES = 2 * 1024 * 1024


def _adamw(parts, w, m, v, name):
    n_parts, n_rows, n_cols = parts.shape
    limit = max(8, ADAMW_BLOCK_BYTES // (n_parts * n_cols * parts.dtype.itemsize))
    tr = max(t for t in range(8, n_rows + 1, 8) if n_rows % t == 0 and t <= limit)

    def kern(p_ref, w_ref, m_ref, v_ref, g_out, d_out, m_out, v_out):
        g = p_ref[0].astype(F32)
        for k in range(1, n_parts):
            g = g + p_ref[k].astype(F32)
        m_new = ADAM_B1 * m_ref[...] + (1.0 - ADAM_B1) * g
        v_new = ADAM_B2 * v_ref[...] + (1.0 - ADAM_B2) * jnp.square(g)
        m_hat = m_new / (1.0 - ADAM_B1 ** ADAM_STEP)
        v_hat = v_new / (1.0 - ADAM_B2 ** ADAM_STEP)
        g_out[...] = g
        d_out[...] = -ADAM_LR * (m_hat / (jnp.sqrt(v_hat) + ADAM_EPS) + ADAM_WD * w_ref[...])
        m_out[...] = m_new
        v_out[...] = v_new

    row = pl.BlockSpec((tr, n_cols), lambda i: (i, 0))
    out = jax.ShapeDtypeStruct((n_rows, n_cols), F32)
    return pl.pallas_call(
        kern, grid=(n_rows // tr,),
        in_specs=[pl.BlockSpec((n_parts, tr, n_cols), lambda i: (0, i, 0)), row, row, row],
        out_specs=[row, row, row, row], out_shape=[out, out, out, out], name=name,
        compiler_params=_params(1))(parts, w, m, v)


SHARDED = {
    "w_in": ((D_MODEL, D_IN), 1), "w_br_swa": ((512, D_MODEL), 1), "w_br_fox": ((512, D_MODEL), 1),
    "w_mix_out": ((D_MODEL, D_MODEL), 0), "w_ff1": ((D_MODEL, D_FF), 1), "w_ff2": ((D_FF, D_MODEL), 0),
    "w_ple_gate": ((D_MODEL, D_MODEL), 0), "w_ple_proj": ((PLE_DIM, D_MODEL), 1),
}
W_IN_SHARD = D_IN // N_DEV
W_IN_PAD = 640
SMALL = ("g_mix", "g_mlp", "g_ple", "g_final", "b_forget", "swa_sinks")
SMALL_COLS = 1024


def _wire_shard(name, a):
    a = a.reshape(a.shape[-2:])
    return jnp.pad(a, ((0, 0), (0, W_IN_PAD - W_IN_SHARD))) if name == "w_in" else a


def _from_wire(name, a):
    return (a[:, :W_IN_SHARD] if name == "w_in" else a)[None]


def _w_all_from_wire(stacked):
    w_in = jnp.concatenate([stacked[d][:, :W_IN_SHARD] for d in range(N_DEV)], axis=1)
    fpad = jnp.zeros((D_MODEL, N_FPAD - FOX_HEADS), stacked.dtype)
    return jnp.concatenate([w_in[:, :N_MAIN + FOX_HEADS], fpad, w_in[:, N_MAIN + FOX_HEADS:]], axis=1)


def _dw_in_to_wire(dw_all):
    dw_in = jnp.concatenate([dw_all[:, :N_MAIN + FOX_HEADS], dw_all[:, N_MAIN + N_FPAD:]], axis=1)
    pad = jnp.zeros((D_MODEL, W_IN_PAD - W_IN_SHARD), dw_all.dtype)
    return jnp.stack([jnp.concatenate([dw_in[:, d * W_IN_SHARD:(d + 1) * W_IN_SHARD], pad], axis=1)
                      for d in range(N_DEV)])


def _pack_small(vals, scalar=None):
    rows = [jnp.pad(vals[n].reshape(-1), (0, SMALL_COLS - vals[n].size)) for n in SMALL]
    if scalar is not None:
        rows.append(jnp.pad(scalar.reshape(1), (0, SMALL_COLS - 1)))
    rows += [jnp.zeros((SMALL_COLS,), F32)] * (8 - len(rows))
    return jnp.stack(rows)


def _unpack_small(slab, like):
    return {n: slab[r, :like[n].size].reshape(like[n].shape) for r, n in enumerate(SMALL)}


def _local_step(x, p, tgt, w, small, tm, tq, ts, late_shards=None):
    n_tok = x.shape[0]
    row = lambda v: v.reshape(1, -1)
    g_mix, g_mlp, g_ple, g_fin = row(small["g_mix"]), row(small["g_mlp"]), row(small["g_ple"]), row(small["g_final"])
    sinks = small["swa_sinks"].reshape(-1)
    b_col = small["b_forget"].reshape(FOX_HEADS, 1)

    assert tm == tq
    u1, zm, zfg, zf, nrm = _in_proj(x, g_mix, w["w_all"], tm)
    f_t = zf[:, :FOX_HEADS].T
    c_pairs = _decay_cumsum(f_t, b_col).reshape(FOX_HEADS // 2, 2, n_tok)
    attn_a, lse_a = _swa_fwd(zm, sinks)
    dead = _fox_dead_steps(nrm, c_pairs, tq)
    if late_shards is None:
        attn_b, ln_b = _fox_fwd(zm, c_pairs, dead, tq)
    else:
        attn_b, ln_b, *late = _fox_fwd(zm, c_pairs, dead, tq, _gather_ride(list(late_shards.values())))
        w = {**w, **_gathered_to_local(dict(zip(late_shards, late)))}
    ya, yb, mixed, h1, u2, a, r, h2 = _mix_ffn_fwd(attn_a, attn_b, zfg, x, w["w_br_swa"], w["w_br_fox"],
                                                   w["w_mix_out"], g_mlp, w["w_ff1"], w["w_ff2"], tm // 2)

    dlg, dpp, u3, dh2, dh2b, da, loss_acc, dgf, dgp = _head_ffn_bwd(
        h2, p, tgt, a, g_ple, w["w_ple_gate"], w["w_ple_proj"], g_fin, w["w_ff2"], tm // 2)
    dh1, dh1b, dgl, dya, dyb, daa, dab, delta_b, dgm = _ffn_bwd_b(
        da, dh2, h1, ya, yb, zfg, attn_b, w["w_ff1"], g_mlp, w["w_mix_out"], w["w_br_swa"], w["w_br_fox"], tm // 2)
    dq_a, dkp, dkc, dvp, dvc, dsk = _swa_bwd(zm, sinks, daa, attn_a, lse_a)
    dw = {
        "w_br_swa": _matmul_tn(attn_a, dya, "dw_br_swa", ts, stack_cols=D_MODEL // N_DEV),
        "w_br_fox": _matmul_tn(attn_b, dyb, "dw_br_fox", ts, stack_cols=D_MODEL // N_DEV),
        "w_mix_out": _matmul_tn(mixed, dh1b, "dw_mix_out", ts),
        "w_ff1": _matmul_tn(u2, da, "dw_ff1", ts, stack_cols=D_FF // N_DEV),
        "w_ff2": _matmul_tn(r, dh2b, "dw_ff2", ts),
        "w_ple_gate": _matmul_tn(u3, dlg, "dw_ple_gate", ts),
        "w_ple_proj": _matmul_tn(p, dpp, "dw_ple_proj", ts, stack_cols=D_MODEL // N_DEV),
    }
    if late_shards is None:
        dq_b, dk_b, dv_b, cs, rs = _fox_bwd(zm, c_pairs, dead, dab, ln_b, delta_b, tq)
        late_parts = None
    else:
        wire = _local_to_wire(dw)
        dq_b, dk_b, dv_b, cs, rs, *parts = _fox_bwd(zm, c_pairs, dead, dab, ln_b, delta_b, tq,
                                                    _scatter_ride([wire[n] for n in late_shards]))
        late_parts = dict(zip(late_shards, parts))

    up = lambda t: jnp.concatenate([t[SWA_BLOCK:], jnp.zeros((SWA_BLOCK, LANES), F32)], axis=0)
    dk_a, dv_a = dkc + up(dkp), dvc + up(dvp)
    df_t, db = _decay_bwd(cs, rs, f_t, b_col)
    df = jnp.pad(df_t.T, ((0, 0), (0, N_FPAD - FOX_HEADS)))
    dz = jnp.concatenate([dq_a, dk_a.astype(BF16), dv_a.astype(BF16), dq_b, dk_b, dv_b,
                          df.astype(BF16), dgl], axis=1)
    dw["w_all"] = _matmul_tn(u1, dz, "dw_in", ts)
    if late_shards is None:
        dx, dgx = _in_proj_bwd(dz, dh1, x, w["w_all"], g_mix, tm)
    else:
        dx, dgx, late_parts["w_in"] = _in_proj_bwd(dz, dh1, x, w["w_all"], g_mix, tm,
                                                   _scatter_ride([_dw_in_to_wire(dw["w_all"])]))
    dsmall = {"g_mix": dgx[0], "g_mlp": dgm[0], "g_ple": dgp[0], "g_final": dgf[0],
              "b_forget": db[:, 0], "swa_sinks": dsk[:, 0]}
    return loss_acc[0, 0], dx, dw, dsmall, late_parts


_ROWS = lambda t: t.reshape(-1, t.shape[-1])
_BY_ROWS = lambda t: t.reshape(N_DEV, t.shape[0] // N_DEV, t.shape[1])
_SAME = lambda t: t
LOCAL_LAYOUT = {
    "w_in": ("w_all", _w_all_from_wire, _dw_in_to_wire), "w_br_swa": ("w_br_swa", _SAME, _SAME),
    "w_br_fox": ("w_br_fox", _SAME, _SAME), "w_mix_out": ("w_mix_out", _ROWS, _BY_ROWS),
    "w_ff1": ("w_ff1", _SAME, _SAME), "w_ff2": ("w_ff2", _SAME, _BY_ROWS),
    "w_ple_gate": ("w_ple_gate", _ROWS, _BY_ROWS), "w_ple_proj": ("w_ple_proj", _SAME, _SAME),
}


def _gathered_to_local(g):
    return {LOCAL_LAYOUT[n][0]: LOCAL_LAYOUT[n][1](t) for n, t in g.items()}


def _local_to_wire(dw):
    names = {local: n for n, (local, _, _) in LOCAL_LAYOUT.items()}
    return {names[local]: LOCAL_LAYOUT[names[local]][2](t) for local, t in dw.items()}


def kernel(x, p, g_mix, w_in, b_forget, swa_sinks, w_br_swa, w_br_fox, w_mix_out, g_mlp, w_ff1, w_ff2, g_ple, w_ple_gate, w_ple_proj, g_final, loss_target, m_g_mix, m_w_in, m_b_forget, m_swa_sinks, m_w_br_swa, m_w_br_fox, m_w_mix_out, m_g_mlp, m_w_ff1, m_w_ff2, m_g_ple, m_w_ple_gate, m_w_ple_proj, m_g_final, v_g_mix, v_w_in, v_b_forget, v_swa_sinks, v_w_br_swa, v_w_br_fox, v_w_mix_out, v_g_mlp, v_w_ff1, v_w_ff2, v_g_ple, v_w_ple_gate, v_w_ple_proj, v_g_final):
    given = dict(g_mix=g_mix, w_in=w_in, b_forget=b_forget, swa_sinks=swa_sinks, w_br_swa=w_br_swa, w_br_fox=w_br_fox,
                 w_mix_out=w_mix_out, g_mlp=g_mlp, w_ff1=w_ff1, w_ff2=w_ff2, g_ple=g_ple, w_ple_gate=w_ple_gate,
                 w_ple_proj=w_ple_proj, g_final=g_final)
    mom = dict(g_mix=m_g_mix, w_in=m_w_in, b_forget=m_b_forget, swa_sinks=m_swa_sinks, w_br_swa=m_w_br_swa,
               w_br_fox=m_w_br_fox, w_mix_out=m_w_mix_out, g_mlp=m_g_mlp, w_ff1=m_w_ff1, w_ff2=m_w_ff2, g_ple=m_g_ple,
               w_ple_gate=m_w_ple_gate, w_ple_proj=m_w_ple_proj, g_final=m_g_final)
    vel = dict(g_mix=v_g_mix, w_in=v_w_in, b_forget=v_b_forget, swa_sinks=v_swa_sinks, w_br_swa=v_w_br_swa,
               w_br_fox=v_w_br_fox, w_mix_out=v_w_mix_out, g_mlp=v_g_mlp, w_ff1=v_w_ff1, w_ff2=v_w_ff2, g_ple=v_g_ple,
               w_ple_gate=v_w_ple_gate, w_ple_proj=v_w_ple_proj, g_final=v_g_final)
    names = list(given)
    sharded = list(SHARDED)

    w_wire = {n: _wire_shard(n, given[n]) for n in sharded}
    late = [n for n in sharded if n != "w_in"]
    gathered = _all_gather([w_wire["w_in"].astype(BF16)])
    local_w = _gathered_to_local({"w_in": gathered[0]})
    small = {n: given[n].reshape(-1) for n in SMALL}

    n_tok = x.shape[1]
    tile = min(TOKEN_TILE, n_tok // 4)
    loss_part, dx, dw, dsmall, parts = _local_step(
        x[0], p[0, 0], loss_target[0], local_w, small, tm=tile, tq=tile, ts=min(DW_TOKENS_PER_STEP, n_tok // 4),
        late_shards={n: w_wire[n].astype(BF16) for n in late})
    small_all = _small_exchange(_pack_small(dsmall, loss_part))

    res = {}
    for n in sharded:
        part = parts[n]
        flat = part.reshape(N_DEV, -1, part.shape[-1])
        outs = _adamw(flat, w_wire[n], _wire_shard(n, mom[n]), _wire_shard(n, vel[n]), "adamw_" + n)
        res[n] = [_from_wire(n, o) for o in outs]
    outs_s = _adamw(small_all, _pack_small(small), _pack_small({n: mom[n] for n in SMALL}),
                    _pack_small({n: vel[n] for n in SMALL}), "adamw_small")
    small_res = [_unpack_small(o, given) for o in outs_s]
    loss = outs_s[0][len(SMALL), 0]

    groups = [[res[n][k] if n in res else small_res[k][n] for n in names] for k in range(4)]
    return (loss, dx[None], *groups[0], *groups[1], *groups[2], *groups[3])
```

```python
import numpy as np
import jax
import jax.numpy as jnp
from jax import lax
from jax.experimental import pallas as pl
from jax.experimental.pallas import tpu as pltpu

F32 = jnp.float32
BF16 = jnp.bfloat16

D_MODEL = 1024
HEAD_DIM = 64
SWA_HEADS = 8
FOX_HEADS = 8
CHUNK_SHIFT = 6
SWA_BLOCK = 128
WINDOW_CHUNKS = 2
D_FF = 4096
PLE_DIM = 256
RMS_EPS = 1e-6
N_MAIN = 2304
N_FPAD = 128
N_GATE = 2048
D_IN = N_MAIN + FOX_HEADS + N_GATE
SCALE = HEAD_DIM ** -0.5
NEG = -1e30

ADAM_LR = 0.001
ADAM_B1 = 0.9
ADAM_B2 = 0.999
ADAM_EPS = 1e-08
ADAM_WD = 0.01
ADAM_STEP = 10

N_DEV = 8
TOKEN_TILE = 512
DW_TOKENS_PER_STEP = 2048
LANES = 128
V7X_VMEM_BYTES = 64 * 1024 * 1024
VMEM_LIMIT = V7X_VMEM_BYTES * 3 // 4
FOX_BWD_VMEM = V7X_VMEM_BYTES * 7 // 8
MESH = pl.DeviceIdType.MESH

_NT = (((1,), (1,)), ((), ()))
_TN = (((0,), (0,)), ((), ()))


def _params(n_grid, vmem_limit=VMEM_LIMIT):
    return pltpu.CompilerParams(dimension_semantics=("arbitrary",) * n_grid, vmem_limit_bytes=vmem_limit)


def _chunks(n, step):
    return [(s, min(step, n - s)) for s in range(0, n, step)]


def _sigmoid(x):
    return 1.0 / (1.0 + jnp.exp(-x))


def _dot(a, b):
    return jnp.dot(a, b, preferred_element_type=F32)


def _dot_nt(a, b):
    return lax.dot_general(a, b, _NT, preferred_element_type=F32)


def _dot_tn(a, b):
    return lax.dot_general(a, b, _TN, preferred_element_type=F32)


def _lane_concat(stacked_ref):
    return jnp.concatenate([stacked_ref[d] for d in range(N_DEV)], axis=1)


def _rms(h):
    return lax.rsqrt(jnp.mean(h * h, axis=-1, keepdims=True) + RMS_EPS)


def _rms_bwd(h, g, du):
    rs = _rms(h)
    n = h * rs
    dn = du * g
    dh = rs * (dn - n * jnp.mean(dn * n, axis=-1, keepdims=True))
    return dh, jnp.sum(du * n, axis=0, keepdims=True)


def _acc_rows(ref, i, row):
    @pl.when(i == 0)
    def _():
        ref[...] = jnp.zeros_like(ref)
    ref[...] += jnp.broadcast_to(row, ref.shape)


def _row_call(body, name, n_rows, tm, row_ins, const_ins, row_outs, acc_outs, ride=None, tile_outs=()):
    row_outs = list(row_outs)
    n_ri, n_ci, n_ro, n_ao = len(row_ins), len(const_ins), len(row_outs) + len(tile_outs), len(acc_outs)
    extra = ride if ride else _NO_RIDE
    n_ride = len(extra.arrays)
    grid = (n_rows // tm,)

    def kern(*refs):
        i = pl.program_id(0)
        ins, refs = refs[:n_ri + n_ci], refs[n_ri + n_ci:]
        ride_in, refs = refs[:n_ride], refs[n_ride:]
        outs, refs = refs[:n_ro + n_ao], refs[n_ro + n_ao:]
        ride_out, sems = refs[:n_ride], refs[n_ride:]
        if ride:
            ride.at_first_step(grid, ride_in, ride_out, sems)
        body(i, ins[:n_ri], ins[n_ri:], outs[:n_ro], outs[n_ro:])
        if ride:
            ride.at_last_step(grid, ride_in, ride_out, sems)

    def whole(a):
        zeros = (0,) * a.ndim
        return pl.BlockSpec(a.shape, lambda i: zeros, pipeline_mode=pl.Buffered(1))

    in_specs = [pl.BlockSpec((tm, a.shape[1]), lambda i: (i, 0)) for a in row_ins]
    in_specs += [whole(a) for a in const_ins] + extra.in_specs
    out_specs = [pl.BlockSpec((tm, c), lambda i: (i, 0)) for c, _ in row_outs]
    out_specs += [pl.BlockSpec((8, c), lambda i: (i, 0)) for c in tile_outs]
    out_specs += [pl.BlockSpec((8, c), lambda i: (0, 0)) for c in acc_outs] + extra.out_specs
    out_shape = [jax.ShapeDtypeStruct((n_rows, c), dt) for c, dt in row_outs]
    out_shape += [jax.ShapeDtypeStruct((8 * grid[0], c), F32) for c in tile_outs]
    out_shape += [jax.ShapeDtypeStruct((8, c), F32) for c in acc_outs] + extra.out_shape
    return pl.pallas_call(kern, grid=grid, in_specs=in_specs, out_specs=out_specs, out_shape=out_shape,
                          scratch_shapes=extra.scratch, name=name,
                          compiler_params=_params(1))(*row_ins, *const_ins, *extra.arrays)


def _in_proj(x, g_mix, w_all, tm):
    def body(i, ins, consts, outs, accs):
        x_ref, = ins
        g_ref, w_ref = consts
        u_ref, zm_ref, zfg_ref, zf_ref, nrm_ref = outs
        xv = x_ref[...]
        u = ((xv * _rms(xv)) * g_ref[...]).astype(BF16)
        u_ref[...] = u
        for s, n in _chunks(N_MAIN, 768):
            zm_ref[:, s:s + n] = _dot_nt(u, w_ref[s:s + n, :]).astype(BF16)
        for s, n in _chunks(N_FPAD + N_GATE, 512):
            zfg_ref[:, s:s + n] = _dot_nt(u, w_ref[N_MAIN + s:N_MAIN + s + n, :])
        zf_ref[...] = zfg_ref[:, :N_FPAD]
        lane = lax.broadcasted_iota(jnp.int32, (4 * LANES, LANES), 0)
        head = lax.broadcasted_iota(jnp.int32, (4 * LANES, LANES), 1)
        pick = (lane // HEAD_DIM == head).astype(BF16)
        tq_, tk_ = (zm_ref[:, col * LANES:(col + 4) * LANES].astype(F32) for col in (Q_COL, K_COL))
        rows = [jnp.max(_dot((t * t).astype(BF16), pick), axis=0, keepdims=True) for t in (tq_, tk_)]
        rows.append(jnp.min(_dot((tq_ * tk_).astype(BF16), pick), axis=0, keepdims=True))
        nrm_ref[...] = jnp.concatenate(rows + [jnp.zeros((5, LANES), F32)], axis=0)

    *outs, nrm = _row_call(body, "in_proj", x.shape[0], tm, [x], [g_mix, w_all],
                           [(D_MODEL, BF16), (N_MAIN, BF16), (N_FPAD + N_GATE, F32), (N_FPAD, F32)], [],
                           tile_outs=[LANES])
    return (*outs, nrm)


def _mix_ffn_fwd(attn_a, attn_b, zfg, x, w_sa, w_fo, w_mo, g_mlp, w1s, w2s, tm):
    ch = D_FF // N_DEV

    def body(i, ins, consts, outs, accs):
        aa_ref, ab_ref, zfg_ref, x_ref = ins
        wsa_ref, wfo_ref, wmo_ref, g_ref, w1_ref, w2_ref = consts
        ya_ref, yb_ref, mx_ref, h1_ref, u2_ref, a_ref, r_ref, h2_ref = outs
        ya = _dot(aa_ref[...], _lane_concat(wsa_ref))
        yb = _dot(ab_ref[...], _lane_concat(wfo_ref))
        g0 = _sigmoid(zfg_ref[:, N_FPAD:N_FPAD + D_MODEL])
        g1 = _sigmoid(zfg_ref[:, N_FPAD + D_MODEL:N_FPAD + 2 * D_MODEL])
        mixed = (g0 * ya + g1 * yb).astype(BF16)
        ya_ref[...] = ya.astype(BF16)
        yb_ref[...] = yb.astype(BF16)
        mx_ref[...] = mixed
        h1 = x_ref[...] + _dot(mixed, wmo_ref[...])
        h1_ref[...] = h1
        u = ((h1 * _rms(h1)) * g_ref[...]).astype(BF16)
        u2_ref[...] = u
        acc = h1
        for c in range(N_DEV):
            a = _dot(u, w1_ref[c])
            a_ref[:, c * ch:(c + 1) * ch] = a.astype(BF16)
            r = jnp.square(jnp.maximum(a, 0.0)).astype(BF16)
            r_ref[:, c * ch:(c + 1) * ch] = r
            acc = acc + _dot(r, w2_ref[c])
        h2_ref[...] = acc

    return _row_call(body, "mix_ffn_fwd", x.shape[0], tm, [attn_a, attn_b, zfg, x],
                     [w_sa, w_fo, w_mo, g_mlp, w1s, w2s],
                     [(D_MODEL, BF16), (D_MODEL, BF16), (D_MODEL, BF16), (D_MODEL, F32), (D_MODEL, BF16),
                      (D_FF, BF16), (D_FF, BF16), (D_MODEL, F32)], [])


def _head_ffn_bwd(h2, p, tgt, a, g_ple, w_pg, w_pp, g_fin, w2s, tm):
    ch = D_FF // N_DEV

    def body(i, ins, consts, outs, accs):
        h2_ref, p_ref, t_ref, a_ref = ins
        gp_ref, wpg_ref, wpp_ref, gf_ref, w2_ref = consts
        dlg_ref, dpp_ref, u3_ref, dh2_ref, dh2b_ref, da_ref = outs
        loss_ref, dgf_ref, dgp_ref = accs
        h2 = h2_ref[...]
        gp = gp_ref[...]
        u3 = ((h2 * _rms(h2)) * gp).astype(BF16)
        u3_ref[...] = u3
        pg = _sigmoid(_dot(u3, wpg_ref[...]))
        pp = _dot(p_ref[...].astype(BF16), _lane_concat(wpp_ref))
        h3 = h2 + pg * pp
        rs3 = _rms(h3)
        n3 = h3 * rs3
        gf = gf_ref[...]
        err = n3 * gf - t_ref[...]
        row_loss = 0.5 * jnp.mean(err * err, axis=-1, keepdims=True)
        _acc_rows(loss_ref, i, jnp.broadcast_to(jnp.sum(row_loss, axis=0, keepdims=True), (1, LANES)))
        dy = err * (1.0 / D_MODEL)
        _acc_rows(dgf_ref, i, jnp.sum(dy * n3, axis=0, keepdims=True))
        dn = dy * gf
        dh3 = rs3 * (dn - n3 * jnp.mean(dn * n3, axis=-1, keepdims=True))
        dpp_ref[...] = (dh3 * pg).astype(BF16)
        dlg = ((dh3 * pp) * pg * (1.0 - pg)).astype(BF16)
        dlg_ref[...] = dlg
        dh, dg = _rms_bwd(h2, gp, _dot_nt(dlg, wpg_ref[...]))
        _acc_rows(dgp_ref, i, dg)
        dh2 = dh3 + dh
        dh2_ref[...] = dh2
        dh2b = dh2.astype(BF16)
        dh2b_ref[...] = dh2b
        for c in range(N_DEV):
            dr = _dot_nt(dh2b, w2_ref[c])
            av = a_ref[:, c * ch:(c + 1) * ch].astype(F32)
            da_ref[:, c * ch:(c + 1) * ch] = (dr * (2.0 * jnp.maximum(av, 0.0))).astype(BF16)

    return _row_call(body, "head_ffn_bwd", h2.shape[0], tm, [h2, p, tgt, a], [g_ple, w_pg, w_pp, g_fin, w2s],
                     [(D_MODEL, BF16), (D_MODEL, BF16), (D_MODEL, BF16), (D_MODEL, F32), (D_MODEL, BF16),
                      (D_FF, BF16)], [LANES, D_MODEL, D_MODEL])


def _ffn_bwd_b(da, dh2, h1, ya, yb, zfg, attn_b, w1s, g_mlp, w_mo, w_sa, w_fo, tm):
    ch = D_FF // N_DEV

    def body(i, ins, consts, outs, accs):
        da_ref, dh2_ref, h1_ref, ya_ref, yb_ref, zfg_ref, ob_ref = ins
        w1_ref, gm_ref, wmo_ref, wsa_ref, wfo_ref = consts
        dh1_ref, dh1b_ref, dgl_ref, dya_ref, dyb_ref, daa_ref, dab_ref, dl_ref = outs
        dgm_ref, = accs
        du2 = _dot_nt(da_ref[:, 0:ch], w1_ref[0])
        for c in range(1, N_DEV):
            du2 = du2 + _dot_nt(da_ref[:, c * ch:(c + 1) * ch], w1_ref[c])
        dh, dg = _rms_bwd(h1_ref[...], gm_ref[...], du2)
        _acc_rows(dgm_ref, i, dg)
        dh1 = dh2_ref[...] + dh
        dh1_ref[...] = dh1
        dh1b = dh1.astype(BF16)
        dh1b_ref[...] = dh1b
        dmx = _dot_nt(dh1b, wmo_ref[...])
        g0 = _sigmoid(zfg_ref[:, N_FPAD:N_FPAD + D_MODEL])
        g1 = _sigmoid(zfg_ref[:, N_FPAD + D_MODEL:N_FPAD + 2 * D_MODEL])
        dya = (dmx * g0).astype(BF16)
        dyb = (dmx * g1).astype(BF16)
        dya_ref[...] = dya
        dyb_ref[...] = dyb
        dgl_ref[:, 0:D_MODEL] = ((dmx * ya_ref[...].astype(F32)) * g0 * (1.0 - g0)).astype(BF16)
        dgl_ref[:, D_MODEL:2 * D_MODEL] = ((dmx * yb_ref[...].astype(F32)) * g1 * (1.0 - g1)).astype(BF16)
        daa_ref[...] = _dot_nt(dya, _lane_concat(wsa_ref)).astype(BF16)
        dab = _dot_nt(dyb, _lane_concat(wfo_ref)).astype(BF16)
        dab_ref[...] = dab
        half_in = lax.broadcasted_iota(jnp.int32, (LANES, 2 * LANES), 0) // HEAD_DIM
        half_out = lax.broadcasted_iota(jnp.int32, (LANES, 2 * LANES), 1) // LANES
        pick = (half_in == half_out).astype(BF16)
        for pair in range(FOX_HEADS // 2):
            cols = slice(pair * LANES, (pair + 1) * LANES)
            prod = dab[:, cols].astype(F32) * ob_ref[:, cols].astype(F32)
            hi = prod.astype(BF16)
            lo_part = (prod - hi.astype(F32)).astype(BF16)
            dl_ref[:, 2 * pair * LANES:(2 * pair + 2) * LANES] = _dot(hi, pick) + _dot(lo_part, pick)

    half = D_MODEL // 2
    return _row_call(body, "ffn_bwd_b", h1.shape[0], tm, [da, dh2, h1, ya, yb, zfg, attn_b],
                     [w1s, g_mlp, w_mo, w_sa, w_fo],
                     [(D_MODEL, F32), (D_MODEL, BF16), (N_GATE, BF16), (D_MODEL, BF16), (D_MODEL, BF16),
                      (half, BF16), (half, BF16), (FOX_HEADS * LANES, F32)], [D_MODEL])


def _in_proj_bwd(dz, dh1, x, w_all, g_mix, tm, ride=None):
    def body(i, ins, consts, outs, accs):
        dz_ref, dh1_ref, x_ref = ins
        w_ref, g_ref = consts
        dx_ref, = outs
        dgx_ref, = accs
        du1 = _dot(dz_ref[...], w_ref[...])
        dh, dg = _rms_bwd(x_ref[...], g_ref[...], du1)
        _acc_rows(dgx_ref, i, dg)
        dx_ref[...] = dh1_ref[...] + dh

    return _row_call(body, "in_proj_bwd", x.shape[0], tm, [dz, dh1, x], [w_all, g_mix],
                     [(D_MODEL, F32)], [D_MODEL], ride)


def _matmul_tn(a, b, name, ts, stack_cols=0):
    n_rows, ka = a.shape
    n = b.shape[1]
    tk = min(ka, 896 if ka % 1024 else 1024)
    tn = 896 if n % 1024 else 1024
    n_stack = tn // stack_cols if stack_cols else 0
    assert ka % tk == 0 and n % tn == 0 and n_rows % ts == 0 and (not stack_cols or tk == ka)
    n_steps = n_rows // ts

    def kern(a_ref, b_ref, o_ref, acc_ref):
        s = pl.program_id(2)

        @pl.when(s == 0)
        def _():
            acc_ref[...] = jnp.zeros_like(acc_ref)
        acc_ref[...] += _dot_tn(a_ref[...].astype(BF16), b_ref[...])

        @pl.when(s == n_steps - 1)
        def _():
            if stack_cols:
                for c in range(n_stack):
                    o_ref[c] = acc_ref[:, c * stack_cols:(c + 1) * stack_cols].astype(BF16)
            else:
                o_ref[...] = acc_ref[...].astype(BF16)

    if stack_cols:
        out_spec = pl.BlockSpec((n_stack, tk, stack_cols), lambda i, j, s: (j, 0, 0))
        out_shape = jax.ShapeDtypeStruct((n // stack_cols, ka, stack_cols), BF16)
    else:
        out_spec = pl.BlockSpec((tk, tn), lambda i, j, s: (i, j))
        out_shape = jax.ShapeDtypeStruct((ka, n), BF16)
    return pl.pallas_call(
        kern, grid=(ka // tk, n // tn, n_steps),
        in_specs=[pl.BlockSpec((ts, tk), lambda i, j, s: (s, i)), pl.BlockSpec((ts, tn), lambda i, j, s: (s, j))],
        out_specs=out_spec, out_shape=out_shape, scratch_shapes=[pltpu.VMEM((tk, tn), F32)], name=name,
        compiler_params=_params(3))(a, b)


SCAN_CHUNK = 512
BWD_SCAN_CHUNK = 1024


def _decay_cumsum(f_t, b_col):
    n_tok = f_t.shape[1]
    ch = min(SCAN_CHUNK, n_tok)

    def kern(f_ref, b_ref, c_ref):
        r = lax.broadcasted_iota(jnp.int32, (ch, ch), 0)
        c = lax.broadcasted_iota(jnp.int32, (ch, ch), 1)
        tri = (r <= c).astype(F32)
        carry = jnp.zeros((8, 1), F32)
        for k in range(n_tok // ch):
            xv = f_ref[:, k * ch:(k + 1) * ch] + b_ref[...]
            lf = jnp.minimum(xv, 0.0) - jnp.log(1.0 + jnp.exp(-jnp.abs(xv)))
            cs = jnp.dot(lf, tri, precision=lax.Precision.HIGHEST, preferred_element_type=F32) + carry
            c_ref[:, k * ch:(k + 1) * ch] = cs
            carry = cs[:, ch - 1:ch]

    return pl.pallas_call(kern, out_shape=jax.ShapeDtypeStruct((8, n_tok), F32), name="decay_cumsum",
                          compiler_params=_params(0))(f_t, b_col)


def _decay_bwd(cs, rs, f_t, b_col):
    n_tok = f_t.shape[1]
    ch = min(BWD_SCAN_CHUNK, n_tok)
    n_ch = n_tok // ch

    def kern(cs_ref, rs_ref, f_ref, b_ref, df_ref, db_ref, carry_ref):
        k = pl.program_id(0)

        @pl.when(k == 0)
        def _():
            carry_ref[...] = jnp.zeros_like(carry_ref)
            db_ref[...] = jnp.zeros_like(db_ref)

        r = lax.broadcasted_iota(jnp.int32, (ch, ch), 0)
        c = lax.broadcasted_iota(jnp.int32, (ch, ch), 1)
        tri = (r >= c).astype(F32)
        head = lax.broadcasted_iota(jnp.int32, (8, 4 * LANES), 0)
        lane = lax.broadcasted_iota(jnp.int32, (8, 4 * LANES), 1)
        pick = (lane == HEAD_DIM * head).astype(F32)
        dc = lax.dot_general(pick, rs_ref[...] - cs_ref[...], _NT, precision=lax.Precision.HIGHEST,
                             preferred_element_type=F32)
        rc = jnp.dot(dc, tri, precision=lax.Precision.HIGHEST, preferred_element_type=F32) + carry_ref[:, 0:1]
        carry_ref[...] = jnp.broadcast_to(rc[:, 0:1], carry_ref.shape)
        df = rc / (1.0 + jnp.exp(f_ref[...] + b_ref[...]))
        df_ref[...] = df
        db_ref[...] += jnp.broadcast_to(jnp.sum(df, axis=1, keepdims=True), db_ref.shape)

    back = lambda k: n_ch - 1 - k
    wide = pl.BlockSpec((ch, 4 * LANES), lambda k: (back(k), 0))
    row = pl.BlockSpec((8, ch), lambda k: (0, back(k)))
    return pl.pallas_call(
        kern, grid=(n_ch,),
        in_specs=[wide, wide, row, pl.BlockSpec((8, 1), lambda k: (0, 0))],
        out_specs=[row, pl.BlockSpec((8, LANES), lambda k: (0, 0))],
        out_shape=[jax.ShapeDtypeStruct((8, n_tok), F32), jax.ShapeDtypeStruct((8, LANES), F32)],
        scratch_shapes=[pltpu.VMEM((8, LANES), F32)], name="decay_bwd", compiler_params=_params(1))(cs, rs, f_t, b_col)


def _swa_bias_table():
    row = jnp.arange(SWA_BLOCK)[:, None] + SWA_BLOCK
    col = jnp.arange(2 * SWA_BLOCK)[None, :]
    cd = (row >> CHUNK_SHIFT) - (col >> CHUNK_SHIFT)
    band = (cd >= 0) & (cd <= WINDOW_CHUNKS)
    slopes = jnp.asarray([2.0 ** -(h + 1) for h in range(SWA_HEADS)], F32)
    bias = -slopes[:, None, None] * jnp.abs(row - col).astype(F32)[None]
    return jnp.stack([jnp.where(band & (col >= SWA_BLOCK), bias, NEG), jnp.where(band, bias, NEG)])


SWA_PER_STEP = 4


def _swap_halves(t):
    return pltpu.roll(t.astype(F32), HEAD_DIM, axis=1).astype(t.dtype)


def _swa_specs():
    blk, rows = SWA_BLOCK, SWA_PER_STEP * SWA_BLOCK
    q = pl.BlockSpec((rows, 4 * LANES), lambda n: (n, 0))
    before = lambda col: pl.BlockSpec((blk, LANES), lambda n: (jnp.maximum(SWA_PER_STEP * n - 1, 0), col))
    own = lambda col: pl.BlockSpec((rows, LANES), lambda n: (n, col))
    bias = pl.BlockSpec((2, SWA_HEADS, blk, 2 * blk), lambda n: (0, 0, 0, 0))
    return [q, before(4), own(4), before(5), own(5), bias]


def _swa_band(before_ref, own_ref):
    both = jnp.concatenate([before_ref[...], own_ref[...]], axis=0)
    return both, _swap_halves(both)


def _swa_bias(bias_ref, n, b):
    return bias_ref.at[jnp.minimum(n, 1)] if b == 0 else bias_ref.at[1]


SWA_GROUPS = ([h for h in range(SWA_HEADS) if h % 2 == h // 4], [h for h in range(SWA_HEADS) if h % 2 != h // 4])


def _stack_heads(ref, rows, heads, lo, mask_halves):
    tiles = []
    for h in heads:
        t = ref[rows, (h // 2) * LANES:(h // 2 + 1) * LANES]
        tiles.append(jnp.where(lo if h % 2 == 0 else ~lo, t, jnp.zeros_like(t)) if mask_halves else t)
    return jnp.concatenate(tiles, axis=0)


def _per_head_column(values, heads):
    return jnp.concatenate([jnp.full((SWA_BLOCK, 1), values(h), F32) for h in heads], axis=0)


def _swa_scores(q_ref, rows, kx, heads, lo, bias):
    qa = _stack_heads(q_ref, rows, heads, lo, True) * SCALE
    return qa, _dot_nt(qa, kx) + jnp.concatenate([bias[h] for h in heads], axis=0)


def _swa_fwd(zm, sinks):
    n_tok = zm.shape[0]
    blk, step_rows = SWA_BLOCK, SWA_PER_STEP * SWA_BLOCK

    def kern(q_ref, kp_ref, kc_ref, vp_ref, vc_ref, bias_ref, sink_ref, o_ref, lse_ref):
        n = pl.program_id(0)
        (k_all, k_all_sw), (v_all, v_all_sw) = _swa_band(kp_ref, kc_ref), _swa_band(vp_ref, vc_ref)
        lane = lax.broadcasted_iota(jnp.int32, (blk, LANES), 1)
        lo = lane < HEAD_DIM
        for b in range(SWA_PER_STEP):
            rows, band = slice(b * blk, (b + 1) * blk), slice(b * blk, (b + 2) * blk)
            bias = _swa_bias(bias_ref, n, b)
            lse_t = jnp.zeros((blk, LANES), F32)
            for pair in range(SWA_HEADS // 2):
                q2 = q_ref[rows, pair * LANES:(pair + 1) * LANES]
                outs = []
                for a in range(2):
                    h = 2 * pair + a
                    qa = jnp.where(lo if a == 0 else ~lo, q2, jnp.zeros_like(q2)) * SCALE
                    kx, vx = (k_all[band], v_all[band]) if h in SWA_GROUPS[0] else (k_all_sw[band], v_all_sw[band])
                    s = _dot_nt(qa, kx) + bias[h]
                    sink = sink_ref[h]
                    m = jnp.maximum(jnp.max(s, axis=-1, keepdims=True), sink)
                    e = jnp.exp(s - m)
                    l = jnp.sum(e, axis=-1, keepdims=True) + jnp.exp(sink - m)
                    pn = (e * (1.0 / l)).astype(BF16)
                    outs.append(_dot(pn, vx))
                    lse_t = jnp.where(lane == h, m + jnp.log(l), lse_t)
                o_ref[rows, pair * LANES:(pair + 1) * LANES] = jnp.where(lo, outs[0], outs[1]).astype(BF16)
            lse_ref[rows, :] = lse_t

    return pl.pallas_call(
        kern, grid=(n_tok // step_rows,),
        in_specs=_swa_specs() + [pl.BlockSpec(memory_space=pltpu.SMEM)],
        out_specs=[pl.BlockSpec((step_rows, 4 * LANES), lambda n: (n, 0)),
                   pl.BlockSpec((step_rows, LANES), lambda n: (n, 0))],
        out_shape=[jax.ShapeDtypeStruct((n_tok, 4 * LANES), BF16), jax.ShapeDtypeStruct((n_tok, LANES), F32)],
        name="swa_fwd", compiler_params=_params(1))(zm, zm, zm, zm, zm, _swa_bias_table(), sinks)


def _swa_bwd(zm, sinks, d_out, out, lse):
    n_tok = zm.shape[0]
    blk, step_rows = SWA_BLOCK, SWA_PER_STEP * SWA_BLOCK

    def kern(q_ref, kp_ref, kc_ref, vp_ref, vc_ref, bias_ref, do_ref, o_ref, lse_ref, sink_ref,
             dq_ref, dkp_ref, dkc_ref, dvp_ref, dvc_ref, dsk_ref):
        n = pl.program_id(0)

        @pl.when(n == 0)
        def _():
            dsk_ref[...] = jnp.zeros_like(dsk_ref)

        bands = (_swa_band(kp_ref, kc_ref), _swa_band(vp_ref, vc_ref))
        lane = lax.broadcasted_iota(jnp.int32, (blk, LANES), 1)
        lo = lane < HEAD_DIM
        for b in range(SWA_PER_STEP):
            rows, band = slice(b * blk, (b + 1) * blk), slice(b * blk, (b + 2) * blk)
            bias = _swa_bias(bias_ref, n, b)
            lse_t = lse_ref[rows, :]
            dqs, dkv = {}, []
            for g, heads in enumerate(SWA_GROUPS):
                kx, vx = bands[0][g][band], bands[1][g][band]
                qa, s = _swa_scores(q_ref, rows, kx, heads, lo, bias)
                doa = _stack_heads(do_ref, rows, heads, lo, True)
                lse_g = jnp.concatenate([lse_t[:, h:h + 1] for h in heads], axis=0)
                prob = jnp.exp(s - lse_g)
                o_g = _stack_heads(o_ref, rows, heads, lo, False)
                dd = jnp.sum(doa.astype(F32) * o_g.astype(F32), axis=-1, keepdims=True)
                ds = (prob * (_dot_nt(doa, vx) - dd)).astype(BF16)
                sink_part = -jnp.exp(_per_head_column(lambda h: sink_ref[h], heads) - lse_g) * dd
                dq = _dot(ds, kx) * SCALE
                for r, h in enumerate(heads):
                    dqs[h] = dq[r * blk:(r + 1) * blk]
                    dsk_ref[h:h + 1, :] += jnp.broadcast_to(
                        jnp.sum(sink_part[r * blk:(r + 1) * blk], axis=0, keepdims=True), (1, LANES))
                dkv.append((_dot_tn(ds, qa), _dot_tn(prob.astype(BF16), doa)))
            for pair in range(SWA_HEADS // 2):
                dq_ref[rows, pair * LANES:(pair + 1) * LANES] = jnp.where(
                    lo, dqs[2 * pair], dqs[2 * pair + 1]).astype(BF16)
            dk = dkv[0][0] + pltpu.roll(dkv[1][0], HEAD_DIM, axis=1)
            dv = dkv[0][1] + pltpu.roll(dkv[1][1], HEAD_DIM, axis=1)
            dkp_ref[rows, :] = dk[0:blk]
            dkc_ref[rows, :] = dk[blk:2 * blk]
            dvp_ref[rows, :] = dv[0:blk]
            dvc_ref[rows, :] = dv[blk:2 * blk]

    wide = pl.BlockSpec((step_rows, 4 * LANES), lambda n: (n, 0))
    narrow = pl.BlockSpec((step_rows, LANES), lambda n: (n, 0))
    part = jax.ShapeDtypeStruct((n_tok, LANES), F32)
    return pl.pallas_call(
        kern, grid=(n_tok // step_rows,),
        in_specs=_swa_specs() + [wide, wide, narrow, pl.BlockSpec(memory_space=pltpu.SMEM)],
        out_specs=[wide, narrow, narrow, narrow, narrow, pl.BlockSpec((8, LANES), lambda n: (0, 0))],
        out_shape=[jax.ShapeDtypeStruct((n_tok, 4 * LANES), BF16), part, part, part, part,
                   jax.ShapeDtypeStruct((8, LANES), F32)],
        name="swa_bwd", compiler_params=_params(1))(zm, zm, zm, zm, zm, _swa_bias_table(), d_out, out, lse, sinks)


def _my_pos():
    return lax.axis_index("x"), lax.axis_index("y"), lax.axis_index("c")


def _peer(k):
    x, y, c = _my_pos()
    px, py, pc = x ^ (k >> 2), y ^ ((k >> 1) & 1), c ^ (k & 1)
    return (px, py, pc), 4 * px + 2 * py + pc


def _gather_copies(x_refs, out_refs, send_sems, recv_sems, local_sems):
    x, y, c = _my_pos()
    my_id = 4 * x + 2 * y + c
    local = [pltpu.make_async_copy(x_refs[w], out_refs[w].at[my_id], local_sems.at[w]) for w in range(len(x_refs))]
    sends, arrivals = [], []
    for k in range(1, N_DEV):
        peer, peer_id = _peer(k)
        for w in range(len(x_refs)):
            sems = dict(send_sem=send_sems.at[7 * w + k - 1], recv_sem=recv_sems.at[7 * w + k - 1],
                        device_id=peer, device_id_type=MESH)
            sends.append(pltpu.make_async_remote_copy(src_ref=x_refs[w], dst_ref=out_refs[w].at[my_id], **sems))
            arrivals.append(pltpu.make_async_remote_copy(src_ref=x_refs[w], dst_ref=out_refs[w].at[peer_id], **sems))
    return local, sends, arrivals


def _scatter_copies(g_refs, part_refs, send_sems, recv_sems, local_sems):
    x, y, c = _my_pos()
    my_id = 4 * x + 2 * y + c
    local = [pltpu.make_async_copy(g_refs[w].at[my_id], part_refs[w].at[0], local_sems.at[w])
             for w in range(len(g_refs))]
    sends, arrivals = [], []
    for k in range(1, N_DEV):
        peer, peer_id = _peer(k)
        for w in range(len(g_refs)):
            sems = dict(send_sem=send_sems.at[7 * w + k - 1], recv_sem=recv_sems.at[7 * w + k - 1],
                        device_id=peer, device_id_type=MESH)
            sends.append(pltpu.make_async_remote_copy(src_ref=g_refs[w].at[peer_id], dst_ref=part_refs[w].at[k], **sems))
            arrivals.append(pltpu.make_async_remote_copy(src_ref=g_refs[w].at[my_id], dst_ref=part_refs[w].at[k], **sems))
    return local, sends, arrivals


def _start_copies(local, sends, arrivals):
    for cp in local + sends:
        cp.start()


def _finish_copies(local, sends, arrivals):
    for cp in arrivals:
        cp.wait_recv()
    for cp in sends:
        cp.wait_send()
    for cp in local:
        cp.wait()


def _exchange_scratch(n_arrays):
    return [pltpu.SemaphoreType.DMA((7 * n_arrays,)), pltpu.SemaphoreType.DMA((7 * n_arrays,)),
            pltpu.SemaphoreType.DMA((n_arrays,))]


class _Ride:
    def __init__(self, arrays, out_shape, copies):
        self.arrays, self.out_shape, self.copies = list(arrays), list(out_shape), copies
        any_spec = pl.BlockSpec(memory_space=pl.ANY)
        self.in_specs = [any_spec] * len(self.arrays)
        self.out_specs = [any_spec] * len(self.arrays)
        self.scratch = _exchange_scratch(len(self.arrays)) if self.arrays else []

    @staticmethod
    def _at(grid, last):
        hit = [pl.program_id(d) == (n - 1 if last else 0) for d, n in enumerate(grid)]
        return hit[0] if len(hit) == 1 else jnp.logical_and(*hit)

    def at_first_step(self, grid, in_refs, out_refs, sems):
        @pl.when(self._at(grid, False))
        def _():
            _start_copies(*self.copies(in_refs, out_refs, *sems))

    def at_last_step(self, grid, in_refs, out_refs, sems):
        @pl.when(self._at(grid, True))
        def _():
            _finish_copies(*self.copies(in_refs, out_refs, *sems))


_NO_RIDE = _Ride([], [], None)


def _gather_ride(shards):
    return _Ride(shards, [jax.ShapeDtypeStruct((N_DEV,) + s.shape, s.dtype) for s in shards], _gather_copies)


def _scatter_ride(grads):
    return _Ride(grads, [jax.ShapeDtypeStruct(g.shape, g.dtype) for g in grads], _scatter_copies)


Q_COL, K_COL, V_COL = 6, 10, 14


def _lane_tile(stat, width):
    return jnp.tile(stat, (1, width // LANES))


def _fox_steps(nq):
    steps = [(i2, j2, int(j2 == i2)) for i2 in range(nq // 2) for j2 in range(i2 + 1)]
    return [np.asarray(col, np.int32) for col in zip(*steps)]


_BELOW = [(0, 0, 2, None), (1, 0, 2, None)]
_BELOW_NEAR = [(0, 1, 2, None), (1, 1, 2, None)]
_DIAGONAL = [(0, 0, 1, 0), (1, 0, 2, 1)]


def _causal(t, first_row):
    row = lax.broadcasted_iota(jnp.int32, t.shape, 0) + first_row
    col = lax.broadcasted_iota(jnp.int32, t.shape, 1)
    return jnp.where(col <= row, t, NEG)


def _fox_dispatch(sweep, last, dead_ref, head0, idx):
    below = last == 0
    takes_all, takes_near = [], []
    for h in (head0, head0 + 1):
        far_live, near_live = dead_ref[h, idx] < 0.5, dead_ref[h, idx + 1] < 0.5
        takes_all.append(far_live)
        takes_near.append(jnp.logical_and(jnp.logical_not(far_live), near_live))
    joint = jnp.logical_and(takes_all[0], takes_all[1])
    pl.when(jnp.logical_and(below, joint))(lambda: sweep(_BELOW, (0, 1)))
    apart = jnp.logical_and(below, jnp.logical_not(joint))
    for a in range(2):
        pl.when(jnp.logical_and(apart, takes_all[a]))(lambda a=a: sweep(_BELOW, (a,)))
        pl.when(jnp.logical_and(below, takes_near[a]))(lambda a=a: sweep(_BELOW_NEAR, (a,)))
    pl.when(last == 1)(lambda: sweep(_DIAGONAL, (0, 1)))


EXP_ZERO = 104.5
NORM_SLACK = 1.005


def _fox_dead_steps(nrm, c_pairs, tq):
    nq = nrm.shape[0] // 8
    stats = nrm.reshape(nq, 8, LANES)[:, :3, :FOX_HEADS]
    qn, kn, own = jnp.sqrt(stats[:, 0]) * SCALE, jnp.sqrt(stats[:, 1]), stats[:, 2] * SCALE
    cb = c_pairs.reshape(FOX_HEADS, nq, tq)
    c_max, c_min = jnp.max(cb, axis=-1).T, jnp.min(cb, axis=-1).T
    both = lambda t, pick: pick(t.reshape(nq // 2, 2, FOX_HEADS), axis=1)
    qn2, kn2, c_max2, own2 = both(qn, jnp.max), both(kn, jnp.max), both(c_max, jnp.max), both(own, jnp.min)
    row_max_floor = own2 - (NORM_SLACK - 1.0) * qn2 * kn2 - c_max2
    gap = qn2[:, None] * kn[None] * NORM_SLACK - c_min[None] - row_max_floor[:, None]
    below = jnp.arange(nq)[None, :] < 2 * jnp.arange(nq // 2)[:, None]
    dead = jnp.logical_and(gap < -EXP_ZERO, below[..., None])
    return dead.transpose(2, 0, 1).reshape(FOX_HEADS, -1).astype(F32)


def _fox_fwd(zm, c_pairs, dead, tq, ride=None):
    n_tok = zm.shape[0]
    nq = n_tok // tq
    ii, jj, kk = _fox_steps(nq)
    n_steps = len(ii)
    n_ride = len(ride.arrays) if ride else 0

    def kern(ii_ref, jj_ref, kk_ref, q_ref, k_ref, v_ref, ck_ref, dead_ref, *more):
        ride_in, (o_ref, ln_ref), ride_out = more[:n_ride], more[n_ride:n_ride + 2], more[n_ride + 2:2 * n_ride + 2]
        qs_ref, m_ref, l_ref, acc_ref = more[2 * n_ride + 2:2 * n_ride + 6]
        step = pl.program_id(1)
        j, last = jj_ref[step], kk_ref[step]
        lo = lax.broadcasted_iota(jnp.int32, (2 * tq, LANES), 1) < HEAD_DIM
        if ride:
            ride.at_first_step((FOX_HEADS // 2, n_steps), ride_in, ride_out, more[2 * n_ride + 6:])

        @pl.when(j == 0)
        def _():
            q2 = q_ref[...]
            zq = jnp.zeros_like(q2)
            qs_ref[0] = jnp.where(lo, q2, zq) * SCALE
            qs_ref[1] = jnp.where(lo, zq, q2) * SCALE
            m_ref[...] = jnp.full(m_ref.shape, NEG, F32)
            l_ref[...] = jnp.zeros(l_ref.shape, F32)
            acc_ref[...] = jnp.zeros(acc_ref.shape, F32)

        def sweep(tiles, heads):
            v_ones = jnp.concatenate([v_ref[...], jnp.ones((2 * tq, LANES), BF16)], axis=1)
            for sub, k0, k1, diagonal in tiles:
                rows, keys = slice(sub * tq, (sub + 1) * tq), slice(k0 * tq, k1 * tq)
                for a in heads:
                    t = _dot_nt(qs_ref[a, rows], k_ref[keys, :]) - ck_ref[a:a + 1, keys]
                    if diagonal is not None:
                        t = _causal(t, diagonal * tq)
                    m_old = m_ref[a, rows]
                    m_new = jnp.maximum(m_old, jnp.max(t, axis=-1, keepdims=True))
                    alpha = jnp.exp(m_old - m_new)
                    e = jnp.exp(t - _lane_tile(m_new, (k1 - k0) * tq)).astype(BF16)
                    pv = _dot(e, v_ones[keys])
                    acc_ref[a, rows] = alpha * acc_ref[a, rows] + pv[:, :LANES]
                    l_ref[a, rows] = alpha * l_ref[a, rows] + pv[:, LANES:]
                    m_ref[a, rows] = m_new

        _fox_dispatch(sweep, last, dead_ref, 2 * pl.program_id(0), ii_ref[step] * nq + 2 * j)

        @pl.when(last == 1)
        def _():
            o_ref[...] = jnp.where(lo, acc_ref[0] / l_ref[0], acc_ref[1] / l_ref[1]).astype(BF16)
            ln_ref[:, :LANES] = m_ref[0] + jnp.log(l_ref[0])
            ln_ref[:, LANES:] = m_ref[1] + jnp.log(l_ref[1])

        if ride:
            ride.at_last_step((FOX_HEADS // 2, n_steps), ride_in, ride_out, more[2 * n_ride + 6:])

    blk = (2 * tq, LANES)
    by_i = lambda col: (lambda hp, s, ii, jj, kk: (ii[s], col + hp))
    by_j = lambda col: (lambda hp, s, ii, jj, kk: (jj[s], col + hp))
    extra = ride if ride else _NO_RIDE
    grid_spec = pltpu.PrefetchScalarGridSpec(
        num_scalar_prefetch=3, grid=(FOX_HEADS // 2, n_steps),
        in_specs=[pl.BlockSpec(blk, by_i(Q_COL)), pl.BlockSpec(blk, by_j(K_COL)), pl.BlockSpec(blk, by_j(V_COL)),
                  pl.BlockSpec((None, 2, 2 * tq), lambda hp, s, ii, jj, kk: (hp, 0, jj[s])),
                  pl.BlockSpec(memory_space=pltpu.SMEM)] + extra.in_specs,
        out_specs=[pl.BlockSpec(blk, by_i(0)), pl.BlockSpec((2 * tq, 2 * LANES), by_i(0))] + extra.out_specs,
        scratch_shapes=[pltpu.VMEM((2, 2 * tq, LANES), BF16), pltpu.VMEM((2, 2 * tq, LANES), F32),
                        pltpu.VMEM((2, 2 * tq, LANES), F32), pltpu.VMEM((2, 2 * tq, LANES), F32)] + extra.scratch)
    return pl.pallas_call(
        kern, grid_spec=grid_spec,
        out_shape=[jax.ShapeDtypeStruct((n_tok, 4 * LANES), BF16),
                   jax.ShapeDtypeStruct((n_tok, FOX_HEADS * LANES), F32)] + extra.out_shape,
        name="fox_fwd", compiler_params=_params(2))(ii, jj, kk, zm, zm, zm, c_pairs, dead, *extra.arrays)


def _fox_bwd(zm, c_pairs, dead, d_out, lnorm, delta, tq, ride=None):
    n_tok = zm.shape[0]
    nq = n_tok // tq
    ii, jj, kk = _fox_steps(nq)
    n_steps = len(ii)
    n_ride = len(ride.arrays) if ride else 0

    def kern(ii_ref, jj_ref, kk_ref, q_ref, k_ref, v_ref, ck_ref, dead_ref, do_ref, ln_ref, dl_ref, *more):
        ride_in, ride_out = more[:n_ride], more[n_ride + 5:2 * n_ride + 5]
        dq_ref, dk_out, dv_out, cs_ref, rs_ref = more[n_ride:n_ride + 5]
        qs_ref, qo_ref, dos_ref, dq_acc, dk_ref, dv_ref = more[2 * n_ride + 5:2 * n_ride + 11]
        step = pl.program_id(1)
        j, last = jj_ref[step], kk_ref[step]
        lo = lax.broadcasted_iota(jnp.int32, (2 * tq, LANES), 1) < HEAD_DIM
        if ride:
            ride.at_first_step((FOX_HEADS // 2, n_steps), ride_in, ride_out, more[2 * n_ride + 11:])

        @pl.when(step == 0)
        def _():
            dk_ref[...] = jnp.zeros_like(dk_ref)
            dv_ref[...] = jnp.zeros_like(dv_ref)
            cs_ref[...] = jnp.zeros_like(cs_ref)

        @pl.when(j == 0)
        def _():
            q2, do2 = q_ref[...], do_ref[...]
            zq = jnp.zeros_like(q2)
            ones = jnp.ones((2 * tq, LANES), BF16)
            for a in range(2):
                half = lo if a == 0 else ~lo
                qa = jnp.where(half, q2, zq) * SCALE
                qs_ref[a] = qa
                qo_ref[a] = jnp.concatenate([qa, ones], axis=1)
                dos_ref[a] = jnp.where(half, do2, zq)
            dq_acc[...] = jnp.zeros(dq_acc.shape, F32)

        def sweep(tiles, heads):
            k_ones = jnp.concatenate([k_ref[...], jnp.ones((2 * tq, LANES), BF16)], axis=1)
            sums = {}
            for sub, k0, k1, diagonal in tiles:
                rows, keys, n_keys = slice(sub * tq, (sub + 1) * tq), slice(k0 * tq, k1 * tq), (k1 - k0) * tq
                part = sums.setdefault((k0, k1), [0.0, 0.0, 0.0, 0.0])
                for a in heads:
                    t = _dot_nt(qs_ref[a, rows], k_ref[keys, :]) - ck_ref[a:a + 1, keys]
                    if diagonal is not None:
                        t = _causal(t, diagonal * tq)
                    prob = jnp.exp(t - _lane_tile(ln_ref[rows, a * LANES:(a + 1) * LANES], n_keys))
                    dp = _dot_nt(dos_ref[a, rows], v_ref[keys, :])
                    ds = (prob * (dp - _lane_tile(dl_ref[rows, a * LANES:(a + 1) * LANES], n_keys))).astype(BF16)
                    dq_acc[a, rows] += _dot(ds, k_ones[keys])
                    dk_cs = _dot_tn(ds, qo_ref[a, rows])
                    part[0] = part[0] + dk_cs[:, :LANES]
                    part[1] = part[1] + _dot_tn(prob.astype(BF16), dos_ref[a, rows])
                    part[2 + a] = part[2 + a] + dk_cs[:, LANES:]
            for (k0, k1), (dk, dv, cs0, cs1) in sums.items():
                keys = pl.ds(pl.multiple_of((2 * j + k0) * tq, tq), (k1 - k0) * tq)
                dk_ref[keys, :] += dk
                cs_ref[keys, :] += jnp.where(lo[:(k1 - k0) * tq], cs0, cs1)
                dv_ref[keys, :] += dv

        _fox_dispatch(sweep, last, dead_ref, 2 * pl.program_id(0), ii_ref[step] * nq + 2 * j)

        @pl.when(last == 1)
        def _():
            dq_ref[...] = (jnp.where(lo, dq_acc[0, :, :LANES], dq_acc[1, :, :LANES]) * SCALE).astype(BF16)
            rs_ref[...] = jnp.where(lo, dq_acc[0, :, LANES:], dq_acc[1, :, LANES:])

        @pl.when(step == n_steps - 1)
        def _():
            dk_out[...] = dk_ref[...].astype(BF16)
            dv_out[...] = dv_ref[...].astype(BF16)

        if ride:
            ride.at_last_step((FOX_HEADS // 2, n_steps), ride_in, ride_out, more[2 * n_ride + 11:])

    blk = (2 * tq, LANES)
    by_i = lambda col: (lambda hp, s, ii, jj, kk: (ii[s], col + hp))
    by_j = lambda col: (lambda hp, s, ii, jj, kk: (jj[s], col + hp))
    resident = pl.BlockSpec(blk, by_i(0))
    stat = pl.BlockSpec((2 * tq, 2 * LANES), by_i(0))
    whole = pl.BlockSpec((n_tok, LANES), lambda hp, s, ii, jj, kk: (0, hp))
    extra = ride if ride else _NO_RIDE
    grid_spec = pltpu.PrefetchScalarGridSpec(
        num_scalar_prefetch=3, grid=(FOX_HEADS // 2, n_steps),
        in_specs=[pl.BlockSpec(blk, by_i(Q_COL)), pl.BlockSpec(blk, by_j(K_COL)), pl.BlockSpec(blk, by_j(V_COL)),
                  pl.BlockSpec((None, 2, 2 * tq), lambda hp, s, ii, jj, kk: (hp, 0, jj[s])),
                  pl.BlockSpec(memory_space=pltpu.SMEM), resident, stat, stat] + extra.in_specs,
        out_specs=[resident, whole, whole, whole, resident] + extra.out_specs,
        scratch_shapes=[pltpu.VMEM((2, 2 * tq, LANES), BF16), pltpu.VMEM((2, 2 * tq, 2 * LANES), BF16),
                        pltpu.VMEM((2, 2 * tq, LANES), BF16), pltpu.VMEM((2, 2 * tq, 2 * LANES), F32),
                        pltpu.VMEM((n_tok, LANES), F32), pltpu.VMEM((n_tok, LANES), F32)] + extra.scratch)
    wide = lambda dt: jax.ShapeDtypeStruct((n_tok, 4 * LANES), dt)
    return pl.pallas_call(
        kern, grid_spec=grid_spec, name="fox_bwd",
        out_shape=[wide(BF16), wide(BF16), wide(BF16), wide(F32), wide(F32)] + extra.out_shape,
        compiler_params=_params(2, FOX_BWD_VMEM))(ii, jj, kk, zm, zm, zm, c_pairs, dead, d_out, lnorm, delta,
                                                  *extra.arrays)


def _all_gather(shards):
    n_w = len(shards)

    def kern(*refs):
        x_refs, out_refs = refs[:n_w], refs[n_w:2 * n_w]
        send_sems, recv_sems, local_sems = refs[2 * n_w:]
        x, y, c = _my_pos()
        me, sibling = (x, y, c), (x, y, 1 - c)
        chips = [(1 - x, y), (x, 1 - y), (1 - x, 1 - y)]

        def slot(w, px, py, pc):
            return out_refs[w].at[4 * px + 2 * py + pc]

        def copy(w, k, block, to, src=None):
            return pltpu.make_async_remote_copy(
                src_ref=slot(w, *block) if src is None else src, dst_ref=slot(w, *block),
                send_sem=send_sems.at[7 * w + k], recv_sem=recv_sems.at[7 * w + k], device_id=to, device_id_type=MESH)

        local, started = [], []
        for w in range(n_w):
            mine = pltpu.make_async_copy(x_refs[w], slot(w, *me), local_sems.at[w])
            mine.start()
            local.append(mine)
            first = [copy(w, 0, me, sibling, src=x_refs[w])]
            first += [copy(w, 1 + k, me, (*chip, c), src=x_refs[w]) for k, chip in enumerate(chips)]
            for cp in first:
                cp.start()
            started += first
        for k, chip in enumerate(chips):
            for w in range(n_w):
                copy(w, 1 + k, (*chip, c), me).wait_recv()
                passed = copy(w, 4 + k, (*chip, c), sibling)
                passed.start()
                started.append(passed)
        for w in range(n_w):
            copy(w, 0, sibling, me).wait_recv()
            for k, chip in enumerate(chips):
                copy(w, 4 + k, (*chip, 1 - c), me).wait_recv()
        for cp in started:
            cp.wait_send()
        for cp in local:
            cp.wait()

    any_spec = pl.BlockSpec(memory_space=pl.ANY)
    return pl.pallas_call(
        kern, out_shape=[jax.ShapeDtypeStruct((N_DEV,) + s.shape, s.dtype) for s in shards],
        in_specs=[any_spec] * n_w, out_specs=[any_spec] * n_w,
        scratch_shapes=[pltpu.SemaphoreType.DMA((7 * n_w,)), pltpu.SemaphoreType.DMA((7 * n_w,)),
                        pltpu.SemaphoreType.DMA((n_w,))],
        name="weight_all_gather")(*shards)


def _small_exchange(small):
    def kern(s_ref, sall_ref, *sems):
        copies = _gather_copies([s_ref], [sall_ref], *sems)
        _start_copies(*copies)
        _finish_copies(*copies)

    any_spec = pl.BlockSpec(memory_space=pl.ANY)
    return pl.pallas_call(
        kern, out_shape=jax.ShapeDtypeStruct((N_DEV,) + small.shape, small.dtype), in_specs=[any_spec],
        out_specs=any_spec, scratch_shapes=_exchange_scratch(1), name="small_grad_exchange")(small)


ADAMW_BLOCK_BYTES = 2 * 1024 * 1024


def _adamw(parts, w, m, v, name):
    n_parts, n_rows, n_cols = parts.shape
    limit = max(8, ADAMW_BLOCK_BYTES // (n_parts * n_cols * parts.dtype.itemsize))
    tr = max(t for t in range(8, n_rows + 1, 8) if n_rows % t == 0 and t <= limit)

    def kern(p_ref, w_ref, m_ref, v_ref, g_out, d_out, m_out, v_out):
        g = p_ref[0].astype(F32)
        for k in range(1, n_parts):
            g = g + p_ref[k].astype(F32)
        m_new = ADAM_B1 * m_ref[...] + (1.0 - ADAM_B1) * g
        v_new = ADAM_B2 * v_ref[...] + (1.0 - ADAM_B2) * jnp.square(g)
        m_hat = m_new / (1.0 - ADAM_B1 ** ADAM_STEP)
        v_hat = v_new / (1.0 - ADAM_B2 ** ADAM_STEP)
        g_out[...] = g
        d_out[...] = -ADAM_LR * (m_hat / (jnp.sqrt(v_hat) + ADAM_EPS) + ADAM_WD * w_ref[...])
        m_out[...] = m_new
        v_out[...] = v_new

    row = pl.BlockSpec((tr, n_cols), lambda i: (i, 0))
    out = jax.ShapeDtypeStruct((n_rows, n_cols), F32)
    return pl.pallas_call(
        kern, grid=(n_rows // tr,),
        in_specs=[pl.BlockSpec((n_parts, tr, n_cols), lambda i: (0, i, 0)), row, row, row],
        out_specs=[row, row, row, row], out_shape=[out, out, out, out], name=name,
        compiler_params=_params(1))(parts, w, m, v)


SHARDED = {
    "w_in": ((D_MODEL, D_IN), 1), "w_br_swa": ((512, D_MODEL), 1), "w_br_fox": ((512, D_MODEL), 1),
    "w_mix_out": ((D_MODEL, D_MODEL), 0), "w_ff1": ((D_MODEL, D_FF), 1), "w_ff2": ((D_FF, D_MODEL), 0),
    "w_ple_gate": ((D_MODEL, D_MODEL), 0), "w_ple_proj": ((PLE_DIM, D_MODEL), 1),
}
W_IN_SHARD = D_IN // N_DEV
W_IN_PAD = 640
SMALL = ("g_mix", "g_mlp", "g_ple", "g_final", "b_forget", "swa_sinks")
SMALL_COLS = 1024


def _wire_shard(name, a):
    a = a.reshape(a.shape[-2:])
    return jnp.pad(a.T, ((0, W_IN_PAD - W_IN_SHARD), (0, 0))) if name == "w_in" else a


def _from_wire(name, a):
    return (a[:W_IN_SHARD].T if name == "w_in" else a)[None]


def _w_all_from_wire(stacked):
    w_in = jnp.concatenate([stacked[d][:W_IN_SHARD] for d in range(N_DEV)], axis=0)
    fpad = jnp.zeros((N_FPAD - FOX_HEADS, D_MODEL), stacked.dtype)
    return jnp.concatenate([w_in[:N_MAIN + FOX_HEADS], fpad, w_in[N_MAIN + FOX_HEADS:]], axis=0)


def _dw_in_to_wire(dw_all):
    dw_in = jnp.concatenate([dw_all[:N_MAIN + FOX_HEADS], dw_all[N_MAIN + N_FPAD:]], axis=0)
    pad = jnp.zeros((W_IN_PAD - W_IN_SHARD, D_MODEL), dw_all.dtype)
    return jnp.stack([jnp.concatenate([dw_in[d * W_IN_SHARD:(d + 1) * W_IN_SHARD], pad], axis=0)
                      for d in range(N_DEV)])


def _pack_small(vals, scalar=None):
    rows = [jnp.pad(vals[n].reshape(-1), (0, SMALL_COLS - vals[n].size)) for n in SMALL]
    if scalar is not None:
        rows.append(jnp.pad(scalar.reshape(1), (0, SMALL_COLS - 1)))
    rows += [jnp.zeros((SMALL_COLS,), F32)] * (8 - len(rows))
    return jnp.stack(rows)


def _unpack_small(slab, like):
    return {n: slab[r, :like[n].size].reshape(like[n].shape) for r, n in enumerate(SMALL)}


def _local_step(x, p, tgt, w, small, tm, tq, ts, late_shards=None):
    n_tok = x.shape[0]
    row = lambda v: v.reshape(1, -1)
    g_mix, g_mlp, g_ple, g_fin = row(small["g_mix"]), row(small["g_mlp"]), row(small["g_ple"]), row(small["g_final"])
    sinks = small["swa_sinks"].reshape(-1)
    b_col = small["b_forget"].reshape(FOX_HEADS, 1)

    assert tm == tq
    u1, zm, zfg, zf, nrm = _in_proj(x, g_mix, w["w_all"], tm)
    f_t = zf[:, :FOX_HEADS].T
    c_pairs = _decay_cumsum(f_t, b_col).reshape(FOX_HEADS // 2, 2, n_tok)
    attn_a, lse_a = _swa_fwd(zm, sinks)
    dead = _fox_dead_steps(nrm, c_pairs, tq)
    if late_shards is None:
        attn_b, ln_b = _fox_fwd(zm, c_pairs, dead, tq)
    else:
        attn_b, ln_b, *late = _fox_fwd(zm, c_pairs, dead, tq, _gather_ride(list(late_shards.values())))
        w = {**w, **_gathered_to_local(dict(zip(late_shards, late)))}
    ya, yb, mixed, h1, u2, a, r, h2 = _mix_ffn_fwd(attn_a, attn_b, zfg, x, w["w_br_swa"], w["w_br_fox"],
                                                   w["w_mix_out"], g_mlp, w["w_ff1"], w["w_ff2"], tm // 2)

    dlg, dpp, u3, dh2, dh2b, da, loss_acc, dgf, dgp = _head_ffn_bwd(
        h2, p, tgt, a, g_ple, w["w_ple_gate"], w["w_ple_proj"], g_fin, w["w_ff2"], tm // 2)
    dh1, dh1b, dgl, dya, dyb, daa, dab, delta_b, dgm = _ffn_bwd_b(
        da, dh2, h1, ya, yb, zfg, attn_b, w["w_ff1"], g_mlp, w["w_mix_out"], w["w_br_swa"], w["w_br_fox"], tm // 2)
    dq_a, dkp, dkc, dvp, dvc, dsk = _swa_bwd(zm, sinks, daa, attn_a, lse_a)
    dw = {
        "w_br_swa": _matmul_tn(attn_a, dya, "dw_br_swa", ts, stack_cols=D_MODEL // N_DEV),
        "w_br_fox": _matmul_tn(attn_b, dyb, "dw_br_fox", ts, stack_cols=D_MODEL // N_DEV),
        "w_mix_out": _matmul_tn(mixed, dh1b, "dw_mix_out", ts),
        "w_ff1": _matmul_tn(u2, da, "dw_ff1", ts, stack_cols=D_FF // N_DEV),
        "w_ff2": _matmul_tn(r, dh2b, "dw_ff2", ts),
        "w_ple_gate": _matmul_tn(u3, dlg, "dw_ple_gate", ts),
        "w_ple_proj": _matmul_tn(p, dpp, "dw_ple_proj", ts, stack_cols=D_MODEL // N_DEV),
    }
    if late_shards is None:
        dq_b, dk_b, dv_b, cs, rs = _fox_bwd(zm, c_pairs, dead, dab, ln_b, delta_b, tq)
        late_parts = None
    else:
        wire = _local_to_wire(dw)
        dq_b, dk_b, dv_b, cs, rs, *parts = _fox_bwd(zm, c_pairs, dead, dab, ln_b, delta_b, tq,
                                                    _scatter_ride([wire[n] for n in late_shards]))
        late_parts = dict(zip(late_shards, parts))

    up = lambda t: jnp.concatenate([t[SWA_BLOCK:], jnp.zeros((SWA_BLOCK, LANES), F32)], axis=0)
    dk_a, dv_a = dkc + up(dkp), dvc + up(dvp)
    df_t, db = _decay_bwd(cs, rs, f_t, b_col)
    df = jnp.pad(df_t.T, ((0, 0), (0, N_FPAD - FOX_HEADS)))
    dz = jnp.concatenate([dq_a, dk_a.astype(BF16), dv_a.astype(BF16), dq_b, dk_b, dv_b,
                          df.astype(BF16), dgl], axis=1)
    dw["w_all"] = _matmul_tn(dz, u1, "dw_in", ts)
    if late_shards is None:
        dx, dgx = _in_proj_bwd(dz, dh1, x, w["w_all"], g_mix, tm)
    else:
        dx, dgx, late_parts["w_in"] = _in_proj_bwd(dz, dh1, x, w["w_all"], g_mix, tm,
                                                   _scatter_ride([_dw_in_to_wire(dw["w_all"])]))
    dsmall = {"g_mix": dgx[0], "g_mlp": dgm[0], "g_ple": dgp[0], "g_final": dgf[0],
              "b_forget": db[:, 0], "swa_sinks": dsk[:, 0]}
    return loss_acc[0, 0], dx, dw, dsmall, late_parts


_ROWS = lambda t: t.reshape(-1, t.shape[-1])
_BY_ROWS = lambda t: t.reshape(N_DEV, t.shape[0] // N_DEV, t.shape[1])
_SAME = lambda t: t
LOCAL_LAYOUT = {
    "w_in": ("w_all", _w_all_from_wire, _dw_in_to_wire), "w_br_swa": ("w_br_swa", _SAME, _SAME),
    "w_br_fox": ("w_br_fox", _SAME, _SAME), "w_mix_out": ("w_mix_out", _ROWS, _BY_ROWS),
    "w_ff1": ("w_ff1", _SAME, _SAME), "w_ff2": ("w_ff2", _SAME, _BY_ROWS),
    "w_ple_gate": ("w_ple_gate", _ROWS, _BY_ROWS), "w_ple_proj": ("w_ple_proj", _SAME, _SAME),
}


def _gathered_to_local(g):
    return {LOCAL_LAYOUT[n][0]: LOCAL_LAYOUT[n][1](t) for n, t in g.items()}


def _local_to_wire(dw):
    names = {local: n for n, (local, _, _) in LOCAL_LAYOUT.items()}
    return {names[local]: LOCAL_LAYOUT[names[local]][2](t) for local, t in dw.items()}


def kernel(x, p, g_mix, w_in, b_forget, swa_sinks, w_br_swa, w_br_fox, w_mix_out, g_mlp, w_ff1, w_ff2, g_ple, w_ple_gate, w_ple_proj, g_final, loss_target, m_g_mix, m_w_in, m_b_forget, m_swa_sinks, m_w_br_swa, m_w_br_fox, m_w_mix_out, m_g_mlp, m_w_ff1, m_w_ff2, m_g_ple, m_w_ple_gate, m_w_ple_proj, m_g_final, v_g_mix, v_w_in, v_b_forget, v_swa_sinks, v_w_br_swa, v_w_br_fox, v_w_mix_out, v_g_mlp, v_w_ff1, v_w_ff2, v_g_ple, v_w_ple_gate, v_w_ple_proj, v_g_final):
    given = dict(g_mix=g_mix, w_in=w_in, b_forget=b_forget, swa_sinks=swa_sinks, w_br_swa=w_br_swa, w_br_fox=w_br_fox,
                 w_mix_out=w_mix_out, g_mlp=g_mlp, w_ff1=w_ff1, w_ff2=w_ff2, g_ple=g_ple, w_ple_gate=w_ple_gate,
                 w_ple_proj=w_ple_proj, g_final=g_final)
    mom = dict(g_mix=m_g_mix, w_in=m_w_in, b_forget=m_b_forget, swa_sinks=m_swa_sinks, w_br_swa=m_w_br_swa,
               w_br_fox=m_w_br_fox, w_mix_out=m_w_mix_out, g_mlp=m_g_mlp, w_ff1=m_w_ff1, w_ff2=m_w_ff2, g_ple=m_g_ple,
               w_ple_gate=m_w_ple_gate, w_ple_proj=m_w_ple_proj, g_final=m_g_final)
    vel = dict(g_mix=v_g_mix, w_in=v_w_in, b_forget=v_b_forget, swa_sinks=v_swa_sinks, w_br_swa=v_w_br_swa,
               w_br_fox=v_w_br_fox, w_mix_out=v_w_mix_out, g_mlp=v_g_mlp, w_ff1=v_w_ff1, w_ff2=v_w_ff2, g_ple=v_g_ple,
               w_ple_gate=v_w_ple_gate, w_ple_proj=v_w_ple_proj, g_final=v_g_final)
    names = list(given)
    sharded = list(SHARDED)

    w_wire = {n: _wire_shard(n, given[n]) for n in sharded}
    late = [n for n in sharded if n != "w_in"]
    gathered = _all_gather([w_wire["w_in"].astype(BF16)])
    local_w = _gathered_to_local({"w_in": gathered[0]})
    small = {n: given[n].reshape(-1) for n in SMALL}

    n_tok = x.shape[1]
    tile = min(TOKEN_TILE, n_tok // 4)
    loss_part, dx, dw, dsmall, parts = _local_step(
        x[0], p[0, 0], loss_target[0], local_w, small, tm=tile, tq=tile, ts=min(DW_TOKENS_PER_STEP, n_tok // 4),
        late_shards={n: w_wire[n].astype(BF16) for n in late})
    small_all = _small_exchange(_pack_small(dsmall, loss_part))

    res = {}
    for n in sharded:
        part = parts[n]
        flat = part.reshape(N_DEV, -1, part.shape[-1])
        outs = _adamw(flat, w_wire[n], _wire_shard(n, mom[n]), _wire_shard(n, vel[n]), "adamw_" + n)
        res[n] = [_from_wire(n, o) for o in outs]
    outs_s = _adamw(small_all, _pack_small(small), _pack_small({n: mom[n] for n in SMALL}),
                    _pack_small({n: vel[n] for n in SMALL}), "adamw_small")
    small_res = [_unpack_small(o, given) for o in outs_s]
    loss = outs_s[0][len(SMALL), 0]

    groups = [[res[n][k] if n in res else small_res[k][n] for n in names] for k in range(4)]
    return (loss, dx[None], *groups[0], *groups[1], *groups[2], *groups[3])
```

```python
import numpy as np
import jax
import jax.numpy as jnp
from jax import lax
from jax.experimental import pallas as pl
from jax.experimental.pallas import tpu as pltpu

F32 = jnp.float32
BF16 = jnp.bfloat16

D_MODEL = 1024
HEAD_DIM = 64
SWA_HEADS = 8
FOX_HEADS = 8
CHUNK_SHIFT = 6
SWA_BLOCK = 128
WINDOW_CHUNKS = 2
D_FF = 4096
PLE_DIM = 256
RMS_EPS = 1e-6
N_MAIN = 2304
N_FPAD = 128
N_GATE = 2048
D_IN = N_MAIN + FOX_HEADS + N_GATE
SCALE = HEAD_DIM ** -0.5
NEG = -1e30

ADAM_LR = 0.001
ADAM_B1 = 0.9
ADAM_B2 = 0.999
ADAM_EPS = 1e-08
ADAM_WD = 0.01
ADAM_STEP = 10

N_DEV = 8
TOKEN_TILE = 512
DW_TOKENS_PER_STEP = 2048
LANES = 128
V7X_VMEM_BYTES = 64 * 1024 * 1024
VMEM_LIMIT = V7X_VMEM_BYTES * 3 // 4
FOX_BWD_VMEM = V7X_VMEM_BYTES * 7 // 8
MESH = pl.DeviceIdType.MESH

_NT = (((1,), (1,)), ((), ()))
_TN = (((0,), (0,)), ((), ()))


def _params(n_grid, vmem_limit=VMEM_LIMIT):
    return pltpu.CompilerParams(dimension_semantics=("arbitrary",) * n_grid, vmem_limit_bytes=vmem_limit)


def _chunks(n, step):
    return [(s, min(step, n - s)) for s in range(0, n, step)]


def _sigmoid(x):
    return 1.0 / (1.0 + jnp.exp(-x))


def _dot(a, b):
    return jnp.dot(a, b, preferred_element_type=F32)


def _dot_nt(a, b):
    return lax.dot_general(a, b, _NT, preferred_element_type=F32)


def _dot_tn(a, b):
    return lax.dot_general(a, b, _TN, preferred_element_type=F32)


def _lane_concat(stacked_ref):
    return jnp.concatenate([stacked_ref[d] for d in range(N_DEV)], axis=1)


def _rms(h):
    return lax.rsqrt(jnp.mean(h * h, axis=-1, keepdims=True) + RMS_EPS)


def _rms_bwd(h, g, du):
    rs = _rms(h)
    n = h * rs
    dn = du * g
    dh = rs * (dn - n * jnp.mean(dn * n, axis=-1, keepdims=True))
    return dh, jnp.sum(du * n, axis=0, keepdims=True)


def _acc_rows(ref, i, row):
    @pl.when(i == 0)
    def _():
        ref[...] = jnp.zeros_like(ref)
    ref[...] += jnp.broadcast_to(row, ref.shape)


def _row_call(body, name, n_rows, tm, row_ins, const_ins, row_outs, acc_outs, ride=None, tile_outs=()):
    row_outs = list(row_outs)
    n_ri, n_ci, n_ro, n_ao = len(row_ins), len(const_ins), len(row_outs) + len(tile_outs), len(acc_outs)
    extra = ride if ride else _NO_RIDE
    n_ride = len(extra.arrays)
    grid = (n_rows // tm,)

    def kern(*refs):
        i = pl.program_id(0)
        ins, refs = refs[:n_ri + n_ci], refs[n_ri + n_ci:]
        ride_in, refs = refs[:n_ride], refs[n_ride:]
        outs, refs = refs[:n_ro + n_ao], refs[n_ro + n_ao:]
        ride_out, sems = refs[:n_ride], refs[n_ride:]
        if ride:
            ride.at_first_step(grid, ride_in, ride_out, sems)
        body(i, ins[:n_ri], ins[n_ri:], outs[:n_ro], outs[n_ro:])
        if ride:
            ride.at_last_step(grid, ride_in, ride_out, sems)

    def whole(a):
        zeros = (0,) * a.ndim
        return pl.BlockSpec(a.shape, lambda i: zeros, pipeline_mode=pl.Buffered(1))

    in_specs = [pl.BlockSpec((tm, a.shape[1]), lambda i: (i, 0)) for a in row_ins]
    in_specs += [whole(a) for a in const_ins] + extra.in_specs
    out_specs = [pl.BlockSpec((tm, c), lambda i: (i, 0)) for c, _ in row_outs]
    out_specs += [pl.BlockSpec((8, c), lambda i: (i, 0)) for c in tile_outs]
    out_specs += [pl.BlockSpec((8, c), lambda i: (0, 0)) for c in acc_outs] + extra.out_specs
    out_shape = [jax.ShapeDtypeStruct((n_rows, c), dt) for c, dt in row_outs]
    out_shape += [jax.ShapeDtypeStruct((8 * grid[0], c), F32) for c in tile_outs]
    out_shape += [jax.ShapeDtypeStruct((8, c), F32) for c in acc_outs] + extra.out_shape
    return pl.pallas_call(kern, grid=grid, in_specs=in_specs, out_specs=out_specs, out_shape=out_shape,
                          scratch_shapes=extra.scratch, name=name,
                          compiler_params=_params(1))(*row_ins, *const_ins, *extra.arrays)


def _in_proj(x, g_mix, w_all, tm):
    def body(i, ins, consts, outs, accs):
        x_ref, = ins
        g_ref, w_ref = consts
        u_ref, zm_ref, zfg_ref, zf_ref, nrm_ref = outs
        xv = x_ref[...]
        u = ((xv * _rms(xv)) * g_ref[...]).astype(BF16)
        u_ref[...] = u
        for s, n in _chunks(N_MAIN, 768):
            zm_ref[:, s:s + n] = _dot_nt(u, w_ref[s:s + n, :]).astype(BF16)
        for s, n in _chunks(N_FPAD + N_GATE, 512):
            zfg_ref[:, s:s + n] = _dot_nt(u, w_ref[N_MAIN + s:N_MAIN + s + n, :])
        zf_ref[...] = zfg_ref[:, :N_FPAD]
        lane = lax.broadcasted_iota(jnp.int32, (4 * LANES, LANES), 0)
        head = lax.broadcasted_iota(jnp.int32, (4 * LANES, LANES), 1)
        pick = (lane // HEAD_DIM == head).astype(BF16)
        tq_, tk_ = (zm_ref[:, col * LANES:(col + 4) * LANES].astype(F32) for col in (Q_COL, K_COL))
        rows = [jnp.max(_dot((t * t).astype(BF16), pick), axis=0, keepdims=True) for t in (tq_, tk_)]
        rows.append(jnp.min(_dot((tq_ * tk_).astype(BF16), pick), axis=0, keepdims=True))
        nrm_ref[...] = jnp.concatenate(rows + [jnp.zeros((5, LANES), F32)], axis=0)

    *outs, nrm = _row_call(body, "in_proj", x.shape[0], tm, [x], [g_mix, w_all],
                           [(D_MODEL, BF16), (N_MAIN, BF16), (N_FPAD + N_GATE, F32), (N_FPAD, F32)], [],
                           tile_outs=[LANES])
    return (*outs, nrm)


def _mix_ffn_fwd(attn_a, attn_b, zfg, x, w_sa, w_fo, w_mo, g_mlp, w1s, w2s, tm):
    ch = D_FF // N_DEV

    def body(i, ins, consts, outs, accs):
        aa_ref, ab_ref, zfg_ref, x_ref = ins
        wsa_ref, wfo_ref, wmo_ref, g_ref, w1_ref, w2_ref = consts
        ya_ref, yb_ref, mx_ref, h1_ref, u2_ref, a_ref, r_ref, h2_ref = outs
        ya = _dot(aa_ref[...], _lane_concat(wsa_ref))
        yb = _dot(ab_ref[...], _lane_concat(wfo_ref))
        g0 = _sigmoid(zfg_ref[:, N_FPAD:N_FPAD + D_MODEL])
        g1 = _sigmoid(zfg_ref[:, N_FPAD + D_MODEL:N_FPAD + 2 * D_MODEL])
        mixed = (g0 * ya + g1 * yb).astype(BF16)
        ya_ref[...] = ya.astype(BF16)
        yb_ref[...] = yb.astype(BF16)
        mx_ref[...] = mixed
        h1 = x_ref[...] + _dot(mixed, wmo_ref[...])
        h1_ref[...] = h1
        u = ((h1 * _rms(h1)) * g_ref[...]).astype(BF16)
        u2_ref[...] = u
        acc = h1
        for c in range(N_DEV):
            a = _dot(u, w1_ref[c])
            a_ref[:, c * ch:(c + 1) * ch] = a.astype(BF16)
            r = jnp.square(jnp.maximum(a, 0.0)).astype(BF16)
            r_ref[:, c * ch:(c + 1) * ch] = r
            acc = acc + _dot(r, w2_ref[c])
        h2_ref[...] = acc

    return _row_call(body, "mix_ffn_fwd", x.shape[0], tm, [attn_a, attn_b, zfg, x],
                     [w_sa, w_fo, w_mo, g_mlp, w1s, w2s],
                     [(D_MODEL, BF16), (D_MODEL, BF16), (D_MODEL, BF16), (D_MODEL, F32), (D_MODEL, BF16),
                      (D_FF, BF16), (D_FF, BF16), (D_MODEL, F32)], [])


def _head_ffn_bwd(h2, p, tgt, a, g_ple, w_pg, w_pp, g_fin, w2s, tm):
    ch = D_FF // N_DEV

    def body(i, ins, consts, outs, accs):
        h2_ref, p_ref, t_ref, a_ref = ins
        gp_ref, wpg_ref, wpp_ref, gf_ref, w2_ref = consts
        dlg_ref, dpp_ref, u3_ref, dh2_ref, dh2b_ref, da_ref = outs
        loss_ref, dgf_ref, dgp_ref = accs
        h2 = h2_ref[...]
        gp = gp_ref[...]
        u3 = ((h2 * _rms(h2)) * gp).astype(BF16)
        u3_ref[...] = u3
        pg = _sigmoid(_dot(u3, wpg_ref[...]))
        pp = _dot(p_ref[...].astype(BF16), _lane_concat(wpp_ref))
        h3 = h2 + pg * pp
        rs3 = _rms(h3)
        n3 = h3 * rs3
        gf = gf_ref[...]
        err = n3 * gf - t_ref[...]
        row_loss = 0.5 * jnp.mean(err * err, axis=-1, keepdims=True)
        _acc_rows(loss_ref, i, jnp.broadcast_to(jnp.sum(row_loss, axis=0, keepdims=True), (1, LANES)))
        dy = err * (1.0 / D_MODEL)
        _acc_rows(dgf_ref, i, jnp.sum(dy * n3, axis=0, keepdims=True))
        dn = dy * gf
        dh3 = rs3 * (dn - n3 * jnp.mean(dn * n3, axis=-1, keepdims=True))
        dpp_ref[...] = (dh3 * pg).astype(BF16)
        dlg = ((dh3 * pp) * pg * (1.0 - pg)).astype(BF16)
        dlg_ref[...] = dlg
        dh, dg = _rms_bwd(h2, gp, _dot_nt(dlg, wpg_ref[...]))
        _acc_rows(dgp_ref, i, dg)
        dh2 = dh3 + dh
        dh2_ref[...] = dh2
        dh2b = dh2.astype(BF16)
        dh2b_ref[...] = dh2b
        for c in range(N_DEV):
            dr = _dot_nt(dh2b, w2_ref[c])
            av = a_ref[:, c * ch:(c + 1) * ch].astype(F32)
            da_ref[:, c * ch:(c + 1) * ch] = (dr * (2.0 * jnp.maximum(av, 0.0))).astype(BF16)

    return _row_call(body, "head_ffn_bwd", h2.shape[0], tm, [h2, p, tgt, a], [g_ple, w_pg, w_pp, g_fin, w2s],
                     [(D_MODEL, BF16), (D_MODEL, BF16), (D_MODEL, BF16), (D_MODEL, F32), (D_MODEL, BF16),
                      (D_FF, BF16)], [LANES, D_MODEL, D_MODEL])


def _ffn_bwd_b(da, dh2, h1, ya, yb, zfg, attn_b, w1s, g_mlp, w_mo, w_sa, w_fo, tm):
    ch = D_FF // N_DEV

    def body(i, ins, consts, outs, accs):
        da_ref, dh2_ref, h1_ref, ya_ref, yb_ref, zfg_ref, ob_ref = ins
        w1_ref, gm_ref, wmo_ref, wsa_ref, wfo_ref = consts
        dh1_ref, dh1b_ref, dgl_ref, dya_ref, dyb_ref, daa_ref, dab_ref, dl_ref = outs
        dgm_ref, = accs
        du2 = _dot_nt(da_ref[:, 0:ch], w1_ref[0])
        for c in range(1, N_DEV):
            du2 = du2 + _dot_nt(da_ref[:, c * ch:(c + 1) * ch], w1_ref[c])
        dh, dg = _rms_bwd(h1_ref[...], gm_ref[...], du2)
        _acc_rows(dgm_ref, i, dg)
        dh1 = dh2_ref[...] + dh
        dh1_ref[...] = dh1
        dh1b = dh1.astype(BF16)
        dh1b_ref[...] = dh1b
        dmx = _dot_nt(dh1b, wmo_ref[...])
        g0 = _sigmoid(zfg_ref[:, N_FPAD:N_FPAD + D_MODEL])
        g1 = _sigmoid(zfg_ref[:, N_FPAD + D_MODEL:N_FPAD + 2 * D_MODEL])
        dya = (dmx * g0).astype(BF16)
        dyb = (dmx * g1).astype(BF16)
        dya_ref[...] = dya
        dyb_ref[...] = dyb
        dgl_ref[:, 0:D_MODEL] = ((dmx * ya_ref[...].astype(F32)) * g0 * (1.0 - g0)).astype(BF16)
        dgl_ref[:, D_MODEL:2 * D_MODEL] = ((dmx * yb_ref[...].astype(F32)) * g1 * (1.0 - g1)).astype(BF16)
        daa_ref[...] = _dot_nt(dya, _lane_concat(wsa_ref)).astype(BF16)
        dab = _dot_nt(dyb, _lane_concat(wfo_ref)).astype(BF16)
        dab_ref[...] = dab
        half_in = lax.broadcasted_iota(jnp.int32, (LANES, 2 * LANES), 0) // HEAD_DIM
        half_out = lax.broadcasted_iota(jnp.int32, (LANES, 2 * LANES), 1) // LANES
        pick = (half_in == half_out).astype(BF16)
        for pair in range(FOX_HEADS // 2):
            cols = slice(pair * LANES, (pair + 1) * LANES)
            prod = dab[:, cols].astype(F32) * ob_ref[:, cols].astype(F32)
            hi = prod.astype(BF16)
            lo_part = (prod - hi.astype(F32)).astype(BF16)
            dl_ref[:, 2 * pair * LANES:(2 * pair + 2) * LANES] = _dot(hi, pick) + _dot(lo_part, pick)

    half = D_MODEL // 2
    return _row_call(body, "ffn_bwd_b", h1.shape[0], tm, [da, dh2, h1, ya, yb, zfg, attn_b],
                     [w1s, g_mlp, w_mo, w_sa, w_fo],
                     [(D_MODEL, F32), (D_MODEL, BF16), (N_GATE, BF16), (D_MODEL, BF16), (D_MODEL, BF16),
                      (half, BF16), (half, BF16), (FOX_HEADS * LANES, F32)], [D_MODEL])


def _in_proj_bwd(dz, dh1, x, w_all, g_mix, tm, ride=None):
    def body(i, ins, consts, outs, accs):
        dz_ref, dh1_ref, x_ref = ins
        w_ref, g_ref = consts
        dx_ref, = outs
        dgx_ref, = accs
        du1 = _dot(dz_ref[...], w_ref[...])
        dh, dg = _rms_bwd(x_ref[...], g_ref[...], du1)
        _acc_rows(dgx_ref, i, dg)
        dx_ref[...] = dh1_ref[...] + dh

    return _row_call(body, "in_proj_bwd", x.shape[0], tm, [dz, dh1, x], [w_all, g_mix],
                     [(D_MODEL, F32)], [D_MODEL], ride)


def _matmul_tn(a, b, name, ts, stack_cols=0):
    n_rows, ka = a.shape
    n = b.shape[1]
    tk = min(ka, 896 if ka % 1024 else 1024)
    tn = 896 if n % 1024 else 1024
    n_stack = tn // stack_cols if stack_cols else 0
    assert ka % tk == 0 and n % tn == 0 and n_rows % ts == 0 and (not stack_cols or tk == ka)
    n_steps = n_rows // ts

    def kern(a_ref, b_ref, o_ref, acc_ref):
        s = pl.program_id(2)

        @pl.when(s == 0)
        def _():
            acc_ref[...] = jnp.zeros_like(acc_ref)
        acc_ref[...] += _dot_tn(a_ref[...].astype(BF16), b_ref[...])

        @pl.when(s == n_steps - 1)
        def _():
            if stack_cols:
                for c in range(n_stack):
                    o_ref[c] = acc_ref[:, c * stack_cols:(c + 1) * stack_cols].astype(BF16)
            else:
                o_ref[...] = acc_ref[...].astype(BF16)

    if stack_cols:
        out_spec = pl.BlockSpec((n_stack, tk, stack_cols), lambda i, j, s: (j, 0, 0))
        out_shape = jax.ShapeDtypeStruct((n // stack_cols, ka, stack_cols), BF16)
    else:
        out_spec = pl.BlockSpec((tk, tn), lambda i, j, s: (i, j))
        out_shape = jax.ShapeDtypeStruct((ka, n), BF16)
    return pl.pallas_call(
        kern, grid=(ka // tk, n // tn, n_steps),
        in_specs=[pl.BlockSpec((ts, tk), lambda i, j, s: (s, i)), pl.BlockSpec((ts, tn), lambda i, j, s: (s, j))],
        out_specs=out_spec, out_shape=out_shape, scratch_shapes=[pltpu.VMEM((tk, tn), F32)], name=name,
        compiler_params=_params(3))(a, b)


SCAN_CHUNK = 512
BWD_SCAN_CHUNK = 1024


def _decay_cumsum(f_t, b_col):
    n_tok = f_t.shape[1]
    ch = min(SCAN_CHUNK, n_tok)

    def kern(f_ref, b_ref, c_ref):
        r = lax.broadcasted_iota(jnp.int32, (ch, ch), 0)
        c = lax.broadcasted_iota(jnp.int32, (ch, ch), 1)
        tri = (r <= c).astype(F32)
        carry = jnp.zeros((8, 1), F32)
        for k in range(n_tok // ch):
            xv = f_ref[:, k * ch:(k + 1) * ch] + b_ref[...]
            lf = jnp.minimum(xv, 0.0) - jnp.log(1.0 + jnp.exp(-jnp.abs(xv)))
            cs = jnp.dot(lf, tri, precision=lax.Precision.HIGHEST, preferred_element_type=F32) + carry
            c_ref[:, k * ch:(k + 1) * ch] = cs
            carry = cs[:, ch - 1:ch]

    return pl.pallas_call(kern, out_shape=jax.ShapeDtypeStruct((8, n_tok), F32), name="decay_cumsum",
                          compiler_params=_params(0))(f_t, b_col)


def _decay_bwd(cs, rs, f_t, b_col):
    n_tok = f_t.shape[1]
    ch = min(BWD_SCAN_CHUNK, n_tok)
    n_ch = n_tok // ch

    def kern(cs_ref, rs_ref, f_ref, b_ref, df_ref, db_ref, carry_ref):
        k = pl.program_id(0)

        @pl.when(k == 0)
        def _():
            carry_ref[...] = jnp.zeros_like(carry_ref)
            db_ref[...] = jnp.zeros_like(db_ref)

        r = lax.broadcasted_iota(jnp.int32, (ch, ch), 0)
        c = lax.broadcasted_iota(jnp.int32, (ch, ch), 1)
        tri = (r >= c).astype(F32)
        head = lax.broadcasted_iota(jnp.int32, (8, 4 * LANES), 0)
        lane = lax.broadcasted_iota(jnp.int32, (8, 4 * LANES), 1)
        pick = (lane == HEAD_DIM * head).astype(F32)
        dc = lax.dot_general(pick, rs_ref[...] - cs_ref[...], _NT, precision=lax.Precision.HIGHEST,
                             preferred_element_type=F32)
        rc = jnp.dot(dc, tri, precision=lax.Precision.HIGHEST, preferred_element_type=F32) + carry_ref[:, 0:1]
        carry_ref[...] = jnp.broadcast_to(rc[:, 0:1], carry_ref.shape)
        df = rc / (1.0 + jnp.exp(f_ref[...] + b_ref[...]))
        df_ref[...] = df
        db_ref[...] += jnp.broadcast_to(jnp.sum(df, axis=1, keepdims=True), db_ref.shape)

    back = lambda k: n_ch - 1 - k
    wide = pl.BlockSpec((ch, 4 * LANES), lambda k: (back(k), 0))
    row = pl.BlockSpec((8, ch), lambda k: (0, back(k)))
    return pl.pallas_call(
        kern, grid=(n_ch,),
        in_specs=[wide, wide, row, pl.BlockSpec((8, 1), lambda k: (0, 0))],
        out_specs=[row, pl.BlockSpec((8, LANES), lambda k: (0, 0))],
        out_shape=[jax.ShapeDtypeStruct((8, n_tok), F32), jax.ShapeDtypeStruct((8, LANES), F32)],
        scratch_shapes=[pltpu.VMEM((8, LANES), F32)], name="decay_bwd", compiler_params=_params(1))(cs, rs, f_t, b_col)


def _swa_bias_table():
    row = jnp.arange(SWA_BLOCK)[:, None] + SWA_BLOCK
    col = jnp.arange(2 * SWA_BLOCK)[None, :]
    cd = (row >> CHUNK_SHIFT) - (col >> CHUNK_SHIFT)
    band = (cd >= 0) & (cd <= WINDOW_CHUNKS)
    slopes = jnp.asarray([2.0 ** -(h + 1) for h in range(SWA_HEADS)], F32)
    bias = -slopes[:, None, None] * jnp.abs(row - col).astype(F32)[None]
    return jnp.stack([jnp.where(band & (col >= SWA_BLOCK), bias, NEG), jnp.where(band, bias, NEG)])


SWA_PER_STEP = 4


def _swap_halves(t):
    return pltpu.roll(t.astype(F32), HEAD_DIM, axis=1).astype(t.dtype)


def _swa_specs():
    blk, rows = SWA_BLOCK, SWA_PER_STEP * SWA_BLOCK
    q = pl.BlockSpec((rows, 4 * LANES), lambda n: (n, 0))
    before = lambda col: pl.BlockSpec((blk, LANES), lambda n: (jnp.maximum(SWA_PER_STEP * n - 1, 0), col))
    own = lambda col: pl.BlockSpec((rows, LANES), lambda n: (n, col))
    bias = pl.BlockSpec((2, SWA_HEADS, blk, 2 * blk), lambda n: (0, 0, 0, 0))
    return [q, before(4), own(4), before(5), own(5), bias]


def _swa_band(before_ref, own_ref):
    both = jnp.concatenate([before_ref[...], own_ref[...]], axis=0)
    return both, _swap_halves(both)


def _swa_bias(bias_ref, n, b):
    return bias_ref.at[jnp.minimum(n, 1)] if b == 0 else bias_ref.at[1]


SWA_GROUPS = ([h for h in range(SWA_HEADS) if h % 2 == h // 4], [h for h in range(SWA_HEADS) if h % 2 != h // 4])


def _stack_heads(ref, rows, heads, lo, mask_halves):
    tiles = []
    for h in heads:
        t = ref[rows, (h // 2) * LANES:(h // 2 + 1) * LANES]
        tiles.append(jnp.where(lo if h % 2 == 0 else ~lo, t, jnp.zeros_like(t)) if mask_halves else t)
    return jnp.concatenate(tiles, axis=0)


def _per_head_column(values, heads):
    return jnp.concatenate([jnp.full((SWA_BLOCK, 1), values(h), F32) for h in heads], axis=0)


def _swa_scores(q_ref, rows, kx, heads, lo, bias):
    qa = _stack_heads(q_ref, rows, heads, lo, True) * SCALE
    return qa, _dot_nt(qa, kx) + jnp.concatenate([bias[h] for h in heads], axis=0)


def _swa_fwd(zm, sinks):
    n_tok = zm.shape[0]
    blk, step_rows = SWA_BLOCK, SWA_PER_STEP * SWA_BLOCK

    def kern(q_ref, kp_ref, kc_ref, vp_ref, vc_ref, bias_ref, sink_ref, o_ref, lse_ref):
        n = pl.program_id(0)
        (k_all, k_all_sw), (v_all, v_all_sw) = _swa_band(kp_ref, kc_ref), _swa_band(vp_ref, vc_ref)
        lane = lax.broadcasted_iota(jnp.int32, (blk, LANES), 1)
        lo = lane < HEAD_DIM
        for b in range(SWA_PER_STEP):
            rows, band = slice(b * blk, (b + 1) * blk), slice(b * blk, (b + 2) * blk)
            bias = _swa_bias(bias_ref, n, b)
            lse_t = jnp.zeros((blk, LANES), F32)
            for pair in range(SWA_HEADS // 2):
                q2 = q_ref[rows, pair * LANES:(pair + 1) * LANES]
                outs = []
                for a in range(2):
                    h = 2 * pair + a
                    qa = jnp.where(lo if a == 0 else ~lo, q2, jnp.zeros_like(q2)) * SCALE
                    kx, vx = (k_all[band], v_all[band]) if h in SWA_GROUPS[0] else (k_all_sw[band], v_all_sw[band])
                    s = _dot_nt(qa, kx) + bias[h]
                    sink = sink_ref[h]
                    m = jnp.maximum(jnp.max(s, axis=-1, keepdims=True), sink)
                    e = jnp.exp(s - m)
                    l = jnp.sum(e, axis=-1, keepdims=True) + jnp.exp(sink - m)
                    pn = (e * (1.0 / l)).astype(BF16)
                    outs.append(_dot(pn, vx))
                    lse_t = jnp.where(lane == h, m + jnp.log(l), lse_t)
                o_ref[rows, pair * LANES:(pair + 1) * LANES] = jnp.where(lo, outs[0], outs[1]).astype(BF16)
            lse_ref[rows, :] = lse_t

    return pl.pallas_call(
        kern, grid=(n_tok // step_rows,),
        in_specs=_swa_specs() + [pl.BlockSpec(memory_space=pltpu.SMEM)],
        out_specs=[pl.BlockSpec((step_rows, 4 * LANES), lambda n: (n, 0)),
                   pl.BlockSpec((step_rows, LANES), lambda n: (n, 0))],
        out_shape=[jax.ShapeDtypeStruct((n_tok, 4 * LANES), BF16), jax.ShapeDtypeStruct((n_tok, LANES), F32)],
        name="swa_fwd", compiler_params=_params(1))(zm, zm, zm, zm, zm, _swa_bias_table(), sinks)


def _swa_bwd(zm, sinks, d_out, out, lse):
    n_tok = zm.shape[0]
    blk, step_rows = SWA_BLOCK, SWA_PER_STEP * SWA_BLOCK

    def kern(q_ref, kp_ref, kc_ref, vp_ref, vc_ref, bias_ref, do_ref, o_ref, lse_ref, sink_ref,
             dq_ref, dkp_ref, dkc_ref, dvp_ref, dvc_ref, dsk_ref):
        n = pl.program_id(0)

        @pl.when(n == 0)
        def _():
            dsk_ref[...] = jnp.zeros_like(dsk_ref)

        bands = (_swa_band(kp_ref, kc_ref), _swa_band(vp_ref, vc_ref))
        lane = lax.broadcasted_iota(jnp.int32, (blk, LANES), 1)
        lo = lane < HEAD_DIM
        for b in range(SWA_PER_STEP):
            rows, band = slice(b * blk, (b + 1) * blk), slice(b * blk, (b + 2) * blk)
            bias = _swa_bias(bias_ref, n, b)
            lse_t = lse_ref[rows, :]
            dqs, dkv = {}, []
            for g, heads in enumerate(SWA_GROUPS):
                kx, vx = bands[0][g][band], bands[1][g][band]
                qa, s = _swa_scores(q_ref, rows, kx, heads, lo, bias)
                doa = _stack_heads(do_ref, rows, heads, lo, True)
                lse_g = jnp.concatenate([lse_t[:, h:h + 1] for h in heads], axis=0)
                prob = jnp.exp(s - lse_g)
                o_g = _stack_heads(o_ref, rows, heads, lo, False)
                dd = jnp.sum(doa.astype(F32) * o_g.astype(F32), axis=-1, keepdims=True)
                ds = (prob * (_dot_nt(doa, vx) - dd)).astype(BF16)
                sink_part = -jnp.exp(_per_head_column(lambda h: sink_ref[h], heads) - lse_g) * dd
                dq = _dot(ds, kx) * SCALE
                for r, h in enumerate(heads):
                    dqs[h] = dq[r * blk:(r + 1) * blk]
                    dsk_ref[h:h + 1, :] += jnp.broadcast_to(
                        jnp.sum(sink_part[r * blk:(r + 1) * blk], axis=0, keepdims=True), (1, LANES))
                dkv.append((_dot_tn(ds, qa), _dot_tn(prob.astype(BF16), doa)))
            for pair in range(SWA_HEADS // 2):
                dq_ref[rows, pair * LANES:(pair + 1) * LANES] = jnp.where(
                    lo, dqs[2 * pair], dqs[2 * pair + 1]).astype(BF16)
            dk = dkv[0][0] + pltpu.roll(dkv[1][0], HEAD_DIM, axis=1)
            dv = dkv[0][1] + pltpu.roll(dkv[1][1], HEAD_DIM, axis=1)
            dkp_ref[rows, :] = dk[0:blk]
            dkc_ref[rows, :] = dk[blk:2 * blk]
            dvp_ref[rows, :] = dv[0:blk]
            dvc_ref[rows, :] = dv[blk:2 * blk]

    wide = pl.BlockSpec((step_rows, 4 * LANES), lambda n: (n, 0))
    narrow = pl.BlockSpec((step_rows, LANES), lambda n: (n, 0))
    part = jax.ShapeDtypeStruct((n_tok, LANES), F32)
    return pl.pallas_call(
        kern, grid=(n_tok // step_rows,),
        in_specs=_swa_specs() + [wide, wide, narrow, pl.BlockSpec(memory_space=pltpu.SMEM)],
        out_specs=[wide, narrow, narrow, narrow, narrow, pl.BlockSpec((8, LANES), lambda n: (0, 0))],
        out_shape=[jax.ShapeDtypeStruct((n_tok, 4 * LANES), BF16), part, part, part, part,
                   jax.ShapeDtypeStruct((8, LANES), F32)],
        name="swa_bwd", compiler_params=_params(1))(zm, zm, zm, zm, zm, _swa_bias_table(), d_out, out, lse, sinks)


def _my_pos():
    return lax.axis_index("x"), lax.axis_index("y"), lax.axis_index("c")


def _peer(k):
    x, y, c = _my_pos()
    px, py, pc = x ^ (k >> 2), y ^ ((k >> 1) & 1), c ^ (k & 1)
    return (px, py, pc), 4 * px + 2 * py + pc


def _gather_copies(x_refs, out_refs, send_sems, recv_sems, local_sems):
    x, y, c = _my_pos()
    my_id = 4 * x + 2 * y + c
    local = [pltpu.make_async_copy(x_refs[w], out_refs[w].at[my_id], local_sems.at[w]) for w in range(len(x_refs))]
    sends, arrivals = [], []
    for k in range(1, N_DEV):
        peer, peer_id = _peer(k)
        for w in range(len(x_refs)):
            sems = dict(send_sem=send_sems.at[7 * w + k - 1], recv_sem=recv_sems.at[7 * w + k - 1],
                        device_id=peer, device_id_type=MESH)
            sends.append(pltpu.make_async_remote_copy(src_ref=x_refs[w], dst_ref=out_refs[w].at[my_id], **sems))
            arrivals.append(pltpu.make_async_remote_copy(src_ref=x_refs[w], dst_ref=out_refs[w].at[peer_id], **sems))
    return local, sends, arrivals


def _scatter_copies(g_refs, part_refs, send_sems, recv_sems, local_sems):
    x, y, c = _my_pos()
    my_id = 4 * x + 2 * y + c
    local = [pltpu.make_async_copy(g_refs[w].at[my_id], part_refs[w].at[0], local_sems.at[w])
             for w in range(len(g_refs))]
    sends, arrivals = [], []
    for k in range(1, N_DEV):
        peer, peer_id = _peer(k)
        for w in range(len(g_refs)):
            sems = dict(send_sem=send_sems.at[7 * w + k - 1], recv_sem=recv_sems.at[7 * w + k - 1],
                        device_id=peer, device_id_type=MESH)
            sends.append(pltpu.make_async_remote_copy(src_ref=g_refs[w].at[peer_id], dst_ref=part_refs[w].at[k], **sems))
            arrivals.append(pltpu.make_async_remote_copy(src_ref=g_refs[w].at[my_id], dst_ref=part_refs[w].at[k], **sems))
    return local, sends, arrivals


def _start_copies(local, sends, arrivals):
    for cp in local + sends:
        cp.start()


def _finish_copies(local, sends, arrivals):
    for cp in arrivals:
        cp.wait_recv()
    for cp in sends:
        cp.wait_send()
    for cp in local:
        cp.wait()


def _exchange_scratch(n_arrays):
    return [pltpu.SemaphoreType.DMA((7 * n_arrays,)), pltpu.SemaphoreType.DMA((7 * n_arrays,)),
            pltpu.SemaphoreType.DMA((n_arrays,))]


class _Ride:
    def __init__(self, arrays, out_shape, copies):
        self.arrays, self.out_shape, self.copies = list(arrays), list(out_shape), copies
        any_spec = pl.BlockSpec(memory_space=pl.ANY)
        self.in_specs = [any_spec] * len(self.arrays)
        self.out_specs = [any_spec] * len(self.arrays)
        self.scratch = _exchange_scratch(len(self.arrays)) if self.arrays else []

    @staticmethod
    def _at(grid, last):
        hit = [pl.program_id(d) == (n - 1 if last else 0) for d, n in enumerate(grid)]
        return hit[0] if len(hit) == 1 else jnp.logical_and(*hit)

    def at_first_step(self, grid, in_refs, out_refs, sems):
        @pl.when(self._at(grid, False))
        def _():
            _start_copies(*self.copies(in_refs, out_refs, *sems))

    def at_last_step(self, grid, in_refs, out_refs, sems):
        @pl.when(self._at(grid, True))
        def _():
            _finish_copies(*self.copies(in_refs, out_refs, *sems))


_NO_RIDE = _Ride([], [], None)


def _gather_ride(shards):
    return _Ride(shards, [jax.ShapeDtypeStruct((N_DEV,) + s.shape, s.dtype) for s in shards], _gather_copies)


def _scatter_ride(grads):
    return _Ride(grads, [jax.ShapeDtypeStruct(g.shape, g.dtype) for g in grads], _scatter_copies)


Q_COL, K_COL, V_COL = 6, 10, 14


def _lane_tile(stat, width):
    return jnp.tile(stat, (1, width // LANES))


def _fox_steps(nq):
    steps = [(i2, j2, int(j2 == i2)) for i2 in range(nq // 2) for j2 in range(i2 + 1)]
    return [np.asarray(col, np.int32) for col in zip(*steps)]


_BELOW = [(0, 0, 2, None), (1, 0, 2, None)]
_BELOW_NEAR = [(0, 1, 2, None), (1, 1, 2, None)]
_DIAGONAL = [(0, 0, 1, 0), (1, 0, 2, 1)]


def _causal(t, first_row):
    row = lax.broadcasted_iota(jnp.int32, t.shape, 0) + first_row
    col = lax.broadcasted_iota(jnp.int32, t.shape, 1)
    return jnp.where(col <= row, t, NEG)


def _fox_dispatch(sweep, last, dead_ref, head0, idx):
    below = last == 0
    takes_all, takes_near = [], []
    for h in (head0, head0 + 1):
        far_live, near_live = dead_ref[h, idx] < 0.5, dead_ref[h, idx + 1] < 0.5
        takes_all.append(far_live)
        takes_near.append(jnp.logical_and(jnp.logical_not(far_live), near_live))
    joint = jnp.logical_and(takes_all[0], takes_all[1])
    pl.when(jnp.logical_and(below, joint))(lambda: sweep(_BELOW, (0, 1)))
    apart = jnp.logical_and(below, jnp.logical_not(joint))
    for a in range(2):
        pl.when(jnp.logical_and(apart, takes_all[a]))(lambda a=a: sweep(_BELOW, (a,)))
        pl.when(jnp.logical_and(below, takes_near[a]))(lambda a=a: sweep(_BELOW_NEAR, (a,)))
    pl.when(last == 1)(lambda: sweep(_DIAGONAL, (0, 1)))


EXP_ZERO = 104.5
NORM_SLACK = 1.005


def _fox_dead_steps(nrm, c_pairs, tq):
    nq = nrm.shape[0] // 8
    stats = nrm.reshape(nq, 8, LANES)[:, :3, :FOX_HEADS]
    qn, kn, own = jnp.sqrt(stats[:, 0]) * SCALE, jnp.sqrt(stats[:, 1]), stats[:, 2] * SCALE
    cb = c_pairs.reshape(FOX_HEADS, nq, tq)
    c_max, c_min = jnp.max(cb, axis=-1).T, jnp.min(cb, axis=-1).T
    both = lambda t, pick: pick(t.reshape(nq // 2, 2, FOX_HEADS), axis=1)
    qn2, kn2, c_max2, own2 = both(qn, jnp.max), both(kn, jnp.max), both(c_max, jnp.max), both(own, jnp.min)
    row_max_floor = own2 - (NORM_SLACK - 1.0) * qn2 * kn2 - c_max2
    gap = qn2[:, None] * kn[None] * NORM_SLACK - c_min[None] - row_max_floor[:, None]
    below = jnp.arange(nq)[None, :] < 2 * jnp.arange(nq // 2)[:, None]
    dead = jnp.logical_and(gap < -EXP_ZERO, below[..., None])
    return dead.transpose(2, 0, 1).reshape(FOX_HEADS, -1).astype(F32)


def _fox_fwd(zm, c_pairs, dead, tq, ride=None):
    n_tok = zm.shape[0]
    nq = n_tok // tq
    ii, jj, kk = _fox_steps(nq)
    n_steps = len(ii)
    n_ride = len(ride.arrays) if ride else 0

    def kern(ii_ref, jj_ref, kk_ref, q_ref, k_ref, v_ref, ck_ref, dead_ref, *more):
        ride_in, (o_ref, ln_ref), ride_out = more[:n_ride], more[n_ride:n_ride + 2], more[n_ride + 2:2 * n_ride + 2]
        qs_ref, m_ref, l_ref, acc_ref = more[2 * n_ride + 2:2 * n_ride + 6]
        step = pl.program_id(1)
        j, last = jj_ref[step], kk_ref[step]
        lo = lax.broadcasted_iota(jnp.int32, (2 * tq, LANES), 1) < HEAD_DIM
        if ride:
            ride.at_first_step((FOX_HEADS // 2, n_steps), ride_in, ride_out, more[2 * n_ride + 6:])

        @pl.when(j == 0)
        def _():
            q2 = q_ref[...]
            zq = jnp.zeros_like(q2)
            qs_ref[0] = jnp.where(lo, q2, zq) * SCALE
            qs_ref[1] = jnp.where(lo, zq, q2) * SCALE
            m_ref[...] = jnp.full(m_ref.shape, NEG, F32)
            l_ref[...] = jnp.zeros(l_ref.shape, F32)
            acc_ref[...] = jnp.zeros(acc_ref.shape, F32)

        def sweep(tiles, heads):
            v_ones = jnp.concatenate([v_ref[...], jnp.ones((2 * tq, LANES), BF16)], axis=1)
            for sub, k0, k1, diagonal in tiles:
                rows, keys = slice(sub * tq, (sub + 1) * tq), slice(k0 * tq, k1 * tq)
                for a in heads:
                    t = _dot_nt(qs_ref[a, rows], k_ref[keys, :]) - ck_ref[a:a + 1, keys]
                    if diagonal is not None:
                        t = _causal(t, diagonal * tq)
                    m_old = m_ref[a, rows]
                    m_new = jnp.maximum(m_old, jnp.max(t, axis=-1, keepdims=True))
                    alpha = jnp.exp(m_old - m_new)
                    e = jnp.exp(t - _lane_tile(m_new, (k1 - k0) * tq)).astype(BF16)
                    pv = _dot(e, v_ones[keys])
                    acc_ref[a, rows] = alpha * acc_ref[a, rows] + pv[:, :LANES]
                    l_ref[a, rows] = alpha * l_ref[a, rows] + pv[:, LANES:]
                    m_ref[a, rows] = m_new

        _fox_dispatch(sweep, last, dead_ref, 2 * pl.program_id(0), ii_ref[step] * nq + 2 * j)

        @pl.when(last == 1)
        def _():
            o_ref[...] = jnp.where(lo, acc_ref[0] / l_ref[0], acc_ref[1] / l_ref[1]).astype(BF16)
            ln_ref[:, :LANES] = m_ref[0] + jnp.log(l_ref[0])
            ln_ref[:, LANES:] = m_ref[1] + jnp.log(l_ref[1])

        if ride:
            ride.at_last_step((FOX_HEADS // 2, n_steps), ride_in, ride_out, more[2 * n_ride + 6:])

    blk = (2 * tq, LANES)
    by_i = lambda col: (lambda hp, s, ii, jj, kk: (ii[s], col + hp))
    by_j = lambda col: (lambda hp, s, ii, jj, kk: (jj[s], col + hp))
    extra = ride if ride else _NO_RIDE
    grid_spec = pltpu.PrefetchScalarGridSpec(
        num_scalar_prefetch=3, grid=(FOX_HEADS // 2, n_steps),
        in_specs=[pl.BlockSpec(blk, by_i(Q_COL)), pl.BlockSpec(blk, by_j(K_COL)), pl.BlockSpec(blk, by_j(V_COL)),
                  pl.BlockSpec((None, 2, 2 * tq), lambda hp, s, ii, jj, kk: (hp, 0, jj[s])),
                  pl.BlockSpec(memory_space=pltpu.SMEM)] + extra.in_specs,
        out_specs=[pl.BlockSpec(blk, by_i(0)), pl.BlockSpec((2 * tq, 2 * LANES), by_i(0))] + extra.out_specs,
        scratch_shapes=[pltpu.VMEM((2, 2 * tq, LANES), BF16), pltpu.VMEM((2, 2 * tq, LANES), F32),
                        pltpu.VMEM((2, 2 * tq, LANES), F32), pltpu.VMEM((2, 2 * tq, LANES), F32)] + extra.scratch)
    return pl.pallas_call(
        kern, grid_spec=grid_spec,
        out_shape=[jax.ShapeDtypeStruct((n_tok, 4 * LANES), BF16),
                   jax.ShapeDtypeStruct((n_tok, FOX_HEADS * LANES), F32)] + extra.out_shape,
        name="fox_fwd", compiler_params=_params(2))(ii, jj, kk, zm, zm, zm, c_pairs, dead, *extra.arrays)


def _fox_bwd(zm, c_pairs, dead, d_out, lnorm, delta, tq, ride=None):
    n_tok = zm.shape[0]
    nq = n_tok // tq
    ii, jj, kk = _fox_steps(nq)
    n_steps = len(ii)
    n_ride = len(ride.arrays) if ride else 0

    def kern(ii_ref, jj_ref, kk_ref, q_ref, k_ref, v_ref, ck_ref, dead_ref, do_ref, ln_ref, dl_ref, *more):
        ride_in, ride_out = more[:n_ride], more[n_ride + 5:2 * n_ride + 5]
        dq_ref, dk_out, dv_out, cs_ref, rs_ref = more[n_ride:n_ride + 5]
        qs_ref, qo_ref, dos_ref, dq_acc, dk_ref, dv_ref = more[2 * n_ride + 5:2 * n_ride + 11]
        step = pl.program_id(1)
        j, last = jj_ref[step], kk_ref[step]
        lo = lax.broadcasted_iota(jnp.int32, (2 * tq, LANES), 1) < HEAD_DIM
        if ride:
            ride.at_first_step((FOX_HEADS // 2, n_steps), ride_in, ride_out, more[2 * n_ride + 11:])

        @pl.when(step == 0)
        def _():
            dk_ref[...] = jnp.zeros_like(dk_ref)
            dv_ref[...] = jnp.zeros_like(dv_ref)
            cs_ref[...] = jnp.zeros_like(cs_ref)

        @pl.when(j == 0)
        def _():
            q2, do2 = q_ref[...], do_ref[...]
            zq = jnp.zeros_like(q2)
            ones = jnp.ones((2 * tq, LANES), BF16)
            for a in range(2):
                half = lo if a == 0 else ~lo
                qa = jnp.where(half, q2, zq) * SCALE
                qs_ref[a] = qa
                qo_ref[a] = jnp.concatenate([qa, ones], axis=1)
                dos_ref[a] = jnp.where(half, do2, zq)
            dq_acc[...] = jnp.zeros(dq_acc.shape, F32)

        def sweep(tiles, heads):
            k_ones = jnp.concatenate([k_ref[...], jnp.ones((2 * tq, LANES), BF16)], axis=1)
            sums = {}
            for sub, k0, k1, diagonal in tiles:
                rows, keys, n_keys = slice(sub * tq, (sub + 1) * tq), slice(k0 * tq, k1 * tq), (k1 - k0) * tq
                part = sums.setdefault((k0, k1), [0.0, 0.0, 0.0, 0.0])
                for a in heads:
                    t = _dot_nt(qs_ref[a, rows], k_ref[keys, :]) - ck_ref[a:a + 1, keys]
                    if diagonal is not None:
                        t = _causal(t, diagonal * tq)
                    prob = jnp.exp(t - _lane_tile(ln_ref[rows, a * LANES:(a + 1) * LANES], n_keys))
                    dp = _dot_nt(dos_ref[a, rows], v_ref[keys, :])
                    ds = (prob * (dp - _lane_tile(dl_ref[rows, a * LANES:(a + 1) * LANES], n_keys))).astype(BF16)
                    dq_acc[a, rows] += _dot(ds, k_ones[keys])
                    dk_cs = _dot_tn(ds, qo_ref[a, rows])
                    part[0] = part[0] + dk_cs[:, :LANES]
                    part[1] = part[1] + _dot_tn(prob.astype(BF16), dos_ref[a, rows])
                    part[2 + a] = part[2 + a] + dk_cs[:, LANES:]
            for (k0, k1), (dk, dv, cs0, cs1) in sums.items():
                keys = pl.ds(pl.multiple_of((2 * j + k0) * tq, tq), (k1 - k0) * tq)
                dk_ref[keys, :] += dk
                cs_ref[keys, :] += jnp.where(lo[:(k1 - k0) * tq], cs0, cs1)
                dv_ref[keys, :] += dv

        _fox_dispatch(sweep, last, dead_ref, 2 * pl.program_id(0), ii_ref[step] * nq + 2 * j)

        @pl.when(last == 1)
        def _():
            dq_ref[...] = (jnp.where(lo, dq_acc[0, :, :LANES], dq_acc[1, :, :LANES]) * SCALE).astype(BF16)
            rs_ref[...] = jnp.where(lo, dq_acc[0, :, LANES:], dq_acc[1, :, LANES:])

        @pl.when(step == n_steps - 1)
        def _():
            dk_out[...] = dk_ref[...].astype(BF16)
            dv_out[...] = dv_ref[...].astype(BF16)

        if ride:
            ride.at_last_step((FOX_HEADS // 2, n_steps), ride_in, ride_out, more[2 * n_ride + 11:])

    blk = (2 * tq, LANES)
    by_i = lambda col: (lambda hp, s, ii, jj, kk: (ii[s], col + hp))
    by_j = lambda col: (lambda hp, s, ii, jj, kk: (jj[s], col + hp))
    resident = pl.BlockSpec(blk, by_i(0))
    stat = pl.BlockSpec((2 * tq, 2 * LANES), by_i(0))
    whole = pl.BlockSpec((n_tok, LANES), lambda hp, s, ii, jj, kk: (0, hp))
    extra = ride if ride else _NO_RIDE
    grid_spec = pltpu.PrefetchScalarGridSpec(
        num_scalar_prefetch=3, grid=(FOX_HEADS // 2, n_steps),
        in_specs=[pl.BlockSpec(blk, by_i(Q_COL)), pl.BlockSpec(blk, by_j(K_COL)), pl.BlockSpec(blk, by_j(V_COL)),
                  pl.BlockSpec((None, 2, 2 * tq), lambda hp, s, ii, jj, kk: (hp, 0, jj[s])),
                  pl.BlockSpec(memory_space=pltpu.SMEM), resident, stat, stat] + extra.in_specs,
        out_specs=[resident, whole, whole, whole, resident] + extra.out_specs,
        scratch_shapes=[pltpu.VMEM((2, 2 * tq, LANES), BF16), pltpu.VMEM((2, 2 * tq, 2 * LANES), BF16),
                        pltpu.VMEM((2, 2 * tq, LANES), BF16), pltpu.VMEM((2, 2 * tq, 2 * LANES), F32),
                        pltpu.VMEM((n_tok, LANES), F32), pltpu.VMEM((n_tok, LANES), F32)] + extra.scratch)
    wide = lambda dt: jax.ShapeDtypeStruct((n_tok, 4 * LANES), dt)
    return pl.pallas_call(
        kern, grid_spec=grid_spec, name="fox_bwd",
        out_shape=[wide(BF16), wide(BF16), wide(BF16), wide(F32), wide(F32)] + extra.out_shape,
        compiler_params=_params(2, FOX_BWD_VMEM))(ii, jj, kk, zm, zm, zm, c_pairs, dead, d_out, lnorm, delta,
                                                  *extra.arrays)


def _all_gather(shards):
    n_w = len(shards)

    def kern(*refs):
        x_refs, out_refs = refs[:n_w], refs[n_w:2 * n_w]
        send_sems, recv_sems, local_sems = refs[2 * n_w:]
        x, y, c = _my_pos()
        me, sibling = (x, y, c), (x, y, 1 - c)
        chips = [(1 - x, y), (x, 1 - y), (1 - x, 1 - y)]

        def slot(w, px, py, pc):
            return out_refs[w].at[4 * px + 2 * py + pc]

        def copy(w, k, block, to, src=None):
            return pltpu.make_async_remote_copy(
                src_ref=slot(w, *block) if src is None else src, dst_ref=slot(w, *block),
                send_sem=send_sems.at[7 * w + k], recv_sem=recv_sems.at[7 * w + k], device_id=to, device_id_type=MESH)

        local, started = [], []
        for w in range(n_w):
            mine = pltpu.make_async_copy(x_refs[w], slot(w, *me), local_sems.at[w])
            mine.start()
            local.append(mine)
            first = [copy(w, 0, me, sibling, src=x_refs[w])]
            first += [copy(w, 1 + k, me, (*chip, c), src=x_refs[w]) for k, chip in enumerate(chips)]
            for cp in first:
                cp.start()
            started += first
        for k, chip in enumerate(chips):
            for w in range(n_w):
                copy(w, 1 + k, (*chip, c), me).wait_recv()
                passed = copy(w, 4 + k, (*chip, c), sibling)
                passed.start()
                started.append(passed)
        for w in range(n_w):
            copy(w, 0, sibling, me).wait_recv()
            for k, chip in enumerate(chips):
                copy(w, 4 + k, (*chip, 1 - c), me).wait_recv()
        for cp in started:
            cp.wait_send()
        for cp in local:
            cp.wait()

    any_spec = pl.BlockSpec(memory_space=pl.ANY)
    return pl.pallas_call(
        kern, out_shape=[jax.ShapeDtypeStruct((N_DEV,) + s.shape, s.dtype) for s in shards],
        in_specs=[any_spec] * n_w, out_specs=[any_spec] * n_w,
        scratch_shapes=[pltpu.SemaphoreType.DMA((7 * n_w,)), pltpu.SemaphoreType.DMA((7 * n_w,)),
                        pltpu.SemaphoreType.DMA((n_w,))],
        name="weight_all_gather")(*shards)


def _small_exchange(small):
    def kern(s_ref, sall_ref, *sems):
        copies = _gather_copies([s_ref], [sall_ref], *sems)
        _start_copies(*copies)
        _finish_copies(*copies)

    any_spec = pl.BlockSpec(memory_space=pl.ANY)
    return pl.pallas_call(
        kern, out_shape=jax.ShapeDtypeStruct((N_DEV,) + small.shape, small.dtype), in_specs=[any_spec],
        out_specs=any_spec, scratch_shapes=_exchange_scratch(1), name="small_grad_exchange")(small)


ADAMW_BLOCK_BYTES = 2 * 1024 * 1024


def _adamw(parts, w, m, v, name):
    n_parts, n_rows, n_cols = parts.shape
    limit = max(8, ADAMW_BLOCK_BYTES // (n_parts * n_cols * parts.dtype.itemsize))
    tr = max(t for t in range(8, n_rows + 1, 8) if n_rows % t == 0 and t <= limit)

    def kern(p_ref, w_ref, m_ref, v_ref, g_out, d_out, m_out, v_out):
        g = p_ref[0].astype(F32)
        for k in range(1, n_parts):
            g = g + p_ref[k].astype(F32)
        m_new = ADAM_B1 * m_ref[...] + (1.0 - ADAM_B1) * g
        v_new = ADAM_B2 * v_ref[...] + (1.0 - ADAM_B2) * jnp.square(g)
        m_hat = m_new / (1.0 - ADAM_B1 ** ADAM_STEP)
        v_hat = v_new / (1.0 - ADAM_B2 ** ADAM_STEP)
        g_out[...] = g
        d_out[...] = -ADAM_LR * (m_hat / (jnp.sqrt(v_hat) + ADAM_EPS) + ADAM_WD * w_ref[...])
        m_out[...] = m_new
        v_out[...] = v_new

    row = pl.BlockSpec((tr, n_cols), lambda i: (i, 0))
    out = jax.ShapeDtypeStruct((n_rows, n_cols), F32)
    return pl.pallas_call(
        kern, grid=(n_rows // tr,),
        in_specs=[pl.BlockSpec((n_parts, tr, n_cols), lambda i: (0, i, 0)), row, row, row],
        out_specs=[row, row, row, row], out_shape=[out, out, out, out], name=name,
        compiler_params=_params(1))(parts, w, m, v)


SHARDED = {
    "w_in": ((D_MODEL, D_IN), 1), "w_br_swa": ((512, D_MODEL), 1), "w_br_fox": ((512, D_MODEL), 1),
    "w_mix_out": ((D_MODEL, D_MODEL), 0), "w_ff1": ((D_MODEL, D_FF), 1), "w_ff2": ((D_FF, D_MODEL), 0),
    "w_ple_gate": ((D_MODEL, D_MODEL), 0), "w_ple_proj": ((PLE_DIM, D_MODEL), 1),
}
W_IN_SHARD = D_IN // N_DEV
W_IN_PAD = 560
SMALL = ("g_mix", "g_mlp", "g_ple", "g_final", "b_forget", "swa_sinks")
SMALL_COLS = 1024


def _wire_shard(name, a):
    a = a.reshape(a.shape[-2:])
    return jnp.pad(a.T, ((0, W_IN_PAD - W_IN_SHARD), (0, 0))) if name == "w_in" else a


def _from_wire(name, a):
    return (a[:W_IN_SHARD].T if name == "w_in" else a)[None]


def _w_all_from_wire(stacked):
    w_in = jnp.concatenate([stacked[d][:W_IN_SHARD] for d in range(N_DEV)], axis=0)
    fpad = jnp.zeros((N_FPAD - FOX_HEADS, D_MODEL), stacked.dtype)
    return jnp.concatenate([w_in[:N_MAIN + FOX_HEADS], fpad, w_in[N_MAIN + FOX_HEADS:]], axis=0)


def _dw_in_to_wire(dw_all):
    dw_in = jnp.concatenate([dw_all[:N_MAIN + FOX_HEADS], dw_all[N_MAIN + N_FPAD:]], axis=0)
    pad = jnp.zeros((W_IN_PAD - W_IN_SHARD, D_MODEL), dw_all.dtype)
    return jnp.stack([jnp.concatenate([dw_in[d * W_IN_SHARD:(d + 1) * W_IN_SHARD], pad], axis=0)
                      for d in range(N_DEV)])


def _pack_small(vals, scalar=None):
    rows = [jnp.pad(vals[n].reshape(-1), (0, SMALL_COLS - vals[n].size)) for n in SMALL]
    if scalar is not None:
        rows.append(jnp.pad(scalar.reshape(1), (0, SMALL_COLS - 1)))
    rows += [jnp.zeros((SMALL_COLS,), F32)] * (8 - len(rows))
    return jnp.stack(rows)


def _unpack_small(slab, like):
    return {n: slab[r, :like[n].size].reshape(like[n].shape) for r, n in enumerate(SMALL)}


def _local_step(x, p, tgt, w, small, tm, tq, ts, late_shards=None):
    n_tok = x.shape[0]
    row = lambda v: v.reshape(1, -1)
    g_mix, g_mlp, g_ple, g_fin = row(small["g_mix"]), row(small["g_mlp"]), row(small["g_ple"]), row(small["g_final"])
    sinks = small["swa_sinks"].reshape(-1)
    b_col = small["b_forget"].reshape(FOX_HEADS, 1)

    assert tm == tq
    u1, zm, zfg, zf, nrm = _in_proj(x, g_mix, w["w_all"], tm)
    f_t = zf[:, :FOX_HEADS].T
    c_pairs = _decay_cumsum(f_t, b_col).reshape(FOX_HEADS // 2, 2, n_tok)
    attn_a, lse_a = _swa_fwd(zm, sinks)
    dead = _fox_dead_steps(nrm, c_pairs, tq)
    if late_shards is None:
        attn_b, ln_b = _fox_fwd(zm, c_pairs, dead, tq)
    else:
        attn_b, ln_b, *late = _fox_fwd(zm, c_pairs, dead, tq, _gather_ride(list(late_shards.values())))
        w = {**w, **_gathered_to_local(dict(zip(late_shards, late)))}
    ya, yb, mixed, h1, u2, a, r, h2 = _mix_ffn_fwd(attn_a, attn_b, zfg, x, w["w_br_swa"], w["w_br_fox"],
                                                   w["w_mix_out"], g_mlp, w["w_ff1"], w["w_ff2"], tm // 2)

    dlg, dpp, u3, dh2, dh2b, da, loss_acc, dgf, dgp = _head_ffn_bwd(
        h2, p, tgt, a, g_ple, w["w_ple_gate"], w["w_ple_proj"], g_fin, w["w_ff2"], tm // 2)
    dh1, dh1b, dgl, dya, dyb, daa, dab, delta_b, dgm = _ffn_bwd_b(
        da, dh2, h1, ya, yb, zfg, attn_b, w["w_ff1"], g_mlp, w["w_mix_out"], w["w_br_swa"], w["w_br_fox"], tm // 2)
    dq_a, dkp, dkc, dvp, dvc, dsk = _swa_bwd(zm, sinks, daa, attn_a, lse_a)
    dw = {
        "w_br_swa": _matmul_tn(attn_a, dya, "dw_br_swa", ts, stack_cols=D_MODEL // N_DEV),
        "w_br_fox": _matmul_tn(attn_b, dyb, "dw_br_fox", ts, stack_cols=D_MODEL // N_DEV),
        "w_mix_out": _matmul_tn(mixed, dh1b, "dw_mix_out", ts),
        "w_ff1": _matmul_tn(u2, da, "dw_ff1", ts, stack_cols=D_FF // N_DEV),
        "w_ff2": _matmul_tn(r, dh2b, "dw_ff2", ts),
        "w_ple_gate": _matmul_tn(u3, dlg, "dw_ple_gate", ts),
        "w_ple_proj": _matmul_tn(p, dpp, "dw_ple_proj", ts, stack_cols=D_MODEL // N_DEV),
    }
    if late_shards is None:
        dq_b, dk_b, dv_b, cs, rs = _fox_bwd(zm, c_pairs, dead, dab, ln_b, delta_b, tq)
        late_parts = None
    else:
        wire = _local_to_wire(dw)
        dq_b, dk_b, dv_b, cs, rs, *parts = _fox_bwd(zm, c_pairs, dead, dab, ln_b, delta_b, tq,
                                                    _scatter_ride([wire[n] for n in late_shards]))
        late_parts = dict(zip(late_shards, parts))

    up = lambda t: jnp.concatenate([t[SWA_BLOCK:], jnp.zeros((SWA_BLOCK, LANES), F32)], axis=0)
    dk_a, dv_a = dkc + up(dkp), dvc + up(dvp)
    df_t, db = _decay_bwd(cs, rs, f_t, b_col)
    df = jnp.pad(df_t.T, ((0, 0), (0, N_FPAD - FOX_HEADS)))
    dz = jnp.concatenate([dq_a, dk_a.astype(BF16), dv_a.astype(BF16), dq_b, dk_b, dv_b,
                          df.astype(BF16), dgl], axis=1)
    dw["w_all"] = _matmul_tn(dz, u1, "dw_in", ts)
    if late_shards is None:
        dx, dgx = _in_proj_bwd(dz, dh1, x, w["w_all"], g_mix, tm)
    else:
        dx, dgx, late_parts["w_in"] = _in_proj_bwd(dz, dh1, x, w["w_all"], g_mix, tm,
                                                   _scatter_ride([_dw_in_to_wire(dw["w_all"])]))
    dsmall = {"g_mix": dgx[0], "g_mlp": dgm[0], "g_ple": dgp[0], "g_final": dgf[0],
              "b_forget": db[:, 0], "swa_sinks": dsk[:, 0]}
    return loss_acc[0, 0], dx, dw, dsmall, late_parts


_ROWS = lambda t: t.reshape(-1, t.shape[-1])
_BY_ROWS = lambda t: t.reshape(N_DEV, t.shape[0] // N_DEV, t.shape[1])
_SAME = lambda t: t
LOCAL_LAYOUT = {
    "w_in": ("w_all", _w_all_from_wire, _dw_in_to_wire), "w_br_swa": ("w_br_swa", _SAME, _SAME),
    "w_br_fox": ("w_br_fox", _SAME, _SAME), "w_mix_out": ("w_mix_out", _ROWS, _BY_ROWS),
    "w_ff1": ("w_ff1", _SAME, _SAME), "w_ff2": ("w_ff2", _SAME, _BY_ROWS),
    "w_ple_gate": ("w_ple_gate", _ROWS, _BY_ROWS), "w_ple_proj": ("w_ple_proj", _SAME, _SAME),
}


def _gathered_to_local(g):
    return {LOCAL_LAYOUT[n][0]: LOCAL_LAYOUT[n][1](t) for n, t in g.items()}


def _local_to_wire(dw):
    names = {local: n for n, (local, _, _) in LOCAL_LAYOUT.items()}
    return {names[local]: LOCAL_LAYOUT[names[local]][2](t) for local, t in dw.items()}


def kernel(x, p, g_mix, w_in, b_forget, swa_sinks, w_br_swa, w_br_fox, w_mix_out, g_mlp, w_ff1, w_ff2, g_ple, w_ple_gate, w_ple_proj, g_final, loss_target, m_g_mix, m_w_in, m_b_forget, m_swa_sinks, m_w_br_swa, m_w_br_fox, m_w_mix_out, m_g_mlp, m_w_ff1, m_w_ff2, m_g_ple, m_w_ple_gate, m_w_ple_proj, m_g_final, v_g_mix, v_w_in, v_b_forget, v_swa_sinks, v_w_br_swa, v_w_br_fox, v_w_mix_out, v_g_mlp, v_w_ff1, v_w_ff2, v_g_ple, v_w_ple_gate, v_w_ple_proj, v_g_final):
    given = dict(g_mix=g_mix, w_in=w_in, b_forget=b_forget, swa_sinks=swa_sinks, w_br_swa=w_br_swa, w_br_fox=w_br_fox,
                 w_mix_out=w_mix_out, g_mlp=g_mlp, w_ff1=w_ff1, w_ff2=w_ff2, g_ple=g_ple, w_ple_gate=w_ple_gate,
                 w_ple_proj=w_ple_proj, g_final=g_final)
    mom = dict(g_mix=m_g_mix, w_in=m_w_in, b_forget=m_b_forget, swa_sinks=m_swa_sinks, w_br_swa=m_w_br_swa,
               w_br_fox=m_w_br_fox, w_mix_out=m_w_mix_out, g_mlp=m_g_mlp, w_ff1=m_w_ff1, w_ff2=m_w_ff2, g_ple=m_g_ple,
               w_ple_gate=m_w_ple_gate, w_ple_proj=m_w_ple_proj, g_final=m_g_final)
    vel = dict(g_mix=v_g_mix, w_in=v_w_in, b_forget=v_b_forget, swa_sinks=v_swa_sinks, w_br_swa=v_w_br_swa,
               w_br_fox=v_w_br_fox, w_mix_out=v_w_mix_out, g_mlp=v_g_mlp, w_ff1=v_w_ff1, w_ff2=v_w_ff2, g_ple=v_g_ple,
               w_ple_gate=v_w_ple_gate, w_ple_proj=v_w_ple_proj, g_final=v_g_final)
    names = list(given)
    sharded = list(SHARDED)

    w_wire = {n: _wire_shard(n, given[n]) for n in sharded}
    late = [n for n in sharded if n != "w_in"]
    gathered = _all_gather([w_wire["w_in"].astype(BF16)])
    local_w = _gathered_to_local({"w_in": gathered[0]})
    small = {n: given[n].reshape(-1) for n in SMALL}

    n_tok = x.shape[1]
    tile = min(TOKEN_TILE, n_tok // 4)
    loss_part, dx, dw, dsmall, parts = _local_step(
        x[0], p[0, 0], loss_target[0], local_w, small, tm=tile, tq=tile, ts=min(DW_TOKENS_PER_STEP, n_tok // 4),
        late_shards={n: w_wire[n].astype(BF16) for n in late})
    small_all = _small_exchange(_pack_small(dsmall, loss_part))

    res = {}
    for n in sharded:
        part = parts[n]
        flat = part.reshape(N_DEV, -1, part.shape[-1])
        outs = _adamw(flat, w_wire[n], _wire_shard(n, mom[n]), _wire_shard(n, vel[n]), "adamw_" + n)
        res[n] = [_from_wire(n, o) for o in outs]
    outs_s = _adamw(small_all, _pack_small(small), _pack_small({n: mom[n] for n in SMALL}),
                    _pack_small({n: vel[n] for n in SMALL}), "adamw_small")
    small_res = [_unpack_small(o, given) for o in outs_s]
    loss = outs_s[0][len(SMALL), 0]

    groups = [[res[n][k] if n in res else small_res[k][n] for n in names] for k in range(4)]
    return (loss, dx[None], *groups[0], *groups[1], *groups[2], *groups[3])
```
